```python
import math
import jax, jax.numpy as jnp
from jax import lax
import numpy as np

D_MODEL = 2048
BATCH = 16
SEQ = 256
DEPTH = 2
DEC_BATCH = 8
DEC_SEQ = 2048
PAST_LEN = 512

GRID_W = 64
BLOCK = 128
WINDOW = 128
EPS = 1e-6
ROPE_BASE = 10000.0

RET_HEADS = 4
RET_DK = 256
RET_DV = 256
ATT_HEADS = 8
ATT_KV_HEADS = 2
ATT_HD = 128
ATT_GROUP = ATT_HEADS // ATT_KV_HEADS
EVEN_SPLIT_SIZES = (RET_HEADS * RET_DK, RET_HEADS * RET_DK, RET_HEADS * RET_DV, RET_HEADS * RET_DV,
                    ATT_HEADS * ATT_HD, ATT_KV_HEADS * ATT_HD, ATT_KV_HEADS * ATT_HD)
EVEN_SPLIT_IDX = tuple(int(v) for v in np.cumsum(EVEN_SPLIT_SIZES)[:-1])
EVEN_IN = int(sum(EVEN_SPLIT_SIZES))
EVEN_MIX = RET_HEADS * RET_DV + ATT_HEADS * ATT_HD

D_INNER = 2 * D_MODEL
SSD_P = 64
SSD_HEADS = D_INNER // SSD_P
SSD_N = 128
SSD_GROUPS = 8
SSD_R = SSD_HEADS // SSD_GROUPS
CONV_W = 5
CONV_CH = D_INNER + 2 * SSD_GROUPS * SSD_N
SSD_IN = D_INNER + CONV_CH + 2 * SSD_HEADS

D_FF = 5632
N_EXPERTS = 8
TOP_K = 2
D_FF_EXP = 5632
MOE_BLOCK = 128

N_EVEN = (DEPTH + 1) // 2
N_ODD = DEPTH // 2

kernel_name = "hybrid_retention_swa_ssd_diffusion_step"


def rms_norm(x, g):
    xf = x.astype(jnp.float32)
    y = xf * lax.rsqrt(jnp.mean(xf * xf, axis=-1, keepdims=True) + EPS)
    return (y * g.astype(jnp.float32)).astype(x.dtype)


def modulation(cvec, ada_w, ada_b):
    m = jax.nn.silu(cvec) @ ada_w + ada_b
    return jnp.split(m, 6, axis=-1)


def adaln(x, g, shift, scale):
    return (rms_norm(x, g) * (1 + scale) + shift).astype(x.dtype)


def grid_rope_tables(n_tokens, dim):
    n_rows = n_tokens // GRID_W
    row = jnp.repeat(jnp.arange(n_rows), GRID_W).astype(jnp.float32)
    col = jnp.tile(jnp.arange(GRID_W), n_rows).astype(jnp.float32)
    n_freq = dim // 4
    inv = ROPE_BASE ** (-jnp.arange(n_freq, dtype=jnp.float32) / n_freq)
    ang = jnp.concatenate([row[:, None] * inv, col[:, None] * inv], axis=-1)
    return jnp.cos(ang), jnp.sin(ang)


def apply_rope(x, cos, sin):
    x1, x2 = jnp.split(x, 2, axis=-1)
    c = cos[None, :, None, :]
    s = sin[None, :, None, :]
    return jnp.concatenate([x1 * c - x2 * s, x1 * s + x2 * c], axis=-1).astype(x.dtype)


def chunk_scan(q, k, v, a, s0, strict):
    b, L, G, DK = q.shape
    R, DV = v.shape[3], v.shape[4]
    n = L // BLOCK
    qc = q.reshape(b, n, BLOCK, G, DK)
    kc = k.reshape(b, n, BLOCK, G, DK)
    vc = v.reshape(b, n, BLOCK, G, R, DV)
    cum = jnp.cumsum(a.astype(jnp.float32).reshape(b, n, BLOCK, G, R), axis=2)
    cum_t = jnp.moveaxis(cum, 2, -1)
    idx = jnp.arange(BLOCK)
    mask = (idx[:, None] > idx[None, :]) if strict else (idx[:, None] >= idx[None, :])
    decay = jnp.exp(jnp.where(mask, cum_t[..., :, None] - cum_t[..., None, :], -jnp.inf))
    scores = jnp.einsum('bnigd,bnjgd->bngij', qc, kc)
    o = jnp.einsum('bngrij,bnjgrv->bnigrv', scores[:, :, :, None] * decay, vc)
    last = cum_t[..., -1]
    v_w = vc * jnp.exp(last[:, :, None] - cum)[..., None]
    u = jnp.einsum('bnjgd,bnjgrv->bngrdv', kc, v_w).astype(jnp.float32)

    def step(s, xs):
        dec, uu = xs
        return dec[..., None, None] * s + uu, s

    s_fin, s_start = lax.scan(step, s0.astype(jnp.float32),
                              (jnp.moveaxis(jnp.exp(last), 1, 0), jnp.moveaxis(u, 1, 0)))
    s_start = jnp.moveaxis(s_start, 0, 1)
    o = o + jnp.einsum('bnigd,bngrdv->bnigrv', qc, s_start) * jnp.exp(cum)[..., None]
    return o.reshape(b, L, G, R, DV), s_fin


def bidir_scan(q, k, v_f, v_b, a_f, a_b, s0_f, s0_b):
    o_f, s_f = chunk_scan(q, k, v_f, a_f, s0_f, strict=False)
    fl = lambda t: jnp.flip(t, axis=1)
    o_b, s_b = chunk_scan(fl(q), fl(k), fl(v_b), fl(a_b), s0_b, strict=True)
    return o_f + fl(o_b), s_f, s_b


def retention(q, k, v, decay_logit, s0_f, s0_b):
    b, L = q.shape[:2]
    lg = jax.nn.log_sigmoid(decay_logit.astype(jnp.float32))
    a_f = jnp.broadcast_to(lg[0][:, None], (b, L, RET_HEADS, 1))
    a_b = jnp.broadcast_to(lg[1][:, None], (b, L, RET_HEADS, 1))
    v5 = v[:, :, :, None, :]
    o, s_f, s_b = bidir_scan(q, k, v5, v5, a_f, a_b, s0_f, s0_b)
    return o[:, :, :, 0], s_f[:, :, 0], s_b[:, :, 0]


def even_project(h, w_in, q_norm, k_norm, ropes):
    b, L, _ = h.shape
    rq, rk, rv, rg, aq, ak, av = jnp.split(h @ w_in, EVEN_SPLIT_IDX, axis=-1)
    rq = rq.reshape(b, L, RET_HEADS, RET_DK)
    rk = rk.reshape(b, L, RET_HEADS, RET_DK) * (RET_DK ** -0.5)
    rv = rv.reshape(b, L, RET_HEADS, RET_DV)
    aq = rms_norm(aq.reshape(b, L, ATT_HEADS, ATT_HD), q_norm)
    ak = rms_norm(ak.reshape(b, L, ATT_KV_HEADS, ATT_HD), k_norm)
    av = av.reshape(b, L, ATT_KV_HEADS, ATT_HD)
    if ropes is not None:
        cos_r, sin_r, cos_a, sin_a = ropes
        rq, rk = apply_rope(rq, cos_r, sin_r), apply_rope(rk, cos_r, sin_r)
        aq, ak = apply_rope(aq, cos_a, sin_a), apply_rope(ak, cos_a, sin_a)
    return rq, rk, rv, rg, aq, ak, av


def even_merge(o_ret, rg, o_att, ret_norm, w_out):
    b, L = o_ret.shape[:2]
    o_ret = rms_norm(o_ret, ret_norm.reshape(RET_HEADS, RET_DV)).reshape(b, L, RET_HEADS * RET_DV)
    o_ret = o_ret.astype(rg.dtype) * jax.nn.silu(rg)
    return jnp.concatenate([o_ret, o_att.astype(rg.dtype)], axis=-1) @ w_out


def sink_softmax(logits, sink):
    s = jnp.broadcast_to(sink, logits.shape[:-1] + (1,))
    p = jax.nn.softmax(jnp.concatenate([logits, s], axis=-1), axis=-1)
    return p[..., :-1]


def dense_context_attention(q, k, v, sink):
    b, Lc = q.shape[:2]
    nb = Lc // BLOCK
    scale = ATT_HD ** -0.5
    sk = sink.astype(jnp.float32).reshape(ATT_KV_HEADS, ATT_GROUP)[None, :, :, None, None]
    qb = jnp.moveaxis(q.reshape(b, nb, BLOCK, ATT_KV_HEADS, ATT_GROUP, ATT_HD), 1, 0)

    def one_block(qblk):
        s = jnp.einsum('bqkgd,bckd->bkgqc', qblk, k).astype(jnp.float32) * scale
        p = sink_softmax(s, sk).astype(v.dtype)
        return jnp.einsum('bkgqc,bckd->bqkgd', p, v)

    out = lax.map(one_block, qb)
    return jnp.moveaxis(out, 0, 1).reshape(b, Lc, ATT_HEADS * ATT_HD)


def banded_blocks(t, nb):
    tp = jnp.pad(t, ((0, 0), (BLOCK, BLOCK), (0, 0), (0, 0)))
    tp = tp.reshape(t.shape[0], nb + 2, BLOCK, t.shape[2], t.shape[3])
    return jnp.concatenate([tp[:, :-2], tp[:, 1:-1], tp[:, 2:]], axis=2)


def window_attention(q, k, v, k_ctx, v_ctx, sink):
    b, L = q.shape[:2]
    nb = L // BLOCK
    scale = ATT_HD ** -0.5
    qb = q.reshape(b, nb, BLOCK, ATT_KV_HEADS, ATT_GROUP, ATT_HD)
    kb = banded_blocks(k, nb)
    vb = banded_blocks(v, nb)
    blk = jnp.arange(nb)[:, None]
    qpos = blk * BLOCK + jnp.arange(BLOCK)[None, :]
    kpos = (blk - 1) * BLOCK + jnp.arange(3 * BLOCK)[None, :]
    mask = (jnp.abs(qpos[:, :, None] - kpos[:, None, :]) <= WINDOW) & (kpos[:, None, :] >= 0) & (kpos[:, None, :] < L)
    s_loc = jnp.einsum('bnqkgd,bnskd->bnkgqs', qb, kb).astype(jnp.float32) * scale
    s_loc = jnp.where(mask[None, :, None, None], s_loc, -jnp.inf)
    s_ctx = jnp.einsum('bnqkgd,bckd->bnkgqc', qb, k_ctx).astype(jnp.float32) * scale
    sk = sink.astype(jnp.float32).reshape(ATT_KV_HEADS, ATT_GROUP)[None, None, :, :, None, None]
    p = sink_softmax(jnp.concatenate([s_loc, s_ctx], axis=-1), sk).astype(v.dtype)
    p_loc, p_ctx = p[..., :3 * BLOCK], p[..., 3 * BLOCK:]
    out = (jnp.einsum('bnkgqs,bnskd->bnqkgd', p_loc, vb)
           + jnp.einsum('bnkgqc,bckd->bnqkgd', p_ctx, v_ctx))
    return out.reshape(b, L, ATT_HEADS * ATT_HD)


def centred_dwconv(x, w, bias):
    y = lax.conv_general_dilated(x, w[:, None, :].astype(x.dtype), window_strides=(1,),
                                 padding=((CONV_W // 2, CONV_W // 2),),
                                 dimension_numbers=('NWC', 'WIO', 'NWC'),
                                 feature_group_count=x.shape[-1])
    return y + bias


def ssd_mixer(h, w_in, conv_w, conv_b, a_log, dt_bias, d_skip, norm_g, w_out, s0_f, s0_b):
    b, L, _ = h.shape
    z, xbc, dt_raw = jnp.split(h @ w_in, [D_INNER, D_INNER + CONV_CH], axis=-1)
    xbc = jax.nn.silu(centred_dwconv(xbc, conv_w, conv_b))
    xs, bm, cm = jnp.split(xbc, [D_INNER, D_INNER + SSD_GROUPS * SSD_N], axis=-1)
    xs = xs.reshape(b, L, SSD_GROUPS, SSD_R, SSD_P)
    bm = bm.reshape(b, L, SSD_GROUPS, SSD_N)
    cm = cm.reshape(b, L, SSD_GROUPS, SSD_N)
    dt = jax.nn.softplus(dt_raw.astype(jnp.float32).reshape(b, L, 2, SSD_GROUPS, SSD_R)
                         + dt_bias.astype(jnp.float32).reshape(2, SSD_GROUPS, SSD_R))
    A = -jnp.exp(a_log.astype(jnp.float32)).reshape(2, SSD_GROUPS, SSD_R)
    y, s_f, s_b = bidir_scan(cm, bm, xs * dt[:, :, 0, ..., None], xs * dt[:, :, 1, ..., None],
                             dt[:, :, 0] * A[0], dt[:, :, 1] * A[1], s0_f, s0_b)
    y = y + d_skip.reshape(SSD_GROUPS, SSD_R)[..., None] * xs
    y = y.reshape(b, L, D_INNER)
    y = rms_norm(y * jax.nn.silu(z), norm_g)
    return (y @ w_out).astype(h.dtype), s_f, s_b


def swiglu(h, wg, wu, wd):
    return (jax.nn.silu(h @ wg) * (h @ wu)) @ wd


def moe_swiglu(h, router, w_gate, w_up, w_down):
    b, L, D = h.shape
    t = h.reshape(-1, D)
    T = t.shape[0]
    logits = (t @ router).astype(jnp.float32)
    top_logit, top_idx = lax.top_k(logits, TOP_K)
    top_w = jax.nn.softmax(top_logit, axis=-1)
    n_slots = T * TOP_K
    flat_e = top_idx.reshape(-1)
    flat_tok = jnp.arange(n_slots) // TOP_K
    order = jnp.argsort(flat_e)
    sorted_e = flat_e[order]
    counts = jnp.bincount(flat_e, length=N_EXPERTS)
    padded = (counts + MOE_BLOCK - 1) // MOE_BLOCK * MOE_BLOCK
    start = jnp.cumsum(counts) - counts
    pstart = jnp.cumsum(padded) - padded
    dest = pstart[sorted_e] + jnp.arange(n_slots) - start[sorted_e]
    n_blocks = -(-n_slots // MOE_BLOCK) + N_EXPERTS
    cap = n_blocks * MOE_BLOCK
    row_tok = jnp.full((cap,), T, jnp.int32).at[dest].set(flat_tok[order].astype(jnp.int32))
    row_w = jnp.zeros((cap,), jnp.float32).at[dest].set(top_w.reshape(-1)[order])
    block_e = jnp.clip(jnp.searchsorted(jnp.cumsum(padded), jnp.arange(n_blocks) * MOE_BLOCK, side='right'),
                       0, N_EXPERTS - 1)
    t_pad = jnp.concatenate([t, jnp.zeros((1, D), t.dtype)], axis=0)
    rows = t_pad[row_tok].reshape(n_blocks, MOE_BLOCK, D)

    def expert_block(args):
        xb, e = args
        return swiglu(xb, w_gate[e], w_up[e], w_down[e])

    out = lax.map(expert_block, (rows, block_e)).reshape(cap, D)
    y = jnp.zeros((T + 1, D), jnp.float32).at[row_tok].add(out.astype(jnp.float32) * row_w[:, None])
    return y[:T].reshape(b, L, D).astype(h.dtype)


def setup_inputs(seed: int = 0) -> dict:
    key = jax.random.key(seed)
    ks = iter(jax.random.split(key, 48))
    nrm = lambda shape, scale: jax.random.normal(next(ks), shape, jnp.float32) * scale
    gain = lambda shape: 1.0 + nrm(shape, 0.02)
    D = D_MODEL
    s = 5.0 + jnp.arange(RET_HEADS, dtype=jnp.float32)
    ret_logit0 = jnp.log(2.0 ** s - 1.0)
    ret_decay_logit = jnp.broadcast_to(ret_logit0, (N_EVEN, 2, RET_HEADS)) + nrm((N_EVEN, 2, RET_HEADS), 0.1)
    ssd_a_log = jnp.log(jax.random.uniform(next(ks), (N_ODD, 2, SSD_HEADS), jnp.float32, 1.0, 16.0))
    dt0 = jnp.exp(jax.random.uniform(next(ks), (N_ODD, 2, SSD_HEADS), jnp.float32,
                                     math.log(1e-3), math.log(1e-1)))
    ssd_dt_bias = dt0 + jnp.log(-jnp.expm1(-dt0))
    return {
        'x_prompt': nrm((BATCH, SEQ, D), 1.0),
        'x_sample': nrm((DEC_BATCH, DEC_SEQ, D), 1.0),
        'state_ret': nrm((DEC_BATCH, N_EVEN, 2, RET_HEADS, RET_DK, RET_DV), 1.0),
        'cache_k': nrm((DEC_BATCH, N_EVEN, PAST_LEN, ATT_KV_HEADS, ATT_HD), 1.0),
        'cache_v': nrm((DEC_BATCH, N_EVEN, PAST_LEN, ATT_KV_HEADS, ATT_HD), 1.0),
        'state_ssd': nrm((DEC_BATCH, N_ODD, 2, SSD_HEADS, SSD_N, SSD_P), 0.1),
        'c': nrm((DEC_BATCH, D), 1.0),
        'c_ctx': nrm((D,), 1.0),
        'ada_w': nrm((DEPTH, D, 6 * D), 0.5 * D ** -0.5),
        'ada_b': nrm((DEPTH, 6 * D), 0.01),
        'norm_mix': gain((DEPTH, D)),
        'norm_ffn': gain((DEPTH, D)),
        'ev_w_in': nrm((N_EVEN, D, EVEN_IN), D ** -0.5),
        'ev_w_out': nrm((N_EVEN, EVEN_MIX, D), EVEN_MIX ** -0.5),
        'ret_decay_logit': ret_decay_logit,
        'ret_norm': gain((N_EVEN, RET_HEADS * RET_DV)),
        'att_q_norm': gain((N_EVEN, ATT_HD)),
        'att_k_norm': gain((N_EVEN, ATT_HD)),
        'att_sink': nrm((N_EVEN, ATT_HEADS), 0.5),
        'ffn_w_gate': nrm((N_EVEN, D, D_FF), D ** -0.5),
        'ffn_w_up': nrm((N_EVEN, D, D_FF), D ** -0.5),
        'ffn_w_down': nrm((N_EVEN, D_FF, D), D_FF ** -0.5),
        'ssd_w_in': nrm((N_ODD, D, SSD_IN), D ** -0.5),
        'ssd_conv_w': nrm((N_ODD, CONV_W, CONV_CH), CONV_W ** -0.5),
        'ssd_conv_b': nrm((N_ODD, CONV_CH), 0.01),
        'ssd_a_log': ssd_a_log,
        'ssd_dt_bias': ssd_dt_bias,
        'ssd_d': 1.0 + nrm((N_ODD, SSD_HEADS), 0.1),
        'ssd_norm': gain((N_ODD, D_INNER)),
        'ssd_w_out': nrm((N_ODD, D_INNER, D), D_INNER ** -0.5),
        'moe_router': nrm((N_ODD, D, N_EXPERTS), D ** -0.5),
        'moe_w_gate': nrm((N_ODD, N_EXPERTS, D, D_FF_EXP), D ** -0.5),
        'moe_w_up': nrm((N_ODD, N_EXPERTS, D, D_FF_EXP), D ** -0.5),
        'moe_w_down': nrm((N_ODD, N_EXPERTS, D_FF_EXP, D), D_FF_EXP ** -0.5),
    }


def reference(x_prompt, x_sample, state_ret, cache_k, cache_v, state_ssd, c, c_ctx,
              ada_w, ada_b, norm_mix, norm_ffn, ev_w_in, ev_w_out, ret_decay_logit, ret_norm,
              att_q_norm, att_k_norm, att_sink, ffn_w_gate, ffn_w_up, ffn_w_down,
              ssd_w_in, ssd_conv_w, ssd_conv_b, ssd_a_log, ssd_dt_bias, ssd_d, ssd_norm, ssd_w_out,
              moe_router, moe_w_gate, moe_w_up, moe_w_down):
    bp = x_prompt.shape[0]
    bs, Ls = x_sample.shape[:2]
    cos_r, sin_r = grid_rope_tables(Ls, RET_DK)
    cos_a, sin_a = grid_rope_tables(Ls, ATT_HD)
    ropes = (cos_r, sin_r, cos_a, sin_a)
    xp, xs = x_prompt, x_sample
    new_ret, new_k, new_v, new_ssd = [], [], [], []
    for layer in range(DEPTH):
        j = layer // 2
        sm_p, cm_p, gm_p, sf_p, cf_p, gf_p = modulation(c_ctx[None, None, :], ada_w[layer], ada_b[layer])
        sm_s, cm_s, gm_s, sf_s, cf_s, gf_s = modulation(c[:, None, :], ada_w[layer], ada_b[layer])
        hp = adaln(xp, norm_mix[layer], sm_p, cm_p)
        hs = adaln(xs, norm_mix[layer], sm_s, cm_s)
        if layer % 2 == 0:
            rq, rk, rv, rg, aq, ak, av = even_project(hp, ev_w_in[j], att_q_norm[j], att_k_norm[j], None)
            z0 = jnp.zeros((bp, RET_HEADS, 1, RET_DK, RET_DV), jnp.float32)
            o_ret, s_f, s_b = retention(rq, rk, rv, ret_decay_logit[j], z0, z0)
            o_att = dense_context_attention(aq, ak, av, att_sink[j])
            mix_p = even_merge(o_ret, rg, o_att, ret_norm[j], ev_w_out[j])
            new_ret.append(jnp.stack([s_f, s_b], axis=1))
            new_k.append(ak)
            new_v.append(av)
            rq, rk, rv, rg, aq, ak, av = even_project(hs, ev_w_in[j], att_q_norm[j], att_k_norm[j], ropes)
            o_ret, _, _ = retention(rq, rk, rv, ret_decay_logit[j],
                                    state_ret[:, j, 0][:, :, None], state_ret[:, j, 1][:, :, None])
            o_att = window_attention(aq, ak, av, cache_k[:, j], cache_v[:, j], att_sink[j])
            mix_s = even_merge(o_ret, rg, o_att, ret_norm[j], ev_w_out[j])
        else:
            z0 = jnp.zeros((bp, SSD_GROUPS, SSD_R, SSD_N, SSD_P), jnp.float32)
            mix_p, s_f, s_b = ssd_mixer(hp, ssd_w_in[j], ssd_conv_w[j], ssd_conv_b[j], ssd_a_log[j],
                                        ssd_dt_bias[j], ssd_d[j], ssd_norm[j], ssd_w_out[j], z0, z0)
            new_ssd.append(jnp.stack([s_f.reshape(bp, SSD_HEADS, SSD_N, SSD_P),
                                      s_b.reshape(bp, SSD_HEADS, SSD_N, SSD_P)], axis=1))
            s0_f = state_ssd[:, j, 0].reshape(bs, SSD_GROUPS, SSD_R, SSD_N, SSD_P)
            s0_b = state_ssd[:, j, 1].reshape(bs, SSD_GROUPS, SSD_R, SSD_N, SSD_P)
            mix_s, _, _ = ssd_mixer(hs, ssd_w_in[j], ssd_conv_w[j], ssd_conv_b[j], ssd_a_log[j],
                                    ssd_dt_bias[j], ssd_d[j], ssd_norm[j], ssd_w_out[j], s0_f, s0_b)
        xp = (xp + gm_p * mix_p).astype(x_prompt.dtype)
        xs = (xs + gm_s * mix_s).astype(x_sample.dtype)
        hp = adaln(xp, norm_ffn[layer], sf_p, cf_p)
        hs = adaln(xs, norm_ffn[layer], sf_s, cf_s)
        if layer % 2 == 0:
            fp = swiglu(hp, ffn_w_gate[j], ffn_w_up[j], ffn_w_down[j])
            fs = swiglu(hs, ffn_w_gate[j], ffn_w_up[j], ffn_w_down[j])
        else:
            fp = moe_swiglu(hp, moe_router[j], moe_w_gate[j], moe_w_up[j], moe_w_down[j])
            fs = moe_swiglu(hs, moe_router[j], moe_w_gate[j], moe_w_up[j], moe_w_down[j])
        xp = (xp + gf_p * fp).astype(x_prompt.dtype)
        xs = (xs + gf_s * fs).astype(x_sample.dtype)
    new_state_ret = jnp.stack(new_ret, axis=1)
    new_cache_k = jnp.stack(new_k, axis=1)
    new_cache_v = jnp.stack(new_v, axis=1)
    new_state_ssd = jnp.stack(new_ssd, axis=1)
    return (xp, xs, new_state_ret, new_cache_k, new_cache_v, new_state_ssd)
```

```python
import functools

import jax
import jax.numpy as jnp
from jax import lax
from jax.experimental import pallas as pl
from jax.experimental.pallas import tpu as pltpu

F32 = jnp.float32
BF16 = jnp.bfloat16

D_MODEL = 2048
BATCH = 16
SEQ = 256
DEC_BATCH = 8
DEC_SEQ = 2048
PAST_LEN = 512
GRID_W = 64
BLOCK = 128
WINDOW = 128
EPS = 1e-6
ROPE_BASE = 10000.0
RET_HEADS = 4
RET_DK = 256
RET_DV = 256
ATT_HEADS = 8
ATT_KV_HEADS = 2
ATT_HD = 128
ATT_GROUP = ATT_HEADS // ATT_KV_HEADS
EVEN_IN = 5632
D_INNER = 2 * D_MODEL
SSD_P = 64
SSD_HEADS = D_INNER // SSD_P
SSD_N = 128
SSD_GROUPS = 8
SSD_R = SSD_HEADS // SSD_GROUPS
CONV_W = 5
CONV_CH = D_INNER + 2 * SSD_GROUPS * SSD_N
SSD_ZX = D_INNER + CONV_CH
D_FF = 5632
N_EXPERTS = 8
TOP_K = 2

T_PROMPT = BATCH * SEQ
T_SAMPLE = DEC_BATCH * DEC_SEQ
T_ALL = T_PROMPT + T_SAMPLE

LANES = 128
ROW_TILE = 1024
SUB_ROWS = 256
VMEM_LIMIT = 56 * 1024 * 1024
N_ROW_TILES = T_ALL // ROW_TILE
MOE_ROWS = 1024
NEG_INF = float("-inf")


def _cparams(*sem):
    return pltpu.CompilerParams(dimension_semantics=sem, vmem_limit_bytes=VMEM_LIMIT)


def _silu(x):
    return x * jax.nn.sigmoid(x)


def _bdot(a, b):
    return jnp.dot(a.astype(BF16), b.astype(BF16), preferred_element_type=F32)


def _bdot_nt(a, b):
    return lax.dot_general(a.astype(BF16), b.astype(BF16), (((1,), (1,)), ((), ())),
                           preferred_element_type=F32)


def _bdot_tn(a, b):
    return lax.dot_general(a.astype(BF16), b.astype(BF16), (((0,), (0,)), ((), ())),
                           preferred_element_type=F32)


def _rows(c, n=BLOCK):
    return pl.ds(pl.multiple_of(c * n, n), n)


MOD_ROWS = 16
MOD_TN = 1024


def _mod_kernel(c_ref, w_ref, b_ref, o_ref):
    o_ref[...] = _bdot(_silu(c_ref[...]), w_ref[...]) + b_ref[...]


def _modulation(cvecs, ada_w, ada_b):
    depth, d, n = ada_w.shape
    return pl.pallas_call(
        _mod_kernel,
        grid=(depth, n // MOD_TN),
        in_specs=[pl.BlockSpec((MOD_ROWS, d), lambda l, j: (0, 0)),
                  pl.BlockSpec((None, d, MOD_TN), lambda l, j: (l, 0, j)),
                  pl.BlockSpec((None, 1, MOD_TN), lambda l, j: (l, 0, j))],
        out_specs=pl.BlockSpec((None, MOD_ROWS, MOD_TN), lambda l, j: (l, 0, j)),
        out_shape=jax.ShapeDtypeStruct((depth, MOD_ROWS, n), F32),
        compiler_params=_cparams("arbitrary", "arbitrary"),
        name="modulation",
    )(cvecs, ada_w, ada_b.reshape(depth, 1, n))


ADALN_CHUNK = 64


def _adaln_to(x_ref, g_ref, mod_ref, shift_row, scale_row, hn_ref):
    mult = g_ref[...] * (1.0 + mod_ref[scale_row:scale_row + 1, :])
    shift = mod_ref[shift_row:shift_row + 1, :]

    def body(i, carry):
        r = _rows(i, ADALN_CHUNK)
        x = x_ref[r, :]
        ms = jnp.mean(x * x, axis=-1, keepdims=True)
        hn_ref[r, :] = (x * lax.rsqrt(ms + EPS) * mult + shift).astype(hn_ref.dtype)
        return carry

    lax.fori_loop(0, x_ref.shape[0] // ADALN_CHUNK, body, 0)


def _adaln_mm_kernel(shift_row, scale_row, x_ref, g_ref, mod_ref, w_ref, o_ref, hn_ref):
    @pl.when(pl.program_id(1) == 0)
    def _():
        _adaln_to(x_ref, g_ref, mod_ref, shift_row, scale_row, hn_ref)

    o_ref[...] = _bdot(hn_ref[...], w_ref[...]).astype(o_ref.dtype)


def _adaln_matmul(x, gain, modt, shift_row, scale_row, w, tn, n_out=None, col_block0=0, out_dtype=F32,
                  name="adaln_mm"):
    t, d = x.shape
    n_out = w.shape[1] if n_out is None else n_out
    return pl.pallas_call(
        functools.partial(_adaln_mm_kernel, shift_row, scale_row),
        grid=(t // ROW_TILE, n_out // tn),
        in_specs=[pl.BlockSpec((ROW_TILE, d), lambda i, j: (i, 0)),
                  pl.BlockSpec((1, d), lambda i, j: (0, 0)),
                  pl.BlockSpec((None, 8, d), lambda i, j: (i, 0, 0)),
                  pl.BlockSpec((d, tn), lambda i, j: (0, j + col_block0))],
        out_specs=pl.BlockSpec((ROW_TILE, tn), lambda i, j: (i, j)),
        out_shape=jax.ShapeDtypeStruct((t, n_out), out_dtype),
        scratch_shapes=[pltpu.VMEM((ROW_TILE, d), BF16)],
        compiler_params=_cparams("arbitrary", "arbitrary"),
        name=name,
    )(x, gain.reshape(1, d), modt, w)


def _swiglu_accumulate(hn_ref, wg_ref, wu_ref, wd_ref, acc_ref, first):
    wg = wg_ref[...].astype(BF16)
    wu = wu_ref[...].astype(BF16)
    wd = wd_ref[...].astype(BF16)
    for rc in range(hn_ref.shape[0] // SUB_ROWS):
        r = slice(rc * SUB_ROWS, (rc + 1) * SUB_ROWS)
        hn = hn_ref[r, :]
        act = _silu(_bdot(hn, wg)) * _bdot(hn, wu)
        part = _bdot(act, wd)

        @pl.when(first)
        def _():
            acc_ref[r, :] = part

        @pl.when(jnp.logical_not(first))
        def _():
            acc_ref[r, :] += part


FFN_TF = 256


def _ffn_kernel(x_ref, g_ref, mod_ref, wg_ref, wu_ref, wd_ref, o_ref, hn_ref):
    f = pl.program_id(1)

    @pl.when(f == 0)
    def _():
        _adaln_to(x_ref, g_ref, mod_ref, 3, 4, hn_ref)

    _swiglu_accumulate(hn_ref, wg_ref, wu_ref, wd_ref, o_ref, f == 0)

    @pl.when(f == pl.num_programs(1) - 1)
    def _():
        o_ref[...] = x_ref[...] + mod_ref[5:6, :] * o_ref[...]


def _ffn(x, gain, modt, wg, wu, wd):
    t, d = x.shape
    ff = wg.shape[1]
    return pl.pallas_call(
        _ffn_kernel,
        grid=(t // ROW_TILE, ff // FFN_TF),
        in_specs=[pl.BlockSpec((ROW_TILE, d), lambda i, f: (i, 0), pipeline_mode=pl.Buffered(1)),
                  pl.BlockSpec((1, d), lambda i, f: (0, 0)),
                  pl.BlockSpec((None, 8, d), lambda i, f: (i, 0, 0)),
                  pl.BlockSpec((d, FFN_TF), lambda i, f: (0, f)),
                  pl.BlockSpec((d, FFN_TF), lambda i, f: (0, f)),
                  pl.BlockSpec((FFN_TF, d), lambda i, f: (f, 0))],
        out_specs=pl.BlockSpec((ROW_TILE, d), lambda i, f: (i, 0)),
        out_shape=jax.ShapeDtypeStruct((t, d), F32),
        scratch_shapes=[pltpu.VMEM((ROW_TILE, d), BF16)],
        compiler_params=_cparams("arbitrary", "arbitrary"),
        name="ffn_swiglu",
    )(x, gain.reshape(1, d), modt, wg, wu, wd)


def _proj_res_kernel(n_a, gate_row, norm, *refs):
    a_refs = refs[:n_a]
    w_refs = refs[n_a:2 * n_a]
    x_ref, mod_ref = refs[2 * n_a], refs[2 * n_a + 1]
    pos = 2 * n_a + 2
    if norm:
        ng_ref, o_ref, hn_ref = refs[pos], refs[pos + 1], refs[pos + 2]

        @pl.when(pl.program_id(1) == 0)
        def _():
            ng = ng_ref[...]

            def body(i, carry):
                r = _rows(i, ADALN_CHUNK)
                y = a_refs[0][r, :].astype(F32)
                ms = jnp.mean(y * y, axis=-1, keepdims=True)
                hn_ref[r, :] = (y * lax.rsqrt(ms + EPS) * ng).astype(BF16)
                return carry

            lax.fori_loop(0, ROW_TILE // ADALN_CHUNK, body, 0)

        acc = _bdot(hn_ref[...], w_refs[0][...])
    else:
        o_ref = refs[pos]
        acc = _bdot(a_refs[0][...], w_refs[0][...])
        for k in range(1, n_a):
            acc = acc + _bdot(a_refs[k][...], w_refs[k][...])
    o_ref[...] = x_ref[...] + mod_ref[gate_row:gate_row + 1, :] * acc


def _proj_residual(a_list, w, x, modt, gate_row, tn, norm_gain=None, name="proj_res"):
    t, d = x.shape
    n_a = len(a_list)
    norm = norm_gain is not None
    assert not norm or n_a == 1
    in_specs, args, k0 = [], [], 0
    for a in a_list:
        in_specs.append(pl.BlockSpec((ROW_TILE, a.shape[1]), lambda i, j: (i, 0)))
        args.append(a)
    for a in a_list:
        ka = a.shape[1]
        assert k0 % ka == 0
        in_specs.append(pl.BlockSpec((ka, tn), lambda i, j, kb=k0 // ka: (kb, j)))
        args.append(w)
        k0 += ka
    in_specs += [pl.BlockSpec((ROW_TILE, tn), lambda i, j: (i, j)),
                 pl.BlockSpec((None, 8, tn), lambda i, j: (i, 0, j))]
    args += [x, modt]
    scratch = []
    if norm:
        in_specs.append(pl.BlockSpec((1, a_list[0].shape[1]), lambda i, j: (0, 0)))
        args.append(norm_gain.reshape(1, -1))
        scratch.append(pltpu.VMEM((ROW_TILE, a_list[0].shape[1]), BF16))
    return pl.pallas_call(
        functools.partial(_proj_res_kernel, n_a, gate_row, norm),
        grid=(t // ROW_TILE, d // tn),
        in_specs=in_specs,
        out_specs=pl.BlockSpec((ROW_TILE, tn), lambda i, j: (i, j)),
        out_shape=jax.ShapeDtypeStruct((t, d), F32),
        scratch_shapes=scratch,
        compiler_params=_cparams("arbitrary", "arbitrary"),
        name=name,
    )(*args)


def _ret_kernel(n_chunks, rope, has_s0, emit_state, lg_ref, *refs):
    it = iter(refs)
    q_ref, k_ref, v_ref, gt_ref = next(it), next(it), next(it), next(it)
    cos_ref, sin_ref = (next(it), next(it)) if rope else (None, None)
    s0_ref = next(it) if has_s0 else None
    gain_ref = next(it)
    o_ref = next(it)
    sfin_ref = next(it) if emit_state else None
    qs_ref, ks_ref, sfs_ref, dm_ref, dec_ref, sf_ref, sb_ref = (next(it) for _ in range(7))

    h = pl.program_id(1)
    lgf = lg_ref[0, h]
    lgb = lg_ref[1, h]
    half = RET_DK // 2

    def prep(c, carry):
        r = _rows(c)
        q = q_ref[r, :]
        k = k_ref[r, :] * (RET_DK ** -0.5)
        if rope:
            cs, sn = cos_ref[r, :], sin_ref[r, :]
            for src, dst in ((q, qs_ref), (k, ks_ref)):
                x1, x2 = src[:, :half], src[:, half:]
                dst[r, :half] = (x1 * cs - x2 * sn).astype(BF16)
                dst[r, half:] = (x1 * sn + x2 * cs).astype(BF16)
        else:
            qs_ref[r, :] = q.astype(BF16)
            ks_ref[r, :] = k.astype(BF16)
        return carry

    lax.fori_loop(0, n_chunks, prep, 0)

    ii = lax.broadcasted_iota(jnp.int32, (BLOCK, BLOCK), 0)
    jj = lax.broadcasted_iota(jnp.int32, (BLOCK, BLOCK), 1)
    diff = (ii - jj).astype(F32)
    dm_ref[...] = jnp.exp(jnp.where(jj <= ii, diff * lgf, -diff * lgb))
    pos = lax.broadcasted_iota(jnp.int32, (BLOCK, RET_DV), 0).astype(F32)
    dec_ref[0] = jnp.exp((pos + 1.0) * lgf)
    dec_ref[1] = jnp.exp((BLOCK - pos) * lgb)
    dec_ref[2] = jnp.exp((BLOCK - 1.0 - pos) * lgf)
    dec_ref[3] = jnp.exp(pos * lgb)
    tot_f = jnp.exp(jnp.full((1, RET_DV), BLOCK * lgf, F32))
    tot_b = jnp.exp(jnp.full((1, RET_DV), BLOCK * lgb, F32))

    if has_s0:
        sf_ref[...] = s0_ref[0]
        sb_ref[...] = s0_ref[1]
    else:
        sf_ref[...] = jnp.zeros_like(sf_ref)
        sb_ref[...] = jnp.zeros_like(sb_ref)

    def fwd(c, carry):
        r = _rows(c)
        sfs_ref[c] = sf_ref[...].astype(BF16)
        kd = ks_ref[r, :].astype(F32) * dec_ref[2]
        sf_ref[...] = sf_ref[...] * tot_f + _bdot_tn(kd, v_ref[r, :])
        return carry

    lax.fori_loop(0, n_chunks, fwd, 0)
    if emit_state:
        sfin_ref[0] = sf_ref[...]

    def bwd(t, carry):
        c = n_chunks - 1 - t
        r = _rows(c)
        q = qs_ref[r, :]
        k = ks_ref[r, :]
        v = v_ref[r, :].astype(BF16)
        p = _bdot_nt(q, k) * dm_ref[...]
        o = _bdot(p, v)
        o = o + _bdot(q, sfs_ref[c]) * dec_ref[0]
        o = o + _bdot(q, sb_ref[...]) * dec_ref[1]
        ms = jnp.mean(o * o, axis=-1, keepdims=True)
        y = o * lax.rsqrt(ms + EPS) * gain_ref[...]
        o_ref[r, :] = (y * _silu(gt_ref[r, :])).astype(o_ref.dtype)
        kd = k.astype(F32) * dec_ref[3]
        sb_ref[...] = sb_ref[...] * tot_b + _bdot_tn(kd, v)
        return carry

    lax.fori_loop(0, n_chunks, bwd, 0)
    if emit_state:
        sfin_ref[1] = sb_ref[...]


def _retention(proj, lg, ret_norm, seq_len, n_seq, row_block0, ropes=None, s0=None, emit_state=False):
    n_chunks = seq_len // BLOCK
    rope, has_s0 = ropes is not None, s0 is not None

    def col(cb):
        return pl.BlockSpec((seq_len, RET_DK), lambda b, h, cb=cb: (b + row_block0, cb * RET_HEADS + h))

    in_specs = [pl.BlockSpec(memory_space=pltpu.SMEM), col(0), col(1), col(2), col(3)]
    args = [lg, proj, proj, proj, proj]
    if rope:
        in_specs += [pl.BlockSpec((seq_len, RET_DK // 2), lambda b, h: (0, 0))] * 2
        args += list(ropes)
    if has_s0:
        in_specs.append(pl.BlockSpec((None, None, 2, None, RET_DK, RET_DV), lambda b, h: (b, 0, 0, h, 0, 0)))
        args.append(s0)
    in_specs.append(pl.BlockSpec((1, RET_DV), lambda b, h: (0, h)))
    args.append(ret_norm.reshape(1, RET_HEADS * RET_DV))
    out_specs = [pl.BlockSpec((seq_len, RET_DV), lambda b, h: (b, h))]
    out_shape = [jax.ShapeDtypeStruct((n_seq * seq_len, RET_HEADS * RET_DV), BF16)]
    if emit_state:
        out_specs.append(pl.BlockSpec((None, None, 2, None, RET_DK, RET_DV), lambda b, h: (b, 0, 0, h, 0, 0)))
        out_shape.append(jax.ShapeDtypeStruct((n_seq, 1, 2, RET_HEADS, RET_DK, RET_DV), F32))
    return pl.pallas_call(
        functools.partial(_ret_kernel, n_chunks, rope, has_s0, emit_state),
        grid=(n_seq, RET_HEADS),
        in_specs=in_specs,
        out_specs=out_specs,
        out_shape=out_shape,
        scratch_shapes=[pltpu.VMEM((seq_len, RET_DK), BF16), pltpu.VMEM((seq_len, RET_DK), BF16),
                        pltpu.VMEM((n_chunks, RET_DK, RET_DV), BF16), pltpu.VMEM((BLOCK, BLOCK), F32),
                        pltpu.VMEM((4, BLOCK, RET_DV), F32), pltpu.VMEM((RET_DK, RET_DV), F32),
                        pltpu.VMEM((RET_DK, RET_DV), F32)],
        compiler_params=_cparams("arbitrary", "arbitrary"),
        name="retention_latent" if rope else "retention_prompt",
    )(*args)


def _head_norm(x, gain):
    return x * lax.rsqrt(jnp.mean(x * x, axis=-1, keepdims=True) + EPS) * gain


def _rope_full(x, cs, sn):
    return x * cs + pltpu.roll(x, ATT_HD // 2, 1) * sn


def _sink_column(sink_ref, kh, rows_per_head):
    n = ATT_GROUP * rows_per_head
    head = lax.broadcasted_iota(jnp.int32, (n, 1), 0) // rows_per_head
    col = jnp.full((n, 1), sink_ref[kh * ATT_GROUP], F32)
    for g in range(1, ATT_GROUP):
        col = jnp.where(head == g, sink_ref[kh * ATT_GROUP + g], col)
    return col


def _att_latent_kernel(sink_ref, q_ref, k_ref, v_ref, ck_ref, cv_ref, cos_ref, sin_ref, qn_ref, kn_ref,
                       o_ref, kp_ref, vp_ref, ckp_ref, cvp_ref):
    kh, qb = pl.program_id(1), pl.program_id(2)
    n_chunks = DEC_SEQ // BLOCK
    loc = 3 * BLOCK

    @pl.when(qb == 0)
    def _():
        def prep(c, carry):
            r = _rows(c)
            kp_ref[r, :] = _rope_full(_head_norm(k_ref[r, :], kn_ref[...]), cos_ref[r, :], sin_ref[r, :]).astype(BF16)
            vp_ref[r, :] = v_ref[r, :].astype(BF16)
            return carry

        lax.fori_loop(0, n_chunks, prep, 0)
        ckp_ref[...] = ck_ref[...].astype(BF16)
        cvp_ref[...] = cv_ref[...].astype(BF16)

    rq = _rows(qb)
    cs, sn = cos_ref[rq, :], sin_ref[rq, :]
    qg = qn_ref[...] * (ATT_HD ** -0.5)
    q = jnp.concatenate(
        [_rope_full(_head_norm(q_ref[:, g * ATT_HD:(g + 1) * ATT_HD], qg), cs, sn).astype(BF16)
         for g in range(ATT_GROUP)], axis=0)
    start = pl.multiple_of(jnp.clip((qb - 1) * BLOCK, 0, DEC_SEQ - loc), BLOCK)
    s_loc = _bdot_nt(q, kp_ref[pl.ds(start, loc), :])
    qpos = qb * BLOCK + (lax.broadcasted_iota(jnp.int32, (ATT_GROUP * BLOCK, loc), 0) & (BLOCK - 1))
    kpos = start + lax.broadcasted_iota(jnp.int32, (ATT_GROUP * BLOCK, loc), 1)
    s_loc = jnp.where(jnp.abs(qpos - kpos) <= WINDOW, s_loc, NEG_INF)
    s_ctx = _bdot_nt(q, ckp_ref[...])
    sink = _sink_column(sink_ref, kh, BLOCK)
    m = jnp.maximum(jnp.maximum(jnp.max(s_loc, axis=-1, keepdims=True), jnp.max(s_ctx, axis=-1, keepdims=True)),
                    sink)
    p_loc = jnp.exp(s_loc - m)
    p_ctx = jnp.exp(s_ctx - m)
    den = jnp.sum(p_loc, axis=-1, keepdims=True) + jnp.sum(p_ctx, axis=-1, keepdims=True) + jnp.exp(sink - m)
    o = (_bdot(p_loc, vp_ref[pl.ds(start, loc), :]) + _bdot(p_ctx, cvp_ref[...])) / den
    for g in range(ATT_GROUP):
        o_ref[:, g * ATT_HD:(g + 1) * ATT_HD] = o[g * BLOCK:(g + 1) * BLOCK, :].astype(o_ref.dtype)


def _attention_latent(proj, cache_k, cache_v, sink, cos_a, sin_a, q_norm, k_norm):
    nqb = DEC_SEQ // BLOCK
    rb0 = T_PROMPT // DEC_SEQ
    qcol0 = 4 * RET_HEADS * RET_DK // (ATT_GROUP * ATT_HD)
    kcol0 = (4 * RET_HEADS * RET_DK + ATT_HEADS * ATT_HD) // ATT_HD
    vcol0 = kcol0 + ATT_KV_HEADS
    ck = cache_k.reshape(DEC_BATCH, PAST_LEN, ATT_KV_HEADS * ATT_HD)
    cv = cache_v.reshape(DEC_BATCH, PAST_LEN, ATT_KV_HEADS * ATT_HD)
    return pl.pallas_call(
        _att_latent_kernel,
        grid=(DEC_BATCH, ATT_KV_HEADS, nqb),
        in_specs=[pl.BlockSpec(memory_space=pltpu.SMEM),
                  pl.BlockSpec((BLOCK, ATT_GROUP * ATT_HD),
                               lambda b, kh, qb: (T_PROMPT // BLOCK + b * nqb + qb, qcol0 + kh)),
                  pl.BlockSpec((DEC_SEQ, ATT_HD), lambda b, kh, qb: (rb0 + b, kcol0 + kh)),
                  pl.BlockSpec((DEC_SEQ, ATT_HD), lambda b, kh, qb: (rb0 + b, vcol0 + kh)),
                  pl.BlockSpec((None, PAST_LEN, ATT_HD), lambda b, kh, qb: (b, 0, kh)),
                  pl.BlockSpec((None, PAST_LEN, ATT_HD), lambda b, kh, qb: (b, 0, kh)),
                  pl.BlockSpec((DEC_SEQ, ATT_HD), lambda b, kh, qb: (0, 0)),
                  pl.BlockSpec((DEC_SEQ, ATT_HD), lambda b, kh, qb: (0, 0)),
                  pl.BlockSpec((1, ATT_HD), lambda b, kh, qb: (0, 0)),
                  pl.BlockSpec((1, ATT_HD), lambda b, kh, qb: (0, 0))],
        out_specs=pl.BlockSpec((BLOCK, ATT_GROUP * ATT_HD), lambda b, kh, qb: (b * nqb + qb, kh)),
        out_shape=jax.ShapeDtypeStruct((T_SAMPLE, ATT_HEADS * ATT_HD), BF16),
        scratch_shapes=[pltpu.VMEM((DEC_SEQ, ATT_HD), BF16), pltpu.VMEM((DEC_SEQ, ATT_HD), BF16),
                        pltpu.VMEM((PAST_LEN, ATT_HD), BF16), pltpu.VMEM((PAST_LEN, ATT_HD), BF16)],
        compiler_params=_cparams("arbitrary", "arbitrary", "arbitrary"),
        name="attention_latent",
    )(sink, proj, proj, proj, ck, cv, cos_a, sin_a, q_norm.reshape(1, ATT_HD), k_norm.reshape(1, ATT_HD))


def _att_prompt_kernel(sink_ref, q_ref, k_ref, v_ref, qn_ref, kn_ref, o_ref, nk_ref, nv_ref):
    kh = pl.program_id(1)
    kn = _head_norm(k_ref[...], kn_ref[...])
    v = v_ref[...]
    nk_ref[...] = kn
    nv_ref[...] = v
    qg = qn_ref[...] * (ATT_HD ** -0.5)
    q = jnp.concatenate([_head_norm(q_ref[:, g * ATT_HD:(g + 1) * ATT_HD], qg).astype(BF16)
                         for g in range(ATT_GROUP)], axis=0)
    s = _bdot_nt(q, kn)
    sink = _sink_column(sink_ref, kh, SEQ)
    m = jnp.maximum(jnp.max(s, axis=-1, keepdims=True), sink)
    p = jnp.exp(s - m)
    den = jnp.sum(p, axis=-1, keepdims=True) + jnp.exp(sink - m)
    o = _bdot(p, v) / den
    for g in range(ATT_GROUP):
        o_ref[:, g * ATT_HD:(g + 1) * ATT_HD] = o[g * SEQ:(g + 1) * SEQ, :].astype(o_ref.dtype)


def _attention_prompt(proj, sink, q_norm, k_norm):
    qcol0 = 4 * RET_HEADS * RET_DK // (ATT_GROUP * ATT_HD)
    kcol0 = (4 * RET_HEADS * RET_DK + ATT_HEADS * ATT_HD) // ATT_HD
    vcol0 = kcol0 + ATT_KV_HEADS
    kv_spec = pl.BlockSpec((None, SEQ, ATT_HD), lambda b, kh: (b, 0, kh))
    kv_shape = jax.ShapeDtypeStruct((BATCH, SEQ, ATT_KV_HEADS * ATT_HD), F32)
    return pl.pallas_call(
        _att_prompt_kernel,
        grid=(BATCH, ATT_KV_HEADS),
        in_specs=[pl.BlockSpec(memory_space=pltpu.SMEM),
                  pl.BlockSpec((SEQ, ATT_GROUP * ATT_HD), lambda b, kh: (b, qcol0 + kh)),
                  pl.BlockSpec((SEQ, ATT_HD), lambda b, kh: (b, kcol0 + kh)),
                  pl.BlockSpec((SEQ, ATT_HD), lambda b, kh: (b, vcol0 + kh)),
                  pl.BlockSpec((1, ATT_HD), lambda b, kh: (0, 0)),
                  pl.BlockSpec((1, ATT_HD), lambda b, kh: (0, 0))],
        out_specs=[pl.BlockSpec((SEQ, ATT_GROUP * ATT_HD), lambda b, kh: (b, kh)), kv_spec, kv_spec],
        out_shape=[jax.ShapeDtypeStruct((T_PROMPT, ATT_HEADS * ATT_HD), BF16), kv_shape, kv_shape],
        compiler_params=_cparams("arbitrary", "arbitrary"),
        name="attention_prompt",
    )(sink, proj, proj, proj, q_norm.reshape(1, ATT_HD), k_norm.reshape(1, ATT_HD))


def _split_dot(m01, a):
    hi = a.astype(BF16)
    r1 = a - hi.astype(F32)
    mid = r1.astype(BF16)
    lo = (r1 - mid.astype(F32)).astype(BF16)
    return (jnp.dot(m01, hi, preferred_element_type=F32) + jnp.dot(m01, mid, preferred_element_type=F32)
            + jnp.dot(m01, lo, preferred_element_type=F32))


def _ssd_prep_kernel(raw_ref, bias_ref, alog_ref, cum_ref, dt_ref, w_ref, tot_ref):
    x = raw_ref[...] + bias_ref[...]
    dt = jnp.maximum(x, 0.0) + jnp.log1p(jnp.exp(-jnp.abs(x)))
    a = dt * (-jnp.exp(alog_ref[...]))
    ii = lax.broadcasted_iota(jnp.int32, (BLOCK, BLOCK), 0)
    jj = lax.broadcasted_iota(jnp.int32, (BLOCK, BLOCK), 1)
    incl = _split_dot(jnp.where(jj <= ii, 1.0, 0.0).astype(BF16), a)
    rincl = _split_dot(jnp.where(jj >= ii, 1.0, 0.0).astype(BF16), a)
    fwd_lane = lax.broadcasted_iota(jnp.int32, (BLOCK, LANES), 1) < SSD_HEADS
    cum = jnp.where(fwd_lane, incl, rincl)
    tot = jnp.where(fwd_lane[:1], incl[BLOCK - 1:BLOCK, :], rincl[0:1, :])
    w = dt * jnp.exp(tot - cum)
    cum_ref[...] = cum.T
    dt_ref[...] = dt.T
    w_ref[...] = w.T
    tot_ref[...] = jnp.broadcast_to(jnp.exp(tot), (BLOCK, LANES)).T


def _ssd_prep(dt_raw, dt_bias, a_log):
    nc = T_ALL // BLOCK
    spec = pl.BlockSpec((None, 2 * SSD_HEADS, BLOCK), lambda c: (c, 0, 0))
    shape = jax.ShapeDtypeStruct((nc, 2 * SSD_HEADS, BLOCK), F32)
    return pl.pallas_call(
        _ssd_prep_kernel,
        grid=(nc,),
        in_specs=[pl.BlockSpec((BLOCK, 2 * SSD_HEADS), lambda c: (c, 0)),
                  pl.BlockSpec((1, 2 * SSD_HEADS), lambda c: (0, 0)),
                  pl.BlockSpec((1, 2 * SSD_HEADS), lambda c: (0, 0))],
        out_specs=[spec] * 4,
        out_shape=[shape] * 4,
        compiler_params=_cparams("arbitrary"),
        name="ssd_prep",
    )(dt_raw, dt_bias.reshape(1, 2 * SSD_HEADS), a_log.reshape(1, 2 * SSD_HEADS))


GW = SSD_R * SSD_P
HALO = 8


def _pair_tiles(per_head):
    low = lax.broadcasted_iota(jnp.int32, per_head[0].shape, 1) < SSD_P
    return jnp.concatenate([jnp.where(low, per_head[2 * t], per_head[2 * t + 1]) for t in range(SSD_R // 2)],
                           axis=1)


def _row_bcast(ref, c, r):
    return jnp.broadcast_to(ref[c, r:r + 1, :], (BLOCK, BLOCK))


def _ssd_kernel(n_chunks, has_s0, emit_state, *refs):
    it = iter(refs)
    z_ref, x_ref, b_ref, c_ref = (next(it) for _ in range(4))
    cumf_ref, cumb_ref, dtf_ref, dtb_ref, wf_ref, wb_ref, totf_ref, totb_ref = (next(it) for _ in range(8))
    cwx_ref, cwb_ref, cwc_ref, cbx_ref, cbb_ref, cbc_ref, d_ref = (next(it) for _ in range(7))
    s0_ref = next(it) if has_s0 else None
    o_ref = next(it)
    sfin_ref = next(it) if emit_state else None
    pad_ref, xc_ref, bc_ref, cc_ref, sfs_ref, sf_ref, sb_ref = (next(it) for _ in range(7))
    seq_len = n_chunks * BLOCK

    pad_ref[0:HALO, :] = jnp.zeros((HALO, GW + 2 * SSD_N), F32)
    pad_ref[HALO + seq_len:2 * HALO + seq_len, :] = jnp.zeros((HALO, GW + 2 * SSD_N), F32)

    def fill(c, carry):
        dst = pl.ds(pl.multiple_of(c * BLOCK, BLOCK) + HALO, BLOCK)
        r = _rows(c)
        pad_ref[dst, 0:GW] = x_ref[r, :]
        pad_ref[dst, GW:GW + SSD_N] = b_ref[r, :]
        pad_ref[dst, GW + SSD_N:GW + 2 * SSD_N] = c_ref[r, :]
        return carry

    lax.fori_loop(0, n_chunks, fill, 0)

    def conv(c, carry):
        r = _rows(c)
        win = pad_ref[pl.ds(pl.multiple_of(c * BLOCK, BLOCK), BLOCK + 2 * HALO), :]
        for lo, hi, cw_ref, cb_ref, dst in ((0, GW, cwx_ref, cbx_ref, xc_ref),
                                            (GW, GW + SSD_N, cwb_ref, cbb_ref, bc_ref),
                                            (GW + SSD_N, GW + 2 * SSD_N, cwc_ref, cbc_ref, cc_ref)):
            acc = jnp.broadcast_to(cb_ref[...], (BLOCK, hi - lo))
            for w in range(CONV_W):
                off = HALO - CONV_W // 2 + w
                acc = acc + win[off:off + BLOCK, lo:hi] * cw_ref[w:w + 1, :]
            dst[r, :] = _silu(acc).astype(dst.dtype)
        return carry

    lax.fori_loop(0, n_chunks, conv, 0)

    if has_s0:
        for r in range(SSD_R):
            sf_ref[:, r * SSD_P:(r + 1) * SSD_P] = s0_ref[0, r]
            sb_ref[:, r * SSD_P:(r + 1) * SSD_P] = s0_ref[1, r]
    else:
        sf_ref[...] = jnp.zeros_like(sf_ref)
        sb_ref[...] = jnp.zeros_like(sb_ref)

    def state_update(s_ref, c, w_ref, tot_ref, bm, xs):
        wcol = _pair_tiles([_row_bcast(w_ref, c, r).T for r in range(SSD_R)])
        tot = _pair_tiles([tot_ref[c, r:r + 1, :] for r in range(SSD_R)])
        s_ref[...] = s_ref[...] * tot + _bdot_tn(bm, xs * wcol)

    def fwd(c, carry):
        r = _rows(c)
        sfs_ref[c] = sf_ref[...].astype(BF16)
        state_update(sf_ref, c, wf_ref, totf_ref, bc_ref[r, :], xc_ref[r, :])
        return carry

    lax.fori_loop(0, n_chunks, fwd, 0)
    if emit_state:
        for r in range(SSD_R):
            sfin_ref[0, r] = sf_ref[:, r * SSD_P:(r + 1) * SSD_P]

    ii = lax.broadcasted_iota(jnp.int32, (BLOCK, BLOCK), 0)
    jj = lax.broadcasted_iota(jnp.int32, (BLOCK, BLOCK), 1)
    causal = jj <= ii
    low = lax.broadcasted_iota(jnp.int32, (BLOCK, LANES), 1) < SSD_P

    def bwd(t, carry):
        c = n_chunks - 1 - t
        r = _rows(c)
        cm, bm, xs = cc_ref[r, :], bc_ref[r, :], xc_ref[r, :]
        xb = xs.astype(BF16)
        sc = _bdot_nt(cm, bm)
        yf = _bdot(cm, sfs_ref[c])
        yb = _bdot(cm, sb_ref[...])
        ef, eb, tiles = [], [], []
        for t2 in range(SSD_R // 2):
            xt = xb[:, t2 * LANES:(t2 + 1) * LANES]
            acc = None
            for hh in range(2):
                hr = 2 * t2 + hh
                rf, rb = _row_bcast(cumf_ref, c, hr), _row_bcast(cumb_ref, c, hr)
                cf, cb = rf.T, rb.T
                e = jnp.exp(jnp.where(causal, cf - rf, cb - rb))
                e = e * jnp.where(causal, _row_bcast(dtf_ref, c, hr), _row_bcast(dtb_ref, c, hr))
                xh = jnp.where(low, xt, 0.0) if hh == 0 else jnp.where(low, 0.0, xt)
                part = _bdot(sc * e, xh)
                acc = part if acc is None else acc + part
                ef.append(jnp.exp(cf))
                eb.append(jnp.exp(cb))
            tiles.append(acc)
        y = jnp.concatenate(tiles, axis=1) + _pair_tiles(ef) * yf + _pair_tiles(eb) * yb + d_ref[...] * xs
        o_ref[r, :] = (y * _silu(z_ref[r, :])).astype(o_ref.dtype)
        state_update(sb_ref, c, wb_ref, totb_ref, bm, xs)
        return carry

    lax.fori_loop(0, n_chunks, bwd, 0)
    if emit_state:
        for r in range(SSD_R):
            sfin_ref[1, r] = sb_ref[:, r * SSD_P:(r + 1) * SSD_P]


def _ssd_scan(zx, prep, conv_w, conv_b, d_exp, seq_len, n_seq, row_block0, s0=None, emit_state=False):
    n_chunks = seq_len // BLOCK
    has_s0 = s0 is not None
    xcol0 = D_INNER // GW
    bcol0 = 2 * D_INNER // SSD_N
    ccol0 = bcol0 + SSD_GROUPS
    cwb0 = D_INNER // SSD_N

    def rowcol(width, col0):
        return pl.BlockSpec((seq_len, width), lambda b, g, col0=col0: (b + row_block0, col0 + g))

    def headrows(direction):
        return pl.BlockSpec((n_chunks, SSD_R, BLOCK),
                            lambda b, g, direction=direction: (b + row_block0, direction * SSD_GROUPS + g, 0))

    in_specs = [rowcol(GW, 0), rowcol(GW, xcol0), rowcol(SSD_N, bcol0), rowcol(SSD_N, ccol0)]
    args = [zx, zx, zx, zx]
    for arr in prep:
        in_specs += [headrows(0), headrows(1)]
        args += [arr, arr]
    in_specs += [pl.BlockSpec((CONV_W, GW), lambda b, g: (0, g)),
                 pl.BlockSpec((CONV_W, SSD_N), lambda b, g: (0, cwb0 + g)),
                 pl.BlockSpec((CONV_W, SSD_N), lambda b, g: (0, cwb0 + SSD_GROUPS + g)),
                 pl.BlockSpec((1, GW), lambda b, g: (0, g)),
                 pl.BlockSpec((1, SSD_N), lambda b, g: (0, cwb0 + g)),
                 pl.BlockSpec((1, SSD_N), lambda b, g: (0, cwb0 + SSD_GROUPS + g)),
                 pl.BlockSpec((1, GW), lambda b, g: (0, g))]
    args += [conv_w, conv_w, conv_w, conv_b, conv_b, conv_b, d_exp]
    state_spec = pl.BlockSpec((None, None, 2, SSD_R, SSD_N, SSD_P), lambda b, g: (b, 0, 0, g, 0, 0))
    if has_s0:
        in_specs.append(state_spec)
        args.append(s0)
    out_specs = [pl.BlockSpec((seq_len, GW), lambda b, g: (b, g))]
    out_shape = [jax.ShapeDtypeStruct((n_seq * seq_len, D_INNER), BF16)]
    if emit_state:
        out_specs.append(state_spec)
        out_shape.append(jax.ShapeDtypeStruct((n_seq, 1, 2, SSD_HEADS, SSD_N, SSD_P), F32))
    return pl.pallas_call(
        functools.partial(_ssd_kernel, n_chunks, has_s0, emit_state),
        grid=(n_seq, SSD_GROUPS),
        in_specs=in_specs,
        out_specs=out_specs,
        out_shape=out_shape,
        scratch_shapes=[pltpu.VMEM((seq_len + 2 * HALO, GW + 2 * SSD_N), F32),
                        pltpu.VMEM((seq_len, GW), F32), pltpu.VMEM((seq_len, SSD_N), BF16),
                        pltpu.VMEM((seq_len, SSD_N), BF16), pltpu.VMEM((n_chunks, SSD_N, GW), BF16),
                        pltpu.VMEM((SSD_N, GW), F32), pltpu.VMEM((SSD_N, GW), F32)],
        compiler_params=_cparams("arbitrary", "arbitrary"),
        name="ssd_scan_latent" if has_s0 else "ssd_scan_prompt",
    )(*args)


def _router_kernel(x_ref, g_ref, mod_ref, rw_ref, hn_ref, idx_ref, wgt_ref, hf_ref):
    _adaln_to(x_ref, g_ref, mod_ref, 3, 4, hf_ref)
    hf = hf_ref[...]
    hn_ref[...] = hf.astype(BF16)
    logits = jnp.dot(hf, rw_ref[...], preferred_element_type=F32, precision=lax.Precision.HIGHEST)
    lane = lax.broadcasted_iota(jnp.int32, logits.shape, 1)
    lg = jnp.where(lane < N_EXPERTS, logits, NEG_INF)
    m1 = jnp.max(lg, axis=-1, keepdims=True)
    i1 = jnp.min(jnp.where(lg == m1, lane, LANES), axis=-1, keepdims=True)
    lg2 = jnp.where(lane == i1, NEG_INF, lg)
    m2 = jnp.max(lg2, axis=-1, keepdims=True)
    i2 = jnp.min(jnp.where(lg2 == m2, lane, LANES), axis=-1, keepdims=True)
    e2 = jnp.exp(m2 - m1)
    w1 = 1.0 / (1.0 + e2)
    idx_ref[...] = jnp.where(lane == 0, i1, jnp.where(lane == 1, i2, 0))
    wgt_ref[...] = jnp.where(lane == 0, w1, jnp.where(lane == 1, e2 * w1, 0.0))


def _router(x, gain, modt, router_w):
    t, d = x.shape
    rw = jnp.pad(router_w, ((0, 0), (0, LANES - N_EXPERTS)))
    return pl.pallas_call(
        _router_kernel,
        grid=(t // ROW_TILE,),
        in_specs=[pl.BlockSpec((ROW_TILE, d), lambda i: (i, 0)),
                  pl.BlockSpec((1, d), lambda i: (0, 0)),
                  pl.BlockSpec((None, 8, d), lambda i: (i, 0, 0)),
                  pl.BlockSpec((d, LANES), lambda i: (0, 0))],
        out_specs=[pl.BlockSpec((ROW_TILE, d), lambda i: (i, 0)),
                   pl.BlockSpec((ROW_TILE, LANES), lambda i: (i, 0)),
                   pl.BlockSpec((ROW_TILE, LANES), lambda i: (i, 0))],
        out_shape=[jax.ShapeDtypeStruct((t, d), BF16), jax.ShapeDtypeStruct((t, LANES), jnp.int32),
                   jax.ShapeDtypeStruct((t, LANES), F32)],
        scratch_shapes=[pltpu.VMEM((ROW_TILE, d), F32)],
        compiler_params=_cparams("arbitrary"),
        name="moe_router",
    )(x, gain.reshape(1, d), modt, rw)


def _expert_kernel(be_ref, nu_ref, xs_ref, wg_ref, wu_ref, wd_ref, o_ref):
    i, f = pl.program_id(0), pl.program_id(1)
    used = i < nu_ref[0]

    @pl.when(used)
    def _():
        _swiglu_accumulate(xs_ref, wg_ref, wu_ref, wd_ref, o_ref, f == 0)

    @pl.when(jnp.logical_and(jnp.logical_not(used), f == 0))
    def _():
        o_ref[...] = jnp.zeros_like(o_ref)


def _experts(xs_sorted, block_e, n_used, wg, wu, wd):
    cap, d = xs_sorted.shape
    ff = wg.shape[2]
    nf = ff // FFN_TF

    def widx(i, f, be, nu):
        last = jnp.maximum(nu[0] - 1, 0)
        return be[jnp.minimum(i, last)], jnp.where(i < nu[0], f, nf - 1)

    def w_in_map(i, f, be, nu):
        e, ft = widx(i, f, be, nu)
        return e, 0, ft

    def w_out_map(i, f, be, nu):
        e, ft = widx(i, f, be, nu)
        return e, ft, 0

    return pl.pallas_call(
        _expert_kernel,
        grid_spec=pltpu.PrefetchScalarGridSpec(
            num_scalar_prefetch=2,
            grid=(cap // MOE_ROWS, nf),
            in_specs=[pl.BlockSpec((MOE_ROWS, d), lambda i, f, be, nu: (i, 0)),
                      pl.BlockSpec((None, d, FFN_TF), w_in_map),
                      pl.BlockSpec((None, d, FFN_TF), w_in_map),
                      pl.BlockSpec((None, FFN_TF, d), w_out_map)],
            out_specs=pl.BlockSpec((MOE_ROWS, d), lambda i, f, be, nu: (i, 0)),
        ),
        out_shape=jax.ShapeDtypeStruct((cap, d), F32),
        compiler_params=_cparams("arbitrary", "arbitrary"),
        name="moe_experts",
    )(block_e, n_used, xs_sorted, wg, wu, wd)


def _moe(x, gain, modt, router_w, wg, wu, wd):
    t, d = x.shape
    hn, idx, wgt = _router(x, gain, modt, router_w)
    top_idx, top_w = idx[:, :TOP_K], wgt[:, :TOP_K]
    n_slots = t * TOP_K
    flat_e = top_idx.reshape(-1)
    onehot = (flat_e[:, None] == jnp.arange(N_EXPERTS, dtype=jnp.int32)[None, :]).astype(jnp.int32)
    incl = jnp.cumsum(onehot, axis=0)
    counts = incl[-1]
    rank = jnp.sum((incl - onehot) * onehot, axis=1)
    padded = (counts + MOE_ROWS - 1) // MOE_ROWS * MOE_ROWS
    pend = jnp.cumsum(padded)
    pstart = pend - padded
    dest = pstart[flat_e] + rank
    n_blocks = n_slots // MOE_ROWS + N_EXPERTS
    cap = n_blocks * MOE_ROWS
    row_tok = jnp.full((cap,), t, jnp.int32).at[dest].set(jnp.arange(n_slots, dtype=jnp.int32) // TOP_K)
    block_e = jnp.clip(jnp.searchsorted(pend, jnp.arange(n_blocks, dtype=jnp.int32) * MOE_ROWS, side='right'),
                       0, N_EXPERTS - 1).astype(jnp.int32)
    n_used = (pend[-1:] // MOE_ROWS).astype(jnp.int32)
    hn_pad = jnp.concatenate([hn, jnp.zeros((1, d), hn.dtype)], axis=0)
    out = _experts(hn_pad[row_tok], block_e, n_used, wg, wu, wd)
    picked = out[dest].reshape(t, TOP_K, d)
    y = jnp.sum(picked * top_w[:, :, None], axis=1)
    gate = jnp.repeat(modt[:, 5, :], ROW_TILE, axis=0)
    return x + gate * y


def _rope_tables(n_tokens, dim):
    n_rows = n_tokens // GRID_W
    row = jnp.repeat(jnp.arange(n_rows), GRID_W).astype(F32)
    col = jnp.tile(jnp.arange(GRID_W), n_rows).astype(F32)
    n_freq = dim // 4
    inv = ROPE_BASE ** (-jnp.arange(n_freq, dtype=F32) / n_freq)
    ang = jnp.concatenate([row[:, None] * inv, col[:, None] * inv], axis=-1)
    return jnp.cos(ang), jnp.sin(ang)


def kernel(x_prompt, x_sample, state_ret, cache_k, cache_v, state_ssd, c, c_ctx, ada_w, ada_b, norm_mix, norm_ffn, ev_w_in, ev_w_out, ret_decay_logit, ret_norm, att_q_norm, att_k_norm, att_sink, ffn_w_gate, ffn_w_up, ffn_w_down, ssd_w_in, ssd_conv_w, ssd_conv_b, ssd_a_log, ssd_dt_bias, ssd_d, ssd_norm, ssd_w_out, moe_router, moe_w_gate, moe_w_up, moe_w_down):
    d = D_MODEL
    x = jnp.concatenate([x_prompt.reshape(T_PROMPT, d), x_sample.reshape(T_SAMPLE, d)], axis=0)

    cvecs = jnp.concatenate([c_ctx[None, :], c, jnp.zeros((MOD_ROWS - 1 - DEC_BATCH, d), F32)], axis=0)
    mods = _modulation(cvecs, ada_w, ada_b).reshape(2, MOD_ROWS, 6, d)
    tiles_per_seq = DEC_SEQ // ROW_TILE
    tile_row = jnp.concatenate([jnp.zeros((T_PROMPT // ROW_TILE,), jnp.int32),
                                1 + jnp.arange(T_SAMPLE // ROW_TILE, dtype=jnp.int32) // tiles_per_seq])
    modt = jnp.pad(mods[:, tile_row], ((0, 0), (0, 0), (0, 2), (0, 0)))

    proj = _adaln_matmul(x, norm_mix[0], modt[0], 0, 1, ev_w_in[0], tn=512, name="even_in_proj")
    lg = jax.nn.log_sigmoid(ret_decay_logit[0].astype(F32))
    cos_r, sin_r = _rope_tables(DEC_SEQ, RET_DK)
    cos_a, sin_a = _rope_tables(DEC_SEQ, ATT_HD)
    cos_a2 = jnp.concatenate([cos_a, cos_a], axis=-1)
    sin_a2 = jnp.concatenate([-sin_a, sin_a], axis=-1)
    ret_p, new_state_ret = _retention(proj, lg, ret_norm[0], SEQ, BATCH, 0, emit_state=True)
    ret_s, = _retention(proj, lg, ret_norm[0], DEC_SEQ, DEC_BATCH, T_PROMPT // DEC_SEQ,
                        ropes=(cos_r, sin_r), s0=state_ret)
    att_p, new_k, new_v = _attention_prompt(proj, att_sink[0], att_q_norm[0], att_k_norm[0])
    att_s = _attention_latent(proj, cache_k[:, 0], cache_v[:, 0], att_sink[0], cos_a2, sin_a2,
                              att_q_norm[0], att_k_norm[0])
    mix_ret = jnp.concatenate([ret_p, ret_s], axis=0)
    mix_att = jnp.concatenate([att_p, att_s], axis=0)
    x = _proj_residual([mix_ret, mix_att], ev_w_out[0], x, modt[0], 2, tn=512, name="even_out_proj")
    x = _ffn(x, norm_ffn[0], modt[0], ffn_w_gate[0], ffn_w_up[0], ffn_w_down[0])

    zx = _adaln_matmul(x, norm_mix[1], modt[1], 0, 1, ssd_w_in[0], tn=512, n_out=SSD_ZX, name="ssd_in_proj")
    dt_raw = _adaln_matmul(x, norm_mix[1], modt[1], 0, 1, ssd_w_in[0], tn=2 * SSD_HEADS, n_out=2 * SSD_HEADS,
                           col_block0=SSD_ZX // (2 * SSD_HEADS), name="ssd_dt_proj")
    prep = _ssd_prep(dt_raw, ssd_dt_bias[0], ssd_a_log[0])
    d_exp = jnp.repeat(ssd_d[0], SSD_P)[None, :]
    conv_b = ssd_conv_b[0][None, :]
    yg_p, new_state_ssd = _ssd_scan(zx, prep, ssd_conv_w[0], conv_b, d_exp, SEQ, BATCH, 0, emit_state=True)
    yg_s, = _ssd_scan(zx, prep, ssd_conv_w[0], conv_b, d_exp, DEC_SEQ, DEC_BATCH, T_PROMPT // DEC_SEQ,
                      s0=state_ssd)
    yg = jnp.concatenate([yg_p, yg_s], axis=0)
    x = _proj_residual([yg], ssd_w_out[0], x, modt[1], 2, tn=256, norm_gain=ssd_norm[0], name="ssd_out_proj")
    x = _moe(x, norm_ffn[1], modt[1], moe_router[0], moe_w_gate[0], moe_w_up[0], moe_w_down[0])

    y_prompt = x[:T_PROMPT].reshape(BATCH, SEQ, d)
    y_sample = x[T_PROMPT:].reshape(DEC_BATCH, DEC_SEQ, d)
    new_cache_k = new_k.reshape(BATCH, 1, SEQ, ATT_KV_HEADS, ATT_HD)
    new_cache_v = new_v.reshape(BATCH, 1, SEQ, ATT_KV_HEADS, ATT_HD)
    return (y_prompt, y_sample, new_state_ret, new_cache_k, new_cache_v, new_state_ssd)
```

```python
import functools

import jax
import jax.numpy as jnp
from jax import lax
from jax.experimental import pallas as pl
from jax.experimental.pallas import tpu as pltpu

F32 = jnp.float32
BF16 = jnp.bfloat16

D_MODEL = 2048
BATCH = 16
SEQ = 256
DEC_BATCH = 8
DEC_SEQ = 2048
PAST_LEN = 512
GRID_W = 64
BLOCK = 128
WINDOW = 128
EPS = 1e-6
ROPE_BASE = 10000.0
RET_HEADS = 4
RET_DK = 256
RET_DV = 256
ATT_HEADS = 8
ATT_KV_HEADS = 2
ATT_HD = 128
ATT_GROUP = ATT_HEADS // ATT_KV_HEADS
EVEN_IN = 5632
D_INNER = 2 * D_MODEL
SSD_P = 64
SSD_HEADS = D_INNER // SSD_P
SSD_N = 128
SSD_GROUPS = 8
SSD_R = SSD_HEADS // SSD_GROUPS
CONV_W = 5
CONV_CH = D_INNER + 2 * SSD_GROUPS * SSD_N
SSD_ZX = D_INNER + CONV_CH
D_FF = 5632
N_EXPERTS = 8
TOP_K = 2

T_PROMPT = BATCH * SEQ
T_SAMPLE = DEC_BATCH * DEC_SEQ
T_ALL = T_PROMPT + T_SAMPLE

LANES = 128
ROW_TILE = 1024
VMEM_LIMIT = 56 * 1024 * 1024
N_ROW_TILES = T_ALL // ROW_TILE
MOE_ROWS = 1024
NEG_INF = float("-inf")


def _cparams(*sem):
    return pltpu.CompilerParams(dimension_semantics=sem, vmem_limit_bytes=VMEM_LIMIT)


def _silu(x):
    return x * jax.nn.sigmoid(x)


def _bdot(a, b):
    return jnp.dot(a.astype(BF16), b.astype(BF16), preferred_element_type=F32)


def _bdot_nt(a, b):
    return lax.dot_general(a.astype(BF16), b.astype(BF16), (((1,), (1,)), ((), ())),
                           preferred_element_type=F32)


def _bdot_tn(a, b):
    return lax.dot_general(a.astype(BF16), b.astype(BF16), (((0,), (0,)), ((), ())),
                           preferred_element_type=F32)


def _rows(c, n=BLOCK):
    return pl.ds(pl.multiple_of(c * n, n), n)


MOD_ROWS = 16
MOD_TN = 1024


def _mod_kernel(c_ref, w_ref, b_ref, o_ref):
    o_ref[...] = _bdot(_silu(c_ref[...]), w_ref[...]) + b_ref[...]


def _modulation(cvecs, ada_w, ada_b):
    depth, d, n = ada_w.shape
    return pl.pallas_call(
        _mod_kernel,
        grid=(depth, n // MOD_TN),
        in_specs=[pl.BlockSpec((MOD_ROWS, d), lambda l, j: (0, 0)),
                  pl.BlockSpec((None, d, MOD_TN), lambda l, j: (l, 0, j)),
                  pl.BlockSpec((None, 1, MOD_TN), lambda l, j: (l, 0, j))],
        out_specs=pl.BlockSpec((None, MOD_ROWS, MOD_TN), lambda l, j: (l, 0, j)),
        out_shape=jax.ShapeDtypeStruct((depth, MOD_ROWS, n), F32),
        compiler_params=_cparams("arbitrary", "arbitrary"),
        name="modulation",
    )(cvecs, ada_w, ada_b.reshape(depth, 1, n))


ADALN_CHUNK = 64


def _adaln_to(x_ref, g_ref, mod_ref, shift_row, scale_row, hn_ref):
    mult = g_ref[...] * (1.0 + mod_ref[scale_row:scale_row + 1, :])
    shift = mod_ref[shift_row:shift_row + 1, :]

    def body(i, carry):
        r = _rows(i, ADALN_CHUNK)
        x = x_ref[r, :]
        ms = jnp.mean(x * x, axis=-1, keepdims=True)
        hn_ref[r, :] = (x * lax.rsqrt(ms + EPS) * mult + shift).astype(hn_ref.dtype)
        return carry

    lax.fori_loop(0, x_ref.shape[0] // ADALN_CHUNK, body, 0)


def _adaln_mm_kernel(shift_row, scale_row, x_ref, g_ref, mod_ref, w_ref, o_ref, hn_ref):
    @pl.when(pl.program_id(1) == 0)
    def _():
        _adaln_to(x_ref, g_ref, mod_ref, shift_row, scale_row, hn_ref)

    o_ref[...] = _bdot(hn_ref[...], w_ref[...]).astype(o_ref.dtype)


def _adaln_matmul(x, gain, modt, shift_row, scale_row, w, tn, n_out=None, col_block0=0, out_dtype=F32,
                  name="adaln_mm"):
    t, d = x.shape
    n_out = w.shape[1] if n_out is None else n_out
    return pl.pallas_call(
        functools.partial(_adaln_mm_kernel, shift_row, scale_row),
        grid=(t // ROW_TILE, n_out // tn),
        in_specs=[pl.BlockSpec((ROW_TILE, d), lambda i, j: (i, 0)),
                  pl.BlockSpec((1, d), lambda i, j: (0, 0)),
                  pl.BlockSpec((None, 8, d), lambda i, j: (i, 0, 0)),
                  pl.BlockSpec((d, tn), lambda i, j: (0, j + col_block0))],
        out_specs=pl.BlockSpec((ROW_TILE, tn), lambda i, j: (i, j)),
        out_shape=jax.ShapeDtypeStruct((t, n_out), out_dtype),
        scratch_shapes=[pltpu.VMEM((ROW_TILE, d), BF16)],
        compiler_params=_cparams("arbitrary", "arbitrary"),
        name=name,
    )(x, gain.reshape(1, d), modt, w)


FFN_TF = 512


def _gateup_kernel(x_ref, g_ref, mod_ref, wg_ref, wu_ref, h_ref, hn_ref):
    @pl.when(pl.program_id(1) == 0)
    def _():
        _adaln_to(x_ref, g_ref, mod_ref, 3, 4, hn_ref)

    hn = hn_ref[...]
    h_ref[...] = (_silu(_bdot(hn, wg_ref[...])) * _bdot(hn, wu_ref[...])).astype(h_ref.dtype)


def _ffn_gateup(x, gain, modt, wg, wu):
    t, d = x.shape
    ff = wg.shape[1]
    return pl.pallas_call(
        _gateup_kernel,
        grid=(t // ROW_TILE, ff // FFN_TF),
        in_specs=[pl.BlockSpec((ROW_TILE, d), lambda i, f: (i, 0)),
                  pl.BlockSpec((1, d), lambda i, f: (0, 0)),
                  pl.BlockSpec((None, 8, d), lambda i, f: (i, 0, 0)),
                  pl.BlockSpec((d, FFN_TF), lambda i, f: (0, f)),
                  pl.BlockSpec((d, FFN_TF), lambda i, f: (0, f))],
        out_specs=pl.BlockSpec((ROW_TILE, FFN_TF), lambda i, f: (i, f)),
        out_shape=jax.ShapeDtypeStruct((t, ff), BF16),
        scratch_shapes=[pltpu.VMEM((ROW_TILE, d), BF16)],
        compiler_params=_cparams("arbitrary", "arbitrary"),
        name="ffn_gateup",
    )(x, gain.reshape(1, d), modt, wg, wu)


def _proj_res_kernel(n_a, gate_row, norm, *refs):
    a_refs = refs[:n_a]
    w_refs = refs[n_a:2 * n_a]
    x_ref, mod_ref = refs[2 * n_a], refs[2 * n_a + 1]
    pos = 2 * n_a + 2
    if norm:
        ng_ref, o_ref, hn_ref = refs[pos], refs[pos + 1], refs[pos + 2]

        @pl.when(pl.program_id(1) == 0)
        def _():
            ng = ng_ref[...]

            def body(i, carry):
                r = _rows(i, ADALN_CHUNK)
                y = a_refs[0][r, :].astype(F32)
                ms = jnp.mean(y * y, axis=-1, keepdims=True)
                hn_ref[r, :] = (y * lax.rsqrt(ms + EPS) * ng).astype(BF16)
                return carry

            lax.fori_loop(0, ROW_TILE // ADALN_CHUNK, body, 0)

        acc = _bdot(hn_ref[...], w_refs[0][...])
    else:
        o_ref = refs[pos]
        acc = _bdot(a_refs[0][...], w_refs[0][...])
        for k in range(1, n_a):
            acc = acc + _bdot(a_refs[k][...], w_refs[k][...])
    o_ref[...] = x_ref[...] + mod_ref[gate_row:gate_row + 1, :] * acc


def _proj_residual(a_list, w, x, modt, gate_row, tn, norm_gain=None, name="proj_res"):
    t, d = x.shape
    n_a = len(a_list)
    norm = norm_gain is not None
    assert not norm or n_a == 1
    in_specs, args, k0 = [], [], 0
    for a in a_list:
        in_specs.append(pl.BlockSpec((ROW_TILE, a.shape[1]), lambda i, j: (i, 0)))
        args.append(a)
    for a in a_list:
        ka = a.shape[1]
        assert k0 % ka == 0
        in_specs.append(pl.BlockSpec((ka, tn), lambda i, j, kb=k0 // ka: (kb, j)))
        args.append(w)
        k0 += ka
    in_specs += [pl.BlockSpec((ROW_TILE, tn), lambda i, j: (i, j)),
                 pl.BlockSpec((None, 8, tn), lambda i, j: (i, 0, j))]
    args += [x, modt]
    scratch = []
    if norm:
        in_specs.append(pl.BlockSpec((1, a_list[0].shape[1]), lambda i, j: (0, 0)))
        args.append(norm_gain.reshape(1, -1))
        scratch.append(pltpu.VMEM((ROW_TILE, a_list[0].shape[1]), BF16))
    return pl.pallas_call(
        functools.partial(_proj_res_kernel, n_a, gate_row, norm),
        grid=(t // ROW_TILE, d // tn),
        in_specs=in_specs,
        out_specs=pl.BlockSpec((ROW_TILE, tn), lambda i, j: (i, j)),
        out_shape=jax.ShapeDtypeStruct((t, d), F32),
        scratch_shapes=scratch,
        compiler_params=_cparams("arbitrary", "arbitrary"),
        name=name,
    )(*args)


def _ret_kernel(n_chunks, rope, has_s0, emit_state, has_fill, lg_ref, *refs):
    it = iter(refs)
    q_ref, k_ref, v_ref, gt_ref = next(it), next(it), next(it), next(it)
    cos_ref, sin_ref = (next(it), next(it)) if rope else (None, None)
    s0_ref = next(it) if has_s0 else None
    gain_ref = next(it)
    if has_fill:
        next(it)
    o_ref = next(it)
    sfin_ref = next(it) if emit_state else None
    qs_ref, ks_ref, sfs_ref, dm_ref, dec_ref, sf_ref, sb_ref = (next(it) for _ in range(7))

    h = pl.program_id(1)
    lgf = lg_ref[0, h]
    lgb = lg_ref[1, h]
    half = RET_DK // 2

    def prep(c, carry):
        r = _rows(c)
        q = q_ref[r, :]
        k = k_ref[r, :] * (RET_DK ** -0.5)
        if rope:
            cs, sn = cos_ref[r, :], sin_ref[r, :]
            for src, dst in ((q, qs_ref), (k, ks_ref)):
                x1, x2 = src[:, :half], src[:, half:]
                dst[r, :half] = (x1 * cs - x2 * sn).astype(BF16)
                dst[r, half:] = (x1 * sn + x2 * cs).astype(BF16)
        else:
            qs_ref[r, :] = q.astype(BF16)
            ks_ref[r, :] = k.astype(BF16)
        return carry

    lax.fori_loop(0, n_chunks, prep, 0)

    ii = lax.broadcasted_iota(jnp.int32, (BLOCK, BLOCK), 0)
    jj = lax.broadcasted_iota(jnp.int32, (BLOCK, BLOCK), 1)
    diff = (ii - jj).astype(F32)
    dm_ref[...] = jnp.exp(jnp.where(jj <= ii, diff * lgf, -diff * lgb))
    pos = lax.broadcasted_iota(jnp.int32, (BLOCK, RET_DV), 0).astype(F32)
    dec_ref[0] = jnp.exp((pos + 1.0) * lgf)
    dec_ref[1] = jnp.exp((BLOCK - pos) * lgb)
    dec_ref[2] = jnp.exp((BLOCK - 1.0 - pos) * lgf)
    dec_ref[3] = jnp.exp(pos * lgb)
    tot_f = jnp.exp(jnp.full((1, RET_DV), BLOCK * lgf, F32))
    tot_b = jnp.exp(jnp.full((1, RET_DV), BLOCK * lgb, F32))

    if has_s0:
        sf_ref[...] = s0_ref[0]
        sb_ref[...] = s0_ref[1]
    else:
        sf_ref[...] = jnp.zeros_like(sf_ref)
        sb_ref[...] = jnp.zeros_like(sb_ref)

    def fwd(c, carry):
        r = _rows(c)
        sfs_ref[c] = sf_ref[...].astype(BF16)
        kd = ks_ref[r, :].astype(F32) * dec_ref[2]
        sf_ref[...] = sf_ref[...] * tot_f + _bdot_tn(kd, v_ref[r, :])
        return carry

    lax.fori_loop(0, n_chunks, fwd, 0)
    if emit_state:
        sfin_ref[0] = sf_ref[...]

    def bwd(t, carry):
        c = n_chunks - 1 - t
        r = _rows(c)
        q = qs_ref[r, :]
        k = ks_ref[r, :]
        v = v_ref[r, :].astype(BF16)
        p = _bdot_nt(q, k) * dm_ref[...]
        o = _bdot(p, v)
        o = o + _bdot(q, sfs_ref[c]) * dec_ref[0]
        o = o + _bdot(q, sb_ref[...]) * dec_ref[1]
        ms = jnp.mean(o * o, axis=-1, keepdims=True)
        y = o * lax.rsqrt(ms + EPS) * gain_ref[...]
        o_ref[r, :] = (y * _silu(gt_ref[r, :])).astype(o_ref.dtype)
        kd = k.astype(F32) * dec_ref[3]
        sb_ref[...] = sb_ref[...] * tot_b + _bdot_tn(kd, v)
        return carry

    lax.fori_loop(0, n_chunks, bwd, 0)
    if emit_state:
        sfin_ref[1] = sb_ref[...]


def _fill_alias(fill, in_specs, args):
    if fill is None:
        return {}
    in_specs.append(pl.BlockSpec(memory_space=pl.ANY))
    args.append(fill)
    return {len(args) - 1: 0}


def _retention(proj, lg, ret_norm, seq_len, n_seq, row_block0, ropes=None, s0=None, emit_state=False, fill=None):
    n_chunks = seq_len // BLOCK
    rope, has_s0 = ropes is not None, s0 is not None

    def col(cb):
        return pl.BlockSpec((seq_len, RET_DK), lambda b, h, cb=cb: (b + row_block0, cb * RET_HEADS + h))

    in_specs = [pl.BlockSpec(memory_space=pltpu.SMEM), col(0), col(1), col(2), col(3)]
    args = [lg, proj, proj, proj, proj]
    if rope:
        in_specs += [pl.BlockSpec((seq_len, RET_DK // 2), lambda b, h: (0, 0))] * 2
        args += list(ropes)
    if has_s0:
        in_specs.append(pl.BlockSpec((None, None, 2, None, RET_DK, RET_DV), lambda b, h: (b, 0, 0, h, 0, 0)))
        args.append(s0)
    in_specs.append(pl.BlockSpec((1, RET_DV), lambda b, h: (0, h)))
    args.append(ret_norm.reshape(1, RET_HEADS * RET_DV))
    aliases = _fill_alias(fill, in_specs, args)
    out_specs = [pl.BlockSpec((seq_len, RET_DV), lambda b, h: (b + row_block0, h))]
    out_shape = [jax.ShapeDtypeStruct((T_ALL, RET_HEADS * RET_DV), BF16)]
    if emit_state:
        out_specs.append(pl.BlockSpec((None, None, 2, None, RET_DK, RET_DV), lambda b, h: (b, 0, 0, h, 0, 0)))
        out_shape.append(jax.ShapeDtypeStruct((n_seq, 1, 2, RET_HEADS, RET_DK, RET_DV), F32))
    return pl.pallas_call(
        functools.partial(_ret_kernel, n_chunks, rope, has_s0, emit_state, fill is not None),
        grid=(n_seq, RET_HEADS),
        in_specs=in_specs,
        out_specs=out_specs,
        out_shape=out_shape,
        input_output_aliases=aliases,
        scratch_shapes=[pltpu.VMEM((seq_len, RET_DK), BF16), pltpu.VMEM((seq_len, RET_DK), BF16),
                        pltpu.VMEM((n_chunks, RET_DK, RET_DV), BF16), pltpu.VMEM((BLOCK, BLOCK), F32),
                        pltpu.VMEM((4, BLOCK, RET_DV), F32), pltpu.VMEM((RET_DK, RET_DV), F32),
                        pltpu.VMEM((RET_DK, RET_DV), F32)],
        compiler_params=_cparams("arbitrary", "arbitrary"),
        name="retention_latent" if rope else "retention_prompt",
    )(*args)


def _head_norm(x, gain):
    return x * lax.rsqrt(jnp.mean(x * x, axis=-1, keepdims=True) + EPS) * gain


def _rope_full(x, cs, sn):
    return x * cs + pltpu.roll(x, ATT_HD // 2, 1) * sn


def _sink_column(sink_ref, kh, rows_per_head):
    n = ATT_GROUP * rows_per_head
    head = lax.broadcasted_iota(jnp.int32, (n, 1), 0) // rows_per_head
    col = jnp.full((n, 1), sink_ref[kh * ATT_GROUP], F32)
    for g in range(1, ATT_GROUP):
        col = jnp.where(head == g, sink_ref[kh * ATT_GROUP + g], col)
    return col


def _att_latent_kernel(sink_ref, q_ref, k_ref, v_ref, ck_ref, cv_ref, cos_ref, sin_ref, qn_ref, kn_ref,
                       fill_ref, o_ref, kp_ref, vp_ref, ckp_ref, cvp_ref):
    del fill_ref
    kh, qb = pl.program_id(1), pl.program_id(2)
    n_chunks = DEC_SEQ // BLOCK
    loc = 3 * BLOCK

    @pl.when(qb == 0)
    def _():
        def prep(c, carry):
            r = _rows(c)
            kp_ref[r, :] = _rope_full(_head_norm(k_ref[r, :], kn_ref[...]), cos_ref[r, :], sin_ref[r, :]).astype(BF16)
            vp_ref[r, :] = v_ref[r, :].astype(BF16)
            return carry

        lax.fori_loop(0, n_chunks, prep, 0)
        ckp_ref[...] = ck_ref[...].astype(BF16)
        cvp_ref[...] = cv_ref[...].astype(BF16)

    rq = _rows(qb)
    cs, sn = cos_ref[rq, :], sin_ref[rq, :]
    qg = qn_ref[...] * (ATT_HD ** -0.5)
    q = jnp.concatenate(
        [_rope_full(_head_norm(q_ref[:, g * ATT_HD:(g + 1) * ATT_HD], qg), cs, sn).astype(BF16)
         for g in range(ATT_GROUP)], axis=0)
    start = pl.multiple_of(jnp.clip((qb - 1) * BLOCK, 0, DEC_SEQ - loc), BLOCK)
    s_loc = _bdot_nt(q, kp_ref[pl.ds(start, loc), :])
    qpos = qb * BLOCK + (lax.broadcasted_iota(jnp.int32, (ATT_GROUP * BLOCK, loc), 0) & (BLOCK - 1))
    kpos = start + lax.broadcasted_iota(jnp.int32, (ATT_GROUP * BLOCK, loc), 1)
    s_loc = jnp.where(jnp.abs(qpos - kpos) <= WINDOW, s_loc, NEG_INF)
    s_ctx = _bdot_nt(q, ckp_ref[...])
    sink = _sink_column(sink_ref, kh, BLOCK)
    m = jnp.maximum(jnp.maximum(jnp.max(s_loc, axis=-1, keepdims=True), jnp.max(s_ctx, axis=-1, keepdims=True)),
                    sink)
    p_loc = jnp.exp(s_loc - m)
    p_ctx = jnp.exp(s_ctx - m)
    den = jnp.sum(p_loc, axis=-1, keepdims=True) + jnp.sum(p_ctx, axis=-1, keepdims=True) + jnp.exp(sink - m)
    o = (_bdot(p_loc, vp_ref[pl.ds(start, loc), :]) + _bdot(p_ctx, cvp_ref[...])) / den
    for g in range(ATT_GROUP):
        o_ref[:, g * ATT_HD:(g + 1) * ATT_HD] = o[g * BLOCK:(g + 1) * BLOCK, :].astype(o_ref.dtype)


def _attention_latent(proj, cache_k, cache_v, sink, cos_a, sin_a, q_norm, k_norm, fill):
    nqb = DEC_SEQ // BLOCK
    rb0 = T_PROMPT // DEC_SEQ
    qcol0 = 4 * RET_HEADS * RET_DK // (ATT_GROUP * ATT_HD)
    kcol0 = (4 * RET_HEADS * RET_DK + ATT_HEADS * ATT_HD) // ATT_HD
    vcol0 = kcol0 + ATT_KV_HEADS
    ck = cache_k.reshape(DEC_BATCH, PAST_LEN, ATT_KV_HEADS * ATT_HD)
    cv = cache_v.reshape(DEC_BATCH, PAST_LEN, ATT_KV_HEADS * ATT_HD)
    return pl.pallas_call(
        _att_latent_kernel,
        grid=(DEC_BATCH, ATT_KV_HEADS, nqb),
        in_specs=[pl.BlockSpec(memory_space=pltpu.SMEM),
                  pl.BlockSpec((BLOCK, ATT_GROUP * ATT_HD),
                               lambda b, kh, qb: (T_PROMPT // BLOCK + b * nqb + qb, qcol0 + kh)),
                  pl.BlockSpec((DEC_SEQ, ATT_HD), lambda b, kh, qb: (rb0 + b, kcol0 + kh)),
                  pl.BlockSpec((DEC_SEQ, ATT_HD), lambda b, kh, qb: (rb0 + b, vcol0 + kh)),
                  pl.BlockSpec((None, PAST_LEN, ATT_HD), lambda b, kh, qb: (b, 0, kh)),
                  pl.BlockSpec((None, PAST_LEN, ATT_HD), lambda b, kh, qb: (b, 0, kh)),
                  pl.BlockSpec((DEC_SEQ, ATT_HD), lambda b, kh, qb: (0, 0)),
                  pl.BlockSpec((DEC_SEQ, ATT_HD), lambda b, kh, qb: (0, 0)),
                  pl.BlockSpec((1, ATT_HD), lambda b, kh, qb: (0, 0)),
                  pl.BlockSpec((1, ATT_HD), lambda b, kh, qb: (0, 0)),
                  pl.BlockSpec(memory_space=pl.ANY)],
        out_specs=pl.BlockSpec((BLOCK, ATT_GROUP * ATT_HD),
                               lambda b, kh, qb: (T_PROMPT // BLOCK + b * nqb + qb, kh)),
        out_shape=jax.ShapeDtypeStruct((T_ALL, ATT_HEADS * ATT_HD), BF16),
        input_output_aliases={10: 0},
        scratch_shapes=[pltpu.VMEM((DEC_SEQ, ATT_HD), BF16), pltpu.VMEM((DEC_SEQ, ATT_HD), BF16),
                        pltpu.VMEM((PAST_LEN, ATT_HD), BF16), pltpu.VMEM((PAST_LEN, ATT_HD), BF16)],
        compiler_params=_cparams("arbitrary", "arbitrary", "arbitrary"),
        name="attention_latent",
    )(sink, proj, proj, proj, ck, cv, cos_a, sin_a, q_norm.reshape(1, ATT_HD), k_norm.reshape(1, ATT_HD), fill)


def _att_prompt_kernel(sink_ref, q_ref, k_ref, v_ref, qn_ref, kn_ref, o_ref, nk_ref, nv_ref):
    kh = pl.program_id(1)
    kn = _head_norm(k_ref[...], kn_ref[...])
    v = v_ref[...]
    nk_ref[...] = kn
    nv_ref[...] = v
    qg = qn_ref[...] * (ATT_HD ** -0.5)
    q = jnp.concatenate([_head_norm(q_ref[:, g * ATT_HD:(g + 1) * ATT_HD], qg).astype(BF16)
                         for g in range(ATT_GROUP)], axis=0)
    s = _bdot_nt(q, kn)
    sink = _sink_column(sink_ref, kh, SEQ)
    m = jnp.maximum(jnp.max(s, axis=-1, keepdims=True), sink)
    p = jnp.exp(s - m)
    den = jnp.sum(p, axis=-1, keepdims=True) + jnp.exp(sink - m)
    o = _bdot(p, v) / den
    for g in range(ATT_GROUP):
        o_ref[:, g * ATT_HD:(g + 1) * ATT_HD] = o[g * SEQ:(g + 1) * SEQ, :].astype(o_ref.dtype)


def _attention_prompt(proj, sink, q_norm, k_norm):
    qcol0 = 4 * RET_HEADS * RET_DK // (ATT_GROUP * ATT_HD)
    kcol0 = (4 * RET_HEADS * RET_DK + ATT_HEADS * ATT_HD) // ATT_HD
    vcol0 = kcol0 + ATT_KV_HEADS
    kv_spec = pl.BlockSpec((None, SEQ, ATT_HD), lambda b, kh: (b, 0, kh))
    kv_shape = jax.ShapeDtypeStruct((BATCH, SEQ, ATT_KV_HEADS * ATT_HD), F32)
    return pl.pallas_call(
        _att_prompt_kernel,
        grid=(BATCH, ATT_KV_HEADS),
        in_specs=[pl.BlockSpec(memory_space=pltpu.SMEM),
                  pl.BlockSpec((SEQ, ATT_GROUP * ATT_HD), lambda b, kh: (b, qcol0 + kh)),
                  pl.BlockSpec((SEQ, ATT_HD), lambda b, kh: (b, kcol0 + kh)),
                  pl.BlockSpec((SEQ, ATT_HD), lambda b, kh: (b, vcol0 + kh)),
                  pl.BlockSpec((1, ATT_HD), lambda b, kh: (0, 0)),
                  pl.BlockSpec((1, ATT_HD), lambda b, kh: (0, 0))],
        out_specs=[pl.BlockSpec((SEQ, ATT_GROUP * ATT_HD), lambda b, kh: (b, kh)), kv_spec, kv_spec],
        out_shape=[jax.ShapeDtypeStruct((T_ALL, ATT_HEADS * ATT_HD), BF16), kv_shape, kv_shape],
        compiler_params=_cparams("arbitrary", "arbitrary"),
        name="attention_prompt",
    )(sink, proj, proj, proj, q_norm.reshape(1, ATT_HD), k_norm.reshape(1, ATT_HD))


def _split_dot(m01, a):
    hi = a.astype(BF16)
    r1 = a - hi.astype(F32)
    mid = r1.astype(BF16)
    lo = (r1 - mid.astype(F32)).astype(BF16)
    return (jnp.dot(m01, hi, preferred_element_type=F32) + jnp.dot(m01, mid, preferred_element_type=F32)
            + jnp.dot(m01, lo, preferred_element_type=F32))


def _ssd_prep_kernel(raw_ref, bias_ref, alog_ref, cum_ref, dt_ref, w_ref, tot_ref):
    x = raw_ref[...] + bias_ref[...]
    dt = jnp.maximum(x, 0.0) + jnp.log1p(jnp.exp(-jnp.abs(x)))
    a = dt * (-jnp.exp(alog_ref[...]))
    ii = lax.broadcasted_iota(jnp.int32, (BLOCK, BLOCK), 0)
    jj = lax.broadcasted_iota(jnp.int32, (BLOCK, BLOCK), 1)
    incl = _split_dot(jnp.where(jj <= ii, 1.0, 0.0).astype(BF16), a)
    rincl = _split_dot(jnp.where(jj >= ii, 1.0, 0.0).astype(BF16), a)
    fwd_lane = lax.broadcasted_iota(jnp.int32, (BLOCK, LANES), 1) < SSD_HEADS
    cum = jnp.where(fwd_lane, incl, rincl)
    tot = jnp.where(fwd_lane[:1], incl[BLOCK - 1:BLOCK, :], rincl[0:1, :])
    w = dt * jnp.exp(tot - cum)
    cum_ref[...] = cum.T
    dt_ref[...] = dt.T
    w_ref[...] = w.T
    tot_ref[...] = jnp.broadcast_to(jnp.exp(tot), (BLOCK, LANES)).T


def _ssd_prep(dt_raw, dt_bias, a_log):
    nc = T_ALL // BLOCK
    spec = pl.BlockSpec((None, 2 * SSD_HEADS, BLOCK), lambda c: (c, 0, 0))
    shape = jax.ShapeDtypeStruct((nc, 2 * SSD_HEADS, BLOCK), F32)
    return pl.pallas_call(
        _ssd_prep_kernel,
        grid=(nc,),
        in_specs=[pl.BlockSpec((BLOCK, 2 * SSD_HEADS), lambda c: (c, 0)),
                  pl.BlockSpec((1, 2 * SSD_HEADS), lambda c: (0, 0)),
                  pl.BlockSpec((1, 2 * SSD_HEADS), lambda c: (0, 0))],
        out_specs=[spec] * 4,
        out_shape=[shape] * 4,
        compiler_params=_cparams("arbitrary"),
        name="ssd_prep",
    )(dt_raw, dt_bias.reshape(1, 2 * SSD_HEADS), a_log.reshape(1, 2 * SSD_HEADS))


GW = SSD_R * SSD_P
HALO = 8


def _pair_tiles(per_head):
    low = lax.broadcasted_iota(jnp.int32, per_head[0].shape, 1) < SSD_P
    return jnp.concatenate([jnp.where(low, per_head[2 * t], per_head[2 * t + 1]) for t in range(SSD_R // 2)],
                           axis=1)


def _row_bcast(ref, c, r):
    return jnp.broadcast_to(ref[c, r:r + 1, :], (BLOCK, BLOCK))


def _ssd_kernel(n_chunks, has_s0, emit_state, has_fill, *refs):
    it = iter(refs)
    z_ref, x_ref, b_ref, c_ref = (next(it) for _ in range(4))
    cumf_ref, cumb_ref, dtf_ref, dtb_ref, wf_ref, wb_ref, totf_ref, totb_ref = (next(it) for _ in range(8))
    cwx_ref, cwb_ref, cwc_ref, cbx_ref, cbb_ref, cbc_ref, d_ref = (next(it) for _ in range(7))
    s0_ref = next(it) if has_s0 else None
    if has_fill:
        next(it)
    o_ref = next(it)
    sfin_ref = next(it) if emit_state else None
    pad_ref, xc_ref, bc_ref, cc_ref, sfs_ref, sf_ref, sb_ref = (next(it) for _ in range(7))
    seq_len = n_chunks * BLOCK

    pad_ref[0:HALO, :] = jnp.zeros((HALO, GW + 2 * SSD_N), F32)
    pad_ref[HALO + seq_len:2 * HALO + seq_len, :] = jnp.zeros((HALO, GW + 2 * SSD_N), F32)

    def fill(c, carry):
        dst = pl.ds(pl.multiple_of(c * BLOCK, BLOCK) + HALO, BLOCK)
        r = _rows(c)
        pad_ref[dst, 0:GW] = x_ref[r, :]
        pad_ref[dst, GW:GW + SSD_N] = b_ref[r, :]
        pad_ref[dst, GW + SSD_N:GW + 2 * SSD_N] = c_ref[r, :]
        return carry

    lax.fori_loop(0, n_chunks, fill, 0)

    def conv(c, carry):
        r = _rows(c)
        win = pad_ref[pl.ds(pl.multiple_of(c * BLOCK, BLOCK), BLOCK + 2 * HALO), :]
        for lo, hi, cw_ref, cb_ref, dst in ((0, GW, cwx_ref, cbx_ref, xc_ref),
                                            (GW, GW + SSD_N, cwb_ref, cbb_ref, bc_ref),
                                            (GW + SSD_N, GW + 2 * SSD_N, cwc_ref, cbc_ref, cc_ref)):
            acc = jnp.broadcast_to(cb_ref[...], (BLOCK, hi - lo))
            for w in range(CONV_W):
                off = HALO - CONV_W // 2 + w
                acc = acc + win[off:off + BLOCK, lo:hi] * cw_ref[w:w + 1, :]
            dst[r, :] = _silu(acc).astype(dst.dtype)
        return carry

    lax.fori_loop(0, n_chunks, conv, 0)

    if has_s0:
        for r in range(SSD_R):
            sf_ref[:, r * SSD_P:(r + 1) * SSD_P] = s0_ref[0, r]
            sb_ref[:, r * SSD_P:(r + 1) * SSD_P] = s0_ref[1, r]
    else:
        sf_ref[...] = jnp.zeros_like(sf_ref)
        sb_ref[...] = jnp.zeros_like(sb_ref)

    def state_update(s_ref, c, w_ref, tot_ref, bm, xs):
        wcol = _pair_tiles([_row_bcast(w_ref, c, r).T for r in range(SSD_R)])
        tot = _pair_tiles([tot_ref[c, r:r + 1, :] for r in range(SSD_R)])
        s_ref[...] = s_ref[...] * tot + _bdot_tn(bm, xs * wcol)

    def fwd(c, carry):
        r = _rows(c)
        sfs_ref[c] = sf_ref[...].astype(BF16)
        state_update(sf_ref, c, wf_ref, totf_ref, bc_ref[r, :], xc_ref[r, :])
        return carry

    lax.fori_loop(0, n_chunks, fwd, 0)
    if emit_state:
        for r in range(SSD_R):
            sfin_ref[0, r] = sf_ref[:, r * SSD_P:(r + 1) * SSD_P]

    ii = lax.broadcasted_iota(jnp.int32, (BLOCK, BLOCK), 0)
    jj = lax.broadcasted_iota(jnp.int32, (BLOCK, BLOCK), 1)
    causal = jj <= ii
    low = lax.broadcasted_iota(jnp.int32, (BLOCK, LANES), 1) < SSD_P

    def bwd(t, carry):
        c = n_chunks - 1 - t
        r = _rows(c)
        cm, bm, xs = cc_ref[r, :], bc_ref[r, :], xc_ref[r, :]
        xb = xs.astype(BF16)
        sc = _bdot_nt(cm, bm)
        yf = _bdot(cm, sfs_ref[c])
        yb = _bdot(cm, sb_ref[...])
        ef, eb, tiles = [], [], []
        for t2 in range(SSD_R // 2):
            xt = xb[:, t2 * LANES:(t2 + 1) * LANES]
            acc = None
            for hh in range(2):
                hr = 2 * t2 + hh
                rf, rb = _row_bcast(cumf_ref, c, hr), _row_bcast(cumb_ref, c, hr)
                cf, cb = rf.T, rb.T
                e = jnp.exp(jnp.where(causal, cf - rf, cb - rb))
                e = e * jnp.where(causal, _row_bcast(dtf_ref, c, hr), _row_bcast(dtb_ref, c, hr))
                xh = jnp.where(low, xt, 0.0) if hh == 0 else jnp.where(low, 0.0, xt)
                part = _bdot(sc * e, xh)
                acc = part if acc is None else acc + part
                ef.append(jnp.exp(cf))
                eb.append(jnp.exp(cb))
            tiles.append(acc)
        y = jnp.concatenate(tiles, axis=1) + _pair_tiles(ef) * yf + _pair_tiles(eb) * yb + d_ref[...] * xs
        o_ref[r, :] = (y * _silu(z_ref[r, :])).astype(o_ref.dtype)
        state_update(sb_ref, c, wb_ref, totb_ref, bm, xs)
        return carry

    lax.fori_loop(0, n_chunks, bwd, 0)
    if emit_state:
        for r in range(SSD_R):
            sfin_ref[1, r] = sb_ref[:, r * SSD_P:(r + 1) * SSD_P]


def _ssd_scan(zx, prep, conv_w, conv_b, d_exp, seq_len, n_seq, row_block0, s0=None, emit_state=False, fill=None):
    n_chunks = seq_len // BLOCK
    has_s0 = s0 is not None
    xcol0 = D_INNER // GW
    bcol0 = 2 * D_INNER // SSD_N
    ccol0 = bcol0 + SSD_GROUPS
    cwb0 = D_INNER // SSD_N

    def rowcol(width, col0):
        return pl.BlockSpec((seq_len, width), lambda b, g, col0=col0: (b + row_block0, col0 + g))

    def headrows(direction):
        return pl.BlockSpec((n_chunks, SSD_R, BLOCK),
                            lambda b, g, direction=direction: (b + row_block0, direction * SSD_GROUPS + g, 0))

    in_specs = [rowcol(GW, 0), rowcol(GW, xcol0), rowcol(SSD_N, bcol0), rowcol(SSD_N, ccol0)]
    args = [zx, zx, zx, zx]
    for arr in prep:
        in_specs += [headrows(0), headrows(1)]
        args += [arr, arr]
    in_specs += [pl.BlockSpec((CONV_W, GW), lambda b, g: (0, g)),
                 pl.BlockSpec((CONV_W, SSD_N), lambda b, g: (0, cwb0 + g)),
                 pl.BlockSpec((CONV_W, SSD_N), lambda b, g: (0, cwb0 + SSD_GROUPS + g)),
                 pl.BlockSpec((1, GW), lambda b, g: (0, g)),
                 pl.BlockSpec((1, SSD_N), lambda b, g: (0, cwb0 + g)),
                 pl.BlockSpec((1, SSD_N), lambda b, g: (0, cwb0 + SSD_GROUPS + g)),
                 pl.BlockSpec((1, GW), lambda b, g: (0, g))]
    args += [conv_w, conv_w, conv_w, conv_b, conv_b, conv_b, d_exp]
    state_spec = pl.BlockSpec((None, None, 2, SSD_R, SSD_N, SSD_P), lambda b, g: (b, 0, 0, g, 0, 0))
    if has_s0:
        in_specs.append(state_spec)
        args.append(s0)
    aliases = _fill_alias(fill, in_specs, args)
    out_specs = [pl.BlockSpec((seq_len, GW), lambda b, g: (b + row_block0, g))]
    out_shape = [jax.ShapeDtypeStruct((T_ALL, D_INNER), BF16)]
    if emit_state:
        out_specs.append(state_spec)
        out_shape.append(jax.ShapeDtypeStruct((n_seq, 1, 2, SSD_HEADS, SSD_N, SSD_P), F32))
    return pl.pallas_call(
        functools.partial(_ssd_kernel, n_chunks, has_s0, emit_state, fill is not None),
        grid=(n_seq, SSD_GROUPS),
        in_specs=in_specs,
        out_specs=out_specs,
        out_shape=out_shape,
        input_output_aliases=aliases,
        scratch_shapes=[pltpu.VMEM((seq_len + 2 * HALO, GW + 2 * SSD_N), F32),
                        pltpu.VMEM((seq_len, GW), F32), pltpu.VMEM((seq_len, SSD_N), BF16),
                        pltpu.VMEM((seq_len, SSD_N), BF16), pltpu.VMEM((n_chunks, SSD_N, GW), BF16),
                        pltpu.VMEM((SSD_N, GW), F32), pltpu.VMEM((SSD_N, GW), F32)],
        compiler_params=_cparams("arbitrary", "arbitrary"),
        name="ssd_scan_latent" if has_s0 else "ssd_scan_prompt",
    )(*args)


def _router_kernel(x_ref, g_ref, mod_ref, rw_ref, hn_ref, idx_ref, wgt_ref, hf_ref):
    _adaln_to(x_ref, g_ref, mod_ref, 3, 4, hf_ref)
    hf = hf_ref[...]
    hn_ref[...] = hf.astype(BF16)
    logits = jnp.dot(hf, rw_ref[...], preferred_element_type=F32, precision=lax.Precision.HIGHEST)
    lane = lax.broadcasted_iota(jnp.int32, logits.shape, 1)
    lg = jnp.where(lane < N_EXPERTS, logits, NEG_INF)
    m1 = jnp.max(lg, axis=-1, keepdims=True)
    i1 = jnp.min(jnp.where(lg == m1, lane, LANES), axis=-1, keepdims=True)
    lg2 = jnp.where(lane == i1, NEG_INF, lg)
    m2 = jnp.max(lg2, axis=-1, keepdims=True)
    i2 = jnp.min(jnp.where(lg2 == m2, lane, LANES), axis=-1, keepdims=True)
    e2 = jnp.exp(m2 - m1)
    w1 = 1.0 / (1.0 + e2)
    idx_ref[...] = jnp.where(lane == 0, i1, jnp.where(lane == 1, i2, 0))
    wgt_ref[...] = jnp.where(lane == 0, w1, jnp.where(lane == 1, e2 * w1, 0.0))


def _router(x, gain, modt, router_w):
    t, d = x.shape
    rw = jnp.pad(router_w, ((0, 0), (0, LANES - N_EXPERTS)))
    return pl.pallas_call(
        _router_kernel,
        grid=(t // ROW_TILE,),
        in_specs=[pl.BlockSpec((ROW_TILE, d), lambda i: (i, 0)),
                  pl.BlockSpec((1, d), lambda i: (0, 0)),
                  pl.BlockSpec((None, 8, d), lambda i: (i, 0, 0)),
                  pl.BlockSpec((d, LANES), lambda i: (0, 0))],
        out_specs=[pl.BlockSpec((ROW_TILE, d), lambda i: (i, 0)),
                   pl.BlockSpec((ROW_TILE, LANES), lambda i: (i, 0)),
                   pl.BlockSpec((ROW_TILE, LANES), lambda i: (i, 0))],
        out_shape=[jax.ShapeDtypeStruct((t, d), BF16), jax.ShapeDtypeStruct((t, LANES), jnp.int32),
                   jax.ShapeDtypeStruct((t, LANES), F32)],
        scratch_shapes=[pltpu.VMEM((ROW_TILE, d), F32)],
        compiler_params=_cparams("arbitrary"),
        name="moe_router",
    )(x, gain.reshape(1, d), modt, rw)


DOWN_ROWS = 512
DOWN_TN = 512


def _expert_changed(be_ref, blk, prev_blk, step):
    return jnp.logical_or(step == 0, be_ref[blk] != be_ref[prev_blk])


def _expert_up_kernel(be_ref, nu_ref, xs_ref, wg_ref, wu_ref, h_ref, wgb_ref, wub_ref):
    i = pl.program_id(1)
    used = i < nu_ref[0]

    @pl.when(jnp.logical_and(used, _expert_changed(be_ref, i, jnp.maximum(i - 1, 0), i)))
    def _():
        wgb_ref[...] = wg_ref[...].astype(BF16)
        wub_ref[...] = wu_ref[...].astype(BF16)

    @pl.when(used)
    def _():
        xs = xs_ref[...]
        h_ref[...] = (_silu(_bdot(xs, wgb_ref[...])) * _bdot(xs, wub_ref[...])).astype(h_ref.dtype)

    @pl.when(jnp.logical_not(used))
    def _():
        h_ref[...] = jnp.zeros_like(h_ref)


def _expert_down_kernel(be_ref, nu_ref, h_ref, wd_ref, o_ref, wdb_ref):
    i = pl.program_id(1)
    per = MOE_ROWS // DOWN_ROWS
    blk = i // per
    used = blk < nu_ref[0]

    @pl.when(jnp.logical_and(used, _expert_changed(be_ref, blk, jnp.maximum(i - 1, 0) // per, i)))
    def _():
        wdb_ref[...] = wd_ref[...].astype(BF16)

    @pl.when(used)
    def _():
        o_ref[...] = _bdot(h_ref[...], wdb_ref[...]).astype(o_ref.dtype)

    @pl.when(jnp.logical_not(used))
    def _():
        o_ref[...] = jnp.zeros_like(o_ref)


def _experts(xs_sorted, block_e, n_used, wg, wu, wd):
    cap, d = xs_sorted.shape
    ff = wg.shape[2]
    per = MOE_ROWS // DOWN_ROWS

    def expert_of(blk, be, nu):
        return be[jnp.minimum(blk, jnp.maximum(nu[0] - 1, 0))]

    h = pl.pallas_call(
        _expert_up_kernel,
        grid_spec=pltpu.PrefetchScalarGridSpec(
            num_scalar_prefetch=2,
            grid=(ff // FFN_TF, cap // MOE_ROWS),
            in_specs=[pl.BlockSpec((MOE_ROWS, d), lambda f, i, be, nu: (i, 0)),
                      pl.BlockSpec((None, d, FFN_TF), lambda f, i, be, nu: (expert_of(i, be, nu), 0, f)),
                      pl.BlockSpec((None, d, FFN_TF), lambda f, i, be, nu: (expert_of(i, be, nu), 0, f))],
            out_specs=pl.BlockSpec((MOE_ROWS, FFN_TF), lambda f, i, be, nu: (i, f)),
            scratch_shapes=[pltpu.VMEM((d, FFN_TF), BF16), pltpu.VMEM((d, FFN_TF), BF16)],
        ),
        out_shape=jax.ShapeDtypeStruct((cap, ff), BF16),
        compiler_params=_cparams("arbitrary", "arbitrary"),
        name="moe_expert_up",
    )(block_e, n_used, xs_sorted, wg, wu)
    return pl.pallas_call(
        _expert_down_kernel,
        grid_spec=pltpu.PrefetchScalarGridSpec(
            num_scalar_prefetch=2,
            grid=(d // DOWN_TN, cap // DOWN_ROWS),
            in_specs=[pl.BlockSpec((DOWN_ROWS, ff), lambda n, i, be, nu: (i, 0)),
                      pl.BlockSpec((None, ff, DOWN_TN), lambda n, i, be, nu: (expert_of(i // per, be, nu), 0, n))],
            out_specs=pl.BlockSpec((DOWN_ROWS, DOWN_TN), lambda n, i, be, nu: (i, n)),
            scratch_shapes=[pltpu.VMEM((ff, DOWN_TN), BF16)],
        ),
        out_shape=jax.ShapeDtypeStruct((cap, d), BF16),
        compiler_params=_cparams("arbitrary", "arbitrary"),
        name="moe_expert_down",
    )(block_e, n_used, h, wd)


def _combine_kernel(x_ref, g0_ref, g1_ref, w_ref, mod_ref, o_ref):
    w = w_ref[...]
    y = w[:, 0:1] * g0_ref[...].astype(F32) + w[:, 1:2] * g1_ref[...].astype(F32)
    o_ref[...] = x_ref[...] + mod_ref[5:6, :] * y


COMB_ROWS = 512


def _combine(x, g0, g1, wgt, modt, row0, n_rows):
    d = x.shape[1]
    b0 = row0 // COMB_ROWS
    per = ROW_TILE // COMB_ROWS
    return pl.pallas_call(
        _combine_kernel,
        grid=(n_rows // COMB_ROWS,),
        in_specs=[pl.BlockSpec((COMB_ROWS, d), lambda i: (i + b0, 0)),
                  pl.BlockSpec((COMB_ROWS, d), lambda i: (i + b0, 0)),
                  pl.BlockSpec((COMB_ROWS, d), lambda i: (i + b0, 0)),
                  pl.BlockSpec((COMB_ROWS, LANES), lambda i: (i + b0, 0)),
                  pl.BlockSpec((None, 8, d), lambda i: ((i + b0) // per, 0, 0))],
        out_specs=pl.BlockSpec((COMB_ROWS, d), lambda i: (i, 0)),
        out_shape=jax.ShapeDtypeStruct((n_rows, d), F32),
        compiler_params=_cparams("arbitrary"),
        name="moe_combine",
    )(x, g0, g1, wgt, modt)


def _moe(x, gain, modt, router_w, wg, wu, wd):
    t, d = x.shape
    hn, idx, wgt = _router(x, gain, modt, router_w)
    top_idx = idx[:, :TOP_K]
    n_slots = t * TOP_K
    flat_e = top_idx.reshape(-1)
    onehot = (flat_e[:, None] == jnp.arange(N_EXPERTS, dtype=jnp.int32)[None, :]).astype(jnp.int32)
    incl = jnp.cumsum(onehot, axis=0)
    counts = incl[-1]
    rank = jnp.sum((incl - onehot) * onehot, axis=1)
    padded = (counts + MOE_ROWS - 1) // MOE_ROWS * MOE_ROWS
    pend = jnp.cumsum(padded)
    pstart = pend - padded
    dest = pstart[flat_e] + rank
    n_blocks = n_slots // MOE_ROWS + N_EXPERTS
    cap = n_blocks * MOE_ROWS
    row_tok = jnp.full((cap,), t, jnp.int32).at[dest].set(jnp.arange(n_slots, dtype=jnp.int32) // TOP_K)
    block_e = jnp.clip(jnp.searchsorted(pend, jnp.arange(n_blocks, dtype=jnp.int32) * MOE_ROWS, side='right'),
                       0, N_EXPERTS - 1).astype(jnp.int32)
    n_used = (pend[-1:] // MOE_ROWS).astype(jnp.int32)
    hn_pad = jnp.concatenate([hn, jnp.zeros((1, d), hn.dtype)], axis=0)
    out = _experts(hn_pad[row_tok], block_e, n_used, wg, wu, wd)
    dest = dest.reshape(t, TOP_K)
    g0, g1 = out[dest[:, 0]], out[dest[:, 1]]
    return (_combine(x, g0, g1, wgt, modt, 0, T_PROMPT), _combine(x, g0, g1, wgt, modt, T_PROMPT, T_SAMPLE))


def _rope_tables(n_tokens, dim):
    n_rows = n_tokens // GRID_W
    row = jnp.repeat(jnp.arange(n_rows), GRID_W).astype(F32)
    col = jnp.tile(jnp.arange(GRID_W), n_rows).astype(F32)
    n_freq = dim // 4
    inv = ROPE_BASE ** (-jnp.arange(n_freq, dtype=F32) / n_freq)
    ang = jnp.concatenate([row[:, None] * inv, col[:, None] * inv], axis=-1)
    return jnp.cos(ang), jnp.sin(ang)


def kernel(x_prompt, x_sample, state_ret, cache_k, cache_v, state_ssd, c, c_ctx, ada_w, ada_b, norm_mix, norm_ffn, ev_w_in, ev_w_out, ret_decay_logit, ret_norm, att_q_norm, att_k_norm, att_sink, ffn_w_gate, ffn_w_up, ffn_w_down, ssd_w_in, ssd_conv_w, ssd_conv_b, ssd_a_log, ssd_dt_bias, ssd_d, ssd_norm, ssd_w_out, moe_router, moe_w_gate, moe_w_up, moe_w_down):
    d = D_MODEL
    x = jnp.concatenate([x_prompt.reshape(T_PROMPT, d), x_sample.reshape(T_SAMPLE, d)], axis=0)

    cvecs = jnp.concatenate([c_ctx[None, :], c, jnp.zeros((MOD_ROWS - 1 - DEC_BATCH, d), F32)], axis=0)
    mods = _modulation(cvecs, ada_w, ada_b).reshape(2, MOD_ROWS, 6, d)
    tiles_per_seq = DEC_SEQ // ROW_TILE
    tile_row = jnp.concatenate([jnp.zeros((T_PROMPT // ROW_TILE,), jnp.int32),
                                1 + jnp.arange(T_SAMPLE // ROW_TILE, dtype=jnp.int32) // tiles_per_seq])
    modt = jnp.pad(mods[:, tile_row], ((0, 0), (0, 0), (0, 2), (0, 0)))

    proj = _adaln_matmul(x, norm_mix[0], modt[0], 0, 1, ev_w_in[0], tn=512, name="even_in_proj")
    lg = jax.nn.log_sigmoid(ret_decay_logit[0].astype(F32))
    cos_r, sin_r = _rope_tables(DEC_SEQ, RET_DK)
    cos_a, sin_a = _rope_tables(DEC_SEQ, ATT_HD)
    cos_a2 = jnp.concatenate([cos_a, cos_a], axis=-1)
    sin_a2 = jnp.concatenate([-sin_a, sin_a], axis=-1)
    mix_ret, new_state_ret = _retention(proj, lg, ret_norm[0], SEQ, BATCH, 0, emit_state=True)
    mix_ret, = _retention(proj, lg, ret_norm[0], DEC_SEQ, DEC_BATCH, T_PROMPT // DEC_SEQ,
                          ropes=(cos_r, sin_r), s0=state_ret, fill=mix_ret)
    mix_att, new_k, new_v = _attention_prompt(proj, att_sink[0], att_q_norm[0], att_k_norm[0])
    mix_att = _attention_latent(proj, cache_k[:, 0], cache_v[:, 0], att_sink[0], cos_a2, sin_a2,
                                att_q_norm[0], att_k_norm[0], mix_att)
    x = _proj_residual([mix_ret, mix_att], ev_w_out[0], x, modt[0], 2, tn=512, name="even_out_proj")
    h = _ffn_gateup(x, norm_ffn[0], modt[0], ffn_w_gate[0], ffn_w_up[0])
    x = _proj_residual([h], ffn_w_down[0], x, modt[0], 5, tn=256, name="ffn_down")

    zx = _adaln_matmul(x, norm_mix[1], modt[1], 0, 1, ssd_w_in[0], tn=512, n_out=SSD_ZX, name="ssd_in_proj")
    dt_raw = _adaln_matmul(x, norm_mix[1], modt[1], 0, 1, ssd_w_in[0], tn=2 * SSD_HEADS, n_out=2 * SSD_HEADS,
                           col_block0=SSD_ZX // (2 * SSD_HEADS), name="ssd_dt_proj")
    prep = _ssd_prep(dt_raw, ssd_dt_bias[0], ssd_a_log[0])
    d_exp = jnp.repeat(ssd_d[0], SSD_P)[None, :]
    conv_b = ssd_conv_b[0][None, :]
    yg, new_state_ssd = _ssd_scan(zx, prep, ssd_conv_w[0], conv_b, d_exp, SEQ, BATCH, 0, emit_state=True)
    yg, = _ssd_scan(zx, prep, ssd_conv_w[0], conv_b, d_exp, DEC_SEQ, DEC_BATCH, T_PROMPT // DEC_SEQ,
                    s0=state_ssd, fill=yg)
    x = _proj_residual([yg], ssd_w_out[0], x, modt[1], 2, tn=256, norm_gain=ssd_norm[0], name="ssd_out_proj")
    y_p, y_s = _moe(x, norm_ffn[1], modt[1], moe_router[0], moe_w_gate[0], moe_w_up[0], moe_w_down[0])

    y_prompt = y_p.reshape(BATCH, SEQ, d)
    y_sample = y_s.reshape(DEC_BATCH, DEC_SEQ, d)
    new_cache_k = new_k.reshape(BATCH, 1, SEQ, ATT_KV_HEADS, ATT_HD)
    new_cache_v = new_v.reshape(BATCH, 1, SEQ, ATT_KV_HEADS, ATT_HD)
    return (y_prompt, y_sample, new_state_ret, new_cache_k, new_cache_v, new_state_ssd)
```

```python
import functools

import jax
import jax.numpy as jnp
from jax import lax
from jax.experimental import pallas as pl
from jax.experimental.pallas import tpu as pltpu

F32 = jnp.float32
BF16 = jnp.bfloat16

D_MODEL = 2048
BATCH = 16
SEQ = 256
DEC_BATCH = 8
DEC_SEQ = 2048
PAST_LEN = 512
GRID_W = 64
BLOCK = 128
WINDOW = 128
EPS = 1e-6
ROPE_BASE = 10000.0
RET_HEADS = 4
RET_DK = 256
RET_DV = 256
ATT_HEADS = 8
ATT_KV_HEADS = 2
ATT_HD = 128
ATT_GROUP = ATT_HEADS // ATT_KV_HEADS
EVEN_IN = 5632
D_INNER = 2 * D_MODEL
SSD_P = 64
SSD_HEADS = D_INNER // SSD_P
SSD_N = 128
SSD_GROUPS = 8
SSD_R = SSD_HEADS // SSD_GROUPS
CONV_W = 5
CONV_CH = D_INNER + 2 * SSD_GROUPS * SSD_N
SSD_ZX = D_INNER + CONV_CH
D_FF = 5632
N_EXPERTS = 8
TOP_K = 2

T_PROMPT = BATCH * SEQ
T_SAMPLE = DEC_BATCH * DEC_SEQ
T_ALL = T_PROMPT + T_SAMPLE

LANES = 128
ROW_TILE = 1024
VMEM_LIMIT = 56 * 1024 * 1024
N_ROW_TILES = T_ALL // ROW_TILE
MOE_ROWS = 1024
NEG_INF = float("-inf")


def _cparams(*sem):
    return pltpu.CompilerParams(dimension_semantics=sem, vmem_limit_bytes=VMEM_LIMIT)


def _silu(x):
    return x * jax.nn.sigmoid(x)


def _bdot(a, b):
    return jnp.dot(a.astype(BF16), b.astype(BF16), preferred_element_type=F32)


def _bdot_nt(a, b):
    return lax.dot_general(a.astype(BF16), b.astype(BF16), (((1,), (1,)), ((), ())),
                           preferred_element_type=F32)


def _bdot_tn(a, b):
    return lax.dot_general(a.astype(BF16), b.astype(BF16), (((0,), (0,)), ((), ())),
                           preferred_element_type=F32)


def _rows(c, n=BLOCK):
    return pl.ds(pl.multiple_of(c * n, n), n)


MOD_ROWS = 16
MOD_TN = 1024


def _mod_kernel(c_ref, w_ref, b_ref, o_ref):
    o_ref[...] = _bdot(_silu(c_ref[...]), w_ref[...]) + b_ref[...]


def _modulation(cvecs, ada_w, ada_b):
    depth, d, n = ada_w.shape
    return pl.pallas_call(
        _mod_kernel,
        grid=(depth, n // MOD_TN),
        in_specs=[pl.BlockSpec((MOD_ROWS, d), lambda l, j: (0, 0)),
                  pl.BlockSpec((None, d, MOD_TN), lambda l, j: (l, 0, j)),
                  pl.BlockSpec((None, 1, MOD_TN), lambda l, j: (l, 0, j))],
        out_specs=pl.BlockSpec((None, MOD_ROWS, MOD_TN), lambda l, j: (l, 0, j)),
        out_shape=jax.ShapeDtypeStruct((depth, MOD_ROWS, n), F32),
        compiler_params=_cparams("arbitrary", "arbitrary"),
        name="modulation",
    )(cvecs, ada_w, ada_b.reshape(depth, 1, n))


ADALN_CHUNK = 64


def _adaln_to(x_ref, g_ref, mod_ref, shift_row, scale_row, hn_ref):
    mult = g_ref[...] * (1.0 + mod_ref[scale_row:scale_row + 1, :])
    shift = mod_ref[shift_row:shift_row + 1, :]

    def body(i, carry):
        r = _rows(i, ADALN_CHUNK)
        x = x_ref[r, :]
        ms = jnp.mean(x * x, axis=-1, keepdims=True)
        hn_ref[r, :] = (x * lax.rsqrt(ms + EPS) * mult + shift).astype(hn_ref.dtype)
        return carry

    lax.fori_loop(0, x_ref.shape[0] // ADALN_CHUNK, body, 0)


def _adaln_mm_kernel(shift_row, scale_row, x_ref, g_ref, mod_ref, w_ref, o_ref, hn_ref):
    @pl.when(pl.program_id(1) == 0)
    def _():
        _adaln_to(x_ref, g_ref, mod_ref, shift_row, scale_row, hn_ref)

    o_ref[...] = _bdot(hn_ref[...], w_ref[...]).astype(o_ref.dtype)


def _adaln_matmul(x, gain, modt, shift_row, scale_row, w, tn, n_out=None, col_block0=0, out_dtype=F32,
                  name="adaln_mm"):
    t, d = x.shape
    n_out = w.shape[1] if n_out is None else n_out
    return pl.pallas_call(
        functools.partial(_adaln_mm_kernel, shift_row, scale_row),
        grid=(t // ROW_TILE, n_out // tn),
        in_specs=[pl.BlockSpec((ROW_TILE, d), lambda i, j: (i, 0)),
                  pl.BlockSpec((1, d), lambda i, j: (0, 0)),
                  pl.BlockSpec((None, 8, d), lambda i, j: (i, 0, 0)),
                  pl.BlockSpec((d, tn), lambda i, j: (0, j + col_block0))],
        out_specs=pl.BlockSpec((ROW_TILE, tn), lambda i, j: (i, j)),
        out_shape=jax.ShapeDtypeStruct((t, n_out), out_dtype),
        scratch_shapes=[pltpu.VMEM((ROW_TILE, d), BF16)],
        compiler_params=_cparams("arbitrary", "arbitrary"),
        name=name,
    )(x, gain.reshape(1, d), modt, w)


FFN_TF = 512


def _gateup_kernel(x_ref, g_ref, mod_ref, wg_ref, wu_ref, h_ref, hn_ref):
    @pl.when(pl.program_id(1) == 0)
    def _():
        _adaln_to(x_ref, g_ref, mod_ref, 3, 4, hn_ref)

    hn = hn_ref[...]
    h_ref[...] = (_silu(_bdot(hn, wg_ref[...])) * _bdot(hn, wu_ref[...])).astype(h_ref.dtype)


def _ffn_gateup(x, gain, modt, wg, wu):
    t, d = x.shape
    ff = wg.shape[1]
    return pl.pallas_call(
        _gateup_kernel,
        grid=(t // ROW_TILE, ff // FFN_TF),
        in_specs=[pl.BlockSpec((ROW_TILE, d), lambda i, f: (i, 0)),
                  pl.BlockSpec((1, d), lambda i, f: (0, 0)),
                  pl.BlockSpec((None, 8, d), lambda i, f: (i, 0, 0)),
                  pl.BlockSpec((d, FFN_TF), lambda i, f: (0, f)),
                  pl.BlockSpec((d, FFN_TF), lambda i, f: (0, f))],
        out_specs=pl.BlockSpec((ROW_TILE, FFN_TF), lambda i, f: (i, f)),
        out_shape=jax.ShapeDtypeStruct((t, ff), BF16),
        scratch_shapes=[pltpu.VMEM((ROW_TILE, d), BF16)],
        compiler_params=_cparams("arbitrary", "arbitrary"),
        name="ffn_gateup",
    )(x, gain.reshape(1, d), modt, wg, wu)


def _proj_res_kernel(n_a, gate_row, norm, *refs):
    a_refs = refs[:n_a]
    w_refs = refs[n_a:2 * n_a]
    x_ref, mod_ref = refs[2 * n_a], refs[2 * n_a + 1]
    pos = 2 * n_a + 2
    o_ref = refs[pos + 1] if norm else refs[pos]
    acc = _bdot(a_refs[0][...], w_refs[0][...])
    for k in range(1, n_a):
        acc = acc + _bdot(a_refs[k][...], w_refs[k][...])
    if norm:
        ss_ref = refs[pos]
        k_total = sum(a.shape[1] for a in a_refs)
        ss = ss_ref[:, 0:LANES]
        for k in range(1, ss_ref.shape[1] // LANES):
            ss = ss + ss_ref[:, k * LANES:(k + 1) * LANES]
        rs = lax.rsqrt(ss * (1.0 / k_total) + EPS)
        acc = acc * jnp.concatenate([rs] * (acc.shape[1] // LANES), axis=1)
    o_ref[...] = x_ref[...] + mod_ref[gate_row:gate_row + 1, :] * acc


def _proj_residual(a_list, w, x, modt, gate_row, tn, row_ss=None, name="proj_res"):
    t, d = x.shape
    n_a = len(a_list)
    norm = row_ss is not None
    in_specs, args, k0 = [], [], 0
    for a in a_list:
        in_specs.append(pl.BlockSpec((ROW_TILE, a.shape[1]), lambda i, j: (i, 0)))
        args.append(a)
    for a in a_list:
        ka = a.shape[1]
        assert k0 % ka == 0
        in_specs.append(pl.BlockSpec((ka, tn), lambda i, j, kb=k0 // ka: (kb, j)))
        args.append(w)
        k0 += ka
    in_specs += [pl.BlockSpec((ROW_TILE, tn), lambda i, j: (i, j)),
                 pl.BlockSpec((None, 8, tn), lambda i, j: (i, 0, j))]
    args += [x, modt]
    if norm:
        in_specs.append(pl.BlockSpec((ROW_TILE, row_ss.shape[1]), lambda i, j: (i, 0)))
        args.append(row_ss)
    return pl.pallas_call(
        functools.partial(_proj_res_kernel, n_a, gate_row, norm),
        grid=(t // ROW_TILE, d // tn),
        in_specs=in_specs,
        out_specs=pl.BlockSpec((ROW_TILE, tn), lambda i, j: (i, j)),
        out_shape=jax.ShapeDtypeStruct((t, d), F32),
        compiler_params=_cparams("arbitrary", "arbitrary"),
        name=name,
    )(*args)


def _ret_kernel(n_chunks, rope, has_s0, emit_state, has_fill, lg_ref, *refs):
    it = iter(refs)
    q_ref, k_ref, v_ref, gt_ref = next(it), next(it), next(it), next(it)
    cos_ref, sin_ref = (next(it), next(it)) if rope else (None, None)
    s0_ref = next(it) if has_s0 else None
    gain_ref = next(it)
    if has_fill:
        next(it)
    o_ref = next(it)
    sfin_ref = next(it) if emit_state else None
    qs_ref, ks_ref, sfs_ref, dm_ref, dec_ref, sf_ref, sb_ref = (next(it) for _ in range(7))

    h = pl.program_id(1)
    lgf = lg_ref[0, h]
    lgb = lg_ref[1, h]
    half = RET_DK // 2

    def prep(c, carry):
        r = _rows(c)
        q = q_ref[r, :].astype(F32)
        k = k_ref[r, :].astype(F32) * (RET_DK ** -0.5)
        if rope:
            cs, sn = cos_ref[r, :], sin_ref[r, :]
            for src, dst in ((q, qs_ref), (k, ks_ref)):
                x1, x2 = src[:, :half], src[:, half:]
                dst[r, :half] = (x1 * cs - x2 * sn).astype(BF16)
                dst[r, half:] = (x1 * sn + x2 * cs).astype(BF16)
        else:
            qs_ref[r, :] = q.astype(BF16)
            ks_ref[r, :] = k.astype(BF16)
        return carry

    lax.fori_loop(0, n_chunks, prep, 0)

    ii = lax.broadcasted_iota(jnp.int32, (BLOCK, BLOCK), 0)
    jj = lax.broadcasted_iota(jnp.int32, (BLOCK, BLOCK), 1)
    diff = (ii - jj).astype(F32)
    dm_ref[...] = jnp.exp(jnp.where(jj <= ii, diff * lgf, -diff * lgb))
    pos = lax.broadcasted_iota(jnp.int32, (BLOCK, RET_DV), 0).astype(F32)
    dec_ref[0] = jnp.exp((pos + 1.0) * lgf)
    dec_ref[1] = jnp.exp((BLOCK - pos) * lgb)
    dec_ref[2] = jnp.exp((BLOCK - 1.0 - pos) * lgf)
    dec_ref[3] = jnp.exp(pos * lgb)
    tot_f = jnp.exp(jnp.full((1, RET_DV), BLOCK * lgf, F32))
    tot_b = jnp.exp(jnp.full((1, RET_DV), BLOCK * lgb, F32))

    if has_s0:
        sf_ref[...] = s0_ref[0]
        sb_ref[...] = s0_ref[1]
    else:
        sf_ref[...] = jnp.zeros_like(sf_ref)
        sb_ref[...] = jnp.zeros_like(sb_ref)

    def fwd(c, carry):
        r = _rows(c)
        sfs_ref[c] = sf_ref[...].astype(BF16)
        kd = ks_ref[r, :].astype(F32) * dec_ref[2]
        sf_ref[...] = sf_ref[...] * tot_f + _bdot_tn(kd, v_ref[r, :])
        return carry

    lax.fori_loop(0, n_chunks, fwd, 0)
    if emit_state:
        sfin_ref[0] = sf_ref[...]

    def bwd(t, carry):
        c = n_chunks - 1 - t
        r = _rows(c)
        q = qs_ref[r, :]
        k = ks_ref[r, :]
        v = v_ref[r, :].astype(BF16)
        p = _bdot_nt(q, k) * dm_ref[...]
        o = _bdot(p, v)
        o = o + _bdot(q, sfs_ref[c]) * dec_ref[0]
        o = o + _bdot(q, sb_ref[...]) * dec_ref[1]
        ms = jnp.mean(o * o, axis=-1, keepdims=True)
        y = o * lax.rsqrt(ms + EPS) * gain_ref[...]
        o_ref[r, :] = (y * _silu(gt_ref[r, :].astype(F32))).astype(o_ref.dtype)
        kd = k.astype(F32) * dec_ref[3]
        sb_ref[...] = sb_ref[...] * tot_b + _bdot_tn(kd, v)
        return carry

    lax.fori_loop(0, n_chunks, bwd, 0)
    if emit_state:
        sfin_ref[1] = sb_ref[...]


def _fill_alias(fill, in_specs, args):
    aliases = {}
    for k, arr in enumerate(fill or ()):
        in_specs.append(pl.BlockSpec(memory_space=pl.ANY))
        args.append(arr)
        aliases[len(args) - 1] = k
    return aliases


def _retention(proj, lg, ret_norm, seq_len, n_seq, row_block0, ropes=None, s0=None, emit_state=False, fill=None):
    n_chunks = seq_len // BLOCK
    rope, has_s0 = ropes is not None, s0 is not None

    def col(cb):
        return pl.BlockSpec((seq_len, RET_DK), lambda b, h, cb=cb: (b + row_block0, cb * RET_HEADS + h))

    in_specs = [pl.BlockSpec(memory_space=pltpu.SMEM), col(0), col(1), col(2), col(3)]
    args = [lg, proj, proj, proj, proj]
    if rope:
        in_specs += [pl.BlockSpec((seq_len, RET_DK // 2), lambda b, h: (0, 0))] * 2
        args += list(ropes)
    if has_s0:
        in_specs.append(pl.BlockSpec((None, None, 2, None, RET_DK, RET_DV), lambda b, h: (b, 0, 0, h, 0, 0)))
        args.append(s0)
    in_specs.append(pl.BlockSpec((1, RET_DV), lambda b, h: (0, h)))
    args.append(ret_norm.reshape(1, RET_HEADS * RET_DV))
    aliases = _fill_alias(fill, in_specs, args)
    out_specs = [pl.BlockSpec((seq_len, RET_DV), lambda b, h: (b + row_block0, h))]
    out_shape = [jax.ShapeDtypeStruct((T_ALL, RET_HEADS * RET_DV), BF16)]
    if emit_state:
        out_specs.append(pl.BlockSpec((None, None, 2, None, RET_DK, RET_DV), lambda b, h: (b, 0, 0, h, 0, 0)))
        out_shape.append(jax.ShapeDtypeStruct((n_seq, 1, 2, RET_HEADS, RET_DK, RET_DV), F32))
    return pl.pallas_call(
        functools.partial(_ret_kernel, n_chunks, rope, has_s0, emit_state, bool(fill)),
        grid=(n_seq, RET_HEADS),
        in_specs=in_specs,
        out_specs=out_specs,
        out_shape=out_shape,
        input_output_aliases=aliases,
        scratch_shapes=[pltpu.VMEM((seq_len, RET_DK), BF16), pltpu.VMEM((seq_len, RET_DK), BF16),
                        pltpu.VMEM((n_chunks, RET_DK, RET_DV), BF16), pltpu.VMEM((BLOCK, BLOCK), F32),
                        pltpu.VMEM((4, BLOCK, RET_DV), F32), pltpu.VMEM((RET_DK, RET_DV), F32),
                        pltpu.VMEM((RET_DK, RET_DV), F32)],
        compiler_params=_cparams("arbitrary", "arbitrary"),
        name="retention_latent" if rope else "retention_prompt",
    )(*args)


def _head_norm(x, gain):
    x = x.astype(F32)
    return x * lax.rsqrt(jnp.mean(x * x, axis=-1, keepdims=True) + EPS) * gain


def _rope_full(x, cs, sn):
    return x * cs + pltpu.roll(x, ATT_HD // 2, 1) * sn


def _sink_column(sink_ref, kh, rows_per_head):
    n = ATT_GROUP * rows_per_head
    head = lax.broadcasted_iota(jnp.int32, (n, 1), 0) // rows_per_head
    col = jnp.full((n, 1), sink_ref[kh * ATT_GROUP], F32)
    for g in range(1, ATT_GROUP):
        col = jnp.where(head == g, sink_ref[kh * ATT_GROUP + g], col)
    return col


def _att_latent_kernel(sink_ref, q_ref, k_ref, v_ref, ck_ref, cv_ref, cos_ref, sin_ref, qn_ref, kn_ref,
                       fill_ref, o_ref, kp_ref, vp_ref, ckp_ref, cvp_ref):
    del fill_ref
    kh, qb = pl.program_id(1), pl.program_id(2)
    n_chunks = DEC_SEQ // BLOCK
    loc = 3 * BLOCK

    @pl.when(qb == 0)
    def _():
        def prep(c, carry):
            r = _rows(c)
            kp_ref[r, :] = _rope_full(_head_norm(k_ref[r, :], kn_ref[...]), cos_ref[r, :], sin_ref[r, :]).astype(BF16)
            vp_ref[r, :] = v_ref[r, :].astype(BF16)
            return carry

        lax.fori_loop(0, n_chunks, prep, 0)
        ckp_ref[...] = ck_ref[...].astype(BF16)
        cvp_ref[...] = cv_ref[...].astype(BF16)

    rq = _rows(qb)
    cs, sn = cos_ref[rq, :], sin_ref[rq, :]
    qg = qn_ref[...] * (ATT_HD ** -0.5)
    q = jnp.concatenate(
        [_rope_full(_head_norm(q_ref[:, g * ATT_HD:(g + 1) * ATT_HD], qg), cs, sn).astype(BF16)
         for g in range(ATT_GROUP)], axis=0)
    start = pl.multiple_of(jnp.clip((qb - 1) * BLOCK, 0, DEC_SEQ - loc), BLOCK)
    s_loc = _bdot_nt(q, kp_ref[pl.ds(start, loc), :])
    qpos = qb * BLOCK + (lax.broadcasted_iota(jnp.int32, (ATT_GROUP * BLOCK, loc), 0) & (BLOCK - 1))
    kpos = start + lax.broadcasted_iota(jnp.int32, (ATT_GROUP * BLOCK, loc), 1)
    s_loc = jnp.where(jnp.abs(qpos - kpos) <= WINDOW, s_loc, NEG_INF)
    s_ctx = _bdot_nt(q, ckp_ref[...])
    sink = _sink_column(sink_ref, kh, BLOCK)
    m = jnp.maximum(jnp.maximum(jnp.max(s_loc, axis=-1, keepdims=True), jnp.max(s_ctx, axis=-1, keepdims=True)),
                    sink)
    p_loc = jnp.exp(s_loc - m)
    p_ctx = jnp.exp(s_ctx - m)
    den = jnp.sum(p_loc, axis=-1, keepdims=True) + jnp.sum(p_ctx, axis=-1, keepdims=True) + jnp.exp(sink - m)
    o = (_bdot(p_loc, vp_ref[pl.ds(start, loc), :]) + _bdot(p_ctx, cvp_ref[...])) / den
    for g in range(ATT_GROUP):
        o_ref[:, g * ATT_HD:(g + 1) * ATT_HD] = o[g * BLOCK:(g + 1) * BLOCK, :].astype(o_ref.dtype)


def _attention_latent(proj, cache_k, cache_v, sink, cos_a, sin_a, q_norm, k_norm, fill):
    nqb = DEC_SEQ // BLOCK
    rb0 = T_PROMPT // DEC_SEQ
    qcol0 = 4 * RET_HEADS * RET_DK // (ATT_GROUP * ATT_HD)
    kcol0 = (4 * RET_HEADS * RET_DK + ATT_HEADS * ATT_HD) // ATT_HD
    vcol0 = kcol0 + ATT_KV_HEADS
    ck = cache_k.reshape(DEC_BATCH, PAST_LEN, ATT_KV_HEADS * ATT_HD)
    cv = cache_v.reshape(DEC_BATCH, PAST_LEN, ATT_KV_HEADS * ATT_HD)
    return pl.pallas_call(
        _att_latent_kernel,
        grid=(DEC_BATCH, ATT_KV_HEADS, nqb),
        in_specs=[pl.BlockSpec(memory_space=pltpu.SMEM),
                  pl.BlockSpec((BLOCK, ATT_GROUP * ATT_HD),
                               lambda b, kh, qb: (T_PROMPT // BLOCK + b * nqb + qb, qcol0 + kh)),
                  pl.BlockSpec((DEC_SEQ, ATT_HD), lambda b, kh, qb: (rb0 + b, kcol0 + kh)),
                  pl.BlockSpec((DEC_SEQ, ATT_HD), lambda b, kh, qb: (rb0 + b, vcol0 + kh)),
                  pl.BlockSpec((None, PAST_LEN, ATT_HD), lambda b, kh, qb: (b, 0, kh)),
                  pl.BlockSpec((None, PAST_LEN, ATT_HD), lambda b, kh, qb: (b, 0, kh)),
                  pl.BlockSpec((DEC_SEQ, ATT_HD), lambda b, kh, qb: (0, 0)),
                  pl.BlockSpec((DEC_SEQ, ATT_HD), lambda b, kh, qb: (0, 0)),
                  pl.BlockSpec((1, ATT_HD), lambda b, kh, qb: (0, 0)),
                  pl.BlockSpec((1, ATT_HD), lambda b, kh, qb: (0, 0)),
                  pl.BlockSpec(memory_space=pl.ANY)],
        out_specs=pl.BlockSpec((BLOCK, ATT_GROUP * ATT_HD),
                               lambda b, kh, qb: (T_PROMPT // BLOCK + b * nqb + qb, kh)),
        out_shape=jax.ShapeDtypeStruct((T_ALL, ATT_HEADS * ATT_HD), BF16),
        input_output_aliases={10: 0},
        scratch_shapes=[pltpu.VMEM((DEC_SEQ, ATT_HD), BF16), pltpu.VMEM((DEC_SEQ, ATT_HD), BF16),
                        pltpu.VMEM((PAST_LEN, ATT_HD), BF16), pltpu.VMEM((PAST_LEN, ATT_HD), BF16)],
        compiler_params=_cparams("arbitrary", "arbitrary", "arbitrary"),
        name="attention_latent",
    )(sink, proj, proj, proj, ck, cv, cos_a, sin_a, q_norm.reshape(1, ATT_HD), k_norm.reshape(1, ATT_HD), fill)


def _att_prompt_kernel(sink_ref, q_ref, k_ref, v_ref, qn_ref, kn_ref, o_ref, nk_ref, nv_ref):
    kh = pl.program_id(1)
    kn = _head_norm(k_ref[...], kn_ref[...])
    v = v_ref[...]
    nk_ref[...] = kn
    nv_ref[...] = v.astype(F32)
    qg = qn_ref[...] * (ATT_HD ** -0.5)
    q = jnp.concatenate([_head_norm(q_ref[:, g * ATT_HD:(g + 1) * ATT_HD], qg).astype(BF16)
                         for g in range(ATT_GROUP)], axis=0)
    s = _bdot_nt(q, kn)
    sink = _sink_column(sink_ref, kh, SEQ)
    m = jnp.maximum(jnp.max(s, axis=-1, keepdims=True), sink)
    p = jnp.exp(s - m)
    den = jnp.sum(p, axis=-1, keepdims=True) + jnp.exp(sink - m)
    o = _bdot(p, v) / den
    for g in range(ATT_GROUP):
        o_ref[:, g * ATT_HD:(g + 1) * ATT_HD] = o[g * SEQ:(g + 1) * SEQ, :].astype(o_ref.dtype)


def _attention_prompt(proj, sink, q_norm, k_norm):
    qcol0 = 4 * RET_HEADS * RET_DK // (ATT_GROUP * ATT_HD)
    kcol0 = (4 * RET_HEADS * RET_DK + ATT_HEADS * ATT_HD) // ATT_HD
    vcol0 = kcol0 + ATT_KV_HEADS
    kv_spec = pl.BlockSpec((None, SEQ, ATT_HD), lambda b, kh: (b, 0, kh))
    kv_shape = jax.ShapeDtypeStruct((BATCH, SEQ, ATT_KV_HEADS * ATT_HD), F32)
    return pl.pallas_call(
        _att_prompt_kernel,
        grid=(BATCH, ATT_KV_HEADS),
        in_specs=[pl.BlockSpec(memory_space=pltpu.SMEM),
                  pl.BlockSpec((SEQ, ATT_GROUP * ATT_HD), lambda b, kh: (b, qcol0 + kh)),
                  pl.BlockSpec((SEQ, ATT_HD), lambda b, kh: (b, kcol0 + kh)),
                  pl.BlockSpec((SEQ, ATT_HD), lambda b, kh: (b, vcol0 + kh)),
                  pl.BlockSpec((1, ATT_HD), lambda b, kh: (0, 0)),
                  pl.BlockSpec((1, ATT_HD), lambda b, kh: (0, 0))],
        out_specs=[pl.BlockSpec((SEQ, ATT_GROUP * ATT_HD), lambda b, kh: (b, kh)), kv_spec, kv_spec],
        out_shape=[jax.ShapeDtypeStruct((T_ALL, ATT_HEADS * ATT_HD), BF16), kv_shape, kv_shape],
        compiler_params=_cparams("arbitrary", "arbitrary"),
        name="attention_prompt",
    )(sink, proj, proj, proj, q_norm.reshape(1, ATT_HD), k_norm.reshape(1, ATT_HD))


def _split_dot(m01, a):
    hi = a.astype(BF16)
    r1 = a - hi.astype(F32)
    mid = r1.astype(BF16)
    lo = (r1 - mid.astype(F32)).astype(BF16)
    return (jnp.dot(m01, hi, preferred_element_type=F32) + jnp.dot(m01, mid, preferred_element_type=F32)
            + jnp.dot(m01, lo, preferred_element_type=F32))


def _ssd_prep_kernel(raw_ref, bias_ref, alog_ref, cum_ref, dt_ref, w_ref, tot_ref):
    x = raw_ref[...] + bias_ref[...]
    dt = jnp.maximum(x, 0.0) + jnp.log1p(jnp.exp(-jnp.abs(x)))
    a = dt * (-jnp.exp(alog_ref[...]))
    ii = lax.broadcasted_iota(jnp.int32, (BLOCK, BLOCK), 0)
    jj = lax.broadcasted_iota(jnp.int32, (BLOCK, BLOCK), 1)
    incl = _split_dot(jnp.where(jj <= ii, 1.0, 0.0).astype(BF16), a)
    rincl = _split_dot(jnp.where(jj >= ii, 1.0, 0.0).astype(BF16), a)
    fwd_lane = lax.broadcasted_iota(jnp.int32, (BLOCK, LANES), 1) < SSD_HEADS
    cum = jnp.where(fwd_lane, incl, rincl)
    tot = jnp.where(fwd_lane[:1], incl[BLOCK - 1:BLOCK, :], rincl[0:1, :])
    w = dt * jnp.exp(tot - cum)
    cum_ref[...] = cum.T
    dt_ref[...] = dt.T
    w_ref[...] = w.T
    tot_ref[...] = jnp.broadcast_to(jnp.exp(tot), (BLOCK, LANES)).T


def _ssd_prep(dt_raw, dt_bias, a_log):
    nc = T_ALL // BLOCK
    spec = pl.BlockSpec((None, 2 * SSD_HEADS, BLOCK), lambda c: (c, 0, 0))
    shape = jax.ShapeDtypeStruct((nc, 2 * SSD_HEADS, BLOCK), F32)
    return pl.pallas_call(
        _ssd_prep_kernel,
        grid=(nc,),
        in_specs=[pl.BlockSpec((BLOCK, 2 * SSD_HEADS), lambda c: (c, 0)),
                  pl.BlockSpec((1, 2 * SSD_HEADS), lambda c: (0, 0)),
                  pl.BlockSpec((1, 2 * SSD_HEADS), lambda c: (0, 0))],
        out_specs=[spec] * 4,
        out_shape=[shape] * 4,
        compiler_params=_cparams("arbitrary"),
        name="ssd_prep",
    )(dt_raw, dt_bias.reshape(1, 2 * SSD_HEADS), a_log.reshape(1, 2 * SSD_HEADS))


GW = SSD_R * SSD_P
HALO = 8


def _pair_tiles(per_head):
    low = lax.broadcasted_iota(jnp.int32, per_head[0].shape, 1) < SSD_P
    return jnp.concatenate([jnp.where(low, per_head[2 * t], per_head[2 * t + 1]) for t in range(SSD_R // 2)],
                           axis=1)


def _row_bcast(ref, c, r):
    return jnp.broadcast_to(ref[c, r:r + 1, :], (BLOCK, BLOCK))


def _ssd_kernel(n_chunks, has_s0, emit_state, has_fill, *refs):
    it = iter(refs)
    z_ref, x_ref, b_ref, c_ref = (next(it) for _ in range(4))
    cumf_ref, cumb_ref, dtf_ref, dtb_ref, wf_ref, wb_ref, totf_ref, totb_ref = (next(it) for _ in range(8))
    cwx_ref, cwb_ref, cwc_ref, cbx_ref, cbb_ref, cbc_ref, d_ref, ng_ref = (next(it) for _ in range(8))
    s0_ref = next(it) if has_s0 else None
    if has_fill:
        next(it), next(it)
    o_ref, ss_ref = next(it), next(it)
    sfin_ref = next(it) if emit_state else None
    pad_ref, xc_ref, bc_ref, cc_ref, sfs_ref, sf_ref, sb_ref = (next(it) for _ in range(7))
    seq_len = n_chunks * BLOCK

    pad_ref[0:HALO, :] = jnp.zeros((HALO, GW + 2 * SSD_N), F32)
    pad_ref[HALO + seq_len:2 * HALO + seq_len, :] = jnp.zeros((HALO, GW + 2 * SSD_N), F32)

    def fill(c, carry):
        dst = pl.ds(pl.multiple_of(c * BLOCK, BLOCK) + HALO, BLOCK)
        r = _rows(c)
        pad_ref[dst, 0:GW] = x_ref[r, :].astype(F32)
        pad_ref[dst, GW:GW + SSD_N] = b_ref[r, :].astype(F32)
        pad_ref[dst, GW + SSD_N:GW + 2 * SSD_N] = c_ref[r, :].astype(F32)
        return carry

    lax.fori_loop(0, n_chunks, fill, 0)

    def conv(c, carry):
        r = _rows(c)
        win = pad_ref[pl.ds(pl.multiple_of(c * BLOCK, BLOCK), BLOCK + 2 * HALO), :]
        for lo, hi, cw_ref, cb_ref, dst in ((0, GW, cwx_ref, cbx_ref, xc_ref),
                                            (GW, GW + SSD_N, cwb_ref, cbb_ref, bc_ref),
                                            (GW + SSD_N, GW + 2 * SSD_N, cwc_ref, cbc_ref, cc_ref)):
            acc = jnp.broadcast_to(cb_ref[...], (BLOCK, hi - lo))
            for w in range(CONV_W):
                off = HALO - CONV_W // 2 + w
                acc = acc + win[off:off + BLOCK, lo:hi] * cw_ref[w:w + 1, :]
            dst[r, :] = _silu(acc).astype(dst.dtype)
        return carry

    lax.fori_loop(0, n_chunks, conv, 0)

    if has_s0:
        for r in range(SSD_R):
            sf_ref[:, r * SSD_P:(r + 1) * SSD_P] = s0_ref[0, r]
            sb_ref[:, r * SSD_P:(r + 1) * SSD_P] = s0_ref[1, r]
    else:
        sf_ref[...] = jnp.zeros_like(sf_ref)
        sb_ref[...] = jnp.zeros_like(sb_ref)

    def state_update(s_ref, c, w_ref, tot_ref, bm, xs):
        wcol = _pair_tiles([_row_bcast(w_ref, c, r).T for r in range(SSD_R)])
        tot = _pair_tiles([tot_ref[c, r:r + 1, :] for r in range(SSD_R)])
        s_ref[...] = s_ref[...] * tot + _bdot_tn(bm, xs * wcol)

    def fwd(c, carry):
        r = _rows(c)
        sfs_ref[c] = sf_ref[...].astype(BF16)
        state_update(sf_ref, c, wf_ref, totf_ref, bc_ref[r, :], xc_ref[r, :])
        return carry

    lax.fori_loop(0, n_chunks, fwd, 0, unroll=2)
    if emit_state:
        for r in range(SSD_R):
            sfin_ref[0, r] = sf_ref[:, r * SSD_P:(r + 1) * SSD_P]

    ii = lax.broadcasted_iota(jnp.int32, (BLOCK, BLOCK), 0)
    jj = lax.broadcasted_iota(jnp.int32, (BLOCK, BLOCK), 1)
    causal = jj <= ii
    low = lax.broadcasted_iota(jnp.int32, (BLOCK, LANES), 1) < SSD_P

    def bwd(t, carry):
        c = n_chunks - 1 - t
        r = _rows(c)
        cm, bm, xs = cc_ref[r, :], bc_ref[r, :], xc_ref[r, :]
        xb = xs.astype(BF16)
        sc = _bdot_nt(cm, bm)
        yf = _bdot(cm, sfs_ref[c])
        yb = _bdot(cm, sb_ref[...])
        ef, eb, tiles = [], [], []
        for t2 in range(SSD_R // 2):
            xt = xb[:, t2 * LANES:(t2 + 1) * LANES]
            acc = None
            for hh in range(2):
                hr = 2 * t2 + hh
                rf, rb = _row_bcast(cumf_ref, c, hr), _row_bcast(cumb_ref, c, hr)
                cf, cb = rf.T, rb.T
                e = jnp.exp(jnp.where(causal, cf - rf, cb - rb))
                e = e * jnp.where(causal, _row_bcast(dtf_ref, c, hr), _row_bcast(dtb_ref, c, hr))
                xh = jnp.where(low, xt, 0.0) if hh == 0 else jnp.where(low, 0.0, xt)
                part = _bdot(sc * e, xh)
                acc = part if acc is None else acc + part
                ef.append(jnp.exp(cf))
                eb.append(jnp.exp(cb))
            tiles.append(acc)
        y = jnp.concatenate(tiles, axis=1) + _pair_tiles(ef) * yf + _pair_tiles(eb) * yb + d_ref[...] * xs
        yg = y * _silu(z_ref[r, :].astype(F32))
        ss_ref[r, :] = jnp.broadcast_to(jnp.sum(yg * yg, axis=-1, keepdims=True), (BLOCK, LANES))
        o_ref[r, :] = (yg * ng_ref[...]).astype(o_ref.dtype)
        state_update(sb_ref, c, wb_ref, totb_ref, bm, xs)
        return carry

    lax.fori_loop(0, n_chunks, bwd, 0)
    if emit_state:
        for r in range(SSD_R):
            sfin_ref[1, r] = sb_ref[:, r * SSD_P:(r + 1) * SSD_P]


def _ssd_scan(zx, prep, conv_w, conv_b, d_exp, norm_gain, seq_len, n_seq, row_block0, s0=None, emit_state=False,
              fill=None):
    n_chunks = seq_len // BLOCK
    has_s0 = s0 is not None
    xcol0 = D_INNER // GW
    bcol0 = 2 * D_INNER // SSD_N
    ccol0 = bcol0 + SSD_GROUPS
    cwb0 = D_INNER // SSD_N

    def rowcol(width, col0):
        return pl.BlockSpec((seq_len, width), lambda b, g, col0=col0: (b + row_block0, col0 + g))

    def headrows(direction):
        return pl.BlockSpec((n_chunks, SSD_R, BLOCK),
                            lambda b, g, direction=direction: (b + row_block0, direction * SSD_GROUPS + g, 0))

    in_specs = [rowcol(GW, 0), rowcol(GW, xcol0), rowcol(SSD_N, bcol0), rowcol(SSD_N, ccol0)]
    args = [zx, zx, zx, zx]
    for arr in prep:
        in_specs += [headrows(0), headrows(1)]
        args += [arr, arr]
    in_specs += [pl.BlockSpec((CONV_W, GW), lambda b, g: (0, g)),
                 pl.BlockSpec((CONV_W, SSD_N), lambda b, g: (0, cwb0 + g)),
                 pl.BlockSpec((CONV_W, SSD_N), lambda b, g: (0, cwb0 + SSD_GROUPS + g)),
                 pl.BlockSpec((1, GW), lambda b, g: (0, g)),
                 pl.BlockSpec((1, SSD_N), lambda b, g: (0, cwb0 + g)),
                 pl.BlockSpec((1, SSD_N), lambda b, g: (0, cwb0 + SSD_GROUPS + g)),
                 pl.BlockSpec((1, GW), lambda b, g: (0, g)),
                 pl.BlockSpec((1, GW), lambda b, g: (0, g))]
    args += [conv_w, conv_w, conv_w, conv_b, conv_b, conv_b, d_exp, norm_gain.reshape(1, D_INNER)]
    state_spec = pl.BlockSpec((None, None, 2, SSD_R, SSD_N, SSD_P), lambda b, g: (b, 0, 0, g, 0, 0))
    if has_s0:
        in_specs.append(state_spec)
        args.append(s0)
    aliases = _fill_alias(fill, in_specs, args)
    out_specs = [pl.BlockSpec((seq_len, GW), lambda b, g: (b + row_block0, g)),
                 pl.BlockSpec((seq_len, LANES), lambda b, g: (b + row_block0, g))]
    out_shape = [jax.ShapeDtypeStruct((T_ALL, D_INNER), BF16),
                 jax.ShapeDtypeStruct((T_ALL, SSD_GROUPS * LANES), F32)]
    if emit_state:
        out_specs.append(state_spec)
        out_shape.append(jax.ShapeDtypeStruct((n_seq, 1, 2, SSD_HEADS, SSD_N, SSD_P), F32))
    return pl.pallas_call(
        functools.partial(_ssd_kernel, n_chunks, has_s0, emit_state, bool(fill)),
        grid=(n_seq, SSD_GROUPS),
        in_specs=in_specs,
        out_specs=out_specs,
        out_shape=out_shape,
        input_output_aliases=aliases,
        scratch_shapes=[pltpu.VMEM((seq_len + 2 * HALO, GW + 2 * SSD_N), F32),
                        pltpu.VMEM((seq_len, GW), F32), pltpu.VMEM((seq_len, SSD_N), BF16),
                        pltpu.VMEM((seq_len, SSD_N), BF16), pltpu.VMEM((n_chunks, SSD_N, GW), BF16),
                        pltpu.VMEM((SSD_N, GW), F32), pltpu.VMEM((SSD_N, GW), F32)],
        compiler_params=_cparams("arbitrary", "arbitrary"),
        name="ssd_scan_latent" if has_s0 else "ssd_scan_prompt",
    )(*args)


def _router_kernel(x_ref, g_ref, mod_ref, rw_ref, hn_ref, idx_ref, wgt_ref, hf_ref):
    _adaln_to(x_ref, g_ref, mod_ref, 3, 4, hf_ref)
    hf = hf_ref[...]
    hn_ref[...] = hf.astype(BF16)
    logits = jnp.dot(hf, rw_ref[...], preferred_element_type=F32, precision=lax.Precision.HIGHEST)
    lane = lax.broadcasted_iota(jnp.int32, logits.shape, 1)
    lg = jnp.where(lane < N_EXPERTS, logits, NEG_INF)
    m1 = jnp.max(lg, axis=-1, keepdims=True)
    i1 = jnp.min(jnp.where(lg == m1, lane, LANES), axis=-1, keepdims=True)
    lg2 = jnp.where(lane == i1, NEG_INF, lg)
    m2 = jnp.max(lg2, axis=-1, keepdims=True)
    i2 = jnp.min(jnp.where(lg2 == m2, lane, LANES), axis=-1, keepdims=True)
    e2 = jnp.exp(m2 - m1)
    w1 = 1.0 / (1.0 + e2)
    idx_ref[...] = jnp.where(lane == 0, i1, jnp.where(lane == 1, i2, 0))
    wgt_ref[...] = jnp.where(lane == 0, w1, jnp.where(lane == 1, e2 * w1, 0.0))


def _router(x, gain, modt, router_w):
    t, d = x.shape
    rw = jnp.pad(router_w, ((0, 0), (0, LANES - N_EXPERTS)))
    return pl.pallas_call(
        _router_kernel,
        grid=(t // ROW_TILE,),
        in_specs=[pl.BlockSpec((ROW_TILE, d), lambda i: (i, 0)),
                  pl.BlockSpec((1, d), lambda i: (0, 0)),
                  pl.BlockSpec((None, 8, d), lambda i: (i, 0, 0)),
                  pl.BlockSpec((d, LANES), lambda i: (0, 0))],
        out_specs=[pl.BlockSpec((ROW_TILE, d), lambda i: (i, 0)),
                   pl.BlockSpec((ROW_TILE, LANES), lambda i: (i, 0)),
                   pl.BlockSpec((ROW_TILE, LANES), lambda i: (i, 0))],
        out_shape=[jax.ShapeDtypeStruct((t, d), BF16), jax.ShapeDtypeStruct((t, LANES), jnp.int32),
                   jax.ShapeDtypeStruct((t, LANES), F32)],
        scratch_shapes=[pltpu.VMEM((ROW_TILE, d), F32)],
        compiler_params=_cparams("arbitrary"),
        name="moe_router",
    )(x, gain.reshape(1, d), modt, rw)


DOWN_ROWS = 512
DOWN_TN = 512


def _expert_changed(be_ref, blk, prev_blk, step):
    return jnp.logical_or(step == 0, be_ref[blk] != be_ref[prev_blk])


def _expert_up_kernel(be_ref, nu_ref, xs_ref, wg_ref, wu_ref, h_ref, wgb_ref, wub_ref):
    i = pl.program_id(1)
    used = i < nu_ref[0]

    @pl.when(jnp.logical_and(used, _expert_changed(be_ref, i, jnp.maximum(i - 1, 0), i)))
    def _():
        wgb_ref[...] = wg_ref[...].astype(BF16)
        wub_ref[...] = wu_ref[...].astype(BF16)

    @pl.when(used)
    def _():
        xs = xs_ref[...]
        h_ref[...] = (_silu(_bdot(xs, wgb_ref[...])) * _bdot(xs, wub_ref[...])).astype(h_ref.dtype)

    @pl.when(jnp.logical_not(used))
    def _():
        h_ref[...] = jnp.zeros_like(h_ref)


def _expert_down_kernel(be_ref, nu_ref, h_ref, wd_ref, o_ref, wdb_ref):
    i = pl.program_id(1)
    per = MOE_ROWS // DOWN_ROWS
    blk = i // per
    used = blk < nu_ref[0]

    @pl.when(jnp.logical_and(used, _expert_changed(be_ref, blk, jnp.maximum(i - 1, 0) // per, i)))
    def _():
        wdb_ref[...] = wd_ref[...].astype(BF16)

    @pl.when(used)
    def _():
        o_ref[...] = _bdot(h_ref[...], wdb_ref[...]).astype(o_ref.dtype)

    @pl.when(jnp.logical_not(used))
    def _():
        o_ref[...] = jnp.zeros_like(o_ref)


def _experts(xs_sorted, block_e, n_used, wg, wu, wd):
    cap, d = xs_sorted.shape
    ff = wg.shape[2]
    per = MOE_ROWS // DOWN_ROWS

    def expert_of(blk, be, nu):
        return be[jnp.minimum(blk, jnp.maximum(nu[0] - 1, 0))]

    h = pl.pallas_call(
        _expert_up_kernel,
        grid_spec=pltpu.PrefetchScalarGridSpec(
            num_scalar_prefetch=2,
            grid=(ff // FFN_TF, cap // MOE_ROWS),
            in_specs=[pl.BlockSpec((MOE_ROWS, d), lambda f, i, be, nu: (i, 0)),
                      pl.BlockSpec((None, d, FFN_TF), lambda f, i, be, nu: (expert_of(i, be, nu), 0, f)),
                      pl.BlockSpec((None, d, FFN_TF), lambda f, i, be, nu: (expert_of(i, be, nu), 0, f))],
            out_specs=pl.BlockSpec((MOE_ROWS, FFN_TF), lambda f, i, be, nu: (i, f)),
            scratch_shapes=[pltpu.VMEM((d, FFN_TF), BF16), pltpu.VMEM((d, FFN_TF), BF16)],
        ),
        out_shape=jax.ShapeDtypeStruct((cap, ff), BF16),
        compiler_params=_cparams("arbitrary", "arbitrary"),
        name="moe_expert_up",
    )(block_e, n_used, xs_sorted, wg, wu)
    return pl.pallas_call(
        _expert_down_kernel,
        grid_spec=pltpu.PrefetchScalarGridSpec(
            num_scalar_prefetch=2,
            grid=(d // DOWN_TN, cap // DOWN_ROWS),
            in_specs=[pl.BlockSpec((DOWN_ROWS, ff), lambda n, i, be, nu: (i, 0)),
                      pl.BlockSpec((None, ff, DOWN_TN), lambda n, i, be, nu: (expert_of(i // per, be, nu), 0, n))],
            out_specs=pl.BlockSpec((DOWN_ROWS, DOWN_TN), lambda n, i, be, nu: (i, n)),
            scratch_shapes=[pltpu.VMEM((ff, DOWN_TN), BF16)],
        ),
        out_shape=jax.ShapeDtypeStruct((cap, d), BF16),
        compiler_params=_cparams("arbitrary", "arbitrary"),
        name="moe_expert_down",
    )(block_e, n_used, h, wd)


def _combine_kernel(x_ref, g0_ref, g1_ref, w_ref, mod_ref, o_ref):
    w = w_ref[...]
    y = w[:, 0:1] * g0_ref[...].astype(F32) + w[:, 1:2] * g1_ref[...].astype(F32)
    o_ref[...] = x_ref[...] + mod_ref[5:6, :] * y


COMB_ROWS = 512


def _combine(x, g0, g1, wgt, modt, row0, n_rows):
    d = x.shape[1]
    b0 = row0 // COMB_ROWS
    per = ROW_TILE // COMB_ROWS
    return pl.pallas_call(
        _combine_kernel,
        grid=(n_rows // COMB_ROWS,),
        in_specs=[pl.BlockSpec((COMB_ROWS, d), lambda i: (i + b0, 0)),
                  pl.BlockSpec((COMB_ROWS, d), lambda i: (i + b0, 0)),
                  pl.BlockSpec((COMB_ROWS, d), lambda i: (i + b0, 0)),
                  pl.BlockSpec((COMB_ROWS, LANES), lambda i: (i + b0, 0)),
                  pl.BlockSpec((None, 8, d), lambda i: ((i + b0) // per, 0, 0))],
        out_specs=pl.BlockSpec((COMB_ROWS, d), lambda i: (i, 0)),
        out_shape=jax.ShapeDtypeStruct((n_rows, d), F32),
        compiler_params=_cparams("arbitrary"),
        name="moe_combine",
    )(x, g0, g1, wgt, modt)


def _moe(x, gain, modt, router_w, wg, wu, wd):
    t, d = x.shape
    hn, idx, wgt = _router(x, gain, modt, router_w)
    top_idx = idx[:, :TOP_K]
    n_slots = t * TOP_K
    flat_e = top_idx.reshape(-1)
    onehot = (flat_e[:, None] == jnp.arange(N_EXPERTS, dtype=jnp.int32)[None, :]).astype(jnp.int32)
    incl = jnp.cumsum(onehot, axis=0)
    counts = incl[-1]
    rank = jnp.sum((incl - onehot) * onehot, axis=1)
    padded = (counts + MOE_ROWS - 1) // MOE_ROWS * MOE_ROWS
    pend = jnp.cumsum(padded)
    pstart = pend - padded
    dest = pstart[flat_e] + rank
    n_blocks = n_slots // MOE_ROWS + N_EXPERTS
    cap = n_blocks * MOE_ROWS
    row_tok = (jnp.arange(cap, dtype=jnp.int32) % t).at[dest].set(jnp.arange(n_slots, dtype=jnp.int32) // TOP_K)
    block_e = jnp.clip(jnp.searchsorted(pend, jnp.arange(n_blocks, dtype=jnp.int32) * MOE_ROWS, side='right'),
                       0, N_EXPERTS - 1).astype(jnp.int32)
    n_used = (pend[-1:] // MOE_ROWS).astype(jnp.int32)
    out = _experts(hn[row_tok], block_e, n_used, wg, wu, wd)
    dest = dest.reshape(t, TOP_K)
    g0, g1 = out[dest[:, 0]], out[dest[:, 1]]
    return (_combine(x, g0, g1, wgt, modt, 0, T_PROMPT), _combine(x, g0, g1, wgt, modt, T_PROMPT, T_SAMPLE))


def _rope_tables(n_tokens, dim):
    n_rows = n_tokens // GRID_W
    row = jnp.repeat(jnp.arange(n_rows), GRID_W).astype(F32)
    col = jnp.tile(jnp.arange(GRID_W), n_rows).astype(F32)
    n_freq = dim // 4
    inv = ROPE_BASE ** (-jnp.arange(n_freq, dtype=F32) / n_freq)
    ang = jnp.concatenate([row[:, None] * inv, col[:, None] * inv], axis=-1)
    return jnp.cos(ang), jnp.sin(ang)


def kernel(x_prompt, x_sample, state_ret, cache_k, cache_v, state_ssd, c, c_ctx, ada_w, ada_b, norm_mix, norm_ffn, ev_w_in, ev_w_out, ret_decay_logit, ret_norm, att_q_norm, att_k_norm, att_sink, ffn_w_gate, ffn_w_up, ffn_w_down, ssd_w_in, ssd_conv_w, ssd_conv_b, ssd_a_log, ssd_dt_bias, ssd_d, ssd_norm, ssd_w_out, moe_router, moe_w_gate, moe_w_up, moe_w_down):
    d = D_MODEL
    x = jnp.concatenate([x_prompt.reshape(T_PROMPT, d), x_sample.reshape(T_SAMPLE, d)], axis=0)

    cvecs = jnp.concatenate([c_ctx[None, :], c, jnp.zeros((MOD_ROWS - 1 - DEC_BATCH, d), F32)], axis=0)
    mods = _modulation(cvecs, ada_w, ada_b).reshape(2, MOD_ROWS, 6, d)
    tiles_per_seq = DEC_SEQ // ROW_TILE
    tile_row = jnp.concatenate([jnp.zeros((T_PROMPT // ROW_TILE,), jnp.int32),
                                1 + jnp.arange(T_SAMPLE // ROW_TILE, dtype=jnp.int32) // tiles_per_seq])
    modt = jnp.pad(mods[:, tile_row], ((0, 0), (0, 0), (0, 2), (0, 0)))

    proj = _adaln_matmul(x, norm_mix[0], modt[0], 0, 1, ev_w_in[0], tn=512, out_dtype=BF16, name="even_in_proj")
    lg = jax.nn.log_sigmoid(ret_decay_logit[0].astype(F32))
    cos_r, sin_r = _rope_tables(DEC_SEQ, RET_DK)
    cos_a, sin_a = _rope_tables(DEC_SEQ, ATT_HD)
    cos_a2 = jnp.concatenate([cos_a, cos_a], axis=-1)
    sin_a2 = jnp.concatenate([-sin_a, sin_a], axis=-1)
    mix_ret, new_state_ret = _retention(proj, lg, ret_norm[0], SEQ, BATCH, 0, emit_state=True)
    mix_ret, = _retention(proj, lg, ret_norm[0], DEC_SEQ, DEC_BATCH, T_PROMPT // DEC_SEQ,
                          ropes=(cos_r, sin_r), s0=state_ret, fill=(mix_ret,))
    mix_att, new_k, new_v = _attention_prompt(proj, att_sink[0], att_q_norm[0], att_k_norm[0])
    mix_att = _attention_latent(proj, cache_k[:, 0], cache_v[:, 0], att_sink[0], cos_a2, sin_a2,
                                att_q_norm[0], att_k_norm[0], mix_att)
    x = _proj_residual([mix_ret, mix_att], ev_w_out[0], x, modt[0], 2, tn=1024, name="even_out_proj")
    h = _ffn_gateup(x, norm_ffn[0], modt[0], ffn_w_gate[0], ffn_w_up[0])
    x = _proj_residual([h], ffn_w_down[0], x, modt[0], 5, tn=256, name="ffn_down")

    zx = _adaln_matmul(x, norm_mix[1], modt[1], 0, 1, ssd_w_in[0], tn=1024, n_out=SSD_ZX, out_dtype=BF16,
                       name="ssd_in_proj")
    dt_raw = _adaln_matmul(x, norm_mix[1], modt[1], 0, 1, ssd_w_in[0], tn=2 * SSD_HEADS, n_out=2 * SSD_HEADS,
                           col_block0=SSD_ZX // (2 * SSD_HEADS), name="ssd_dt_proj")
    prep = _ssd_prep(dt_raw, ssd_dt_bias[0], ssd_a_log[0])
    d_exp = jnp.repeat(ssd_d[0], SSD_P)[None, :]
    conv_b = ssd_conv_b[0][None, :]
    yg, yss, new_state_ssd = _ssd_scan(zx, prep, ssd_conv_w[0], conv_b, d_exp, ssd_norm[0], SEQ, BATCH, 0,
                                       emit_state=True)
    yg, yss = _ssd_scan(zx, prep, ssd_conv_w[0], conv_b, d_exp, ssd_norm[0], DEC_SEQ, DEC_BATCH,
                        T_PROMPT // DEC_SEQ, s0=state_ssd, fill=(yg, yss))
    x = _proj_residual([yg], ssd_w_out[0], x, modt[1], 2, tn=512, row_ss=yss, name="ssd_out_proj")
    y_p, y_s = _moe(x, norm_ffn[1], modt[1], moe_router[0], moe_w_gate[0], moe_w_up[0], moe_w_down[0])

    y_prompt = y_p.reshape(BATCH, SEQ, d)
    y_sample = y_s.reshape(DEC_BATCH, DEC_SEQ, d)
    new_cache_k = new_k.reshape(BATCH, 1, SEQ, ATT_KV_HEADS, ATT_HD)
    new_cache_v = new_v.reshape(BATCH, 1, SEQ, ATT_KV_HEADS, ATT_HD)
    return (y_prompt, y_sample, new_state_ret, new_cache_k, new_cache_v, new_state_ssd)
```

```python
import functools

import jax
import jax.numpy as jnp
from jax import lax
from jax.experimental import pallas as pl
from jax.experimental.pallas import tpu as pltpu

F32 = jnp.float32
BF16 = jnp.bfloat16

D_MODEL = 2048
BATCH = 16
SEQ = 256
DEC_BATCH = 8
DEC_SEQ = 2048
PAST_LEN = 512
GRID_W = 64
BLOCK = 128
WINDOW = 128
EPS = 1e-6
ROPE_BASE = 10000.0
RET_HEADS = 4
RET_DK = 256
RET_DV = 256
ATT_HEADS = 8
ATT_KV_HEADS = 2
ATT_HD = 128
ATT_GROUP = ATT_HEADS // ATT_KV_HEADS
EVEN_IN = 5632
D_INNER = 2 * D_MODEL
SSD_P = 64
SSD_HEADS = D_INNER // SSD_P
SSD_N = 128
SSD_GROUPS = 8
SSD_R = SSD_HEADS // SSD_GROUPS
CONV_W = 5
CONV_CH = D_INNER + 2 * SSD_GROUPS * SSD_N
SSD_ZX = D_INNER + CONV_CH
D_FF = 5632
N_EXPERTS = 8
TOP_K = 2

T_PROMPT = BATCH * SEQ
T_SAMPLE = DEC_BATCH * DEC_SEQ
T_ALL = T_PROMPT + T_SAMPLE

LANES = 128
ROW_TILE = 1024
VMEM_LIMIT = 56 * 1024 * 1024
N_ROW_TILES = T_ALL // ROW_TILE
MOE_ROWS = 1024
NEG_INF = float("-inf")


def _cparams(*sem):
    return pltpu.CompilerParams(dimension_semantics=sem, vmem_limit_bytes=VMEM_LIMIT)


def _silu(x):
    return x * jax.nn.sigmoid(x)


def _bdot(a, b):
    return jnp.dot(a.astype(BF16), b.astype(BF16), preferred_element_type=F32)


def _bdot_nt(a, b):
    return lax.dot_general(a.astype(BF16), b.astype(BF16), (((1,), (1,)), ((), ())),
                           preferred_element_type=F32)


def _bdot_tn(a, b):
    return lax.dot_general(a.astype(BF16), b.astype(BF16), (((0,), (0,)), ((), ())),
                           preferred_element_type=F32)


def _rows(c, n=BLOCK):
    return pl.ds(pl.multiple_of(c * n, n), n)


MOD_ROWS = 16
MOD_TN = 1024


def _mod_kernel(c_ref, w_ref, b_ref, o_ref):
    o_ref[...] = _bdot(_silu(c_ref[...]), w_ref[...]) + b_ref[...]


def _modulation(cvecs, ada_w, ada_b):
    depth, d, n = ada_w.shape
    return pl.pallas_call(
        _mod_kernel,
        grid=(depth, n // MOD_TN),
        in_specs=[pl.BlockSpec((MOD_ROWS, d), lambda l, j: (0, 0)),
                  pl.BlockSpec((None, d, MOD_TN), lambda l, j: (l, 0, j)),
                  pl.BlockSpec((None, 1, MOD_TN), lambda l, j: (l, 0, j))],
        out_specs=pl.BlockSpec((None, MOD_ROWS, MOD_TN), lambda l, j: (l, 0, j)),
        out_shape=jax.ShapeDtypeStruct((depth, MOD_ROWS, n), F32),
        compiler_params=_cparams("arbitrary", "arbitrary"),
        name="modulation",
    )(cvecs, ada_w, ada_b.reshape(depth, 1, n))


ADALN_CHUNK = 64


def _adaln_to(x_ref, g_ref, mod_ref, shift_row, scale_row, hn_ref):
    mult = g_ref[...] * (1.0 + mod_ref[scale_row:scale_row + 1, :])
    shift = mod_ref[shift_row:shift_row + 1, :]

    def body(i, carry):
        r = _rows(i, ADALN_CHUNK)
        x = x_ref[r, :]
        ms = jnp.mean(x * x, axis=-1, keepdims=True)
        hn_ref[r, :] = (x * lax.rsqrt(ms + EPS) * mult + shift).astype(hn_ref.dtype)
        return carry

    lax.fori_loop(0, x_ref.shape[0] // ADALN_CHUNK, body, 0)


def _adaln_mm_kernel(shift_row, scale_row, x_ref, g_ref, mod_ref, w_ref, o_ref, hn_ref):
    @pl.when(pl.program_id(1) == 0)
    def _():
        _adaln_to(x_ref, g_ref, mod_ref, shift_row, scale_row, hn_ref)

    o_ref[...] = _bdot(hn_ref[...], w_ref[...]).astype(o_ref.dtype)


def _adaln_matmul(x, gain, modt, shift_row, scale_row, w, tn, n_out=None, col_block0=0, out_dtype=F32,
                  name="adaln_mm"):
    t, d = x.shape
    n_out = w.shape[1] if n_out is None else n_out
    return pl.pallas_call(
        functools.partial(_adaln_mm_kernel, shift_row, scale_row),
        grid=(t // ROW_TILE, n_out // tn),
        in_specs=[pl.BlockSpec((ROW_TILE, d), lambda i, j: (i, 0)),
                  pl.BlockSpec((1, d), lambda i, j: (0, 0)),
                  pl.BlockSpec((None, 8, d), lambda i, j: (i, 0, 0)),
                  pl.BlockSpec((d, tn), lambda i, j: (0, j + col_block0))],
        out_specs=pl.BlockSpec((ROW_TILE, tn), lambda i, j: (i, j)),
        out_shape=jax.ShapeDtypeStruct((t, n_out), out_dtype),
        scratch_shapes=[pltpu.VMEM((ROW_TILE, d), BF16)],
        compiler_params=_cparams("arbitrary", "arbitrary"),
        name=name,
    )(x, gain.reshape(1, d), modt, w)


FFN_TF = 512


def _gateup_kernel(x_ref, g_ref, mod_ref, wg_ref, wu_ref, h_ref, hn_ref):
    @pl.when(pl.program_id(1) == 0)
    def _():
        _adaln_to(x_ref, g_ref, mod_ref, 3, 4, hn_ref)

    hn = hn_ref[...]
    h_ref[...] = (_silu(_bdot(hn, wg_ref[...])) * _bdot(hn, wu_ref[...])).astype(h_ref.dtype)


def _ffn_gateup(x, gain, modt, wg, wu):
    t, d = x.shape
    ff = wg.shape[1]
    return pl.pallas_call(
        _gateup_kernel,
        grid=(t // ROW_TILE, ff // FFN_TF),
        in_specs=[pl.BlockSpec((ROW_TILE, d), lambda i, f: (i, 0)),
                  pl.BlockSpec((1, d), lambda i, f: (0, 0)),
                  pl.BlockSpec((None, 8, d), lambda i, f: (i, 0, 0)),
                  pl.BlockSpec((d, FFN_TF), lambda i, f: (0, f)),
                  pl.BlockSpec((d, FFN_TF), lambda i, f: (0, f))],
        out_specs=pl.BlockSpec((ROW_TILE, FFN_TF), lambda i, f: (i, f)),
        out_shape=jax.ShapeDtypeStruct((t, ff), BF16),
        scratch_shapes=[pltpu.VMEM((ROW_TILE, d), BF16)],
        compiler_params=_cparams("arbitrary", "arbitrary"),
        name="ffn_gateup",
    )(x, gain.reshape(1, d), modt, wg, wu)


def _proj_res_kernel(n_a, gate_row, norm, *refs):
    a_refs = refs[:n_a]
    w_refs = refs[n_a:2 * n_a]
    x_ref, mod_ref = refs[2 * n_a], refs[2 * n_a + 1]
    pos = 2 * n_a + 2
    o_ref = refs[pos + 1] if norm else refs[pos]
    acc = _bdot(a_refs[0][...], w_refs[0][...])
    for k in range(1, n_a):
        acc = acc + _bdot(a_refs[k][...], w_refs[k][...])
    if norm:
        ss_ref = refs[pos]
        k_total = sum(a.shape[1] for a in a_refs)
        ss = ss_ref[:, 0:LANES]
        for k in range(1, ss_ref.shape[1] // LANES):
            ss = ss + ss_ref[:, k * LANES:(k + 1) * LANES]
        rs = lax.rsqrt(ss * (1.0 / k_total) + EPS)
        acc = acc * jnp.concatenate([rs] * (acc.shape[1] // LANES), axis=1)
    o_ref[...] = x_ref[...] + mod_ref[gate_row:gate_row + 1, :] * acc


def _proj_residual(a_list, w, x, modt, gate_row, tn, row_ss=None, name="proj_res"):
    t, d = x.shape
    n_a = len(a_list)
    norm = row_ss is not None
    in_specs, args, k0 = [], [], 0
    for a in a_list:
        in_specs.append(pl.BlockSpec((ROW_TILE, a.shape[1]), lambda i, j: (i, 0)))
        args.append(a)
    for a in a_list:
        ka = a.shape[1]
        assert k0 % ka == 0
        in_specs.append(pl.BlockSpec((ka, tn), lambda i, j, kb=k0 // ka: (kb, j)))
        args.append(w)
        k0 += ka
    in_specs += [pl.BlockSpec((ROW_TILE, tn), lambda i, j: (i, j)),
                 pl.BlockSpec((None, 8, tn), lambda i, j: (i, 0, j))]
    args += [x, modt]
    if norm:
        in_specs.append(pl.BlockSpec((ROW_TILE, row_ss.shape[1]), lambda i, j: (i, 0)))
        args.append(row_ss)
    return pl.pallas_call(
        functools.partial(_proj_res_kernel, n_a, gate_row, norm),
        grid=(t // ROW_TILE, d // tn),
        in_specs=in_specs,
        out_specs=pl.BlockSpec((ROW_TILE, tn), lambda i, j: (i, j)),
        out_shape=jax.ShapeDtypeStruct((t, d), F32),
        compiler_params=_cparams("arbitrary", "arbitrary"),
        name=name,
    )(*args)


def _ret_kernel(n_chunks, rope, has_s0, emit_state, has_fill, lg_ref, *refs):
    it = iter(refs)
    q_ref, k_ref, v_ref, gt_ref = next(it), next(it), next(it), next(it)
    cos_ref, sin_ref = (next(it), next(it)) if rope else (None, None)
    s0_ref = next(it) if has_s0 else None
    gain_ref = next(it)
    if has_fill:
        next(it)
    o_ref = next(it)
    sfin_ref = next(it) if emit_state else None
    qs_ref, ks_ref, sfs_ref, dm_ref, dec_ref, sf_ref, sb_ref = (next(it) for _ in range(7))

    h = pl.program_id(1)
    lgf = lg_ref[0, h]
    lgb = lg_ref[1, h]
    half = RET_DK // 2

    def prep(c, carry):
        r = _rows(c)
        q = q_ref[r, :].astype(F32)
        k = k_ref[r, :].astype(F32) * (RET_DK ** -0.5)
        if rope:
            cs, sn = cos_ref[r, :], sin_ref[r, :]
            for src, dst in ((q, qs_ref), (k, ks_ref)):
                x1, x2 = src[:, :half], src[:, half:]
                dst[r, :half] = (x1 * cs - x2 * sn).astype(BF16)
                dst[r, half:] = (x1 * sn + x2 * cs).astype(BF16)
        else:
            qs_ref[r, :] = q.astype(BF16)
            ks_ref[r, :] = k.astype(BF16)
        return carry

    lax.fori_loop(0, n_chunks, prep, 0)

    ii = lax.broadcasted_iota(jnp.int32, (BLOCK, BLOCK), 0)
    jj = lax.broadcasted_iota(jnp.int32, (BLOCK, BLOCK), 1)
    diff = (ii - jj).astype(F32)
    dm_ref[...] = jnp.exp(jnp.where(jj <= ii, diff * lgf, -diff * lgb))
    pos = lax.broadcasted_iota(jnp.int32, (BLOCK, RET_DV), 0).astype(F32)
    dec_ref[0] = jnp.exp((pos + 1.0) * lgf)
    dec_ref[1] = jnp.exp((BLOCK - pos) * lgb)
    dec_ref[2] = jnp.exp((BLOCK - 1.0 - pos) * lgf)
    dec_ref[3] = jnp.exp(pos * lgb)
    tot_f = jnp.exp(jnp.full((1, RET_DV), BLOCK * lgf, F32))
    tot_b = jnp.exp(jnp.full((1, RET_DV), BLOCK * lgb, F32))

    if has_s0:
        sf_ref[...] = s0_ref[0]
        sb_ref[...] = s0_ref[1]
    else:
        sf_ref[...] = jnp.zeros_like(sf_ref)
        sb_ref[...] = jnp.zeros_like(sb_ref)

    def fwd(c, carry):
        r = _rows(c)
        sfs_ref[c] = sf_ref[...].astype(BF16)
        kd = ks_ref[r, :].astype(F32) * dec_ref[2]
        sf_ref[...] = sf_ref[...] * tot_f + _bdot_tn(kd, v_ref[r, :])
        return carry

    lax.fori_loop(0, n_chunks, fwd, 0)
    if emit_state:
        sfin_ref[0] = sf_ref[...]

    def bwd(t, carry):
        c = n_chunks - 1 - t
        r = _rows(c)
        q = qs_ref[r, :]
        k = ks_ref[r, :]
        v = v_ref[r, :].astype(BF16)
        p = _bdot_nt(q, k) * dm_ref[...]
        o = _bdot(p, v)
        o = o + _bdot(q, sfs_ref[c]) * dec_ref[0]
        o = o + _bdot(q, sb_ref[...]) * dec_ref[1]
        ms = jnp.mean(o * o, axis=-1, keepdims=True)
        y = o * lax.rsqrt(ms + EPS) * gain_ref[...]
        o_ref[r, :] = (y * _silu(gt_ref[r, :].astype(F32))).astype(o_ref.dtype)
        kd = k.astype(F32) * dec_ref[3]
        sb_ref[...] = sb_ref[...] * tot_b + _bdot_tn(kd, v)
        return carry

    lax.fori_loop(0, n_chunks, bwd, 0)
    if emit_state:
        sfin_ref[1] = sb_ref[...]


def _fill_alias(fill, in_specs, args):
    aliases = {}
    for k, arr in enumerate(fill or ()):
        in_specs.append(pl.BlockSpec(memory_space=pl.ANY))
        args.append(arr)
        aliases[len(args) - 1] = k
    return aliases


def _retention(proj, lg, ret_norm, seq_len, n_seq, row_block0, ropes=None, s0=None, emit_state=False, fill=None):
    n_chunks = seq_len // BLOCK
    rope, has_s0 = ropes is not None, s0 is not None

    def col(cb):
        return pl.BlockSpec((seq_len, RET_DK), lambda b, h, cb=cb: (b + row_block0, cb * RET_HEADS + h))

    in_specs = [pl.BlockSpec(memory_space=pltpu.SMEM), col(0), col(1), col(2), col(3)]
    args = [lg, proj, proj, proj, proj]
    if rope:
        in_specs += [pl.BlockSpec((seq_len, RET_DK // 2), lambda b, h: (0, 0))] * 2
        args += list(ropes)
    if has_s0:
        in_specs.append(pl.BlockSpec((None, None, 2, None, RET_DK, RET_DV), lambda b, h: (b, 0, 0, h, 0, 0)))
        args.append(s0)
    in_specs.append(pl.BlockSpec((1, RET_DV), lambda b, h: (0, h)))
    args.append(ret_norm.reshape(1, RET_HEADS * RET_DV))
    aliases = _fill_alias(fill, in_specs, args)
    out_specs = [pl.BlockSpec((seq_len, RET_DV), lambda b, h: (b + row_block0, h))]
    out_shape = [jax.ShapeDtypeStruct((T_ALL, RET_HEADS * RET_DV), BF16)]
    if emit_state:
        out_specs.append(pl.BlockSpec((None, None, 2, None, RET_DK, RET_DV), lambda b, h: (b, 0, 0, h, 0, 0)))
        out_shape.append(jax.ShapeDtypeStruct((n_seq, 1, 2, RET_HEADS, RET_DK, RET_DV), F32))
    return pl.pallas_call(
        functools.partial(_ret_kernel, n_chunks, rope, has_s0, emit_state, bool(fill)),
        grid=(n_seq, RET_HEADS),
        in_specs=in_specs,
        out_specs=out_specs,
        out_shape=out_shape,
        input_output_aliases=aliases,
        scratch_shapes=[pltpu.VMEM((seq_len, RET_DK), BF16), pltpu.VMEM((seq_len, RET_DK), BF16),
                        pltpu.VMEM((n_chunks, RET_DK, RET_DV), BF16), pltpu.VMEM((BLOCK, BLOCK), F32),
                        pltpu.VMEM((4, BLOCK, RET_DV), F32), pltpu.VMEM((RET_DK, RET_DV), F32),
                        pltpu.VMEM((RET_DK, RET_DV), F32)],
        compiler_params=_cparams("arbitrary", "arbitrary"),
        name="retention_latent" if rope else "retention_prompt",
    )(*args)


def _head_norm(x, gain):
    x = x.astype(F32)
    return x * lax.rsqrt(jnp.mean(x * x, axis=-1, keepdims=True) + EPS) * gain


def _rope_full(x, cs, sn):
    return x * cs + pltpu.roll(x, ATT_HD // 2, 1) * sn


def _sink_column(sink_ref, kh, rows_per_head):
    n = ATT_GROUP * rows_per_head
    head = lax.broadcasted_iota(jnp.int32, (n, 1), 0) // rows_per_head
    col = jnp.full((n, 1), sink_ref[kh * ATT_GROUP], F32)
    for g in range(1, ATT_GROUP):
        col = jnp.where(head == g, sink_ref[kh * ATT_GROUP + g], col)
    return col


def _att_latent_kernel(sink_ref, q_ref, k_ref, v_ref, ck_ref, cv_ref, cos_ref, sin_ref, qn_ref, kn_ref,
                       fill_ref, o_ref, kp_ref, vp_ref, ckp_ref, cvp_ref):
    del fill_ref
    kh, qb = pl.program_id(1), pl.program_id(2)
    n_chunks = DEC_SEQ // BLOCK
    loc = 3 * BLOCK

    @pl.when(qb == 0)
    def _():
        def prep(c, carry):
            r = _rows(c)
            kp_ref[r, :] = _rope_full(_head_norm(k_ref[r, :], kn_ref[...]), cos_ref[r, :], sin_ref[r, :]).astype(BF16)
            vp_ref[r, :] = v_ref[r, :].astype(BF16)
            return carry

        lax.fori_loop(0, n_chunks, prep, 0)
        ckp_ref[...] = ck_ref[...].astype(BF16)
        cvp_ref[...] = cv_ref[...].astype(BF16)

    rq = _rows(qb)
    cs, sn = cos_ref[rq, :], sin_ref[rq, :]
    qg = qn_ref[...] * (ATT_HD ** -0.5)
    q = jnp.concatenate(
        [_rope_full(_head_norm(q_ref[:, g * ATT_HD:(g + 1) * ATT_HD], qg), cs, sn).astype(BF16)
         for g in range(ATT_GROUP)], axis=0)
    start = pl.multiple_of(jnp.clip((qb - 1) * BLOCK, 0, DEC_SEQ - loc), BLOCK)
    s_loc = _bdot_nt(q, kp_ref[pl.ds(start, loc), :])
    qpos = qb * BLOCK + (lax.broadcasted_iota(jnp.int32, (ATT_GROUP * BLOCK, loc), 0) & (BLOCK - 1))
    kpos = start + lax.broadcasted_iota(jnp.int32, (ATT_GROUP * BLOCK, loc), 1)
    s_loc = jnp.where(jnp.abs(qpos - kpos) <= WINDOW, s_loc, NEG_INF)
    s_ctx = _bdot_nt(q, ckp_ref[...])
    sink = _sink_column(sink_ref, kh, BLOCK)
    m = jnp.maximum(jnp.maximum(jnp.max(s_loc, axis=-1, keepdims=True), jnp.max(s_ctx, axis=-1, keepdims=True)),
                    sink)
    p_loc = jnp.exp(s_loc - m)
    p_ctx = jnp.exp(s_ctx - m)
    den = jnp.sum(p_loc, axis=-1, keepdims=True) + jnp.sum(p_ctx, axis=-1, keepdims=True) + jnp.exp(sink - m)
    o = (_bdot(p_loc, vp_ref[pl.ds(start, loc), :]) + _bdot(p_ctx, cvp_ref[...])) / den
    for g in range(ATT_GROUP):
        o_ref[:, g * ATT_HD:(g + 1) * ATT_HD] = o[g * BLOCK:(g + 1) * BLOCK, :].astype(o_ref.dtype)


def _attention_latent(proj, cache_k, cache_v, sink, cos_a, sin_a, q_norm, k_norm, fill):
    nqb = DEC_SEQ // BLOCK
    rb0 = T_PROMPT // DEC_SEQ
    qcol0 = 4 * RET_HEADS * RET_DK // (ATT_GROUP * ATT_HD)
    kcol0 = (4 * RET_HEADS * RET_DK + ATT_HEADS * ATT_HD) // ATT_HD
    vcol0 = kcol0 + ATT_KV_HEADS
    ck = cache_k.reshape(DEC_BATCH, PAST_LEN, ATT_KV_HEADS * ATT_HD)
    cv = cache_v.reshape(DEC_BATCH, PAST_LEN, ATT_KV_HEADS * ATT_HD)
    return pl.pallas_call(
        _att_latent_kernel,
        grid=(DEC_BATCH, ATT_KV_HEADS, nqb),
        in_specs=[pl.BlockSpec(memory_space=pltpu.SMEM),
                  pl.BlockSpec((BLOCK, ATT_GROUP * ATT_HD),
                               lambda b, kh, qb: (T_PROMPT // BLOCK + b * nqb + qb, qcol0 + kh)),
                  pl.BlockSpec((DEC_SEQ, ATT_HD), lambda b, kh, qb: (rb0 + b, kcol0 + kh)),
                  pl.BlockSpec((DEC_SEQ, ATT_HD), lambda b, kh, qb: (rb0 + b, vcol0 + kh)),
                  pl.BlockSpec((None, PAST_LEN, ATT_HD), lambda b, kh, qb: (b, 0, kh)),
                  pl.BlockSpec((None, PAST_LEN, ATT_HD), lambda b, kh, qb: (b, 0, kh)),
                  pl.BlockSpec((DEC_SEQ, ATT_HD), lambda b, kh, qb: (0, 0)),
                  pl.BlockSpec((DEC_SEQ, ATT_HD), lambda b, kh, qb: (0, 0)),
                  pl.BlockSpec((1, ATT_HD), lambda b, kh, qb: (0, 0)),
                  pl.BlockSpec((1, ATT_HD), lambda b, kh, qb: (0, 0)),
                  pl.BlockSpec(memory_space=pl.ANY)],
        out_specs=pl.BlockSpec((BLOCK, ATT_GROUP * ATT_HD),
                               lambda b, kh, qb: (T_PROMPT // BLOCK + b * nqb + qb, kh)),
        out_shape=jax.ShapeDtypeStruct((T_ALL, ATT_HEADS * ATT_HD), BF16),
        input_output_aliases={10: 0},
        scratch_shapes=[pltpu.VMEM((DEC_SEQ, ATT_HD), BF16), pltpu.VMEM((DEC_SEQ, ATT_HD), BF16),
                        pltpu.VMEM((PAST_LEN, ATT_HD), BF16), pltpu.VMEM((PAST_LEN, ATT_HD), BF16)],
        compiler_params=_cparams("arbitrary", "arbitrary", "arbitrary"),
        name="attention_latent",
    )(sink, proj, proj, proj, ck, cv, cos_a, sin_a, q_norm.reshape(1, ATT_HD), k_norm.reshape(1, ATT_HD), fill)


def _att_prompt_kernel(sink_ref, q_ref, k_ref, v_ref, qn_ref, kn_ref, o_ref, nk_ref, nv_ref):
    kh = pl.program_id(1)
    kn = _head_norm(k_ref[...], kn_ref[...])
    v = v_ref[...]
    nk_ref[...] = kn
    nv_ref[...] = v.astype(F32)
    qg = qn_ref[...] * (ATT_HD ** -0.5)
    q = jnp.concatenate([_head_norm(q_ref[:, g * ATT_HD:(g + 1) * ATT_HD], qg).astype(BF16)
                         for g in range(ATT_GROUP)], axis=0)
    s = _bdot_nt(q, kn)
    sink = _sink_column(sink_ref, kh, SEQ)
    m = jnp.maximum(jnp.max(s, axis=-1, keepdims=True), sink)
    p = jnp.exp(s - m)
    den = jnp.sum(p, axis=-1, keepdims=True) + jnp.exp(sink - m)
    o = _bdot(p, v) / den
    for g in range(ATT_GROUP):
        o_ref[:, g * ATT_HD:(g + 1) * ATT_HD] = o[g * SEQ:(g + 1) * SEQ, :].astype(o_ref.dtype)


def _attention_prompt(proj, sink, q_norm, k_norm):
    qcol0 = 4 * RET_HEADS * RET_DK // (ATT_GROUP * ATT_HD)
    kcol0 = (4 * RET_HEADS * RET_DK + ATT_HEADS * ATT_HD) // ATT_HD
    vcol0 = kcol0 + ATT_KV_HEADS
    kv_spec = pl.BlockSpec((None, SEQ, ATT_HD), lambda b, kh: (b, 0, kh))
    kv_shape = jax.ShapeDtypeStruct((BATCH, SEQ, ATT_KV_HEADS * ATT_HD), F32)
    return pl.pallas_call(
        _att_prompt_kernel,
        grid=(BATCH, ATT_KV_HEADS),
        in_specs=[pl.BlockSpec(memory_space=pltpu.SMEM),
                  pl.BlockSpec((SEQ, ATT_GROUP * ATT_HD), lambda b, kh: (b, qcol0 + kh)),
                  pl.BlockSpec((SEQ, ATT_HD), lambda b, kh: (b, kcol0 + kh)),
                  pl.BlockSpec((SEQ, ATT_HD), lambda b, kh: (b, vcol0 + kh)),
                  pl.BlockSpec((1, ATT_HD), lambda b, kh: (0, 0)),
                  pl.BlockSpec((1, ATT_HD), lambda b, kh: (0, 0))],
        out_specs=[pl.BlockSpec((SEQ, ATT_GROUP * ATT_HD), lambda b, kh: (b, kh)), kv_spec, kv_spec],
        out_shape=[jax.ShapeDtypeStruct((T_ALL, ATT_HEADS * ATT_HD), BF16), kv_shape, kv_shape],
        compiler_params=_cparams("arbitrary", "arbitrary"),
        name="attention_prompt",
    )(sink, proj, proj, proj, q_norm.reshape(1, ATT_HD), k_norm.reshape(1, ATT_HD))


def _split_dot(m01, a):
    hi = a.astype(BF16)
    r1 = a - hi.astype(F32)
    mid = r1.astype(BF16)
    lo = (r1 - mid.astype(F32)).astype(BF16)
    return (jnp.dot(m01, hi, preferred_element_type=F32) + jnp.dot(m01, mid, preferred_element_type=F32)
            + jnp.dot(m01, lo, preferred_element_type=F32))


def _ssd_prep_kernel(raw_ref, bias_ref, alog_ref, cum_ref, dt_ref, w_ref, tot_ref, ecum_ref):
    x = raw_ref[...] + bias_ref[...]
    dt = jnp.maximum(x, 0.0) + jnp.log1p(jnp.exp(-jnp.abs(x)))
    a = dt * (-jnp.exp(alog_ref[...]))
    ii = lax.broadcasted_iota(jnp.int32, (BLOCK, BLOCK), 0)
    jj = lax.broadcasted_iota(jnp.int32, (BLOCK, BLOCK), 1)
    incl = _split_dot(jnp.where(jj <= ii, 1.0, 0.0).astype(BF16), a)
    rincl = _split_dot(jnp.where(jj >= ii, 1.0, 0.0).astype(BF16), a)
    fwd_lane = lax.broadcasted_iota(jnp.int32, (BLOCK, LANES), 1) < SSD_HEADS
    cum = jnp.where(fwd_lane, incl, rincl)
    tot = jnp.where(fwd_lane[:1], incl[BLOCK - 1:BLOCK, :], rincl[0:1, :])
    w = dt * jnp.exp(tot - cum)
    cum_ref[...] = cum.T
    dt_ref[...] = dt.T
    w_ref[...] = w.T
    tot_ref[...] = jnp.broadcast_to(jnp.exp(tot), (BLOCK, LANES)).T
    ecum_ref[...] = jnp.exp(cum).T


def _ssd_prep(dt_raw, dt_bias, a_log):
    nc = T_ALL // BLOCK
    spec = pl.BlockSpec((None, 2 * SSD_HEADS, BLOCK), lambda c: (c, 0, 0))
    shape = jax.ShapeDtypeStruct((nc, 2 * SSD_HEADS, BLOCK), F32)
    return pl.pallas_call(
        _ssd_prep_kernel,
        grid=(nc,),
        in_specs=[pl.BlockSpec((BLOCK, 2 * SSD_HEADS), lambda c: (c, 0)),
                  pl.BlockSpec((1, 2 * SSD_HEADS), lambda c: (0, 0)),
                  pl.BlockSpec((1, 2 * SSD_HEADS), lambda c: (0, 0))],
        out_specs=[spec] * 5,
        out_shape=[shape] * 5,
        compiler_params=_cparams("arbitrary"),
        name="ssd_prep",
    )(dt_raw, dt_bias.reshape(1, 2 * SSD_HEADS), a_log.reshape(1, 2 * SSD_HEADS))


GW = SSD_R * SSD_P
HALO = 8


def _pair_tiles(per_head):
    low = lax.broadcasted_iota(jnp.int32, per_head[0].shape, 1) < SSD_P
    return jnp.concatenate([jnp.where(low, per_head[2 * t], per_head[2 * t + 1]) for t in range(SSD_R // 2)],
                           axis=1)


def _row_bcast(ref, c, r):
    return jnp.broadcast_to(ref[c, r:r + 1, :], (BLOCK, BLOCK))


def _pair_cols(ref, c):
    top = lax.broadcasted_iota(jnp.int32, (BLOCK, BLOCK), 0) < SSD_P
    return jnp.concatenate(
        [jnp.where(top, _row_bcast(ref, c, 2 * t), _row_bcast(ref, c, 2 * t + 1)).T for t in range(SSD_R // 2)],
        axis=1)


def _ssd_kernel(n_chunks, has_s0, emit_state, has_fill, *refs):
    it = iter(refs)
    z_ref, x_ref, b_ref, c_ref = (next(it) for _ in range(4))
    cumf_ref, cumb_ref, dtf_ref, dtb_ref, wf_ref, wb_ref, totf_ref, totb_ref, ecf_ref, ecb_ref = (
        next(it) for _ in range(10))
    cwx_ref, cwb_ref, cwc_ref, cbx_ref, cbb_ref, cbc_ref, d_ref, ng_ref = (next(it) for _ in range(8))
    s0_ref = next(it) if has_s0 else None
    if has_fill:
        next(it), next(it)
    o_ref, ss_ref = next(it), next(it)
    sfin_ref = next(it) if emit_state else None
    pad_ref, xc_ref, bc_ref, cc_ref, sfs_ref, sf_ref, sb_ref = (next(it) for _ in range(7))
    seq_len = n_chunks * BLOCK

    pad_ref[0:HALO, :] = jnp.zeros((HALO, GW + 2 * SSD_N), F32)
    pad_ref[HALO + seq_len:2 * HALO + seq_len, :] = jnp.zeros((HALO, GW + 2 * SSD_N), F32)

    def fill(c, carry):
        dst = pl.ds(pl.multiple_of(c * BLOCK, BLOCK) + HALO, BLOCK)
        r = _rows(c)
        pad_ref[dst, 0:GW] = x_ref[r, :].astype(F32)
        pad_ref[dst, GW:GW + SSD_N] = b_ref[r, :].astype(F32)
        pad_ref[dst, GW + SSD_N:GW + 2 * SSD_N] = c_ref[r, :].astype(F32)
        return carry

    lax.fori_loop(0, n_chunks, fill, 0)

    def conv(c):
        r = _rows(c)
        src = pl.ds(pl.multiple_of(c * BLOCK, BLOCK), BLOCK + 2 * HALO)
        for col0, width, cw_ref, cb_ref, dst in ((0, GW, cwx_ref, cbx_ref, xc_ref),
                                                 (GW, SSD_N, cwb_ref, cbb_ref, bc_ref),
                                                 (GW + SSD_N, SSD_N, cwc_ref, cbc_ref, cc_ref)):
            for t in range(width // LANES):
                tl = slice(t * LANES, (t + 1) * LANES)
                win = pad_ref[src, col0 + t * LANES:col0 + (t + 1) * LANES]
                acc = jnp.broadcast_to(cb_ref[:, tl], (BLOCK, LANES))
                for w in range(CONV_W):
                    off = HALO - CONV_W // 2 + w
                    acc = acc + win[off:off + BLOCK, :] * cw_ref[w:w + 1, tl]
                dst[r, tl] = _silu(acc).astype(dst.dtype)

    if has_s0:
        for r in range(SSD_R):
            sf_ref[:, r * SSD_P:(r + 1) * SSD_P] = s0_ref[0, r]
            sb_ref[:, r * SSD_P:(r + 1) * SSD_P] = s0_ref[1, r]
    else:
        sf_ref[...] = jnp.zeros_like(sf_ref)
        sb_ref[...] = jnp.zeros_like(sb_ref)

    def state_update(s_ref, c, w_ref, tot_ref, bm, xs):
        tot = _pair_tiles([tot_ref[c, r:r + 1, :] for r in range(SSD_R)])
        s_ref[...] = s_ref[...] * tot + _bdot_tn(bm, xs * _pair_cols(w_ref, c))

    def fwd_step(c):
        r = _rows(c)
        sfs_ref[c] = sf_ref[...].astype(BF16)
        state_update(sf_ref, c, wf_ref, totf_ref, bc_ref[r, :], xc_ref[r, :])

    def fwd(c, carry):
        fwd_step(c)
        conv(c + 1)
        return carry

    conv(0)
    lax.fori_loop(0, n_chunks - 1, fwd, 0)
    fwd_step(n_chunks - 1)
    if emit_state:
        for r in range(SSD_R):
            sfin_ref[0, r] = sf_ref[:, r * SSD_P:(r + 1) * SSD_P]

    ii = lax.broadcasted_iota(jnp.int32, (BLOCK, BLOCK), 0)
    jj = lax.broadcasted_iota(jnp.int32, (BLOCK, BLOCK), 1)
    causal = jj <= ii
    anti = ii <= jj
    low = lax.broadcasted_iota(jnp.int32, (BLOCK, LANES), 1) < SSD_P

    def bwd(t, carry):
        c = n_chunks - 1 - t
        r = _rows(c)
        cm, bm, xs = cc_ref[r, :], bc_ref[r, :], xc_ref[r, :]
        xb = xs.astype(BF16)
        sc = _bdot_nt(cm, bm)
        yf = _bdot(cm, sfs_ref[c])
        yb = _bdot(cm, sb_ref[...])
        tiles = []
        for t2 in range(SSD_R // 2):
            xt = xb[:, t2 * LANES:(t2 + 1) * LANES]
            acc = None
            for hh in range(2):
                hr = 2 * t2 + hh
                rf, rb = _row_bcast(cumf_ref, c, hr), _row_bcast(cumb_ref, c, hr)
                e = jnp.exp(jnp.where(anti, rf, rb).T - jnp.where(causal, rf, rb))
                e = e * jnp.where(causal, _row_bcast(dtf_ref, c, hr), _row_bcast(dtb_ref, c, hr))
                xh = jnp.where(low, xt, 0.0) if hh == 0 else jnp.where(low, 0.0, xt)
                part = _bdot(sc * e, xh)
                acc = part if acc is None else acc + part
            tiles.append(acc)
        y = (jnp.concatenate(tiles, axis=1) + _pair_cols(ecf_ref, c) * yf + _pair_cols(ecb_ref, c) * yb
             + d_ref[...] * xs)
        yg = y * _silu(z_ref[r, :].astype(F32))
        ss_ref[r, :] = jnp.broadcast_to(jnp.sum(yg * yg, axis=-1, keepdims=True), (BLOCK, LANES))
        o_ref[r, :] = (yg * ng_ref[...]).astype(o_ref.dtype)
        state_update(sb_ref, c, wb_ref, totb_ref, bm, xs)
        return carry

    lax.fori_loop(0, n_chunks, bwd, 0)
    if emit_state:
        for r in range(SSD_R):
            sfin_ref[1, r] = sb_ref[:, r * SSD_P:(r + 1) * SSD_P]


def _ssd_scan(zx, prep, conv_w, conv_b, d_exp, norm_gain, seq_len, n_seq, row_block0, s0=None, emit_state=False,
              fill=None):
    n_chunks = seq_len // BLOCK
    has_s0 = s0 is not None
    xcol0 = D_INNER // GW
    bcol0 = 2 * D_INNER // SSD_N
    ccol0 = bcol0 + SSD_GROUPS
    cwb0 = D_INNER // SSD_N

    def rowcol(width, col0):
        return pl.BlockSpec((seq_len, width), lambda b, g, col0=col0: (b + row_block0, col0 + g))

    def headrows(direction):
        return pl.BlockSpec((n_chunks, SSD_R, BLOCK),
                            lambda b, g, direction=direction: (b + row_block0, direction * SSD_GROUPS + g, 0))

    in_specs = [rowcol(GW, 0), rowcol(GW, xcol0), rowcol(SSD_N, bcol0), rowcol(SSD_N, ccol0)]
    args = [zx, zx, zx, zx]
    for arr in prep:
        in_specs += [headrows(0), headrows(1)]
        args += [arr, arr]
    in_specs += [pl.BlockSpec((CONV_W, GW), lambda b, g: (0, g)),
                 pl.BlockSpec((CONV_W, SSD_N), lambda b, g: (0, cwb0 + g)),
                 pl.BlockSpec((CONV_W, SSD_N), lambda b, g: (0, cwb0 + SSD_GROUPS + g)),
                 pl.BlockSpec((1, GW), lambda b, g: (0, g)),
                 pl.BlockSpec((1, SSD_N), lambda b, g: (0, cwb0 + g)),
                 pl.BlockSpec((1, SSD_N), lambda b, g: (0, cwb0 + SSD_GROUPS + g)),
                 pl.BlockSpec((1, GW), lambda b, g: (0, g)),
                 pl.BlockSpec((1, GW), lambda b, g: (0, g))]
    args += [conv_w, conv_w, conv_w, conv_b, conv_b, conv_b, d_exp, norm_gain.reshape(1, D_INNER)]
    state_spec = pl.BlockSpec((None, None, 2, SSD_R, SSD_N, SSD_P), lambda b, g: (b, 0, 0, g, 0, 0))
    if has_s0:
        in_specs.append(state_spec)
        args.append(s0)
    aliases = _fill_alias(fill, in_specs, args)
    out_specs = [pl.BlockSpec((seq_len, GW), lambda b, g: (b + row_block0, g)),
                 pl.BlockSpec((seq_len, LANES), lambda b, g: (b + row_block0, g))]
    out_shape = [jax.ShapeDtypeStruct((T_ALL, D_INNER), BF16),
                 jax.ShapeDtypeStruct((T_ALL, SSD_GROUPS * LANES), F32)]
    if emit_state:
        out_specs.append(state_spec)
        out_shape.append(jax.ShapeDtypeStruct((n_seq, 1, 2, SSD_HEADS, SSD_N, SSD_P), F32))
    return pl.pallas_call(
        functools.partial(_ssd_kernel, n_chunks, has_s0, emit_state, bool(fill)),
        grid=(n_seq, SSD_GROUPS),
        in_specs=in_specs,
        out_specs=out_specs,
        out_shape=out_shape,
        input_output_aliases=aliases,
        scratch_shapes=[pltpu.VMEM((seq_len + 2 * HALO, GW + 2 * SSD_N), F32),
                        pltpu.VMEM((seq_len, GW), F32), pltpu.VMEM((seq_len, SSD_N), BF16),
                        pltpu.VMEM((seq_len, SSD_N), BF16), pltpu.VMEM((n_chunks, SSD_N, GW), BF16),
                        pltpu.VMEM((SSD_N, GW), F32), pltpu.VMEM((SSD_N, GW), F32)],
        compiler_params=_cparams("arbitrary", "arbitrary"),
        name="ssd_scan_latent" if has_s0 else "ssd_scan_prompt",
    )(*args)


def _router_kernel(x_ref, g_ref, mod_ref, rw_ref, hn_ref, idx_ref, wgt_ref, hf_ref):
    _adaln_to(x_ref, g_ref, mod_ref, 3, 4, hf_ref)
    hf = hf_ref[...]
    hn_ref[...] = hf.astype(BF16)
    logits = jnp.dot(hf, rw_ref[...], preferred_element_type=F32, precision=lax.Precision.HIGHEST)
    lane = lax.broadcasted_iota(jnp.int32, logits.shape, 1)
    lg = jnp.where(lane < N_EXPERTS, logits, NEG_INF)
    m1 = jnp.max(lg, axis=-1, keepdims=True)
    i1 = jnp.min(jnp.where(lg == m1, lane, LANES), axis=-1, keepdims=True)
    lg2 = jnp.where(lane == i1, NEG_INF, lg)
    m2 = jnp.max(lg2, axis=-1, keepdims=True)
    i2 = jnp.min(jnp.where(lg2 == m2, lane, LANES), axis=-1, keepdims=True)
    e2 = jnp.exp(m2 - m1)
    w1 = 1.0 / (1.0 + e2)
    idx_ref[...] = jnp.where(lane == 0, i1, jnp.where(lane == 1, i2, 0))
    wgt_ref[...] = jnp.where(lane == 0, w1, jnp.where(lane == 1, e2 * w1, 0.0))


def _router(x, gain, modt, router_w):
    t, d = x.shape
    rw = jnp.pad(router_w, ((0, 0), (0, LANES - N_EXPERTS)))
    return pl.pallas_call(
        _router_kernel,
        grid=(t // ROW_TILE,),
        in_specs=[pl.BlockSpec((ROW_TILE, d), lambda i: (i, 0)),
                  pl.BlockSpec((1, d), lambda i: (0, 0)),
                  pl.BlockSpec((None, 8, d), lambda i: (i, 0, 0)),
                  pl.BlockSpec((d, LANES), lambda i: (0, 0))],
        out_specs=[pl.BlockSpec((ROW_TILE, d), lambda i: (i, 0)),
                   pl.BlockSpec((ROW_TILE, LANES), lambda i: (i, 0)),
                   pl.BlockSpec((ROW_TILE, LANES), lambda i: (i, 0))],
        out_shape=[jax.ShapeDtypeStruct((t, d), BF16), jax.ShapeDtypeStruct((t, LANES), jnp.int32),
                   jax.ShapeDtypeStruct((t, LANES), F32)],
        scratch_shapes=[pltpu.VMEM((ROW_TILE, d), F32)],
        compiler_params=_cparams("arbitrary"),
        name="moe_router",
    )(x, gain.reshape(1, d), modt, rw)


DOWN_ROWS = 512
DOWN_TN = 512


def _expert_changed(be_ref, blk, prev_blk, step):
    return jnp.logical_or(step == 0, be_ref[blk] != be_ref[prev_blk])


def _expert_up_kernel(be_ref, nu_ref, xs_ref, wg_ref, wu_ref, h_ref, wgb_ref, wub_ref):
    i = pl.program_id(1)
    used = i < nu_ref[0]

    @pl.when(jnp.logical_and(used, _expert_changed(be_ref, i, jnp.maximum(i - 1, 0), i)))
    def _():
        wgb_ref[...] = wg_ref[...].astype(BF16)
        wub_ref[...] = wu_ref[...].astype(BF16)

    @pl.when(used)
    def _():
        xs = xs_ref[...]
        h_ref[...] = (_silu(_bdot(xs, wgb_ref[...])) * _bdot(xs, wub_ref[...])).astype(h_ref.dtype)

    @pl.when(jnp.logical_not(used))
    def _():
        h_ref[...] = jnp.zeros_like(h_ref)


def _expert_down_kernel(be_ref, nu_ref, h_ref, wd_ref, o_ref, wdb_ref):
    i = pl.program_id(1)
    per = MOE_ROWS // DOWN_ROWS
    blk = i // per
    used = blk < nu_ref[0]

    @pl.when(jnp.logical_and(used, _expert_changed(be_ref, blk, jnp.maximum(i - 1, 0) // per, i)))
    def _():
        wdb_ref[...] = wd_ref[...].astype(BF16)

    @pl.when(used)
    def _():
        o_ref[...] = _bdot(h_ref[...], wdb_ref[...]).astype(o_ref.dtype)

    @pl.when(jnp.logical_not(used))
    def _():
        o_ref[...] = jnp.zeros_like(o_ref)


def _experts(xs_sorted, block_e, n_used, wg, wu, wd):
    cap, d = xs_sorted.shape
    ff = wg.shape[2]
    per = MOE_ROWS // DOWN_ROWS

    def expert_of(blk, be, nu):
        return be[jnp.minimum(blk, jnp.maximum(nu[0] - 1, 0))]

    h = pl.pallas_call(
        _expert_up_kernel,
        grid_spec=pltpu.PrefetchScalarGridSpec(
            num_scalar_prefetch=2,
            grid=(ff // FFN_TF, cap // MOE_ROWS),
            in_specs=[pl.BlockSpec((MOE_ROWS, d), lambda f, i, be, nu: (i, 0)),
                      pl.BlockSpec((None, d, FFN_TF), lambda f, i, be, nu: (expert_of(i, be, nu), 0, f)),
                      pl.BlockSpec((None, d, FFN_TF), lambda f, i, be, nu: (expert_of(i, be, nu), 0, f))],
            out_specs=pl.BlockSpec((MOE_ROWS, FFN_TF), lambda f, i, be, nu: (i, f)),
            scratch_shapes=[pltpu.VMEM((d, FFN_TF), BF16), pltpu.VMEM((d, FFN_TF), BF16)],
        ),
        out_shape=jax.ShapeDtypeStruct((cap, ff), BF16),
        compiler_params=_cparams("arbitrary", "arbitrary"),
        name="moe_expert_up",
    )(block_e, n_used, xs_sorted, wg, wu)
    return pl.pallas_call(
        _expert_down_kernel,
        grid_spec=pltpu.PrefetchScalarGridSpec(
            num_scalar_prefetch=2,
            grid=(d // DOWN_TN, cap // DOWN_ROWS),
            in_specs=[pl.BlockSpec((DOWN_ROWS, ff), lambda n, i, be, nu: (i, 0)),
                      pl.BlockSpec((None, ff, DOWN_TN), lambda n, i, be, nu: (expert_of(i // per, be, nu), 0, n))],
            out_specs=pl.BlockSpec((DOWN_ROWS, DOWN_TN), lambda n, i, be, nu: (i, n)),
            scratch_shapes=[pltpu.VMEM((ff, DOWN_TN), BF16)],
        ),
        out_shape=jax.ShapeDtypeStruct((cap, d), BF16),
        compiler_params=_cparams("arbitrary", "arbitrary"),
        name="moe_expert_down",
    )(block_e, n_used, h, wd)


def _combine_kernel(x_ref, g0_ref, g1_ref, w_ref, mod_ref, o_ref):
    w = w_ref[...]
    y = w[:, 0:1] * g0_ref[...].astype(F32) + w[:, 1:2] * g1_ref[...].astype(F32)
    o_ref[...] = x_ref[...] + mod_ref[5:6, :] * y


COMB_ROWS = 512


def _combine(x, g0, g1, wgt, modt, row0, n_rows):
    d = x.shape[1]
    b0 = row0 // COMB_ROWS
    per = ROW_TILE // COMB_ROWS
    return pl.pallas_call(
        _combine_kernel,
        grid=(n_rows // COMB_ROWS,),
        in_specs=[pl.BlockSpec((COMB_ROWS, d), lambda i: (i + b0, 0)),
                  pl.BlockSpec((COMB_ROWS, d), lambda i: (i + b0, 0)),
                  pl.BlockSpec((COMB_ROWS, d), lambda i: (i + b0, 0)),
                  pl.BlockSpec((COMB_ROWS, LANES), lambda i: (i + b0, 0)),
                  pl.BlockSpec((None, 8, d), lambda i: ((i + b0) // per, 0, 0))],
        out_specs=pl.BlockSpec((COMB_ROWS, d), lambda i: (i, 0)),
        out_shape=jax.ShapeDtypeStruct((n_rows, d), F32),
        compiler_params=_cparams("arbitrary"),
        name="moe_combine",
    )(x, g0, g1, wgt, modt)


def _moe(x, gain, modt, router_w, wg, wu, wd):
    t, d = x.shape
    hn, idx, wgt = _router(x, gain, modt, router_w)
    top_idx = idx[:, :TOP_K]
    n_slots = t * TOP_K
    flat_e = top_idx.reshape(-1)
    onehot = (flat_e[:, None] == jnp.arange(N_EXPERTS, dtype=jnp.int32)[None, :]).astype(jnp.int32)
    incl = jnp.cumsum(onehot, axis=0)
    counts = incl[-1]
    rank = jnp.sum((incl - onehot) * onehot, axis=1)
    padded = (counts + MOE_ROWS - 1) // MOE_ROWS * MOE_ROWS
    pend = jnp.cumsum(padded)
    pstart = pend - padded
    dest = pstart[flat_e] + rank
    n_blocks = n_slots // MOE_ROWS + N_EXPERTS
    cap = n_blocks * MOE_ROWS
    row_tok = (jnp.arange(cap, dtype=jnp.int32) % t).at[dest].set(jnp.arange(n_slots, dtype=jnp.int32) // TOP_K)
    block_e = jnp.clip(jnp.searchsorted(pend, jnp.arange(n_blocks, dtype=jnp.int32) * MOE_ROWS, side='right'),
                       0, N_EXPERTS - 1).astype(jnp.int32)
    n_used = (pend[-1:] // MOE_ROWS).astype(jnp.int32)
    out = _experts(hn[row_tok], block_e, n_used, wg, wu, wd)
    dest = dest.reshape(t, TOP_K)
    g0, g1 = out[dest[:, 0]], out[dest[:, 1]]
    return (_combine(x, g0, g1, wgt, modt, 0, T_PROMPT), _combine(x, g0, g1, wgt, modt, T_PROMPT, T_SAMPLE))


def _rope_tables(n_tokens, dim):
    n_rows = n_tokens // GRID_W
    row = jnp.repeat(jnp.arange(n_rows), GRID_W).astype(F32)
    col = jnp.tile(jnp.arange(GRID_W), n_rows).astype(F32)
    n_freq = dim // 4
    inv = ROPE_BASE ** (-jnp.arange(n_freq, dtype=F32) / n_freq)
    ang = jnp.concatenate([row[:, None] * inv, col[:, None] * inv], axis=-1)
    return jnp.cos(ang), jnp.sin(ang)


def kernel(x_prompt, x_sample, state_ret, cache_k, cache_v, state_ssd, c, c_ctx, ada_w, ada_b, norm_mix, norm_ffn, ev_w_in, ev_w_out, ret_decay_logit, ret_norm, att_q_norm, att_k_norm, att_sink, ffn_w_gate, ffn_w_up, ffn_w_down, ssd_w_in, ssd_conv_w, ssd_conv_b, ssd_a_log, ssd_dt_bias, ssd_d, ssd_norm, ssd_w_out, moe_router, moe_w_gate, moe_w_up, moe_w_down):
    d = D_MODEL
    x = jnp.concatenate([x_prompt.reshape(T_PROMPT, d), x_sample.reshape(T_SAMPLE, d)], axis=0)

    cvecs = jnp.concatenate([c_ctx[None, :], c, jnp.zeros((MOD_ROWS - 1 - DEC_BATCH, d), F32)], axis=0)
    mods = _modulation(cvecs, ada_w, ada_b).reshape(2, MOD_ROWS, 6, d)
    tiles_per_seq = DEC_SEQ // ROW_TILE
    tile_row = jnp.concatenate([jnp.zeros((T_PROMPT // ROW_TILE,), jnp.int32),
                                1 + jnp.arange(T_SAMPLE // ROW_TILE, dtype=jnp.int32) // tiles_per_seq])
    modt = jnp.pad(mods[:, tile_row], ((0, 0), (0, 0), (0, 2), (0, 0)))

    proj = _adaln_matmul(x, norm_mix[0], modt[0], 0, 1, ev_w_in[0], tn=512, out_dtype=BF16, name="even_in_proj")
    lg = jax.nn.log_sigmoid(ret_decay_logit[0].astype(F32))
    cos_r, sin_r = _rope_tables(DEC_SEQ, RET_DK)
    cos_a, sin_a = _rope_tables(DEC_SEQ, ATT_HD)
    cos_a2 = jnp.concatenate([cos_a, cos_a], axis=-1)
    sin_a2 = jnp.concatenate([-sin_a, sin_a], axis=-1)
    mix_ret, new_state_ret = _retention(proj, lg, ret_norm[0], SEQ, BATCH, 0, emit_state=True)
    mix_ret, = _retention(proj, lg, ret_norm[0], DEC_SEQ, DEC_BATCH, T_PROMPT // DEC_SEQ,
                          ropes=(cos_r, sin_r), s0=state_ret, fill=(mix_ret,))
    mix_att, new_k, new_v = _attention_prompt(proj, att_sink[0], att_q_norm[0], att_k_norm[0])
    mix_att = _attention_latent(proj, cache_k[:, 0], cache_v[:, 0], att_sink[0], cos_a2, sin_a2,
                                att_q_norm[0], att_k_norm[0], mix_att)
    x = _proj_residual([mix_ret, mix_att], ev_w_out[0], x, modt[0], 2, tn=1024, name="even_out_proj")
    h = _ffn_gateup(x, norm_ffn[0], modt[0], ffn_w_gate[0], ffn_w_up[0])
    x = _proj_residual([h], ffn_w_down[0], x, modt[0], 5, tn=256, name="ffn_down")

    zx = _adaln_matmul(x, norm_mix[1], modt[1], 0, 1, ssd_w_in[0], tn=1024, n_out=SSD_ZX, out_dtype=BF16,
                       name="ssd_in_proj")
    dt_raw = _adaln_matmul(x, norm_mix[1], modt[1], 0, 1, ssd_w_in[0], tn=2 * SSD_HEADS, n_out=2 * SSD_HEADS,
                           col_block0=SSD_ZX // (2 * SSD_HEADS), name="ssd_dt_proj")
    prep = _ssd_prep(dt_raw, ssd_dt_bias[0], ssd_a_log[0])
    d_exp = jnp.repeat(ssd_d[0], SSD_P)[None, :]
    conv_b = ssd_conv_b[0][None, :]
    yg, yss, new_state_ssd = _ssd_scan(zx, prep, ssd_conv_w[0], conv_b, d_exp, ssd_norm[0], SEQ, BATCH, 0,
                                       emit_state=True)
    yg, yss = _ssd_scan(zx, prep, ssd_conv_w[0], conv_b, d_exp, ssd_norm[0], DEC_SEQ, DEC_BATCH,
                        T_PROMPT // DEC_SEQ, s0=state_ssd, fill=(yg, yss))
    x = _proj_residual([yg], ssd_w_out[0], x, modt[1], 2, tn=512, row_ss=yss, name="ssd_out_proj")
    y_p, y_s = _moe(x, norm_ffn[1], modt[1], moe_router[0], moe_w_gate[0], moe_w_up[0], moe_w_down[0])

    y_prompt = y_p.reshape(BATCH, SEQ, d)
    y_sample = y_s.reshape(DEC_BATCH, DEC_SEQ, d)
    new_cache_k = new_k.reshape(BATCH, 1, SEQ, ATT_KV_HEADS, ATT_HD)
    new_cache_v = new_v.reshape(BATCH, 1, SEQ, ATT_KV_HEADS, ATT_HD)
    return (y_prompt, y_sample, new_state_ret, new_cache_k, new_cache_v, new_state_ssd)
```

```python
import functools

import jax
import jax.numpy as jnp
from jax import lax
from jax.experimental import pallas as pl
from jax.experimental.pallas import tpu as pltpu

F32 = jnp.float32
BF16 = jnp.bfloat16

D_MODEL = 2048
BATCH = 16
SEQ = 256
DEC_BATCH = 8
DEC_SEQ = 2048
PAST_LEN = 512
GRID_W = 64
BLOCK = 128
WINDOW = 128
EPS = 1e-6
ROPE_BASE = 10000.0
RET_HEADS = 4
RET_DK = 256
RET_DV = 256
ATT_HEADS = 8
ATT_KV_HEADS = 2
ATT_HD = 128
ATT_GROUP = ATT_HEADS // ATT_KV_HEADS
EVEN_IN = 5632
D_INNER = 2 * D_MODEL
SSD_P = 64
SSD_HEADS = D_INNER // SSD_P
SSD_N = 128
SSD_GROUPS = 8
SSD_R = SSD_HEADS // SSD_GROUPS
CONV_W = 5
CONV_CH = D_INNER + 2 * SSD_GROUPS * SSD_N
SSD_ZX = D_INNER + CONV_CH
D_FF = 5632
N_EXPERTS = 8
TOP_K = 2

T_PROMPT = BATCH * SEQ
T_SAMPLE = DEC_BATCH * DEC_SEQ
T_ALL = T_PROMPT + T_SAMPLE

LANES = 128
ROW_TILE = 1024
VMEM_LIMIT = 56 * 1024 * 1024
N_ROW_TILES = T_ALL // ROW_TILE
MOE_ROWS = 1024
NEG_INF = float("-inf")


def _cparams(*sem):
    return pltpu.CompilerParams(dimension_semantics=sem, vmem_limit_bytes=VMEM_LIMIT)


def _silu(x):
    return x * jax.nn.sigmoid(x)


def _bdot(a, b):
    return jnp.dot(a.astype(BF16), b.astype(BF16), preferred_element_type=F32)


def _bdot_nt(a, b):
    return lax.dot_general(a.astype(BF16), b.astype(BF16), (((1,), (1,)), ((), ())),
                           preferred_element_type=F32)


def _bdot_tn(a, b):
    return lax.dot_general(a.astype(BF16), b.astype(BF16), (((0,), (0,)), ((), ())),
                           preferred_element_type=F32)


def _rows(c, n=BLOCK):
    return pl.ds(pl.multiple_of(c * n, n), n)


MOD_ROWS = 16
MOD_TN = 1024


def _mod_kernel(c_ref, w_ref, b_ref, o_ref):
    o_ref[...] = _bdot(_silu(c_ref[...]), w_ref[...]) + b_ref[...]


def _modulation(cvecs, ada_w, ada_b):
    depth, d, n = ada_w.shape
    return pl.pallas_call(
        _mod_kernel,
        grid=(depth, n // MOD_TN),
        in_specs=[pl.BlockSpec((MOD_ROWS, d), lambda l, j: (0, 0)),
                  pl.BlockSpec((None, d, MOD_TN), lambda l, j: (l, 0, j)),
                  pl.BlockSpec((None, 1, MOD_TN), lambda l, j: (l, 0, j))],
        out_specs=pl.BlockSpec((None, MOD_ROWS, MOD_TN), lambda l, j: (l, 0, j)),
        out_shape=jax.ShapeDtypeStruct((depth, MOD_ROWS, n), F32),
        compiler_params=_cparams("arbitrary", "arbitrary"),
        name="modulation",
    )(cvecs, ada_w, ada_b.reshape(depth, 1, n))


ADALN_CHUNK = 64


def _adaln_to(x_ref, g_ref, mod_ref, shift_row, scale_row, hn_ref):
    mult = g_ref[...] * (1.0 + mod_ref[scale_row:scale_row + 1, :])
    shift = mod_ref[shift_row:shift_row + 1, :]

    def body(i, carry):
        r = _rows(i, ADALN_CHUNK)
        x = x_ref[r, :]
        ms = jnp.mean(x * x, axis=-1, keepdims=True)
        hn_ref[r, :] = (x * lax.rsqrt(ms + EPS) * mult + shift).astype(hn_ref.dtype)
        return carry

    lax.fori_loop(0, x_ref.shape[0] // ADALN_CHUNK, body, 0)


def _adaln_mm_kernel(shift_row, scale_row, has_tail, x_ref, g_ref, mod_ref, w_ref, *refs):
    if has_tail:
        wt_ref, o_ref, ot_ref, hn_ref = refs
    else:
        o_ref, hn_ref = refs

    @pl.when(pl.program_id(1) == 0)
    def _():
        _adaln_to(x_ref, g_ref, mod_ref, shift_row, scale_row, hn_ref)
        if has_tail:
            ot_ref[...] = _bdot(hn_ref[...], wt_ref[...])

    o_ref[...] = _bdot(hn_ref[...], w_ref[...]).astype(o_ref.dtype)


def _adaln_matmul(x, gain, modt, shift_row, scale_row, w, tn, n_out=None, tail=0, out_dtype=F32, name="adaln_mm"):
    t, d = x.shape
    n_out = w.shape[1] if n_out is None else n_out
    in_specs = [pl.BlockSpec((ROW_TILE, d), lambda i, j: (i, 0)),
                pl.BlockSpec((1, d), lambda i, j: (0, 0)),
                pl.BlockSpec((None, 8, d), lambda i, j: (i, 0, 0)),
                pl.BlockSpec((d, tn), lambda i, j: (0, j))]
    args = [x, gain.reshape(1, d), modt, w]
    out_specs = [pl.BlockSpec((ROW_TILE, tn), lambda i, j: (i, j))]
    out_shape = [jax.ShapeDtypeStruct((t, n_out), out_dtype)]
    if tail:
        in_specs.append(pl.BlockSpec((d, tail), lambda i, j: (0, n_out // tail)))
        args.append(w)
        out_specs.append(pl.BlockSpec((ROW_TILE, tail), lambda i, j: (i, 0)))
        out_shape.append(jax.ShapeDtypeStruct((t, tail), F32))
    return pl.pallas_call(
        functools.partial(_adaln_mm_kernel, shift_row, scale_row, bool(tail)),
        grid=(t // ROW_TILE, n_out // tn),
        in_specs=in_specs,
        out_specs=out_specs,
        out_shape=out_shape,
        scratch_shapes=[pltpu.VMEM((ROW_TILE, d), BF16)],
        compiler_params=_cparams("arbitrary", "arbitrary"),
        name=name,
    )(*args)


FFN_TF = 512


def _gateup_kernel(x_ref, g_ref, mod_ref, wg_ref, wu_ref, h_ref, hn_ref):
    @pl.when(pl.program_id(1) == 0)
    def _():
        _adaln_to(x_ref, g_ref, mod_ref, 3, 4, hn_ref)

    hn = hn_ref[...]
    h_ref[...] = (_silu(_bdot(hn, wg_ref[...])) * _bdot(hn, wu_ref[...])).astype(h_ref.dtype)


def _ffn_gateup(x, gain, modt, wg, wu):
    t, d = x.shape
    ff = wg.shape[1]
    return pl.pallas_call(
        _gateup_kernel,
        grid=(t // ROW_TILE, ff // FFN_TF),
        in_specs=[pl.BlockSpec((ROW_TILE, d), lambda i, f: (i, 0)),
                  pl.BlockSpec((1, d), lambda i, f: (0, 0)),
                  pl.BlockSpec((None, 8, d), lambda i, f: (i, 0, 0)),
                  pl.BlockSpec((d, FFN_TF), lambda i, f: (0, f)),
                  pl.BlockSpec((d, FFN_TF), lambda i, f: (0, f))],
        out_specs=pl.BlockSpec((ROW_TILE, FFN_TF), lambda i, f: (i, f)),
        out_shape=jax.ShapeDtypeStruct((t, ff), BF16),
        scratch_shapes=[pltpu.VMEM((ROW_TILE, d), BF16)],
        compiler_params=_cparams("arbitrary", "arbitrary"),
        name="ffn_gateup",
    )(x, gain.reshape(1, d), modt, wg, wu)


def _proj_res_kernel(n_a, gate_row, norm, *refs):
    a_refs = refs[:n_a]
    w_refs = refs[n_a:2 * n_a]
    x_ref, mod_ref = refs[2 * n_a], refs[2 * n_a + 1]
    pos = 2 * n_a + 2
    o_ref = refs[pos + 1] if norm else refs[pos]
    acc = _bdot(a_refs[0][...], w_refs[0][...])
    for k in range(1, n_a):
        acc = acc + _bdot(a_refs[k][...], w_refs[k][...])
    if norm:
        ss_ref = refs[pos]
        k_total = sum(a.shape[1] for a in a_refs)
        ss = ss_ref[:, 0:LANES]
        for k in range(1, ss_ref.shape[1] // LANES):
            ss = ss + ss_ref[:, k * LANES:(k + 1) * LANES]
        rs = lax.rsqrt(ss * (1.0 / k_total) + EPS)
        acc = acc * jnp.concatenate([rs] * (acc.shape[1] // LANES), axis=1)
    o_ref[...] = x_ref[...] + mod_ref[gate_row:gate_row + 1, :] * acc


def _proj_residual(a_list, w, x, modt, gate_row, tn, row_ss=None, name="proj_res"):
    t, d = x.shape
    n_a = len(a_list)
    norm = row_ss is not None
    in_specs, args, k0 = [], [], 0
    for a in a_list:
        in_specs.append(pl.BlockSpec((ROW_TILE, a.shape[1]), lambda i, j: (i, 0)))
        args.append(a)
    for a in a_list:
        ka = a.shape[1]
        assert k0 % ka == 0
        in_specs.append(pl.BlockSpec((ka, tn), lambda i, j, kb=k0 // ka: (kb, j)))
        args.append(w)
        k0 += ka
    in_specs += [pl.BlockSpec((ROW_TILE, tn), lambda i, j: (i, j)),
                 pl.BlockSpec((None, 8, tn), lambda i, j: (i, 0, j))]
    args += [x, modt]
    if norm:
        in_specs.append(pl.BlockSpec((ROW_TILE, row_ss.shape[1]), lambda i, j: (i, 0)))
        args.append(row_ss)
    return pl.pallas_call(
        functools.partial(_proj_res_kernel, n_a, gate_row, norm),
        grid=(t // ROW_TILE, d // tn),
        in_specs=in_specs,
        out_specs=pl.BlockSpec((ROW_TILE, tn), lambda i, j: (i, j)),
        out_shape=jax.ShapeDtypeStruct((t, d), F32),
        compiler_params=_cparams("arbitrary", "arbitrary"),
        name=name,
    )(*args)


def _ret_kernel(n_chunks, rope, has_s0, emit_state, has_fill, lg_ref, *refs):
    it = iter(refs)
    q_ref, k_ref, v_ref, gt_ref = next(it), next(it), next(it), next(it)
    cos_ref, sin_ref = (next(it), next(it)) if rope else (None, None)
    s0_ref = next(it) if has_s0 else None
    gain_ref = next(it)
    if has_fill:
        next(it)
    o_ref = next(it)
    sfin_ref = next(it) if emit_state else None
    qs_ref, ks_ref, sfs_ref, dm_ref, dec_ref, sf_ref, sb_ref = (next(it) for _ in range(7))

    h = pl.program_id(1)
    lgf = lg_ref[0, h]
    lgb = lg_ref[1, h]
    half = RET_DK // 2

    def prep(c, carry):
        r = _rows(c)
        q = q_ref[r, :].astype(F32)
        k = k_ref[r, :].astype(F32) * (RET_DK ** -0.5)
        if rope:
            cs, sn = cos_ref[r, :], sin_ref[r, :]
            for src, dst in ((q, qs_ref), (k, ks_ref)):
                x1, x2 = src[:, :half], src[:, half:]
                dst[r, :half] = (x1 * cs - x2 * sn).astype(BF16)
                dst[r, half:] = (x1 * sn + x2 * cs).astype(BF16)
        else:
            qs_ref[r, :] = q.astype(BF16)
            ks_ref[r, :] = k.astype(BF16)
        return carry

    lax.fori_loop(0, n_chunks, prep, 0)

    ii = lax.broadcasted_iota(jnp.int32, (BLOCK, BLOCK), 0)
    jj = lax.broadcasted_iota(jnp.int32, (BLOCK, BLOCK), 1)
    diff = (ii - jj).astype(F32)
    dm_ref[...] = jnp.exp(jnp.where(jj <= ii, diff * lgf, -diff * lgb))
    pos = lax.broadcasted_iota(jnp.int32, (BLOCK, RET_DV), 0).astype(F32)
    dec_ref[0] = jnp.exp((pos + 1.0) * lgf)
    dec_ref[1] = jnp.exp((BLOCK - pos) * lgb)
    dec_ref[2] = jnp.exp((BLOCK - 1.0 - pos) * lgf)
    dec_ref[3] = jnp.exp(pos * lgb)
    tot_f = jnp.exp(jnp.full((1, RET_DV), BLOCK * lgf, F32))
    tot_b = jnp.exp(jnp.full((1, RET_DV), BLOCK * lgb, F32))

    if has_s0:
        sf_ref[...] = s0_ref[0]
        sb_ref[...] = s0_ref[1]
    else:
        sf_ref[...] = jnp.zeros_like(sf_ref)
        sb_ref[...] = jnp.zeros_like(sb_ref)

    def fwd(c, carry):
        r = _rows(c)
        sfs_ref[c] = sf_ref[...].astype(BF16)
        kd = ks_ref[r, :].astype(F32) * dec_ref[2]
        sf_ref[...] = sf_ref[...] * tot_f + _bdot_tn(kd, v_ref[r, :])
        return carry

    lax.fori_loop(0, n_chunks, fwd, 0, unroll=2)
    if emit_state:
        sfin_ref[0] = sf_ref[...]

    def bwd(t, carry):
        c = n_chunks - 1 - t
        r = _rows(c)
        q = qs_ref[r, :]
        k = ks_ref[r, :]
        v = v_ref[r, :].astype(BF16)
        p = _bdot_nt(q, k) * dm_ref[...]
        o = _bdot(p, v)
        o = o + _bdot(q, sfs_ref[c]) * dec_ref[0]
        o = o + _bdot(q, sb_ref[...]) * dec_ref[1]
        ms = jnp.mean(o * o, axis=-1, keepdims=True)
        y = o * lax.rsqrt(ms + EPS) * gain_ref[...]
        o_ref[r, :] = (y * _silu(gt_ref[r, :].astype(F32))).astype(o_ref.dtype)
        kd = k.astype(F32) * dec_ref[3]
        sb_ref[...] = sb_ref[...] * tot_b + _bdot_tn(kd, v)
        return carry

    lax.fori_loop(0, n_chunks, bwd, 0, unroll=2)
    if emit_state:
        sfin_ref[1] = sb_ref[...]


def _fill_alias(fill, in_specs, args):
    aliases = {}
    for k, arr in enumerate(fill or ()):
        in_specs.append(pl.BlockSpec(memory_space=pl.ANY))
        args.append(arr)
        aliases[len(args) - 1] = k
    return aliases


def _retention(proj, lg, ret_norm, seq_len, n_seq, row_block0, ropes=None, s0=None, emit_state=False, fill=None):
    n_chunks = seq_len // BLOCK
    rope, has_s0 = ropes is not None, s0 is not None

    def col(cb):
        return pl.BlockSpec((seq_len, RET_DK), lambda b, h, cb=cb: (b + row_block0, cb * RET_HEADS + h))

    in_specs = [pl.BlockSpec(memory_space=pltpu.SMEM), col(0), col(1), col(2), col(3)]
    args = [lg, proj, proj, proj, proj]
    if rope:
        in_specs += [pl.BlockSpec((seq_len, RET_DK // 2), lambda b, h: (0, 0))] * 2
        args += list(ropes)
    if has_s0:
        in_specs.append(pl.BlockSpec((None, None, 2, None, RET_DK, RET_DV), lambda b, h: (b, 0, 0, h, 0, 0)))
        args.append(s0)
    in_specs.append(pl.BlockSpec((1, RET_DV), lambda b, h: (0, h)))
    args.append(ret_norm.reshape(1, RET_HEADS * RET_DV))
    aliases = _fill_alias(fill, in_specs, args)
    out_specs = [pl.BlockSpec((seq_len, RET_DV), lambda b, h: (b + row_block0, h))]
    out_shape = [jax.ShapeDtypeStruct((T_ALL, RET_HEADS * RET_DV), BF16)]
    if emit_state:
        out_specs.append(pl.BlockSpec((None, None, 2, None, RET_DK, RET_DV), lambda b, h: (b, 0, 0, h, 0, 0)))
        out_shape.append(jax.ShapeDtypeStruct((n_seq, 1, 2, RET_HEADS, RET_DK, RET_DV), F32))
    return pl.pallas_call(
        functools.partial(_ret_kernel, n_chunks, rope, has_s0, emit_state, bool(fill)),
        grid=(n_seq, RET_HEADS),
        in_specs=in_specs,
        out_specs=out_specs,
        out_shape=out_shape,
        input_output_aliases=aliases,
        scratch_shapes=[pltpu.VMEM((seq_len, RET_DK), BF16), pltpu.VMEM((seq_len, RET_DK), BF16),
                        pltpu.VMEM((n_chunks, RET_DK, RET_DV), BF16), pltpu.VMEM((BLOCK, BLOCK), F32),
                        pltpu.VMEM((4, BLOCK, RET_DV), F32), pltpu.VMEM((RET_DK, RET_DV), F32),
                        pltpu.VMEM((RET_DK, RET_DV), F32)],
        compiler_params=_cparams("arbitrary", "arbitrary"),
        name="retention_latent" if rope else "retention_prompt",
    )(*args)


def _head_norm(x, gain):
    x = x.astype(F32)
    return x * lax.rsqrt(jnp.mean(x * x, axis=-1, keepdims=True) + EPS) * gain


def _rope_full(x, cs, sn):
    return x * cs + pltpu.roll(x, ATT_HD // 2, 1) * sn


ATT_SPLIT = 2


def _sink_column(sink_ref, head0, n_heads, rows_per_head):
    n = n_heads * rows_per_head
    head = lax.broadcasted_iota(jnp.int32, (n, 1), 0) // rows_per_head
    col = jnp.full((n, 1), sink_ref[head0], F32)
    for g in range(1, n_heads):
        col = jnp.where(head == g, sink_ref[head0 + g], col)
    return col


def _att_latent_kernel(sink_ref, q_ref, k_ref, v_ref, ck_ref, cv_ref, cos_ref, sin_ref, qn_ref, kn_ref,
                       fill_ref, o_ref, kp_ref, vp_ref, ckp_ref, cvp_ref):
    del fill_ref
    kh, qb = pl.program_id(1), pl.program_id(2)
    n_chunks = DEC_SEQ // BLOCK
    loc = 3 * BLOCK

    @pl.when(qb == 0)
    def _():
        def prep(c, carry):
            r = _rows(c)
            kp_ref[r, :] = _rope_full(_head_norm(k_ref[r, :], kn_ref[...]), cos_ref[r, :], sin_ref[r, :]).astype(BF16)
            vp_ref[r, :] = v_ref[r, :].astype(BF16)
            return carry

        lax.fori_loop(0, n_chunks, prep, 0)
        ckp_ref[...] = ck_ref[...].astype(BF16)
        cvp_ref[...] = cv_ref[...].astype(BF16)

    rq = _rows(qb)
    cs, sn = cos_ref[rq, :], sin_ref[rq, :]
    qg = qn_ref[...] * (ATT_HD ** -0.5)
    start = pl.multiple_of(jnp.clip((qb - 1) * BLOCK, 0, DEC_SEQ - loc), BLOCK)
    kl, vl = kp_ref[pl.ds(start, loc), :], vp_ref[pl.ds(start, loc), :]
    qpos = qb * BLOCK + (lax.broadcasted_iota(jnp.int32, (ATT_SPLIT * BLOCK, loc), 0) & (BLOCK - 1))
    kpos = start + lax.broadcasted_iota(jnp.int32, (ATT_SPLIT * BLOCK, loc), 1)
    in_window = jnp.abs(qpos - kpos) <= WINDOW
    for g0 in range(0, ATT_GROUP, ATT_SPLIT):
        q = jnp.concatenate(
            [_rope_full(_head_norm(q_ref[:, g * ATT_HD:(g + 1) * ATT_HD], qg), cs, sn).astype(BF16)
             for g in range(g0, g0 + ATT_SPLIT)], axis=0)
        s_loc = jnp.where(in_window, _bdot_nt(q, kl), NEG_INF)
        s_ctx = _bdot_nt(q, ckp_ref[...])
        sink = _sink_column(sink_ref, kh * ATT_GROUP + g0, ATT_SPLIT, BLOCK)
        m = jnp.maximum(jnp.maximum(jnp.max(s_loc, axis=-1, keepdims=True),
                                    jnp.max(s_ctx, axis=-1, keepdims=True)), sink)
        p_loc = jnp.exp(s_loc - m)
        p_ctx = jnp.exp(s_ctx - m)
        den = jnp.sum(p_loc, axis=-1, keepdims=True) + jnp.sum(p_ctx, axis=-1, keepdims=True) + jnp.exp(sink - m)
        o = (_bdot(p_loc, vl) + _bdot(p_ctx, cvp_ref[...])) / den
        for g in range(ATT_SPLIT):
            o_ref[:, (g0 + g) * ATT_HD:(g0 + g + 1) * ATT_HD] = o[g * BLOCK:(g + 1) * BLOCK, :].astype(o_ref.dtype)


def _attention_latent(proj, cache_k, cache_v, sink, cos_a, sin_a, q_norm, k_norm, fill):
    nqb = DEC_SEQ // BLOCK
    rb0 = T_PROMPT // DEC_SEQ
    qcol0 = 4 * RET_HEADS * RET_DK // (ATT_GROUP * ATT_HD)
    kcol0 = (4 * RET_HEADS * RET_DK + ATT_HEADS * ATT_HD) // ATT_HD
    vcol0 = kcol0 + ATT_KV_HEADS
    ck = cache_k.reshape(DEC_BATCH, PAST_LEN, ATT_KV_HEADS * ATT_HD)
    cv = cache_v.reshape(DEC_BATCH, PAST_LEN, ATT_KV_HEADS * ATT_HD)
    return pl.pallas_call(
        _att_latent_kernel,
        grid=(DEC_BATCH, ATT_KV_HEADS, nqb),
        in_specs=[pl.BlockSpec(memory_space=pltpu.SMEM),
                  pl.BlockSpec((BLOCK, ATT_GROUP * ATT_HD),
                               lambda b, kh, qb: (T_PROMPT // BLOCK + b * nqb + qb, qcol0 + kh)),
                  pl.BlockSpec((DEC_SEQ, ATT_HD), lambda b, kh, qb: (rb0 + b, kcol0 + kh)),
                  pl.BlockSpec((DEC_SEQ, ATT_HD), lambda b, kh, qb: (rb0 + b, vcol0 + kh)),
                  pl.BlockSpec((None, PAST_LEN, ATT_HD), lambda b, kh, qb: (b, 0, kh)),
                  pl.BlockSpec((None, PAST_LEN, ATT_HD), lambda b, kh, qb: (b, 0, kh)),
                  pl.BlockSpec((DEC_SEQ, ATT_HD), lambda b, kh, qb: (0, 0)),
                  pl.BlockSpec((DEC_SEQ, ATT_HD), lambda b, kh, qb: (0, 0)),
                  pl.BlockSpec((1, ATT_HD), lambda b, kh, qb: (0, 0)),
                  pl.BlockSpec((1, ATT_HD), lambda b, kh, qb: (0, 0)),
                  pl.BlockSpec(memory_space=pl.ANY)],
        out_specs=pl.BlockSpec((BLOCK, ATT_GROUP * ATT_HD),
                               lambda b, kh, qb: (T_PROMPT // BLOCK + b * nqb + qb, kh)),
        out_shape=jax.ShapeDtypeStruct((T_ALL, ATT_HEADS * ATT_HD), BF16),
        input_output_aliases={10: 0},
        scratch_shapes=[pltpu.VMEM((DEC_SEQ, ATT_HD), BF16), pltpu.VMEM((DEC_SEQ, ATT_HD), BF16),
                        pltpu.VMEM((PAST_LEN, ATT_HD), BF16), pltpu.VMEM((PAST_LEN, ATT_HD), BF16)],
        compiler_params=_cparams("arbitrary", "arbitrary", "arbitrary"),
        name="attention_latent",
    )(sink, proj, proj, proj, ck, cv, cos_a, sin_a, q_norm.reshape(1, ATT_HD), k_norm.reshape(1, ATT_HD), fill)


def _att_prompt_kernel(sink_ref, q_ref, k_ref, v_ref, qn_ref, kn_ref, o_ref, nk_ref, nv_ref):
    kh = pl.program_id(1)
    kn = _head_norm(k_ref[...], kn_ref[...])
    v = v_ref[...]
    nk_ref[...] = kn
    nv_ref[...] = v.astype(F32)
    qg = qn_ref[...] * (ATT_HD ** -0.5)
    q = jnp.concatenate([_head_norm(q_ref[:, g * ATT_HD:(g + 1) * ATT_HD], qg).astype(BF16)
                         for g in range(ATT_GROUP)], axis=0)
    s = _bdot_nt(q, kn)
    sink = _sink_column(sink_ref, kh * ATT_GROUP, ATT_GROUP, SEQ)
    m = jnp.maximum(jnp.max(s, axis=-1, keepdims=True), sink)
    p = jnp.exp(s - m)
    den = jnp.sum(p, axis=-1, keepdims=True) + jnp.exp(sink - m)
    o = _bdot(p, v) / den
    for g in range(ATT_GROUP):
        o_ref[:, g * ATT_HD:(g + 1) * ATT_HD] = o[g * SEQ:(g + 1) * SEQ, :].astype(o_ref.dtype)


def _attention_prompt(proj, sink, q_norm, k_norm):
    qcol0 = 4 * RET_HEADS * RET_DK // (ATT_GROUP * ATT_HD)
    kcol0 = (4 * RET_HEADS * RET_DK + ATT_HEADS * ATT_HD) // ATT_HD
    vcol0 = kcol0 + ATT_KV_HEADS
    kv_spec = pl.BlockSpec((None, SEQ, ATT_HD), lambda b, kh: (b, 0, kh))
    kv_shape = jax.ShapeDtypeStruct((BATCH, SEQ, ATT_KV_HEADS * ATT_HD), F32)
    return pl.pallas_call(
        _att_prompt_kernel,
        grid=(BATCH, ATT_KV_HEADS),
        in_specs=[pl.BlockSpec(memory_space=pltpu.SMEM),
                  pl.BlockSpec((SEQ, ATT_GROUP * ATT_HD), lambda b, kh: (b, qcol0 + kh)),
                  pl.BlockSpec((SEQ, ATT_HD), lambda b, kh: (b, kcol0 + kh)),
                  pl.BlockSpec((SEQ, ATT_HD), lambda b, kh: (b, vcol0 + kh)),
                  pl.BlockSpec((1, ATT_HD), lambda b, kh: (0, 0)),
                  pl.BlockSpec((1, ATT_HD), lambda b, kh: (0, 0))],
        out_specs=[pl.BlockSpec((SEQ, ATT_GROUP * ATT_HD), lambda b, kh: (b, kh)), kv_spec, kv_spec],
        out_shape=[jax.ShapeDtypeStruct((T_ALL, ATT_HEADS * ATT_HD), BF16), kv_shape, kv_shape],
        compiler_params=_cparams("arbitrary", "arbitrary"),
        name="attention_prompt",
    )(sink, proj, proj, proj, q_norm.reshape(1, ATT_HD), k_norm.reshape(1, ATT_HD))


def _split_dot(m01, a):
    hi = a.astype(BF16)
    r1 = a - hi.astype(F32)
    mid = r1.astype(BF16)
    lo = (r1 - mid.astype(F32)).astype(BF16)
    return (jnp.dot(m01, hi, preferred_element_type=F32) + jnp.dot(m01, mid, preferred_element_type=F32)
            + jnp.dot(m01, lo, preferred_element_type=F32))


PREP_CHUNKS = ROW_TILE // BLOCK


def _ssd_prep_kernel(raw_ref, bias_ref, alog_ref, cum_ref, dt_ref, w_ref, tot_ref, ecum_ref):
    ii = lax.broadcasted_iota(jnp.int32, (BLOCK, BLOCK), 0)
    jj = lax.broadcasted_iota(jnp.int32, (BLOCK, BLOCK), 1)
    lower = jnp.where(jj <= ii, 1.0, 0.0).astype(BF16)
    upper = jnp.where(jj >= ii, 1.0, 0.0).astype(BF16)
    fwd_lane = lax.broadcasted_iota(jnp.int32, (BLOCK, LANES), 1) < SSD_HEADS
    neg_a = -jnp.exp(alog_ref[...])

    def chunk(k, carry):
        x = raw_ref[_rows(k), :] + bias_ref[...]
        dt = jnp.maximum(x, 0.0) + jnp.log1p(jnp.exp(-jnp.abs(x)))
        a = dt * neg_a
        incl = _split_dot(lower, a)
        rincl = _split_dot(upper, a)
        cum = jnp.where(fwd_lane, incl, rincl)
        tot = jnp.where(fwd_lane[:1], incl[BLOCK - 1:BLOCK, :], rincl[0:1, :])
        cum_ref[k] = cum.T
        dt_ref[k] = dt.T
        w_ref[k] = (dt * jnp.exp(tot - cum)).T
        tot_ref[k] = jnp.broadcast_to(jnp.exp(tot), (BLOCK, LANES)).T
        ecum_ref[k] = jnp.exp(cum).T
        return carry

    lax.fori_loop(0, PREP_CHUNKS, chunk, 0)


def _ssd_prep(dt_raw, dt_bias, a_log):
    nc = T_ALL // BLOCK
    spec = pl.BlockSpec((PREP_CHUNKS, 2 * SSD_HEADS, BLOCK), lambda c: (c, 0, 0))
    shape = jax.ShapeDtypeStruct((nc, 2 * SSD_HEADS, BLOCK), F32)
    return pl.pallas_call(
        _ssd_prep_kernel,
        grid=(nc // PREP_CHUNKS,),
        in_specs=[pl.BlockSpec((ROW_TILE, 2 * SSD_HEADS), lambda c: (c, 0)),
                  pl.BlockSpec((1, 2 * SSD_HEADS), lambda c: (0, 0)),
                  pl.BlockSpec((1, 2 * SSD_HEADS), lambda c: (0, 0))],
        out_specs=[spec] * 5,
        out_shape=[shape] * 5,
        compiler_params=_cparams("arbitrary"),
        name="ssd_prep",
    )(dt_raw, dt_bias.reshape(1, 2 * SSD_HEADS), a_log.reshape(1, 2 * SSD_HEADS))


GW = SSD_R * SSD_P
HALO = 8


def _pair_tiles(per_head):
    low = lax.broadcasted_iota(jnp.int32, per_head[0].shape, 1) < SSD_P
    return jnp.concatenate([jnp.where(low, per_head[2 * t], per_head[2 * t + 1]) for t in range(SSD_R // 2)],
                           axis=1)


def _row_bcast(ref, c, r):
    return jnp.broadcast_to(ref[c, r:r + 1, :], (BLOCK, BLOCK))


def _pair_cols(ref, c):
    top = lax.broadcasted_iota(jnp.int32, (BLOCK, BLOCK), 0) < SSD_P
    return jnp.concatenate(
        [jnp.where(top, _row_bcast(ref, c, 2 * t), _row_bcast(ref, c, 2 * t + 1)).T for t in range(SSD_R // 2)],
        axis=1)


def _ssd_kernel(n_chunks, has_s0, emit_state, has_fill, *refs):
    it = iter(refs)
    z_ref, x_ref, b_ref, c_ref = (next(it) for _ in range(4))
    cumf_ref, cumb_ref, dtf_ref, dtb_ref, wf_ref, wb_ref, totf_ref, totb_ref, ecf_ref, ecb_ref = (
        next(it) for _ in range(10))
    cwx_ref, cwb_ref, cwc_ref, cbx_ref, cbb_ref, cbc_ref, d_ref, ng_ref = (next(it) for _ in range(8))
    s0_ref = next(it) if has_s0 else None
    if has_fill:
        next(it), next(it)
    o_ref, ss_ref = next(it), next(it)
    sfin_ref = next(it) if emit_state else None
    pad_ref, xc_ref, bc_ref, cc_ref, sfs_ref, sf_ref, sb_ref = (next(it) for _ in range(7))
    seq_len = n_chunks * BLOCK

    pad_ref[0:HALO, :] = jnp.zeros((HALO, GW + 2 * SSD_N), F32)
    pad_ref[HALO + seq_len:2 * HALO + seq_len, :] = jnp.zeros((HALO, GW + 2 * SSD_N), F32)

    def fill(c, carry):
        dst = pl.ds(pl.multiple_of(c * BLOCK, BLOCK) + HALO, BLOCK)
        r = _rows(c)
        pad_ref[dst, 0:GW] = x_ref[r, :].astype(F32)
        pad_ref[dst, GW:GW + SSD_N] = b_ref[r, :].astype(F32)
        pad_ref[dst, GW + SSD_N:GW + 2 * SSD_N] = c_ref[r, :].astype(F32)
        return carry

    lax.fori_loop(0, n_chunks, fill, 0)

    def conv(c):
        r = _rows(c)
        src = pl.ds(pl.multiple_of(c * BLOCK, BLOCK), BLOCK + 2 * HALO)
        for col0, width, cw_ref, cb_ref, dst in ((0, GW, cwx_ref, cbx_ref, xc_ref),
                                                 (GW, SSD_N, cwb_ref, cbb_ref, bc_ref),
                                                 (GW + SSD_N, SSD_N, cwc_ref, cbc_ref, cc_ref)):
            for t in range(width // LANES):
                tl = slice(t * LANES, (t + 1) * LANES)
                win = pad_ref[src, col0 + t * LANES:col0 + (t + 1) * LANES]
                acc = jnp.broadcast_to(cb_ref[:, tl], (BLOCK, LANES))
                for w in range(CONV_W):
                    off = HALO - CONV_W // 2 + w
                    acc = acc + win[off:off + BLOCK, :] * cw_ref[w:w + 1, tl]
                dst[r, tl] = _silu(acc).astype(dst.dtype)

    if has_s0:
        for r in range(SSD_R):
            sf_ref[:, r * SSD_P:(r + 1) * SSD_P] = s0_ref[0, r]
            sb_ref[:, r * SSD_P:(r + 1) * SSD_P] = s0_ref[1, r]
    else:
        sf_ref[...] = jnp.zeros_like(sf_ref)
        sb_ref[...] = jnp.zeros_like(sb_ref)

    def state_update(s_ref, c, w_ref, tot_ref, bm, xs):
        tot = _pair_tiles([tot_ref[c, r:r + 1, :] for r in range(SSD_R)])
        s_ref[...] = s_ref[...] * tot + _bdot_tn(bm, xs * _pair_cols(w_ref, c))

    def fwd_step(c):
        r = _rows(c)
        sfs_ref[c] = sf_ref[...].astype(BF16)
        state_update(sf_ref, c, wf_ref, totf_ref, bc_ref[r, :], xc_ref[r, :])

    def fwd(c, carry):
        fwd_step(c)
        conv(c + 1)
        return carry

    conv(0)
    lax.fori_loop(0, n_chunks - 1, fwd, 0)
    fwd_step(n_chunks - 1)
    if emit_state:
        for r in range(SSD_R):
            sfin_ref[0, r] = sf_ref[:, r * SSD_P:(r + 1) * SSD_P]

    ii = lax.broadcasted_iota(jnp.int32, (BLOCK, BLOCK), 0)
    jj = lax.broadcasted_iota(jnp.int32, (BLOCK, BLOCK), 1)
    causal = jj <= ii
    anti = ii <= jj
    low = lax.broadcasted_iota(jnp.int32, (BLOCK, LANES), 1) < SSD_P

    def bwd(t, carry):
        c = n_chunks - 1 - t
        r = _rows(c)
        cm, bm, xs = cc_ref[r, :], bc_ref[r, :], xc_ref[r, :]
        xb = xs.astype(BF16)
        sc = _bdot_nt(cm, bm)
        yf = _bdot(cm, sfs_ref[c])
        yb = _bdot(cm, sb_ref[...])
        tiles = []
        for t2 in range(SSD_R // 2):
            xt = xb[:, t2 * LANES:(t2 + 1) * LANES]
            acc = None
            for hh in range(2):
                hr = 2 * t2 + hh
                rf, rb = _row_bcast(cumf_ref, c, hr), _row_bcast(cumb_ref, c, hr)
                e = jnp.exp(jnp.where(anti, rf, rb).T - jnp.where(causal, rf, rb))
                e = e * jnp.where(causal, _row_bcast(dtf_ref, c, hr), _row_bcast(dtb_ref, c, hr))
                xh = jnp.where(low, xt, 0.0) if hh == 0 else jnp.where(low, 0.0, xt)
                part = _bdot(sc * e, xh)
                acc = part if acc is None else acc + part
            tiles.append(acc)
        y = (jnp.concatenate(tiles, axis=1) + _pair_cols(ecf_ref, c) * yf + _pair_cols(ecb_ref, c) * yb
             + d_ref[...] * xs)
        yg = y * _silu(z_ref[r, :].astype(F32))
        ss_ref[r, :] = jnp.broadcast_to(jnp.sum(yg * yg, axis=-1, keepdims=True), (BLOCK, LANES))
        o_ref[r, :] = (yg * ng_ref[...]).astype(o_ref.dtype)
        state_update(sb_ref, c, wb_ref, totb_ref, bm, xs)
        return carry

    lax.fori_loop(0, n_chunks, bwd, 0)
    if emit_state:
        for r in range(SSD_R):
            sfin_ref[1, r] = sb_ref[:, r * SSD_P:(r + 1) * SSD_P]


def _ssd_scan(zx, prep, conv_w, conv_b, d_exp, norm_gain, seq_len, n_seq, row_block0, s0=None, emit_state=False,
              fill=None):
    n_chunks = seq_len // BLOCK
    has_s0 = s0 is not None
    xcol0 = D_INNER // GW
    bcol0 = 2 * D_INNER // SSD_N
    ccol0 = bcol0 + SSD_GROUPS
    cwb0 = D_INNER // SSD_N

    def rowcol(width, col0):
        return pl.BlockSpec((seq_len, width), lambda b, g, col0=col0: (b + row_block0, col0 + g))

    def headrows(direction):
        return pl.BlockSpec((n_chunks, SSD_R, BLOCK),
                            lambda b, g, direction=direction: (b + row_block0, direction * SSD_GROUPS + g, 0))

    in_specs = [rowcol(GW, 0), rowcol(GW, xcol0), rowcol(SSD_N, bcol0), rowcol(SSD_N, ccol0)]
    args = [zx, zx, zx, zx]
    for arr in prep:
        in_specs += [headrows(0), headrows(1)]
        args += [arr, arr]
    in_specs += [pl.BlockSpec((CONV_W, GW), lambda b, g: (0, g)),
                 pl.BlockSpec((CONV_W, SSD_N), lambda b, g: (0, cwb0 + g)),
                 pl.BlockSpec((CONV_W, SSD_N), lambda b, g: (0, cwb0 + SSD_GROUPS + g)),
                 pl.BlockSpec((1, GW), lambda b, g: (0, g)),
                 pl.BlockSpec((1, SSD_N), lambda b, g: (0, cwb0 + g)),
                 pl.BlockSpec((1, SSD_N), lambda b, g: (0, cwb0 + SSD_GROUPS + g)),
                 pl.BlockSpec((1, GW), lambda b, g: (0, g)),
                 pl.BlockSpec((1, GW), lambda b, g: (0, g))]
    args += [conv_w, conv_w, conv_w, conv_b, conv_b, conv_b, d_exp, norm_gain.reshape(1, D_INNER)]
    state_spec = pl.BlockSpec((None, None, 2, SSD_R, SSD_N, SSD_P), lambda b, g: (b, 0, 0, g, 0, 0))
    if has_s0:
        in_specs.append(state_spec)
        args.append(s0)
    aliases = _fill_alias(fill, in_specs, args)
    out_specs = [pl.BlockSpec((seq_len, GW), lambda b, g: (b + row_block0, g)),
                 pl.BlockSpec((seq_len, LANES), lambda b, g: (b + row_block0, g))]
    out_shape = [jax.ShapeDtypeStruct((T_ALL, D_INNER), BF16),
                 jax.ShapeDtypeStruct((T_ALL, SSD_GROUPS * LANES), F32)]
    if emit_state:
        out_specs.append(state_spec)
        out_shape.append(jax.ShapeDtypeStruct((n_seq, 1, 2, SSD_HEADS, SSD_N, SSD_P), F32))
    return pl.pallas_call(
        functools.partial(_ssd_kernel, n_chunks, has_s0, emit_state, bool(fill)),
        grid=(n_seq, SSD_GROUPS),
        in_specs=in_specs,
        out_specs=out_specs,
        out_shape=out_shape,
        input_output_aliases=aliases,
        scratch_shapes=[pltpu.VMEM((seq_len + 2 * HALO, GW + 2 * SSD_N), F32),
                        pltpu.VMEM((seq_len, GW), F32), pltpu.VMEM((seq_len, SSD_N), BF16),
                        pltpu.VMEM((seq_len, SSD_N), BF16), pltpu.VMEM((n_chunks, SSD_N, GW), BF16),
                        pltpu.VMEM((SSD_N, GW), F32), pltpu.VMEM((SSD_N, GW), F32)],
        compiler_params=_cparams("arbitrary", "arbitrary"),
        name="ssd_scan_latent" if has_s0 else "ssd_scan_prompt",
    )(*args)


def _router_kernel(x_ref, g_ref, mod_ref, rw_ref, hn_ref, idx_ref, wgt_ref, hf_ref):
    _adaln_to(x_ref, g_ref, mod_ref, 3, 4, hf_ref)
    hf = hf_ref[...]
    h_hi = hf.astype(BF16)
    h_lo = (hf - h_hi.astype(F32)).astype(BF16)
    hn_ref[...] = h_hi
    rw = rw_ref[...]
    r_hi = rw.astype(BF16)
    r_lo = (rw - r_hi.astype(F32)).astype(BF16)
    t1 = jnp.dot(h_hi, jnp.concatenate([r_hi, r_lo], axis=1), preferred_element_type=F32)
    logits = t1[:, :LANES] + t1[:, LANES:] + jnp.dot(h_lo, r_hi, preferred_element_type=F32)
    lane = lax.broadcasted_iota(jnp.int32, logits.shape, 1)
    lg = jnp.where(lane < N_EXPERTS, logits, NEG_INF)
    m1 = jnp.max(lg, axis=-1, keepdims=True)
    i1 = jnp.min(jnp.where(lg == m1, lane, LANES), axis=-1, keepdims=True)
    lg2 = jnp.where(lane == i1, NEG_INF, lg)
    m2 = jnp.max(lg2, axis=-1, keepdims=True)
    i2 = jnp.min(jnp.where(lg2 == m2, lane, LANES), axis=-1, keepdims=True)
    e2 = jnp.exp(m2 - m1)
    w1 = 1.0 / (1.0 + e2)
    idx_ref[...] = jnp.where(lane == 0, i1, jnp.where(lane == 1, i2, 0))
    wgt_ref[...] = jnp.where(lane == 0, w1, jnp.where(lane == 1, e2 * w1, 0.0))


def _router(x, gain, modt, router_w):
    t, d = x.shape
    rw = jnp.pad(router_w, ((0, 0), (0, LANES - N_EXPERTS)))
    return pl.pallas_call(
        _router_kernel,
        grid=(t // ROW_TILE,),
        in_specs=[pl.BlockSpec((ROW_TILE, d), lambda i: (i, 0)),
                  pl.BlockSpec((1, d), lambda i: (0, 0)),
                  pl.BlockSpec((None, 8, d), lambda i: (i, 0, 0)),
                  pl.BlockSpec((d, LANES), lambda i: (0, 0))],
        out_specs=[pl.BlockSpec((ROW_TILE, d), lambda i: (i, 0)),
                   pl.BlockSpec((ROW_TILE, LANES), lambda i: (i, 0)),
                   pl.BlockSpec((ROW_TILE, LANES), lambda i: (i, 0))],
        out_shape=[jax.ShapeDtypeStruct((t, d), BF16), jax.ShapeDtypeStruct((t, LANES), jnp.int32),
                   jax.ShapeDtypeStruct((t, LANES), F32)],
        scratch_shapes=[pltpu.VMEM((ROW_TILE, d), F32)],
        compiler_params=_cparams("arbitrary"),
        name="moe_router",
    )(x, gain.reshape(1, d), modt, rw)


DOWN_ROWS = 512
DOWN_TN = 512


def _expert_changed(be_ref, blk, prev_blk, step):
    return jnp.logical_or(step == 0, be_ref[blk] != be_ref[prev_blk])


def _expert_up_kernel(be_ref, nu_ref, xs_ref, wg_ref, wu_ref, h_ref, wgb_ref, wub_ref):
    i = pl.program_id(1)
    used = i < nu_ref[0]

    @pl.when(jnp.logical_and(used, _expert_changed(be_ref, i, jnp.maximum(i - 1, 0), i)))
    def _():
        wgb_ref[...] = wg_ref[...].astype(BF16)
        wub_ref[...] = wu_ref[...].astype(BF16)

    @pl.when(used)
    def _():
        xs = xs_ref[...]
        h_ref[...] = (_silu(_bdot(xs, wgb_ref[...])) * _bdot(xs, wub_ref[...])).astype(h_ref.dtype)

    @pl.when(jnp.logical_not(used))
    def _():
        h_ref[...] = jnp.zeros_like(h_ref)


def _expert_down_kernel(be_ref, nu_ref, h_ref, wd_ref, o_ref, wdb_ref):
    i = pl.program_id(1)
    per = MOE_ROWS // DOWN_ROWS
    blk = i // per
    used = blk < nu_ref[0]

    @pl.when(jnp.logical_and(used, _expert_changed(be_ref, blk, jnp.maximum(i - 1, 0) // per, i)))
    def _():
        wdb_ref[...] = wd_ref[...].astype(BF16)

    @pl.when(used)
    def _():
        o_ref[...] = _bdot(h_ref[...], wdb_ref[...]).astype(o_ref.dtype)

    @pl.when(jnp.logical_not(used))
    def _():
        o_ref[...] = jnp.zeros_like(o_ref)


def _experts(xs_sorted, block_e, n_used, wg, wu, wd):
    cap, d = xs_sorted.shape
    ff = wg.shape[2]
    per = MOE_ROWS // DOWN_ROWS

    def expert_of(blk, be, nu):
        return be[jnp.minimum(blk, jnp.maximum(nu[0] - 1, 0))]

    h = pl.pallas_call(
        _expert_up_kernel,
        grid_spec=pltpu.PrefetchScalarGridSpec(
            num_scalar_prefetch=2,
            grid=(ff // FFN_TF, cap // MOE_ROWS),
            in_specs=[pl.BlockSpec((MOE_ROWS, d), lambda f, i, be, nu: (i, 0)),
                      pl.BlockSpec((None, d, FFN_TF), lambda f, i, be, nu: (expert_of(i, be, nu), 0, f)),
                      pl.BlockSpec((None, d, FFN_TF), lambda f, i, be, nu: (expert_of(i, be, nu), 0, f))],
            out_specs=pl.BlockSpec((MOE_ROWS, FFN_TF), lambda f, i, be, nu: (i, f)),
            scratch_shapes=[pltpu.VMEM((d, FFN_TF), BF16), pltpu.VMEM((d, FFN_TF), BF16)],
        ),
        out_shape=jax.ShapeDtypeStruct((cap, ff), BF16),
        compiler_params=_cparams("arbitrary", "arbitrary"),
        name="moe_expert_up",
    )(block_e, n_used, xs_sorted, wg, wu)
    return pl.pallas_call(
        _expert_down_kernel,
        grid_spec=pltpu.PrefetchScalarGridSpec(
            num_scalar_prefetch=2,
            grid=(d // DOWN_TN, cap // DOWN_ROWS),
            in_specs=[pl.BlockSpec((DOWN_ROWS, ff), lambda n, i, be, nu: (i, 0)),
                      pl.BlockSpec((None, ff, DOWN_TN), lambda n, i, be, nu: (expert_of(i // per, be, nu), 0, n))],
            out_specs=pl.BlockSpec((DOWN_ROWS, DOWN_TN), lambda n, i, be, nu: (i, n)),
            scratch_shapes=[pltpu.VMEM((ff, DOWN_TN), BF16)],
        ),
        out_shape=jax.ShapeDtypeStruct((cap, d), BF16),
        compiler_params=_cparams("arbitrary", "arbitrary"),
        name="moe_expert_down",
    )(block_e, n_used, h, wd)


def _combine_kernel(x_ref, g0_ref, g1_ref, w_ref, mod_ref, o_ref):
    w = w_ref[...]
    y = w[:, 0:1] * g0_ref[...].astype(F32) + w[:, 1:2] * g1_ref[...].astype(F32)
    o_ref[...] = x_ref[...] + mod_ref[5:6, :] * y


COMB_ROWS = 512


def _combine(x, g0, g1, wgt, modt, row0, n_rows):
    d = x.shape[1]
    b0 = row0 // COMB_ROWS
    per = ROW_TILE // COMB_ROWS
    return pl.pallas_call(
        _combine_kernel,
        grid=(n_rows // COMB_ROWS,),
        in_specs=[pl.BlockSpec((COMB_ROWS, d), lambda i: (i + b0, 0)),
                  pl.BlockSpec((COMB_ROWS, d), lambda i: (i + b0, 0)),
                  pl.BlockSpec((COMB_ROWS, d), lambda i: (i + b0, 0)),
                  pl.BlockSpec((COMB_ROWS, LANES), lambda i: (i + b0, 0)),
                  pl.BlockSpec((None, 8, d), lambda i: ((i + b0) // per, 0, 0))],
        out_specs=pl.BlockSpec((COMB_ROWS, d), lambda i: (i, 0)),
        out_shape=jax.ShapeDtypeStruct((n_rows, d), F32),
        compiler_params=_cparams("arbitrary"),
        name="moe_combine",
    )(x, g0, g1, wgt, modt)


def _moe(x, gain, modt, router_w, wg, wu, wd):
    t, d = x.shape
    hn, idx, wgt = _router(x, gain, modt, router_w)
    top_idx = idx[:, :TOP_K]
    n_slots = t * TOP_K
    flat_e = top_idx.reshape(-1)
    onehot = (flat_e[:, None] == jnp.arange(N_EXPERTS, dtype=jnp.int32)[None, :]).astype(jnp.int32)
    incl = jnp.cumsum(onehot, axis=0)
    counts = incl[-1]
    rank = jnp.sum((incl - onehot) * onehot, axis=1)
    padded = (counts + MOE_ROWS - 1) // MOE_ROWS * MOE_ROWS
    pend = jnp.cumsum(padded)
    pstart = pend - padded
    dest = pstart[flat_e] + rank
    n_blocks = n_slots // MOE_ROWS + N_EXPERTS
    cap = n_blocks * MOE_ROWS
    row_tok = (jnp.arange(cap, dtype=jnp.int32) % t).at[dest].set(jnp.arange(n_slots, dtype=jnp.int32) // TOP_K)
    block_e = jnp.clip(jnp.searchsorted(pend, jnp.arange(n_blocks, dtype=jnp.int32) * MOE_ROWS, side='right'),
                       0, N_EXPERTS - 1).astype(jnp.int32)
    n_used = (pend[-1:] // MOE_ROWS).astype(jnp.int32)
    out = _experts(hn[row_tok], block_e, n_used, wg, wu, wd)
    dest = dest.reshape(t, TOP_K)
    g0, g1 = out[dest[:, 0]], out[dest[:, 1]]
    return (_combine(x, g0, g1, wgt, modt, 0, T_PROMPT), _combine(x, g0, g1, wgt, modt, T_PROMPT, T_SAMPLE))


def _rope_tables(n_tokens, dim):
    n_rows = n_tokens // GRID_W
    row = jnp.repeat(jnp.arange(n_rows), GRID_W).astype(F32)
    col = jnp.tile(jnp.arange(GRID_W), n_rows).astype(F32)
    n_freq = dim // 4
    inv = ROPE_BASE ** (-jnp.arange(n_freq, dtype=F32) / n_freq)
    ang = jnp.concatenate([row[:, None] * inv, col[:, None] * inv], axis=-1)
    return jnp.cos(ang), jnp.sin(ang)


def kernel(x_prompt, x_sample, state_ret, cache_k, cache_v, state_ssd, c, c_ctx, ada_w, ada_b, norm_mix, norm_ffn, ev_w_in, ev_w_out, ret_decay_logit, ret_norm, att_q_norm, att_k_norm, att_sink, ffn_w_gate, ffn_w_up, ffn_w_down, ssd_w_in, ssd_conv_w, ssd_conv_b, ssd_a_log, ssd_dt_bias, ssd_d, ssd_norm, ssd_w_out, moe_router, moe_w_gate, moe_w_up, moe_w_down):
    d = D_MODEL
    x = jnp.concatenate([x_prompt.reshape(T_PROMPT, d), x_sample.reshape(T_SAMPLE, d)], axis=0)

    cvecs = jnp.concatenate([c_ctx[None, :], c, jnp.zeros((MOD_ROWS - 1 - DEC_BATCH, d), F32)], axis=0)
    mods = _modulation(cvecs, ada_w, ada_b).reshape(2, MOD_ROWS, 6, d)
    tiles_per_seq = DEC_SEQ // ROW_TILE
    tile_row = jnp.concatenate([jnp.zeros((T_PROMPT // ROW_TILE,), jnp.int32),
                                1 + jnp.arange(T_SAMPLE // ROW_TILE, dtype=jnp.int32) // tiles_per_seq])
    modt = jnp.pad(mods[:, tile_row], ((0, 0), (0, 0), (0, 2), (0, 0)))

    proj, = _adaln_matmul(x, norm_mix[0], modt[0], 0, 1, ev_w_in[0], tn=512, out_dtype=BF16, name="even_in_proj")
    lg = jax.nn.log_sigmoid(ret_decay_logit[0].astype(F32))
    cos_r, sin_r = _rope_tables(DEC_SEQ, RET_DK)
    cos_a, sin_a = _rope_tables(DEC_SEQ, ATT_HD)
    cos_a2 = jnp.concatenate([cos_a, cos_a], axis=-1)
    sin_a2 = jnp.concatenate([-sin_a, sin_a], axis=-1)
    mix_ret, new_state_ret = _retention(proj, lg, ret_norm[0], SEQ, BATCH, 0, emit_state=True)
    mix_ret, = _retention(proj, lg, ret_norm[0], DEC_SEQ, DEC_BATCH, T_PROMPT // DEC_SEQ,
                          ropes=(cos_r, sin_r), s0=state_ret, fill=(mix_ret,))
    mix_att, new_k, new_v = _attention_prompt(proj, att_sink[0], att_q_norm[0], att_k_norm[0])
    mix_att = _attention_latent(proj, cache_k[:, 0], cache_v[:, 0], att_sink[0], cos_a2, sin_a2,
                                att_q_norm[0], att_k_norm[0], mix_att)
    x = _proj_residual([mix_ret, mix_att], ev_w_out[0], x, modt[0], 2, tn=1024, name="even_out_proj")
    h = _ffn_gateup(x, norm_ffn[0], modt[0], ffn_w_gate[0], ffn_w_up[0])
    x = _proj_residual([h], ffn_w_down[0], x, modt[0], 5, tn=256, name="ffn_down")

    zx, dt_raw = _adaln_matmul(x, norm_mix[1], modt[1], 0, 1, ssd_w_in[0], tn=1024, n_out=SSD_ZX,
                               tail=2 * SSD_HEADS, out_dtype=BF16, name="ssd_in_proj")
    prep = _ssd_prep(dt_raw, ssd_dt_bias[0], ssd_a_log[0])
    d_exp = jnp.repeat(ssd_d[0], SSD_P)[None, :]
    conv_b = ssd_conv_b[0][None, :]
    yg, yss, new_state_ssd = _ssd_scan(zx, prep, ssd_conv_w[0], conv_b, d_exp, ssd_norm[0], SEQ, BATCH, 0,
                                       emit_state=True)
    yg, yss = _ssd_scan(zx, prep, ssd_conv_w[0], conv_b, d_exp, ssd_norm[0], DEC_SEQ, DEC_BATCH,
                        T_PROMPT // DEC_SEQ, s0=state_ssd, fill=(yg, yss))
    x = _proj_residual([yg], ssd_w_out[0], x, modt[1], 2, tn=512, row_ss=yss, name="ssd_out_proj")
    y_p, y_s = _moe(x, norm_ffn[1], modt[1], moe_router[0], moe_w_gate[0], moe_w_up[0], moe_w_down[0])

    y_prompt = y_p.reshape(BATCH, SEQ, d)
    y_sample = y_s.reshape(DEC_BATCH, DEC_SEQ, d)
    new_cache_k = new_k.reshape(BATCH, 1, SEQ, ATT_KV_HEADS, ATT_HD)
    new_cache_v = new_v.reshape(BATCH, 1, SEQ, ATT_KV_HEADS, ATT_HD)
    return (y_prompt, y_sample, new_state_ret, new_cache_k, new_cache_v, new_state_ssd)
```

```python
import functools

import jax
import jax.numpy as jnp
from jax import lax
from jax.experimental import pallas as pl
from jax.experimental.pallas import tpu as pltpu

F32 = jnp.float32
BF16 = jnp.bfloat16

D_MODEL = 2048
BATCH = 16
SEQ = 256
DEC_BATCH = 8
DEC_SEQ = 2048
PAST_LEN = 512
GRID_W = 64
BLOCK = 128
WINDOW = 128
EPS = 1e-6
ROPE_BASE = 10000.0
RET_HEADS = 4
RET_DK = 256
RET_DV = 256
ATT_HEADS = 8
ATT_KV_HEADS = 2
ATT_HD = 128
ATT_GROUP = ATT_HEADS // ATT_KV_HEADS
EVEN_IN = 5632
D_INNER = 2 * D_MODEL
SSD_P = 64
SSD_HEADS = D_INNER // SSD_P
SSD_N = 128
SSD_GROUPS = 8
SSD_R = SSD_HEADS // SSD_GROUPS
CONV_W = 5
CONV_CH = D_INNER + 2 * SSD_GROUPS * SSD_N
SSD_ZX = D_INNER + CONV_CH
D_FF = 5632
N_EXPERTS = 8
TOP_K = 2

T_PROMPT = BATCH * SEQ
T_SAMPLE = DEC_BATCH * DEC_SEQ
T_ALL = T_PROMPT + T_SAMPLE

LANES = 128
ROW_TILE = 1024
VMEM_LIMIT = 56 * 1024 * 1024
N_ROW_TILES = T_ALL // ROW_TILE
MOE_ROWS = 1024
NEG_INF = float("-inf")


def _cparams(*sem):
    return pltpu.CompilerParams(dimension_semantics=sem, vmem_limit_bytes=VMEM_LIMIT)


def _silu(x):
    return x * jax.nn.sigmoid(x)


def _bdot(a, b):
    return jnp.dot(a.astype(BF16), b.astype(BF16), preferred_element_type=F32)


def _bdot_nt(a, b):
    return lax.dot_general(a.astype(BF16), b.astype(BF16), (((1,), (1,)), ((), ())),
                           preferred_element_type=F32)


def _bdot_tn(a, b):
    return lax.dot_general(a.astype(BF16), b.astype(BF16), (((0,), (0,)), ((), ())),
                           preferred_element_type=F32)


def _rows(c, n=BLOCK):
    return pl.ds(pl.multiple_of(c * n, n), n)


MOD_ROWS = 16
MOD_TN = 1024


def _mod_kernel(c_ref, w_ref, b_ref, o_ref):
    o_ref[...] = _bdot(_silu(c_ref[...]), w_ref[...]) + b_ref[...]


def _modulation(cvecs, ada_w, ada_b):
    depth, d, n = ada_w.shape
    return pl.pallas_call(
        _mod_kernel,
        grid=(depth, n // MOD_TN),
        in_specs=[pl.BlockSpec((MOD_ROWS, d), lambda l, j: (0, 0)),
                  pl.BlockSpec((None, d, MOD_TN), lambda l, j: (l, 0, j)),
                  pl.BlockSpec((None, 1, MOD_TN), lambda l, j: (l, 0, j))],
        out_specs=pl.BlockSpec((None, MOD_ROWS, MOD_TN), lambda l, j: (l, 0, j)),
        out_shape=jax.ShapeDtypeStruct((depth, MOD_ROWS, n), F32),
        compiler_params=_cparams("arbitrary", "arbitrary"),
        name="modulation",
    )(cvecs, ada_w, ada_b.reshape(depth, 1, n))


ADALN_CHUNK = 64


def _adaln_to(x_ref, g_ref, mod_ref, shift_row, scale_row, hn_ref):
    mult = g_ref[...] * (1.0 + mod_ref[scale_row:scale_row + 1, :])
    shift = mod_ref[shift_row:shift_row + 1, :]

    def body(i, carry):
        r = _rows(i, ADALN_CHUNK)
        x = x_ref[r, :]
        ms = jnp.mean(x * x, axis=-1, keepdims=True)
        hn_ref[r, :] = (x * lax.rsqrt(ms + EPS) * mult + shift).astype(hn_ref.dtype)
        return carry

    lax.fori_loop(0, x_ref.shape[0] // ADALN_CHUNK, body, 0)


def _adaln_mm_kernel(shift_row, scale_row, has_tail, x_ref, g_ref, mod_ref, w_ref, *refs):
    if has_tail:
        wt_ref, o_ref, ot_ref, hn_ref = refs
    else:
        o_ref, hn_ref = refs

    @pl.when(pl.program_id(1) == 0)
    def _():
        _adaln_to(x_ref, g_ref, mod_ref, shift_row, scale_row, hn_ref)
        if has_tail:
            ot_ref[...] = _bdot(hn_ref[...], wt_ref[...])

    o_ref[...] = _bdot(hn_ref[...], w_ref[...]).astype(o_ref.dtype)


def _adaln_matmul(x, gain, modt, shift_row, scale_row, w, tn, n_out=None, tail=0, out_dtype=F32, name="adaln_mm"):
    t, d = x.shape
    n_out = w.shape[1] if n_out is None else n_out
    in_specs = [pl.BlockSpec((ROW_TILE, d), lambda i, j: (i, 0)),
                pl.BlockSpec((1, d), lambda i, j: (0, 0)),
                pl.BlockSpec((None, 8, d), lambda i, j: (i, 0, 0)),
                pl.BlockSpec((d, tn), lambda i, j: (0, j))]
    args = [x, gain.reshape(1, d), modt, w]
    out_specs = [pl.BlockSpec((ROW_TILE, tn), lambda i, j: (i, j))]
    out_shape = [jax.ShapeDtypeStruct((t, n_out), out_dtype)]
    if tail:
        in_specs.append(pl.BlockSpec((d, tail), lambda i, j: (0, n_out // tail)))
        args.append(w)
        out_specs.append(pl.BlockSpec((ROW_TILE, tail), lambda i, j: (i, 0)))
        out_shape.append(jax.ShapeDtypeStruct((t, tail), F32))
    return pl.pallas_call(
        functools.partial(_adaln_mm_kernel, shift_row, scale_row, bool(tail)),
        grid=(t // ROW_TILE, n_out // tn),
        in_specs=in_specs,
        out_specs=out_specs,
        out_shape=out_shape,
        scratch_shapes=[pltpu.VMEM((ROW_TILE, d), BF16)],
        compiler_params=_cparams("arbitrary", "arbitrary"),
        name=name,
    )(*args)


FFN_TF = 512


def _gateup_kernel(x_ref, g_ref, mod_ref, wg_ref, wu_ref, h_ref, hn_ref):
    @pl.when(pl.program_id(1) == 0)
    def _():
        _adaln_to(x_ref, g_ref, mod_ref, 3, 4, hn_ref)

    hn = hn_ref[...]
    h_ref[...] = (_silu(_bdot(hn, wg_ref[...])) * _bdot(hn, wu_ref[...])).astype(h_ref.dtype)


def _ffn_gateup(x, gain, modt, wg, wu):
    t, d = x.shape
    ff = wg.shape[1]
    return pl.pallas_call(
        _gateup_kernel,
        grid=(t // ROW_TILE, ff // FFN_TF),
        in_specs=[pl.BlockSpec((ROW_TILE, d), lambda i, f: (i, 0)),
                  pl.BlockSpec((1, d), lambda i, f: (0, 0)),
                  pl.BlockSpec((None, 8, d), lambda i, f: (i, 0, 0)),
                  pl.BlockSpec((d, FFN_TF), lambda i, f: (0, f)),
                  pl.BlockSpec((d, FFN_TF), lambda i, f: (0, f))],
        out_specs=pl.BlockSpec((ROW_TILE, FFN_TF), lambda i, f: (i, f)),
        out_shape=jax.ShapeDtypeStruct((t, ff), BF16),
        scratch_shapes=[pltpu.VMEM((ROW_TILE, d), BF16)],
        compiler_params=_cparams("arbitrary", "arbitrary"),
        name="ffn_gateup",
    )(x, gain.reshape(1, d), modt, wg, wu)


def _proj_res_kernel(n_a, gate_row, norm, *refs):
    a_refs = refs[:n_a]
    w_refs = refs[n_a:2 * n_a]
    x_ref, mod_ref = refs[2 * n_a], refs[2 * n_a + 1]
    pos = 2 * n_a + 2
    o_ref = refs[pos + 1] if norm else refs[pos]
    acc = _bdot(a_refs[0][...], w_refs[0][...])
    for k in range(1, n_a):
        acc = acc + _bdot(a_refs[k][...], w_refs[k][...])
    if norm:
        ss_ref = refs[pos]
        k_total = sum(a.shape[1] for a in a_refs)
        ss = ss_ref[:, 0:LANES]
        for k in range(1, ss_ref.shape[1] // LANES):
            ss = ss + ss_ref[:, k * LANES:(k + 1) * LANES]
        rs = lax.rsqrt(ss * (1.0 / k_total) + EPS)
        acc = acc * jnp.concatenate([rs] * (acc.shape[1] // LANES), axis=1)
    o_ref[...] = x_ref[...] + mod_ref[gate_row:gate_row + 1, :] * acc


def _proj_residual(a_list, w, x, modt, gate_row, tn, row_ss=None, name="proj_res"):
    t, d = x.shape
    n_a = len(a_list)
    norm = row_ss is not None
    in_specs, args, k0 = [], [], 0
    for a in a_list:
        in_specs.append(pl.BlockSpec((ROW_TILE, a.shape[1]), lambda i, j: (i, 0)))
        args.append(a)
    for a in a_list:
        ka = a.shape[1]
        assert k0 % ka == 0
        in_specs.append(pl.BlockSpec((ka, tn), lambda i, j, kb=k0 // ka: (kb, j)))
        args.append(w)
        k0 += ka
    in_specs += [pl.BlockSpec((ROW_TILE, tn), lambda i, j: (i, j)),
                 pl.BlockSpec((None, 8, tn), lambda i, j: (i, 0, j))]
    args += [x, modt]
    if norm:
        in_specs.append(pl.BlockSpec((ROW_TILE, row_ss.shape[1]), lambda i, j: (i, 0)))
        args.append(row_ss)
    return pl.pallas_call(
        functools.partial(_proj_res_kernel, n_a, gate_row, norm),
        grid=(t // ROW_TILE, d // tn),
        in_specs=in_specs,
        out_specs=pl.BlockSpec((ROW_TILE, tn), lambda i, j: (i, j)),
        out_shape=jax.ShapeDtypeStruct((t, d), F32),
        compiler_params=_cparams("arbitrary", "arbitrary"),
        name=name,
    )(*args)


def _ret_kernel(n_chunks, rope, has_s0, emit_state, has_fill, lg_ref, *refs):
    it = iter(refs)
    q_ref, k_ref, v_ref, gt_ref = next(it), next(it), next(it), next(it)
    cos_ref, sin_ref = (next(it), next(it)) if rope else (None, None)
    s0_ref = next(it) if has_s0 else None
    gain_ref = next(it)
    if has_fill:
        next(it)
    o_ref = next(it)
    sfin_ref = next(it) if emit_state else None
    qs_ref, ks_ref, sfs_ref, dm_ref, dec_ref, sf_ref, sb_ref = (next(it) for _ in range(7))

    h = pl.program_id(1)
    lgf = lg_ref[0, h]
    lgb = lg_ref[1, h]
    half = RET_DK // 2

    def prep(c, carry):
        r = _rows(c)
        q = q_ref[r, :].astype(F32)
        k = k_ref[r, :].astype(F32) * (RET_DK ** -0.5)
        if rope:
            cs, sn = cos_ref[r, :], sin_ref[r, :]
            for src, dst in ((q, qs_ref), (k, ks_ref)):
                x1, x2 = src[:, :half], src[:, half:]
                dst[r, :half] = (x1 * cs - x2 * sn).astype(BF16)
                dst[r, half:] = (x1 * sn + x2 * cs).astype(BF16)
        else:
            qs_ref[r, :] = q.astype(BF16)
            ks_ref[r, :] = k.astype(BF16)
        return carry

    lax.fori_loop(0, n_chunks, prep, 0)

    ii = lax.broadcasted_iota(jnp.int32, (BLOCK, BLOCK), 0)
    jj = lax.broadcasted_iota(jnp.int32, (BLOCK, BLOCK), 1)
    diff = (ii - jj).astype(F32)
    dm_ref[...] = jnp.exp(jnp.where(jj <= ii, diff * lgf, -diff * lgb))
    pos = lax.broadcasted_iota(jnp.int32, (BLOCK, RET_DV), 0).astype(F32)
    dec_ref[0] = jnp.exp((pos + 1.0) * lgf)
    dec_ref[1] = jnp.exp((BLOCK - pos) * lgb)
    dec_ref[2] = jnp.exp((BLOCK - 1.0 - pos) * lgf)
    dec_ref[3] = jnp.exp(pos * lgb)
    tot_f = jnp.exp(jnp.full((1, RET_DV), BLOCK * lgf, F32))
    tot_b = jnp.exp(jnp.full((1, RET_DV), BLOCK * lgb, F32))

    if has_s0:
        sf_ref[...] = s0_ref[0]
        sb_ref[...] = s0_ref[1]
    else:
        sf_ref[...] = jnp.zeros_like(sf_ref)
        sb_ref[...] = jnp.zeros_like(sb_ref)

    def fwd(c, carry):
        r = _rows(c)
        sfs_ref[c] = sf_ref[...].astype(BF16)
        kd = ks_ref[r, :].astype(F32) * dec_ref[2]
        sf_ref[...] = sf_ref[...] * tot_f + _bdot_tn(kd, v_ref[r, :])
        return carry

    lax.fori_loop(0, n_chunks, fwd, 0, unroll=2)
    if emit_state:
        sfin_ref[0] = sf_ref[...]

    def bwd(t, carry):
        c = n_chunks - 1 - t
        r = _rows(c)
        q = qs_ref[r, :]
        k = ks_ref[r, :]
        v = v_ref[r, :].astype(BF16)
        p = _bdot_nt(q, k) * dm_ref[...]
        o = _bdot(p, v)
        o = o + _bdot(q, sfs_ref[c]) * dec_ref[0]
        o = o + _bdot(q, sb_ref[...]) * dec_ref[1]
        ms = jnp.mean(o * o, axis=-1, keepdims=True)
        y = o * lax.rsqrt(ms + EPS) * gain_ref[...]
        o_ref[r, :] = (y * _silu(gt_ref[r, :].astype(F32))).astype(o_ref.dtype)
        kd = k.astype(F32) * dec_ref[3]
        sb_ref[...] = sb_ref[...] * tot_b + _bdot_tn(kd, v)
        return carry

    lax.fori_loop(0, n_chunks, bwd, 0, unroll=2)
    if emit_state:
        sfin_ref[1] = sb_ref[...]


def _fill_alias(fill, in_specs, args):
    aliases = {}
    for k, arr in enumerate(fill or ()):
        in_specs.append(pl.BlockSpec(memory_space=pl.ANY))
        args.append(arr)
        aliases[len(args) - 1] = k
    return aliases


def _retention(proj, lg, ret_norm, seq_len, n_seq, row_block0, ropes=None, s0=None, emit_state=False, fill=None):
    n_chunks = seq_len // BLOCK
    rope, has_s0 = ropes is not None, s0 is not None

    def col(cb):
        return pl.BlockSpec((seq_len, RET_DK), lambda b, h, cb=cb: (b + row_block0, cb * RET_HEADS + h))

    in_specs = [pl.BlockSpec(memory_space=pltpu.SMEM), col(0), col(1), col(2), col(3)]
    args = [lg, proj, proj, proj, proj]
    if rope:
        in_specs += [pl.BlockSpec((seq_len, RET_DK // 2), lambda b, h: (0, 0))] * 2
        args += list(ropes)
    if has_s0:
        in_specs.append(pl.BlockSpec((None, None, 2, None, RET_DK, RET_DV), lambda b, h: (b, 0, 0, h, 0, 0)))
        args.append(s0)
    in_specs.append(pl.BlockSpec((1, RET_DV), lambda b, h: (0, h)))
    args.append(ret_norm.reshape(1, RET_HEADS * RET_DV))
    aliases = _fill_alias(fill, in_specs, args)
    out_specs = [pl.BlockSpec((seq_len, RET_DV), lambda b, h: (b + row_block0, h))]
    out_shape = [jax.ShapeDtypeStruct((T_ALL, RET_HEADS * RET_DV), BF16)]
    if emit_state:
        out_specs.append(pl.BlockSpec((None, None, 2, None, RET_DK, RET_DV), lambda b, h: (b, 0, 0, h, 0, 0)))
        out_shape.append(jax.ShapeDtypeStruct((n_seq, 1, 2, RET_HEADS, RET_DK, RET_DV), F32))
    return pl.pallas_call(
        functools.partial(_ret_kernel, n_chunks, rope, has_s0, emit_state, bool(fill)),
        grid=(n_seq, RET_HEADS),
        in_specs=in_specs,
        out_specs=out_specs,
        out_shape=out_shape,
        input_output_aliases=aliases,
        scratch_shapes=[pltpu.VMEM((seq_len, RET_DK), BF16), pltpu.VMEM((seq_len, RET_DK), BF16),
                        pltpu.VMEM((n_chunks, RET_DK, RET_DV), BF16), pltpu.VMEM((BLOCK, BLOCK), F32),
                        pltpu.VMEM((4, BLOCK, RET_DV), F32), pltpu.VMEM((RET_DK, RET_DV), F32),
                        pltpu.VMEM((RET_DK, RET_DV), F32)],
        compiler_params=_cparams("arbitrary", "arbitrary"),
        name="retention_latent" if rope else "retention_prompt",
    )(*args)


def _head_norm(x, gain):
    x = x.astype(F32)
    return x * lax.rsqrt(jnp.mean(x * x, axis=-1, keepdims=True) + EPS) * gain


def _rope_full(x, cs, sn):
    return x * cs + pltpu.roll(x, ATT_HD // 2, 1) * sn


ATT_SPLIT = 2


def _sink_column(sink_ref, head0, n_heads, rows_per_head):
    n = n_heads * rows_per_head
    head = lax.broadcasted_iota(jnp.int32, (n, 1), 0) // rows_per_head
    col = jnp.full((n, 1), sink_ref[head0], F32)
    for g in range(1, n_heads):
        col = jnp.where(head == g, sink_ref[head0 + g], col)
    return col


def _att_latent_kernel(sink_ref, q_ref, k_ref, v_ref, ck_ref, cv_ref, cos_ref, sin_ref, qn_ref, kn_ref,
                       fill_ref, o_ref, kp_ref, vp_ref, ckp_ref, cvp_ref):
    del fill_ref
    kh, qb = pl.program_id(1), pl.program_id(2)
    n_chunks = DEC_SEQ // BLOCK
    loc = 3 * BLOCK

    @pl.when(qb == 0)
    def _():
        def prep(c, carry):
            r = _rows(c)
            kp_ref[r, :] = _rope_full(_head_norm(k_ref[r, :], kn_ref[...]), cos_ref[r, :], sin_ref[r, :]).astype(BF16)
            vp_ref[r, :] = v_ref[r, :].astype(BF16)
            return carry

        lax.fori_loop(0, n_chunks, prep, 0)
        ckp_ref[...] = ck_ref[...].astype(BF16)
        cvp_ref[...] = cv_ref[...].astype(BF16)

    rq = _rows(qb)
    cs, sn = cos_ref[rq, :], sin_ref[rq, :]
    qg = qn_ref[...] * (ATT_HD ** -0.5)
    start = pl.multiple_of(jnp.clip((qb - 1) * BLOCK, 0, DEC_SEQ - loc), BLOCK)
    kl, vl = kp_ref[pl.ds(start, loc), :], vp_ref[pl.ds(start, loc), :]
    qpos = qb * BLOCK + (lax.broadcasted_iota(jnp.int32, (ATT_SPLIT * BLOCK, loc), 0) & (BLOCK - 1))
    kpos = start + lax.broadcasted_iota(jnp.int32, (ATT_SPLIT * BLOCK, loc), 1)
    in_window = jnp.abs(qpos - kpos) <= WINDOW
    for g0 in range(0, ATT_GROUP, ATT_SPLIT):
        q = jnp.concatenate(
            [_rope_full(_head_norm(q_ref[:, g * ATT_HD:(g + 1) * ATT_HD], qg), cs, sn).astype(BF16)
             for g in range(g0, g0 + ATT_SPLIT)], axis=0)
        s_loc = jnp.where(in_window, _bdot_nt(q, kl), NEG_INF)
        s_ctx = _bdot_nt(q, ckp_ref[...])
        sink = _sink_column(sink_ref, kh * ATT_GROUP + g0, ATT_SPLIT, BLOCK)
        m = jnp.maximum(jnp.maximum(jnp.max(s_loc, axis=-1, keepdims=True),
                                    jnp.max(s_ctx, axis=-1, keepdims=True)), sink)
        p_loc = jnp.exp(s_loc - m)
        p_ctx = jnp.exp(s_ctx - m)
        den = jnp.sum(p_loc, axis=-1, keepdims=True) + jnp.sum(p_ctx, axis=-1, keepdims=True) + jnp.exp(sink - m)
        o = (_bdot(p_loc, vl) + _bdot(p_ctx, cvp_ref[...])) / den
        for g in range(ATT_SPLIT):
            o_ref[:, (g0 + g) * ATT_HD:(g0 + g + 1) * ATT_HD] = o[g * BLOCK:(g + 1) * BLOCK, :].astype(o_ref.dtype)


def _attention_latent(proj, cache_k, cache_v, sink, cos_a, sin_a, q_norm, k_norm, fill):
    nqb = DEC_SEQ // BLOCK
    rb0 = T_PROMPT // DEC_SEQ
    qcol0 = 4 * RET_HEADS * RET_DK // (ATT_GROUP * ATT_HD)
    kcol0 = (4 * RET_HEADS * RET_DK + ATT_HEADS * ATT_HD) // ATT_HD
    vcol0 = kcol0 + ATT_KV_HEADS
    ck = cache_k.reshape(DEC_BATCH, PAST_LEN, ATT_KV_HEADS * ATT_HD)
    cv = cache_v.reshape(DEC_BATCH, PAST_LEN, ATT_KV_HEADS * ATT_HD)
    return pl.pallas_call(
        _att_latent_kernel,
        grid=(DEC_BATCH, ATT_KV_HEADS, nqb),
        in_specs=[pl.BlockSpec(memory_space=pltpu.SMEM),
                  pl.BlockSpec((BLOCK, ATT_GROUP * ATT_HD),
                               lambda b, kh, qb: (T_PROMPT // BLOCK + b * nqb + qb, qcol0 + kh)),
                  pl.BlockSpec((DEC_SEQ, ATT_HD), lambda b, kh, qb: (rb0 + b, kcol0 + kh)),
                  pl.BlockSpec((DEC_SEQ, ATT_HD), lambda b, kh, qb: (rb0 + b, vcol0 + kh)),
                  pl.BlockSpec((None, PAST_LEN, ATT_HD), lambda b, kh, qb: (b, 0, kh)),
                  pl.BlockSpec((None, PAST_LEN, ATT_HD), lambda b, kh, qb: (b, 0, kh)),
                  pl.BlockSpec((DEC_SEQ, ATT_HD), lambda b, kh, qb: (0, 0)),
                  pl.BlockSpec((DEC_SEQ, ATT_HD), lambda b, kh, qb: (0, 0)),
                  pl.BlockSpec((1, ATT_HD), lambda b, kh, qb: (0, 0)),
                  pl.BlockSpec((1, ATT_HD), lambda b, kh, qb: (0, 0)),
                  pl.BlockSpec(memory_space=pl.ANY)],
        out_specs=pl.BlockSpec((BLOCK, ATT_GROUP * ATT_HD),
                               lambda b, kh, qb: (T_PROMPT // BLOCK + b * nqb + qb, kh)),
        out_shape=jax.ShapeDtypeStruct((T_ALL, ATT_HEADS * ATT_HD), BF16),
        input_output_aliases={10: 0},
        scratch_shapes=[pltpu.VMEM((DEC_SEQ, ATT_HD), BF16), pltpu.VMEM((DEC_SEQ, ATT_HD), BF16),
                        pltpu.VMEM((PAST_LEN, ATT_HD), BF16), pltpu.VMEM((PAST_LEN, ATT_HD), BF16)],
        compiler_params=_cparams("arbitrary", "arbitrary", "arbitrary"),
        name="attention_latent",
    )(sink, proj, proj, proj, ck, cv, cos_a, sin_a, q_norm.reshape(1, ATT_HD), k_norm.reshape(1, ATT_HD), fill)


def _att_prompt_kernel(sink_ref, q_ref, k_ref, v_ref, qn_ref, kn_ref, o_ref, nk_ref, nv_ref):
    kh = pl.program_id(1)
    kn = _head_norm(k_ref[...], kn_ref[...])
    v = v_ref[...]
    nk_ref[...] = kn
    nv_ref[...] = v.astype(F32)
    qg = qn_ref[...] * (ATT_HD ** -0.5)
    q = jnp.concatenate([_head_norm(q_ref[:, g * ATT_HD:(g + 1) * ATT_HD], qg).astype(BF16)
                         for g in range(ATT_GROUP)], axis=0)
    s = _bdot_nt(q, kn)
    sink = _sink_column(sink_ref, kh * ATT_GROUP, ATT_GROUP, SEQ)
    m = jnp.maximum(jnp.max(s, axis=-1, keepdims=True), sink)
    p = jnp.exp(s - m)
    den = jnp.sum(p, axis=-1, keepdims=True) + jnp.exp(sink - m)
    o = _bdot(p, v) / den
    for g in range(ATT_GROUP):
        o_ref[:, g * ATT_HD:(g + 1) * ATT_HD] = o[g * SEQ:(g + 1) * SEQ, :].astype(o_ref.dtype)


def _attention_prompt(proj, sink, q_norm, k_norm):
    qcol0 = 4 * RET_HEADS * RET_DK // (ATT_GROUP * ATT_HD)
    kcol0 = (4 * RET_HEADS * RET_DK + ATT_HEADS * ATT_HD) // ATT_HD
    vcol0 = kcol0 + ATT_KV_HEADS
    kv_spec = pl.BlockSpec((None, SEQ, ATT_HD), lambda b, kh: (b, 0, kh))
    kv_shape = jax.ShapeDtypeStruct((BATCH, SEQ, ATT_KV_HEADS * ATT_HD), F32)
    return pl.pallas_call(
        _att_prompt_kernel,
        grid=(BATCH, ATT_KV_HEADS),
        in_specs=[pl.BlockSpec(memory_space=pltpu.SMEM),
                  pl.BlockSpec((SEQ, ATT_GROUP * ATT_HD), lambda b, kh: (b, qcol0 + kh)),
                  pl.BlockSpec((SEQ, ATT_HD), lambda b, kh: (b, kcol0 + kh)),
                  pl.BlockSpec((SEQ, ATT_HD), lambda b, kh: (b, vcol0 + kh)),
                  pl.BlockSpec((1, ATT_HD), lambda b, kh: (0, 0)),
                  pl.BlockSpec((1, ATT_HD), lambda b, kh: (0, 0))],
        out_specs=[pl.BlockSpec((SEQ, ATT_GROUP * ATT_HD), lambda b, kh: (b, kh)), kv_spec, kv_spec],
        out_shape=[jax.ShapeDtypeStruct((T_ALL, ATT_HEADS * ATT_HD), BF16), kv_shape, kv_shape],
        compiler_params=_cparams("arbitrary", "arbitrary"),
        name="attention_prompt",
    )(sink, proj, proj, proj, q_norm.reshape(1, ATT_HD), k_norm.reshape(1, ATT_HD))


def _split_dot(m01, a):
    hi = a.astype(BF16)
    r1 = a - hi.astype(F32)
    mid = r1.astype(BF16)
    lo = (r1 - mid.astype(F32)).astype(BF16)
    return (jnp.dot(m01, hi, preferred_element_type=F32) + jnp.dot(m01, mid, preferred_element_type=F32)
            + jnp.dot(m01, lo, preferred_element_type=F32))


PREP_CHUNKS = ROW_TILE // BLOCK


def _ssd_prep_kernel(raw_ref, bias_ref, alog_ref, cum_ref, dt_ref, w_ref, tot_ref, ecum_ref):
    ii = lax.broadcasted_iota(jnp.int32, (BLOCK, BLOCK), 0)
    jj = lax.broadcasted_iota(jnp.int32, (BLOCK, BLOCK), 1)
    lower = jnp.where(jj <= ii, 1.0, 0.0).astype(BF16)
    upper = jnp.where(jj >= ii, 1.0, 0.0).astype(BF16)
    fwd_lane = lax.broadcasted_iota(jnp.int32, (BLOCK, LANES), 1) < SSD_HEADS
    neg_a = -jnp.exp(alog_ref[...])

    def chunk(k, carry):
        x = raw_ref[_rows(k), :] + bias_ref[...]
        dt = jnp.maximum(x, 0.0) + jnp.log1p(jnp.exp(-jnp.abs(x)))
        a = dt * neg_a
        incl = _split_dot(lower, a)
        rincl = _split_dot(upper, a)
        cum = jnp.where(fwd_lane, incl, rincl)
        tot = jnp.where(fwd_lane[:1], incl[BLOCK - 1:BLOCK, :], rincl[0:1, :])
        cum_ref[k] = cum.T
        dt_ref[k] = dt.T
        w_ref[k] = (dt * jnp.exp(tot - cum)).T
        tot_ref[k] = jnp.broadcast_to(jnp.exp(tot), (BLOCK, LANES)).T
        ecum_ref[k] = jnp.exp(cum).T
        return carry

    lax.fori_loop(0, PREP_CHUNKS, chunk, 0)


def _ssd_prep(dt_raw, dt_bias, a_log):
    nc = T_ALL // BLOCK
    spec = pl.BlockSpec((PREP_CHUNKS, 2 * SSD_HEADS, BLOCK), lambda c: (c, 0, 0))
    shape = jax.ShapeDtypeStruct((nc, 2 * SSD_HEADS, BLOCK), F32)
    return pl.pallas_call(
        _ssd_prep_kernel,
        grid=(nc // PREP_CHUNKS,),
        in_specs=[pl.BlockSpec((ROW_TILE, 2 * SSD_HEADS), lambda c: (c, 0)),
                  pl.BlockSpec((1, 2 * SSD_HEADS), lambda c: (0, 0)),
                  pl.BlockSpec((1, 2 * SSD_HEADS), lambda c: (0, 0))],
        out_specs=[spec] * 5,
        out_shape=[shape] * 5,
        compiler_params=_cparams("arbitrary"),
        name="ssd_prep",
    )(dt_raw, dt_bias.reshape(1, 2 * SSD_HEADS), a_log.reshape(1, 2 * SSD_HEADS))


GW = SSD_R * SSD_P
HALO = 8


def _pair_tiles(per_head):
    low = lax.broadcasted_iota(jnp.int32, per_head[0].shape, 1) < SSD_P
    return jnp.concatenate([jnp.where(low, per_head[2 * t], per_head[2 * t + 1]) for t in range(SSD_R // 2)],
                           axis=1)


def _row_bcast(ref, c, r):
    return jnp.broadcast_to(ref[c, r:r + 1, :], (BLOCK, BLOCK))


def _pair_cols(ref, c):
    top = lax.broadcasted_iota(jnp.int32, (BLOCK, BLOCK), 0) < SSD_P
    return jnp.concatenate(
        [jnp.where(top, _row_bcast(ref, c, 2 * t), _row_bcast(ref, c, 2 * t + 1)).T for t in range(SSD_R // 2)],
        axis=1)


def _ssd_kernel(n_chunks, has_s0, emit_state, has_fill, *refs):
    it = iter(refs)
    z_ref, x_ref, b_ref, c_ref = (next(it) for _ in range(4))
    cumf_ref, cumb_ref, dtf_ref, dtb_ref, wf_ref, wb_ref, totf_ref, totb_ref, ecf_ref, ecb_ref = (
        next(it) for _ in range(10))
    cwx_ref, cwb_ref, cwc_ref, cbx_ref, cbb_ref, cbc_ref, d_ref, ng_ref = (next(it) for _ in range(8))
    s0_ref = next(it) if has_s0 else None
    if has_fill:
        next(it), next(it)
    o_ref, ss_ref = next(it), next(it)
    sfin_ref = next(it) if emit_state else None
    pad_ref, xc_ref, bc_ref, cc_ref, sfs_ref, sf_ref, sb_ref = (next(it) for _ in range(7))
    seq_len = n_chunks * BLOCK

    pad_ref[0:HALO, :] = jnp.zeros((HALO, GW + 2 * SSD_N), F32)
    pad_ref[HALO + seq_len:2 * HALO + seq_len, :] = jnp.zeros((HALO, GW + 2 * SSD_N), F32)

    def fill(c, carry):
        dst = pl.ds(pl.multiple_of(c * BLOCK, BLOCK) + HALO, BLOCK)
        r = _rows(c)
        pad_ref[dst, 0:GW] = x_ref[r, :].astype(F32)
        pad_ref[dst, GW:GW + SSD_N] = b_ref[r, :].astype(F32)
        pad_ref[dst, GW + SSD_N:GW + 2 * SSD_N] = c_ref[r, :].astype(F32)
        return carry

    lax.fori_loop(0, n_chunks, fill, 0)

    def conv(c):
        r = _rows(c)
        src = pl.ds(pl.multiple_of(c * BLOCK, BLOCK), BLOCK + 2 * HALO)
        for col0, width, cw_ref, cb_ref, dst in ((0, GW, cwx_ref, cbx_ref, xc_ref),
                                                 (GW, SSD_N, cwb_ref, cbb_ref, bc_ref),
                                                 (GW + SSD_N, SSD_N, cwc_ref, cbc_ref, cc_ref)):
            for t in range(width // LANES):
                tl = slice(t * LANES, (t + 1) * LANES)
                win = pad_ref[src, col0 + t * LANES:col0 + (t + 1) * LANES]
                acc = jnp.broadcast_to(cb_ref[:, tl], (BLOCK, LANES))
                for w in range(CONV_W):
                    off = HALO - CONV_W // 2 + w
                    acc = acc + win[off:off + BLOCK, :] * cw_ref[w:w + 1, tl]
                dst[r, tl] = _silu(acc).astype(dst.dtype)

    if has_s0:
        for r in range(SSD_R):
            sf_ref[:, r * SSD_P:(r + 1) * SSD_P] = s0_ref[0, r]
            sb_ref[:, r * SSD_P:(r + 1) * SSD_P] = s0_ref[1, r]
    else:
        sf_ref[...] = jnp.zeros_like(sf_ref)
        sb_ref[...] = jnp.zeros_like(sb_ref)

    def state_update(s_ref, c, w_ref, tot_ref, bm, xs):
        tot = _pair_tiles([tot_ref[c, r:r + 1, :] for r in range(SSD_R)])
        s_ref[...] = s_ref[...] * tot + _bdot_tn(bm, xs * _pair_cols(w_ref, c))

    def fwd_step(c):
        r = _rows(c)
        sfs_ref[c] = sf_ref[...].astype(BF16)
        state_update(sf_ref, c, wf_ref, totf_ref, bc_ref[r, :], xc_ref[r, :])

    def fwd(c, carry):
        fwd_step(c)
        conv(c + 1)
        return carry

    conv(0)
    lax.fori_loop(0, n_chunks - 1, fwd, 0)
    fwd_step(n_chunks - 1)
    if emit_state:
        for r in range(SSD_R):
            sfin_ref[0, r] = sf_ref[:, r * SSD_P:(r + 1) * SSD_P]

    ii = lax.broadcasted_iota(jnp.int32, (BLOCK, BLOCK), 0)
    jj = lax.broadcasted_iota(jnp.int32, (BLOCK, BLOCK), 1)
    causal = jj <= ii
    anti = ii <= jj
    low = lax.broadcasted_iota(jnp.int32, (BLOCK, LANES), 1) < SSD_P

    def bwd(t, carry):
        c = n_chunks - 1 - t
        r = _rows(c)
        cm, bm, xs = cc_ref[r, :], bc_ref[r, :], xc_ref[r, :]
        xb = xs.astype(BF16)
        sc = _bdot_nt(cm, bm)
        yf = _bdot(cm, sfs_ref[c])
        yb = _bdot(cm, sb_ref[...])
        tiles = []
        for t2 in range(SSD_R // 2):
            xt = xb[:, t2 * LANES:(t2 + 1) * LANES]
            acc = None
            for hh in range(2):
                hr = 2 * t2 + hh
                rf, rb = _row_bcast(cumf_ref, c, hr), _row_bcast(cumb_ref, c, hr)
                e = jnp.exp(jnp.where(anti, rf, rb).T - jnp.where(causal, rf, rb))
                e = e * jnp.where(causal, _row_bcast(dtf_ref, c, hr), _row_bcast(dtb_ref, c, hr))
                xh = jnp.where(low, xt, 0.0) if hh == 0 else jnp.where(low, 0.0, xt)
                part = _bdot(sc * e, xh)
                acc = part if acc is None else acc + part
            tiles.append(acc)
        y = (jnp.concatenate(tiles, axis=1) + _pair_cols(ecf_ref, c) * yf + _pair_cols(ecb_ref, c) * yb
             + d_ref[...] * xs)
        yg = y * _silu(z_ref[r, :].astype(F32))
        ss_ref[r, :] = jnp.broadcast_to(jnp.sum(yg * yg, axis=-1, keepdims=True), (BLOCK, LANES))
        o_ref[r, :] = (yg * ng_ref[...]).astype(o_ref.dtype)
        state_update(sb_ref, c, wb_ref, totb_ref, bm, xs)
        return carry

    lax.fori_loop(0, n_chunks, bwd, 0)
    if emit_state:
        for r in range(SSD_R):
            sfin_ref[1, r] = sb_ref[:, r * SSD_P:(r + 1) * SSD_P]


def _ssd_scan(zx, prep, conv_w, conv_b, d_exp, norm_gain, seq_len, n_seq, row_block0, s0=None, emit_state=False,
              fill=None):
    n_chunks = seq_len // BLOCK
    has_s0 = s0 is not None
    xcol0 = D_INNER // GW
    bcol0 = 2 * D_INNER // SSD_N
    ccol0 = bcol0 + SSD_GROUPS
    cwb0 = D_INNER // SSD_N

    def rowcol(width, col0):
        return pl.BlockSpec((seq_len, width), lambda b, g, col0=col0: (b + row_block0, col0 + g))

    def headrows(direction):
        return pl.BlockSpec((n_chunks, SSD_R, BLOCK),
                            lambda b, g, direction=direction: (b + row_block0, direction * SSD_GROUPS + g, 0))

    in_specs = [rowcol(GW, 0), rowcol(GW, xcol0), rowcol(SSD_N, bcol0), rowcol(SSD_N, ccol0)]
    args = [zx, zx, zx, zx]
    for arr in prep:
        in_specs += [headrows(0), headrows(1)]
        args += [arr, arr]
    in_specs += [pl.BlockSpec((CONV_W, GW), lambda b, g: (0, g)),
                 pl.BlockSpec((CONV_W, SSD_N), lambda b, g: (0, cwb0 + g)),
                 pl.BlockSpec((CONV_W, SSD_N), lambda b, g: (0, cwb0 + SSD_GROUPS + g)),
                 pl.BlockSpec((1, GW), lambda b, g: (0, g)),
                 pl.BlockSpec((1, SSD_N), lambda b, g: (0, cwb0 + g)),
                 pl.BlockSpec((1, SSD_N), lambda b, g: (0, cwb0 + SSD_GROUPS + g)),
                 pl.BlockSpec((1, GW), lambda b, g: (0, g)),
                 pl.BlockSpec((1, GW), lambda b, g: (0, g))]
    args += [conv_w, conv_w, conv_w, conv_b, conv_b, conv_b, d_exp, norm_gain.reshape(1, D_INNER)]
    state_spec = pl.BlockSpec((None, None, 2, SSD_R, SSD_N, SSD_P), lambda b, g: (b, 0, 0, g, 0, 0))
    if has_s0:
        in_specs.append(state_spec)
        args.append(s0)
    aliases = _fill_alias(fill, in_specs, args)
    out_specs = [pl.BlockSpec((seq_len, GW), lambda b, g: (b + row_block0, g)),
                 pl.BlockSpec((seq_len, LANES), lambda b, g: (b + row_block0, g))]
    out_shape = [jax.ShapeDtypeStruct((T_ALL, D_INNER), BF16),
                 jax.ShapeDtypeStruct((T_ALL, SSD_GROUPS * LANES), F32)]
    if emit_state:
        out_specs.append(state_spec)
        out_shape.append(jax.ShapeDtypeStruct((n_seq, 1, 2, SSD_HEADS, SSD_N, SSD_P), F32))
    return pl.pallas_call(
        functools.partial(_ssd_kernel, n_chunks, has_s0, emit_state, bool(fill)),
        grid=(n_seq, SSD_GROUPS),
        in_specs=in_specs,
        out_specs=out_specs,
        out_shape=out_shape,
        input_output_aliases=aliases,
        scratch_shapes=[pltpu.VMEM((seq_len + 2 * HALO, GW + 2 * SSD_N), F32),
                        pltpu.VMEM((seq_len, GW), F32), pltpu.VMEM((seq_len, SSD_N), BF16),
                        pltpu.VMEM((seq_len, SSD_N), BF16), pltpu.VMEM((n_chunks, SSD_N, GW), BF16),
                        pltpu.VMEM((SSD_N, GW), F32), pltpu.VMEM((SSD_N, GW), F32)],
        compiler_params=_cparams("arbitrary", "arbitrary"),
        name="ssd_scan_latent" if has_s0 else "ssd_scan_prompt",
    )(*args)


def _router_kernel(x_ref, g_ref, mod_ref, rw_ref, hn_ref, idx_ref, wgt_ref, hf_ref):
    _adaln_to(x_ref, g_ref, mod_ref, 3, 4, hf_ref)
    hf = hf_ref[...]
    h_hi = hf.astype(BF16)
    h_lo = (hf - h_hi.astype(F32)).astype(BF16)
    hn_ref[...] = h_hi
    rw = rw_ref[...]
    r_hi = rw.astype(BF16)
    r_lo = (rw - r_hi.astype(F32)).astype(BF16)
    t1 = jnp.dot(h_hi, jnp.concatenate([r_hi, r_lo], axis=1), preferred_element_type=F32)
    logits = t1[:, :LANES] + t1[:, LANES:] + jnp.dot(h_lo, r_hi, preferred_element_type=F32)
    lane = lax.broadcasted_iota(jnp.int32, logits.shape, 1)
    lg = jnp.where(lane < N_EXPERTS, logits, NEG_INF)
    m1 = jnp.max(lg, axis=-1, keepdims=True)
    i1 = jnp.min(jnp.where(lg == m1, lane, LANES), axis=-1, keepdims=True)
    lg2 = jnp.where(lane == i1, NEG_INF, lg)
    m2 = jnp.max(lg2, axis=-1, keepdims=True)
    i2 = jnp.min(jnp.where(lg2 == m2, lane, LANES), axis=-1, keepdims=True)
    e2 = jnp.exp(m2 - m1)
    w1 = 1.0 / (1.0 + e2)
    idx_ref[...] = jnp.where(lane == 0, i1, jnp.where(lane == 1, i2, 0))
    wgt_ref[...] = jnp.where(lane == 0, w1, jnp.where(lane == 1, e2 * w1, 0.0))


def _router(x, gain, modt, router_w):
    t, d = x.shape
    rw = jnp.pad(router_w, ((0, 0), (0, LANES - N_EXPERTS)))
    return pl.pallas_call(
        _router_kernel,
        grid=(t // ROW_TILE,),
        in_specs=[pl.BlockSpec((ROW_TILE, d), lambda i: (i, 0)),
                  pl.BlockSpec((1, d), lambda i: (0, 0)),
                  pl.BlockSpec((None, 8, d), lambda i: (i, 0, 0)),
                  pl.BlockSpec((d, LANES), lambda i: (0, 0))],
        out_specs=[pl.BlockSpec((ROW_TILE, d), lambda i: (i, 0)),
                   pl.BlockSpec((ROW_TILE, LANES), lambda i: (i, 0)),
                   pl.BlockSpec((ROW_TILE, LANES), lambda i: (i, 0))],
        out_shape=[jax.ShapeDtypeStruct((t, d), BF16), jax.ShapeDtypeStruct((t, LANES), jnp.int32),
                   jax.ShapeDtypeStruct((t, LANES), F32)],
        scratch_shapes=[pltpu.VMEM((ROW_TILE, d), F32)],
        compiler_params=_cparams("arbitrary"),
        name="moe_router",
    )(x, gain.reshape(1, d), modt, rw)


DOWN_ROWS = 512
DOWN_TN = 512


def _expert_changed(be_ref, blk, prev_blk, step):
    return jnp.logical_or(step == 0, be_ref[blk] != be_ref[prev_blk])


def _expert_up_kernel(be_ref, nu_ref, nv_ref, xs_ref, wg_ref, wu_ref, h_ref, wgb_ref, wub_ref):
    i = pl.program_id(1)
    valid = nv_ref[i]

    @pl.when(jnp.logical_and(valid > 0, _expert_changed(be_ref, i, jnp.maximum(i - 1, 0), i)))
    def _():
        wgb_ref[...] = wg_ref[...].astype(BF16)
        wub_ref[...] = wu_ref[...].astype(BF16)

    for s in range(MOE_ROWS // DOWN_ROWS):
        r = slice(s * DOWN_ROWS, (s + 1) * DOWN_ROWS)

        @pl.when(valid > s * DOWN_ROWS)
        def _():
            xs = xs_ref[r, :]
            h_ref[r, :] = (_silu(_bdot(xs, wgb_ref[...])) * _bdot(xs, wub_ref[...])).astype(h_ref.dtype)

        @pl.when(valid <= s * DOWN_ROWS)
        def _():
            h_ref[r, :] = jnp.zeros((DOWN_ROWS, h_ref.shape[1]), h_ref.dtype)


def _expert_down_kernel(be_ref, nu_ref, nv_ref, h_ref, wd_ref, o_ref, wdb_ref):
    i = pl.program_id(1)
    per = MOE_ROWS // DOWN_ROWS
    blk = i // per
    used = nv_ref[blk] > (i % per) * DOWN_ROWS

    @pl.when(jnp.logical_and(nv_ref[blk] > 0, _expert_changed(be_ref, blk, jnp.maximum(i - 1, 0) // per, i)))
    def _():
        wdb_ref[...] = wd_ref[...].astype(BF16)

    @pl.when(used)
    def _():
        o_ref[...] = _bdot(h_ref[...], wdb_ref[...]).astype(o_ref.dtype)

    @pl.when(jnp.logical_not(used))
    def _():
        o_ref[...] = jnp.zeros_like(o_ref)


def _experts(xs_sorted, block_e, n_used, n_valid, wg, wu, wd):
    cap, d = xs_sorted.shape
    ff = wg.shape[2]
    per = MOE_ROWS // DOWN_ROWS

    def expert_of(blk, be, nu):
        return be[jnp.minimum(blk, jnp.maximum(nu[0] - 1, 0))]

    h = pl.pallas_call(
        _expert_up_kernel,
        grid_spec=pltpu.PrefetchScalarGridSpec(
            num_scalar_prefetch=3,
            grid=(ff // FFN_TF, cap // MOE_ROWS),
            in_specs=[pl.BlockSpec((MOE_ROWS, d), lambda f, i, be, nu, nv: (i, 0)),
                      pl.BlockSpec((None, d, FFN_TF), lambda f, i, be, nu, nv: (expert_of(i, be, nu), 0, f)),
                      pl.BlockSpec((None, d, FFN_TF), lambda f, i, be, nu, nv: (expert_of(i, be, nu), 0, f))],
            out_specs=pl.BlockSpec((MOE_ROWS, FFN_TF), lambda f, i, be, nu, nv: (i, f)),
            scratch_shapes=[pltpu.VMEM((d, FFN_TF), BF16), pltpu.VMEM((d, FFN_TF), BF16)],
        ),
        out_shape=jax.ShapeDtypeStruct((cap, ff), BF16),
        compiler_params=_cparams("arbitrary", "arbitrary"),
        name="moe_expert_up",
    )(block_e, n_used, n_valid, xs_sorted, wg, wu)
    return pl.pallas_call(
        _expert_down_kernel,
        grid_spec=pltpu.PrefetchScalarGridSpec(
            num_scalar_prefetch=3,
            grid=(d // DOWN_TN, cap // DOWN_ROWS),
            in_specs=[pl.BlockSpec((DOWN_ROWS, ff), lambda n, i, be, nu, nv: (i, 0)),
                      pl.BlockSpec((None, ff, DOWN_TN),
                                   lambda n, i, be, nu, nv: (expert_of(i // per, be, nu), 0, n))],
            out_specs=pl.BlockSpec((DOWN_ROWS, DOWN_TN), lambda n, i, be, nu, nv: (i, n)),
            scratch_shapes=[pltpu.VMEM((ff, DOWN_TN), BF16)],
        ),
        out_shape=jax.ShapeDtypeStruct((cap, d), BF16),
        compiler_params=_cparams("arbitrary", "arbitrary"),
        name="moe_expert_down",
    )(block_e, n_used, n_valid, h, wd)


def _combine_kernel(x_ref, g0_ref, g1_ref, w_ref, mod_ref, o_ref):
    w = w_ref[...]
    y = w[:, 0:1] * g0_ref[...].astype(F32) + w[:, 1:2] * g1_ref[...].astype(F32)
    o_ref[...] = x_ref[...] + mod_ref[5:6, :] * y


COMB_ROWS = 512


def _combine(x, g0, g1, wgt, modt, row0, n_rows):
    d = x.shape[1]
    b0 = row0 // COMB_ROWS
    per = ROW_TILE // COMB_ROWS
    return pl.pallas_call(
        _combine_kernel,
        grid=(n_rows // COMB_ROWS,),
        in_specs=[pl.BlockSpec((COMB_ROWS, d), lambda i: (i + b0, 0)),
                  pl.BlockSpec((COMB_ROWS, d), lambda i: (i + b0, 0)),
                  pl.BlockSpec((COMB_ROWS, d), lambda i: (i + b0, 0)),
                  pl.BlockSpec((COMB_ROWS, LANES), lambda i: (i + b0, 0)),
                  pl.BlockSpec((None, 8, d), lambda i: ((i + b0) // per, 0, 0))],
        out_specs=pl.BlockSpec((COMB_ROWS, d), lambda i: (i, 0)),
        out_shape=jax.ShapeDtypeStruct((n_rows, d), F32),
        compiler_params=_cparams("arbitrary"),
        name="moe_combine",
    )(x, g0, g1, wgt, modt)


def _moe(x, gain, modt, router_w, wg, wu, wd):
    t, d = x.shape
    hn, idx, wgt = _router(x, gain, modt, router_w)
    top_idx = idx[:, :TOP_K]
    n_slots = t * TOP_K
    flat_e = top_idx.reshape(-1)
    onehot = (flat_e[:, None] == jnp.arange(N_EXPERTS, dtype=jnp.int32)[None, :]).astype(jnp.int32)
    incl = jnp.cumsum(onehot, axis=0)
    counts = incl[-1]
    rank = jnp.sum((incl - onehot) * onehot, axis=1)
    padded = (counts + MOE_ROWS - 1) // MOE_ROWS * MOE_ROWS
    pend = jnp.cumsum(padded)
    pstart = pend - padded
    dest = pstart[flat_e] + rank
    n_blocks = n_slots // MOE_ROWS + N_EXPERTS
    cap = n_blocks * MOE_ROWS
    row_tok = (jnp.arange(cap, dtype=jnp.int32) % t).at[dest].set(jnp.arange(n_slots, dtype=jnp.int32) // TOP_K)
    block_e = jnp.clip(jnp.searchsorted(pend, jnp.arange(n_blocks, dtype=jnp.int32) * MOE_ROWS, side='right'),
                       0, N_EXPERTS - 1).astype(jnp.int32)
    n_used = (pend[-1:] // MOE_ROWS).astype(jnp.int32)
    blk = jnp.arange(n_blocks, dtype=jnp.int32)
    n_valid = jnp.where(blk < n_used[0],
                        jnp.clip((pstart + counts)[block_e] - blk * MOE_ROWS, 0, MOE_ROWS), 0).astype(jnp.int32)
    out = _experts(hn[row_tok], block_e, n_used, n_valid, wg, wu, wd)
    dest = dest.reshape(t, TOP_K)
    g0, g1 = out[dest[:, 0]], out[dest[:, 1]]
    return (_combine(x, g0, g1, wgt, modt, 0, T_PROMPT), _combine(x, g0, g1, wgt, modt, T_PROMPT, T_SAMPLE))


def _rope_tables(n_tokens, dim):
    n_rows = n_tokens // GRID_W
    row = jnp.repeat(jnp.arange(n_rows), GRID_W).astype(F32)
    col = jnp.tile(jnp.arange(GRID_W), n_rows).astype(F32)
    n_freq = dim // 4
    inv = ROPE_BASE ** (-jnp.arange(n_freq, dtype=F32) / n_freq)
    ang = jnp.concatenate([row[:, None] * inv, col[:, None] * inv], axis=-1)
    return jnp.cos(ang), jnp.sin(ang)


def kernel(x_prompt, x_sample, state_ret, cache_k, cache_v, state_ssd, c, c_ctx, ada_w, ada_b, norm_mix, norm_ffn, ev_w_in, ev_w_out, ret_decay_logit, ret_norm, att_q_norm, att_k_norm, att_sink, ffn_w_gate, ffn_w_up, ffn_w_down, ssd_w_in, ssd_conv_w, ssd_conv_b, ssd_a_log, ssd_dt_bias, ssd_d, ssd_norm, ssd_w_out, moe_router, moe_w_gate, moe_w_up, moe_w_down):
    d = D_MODEL
    x = jnp.concatenate([x_prompt.reshape(T_PROMPT, d), x_sample.reshape(T_SAMPLE, d)], axis=0)

    cvecs = jnp.concatenate([c_ctx[None, :], c, jnp.zeros((MOD_ROWS - 1 - DEC_BATCH, d), F32)], axis=0)
    mods = _modulation(cvecs, ada_w, ada_b).reshape(2, MOD_ROWS, 6, d)
    tiles_per_seq = DEC_SEQ // ROW_TILE
    tile_row = jnp.concatenate([jnp.zeros((T_PROMPT // ROW_TILE,), jnp.int32),
                                1 + jnp.arange(T_SAMPLE // ROW_TILE, dtype=jnp.int32) // tiles_per_seq])
    modt = jnp.pad(mods[:, tile_row], ((0, 0), (0, 0), (0, 2), (0, 0)))

    ev_w_in_b, ev_w_out_b = ev_w_in[0].astype(BF16), ev_w_out[0].astype(BF16)
    ffn_wg_b, ffn_wu_b, ffn_wd_b = ffn_w_gate[0].astype(BF16), ffn_w_up[0].astype(BF16), ffn_w_down[0].astype(BF16)
    ssd_w_in_b, ssd_w_out_b = ssd_w_in[0].astype(BF16), ssd_w_out[0].astype(BF16)

    proj, = _adaln_matmul(x, norm_mix[0], modt[0], 0, 1, ev_w_in_b, tn=512, out_dtype=BF16, name="even_in_proj")
    lg = jax.nn.log_sigmoid(ret_decay_logit[0].astype(F32))
    cos_r, sin_r = _rope_tables(DEC_SEQ, RET_DK)
    cos_a, sin_a = _rope_tables(DEC_SEQ, ATT_HD)
    cos_a2 = jnp.concatenate([cos_a, cos_a], axis=-1)
    sin_a2 = jnp.concatenate([-sin_a, sin_a], axis=-1)
    mix_ret, new_state_ret = _retention(proj, lg, ret_norm[0], SEQ, BATCH, 0, emit_state=True)
    mix_ret, = _retention(proj, lg, ret_norm[0], DEC_SEQ, DEC_BATCH, T_PROMPT // DEC_SEQ,
                          ropes=(cos_r, sin_r), s0=state_ret, fill=(mix_ret,))
    mix_att, new_k, new_v = _attention_prompt(proj, att_sink[0], att_q_norm[0], att_k_norm[0])
    mix_att = _attention_latent(proj, cache_k[:, 0], cache_v[:, 0], att_sink[0], cos_a2, sin_a2,
                                att_q_norm[0], att_k_norm[0], mix_att)
    x = _proj_residual([mix_ret, mix_att], ev_w_out_b, x, modt[0], 2, tn=1024, name="even_out_proj")
    h = _ffn_gateup(x, norm_ffn[0], modt[0], ffn_wg_b, ffn_wu_b)
    x = _proj_residual([h], ffn_wd_b, x, modt[0], 5, tn=512, name="ffn_down")

    zx, dt_raw = _adaln_matmul(x, norm_mix[1], modt[1], 0, 1, ssd_w_in_b, tn=1024, n_out=SSD_ZX,
                               tail=2 * SSD_HEADS, out_dtype=BF16, name="ssd_in_proj")
    prep = _ssd_prep(dt_raw, ssd_dt_bias[0], ssd_a_log[0])
    d_exp = jnp.repeat(ssd_d[0], SSD_P)[None, :]
    conv_b = ssd_conv_b[0][None, :]
    yg, yss, new_state_ssd = _ssd_scan(zx, prep, ssd_conv_w[0], conv_b, d_exp, ssd_norm[0], SEQ, BATCH, 0,
                                       emit_state=True)
    yg, yss = _ssd_scan(zx, prep, ssd_conv_w[0], conv_b, d_exp, ssd_norm[0], DEC_SEQ, DEC_BATCH,
                        T_PROMPT // DEC_SEQ, s0=state_ssd, fill=(yg, yss))
    x = _proj_residual([yg], ssd_w_out_b, x, modt[1], 2, tn=512, row_ss=yss, name="ssd_out_proj")
    y_p, y_s = _moe(x, norm_ffn[1], modt[1], moe_router[0], moe_w_gate[0], moe_w_up[0], moe_w_down[0])

    y_prompt = y_p.reshape(BATCH, SEQ, d)
    y_sample = y_s.reshape(DEC_BATCH, DEC_SEQ, d)
    new_cache_k = new_k.reshape(BATCH, 1, SEQ, ATT_KV_HEADS, ATT_HD)
    new_cache_v = new_v.reshape(BATCH, 1, SEQ, ATT_KV_HEADS, ATT_HD)
    return (y_prompt, y_sample, new_state_ret, new_cache_k, new_cache_v, new_state_ssd)
```

```python
import functools

import jax
import jax.numpy as jnp
from jax import lax
from jax.experimental import pallas as pl
from jax.experimental.pallas import tpu as pltpu

F32 = jnp.float32
BF16 = jnp.bfloat16

D_MODEL = 2048
BATCH = 16
SEQ = 256
DEC_BATCH = 8
DEC_SEQ = 2048
PAST_LEN = 512
GRID_W = 64
BLOCK = 128
WINDOW = 128
EPS = 1e-6
ROPE_BASE = 10000.0
RET_HEADS = 4
RET_DK = 256
RET_DV = 256
ATT_HEADS = 8
ATT_KV_HEADS = 2
ATT_HD = 128
ATT_GROUP = ATT_HEADS // ATT_KV_HEADS
EVEN_IN = 5632
D_INNER = 2 * D_MODEL
SSD_P = 64
SSD_HEADS = D_INNER // SSD_P
SSD_N = 128
SSD_GROUPS = 8
SSD_R = SSD_HEADS // SSD_GROUPS
CONV_W = 5
CONV_CH = D_INNER + 2 * SSD_GROUPS * SSD_N
SSD_ZX = D_INNER + CONV_CH
D_FF = 5632
N_EXPERTS = 8
TOP_K = 2

T_PROMPT = BATCH * SEQ
T_SAMPLE = DEC_BATCH * DEC_SEQ
T_ALL = T_PROMPT + T_SAMPLE

LANES = 128
ROW_TILE = 1024
VMEM_LIMIT = 56 * 1024 * 1024
N_ROW_TILES = T_ALL // ROW_TILE
MOE_ROWS = 1024
NEG_INF = float("-inf")


def _cparams(*sem):
    return pltpu.CompilerParams(dimension_semantics=sem, vmem_limit_bytes=VMEM_LIMIT)


def _silu(x):
    return x * jax.nn.sigmoid(x)


def _bdot(a, b):
    return jnp.dot(a.astype(BF16), b.astype(BF16), preferred_element_type=F32)


def _bdot_nt(a, b):
    return lax.dot_general(a.astype(BF16), b.astype(BF16), (((1,), (1,)), ((), ())),
                           preferred_element_type=F32)


def _bdot_tn(a, b):
    return lax.dot_general(a.astype(BF16), b.astype(BF16), (((0,), (0,)), ((), ())),
                           preferred_element_type=F32)


def _rows(c, n=BLOCK):
    return pl.ds(pl.multiple_of(c * n, n), n)


MOD_ROWS = 16
MOD_TN = 1024


def _mod_kernel(c_ref, w_ref, b_ref, o_ref):
    o_ref[...] = _bdot(_silu(c_ref[...]), w_ref[...]) + b_ref[...]


def _modulation(cvecs, ada_w, ada_b):
    depth, d, n = ada_w.shape
    return pl.pallas_call(
        _mod_kernel,
        grid=(depth, n // MOD_TN),
        in_specs=[pl.BlockSpec((MOD_ROWS, d), lambda l, j: (0, 0)),
                  pl.BlockSpec((None, d, MOD_TN), lambda l, j: (l, 0, j)),
                  pl.BlockSpec((None, 1, MOD_TN), lambda l, j: (l, 0, j))],
        out_specs=pl.BlockSpec((None, MOD_ROWS, MOD_TN), lambda l, j: (l, 0, j)),
        out_shape=jax.ShapeDtypeStruct((depth, MOD_ROWS, n), F32),
        compiler_params=_cparams("arbitrary", "arbitrary"),
        name="modulation",
    )(cvecs, ada_w, ada_b.reshape(depth, 1, n))


ADALN_CHUNK = 64


def _adaln_to(x_ref, g_ref, mod_ref, shift_row, scale_row, hn_ref):
    mult = g_ref[...] * (1.0 + mod_ref[scale_row:scale_row + 1, :])
    shift = mod_ref[shift_row:shift_row + 1, :]

    def body(i, carry):
        r = _rows(i, ADALN_CHUNK)
        x = x_ref[r, :]
        ms = jnp.mean(x * x, axis=-1, keepdims=True)
        hn_ref[r, :] = (x * lax.rsqrt(ms + EPS) * mult + shift).astype(hn_ref.dtype)
        return carry

    lax.fori_loop(0, x_ref.shape[0] // ADALN_CHUNK, body, 0)


PROMPT_TILES = T_PROMPT // ROW_TILE


def _x_rows_specs(x, cols, col_of_j):
    if not isinstance(x, tuple):
        return [pl.BlockSpec((ROW_TILE, cols), lambda i, j: (i, col_of_j(j)))], [x]
    return ([pl.BlockSpec((ROW_TILE, cols),
                          lambda i, j: (jnp.minimum(i, PROMPT_TILES - 1), jnp.where(i < PROMPT_TILES, col_of_j(j), 0))),
             pl.BlockSpec((ROW_TILE, cols),
                          lambda i, j: (jnp.maximum(i - PROMPT_TILES, 0), jnp.where(i < PROMPT_TILES, 0, col_of_j(j))))],
            list(x))


def _for_x_rows(x_refs, fn):
    if len(x_refs) == 1:
        fn(x_refs[0])
        return
    in_prompt = pl.program_id(0) < PROMPT_TILES
    pl.when(in_prompt)(lambda: fn(x_refs[0]))
    pl.when(jnp.logical_not(in_prompt))(lambda: fn(x_refs[1]))


def _adaln_mm_kernel(shift_row, scale_row, has_tail, n_x, *refs):
    x_refs, (g_ref, mod_ref, w_ref), refs = refs[:n_x], refs[n_x:n_x + 3], refs[n_x + 3:]
    if has_tail:
        wt_ref, o_ref, ot_ref, hn_ref = refs
    else:
        o_ref, hn_ref = refs

    @pl.when(pl.program_id(1) == 0)
    def _():
        _for_x_rows(x_refs, lambda x_ref: _adaln_to(x_ref, g_ref, mod_ref, shift_row, scale_row, hn_ref))
        if has_tail:
            ot_ref[...] = _bdot(hn_ref[...], wt_ref[...])

    o_ref[...] = _bdot(hn_ref[...], w_ref[...]).astype(o_ref.dtype)


def _adaln_matmul(x, gain, modt, shift_row, scale_row, w, tn, n_out=None, tail=0, out_dtype=F32, name="adaln_mm"):
    t, d = T_ALL, w.shape[0]
    n_out = w.shape[1] if n_out is None else n_out
    in_specs, args = _x_rows_specs(x, d, lambda j: 0)
    n_x = len(args)
    in_specs += [pl.BlockSpec((1, d), lambda i, j: (0, 0)),
                 pl.BlockSpec((None, 8, d), lambda i, j: (i, 0, 0)),
                 pl.BlockSpec((d, tn), lambda i, j: (0, j))]
    args += [gain.reshape(1, d), modt, w]
    out_specs = [pl.BlockSpec((ROW_TILE, tn), lambda i, j: (i, j))]
    out_shape = [jax.ShapeDtypeStruct((t, n_out), out_dtype)]
    if tail:
        in_specs.append(pl.BlockSpec((d, tail), lambda i, j: (0, n_out // tail)))
        args.append(w)
        out_specs.append(pl.BlockSpec((ROW_TILE, tail), lambda i, j: (i, 0)))
        out_shape.append(jax.ShapeDtypeStruct((t, tail), F32))
    return pl.pallas_call(
        functools.partial(_adaln_mm_kernel, shift_row, scale_row, bool(tail), n_x),
        grid=(t // ROW_TILE, n_out // tn),
        in_specs=in_specs,
        out_specs=out_specs,
        out_shape=out_shape,
        scratch_shapes=[pltpu.VMEM((ROW_TILE, d), BF16)],
        compiler_params=_cparams("arbitrary", "arbitrary"),
        name=name,
    )(*args)


FFN_TF = 512


def _gateup_kernel(x_ref, g_ref, mod_ref, wg_ref, wu_ref, h_ref, hn_ref):
    @pl.when(pl.program_id(1) == 0)
    def _():
        _adaln_to(x_ref, g_ref, mod_ref, 3, 4, hn_ref)

    hn = hn_ref[...]
    h_ref[...] = (_silu(_bdot(hn, wg_ref[...])) * _bdot(hn, wu_ref[...])).astype(h_ref.dtype)


def _ffn_gateup(x, gain, modt, wg, wu):
    t, d = x.shape
    ff = wg.shape[1]
    return pl.pallas_call(
        _gateup_kernel,
        grid=(t // ROW_TILE, ff // FFN_TF),
        in_specs=[pl.BlockSpec((ROW_TILE, d), lambda i, f: (i, 0)),
                  pl.BlockSpec((1, d), lambda i, f: (0, 0)),
                  pl.BlockSpec((None, 8, d), lambda i, f: (i, 0, 0)),
                  pl.BlockSpec((d, FFN_TF), lambda i, f: (0, f)),
                  pl.BlockSpec((d, FFN_TF), lambda i, f: (0, f))],
        out_specs=pl.BlockSpec((ROW_TILE, FFN_TF), lambda i, f: (i, f)),
        out_shape=jax.ShapeDtypeStruct((t, ff), BF16),
        scratch_shapes=[pltpu.VMEM((ROW_TILE, d), BF16)],
        compiler_params=_cparams("arbitrary", "arbitrary"),
        name="ffn_gateup",
    )(x, gain.reshape(1, d), modt, wg, wu)


def _proj_res_kernel(n_a, n_x, gate_row, norm, *refs):
    a_refs = refs[:n_a]
    w_refs = refs[n_a:2 * n_a]
    x_refs, mod_ref = refs[2 * n_a:2 * n_a + n_x], refs[2 * n_a + n_x]
    pos = 2 * n_a + n_x + 1
    o_ref = refs[pos + 1] if norm else refs[pos]
    acc = _bdot(a_refs[0][...], w_refs[0][...])
    for k in range(1, n_a):
        acc = acc + _bdot(a_refs[k][...], w_refs[k][...])
    if norm:
        ss_ref = refs[pos]
        k_total = sum(a.shape[1] for a in a_refs)
        ss = ss_ref[:, 0:LANES]
        for k in range(1, ss_ref.shape[1] // LANES):
            ss = ss + ss_ref[:, k * LANES:(k + 1) * LANES]
        rs = lax.rsqrt(ss * (1.0 / k_total) + EPS)
        acc = acc * jnp.concatenate([rs] * (acc.shape[1] // LANES), axis=1)
    upd = mod_ref[gate_row:gate_row + 1, :] * acc

    def finish(x_ref):
        o_ref[...] = x_ref[...] + upd

    _for_x_rows(x_refs, finish)


def _proj_residual(a_list, w, x, modt, gate_row, tn, row_ss=None, name="proj_res"):
    t, d = T_ALL, w.shape[1]
    n_a = len(a_list)
    norm = row_ss is not None
    in_specs, args, k0 = [], [], 0
    for a in a_list:
        in_specs.append(pl.BlockSpec((ROW_TILE, a.shape[1]), lambda i, j: (i, 0)))
        args.append(a)
    for a in a_list:
        ka = a.shape[1]
        assert k0 % ka == 0
        in_specs.append(pl.BlockSpec((ka, tn), lambda i, j, kb=k0 // ka: (kb, j)))
        args.append(w)
        k0 += ka
    x_specs, x_args = _x_rows_specs(x, tn, lambda j: j)
    in_specs += x_specs + [pl.BlockSpec((None, 8, tn), lambda i, j: (i, 0, j))]
    args += x_args + [modt]
    if norm:
        in_specs.append(pl.BlockSpec((ROW_TILE, row_ss.shape[1]), lambda i, j: (i, 0)))
        args.append(row_ss)
    return pl.pallas_call(
        functools.partial(_proj_res_kernel, n_a, len(x_args), gate_row, norm),
        grid=(t // ROW_TILE, d // tn),
        in_specs=in_specs,
        out_specs=pl.BlockSpec((ROW_TILE, tn), lambda i, j: (i, j)),
        out_shape=jax.ShapeDtypeStruct((t, d), F32),
        compiler_params=_cparams("arbitrary", "arbitrary"),
        name=name,
    )(*args)


def _ret_kernel(n_chunks, rope, has_s0, emit_state, has_fill, lg_ref, *refs):
    it = iter(refs)
    q_ref, k_ref, v_ref, gt_ref = next(it), next(it), next(it), next(it)
    cos_ref, sin_ref = (next(it), next(it)) if rope else (None, None)
    s0_ref = next(it) if has_s0 else None
    gain_ref = next(it)
    if has_fill:
        next(it)
    o_ref = next(it)
    sfin_ref = next(it) if emit_state else None
    qs_ref, ks_ref, sfs_ref, dm_ref, dec_ref, sf_ref, sb_ref = (next(it) for _ in range(7))

    h = pl.program_id(1)
    lgf = lg_ref[0, h]
    lgb = lg_ref[1, h]
    half = RET_DK // 2

    def prep(c, carry):
        r = _rows(c)
        q = q_ref[r, :].astype(F32)
        k = k_ref[r, :].astype(F32) * (RET_DK ** -0.5)
        if rope:
            cs, sn = cos_ref[r, :], sin_ref[r, :]
            for src, dst in ((q, qs_ref), (k, ks_ref)):
                x1, x2 = src[:, :half], src[:, half:]
                dst[r, :half] = (x1 * cs - x2 * sn).astype(BF16)
                dst[r, half:] = (x1 * sn + x2 * cs).astype(BF16)
        else:
            qs_ref[r, :] = q.astype(BF16)
            ks_ref[r, :] = k.astype(BF16)
        return carry

    lax.fori_loop(0, n_chunks, prep, 0)

    ii = lax.broadcasted_iota(jnp.int32, (BLOCK, BLOCK), 0)
    jj = lax.broadcasted_iota(jnp.int32, (BLOCK, BLOCK), 1)
    diff = (ii - jj).astype(F32)
    dm_ref[...] = jnp.exp(jnp.where(jj <= ii, diff * lgf, -diff * lgb))
    pos = lax.broadcasted_iota(jnp.int32, (BLOCK, RET_DV), 0).astype(F32)
    dec_ref[0] = jnp.exp((pos + 1.0) * lgf)
    dec_ref[1] = jnp.exp((BLOCK - pos) * lgb)
    dec_ref[2] = jnp.exp((BLOCK - 1.0 - pos) * lgf)
    dec_ref[3] = jnp.exp(pos * lgb)
    tot_f = jnp.exp(jnp.full((1, RET_DV), BLOCK * lgf, F32))
    tot_b = jnp.exp(jnp.full((1, RET_DV), BLOCK * lgb, F32))

    if has_s0:
        sf_ref[...] = s0_ref[0]
        sb_ref[...] = s0_ref[1]
    else:
        sf_ref[...] = jnp.zeros_like(sf_ref)
        sb_ref[...] = jnp.zeros_like(sb_ref)

    def fwd(c, carry):
        r = _rows(c)
        sfs_ref[c] = sf_ref[...].astype(BF16)
        kd = ks_ref[r, :].astype(F32) * dec_ref[2]
        sf_ref[...] = sf_ref[...] * tot_f + _bdot_tn(kd, v_ref[r, :])
        return carry

    lax.fori_loop(0, n_chunks, fwd, 0, unroll=2)
    if emit_state:
        sfin_ref[0] = sf_ref[...]

    def bwd(t, carry):
        c = n_chunks - 1 - t
        r = _rows(c)
        q = qs_ref[r, :]
        k = ks_ref[r, :]
        v = v_ref[r, :].astype(BF16)
        p = _bdot_nt(q, k) * dm_ref[...]
        o = _bdot(p, v)
        o = o + _bdot(q, sfs_ref[c]) * dec_ref[0]
        o = o + _bdot(q, sb_ref[...]) * dec_ref[1]
        ms = jnp.mean(o * o, axis=-1, keepdims=True)
        y = o * lax.rsqrt(ms + EPS) * gain_ref[...]
        o_ref[r, :] = (y * _silu(gt_ref[r, :].astype(F32))).astype(o_ref.dtype)
        kd = k.astype(F32) * dec_ref[3]
        sb_ref[...] = sb_ref[...] * tot_b + _bdot_tn(kd, v)
        return carry

    lax.fori_loop(0, n_chunks, bwd, 0, unroll=2)
    if emit_state:
        sfin_ref[1] = sb_ref[...]


def _fill_alias(fill, in_specs, args):
    aliases = {}
    for k, arr in enumerate(fill or ()):
        in_specs.append(pl.BlockSpec(memory_space=pl.ANY))
        args.append(arr)
        aliases[len(args) - 1] = k
    return aliases


def _retention(proj, lg, ret_norm, seq_len, n_seq, row_block0, ropes=None, s0=None, emit_state=False, fill=None):
    n_chunks = seq_len // BLOCK
    rope, has_s0 = ropes is not None, s0 is not None

    def col(cb):
        return pl.BlockSpec((seq_len, RET_DK), lambda b, h, cb=cb: (b + row_block0, cb * RET_HEADS + h))

    in_specs = [pl.BlockSpec(memory_space=pltpu.SMEM), col(0), col(1), col(2), col(3)]
    args = [lg, proj, proj, proj, proj]
    if rope:
        in_specs += [pl.BlockSpec((seq_len, RET_DK // 2), lambda b, h: (0, 0))] * 2
        args += list(ropes)
    if has_s0:
        in_specs.append(pl.BlockSpec((None, None, 2, None, RET_DK, RET_DV), lambda b, h: (b, 0, 0, h, 0, 0)))
        args.append(s0)
    in_specs.append(pl.BlockSpec((1, RET_DV), lambda b, h: (0, h)))
    args.append(ret_norm.reshape(1, RET_HEADS * RET_DV))
    aliases = _fill_alias(fill, in_specs, args)
    out_specs = [pl.BlockSpec((seq_len, RET_DV), lambda b, h: (b + row_block0, h))]
    out_shape = [jax.ShapeDtypeStruct((T_ALL, RET_HEADS * RET_DV), BF16)]
    if emit_state:
        out_specs.append(pl.BlockSpec((None, None, 2, None, RET_DK, RET_DV), lambda b, h: (b, 0, 0, h, 0, 0)))
        out_shape.append(jax.ShapeDtypeStruct((n_seq, 1, 2, RET_HEADS, RET_DK, RET_DV), F32))
    return pl.pallas_call(
        functools.partial(_ret_kernel, n_chunks, rope, has_s0, emit_state, bool(fill)),
        grid=(n_seq, RET_HEADS),
        in_specs=in_specs,
        out_specs=out_specs,
        out_shape=out_shape,
        input_output_aliases=aliases,
        scratch_shapes=[pltpu.VMEM((seq_len, RET_DK), BF16), pltpu.VMEM((seq_len, RET_DK), BF16),
                        pltpu.VMEM((n_chunks, RET_DK, RET_DV), BF16), pltpu.VMEM((BLOCK, BLOCK), F32),
                        pltpu.VMEM((4, BLOCK, RET_DV), F32), pltpu.VMEM((RET_DK, RET_DV), F32),
                        pltpu.VMEM((RET_DK, RET_DV), F32)],
        compiler_params=_cparams("arbitrary", "arbitrary"),
        name="retention_latent" if rope else "retention_prompt",
    )(*args)


def _head_norm(x, gain):
    x = x.astype(F32)
    return x * lax.rsqrt(jnp.mean(x * x, axis=-1, keepdims=True) + EPS) * gain


def _rope_full(x, cs, sn):
    return x * cs + pltpu.roll(x, ATT_HD // 2, 1) * sn


ATT_SPLIT = 2


def _sink_column(sink_ref, head0, n_heads, rows_per_head):
    n = n_heads * rows_per_head
    head = lax.broadcasted_iota(jnp.int32, (n, 1), 0) // rows_per_head
    col = jnp.full((n, 1), sink_ref[head0], F32)
    for g in range(1, n_heads):
        col = jnp.where(head == g, sink_ref[head0 + g], col)
    return col


def _att_latent_kernel(sink_ref, q_ref, k_ref, v_ref, ck_ref, cv_ref, cos_ref, sin_ref, qn_ref, kn_ref,
                       fill_ref, o_ref, kp_ref, vp_ref, ckp_ref, cvp_ref):
    del fill_ref
    kh, qb = pl.program_id(1), pl.program_id(2)
    n_chunks = DEC_SEQ // BLOCK
    loc = 3 * BLOCK

    @pl.when(qb == 0)
    def _():
        def prep(c, carry):
            r = _rows(c)
            kp_ref[r, :] = _rope_full(_head_norm(k_ref[r, :], kn_ref[...]), cos_ref[r, :], sin_ref[r, :]).astype(BF16)
            vp_ref[r, :] = v_ref[r, :].astype(BF16)
            return carry

        lax.fori_loop(0, n_chunks, prep, 0)
        ckp_ref[...] = ck_ref[...].astype(BF16)
        cvp_ref[...] = cv_ref[...].astype(BF16)

    rq = _rows(qb)
    cs, sn = cos_ref[rq, :], sin_ref[rq, :]
    qg = qn_ref[...] * (ATT_HD ** -0.5)
    start = pl.multiple_of(jnp.clip((qb - 1) * BLOCK, 0, DEC_SEQ - loc), BLOCK)
    kl, vl = kp_ref[pl.ds(start, loc), :], vp_ref[pl.ds(start, loc), :]
    qpos = qb * BLOCK + (lax.broadcasted_iota(jnp.int32, (ATT_SPLIT * BLOCK, loc), 0) & (BLOCK - 1))
    kpos = start + lax.broadcasted_iota(jnp.int32, (ATT_SPLIT * BLOCK, loc), 1)
    in_window = jnp.abs(qpos - kpos) <= WINDOW
    for g0 in range(0, ATT_GROUP, ATT_SPLIT):
        q = jnp.concatenate(
            [_rope_full(_head_norm(q_ref[:, g * ATT_HD:(g + 1) * ATT_HD], qg), cs, sn).astype(BF16)
             for g in range(g0, g0 + ATT_SPLIT)], axis=0)
        s_loc = jnp.where(in_window, _bdot_nt(q, kl), NEG_INF)
        s_ctx = _bdot_nt(q, ckp_ref[...])
        sink = _sink_column(sink_ref, kh * ATT_GROUP + g0, ATT_SPLIT, BLOCK)
        m = jnp.maximum(jnp.maximum(jnp.max(s_loc, axis=-1, keepdims=True),
                                    jnp.max(s_ctx, axis=-1, keepdims=True)), sink)
        p_loc = jnp.exp(s_loc - m)
        p_ctx = jnp.exp(s_ctx - m)
        den = jnp.sum(p_loc, axis=-1, keepdims=True) + jnp.sum(p_ctx, axis=-1, keepdims=True) + jnp.exp(sink - m)
        o = (_bdot(p_loc, vl) + _bdot(p_ctx, cvp_ref[...])) / den
        for g in range(ATT_SPLIT):
            o_ref[:, (g0 + g) * ATT_HD:(g0 + g + 1) * ATT_HD] = o[g * BLOCK:(g + 1) * BLOCK, :].astype(o_ref.dtype)


def _attention_latent(proj, cache_k, cache_v, sink, cos_a, sin_a, q_norm, k_norm, fill):
    nqb = DEC_SEQ // BLOCK
    rb0 = T_PROMPT // DEC_SEQ
    qcol0 = 4 * RET_HEADS * RET_DK // (ATT_GROUP * ATT_HD)
    kcol0 = (4 * RET_HEADS * RET_DK + ATT_HEADS * ATT_HD) // ATT_HD
    vcol0 = kcol0 + ATT_KV_HEADS
    ck = cache_k.reshape(DEC_BATCH, PAST_LEN, ATT_KV_HEADS * ATT_HD)
    cv = cache_v.reshape(DEC_BATCH, PAST_LEN, ATT_KV_HEADS * ATT_HD)
    return pl.pallas_call(
        _att_latent_kernel,
        grid=(DEC_BATCH, ATT_KV_HEADS, nqb),
        in_specs=[pl.BlockSpec(memory_space=pltpu.SMEM),
                  pl.BlockSpec((BLOCK, ATT_GROUP * ATT_HD),
                               lambda b, kh, qb: (T_PROMPT // BLOCK + b * nqb + qb, qcol0 + kh)),
                  pl.BlockSpec((DEC_SEQ, ATT_HD), lambda b, kh, qb: (rb0 + b, kcol0 + kh)),
                  pl.BlockSpec((DEC_SEQ, ATT_HD), lambda b, kh, qb: (rb0 + b, vcol0 + kh)),
                  pl.BlockSpec((None, PAST_LEN, ATT_HD), lambda b, kh, qb: (b, 0, kh)),
                  pl.BlockSpec((None, PAST_LEN, ATT_HD), lambda b, kh, qb: (b, 0, kh)),
                  pl.BlockSpec((DEC_SEQ, ATT_HD), lambda b, kh, qb: (0, 0)),
                  pl.BlockSpec((DEC_SEQ, ATT_HD), lambda b, kh, qb: (0, 0)),
                  pl.BlockSpec((1, ATT_HD), lambda b, kh, qb: (0, 0)),
                  pl.BlockSpec((1, ATT_HD), lambda b, kh, qb: (0, 0)),
                  pl.BlockSpec(memory_space=pl.ANY)],
        out_specs=pl.BlockSpec((BLOCK, ATT_GROUP * ATT_HD),
                               lambda b, kh, qb: (T_PROMPT // BLOCK + b * nqb + qb, kh)),
        out_shape=jax.ShapeDtypeStruct((T_ALL, ATT_HEADS * ATT_HD), BF16),
        input_output_aliases={10: 0},
        scratch_shapes=[pltpu.VMEM((DEC_SEQ, ATT_HD), BF16), pltpu.VMEM((DEC_SEQ, ATT_HD), BF16),
                        pltpu.VMEM((PAST_LEN, ATT_HD), BF16), pltpu.VMEM((PAST_LEN, ATT_HD), BF16)],
        compiler_params=_cparams("arbitrary", "arbitrary", "arbitrary"),
        name="attention_latent",
    )(sink, proj, proj, proj, ck, cv, cos_a, sin_a, q_norm.reshape(1, ATT_HD), k_norm.reshape(1, ATT_HD), fill)


def _att_prompt_kernel(sink_ref, q_ref, k_ref, v_ref, qn_ref, kn_ref, o_ref, nk_ref, nv_ref):
    kh = pl.program_id(1)
    kn = _head_norm(k_ref[...], kn_ref[...])
    v = v_ref[...]
    nk_ref[...] = kn
    nv_ref[...] = v.astype(F32)
    qg = qn_ref[...] * (ATT_HD ** -0.5)
    q = jnp.concatenate([_head_norm(q_ref[:, g * ATT_HD:(g + 1) * ATT_HD], qg).astype(BF16)
                         for g in range(ATT_GROUP)], axis=0)
    s = _bdot_nt(q, kn)
    sink = _sink_column(sink_ref, kh * ATT_GROUP, ATT_GROUP, SEQ)
    m = jnp.maximum(jnp.max(s, axis=-1, keepdims=True), sink)
    p = jnp.exp(s - m)
    den = jnp.sum(p, axis=-1, keepdims=True) + jnp.exp(sink - m)
    o = _bdot(p, v) / den
    for g in range(ATT_GROUP):
        o_ref[:, g * ATT_HD:(g + 1) * ATT_HD] = o[g * SEQ:(g + 1) * SEQ, :].astype(o_ref.dtype)


def _attention_prompt(proj, sink, q_norm, k_norm):
    qcol0 = 4 * RET_HEADS * RET_DK // (ATT_GROUP * ATT_HD)
    kcol0 = (4 * RET_HEADS * RET_DK + ATT_HEADS * ATT_HD) // ATT_HD
    vcol0 = kcol0 + ATT_KV_HEADS
    kv_spec = pl.BlockSpec((None, SEQ, ATT_HD), lambda b, kh: (b, 0, kh))
    kv_shape = jax.ShapeDtypeStruct((BATCH, SEQ, ATT_KV_HEADS * ATT_HD), F32)
    return pl.pallas_call(
        _att_prompt_kernel,
        grid=(BATCH, ATT_KV_HEADS),
        in_specs=[pl.BlockSpec(memory_space=pltpu.SMEM),
                  pl.BlockSpec((SEQ, ATT_GROUP * ATT_HD), lambda b, kh: (b, qcol0 + kh)),
                  pl.BlockSpec((SEQ, ATT_HD), lambda b, kh: (b, kcol0 + kh)),
                  pl.BlockSpec((SEQ, ATT_HD), lambda b, kh: (b, vcol0 + kh)),
                  pl.BlockSpec((1, ATT_HD), lambda b, kh: (0, 0)),
                  pl.BlockSpec((1, ATT_HD), lambda b, kh: (0, 0))],
        out_specs=[pl.BlockSpec((SEQ, ATT_GROUP * ATT_HD), lambda b, kh: (b, kh)), kv_spec, kv_spec],
        out_shape=[jax.ShapeDtypeStruct((T_ALL, ATT_HEADS * ATT_HD), BF16), kv_shape, kv_shape],
        compiler_params=_cparams("arbitrary", "arbitrary"),
        name="attention_prompt",
    )(sink, proj, proj, proj, q_norm.reshape(1, ATT_HD), k_norm.reshape(1, ATT_HD))


def _split_dot(m01, a):
    hi = a.astype(BF16)
    r1 = a - hi.astype(F32)
    mid = r1.astype(BF16)
    lo = (r1 - mid.astype(F32)).astype(BF16)
    return (jnp.dot(m01, hi, preferred_element_type=F32) + jnp.dot(m01, mid, preferred_element_type=F32)
            + jnp.dot(m01, lo, preferred_element_type=F32))


PREP_CHUNKS = ROW_TILE // BLOCK


def _ssd_prep_kernel(raw_ref, bias_ref, alog_ref, cum_ref, dt_ref, w_ref, tot_ref, ecum_ref):
    ii = lax.broadcasted_iota(jnp.int32, (BLOCK, BLOCK), 0)
    jj = lax.broadcasted_iota(jnp.int32, (BLOCK, BLOCK), 1)
    lower = jnp.where(jj <= ii, 1.0, 0.0).astype(BF16)
    upper = jnp.where(jj >= ii, 1.0, 0.0).astype(BF16)
    fwd_lane = lax.broadcasted_iota(jnp.int32, (BLOCK, LANES), 1) < SSD_HEADS
    neg_a = -jnp.exp(alog_ref[...])

    def chunk(k, carry):
        x = raw_ref[_rows(k), :] + bias_ref[...]
        dt = jnp.maximum(x, 0.0) + jnp.log1p(jnp.exp(-jnp.abs(x)))
        a = dt * neg_a
        incl = _split_dot(lower, a)
        rincl = _split_dot(upper, a)
        cum = jnp.where(fwd_lane, incl, rincl)
        tot = jnp.where(fwd_lane[:1], incl[BLOCK - 1:BLOCK, :], rincl[0:1, :])
        cum_ref[k] = cum.T
        dt_ref[k] = dt.T
        w_ref[k] = (dt * jnp.exp(tot - cum)).T
        tot_ref[k] = jnp.broadcast_to(jnp.exp(tot), (BLOCK, LANES)).T
        ecum_ref[k] = jnp.exp(cum).T
        return carry

    lax.fori_loop(0, PREP_CHUNKS, chunk, 0)


def _ssd_prep(dt_raw, dt_bias, a_log):
    nc = T_ALL // BLOCK
    spec = pl.BlockSpec((PREP_CHUNKS, 2 * SSD_HEADS, BLOCK), lambda c: (c, 0, 0))
    shape = jax.ShapeDtypeStruct((nc, 2 * SSD_HEADS, BLOCK), F32)
    return pl.pallas_call(
        _ssd_prep_kernel,
        grid=(nc // PREP_CHUNKS,),
        in_specs=[pl.BlockSpec((ROW_TILE, 2 * SSD_HEADS), lambda c: (c, 0)),
                  pl.BlockSpec((1, 2 * SSD_HEADS), lambda c: (0, 0)),
                  pl.BlockSpec((1, 2 * SSD_HEADS), lambda c: (0, 0))],
        out_specs=[spec] * 5,
        out_shape=[shape] * 5,
        compiler_params=_cparams("arbitrary"),
        name="ssd_prep",
    )(dt_raw, dt_bias.reshape(1, 2 * SSD_HEADS), a_log.reshape(1, 2 * SSD_HEADS))


GW = SSD_R * SSD_P
HALO = 8


def _pair_tiles(per_head):
    low = lax.broadcasted_iota(jnp.int32, per_head[0].shape, 1) < SSD_P
    return jnp.concatenate([jnp.where(low, per_head[2 * t], per_head[2 * t + 1]) for t in range(SSD_R // 2)],
                           axis=1)


def _row_bcast(ref, c, r):
    return jnp.broadcast_to(ref[c, r:r + 1, :], (BLOCK, BLOCK))


def _pair_cols(ref, c):
    top = lax.broadcasted_iota(jnp.int32, (BLOCK, BLOCK), 0) < SSD_P
    return jnp.concatenate(
        [jnp.where(top, _row_bcast(ref, c, 2 * t), _row_bcast(ref, c, 2 * t + 1)).T for t in range(SSD_R // 2)],
        axis=1)


def _ssd_kernel(n_chunks, has_s0, emit_state, has_fill, *refs):
    it = iter(refs)
    z_ref, x_ref, b_ref, c_ref = (next(it) for _ in range(4))
    cumf_ref, cumb_ref, dtf_ref, dtb_ref, wf_ref, wb_ref, totf_ref, totb_ref, ecf_ref, ecb_ref = (
        next(it) for _ in range(10))
    cwx_ref, cwb_ref, cwc_ref, cbx_ref, cbb_ref, cbc_ref, d_ref, ng_ref = (next(it) for _ in range(8))
    s0_ref = next(it) if has_s0 else None
    if has_fill:
        next(it), next(it)
    o_ref, ss_ref = next(it), next(it)
    sfin_ref = next(it) if emit_state else None
    pad_ref, xc_ref, bc_ref, cc_ref, sfs_ref, sf_ref, sb_ref = (next(it) for _ in range(7))
    seq_len = n_chunks * BLOCK

    pad_ref[0:HALO, :] = jnp.zeros((HALO, GW + 2 * SSD_N), F32)
    pad_ref[HALO + seq_len:2 * HALO + seq_len, :] = jnp.zeros((HALO, GW + 2 * SSD_N), F32)

    def fill(c, carry):
        dst = pl.ds(pl.multiple_of(c * BLOCK, BLOCK) + HALO, BLOCK)
        r = _rows(c)
        pad_ref[dst, 0:GW] = x_ref[r, :].astype(F32)
        pad_ref[dst, GW:GW + SSD_N] = b_ref[r, :].astype(F32)
        pad_ref[dst, GW + SSD_N:GW + 2 * SSD_N] = c_ref[r, :].astype(F32)
        return carry

    lax.fori_loop(0, n_chunks, fill, 0)

    def conv(c):
        r = _rows(c)
        src = pl.ds(pl.multiple_of(c * BLOCK, BLOCK), BLOCK + 2 * HALO)
        for col0, width, cw_ref, cb_ref, dst in ((0, GW, cwx_ref, cbx_ref, xc_ref),
                                                 (GW, SSD_N, cwb_ref, cbb_ref, bc_ref),
                                                 (GW + SSD_N, SSD_N, cwc_ref, cbc_ref, cc_ref)):
            for t in range(width // LANES):
                tl = slice(t * LANES, (t + 1) * LANES)
                win = pad_ref[src, col0 + t * LANES:col0 + (t + 1) * LANES]
                acc = jnp.broadcast_to(cb_ref[:, tl], (BLOCK, LANES))
                for w in range(CONV_W):
                    off = HALO - CONV_W // 2 + w
                    acc = acc + win[off:off + BLOCK, :] * cw_ref[w:w + 1, tl]
                dst[r, tl] = _silu(acc).astype(dst.dtype)

    if has_s0:
        for r in range(SSD_R):
            sf_ref[:, r * SSD_P:(r + 1) * SSD_P] = s0_ref[0, r]
            sb_ref[:, r * SSD_P:(r + 1) * SSD_P] = s0_ref[1, r]
    else:
        sf_ref[...] = jnp.zeros_like(sf_ref)
        sb_ref[...] = jnp.zeros_like(sb_ref)

    def state_update(s_ref, c, w_ref, tot_ref, bm, xs):
        tot = _pair_tiles([tot_ref[c, r:r + 1, :] for r in range(SSD_R)])
        s_ref[...] = s_ref[...] * tot + _bdot_tn(bm, xs * _pair_cols(w_ref, c))

    def fwd_step(c):
        r = _rows(c)
        sfs_ref[c] = sf_ref[...].astype(BF16)
        state_update(sf_ref, c, wf_ref, totf_ref, bc_ref[r, :], xc_ref[r, :])

    def fwd(c, carry):
        fwd_step(c)
        conv(c + 1)
        return carry

    conv(0)
    lax.fori_loop(0, n_chunks - 1, fwd, 0)
    fwd_step(n_chunks - 1)
    if emit_state:
        for r in range(SSD_R):
            sfin_ref[0, r] = sf_ref[:, r * SSD_P:(r + 1) * SSD_P]

    ii = lax.broadcasted_iota(jnp.int32, (BLOCK, BLOCK), 0)
    jj = lax.broadcasted_iota(jnp.int32, (BLOCK, BLOCK), 1)
    causal = jj <= ii
    anti = ii <= jj
    low = lax.broadcasted_iota(jnp.int32, (BLOCK, LANES), 1) < SSD_P

    def bwd(t, carry):
        c = n_chunks - 1 - t
        r = _rows(c)
        cm, bm, xs = cc_ref[r, :], bc_ref[r, :], xc_ref[r, :]
        xb = xs.astype(BF16)
        sc = _bdot_nt(cm, bm)
        yf = _bdot(cm, sfs_ref[c])
        yb = _bdot(cm, sb_ref[...])
        tiles = []
        for t2 in range(SSD_R // 2):
            xt = xb[:, t2 * LANES:(t2 + 1) * LANES]
            acc = None
            for hh in range(2):
                hr = 2 * t2 + hh
                rf, rb = _row_bcast(cumf_ref, c, hr), _row_bcast(cumb_ref, c, hr)
                e = jnp.exp(jnp.where(anti, rf, rb).T - jnp.where(causal, rf, rb))
                e = e * jnp.where(causal, _row_bcast(dtf_ref, c, hr), _row_bcast(dtb_ref, c, hr))
                xh = jnp.where(low, xt, 0.0) if hh == 0 else jnp.where(low, 0.0, xt)
                part = _bdot(sc * e, xh)
                acc = part if acc is None else acc + part
            tiles.append(acc)
        y = (jnp.concatenate(tiles, axis=1) + _pair_cols(ecf_ref, c) * yf + _pair_cols(ecb_ref, c) * yb
             + d_ref[...] * xs)
        yg = y * _silu(z_ref[r, :].astype(F32))
        ss_ref[r, :] = jnp.broadcast_to(jnp.sum(yg * yg, axis=-1, keepdims=True), (BLOCK, LANES))
        o_ref[r, :] = (yg * ng_ref[...]).astype(o_ref.dtype)
        state_update(sb_ref, c, wb_ref, totb_ref, bm, xs)
        return carry

    lax.fori_loop(0, n_chunks, bwd, 0, unroll=2)
    if emit_state:
        for r in range(SSD_R):
            sfin_ref[1, r] = sb_ref[:, r * SSD_P:(r + 1) * SSD_P]


def _ssd_scan(zx, prep, conv_w, conv_b, d_exp, norm_gain, seq_len, n_seq, row_block0, s0=None, emit_state=False,
              fill=None):
    n_chunks = seq_len // BLOCK
    has_s0 = s0 is not None
    xcol0 = D_INNER // GW
    bcol0 = 2 * D_INNER // SSD_N
    ccol0 = bcol0 + SSD_GROUPS
    cwb0 = D_INNER // SSD_N

    def rowcol(width, col0):
        return pl.BlockSpec((seq_len, width), lambda b, g, col0=col0: (b + row_block0, col0 + g))

    def headrows(direction):
        return pl.BlockSpec((n_chunks, SSD_R, BLOCK),
                            lambda b, g, direction=direction: (b + row_block0, direction * SSD_GROUPS + g, 0))

    in_specs = [rowcol(GW, 0), rowcol(GW, xcol0), rowcol(SSD_N, bcol0), rowcol(SSD_N, ccol0)]
    args = [zx, zx, zx, zx]
    for arr in prep:
        in_specs += [headrows(0), headrows(1)]
        args += [arr, arr]
    in_specs += [pl.BlockSpec((CONV_W, GW), lambda b, g: (0, g)),
                 pl.BlockSpec((CONV_W, SSD_N), lambda b, g: (0, cwb0 + g)),
                 pl.BlockSpec((CONV_W, SSD_N), lambda b, g: (0, cwb0 + SSD_GROUPS + g)),
                 pl.BlockSpec((1, GW), lambda b, g: (0, g)),
                 pl.BlockSpec((1, SSD_N), lambda b, g: (0, cwb0 + g)),
                 pl.BlockSpec((1, SSD_N), lambda b, g: (0, cwb0 + SSD_GROUPS + g)),
                 pl.BlockSpec((1, GW), lambda b, g: (0, g)),
                 pl.BlockSpec((1, GW), lambda b, g: (0, g))]
    args += [conv_w, conv_w, conv_w, conv_b, conv_b, conv_b, d_exp, norm_gain.reshape(1, D_INNER)]
    state_spec = pl.BlockSpec((None, None, 2, SSD_R, SSD_N, SSD_P), lambda b, g: (b, 0, 0, g, 0, 0))
    if has_s0:
        in_specs.append(state_spec)
        args.append(s0)
    aliases = _fill_alias(fill, in_specs, args)
    out_specs = [pl.BlockSpec((seq_len, GW), lambda b, g: (b + row_block0, g)),
                 pl.BlockSpec((seq_len, LANES), lambda b, g: (b + row_block0, g))]
    out_shape = [jax.ShapeDtypeStruct((T_ALL, D_INNER), BF16),
                 jax.ShapeDtypeStruct((T_ALL, SSD_GROUPS * LANES), F32)]
    if emit_state:
        out_specs.append(state_spec)
        out_shape.append(jax.ShapeDtypeStruct((n_seq, 1, 2, SSD_HEADS, SSD_N, SSD_P), F32))
    return pl.pallas_call(
        functools.partial(_ssd_kernel, n_chunks, has_s0, emit_state, bool(fill)),
        grid=(n_seq, SSD_GROUPS),
        in_specs=in_specs,
        out_specs=out_specs,
        out_shape=out_shape,
        input_output_aliases=aliases,
        scratch_shapes=[pltpu.VMEM((seq_len + 2 * HALO, GW + 2 * SSD_N), F32),
                        pltpu.VMEM((seq_len, GW), F32), pltpu.VMEM((seq_len, SSD_N), BF16),
                        pltpu.VMEM((seq_len, SSD_N), BF16), pltpu.VMEM((n_chunks, SSD_N, GW), BF16),
                        pltpu.VMEM((SSD_N, GW), F32), pltpu.VMEM((SSD_N, GW), F32)],
        compiler_params=_cparams("arbitrary", "arbitrary"),
        name="ssd_scan_latent" if has_s0 else "ssd_scan_prompt",
    )(*args)


def _router_kernel(x_ref, g_ref, mod_ref, rw_ref, hn_ref, idx_ref, wgt_ref, hf_ref):
    _adaln_to(x_ref, g_ref, mod_ref, 3, 4, hf_ref)
    hf = hf_ref[...]
    h_hi = hf.astype(BF16)
    h_lo = (hf - h_hi.astype(F32)).astype(BF16)
    hn_ref[...] = h_hi
    rw = rw_ref[...]
    r_hi = rw.astype(BF16)
    r_lo = (rw - r_hi.astype(F32)).astype(BF16)
    t1 = jnp.dot(h_hi, jnp.concatenate([r_hi, r_lo], axis=1), preferred_element_type=F32)
    logits = t1[:, :LANES] + t1[:, LANES:] + jnp.dot(h_lo, r_hi, preferred_element_type=F32)
    lane = lax.broadcasted_iota(jnp.int32, logits.shape, 1)
    lg = jnp.where(lane < N_EXPERTS, logits, NEG_INF)
    m1 = jnp.max(lg, axis=-1, keepdims=True)
    i1 = jnp.min(jnp.where(lg == m1, lane, LANES), axis=-1, keepdims=True)
    lg2 = jnp.where(lane == i1, NEG_INF, lg)
    m2 = jnp.max(lg2, axis=-1, keepdims=True)
    i2 = jnp.min(jnp.where(lg2 == m2, lane, LANES), axis=-1, keepdims=True)
    e2 = jnp.exp(m2 - m1)
    w1 = 1.0 / (1.0 + e2)
    idx_ref[...] = jnp.where(lane == 0, i1, jnp.where(lane == 1, i2, 0))
    wgt_ref[...] = jnp.where(lane == 0, w1, jnp.where(lane == 1, e2 * w1, 0.0))


def _router(x, gain, modt, router_w):
    t, d = x.shape
    rw = jnp.pad(router_w, ((0, 0), (0, LANES - N_EXPERTS)))
    return pl.pallas_call(
        _router_kernel,
        grid=(t // ROW_TILE,),
        in_specs=[pl.BlockSpec((ROW_TILE, d), lambda i: (i, 0)),
                  pl.BlockSpec((1, d), lambda i: (0, 0)),
                  pl.BlockSpec((None, 8, d), lambda i: (i, 0, 0)),
                  pl.BlockSpec((d, LANES), lambda i: (0, 0))],
        out_specs=[pl.BlockSpec((ROW_TILE, d), lambda i: (i, 0)),
                   pl.BlockSpec((ROW_TILE, LANES), lambda i: (i, 0)),
                   pl.BlockSpec((ROW_TILE, LANES), lambda i: (i, 0))],
        out_shape=[jax.ShapeDtypeStruct((t, d), BF16), jax.ShapeDtypeStruct((t, LANES), jnp.int32),
                   jax.ShapeDtypeStruct((t, LANES), F32)],
        scratch_shapes=[pltpu.VMEM((ROW_TILE, d), F32)],
        compiler_params=_cparams("arbitrary"),
        name="moe_router",
    )(x, gain.reshape(1, d), modt, rw)


DOWN_ROWS = 512
DOWN_TN = 512


def _expert_changed(be_ref, blk, prev_blk, step):
    return jnp.logical_or(step == 0, be_ref[blk] != be_ref[prev_blk])


def _expert_up_kernel(be_ref, nu_ref, nv_ref, xs_ref, wg_ref, wu_ref, h_ref, wgb_ref, wub_ref):
    i = pl.program_id(1)
    used = nv_ref[i] > 0

    @pl.when(jnp.logical_and(used, _expert_changed(be_ref, i, jnp.maximum(i - 1, 0), i)))
    def _():
        wgb_ref[...] = wg_ref[...].astype(BF16)
        wub_ref[...] = wu_ref[...].astype(BF16)

    @pl.when(used)
    def _():
        xs = xs_ref[...]
        h_ref[...] = (_silu(_bdot(xs, wgb_ref[...])) * _bdot(xs, wub_ref[...])).astype(h_ref.dtype)

    @pl.when(jnp.logical_not(used))
    def _():
        h_ref[...] = jnp.zeros_like(h_ref)


def _expert_down_kernel(be_ref, nu_ref, nv_ref, h_ref, wd_ref, o_ref, wdb_ref):
    i = pl.program_id(1)
    per = MOE_ROWS // DOWN_ROWS
    blk = i // per
    used = nv_ref[blk] > (i % per) * DOWN_ROWS

    @pl.when(jnp.logical_and(nv_ref[blk] > 0, _expert_changed(be_ref, blk, jnp.maximum(i - 1, 0) // per, i)))
    def _():
        wdb_ref[...] = wd_ref[...].astype(BF16)

    @pl.when(used)
    def _():
        o_ref[...] = _bdot(h_ref[...], wdb_ref[...]).astype(o_ref.dtype)

    @pl.when(jnp.logical_not(used))
    def _():
        o_ref[...] = jnp.zeros_like(o_ref)


def _experts(xs_sorted, block_e, n_used, n_valid, wg, wu, wd):
    cap, d = xs_sorted.shape
    ff = wg.shape[2]
    per = MOE_ROWS // DOWN_ROWS

    def expert_of(blk, be, nu):
        return be[jnp.minimum(blk, jnp.maximum(nu[0] - 1, 0))]

    h = pl.pallas_call(
        _expert_up_kernel,
        grid_spec=pltpu.PrefetchScalarGridSpec(
            num_scalar_prefetch=3,
            grid=(ff // FFN_TF, cap // MOE_ROWS),
            in_specs=[pl.BlockSpec((MOE_ROWS, d), lambda f, i, be, nu, nv: (i, 0)),
                      pl.BlockSpec((None, d, FFN_TF), lambda f, i, be, nu, nv: (expert_of(i, be, nu), 0, f)),
                      pl.BlockSpec((None, d, FFN_TF), lambda f, i, be, nu, nv: (expert_of(i, be, nu), 0, f))],
            out_specs=pl.BlockSpec((MOE_ROWS, FFN_TF), lambda f, i, be, nu, nv: (i, f)),
            scratch_shapes=[pltpu.VMEM((d, FFN_TF), BF16), pltpu.VMEM((d, FFN_TF), BF16)],
        ),
        out_shape=jax.ShapeDtypeStruct((cap, ff), BF16),
        compiler_params=_cparams("arbitrary", "arbitrary"),
        name="moe_expert_up",
    )(block_e, n_used, n_valid, xs_sorted, wg, wu)
    return pl.pallas_call(
        _expert_down_kernel,
        grid_spec=pltpu.PrefetchScalarGridSpec(
            num_scalar_prefetch=3,
            grid=(d // DOWN_TN, cap // DOWN_ROWS),
            in_specs=[pl.BlockSpec((DOWN_ROWS, ff), lambda n, i, be, nu, nv: (i, 0)),
                      pl.BlockSpec((None, ff, DOWN_TN),
                                   lambda n, i, be, nu, nv: (expert_of(i // per, be, nu), 0, n))],
            out_specs=pl.BlockSpec((DOWN_ROWS, DOWN_TN), lambda n, i, be, nu, nv: (i, n)),
            scratch_shapes=[pltpu.VMEM((ff, DOWN_TN), BF16)],
        ),
        out_shape=jax.ShapeDtypeStruct((cap, d), BF16),
        compiler_params=_cparams("arbitrary", "arbitrary"),
        name="moe_expert_down",
    )(block_e, n_used, n_valid, h, wd)


def _combine_kernel(x_ref, g_ref, w_ref, mod_ref, o_ref):
    d = x_ref.shape[1]
    w = w_ref[...]
    y = w[:, 0:1] * g_ref[:, :d].astype(F32) + w[:, 1:2] * g_ref[:, d:].astype(F32)
    o_ref[...] = x_ref[...] + mod_ref[5:6, :] * y


COMB_ROWS = 512


def _combine(x, g, wgt, modt, row0, n_rows):
    d = x.shape[1]
    b0 = row0 // COMB_ROWS
    per = ROW_TILE // COMB_ROWS
    return pl.pallas_call(
        _combine_kernel,
        grid=(n_rows // COMB_ROWS,),
        in_specs=[pl.BlockSpec((COMB_ROWS, d), lambda i: (i + b0, 0)),
                  pl.BlockSpec((COMB_ROWS, TOP_K * d), lambda i: (i + b0, 0)),
                  pl.BlockSpec((COMB_ROWS, LANES), lambda i: (i + b0, 0)),
                  pl.BlockSpec((None, 8, d), lambda i: ((i + b0) // per, 0, 0))],
        out_specs=pl.BlockSpec((COMB_ROWS, d), lambda i: (i, 0)),
        out_shape=jax.ShapeDtypeStruct((n_rows, d), F32),
        compiler_params=_cparams("arbitrary"),
        name="moe_combine",
    )(x, g, wgt, modt)


def _moe(x, gain, modt, router_w, wg, wu, wd):
    t, d = x.shape
    hn, idx, wgt = _router(x, gain, modt, router_w)
    top_idx = idx[:, :TOP_K]
    n_slots = t * TOP_K
    flat_e = top_idx.reshape(-1)
    onehot = (flat_e[:, None] == jnp.arange(N_EXPERTS, dtype=jnp.int32)[None, :]).astype(jnp.int32)
    incl = jnp.cumsum(onehot, axis=0)
    counts = incl[-1]
    rank = jnp.sum((incl - onehot) * onehot, axis=1)
    padded = (counts + MOE_ROWS - 1) // MOE_ROWS * MOE_ROWS
    pend = jnp.cumsum(padded)
    pstart = pend - padded
    dest = pstart[flat_e] + rank
    n_blocks = n_slots // MOE_ROWS + N_EXPERTS
    cap = n_blocks * MOE_ROWS
    row_tok = (jnp.arange(cap, dtype=jnp.int32) % t).at[dest].set(jnp.arange(n_slots, dtype=jnp.int32) // TOP_K)
    block_e = jnp.clip(jnp.searchsorted(pend, jnp.arange(n_blocks, dtype=jnp.int32) * MOE_ROWS, side='right'),
                       0, N_EXPERTS - 1).astype(jnp.int32)
    n_used = (pend[-1:] // MOE_ROWS).astype(jnp.int32)
    blk = jnp.arange(n_blocks, dtype=jnp.int32)
    n_valid = jnp.where(blk < n_used[0],
                        jnp.clip((pstart + counts)[block_e] - blk * MOE_ROWS, 0, MOE_ROWS), 0).astype(jnp.int32)
    out = _experts(hn[row_tok], block_e, n_used, n_valid, wg, wu, wd)
    g = out[dest].reshape(t, TOP_K * d)
    return _combine(x, g, wgt, modt, 0, T_PROMPT), _combine(x, g, wgt, modt, T_PROMPT, T_SAMPLE)


def _rope_tables(n_tokens, dim):
    n_rows = n_tokens // GRID_W
    row = jnp.repeat(jnp.arange(n_rows), GRID_W).astype(F32)
    col = jnp.tile(jnp.arange(GRID_W), n_rows).astype(F32)
    n_freq = dim // 4
    inv = ROPE_BASE ** (-jnp.arange(n_freq, dtype=F32) / n_freq)
    ang = jnp.concatenate([row[:, None] * inv, col[:, None] * inv], axis=-1)
    return jnp.cos(ang), jnp.sin(ang)


def kernel(x_prompt, x_sample, state_ret, cache_k, cache_v, state_ssd, c, c_ctx, ada_w, ada_b, norm_mix, norm_ffn, ev_w_in, ev_w_out, ret_decay_logit, ret_norm, att_q_norm, att_k_norm, att_sink, ffn_w_gate, ffn_w_up, ffn_w_down, ssd_w_in, ssd_conv_w, ssd_conv_b, ssd_a_log, ssd_dt_bias, ssd_d, ssd_norm, ssd_w_out, moe_router, moe_w_gate, moe_w_up, moe_w_down):
    d = D_MODEL
    x = (x_prompt.reshape(T_PROMPT, d), x_sample.reshape(T_SAMPLE, d))

    cvecs = jnp.concatenate([c_ctx[None, :], c, jnp.zeros((MOD_ROWS - 1 - DEC_BATCH, d), F32)], axis=0)
    mods = _modulation(cvecs, ada_w, ada_b).reshape(2, MOD_ROWS, 6, d)
    tiles_per_seq = DEC_SEQ // ROW_TILE
    tile_row = jnp.concatenate([jnp.zeros((T_PROMPT // ROW_TILE,), jnp.int32),
                                1 + jnp.arange(T_SAMPLE // ROW_TILE, dtype=jnp.int32) // tiles_per_seq])
    modt = jnp.pad(mods[:, tile_row], ((0, 0), (0, 0), (0, 2), (0, 0)))

    ev_w_in_b, ev_w_out_b = ev_w_in[0].astype(BF16), ev_w_out[0].astype(BF16)
    ffn_wg_b, ffn_wu_b, ffn_wd_b = ffn_w_gate[0].astype(BF16), ffn_w_up[0].astype(BF16), ffn_w_down[0].astype(BF16)
    ssd_w_in_b, ssd_w_out_b = ssd_w_in[0].astype(BF16), ssd_w_out[0].astype(BF16)

    proj, = _adaln_matmul(x, norm_mix[0], modt[0], 0, 1, ev_w_in_b, tn=512, out_dtype=BF16, name="even_in_proj")
    lg = jax.nn.log_sigmoid(ret_decay_logit[0].astype(F32))
    cos_r, sin_r = _rope_tables(DEC_SEQ, RET_DK)
    cos_a, sin_a = _rope_tables(DEC_SEQ, ATT_HD)
    cos_a2 = jnp.concatenate([cos_a, cos_a], axis=-1)
    sin_a2 = jnp.concatenate([-sin_a, sin_a], axis=-1)
    mix_ret, new_state_ret = _retention(proj, lg, ret_norm[0], SEQ, BATCH, 0, emit_state=True)
    mix_ret, = _retention(proj, lg, ret_norm[0], DEC_SEQ, DEC_BATCH, T_PROMPT // DEC_SEQ,
                          ropes=(cos_r, sin_r), s0=state_ret, fill=(mix_ret,))
    mix_att, new_k, new_v = _attention_prompt(proj, att_sink[0], att_q_norm[0], att_k_norm[0])
    mix_att = _attention_latent(proj, cache_k[:, 0], cache_v[:, 0], att_sink[0], cos_a2, sin_a2,
                                att_q_norm[0], att_k_norm[0], mix_att)
    x = _proj_residual([mix_ret, mix_att], ev_w_out_b, x, modt[0], 2, tn=1024, name="even_out_proj")
    h = _ffn_gateup(x, norm_ffn[0], modt[0], ffn_wg_b, ffn_wu_b)
    x = _proj_residual([h], ffn_wd_b, x, modt[0], 5, tn=512, name="ffn_down")

    zx, dt_raw = _adaln_matmul(x, norm_mix[1], modt[1], 0, 1, ssd_w_in_b, tn=1024, n_out=SSD_ZX,
                               tail=2 * SSD_HEADS, out_dtype=BF16, name="ssd_in_proj")
    prep = _ssd_prep(dt_raw, ssd_dt_bias[0], ssd_a_log[0])
    d_exp = jnp.repeat(ssd_d[0], SSD_P)[None, :]
    conv_b = ssd_conv_b[0][None, :]
    yg, yss, new_state_ssd = _ssd_scan(zx, prep, ssd_conv_w[0], conv_b, d_exp, ssd_norm[0], SEQ, BATCH, 0,
                                       emit_state=True)
    yg, yss = _ssd_scan(zx, prep, ssd_conv_w[0], conv_b, d_exp, ssd_norm[0], DEC_SEQ, DEC_BATCH,
                        T_PROMPT // DEC_SEQ, s0=state_ssd, fill=(yg, yss))
    x = _proj_residual([yg], ssd_w_out_b, x, modt[1], 2, tn=512, row_ss=yss, name="ssd_out_proj")
    y_p, y_s = _moe(x, norm_ffn[1], modt[1], moe_router[0], moe_w_gate[0], moe_w_up[0], moe_w_down[0])

    y_prompt = y_p.reshape(BATCH, SEQ, d)
    y_sample = y_s.reshape(DEC_BATCH, DEC_SEQ, d)
    new_cache_k = new_k.reshape(BATCH, 1, SEQ, ATT_KV_HEADS, ATT_HD)
    new_cache_v = new_v.reshape(BATCH, 1, SEQ, ATT_KV_HEADS, ATT_HD)
    return (y_prompt, y_sample, new_state_ret, new_cache_k, new_cache_v, new_state_ssd)
```

```python
import functools

import jax
import jax.numpy as jnp
from jax import lax
from jax.experimental import pallas as pl
from jax.experimental.pallas import tpu as pltpu

F32 = jnp.float32
BF16 = jnp.bfloat16

D_MODEL = 2048
BATCH = 16
SEQ = 256
DEC_BATCH = 8
DEC_SEQ = 2048
PAST_LEN = 512
GRID_W = 64
BLOCK = 128
WINDOW = 128
EPS = 1e-6
ROPE_BASE = 10000.0
RET_HEADS = 4
RET_DK = 256
RET_DV = 256
ATT_HEADS = 8
ATT_KV_HEADS = 2
ATT_HD = 128
ATT_GROUP = ATT_HEADS // ATT_KV_HEADS
EVEN_IN = 5632
D_INNER = 2 * D_MODEL
SSD_P = 64
SSD_HEADS = D_INNER // SSD_P
SSD_N = 128
SSD_GROUPS = 8
SSD_R = SSD_HEADS // SSD_GROUPS
CONV_W = 5
CONV_CH = D_INNER + 2 * SSD_GROUPS * SSD_N
SSD_ZX = D_INNER + CONV_CH
D_FF = 5632
N_EXPERTS = 8
TOP_K = 2

T_PROMPT = BATCH * SEQ
T_SAMPLE = DEC_BATCH * DEC_SEQ
T_ALL = T_PROMPT + T_SAMPLE

LANES = 128
ROW_TILE = 1024
VMEM_LIMIT = 56 * 1024 * 1024
N_ROW_TILES = T_ALL // ROW_TILE
MOE_ROWS = 1024
NEG_INF = float("-inf")


def _cparams(*sem):
    return pltpu.CompilerParams(dimension_semantics=sem, vmem_limit_bytes=VMEM_LIMIT)


def _silu(x):
    return x * jax.nn.sigmoid(x)


def _bdot(a, b):
    return jnp.dot(a.astype(BF16), b.astype(BF16), preferred_element_type=F32)


def _bdot_nt(a, b):
    return lax.dot_general(a.astype(BF16), b.astype(BF16), (((1,), (1,)), ((), ())),
                           preferred_element_type=F32)


def _bdot_tn(a, b):
    return lax.dot_general(a.astype(BF16), b.astype(BF16), (((0,), (0,)), ((), ())),
                           preferred_element_type=F32)


def _rows(c, n=BLOCK):
    return pl.ds(pl.multiple_of(c * n, n), n)


MOD_ROWS = 16
MOD_TN = 1024


def _mod_kernel(c_ref, w_ref, b_ref, o_ref):
    o_ref[...] = _bdot(_silu(c_ref[...]), w_ref[...]) + b_ref[...]


def _modulation(cvecs, ada_w, ada_b):
    depth, d, n = ada_w.shape
    return pl.pallas_call(
        _mod_kernel,
        grid=(depth, n // MOD_TN),
        in_specs=[pl.BlockSpec((MOD_ROWS, d), lambda l, j: (0, 0)),
                  pl.BlockSpec((None, d, MOD_TN), lambda l, j: (l, 0, j)),
                  pl.BlockSpec((None, 1, MOD_TN), lambda l, j: (l, 0, j))],
        out_specs=pl.BlockSpec((None, MOD_ROWS, MOD_TN), lambda l, j: (l, 0, j)),
        out_shape=jax.ShapeDtypeStruct((depth, MOD_ROWS, n), F32),
        compiler_params=_cparams("arbitrary", "arbitrary"),
        name="modulation",
    )(cvecs, ada_w, ada_b.reshape(depth, 1, n))


ADALN_CHUNK = 64


def _adaln_to(x_ref, g_ref, mod_ref, shift_row, scale_row, hn_ref):
    mult = g_ref[...] * (1.0 + mod_ref[scale_row:scale_row + 1, :])
    shift = mod_ref[shift_row:shift_row + 1, :]

    def body(i, carry):
        r = _rows(i, ADALN_CHUNK)
        x = x_ref[r, :]
        ms = jnp.mean(x * x, axis=-1, keepdims=True)
        hn_ref[r, :] = (x * lax.rsqrt(ms + EPS) * mult + shift).astype(hn_ref.dtype)
        return carry

    lax.fori_loop(0, x_ref.shape[0] // ADALN_CHUNK, body, 0)


PROMPT_TILES = T_PROMPT // ROW_TILE


def _x_rows_specs(x, cols, col_of_j):
    if not isinstance(x, tuple):
        return [pl.BlockSpec((ROW_TILE, cols), lambda i, j: (i, col_of_j(j)))], [x]
    return ([pl.BlockSpec((ROW_TILE, cols),
                          lambda i, j: (jnp.minimum(i, PROMPT_TILES - 1), jnp.where(i < PROMPT_TILES, col_of_j(j), 0))),
             pl.BlockSpec((ROW_TILE, cols),
                          lambda i, j: (jnp.maximum(i - PROMPT_TILES, 0), jnp.where(i < PROMPT_TILES, 0, col_of_j(j))))],
            list(x))


def _for_x_rows(x_refs, fn):
    if len(x_refs) == 1:
        fn(x_refs[0])
        return
    in_prompt = pl.program_id(0) < PROMPT_TILES
    pl.when(in_prompt)(lambda: fn(x_refs[0]))
    pl.when(jnp.logical_not(in_prompt))(lambda: fn(x_refs[1]))


def _adaln_mm_kernel(shift_row, scale_row, has_tail, n_x, *refs):
    x_refs, (g_ref, mod_ref, w_ref), refs = refs[:n_x], refs[n_x:n_x + 3], refs[n_x + 3:]
    if has_tail:
        wt_ref, o_ref, ot_ref, hn_ref = refs
    else:
        o_ref, hn_ref = refs

    @pl.when(pl.program_id(1) == 0)
    def _():
        _for_x_rows(x_refs, lambda x_ref: _adaln_to(x_ref, g_ref, mod_ref, shift_row, scale_row, hn_ref))
        if has_tail:
            ot_ref[...] = _bdot(hn_ref[...], wt_ref[...])

    o_ref[...] = _bdot(hn_ref[...], w_ref[...]).astype(o_ref.dtype)


def _adaln_matmul(x, gain, modt, shift_row, scale_row, w, tn, n_out=None, tail=0, out_dtype=F32, name="adaln_mm"):
    t, d = T_ALL, w.shape[0]
    n_out = w.shape[1] if n_out is None else n_out
    in_specs, args = _x_rows_specs(x, d, lambda j: 0)
    n_x = len(args)
    in_specs += [pl.BlockSpec((1, d), lambda i, j: (0, 0)),
                 pl.BlockSpec((None, 8, d), lambda i, j: (i, 0, 0)),
                 pl.BlockSpec((d, tn), lambda i, j: (0, j))]
    args += [gain.reshape(1, d), modt, w]
    out_specs = [pl.BlockSpec((ROW_TILE, tn), lambda i, j: (i, j))]
    out_shape = [jax.ShapeDtypeStruct((t, n_out), out_dtype)]
    if tail:
        in_specs.append(pl.BlockSpec((d, tail), lambda i, j: (0, n_out // tail)))
        args.append(w)
        out_specs.append(pl.BlockSpec((ROW_TILE, tail), lambda i, j: (i, 0)))
        out_shape.append(jax.ShapeDtypeStruct((t, tail), F32))
    return pl.pallas_call(
        functools.partial(_adaln_mm_kernel, shift_row, scale_row, bool(tail), n_x),
        grid=(t // ROW_TILE, n_out // tn),
        in_specs=in_specs,
        out_specs=out_specs,
        out_shape=out_shape,
        scratch_shapes=[pltpu.VMEM((ROW_TILE, d), BF16)],
        compiler_params=_cparams("arbitrary", "arbitrary"),
        name=name,
    )(*args)


FFN_TF = 512


def _gateup_kernel(x_ref, g_ref, mod_ref, wg_ref, wu_ref, h_ref, hn_ref):
    @pl.when(pl.program_id(1) == 0)
    def _():
        _adaln_to(x_ref, g_ref, mod_ref, 3, 4, hn_ref)

    hn = hn_ref[...]
    h_ref[...] = (_silu(_bdot(hn, wg_ref[...])) * _bdot(hn, wu_ref[...])).astype(h_ref.dtype)


def _ffn_gateup(x, gain, modt, wg, wu):
    t, d = x.shape
    ff = wg.shape[1]
    return pl.pallas_call(
        _gateup_kernel,
        grid=(t // ROW_TILE, ff // FFN_TF),
        in_specs=[pl.BlockSpec((ROW_TILE, d), lambda i, f: (i, 0)),
                  pl.BlockSpec((1, d), lambda i, f: (0, 0)),
                  pl.BlockSpec((None, 8, d), lambda i, f: (i, 0, 0)),
                  pl.BlockSpec((d, FFN_TF), lambda i, f: (0, f)),
                  pl.BlockSpec((d, FFN_TF), lambda i, f: (0, f))],
        out_specs=pl.BlockSpec((ROW_TILE, FFN_TF), lambda i, f: (i, f)),
        out_shape=jax.ShapeDtypeStruct((t, ff), BF16),
        scratch_shapes=[pltpu.VMEM((ROW_TILE, d), BF16)],
        compiler_params=_cparams("arbitrary", "arbitrary"),
        name="ffn_gateup",
    )(x, gain.reshape(1, d), modt, wg, wu)


def _proj_res_kernel(n_a, n_x, gate_row, norm, *refs):
    a_refs = refs[:n_a]
    w_refs = refs[n_a:2 * n_a]
    x_refs, mod_ref = refs[2 * n_a:2 * n_a + n_x], refs[2 * n_a + n_x]
    pos = 2 * n_a + n_x + 1
    o_ref = refs[pos + 1] if norm else refs[pos]
    acc = _bdot(a_refs[0][...], w_refs[0][...])
    for k in range(1, n_a):
        acc = acc + _bdot(a_refs[k][...], w_refs[k][...])
    if norm:
        ss_ref = refs[pos]
        k_total = sum(a.shape[1] for a in a_refs)
        ss = ss_ref[:, 0:LANES]
        for k in range(1, ss_ref.shape[1] // LANES):
            ss = ss + ss_ref[:, k * LANES:(k + 1) * LANES]
        rs = lax.rsqrt(ss * (1.0 / k_total) + EPS)
        acc = acc * jnp.concatenate([rs] * (acc.shape[1] // LANES), axis=1)
    upd = mod_ref[gate_row:gate_row + 1, :] * acc

    def finish(x_ref):
        o_ref[...] = x_ref[...] + upd

    _for_x_rows(x_refs, finish)


def _proj_residual(a_list, w, x, modt, gate_row, tn, row_ss=None, name="proj_res"):
    t, d = T_ALL, w.shape[1]
    n_a = len(a_list)
    norm = row_ss is not None
    in_specs, args, k0 = [], [], 0
    for a in a_list:
        in_specs.append(pl.BlockSpec((ROW_TILE, a.shape[1]), lambda i, j: (i, 0)))
        args.append(a)
    for a in a_list:
        ka = a.shape[1]
        assert k0 % ka == 0
        in_specs.append(pl.BlockSpec((ka, tn), lambda i, j, kb=k0 // ka: (kb, j)))
        args.append(w)
        k0 += ka
    x_specs, x_args = _x_rows_specs(x, tn, lambda j: j)
    in_specs += x_specs + [pl.BlockSpec((None, 8, tn), lambda i, j: (i, 0, j))]
    args += x_args + [modt]
    if norm:
        in_specs.append(pl.BlockSpec((ROW_TILE, row_ss.shape[1]), lambda i, j: (i, 0)))
        args.append(row_ss)
    return pl.pallas_call(
        functools.partial(_proj_res_kernel, n_a, len(x_args), gate_row, norm),
        grid=(t // ROW_TILE, d // tn),
        in_specs=in_specs,
        out_specs=pl.BlockSpec((ROW_TILE, tn), lambda i, j: (i, j)),
        out_shape=jax.ShapeDtypeStruct((t, d), F32),
        compiler_params=_cparams("arbitrary", "arbitrary"),
        name=name,
    )(*args)


def _ret_kernel(n_chunks, rope, has_s0, emit_state, has_fill, lg_ref, *refs):
    it = iter(refs)
    q_ref, k_ref, v_ref, gt_ref = next(it), next(it), next(it), next(it)
    cos_ref, sin_ref = (next(it), next(it)) if rope else (None, None)
    s0_ref = next(it) if has_s0 else None
    gain_ref = next(it)
    if has_fill:
        next(it)
    o_ref = next(it)
    sfin_ref = next(it) if emit_state else None
    qs_ref, ks_ref, sfs_ref, dm_ref, dec_ref, sf_ref, sb_ref = (next(it) for _ in range(7))

    h = pl.program_id(1)
    lgf = lg_ref[0, h]
    lgb = lg_ref[1, h]
    half = RET_DK // 2

    def prep(c, carry):
        r = _rows(c)
        q = q_ref[r, :].astype(F32)
        k = k_ref[r, :].astype(F32) * (RET_DK ** -0.5)
        if rope:
            cs, sn = cos_ref[r, :], sin_ref[r, :]
            for src, dst in ((q, qs_ref), (k, ks_ref)):
                x1, x2 = src[:, :half], src[:, half:]
                dst[r, :half] = (x1 * cs - x2 * sn).astype(BF16)
                dst[r, half:] = (x1 * sn + x2 * cs).astype(BF16)
        else:
            qs_ref[r, :] = q.astype(BF16)
            ks_ref[r, :] = k.astype(BF16)
        return carry

    lax.fori_loop(0, n_chunks, prep, 0)

    ii = lax.broadcasted_iota(jnp.int32, (BLOCK, BLOCK), 0)
    jj = lax.broadcasted_iota(jnp.int32, (BLOCK, BLOCK), 1)
    diff = (ii - jj).astype(F32)
    dm_ref[...] = jnp.exp(jnp.where(jj <= ii, diff * lgf, -diff * lgb))
    pos = lax.broadcasted_iota(jnp.int32, (BLOCK, RET_DV), 0).astype(F32)
    dec_ref[0] = jnp.exp((pos + 1.0) * lgf)
    dec_ref[1] = jnp.exp((BLOCK - pos) * lgb)
    dec_ref[2] = jnp.exp((BLOCK - 1.0 - pos) * lgf)
    dec_ref[3] = jnp.exp(pos * lgb)
    tot_f = jnp.exp(jnp.full((1, RET_DV), BLOCK * lgf, F32))
    tot_b = jnp.exp(jnp.full((1, RET_DV), BLOCK * lgb, F32))

    if has_s0:
        sf_ref[...] = s0_ref[0]
        sb_ref[...] = s0_ref[1]
    else:
        sf_ref[...] = jnp.zeros_like(sf_ref)
        sb_ref[...] = jnp.zeros_like(sb_ref)

    def fwd(c, carry):
        r = _rows(c)
        sfs_ref[c] = sf_ref[...].astype(BF16)
        kd = ks_ref[r, :].astype(F32) * dec_ref[2]
        sf_ref[...] = sf_ref[...] * tot_f + _bdot_tn(kd, v_ref[r, :])
        return carry

    lax.fori_loop(0, n_chunks, fwd, 0, unroll=min(4, n_chunks))
    if emit_state:
        sfin_ref[0] = sf_ref[...]

    def bwd(t, carry):
        c = n_chunks - 1 - t
        r = _rows(c)
        q = qs_ref[r, :]
        k = ks_ref[r, :]
        v = v_ref[r, :].astype(BF16)
        p = _bdot_nt(q, k) * dm_ref[...]
        o = _bdot(p, v)
        o = o + _bdot(q, sfs_ref[c]) * dec_ref[0]
        o = o + _bdot(q, sb_ref[...]) * dec_ref[1]
        ms = jnp.mean(o * o, axis=-1, keepdims=True)
        y = o * lax.rsqrt(ms + EPS) * gain_ref[...]
        o_ref[r, :] = (y * _silu(gt_ref[r, :].astype(F32))).astype(o_ref.dtype)
        kd = k.astype(F32) * dec_ref[3]
        sb_ref[...] = sb_ref[...] * tot_b + _bdot_tn(kd, v)
        return carry

    lax.fori_loop(0, n_chunks, bwd, 0, unroll=min(4, n_chunks))
    if emit_state:
        sfin_ref[1] = sb_ref[...]


def _fill_alias(fill, in_specs, args):
    aliases = {}
    for k, arr in enumerate(fill or ()):
        in_specs.append(pl.BlockSpec(memory_space=pl.ANY))
        args.append(arr)
        aliases[len(args) - 1] = k
    return aliases


def _retention(proj, lg, ret_norm, seq_len, n_seq, row_block0, ropes=None, s0=None, emit_state=False, fill=None):
    n_chunks = seq_len // BLOCK
    rope, has_s0 = ropes is not None, s0 is not None

    def col(cb):
        return pl.BlockSpec((seq_len, RET_DK), lambda b, h, cb=cb: (b + row_block0, cb * RET_HEADS + h))

    in_specs = [pl.BlockSpec(memory_space=pltpu.SMEM), col(0), col(1), col(2), col(3)]
    args = [lg, proj, proj, proj, proj]
    if rope:
        in_specs += [pl.BlockSpec((seq_len, RET_DK // 2), lambda b, h: (0, 0))] * 2
        args += list(ropes)
    if has_s0:
        in_specs.append(pl.BlockSpec((None, None, 2, None, RET_DK, RET_DV), lambda b, h: (b, 0, 0, h, 0, 0)))
        args.append(s0)
    in_specs.append(pl.BlockSpec((1, RET_DV), lambda b, h: (0, h)))
    args.append(ret_norm.reshape(1, RET_HEADS * RET_DV))
    aliases = _fill_alias(fill, in_specs, args)
    out_specs = [pl.BlockSpec((seq_len, RET_DV), lambda b, h: (b + row_block0, h))]
    out_shape = [jax.ShapeDtypeStruct((T_ALL, RET_HEADS * RET_DV), BF16)]
    if emit_state:
        out_specs.append(pl.BlockSpec((None, None, 2, None, RET_DK, RET_DV), lambda b, h: (b, 0, 0, h, 0, 0)))
        out_shape.append(jax.ShapeDtypeStruct((n_seq, 1, 2, RET_HEADS, RET_DK, RET_DV), F32))
    return pl.pallas_call(
        functools.partial(_ret_kernel, n_chunks, rope, has_s0, emit_state, bool(fill)),
        grid=(n_seq, RET_HEADS),
        in_specs=in_specs,
        out_specs=out_specs,
        out_shape=out_shape,
        input_output_aliases=aliases,
        scratch_shapes=[pltpu.VMEM((seq_len, RET_DK), BF16), pltpu.VMEM((seq_len, RET_DK), BF16),
                        pltpu.VMEM((n_chunks, RET_DK, RET_DV), BF16), pltpu.VMEM((BLOCK, BLOCK), F32),
                        pltpu.VMEM((4, BLOCK, RET_DV), F32), pltpu.VMEM((RET_DK, RET_DV), F32),
                        pltpu.VMEM((RET_DK, RET_DV), F32)],
        compiler_params=_cparams("arbitrary", "arbitrary"),
        name="retention_latent" if rope else "retention_prompt",
    )(*args)


def _head_norm(x, gain):
    x = x.astype(F32)
    return x * lax.rsqrt(jnp.mean(x * x, axis=-1, keepdims=True) + EPS) * gain


def _rope_full(x, cs, sn):
    return x * cs + pltpu.roll(x, ATT_HD // 2, 1) * sn


ATT_SPLIT = 2


def _sink_column(sink_ref, head0, n_heads, rows_per_head):
    n = n_heads * rows_per_head
    head = lax.broadcasted_iota(jnp.int32, (n, 1), 0) // rows_per_head
    col = jnp.full((n, 1), sink_ref[head0], F32)
    for g in range(1, n_heads):
        col = jnp.where(head == g, sink_ref[head0 + g], col)
    return col


def _att_latent_kernel(sink_ref, q_ref, k_ref, v_ref, ck_ref, cv_ref, cos_ref, sin_ref, qn_ref, kn_ref,
                       fill_ref, o_ref, kp_ref, vp_ref, ckp_ref, cvp_ref):
    del fill_ref
    kh, qb = pl.program_id(1), pl.program_id(2)
    n_chunks = DEC_SEQ // BLOCK
    loc = 3 * BLOCK

    @pl.when(qb == 0)
    def _():
        def prep(c, carry):
            r = _rows(c)
            kp_ref[r, :] = _rope_full(_head_norm(k_ref[r, :], kn_ref[...]), cos_ref[r, :], sin_ref[r, :]).astype(BF16)
            vp_ref[r, :] = v_ref[r, :].astype(BF16)
            return carry

        lax.fori_loop(0, n_chunks, prep, 0)
        ckp_ref[...] = ck_ref[...].astype(BF16)
        cvp_ref[...] = cv_ref[...].astype(BF16)

    rq = _rows(qb)
    cs, sn = cos_ref[rq, :], sin_ref[rq, :]
    qg = qn_ref[...] * (ATT_HD ** -0.5)
    start = pl.multiple_of(jnp.clip((qb - 1) * BLOCK, 0, DEC_SEQ - loc), BLOCK)
    kl, vl = kp_ref[pl.ds(start, loc), :], vp_ref[pl.ds(start, loc), :]
    qpos = qb * BLOCK + (lax.broadcasted_iota(jnp.int32, (ATT_SPLIT * BLOCK, loc), 0) & (BLOCK - 1))
    kpos = start + lax.broadcasted_iota(jnp.int32, (ATT_SPLIT * BLOCK, loc), 1)
    in_window = jnp.abs(qpos - kpos) <= WINDOW
    for g0 in range(0, ATT_GROUP, ATT_SPLIT):
        q = jnp.concatenate(
            [_rope_full(_head_norm(q_ref[:, g * ATT_HD:(g + 1) * ATT_HD], qg), cs, sn).astype(BF16)
             for g in range(g0, g0 + ATT_SPLIT)], axis=0)
        s_loc = jnp.where(in_window, _bdot_nt(q, kl), NEG_INF)
        s_ctx = _bdot_nt(q, ckp_ref[...])
        sink = _sink_column(sink_ref, kh * ATT_GROUP + g0, ATT_SPLIT, BLOCK)
        m = jnp.maximum(jnp.maximum(jnp.max(s_loc, axis=-1, keepdims=True),
                                    jnp.max(s_ctx, axis=-1, keepdims=True)), sink)
        p_loc = jnp.exp(s_loc - m)
        p_ctx = jnp.exp(s_ctx - m)
        den = jnp.sum(p_loc, axis=-1, keepdims=True) + jnp.sum(p_ctx, axis=-1, keepdims=True) + jnp.exp(sink - m)
        o = (_bdot(p_loc, vl) + _bdot(p_ctx, cvp_ref[...])) / den
        for g in range(ATT_SPLIT):
            o_ref[:, (g0 + g) * ATT_HD:(g0 + g + 1) * ATT_HD] = o[g * BLOCK:(g + 1) * BLOCK, :].astype(o_ref.dtype)


def _attention_latent(proj, cache_k, cache_v, sink, cos_a, sin_a, q_norm, k_norm, fill):
    nqb = DEC_SEQ // BLOCK
    rb0 = T_PROMPT // DEC_SEQ
    qcol0 = 4 * RET_HEADS * RET_DK // (ATT_GROUP * ATT_HD)
    kcol0 = (4 * RET_HEADS * RET_DK + ATT_HEADS * ATT_HD) // ATT_HD
    vcol0 = kcol0 + ATT_KV_HEADS
    ck = cache_k.reshape(DEC_BATCH, PAST_LEN, ATT_KV_HEADS * ATT_HD)
    cv = cache_v.reshape(DEC_BATCH, PAST_LEN, ATT_KV_HEADS * ATT_HD)
    return pl.pallas_call(
        _att_latent_kernel,
        grid=(DEC_BATCH, ATT_KV_HEADS, nqb),
        in_specs=[pl.BlockSpec(memory_space=pltpu.SMEM),
                  pl.BlockSpec((BLOCK, ATT_GROUP * ATT_HD),
                               lambda b, kh, qb: (T_PROMPT // BLOCK + b * nqb + qb, qcol0 + kh)),
                  pl.BlockSpec((DEC_SEQ, ATT_HD), lambda b, kh, qb: (rb0 + b, kcol0 + kh)),
                  pl.BlockSpec((DEC_SEQ, ATT_HD), lambda b, kh, qb: (rb0 + b, vcol0 + kh)),
                  pl.BlockSpec((None, PAST_LEN, ATT_HD), lambda b, kh, qb: (b, 0, kh)),
                  pl.BlockSpec((None, PAST_LEN, ATT_HD), lambda b, kh, qb: (b, 0, kh)),
                  pl.BlockSpec((DEC_SEQ, ATT_HD), lambda b, kh, qb: (0, 0)),
                  pl.BlockSpec((DEC_SEQ, ATT_HD), lambda b, kh, qb: (0, 0)),
                  pl.BlockSpec((1, ATT_HD), lambda b, kh, qb: (0, 0)),
                  pl.BlockSpec((1, ATT_HD), lambda b, kh, qb: (0, 0)),
                  pl.BlockSpec(memory_space=pl.ANY)],
        out_specs=pl.BlockSpec((BLOCK, ATT_GROUP * ATT_HD),
                               lambda b, kh, qb: (T_PROMPT // BLOCK + b * nqb + qb, kh)),
        out_shape=jax.ShapeDtypeStruct((T_ALL, ATT_HEADS * ATT_HD), BF16),
        input_output_aliases={10: 0},
        scratch_shapes=[pltpu.VMEM((DEC_SEQ, ATT_HD), BF16), pltpu.VMEM((DEC_SEQ, ATT_HD), BF16),
                        pltpu.VMEM((PAST_LEN, ATT_HD), BF16), pltpu.VMEM((PAST_LEN, ATT_HD), BF16)],
        compiler_params=_cparams("arbitrary", "arbitrary", "arbitrary"),
        name="attention_latent",
    )(sink, proj, proj, proj, ck, cv, cos_a, sin_a, q_norm.reshape(1, ATT_HD), k_norm.reshape(1, ATT_HD), fill)


def _att_prompt_kernel(sink_ref, q_ref, k_ref, v_ref, qn_ref, kn_ref, o_ref, nk_ref, nv_ref):
    kh = pl.program_id(1)
    kn = _head_norm(k_ref[...], kn_ref[...])
    v = v_ref[...]
    nk_ref[...] = kn
    nv_ref[...] = v.astype(F32)
    qg = qn_ref[...] * (ATT_HD ** -0.5)
    q = jnp.concatenate([_head_norm(q_ref[:, g * ATT_HD:(g + 1) * ATT_HD], qg).astype(BF16)
                         for g in range(ATT_GROUP)], axis=0)
    s = _bdot_nt(q, kn)
    sink = _sink_column(sink_ref, kh * ATT_GROUP, ATT_GROUP, SEQ)
    m = jnp.maximum(jnp.max(s, axis=-1, keepdims=True), sink)
    p = jnp.exp(s - m)
    den = jnp.sum(p, axis=-1, keepdims=True) + jnp.exp(sink - m)
    o = _bdot(p, v) / den
    for g in range(ATT_GROUP):
        o_ref[:, g * ATT_HD:(g + 1) * ATT_HD] = o[g * SEQ:(g + 1) * SEQ, :].astype(o_ref.dtype)


def _attention_prompt(proj, sink, q_norm, k_norm):
    qcol0 = 4 * RET_HEADS * RET_DK // (ATT_GROUP * ATT_HD)
    kcol0 = (4 * RET_HEADS * RET_DK + ATT_HEADS * ATT_HD) // ATT_HD
    vcol0 = kcol0 + ATT_KV_HEADS
    kv_spec = pl.BlockSpec((None, SEQ, ATT_HD), lambda b, kh: (b, 0, kh))
    kv_shape = jax.ShapeDtypeStruct((BATCH, SEQ, ATT_KV_HEADS * ATT_HD), F32)
    return pl.pallas_call(
        _att_prompt_kernel,
        grid=(BATCH, ATT_KV_HEADS),
        in_specs=[pl.BlockSpec(memory_space=pltpu.SMEM),
                  pl.BlockSpec((SEQ, ATT_GROUP * ATT_HD), lambda b, kh: (b, qcol0 + kh)),
                  pl.BlockSpec((SEQ, ATT_HD), lambda b, kh: (b, kcol0 + kh)),
                  pl.BlockSpec((SEQ, ATT_HD), lambda b, kh: (b, vcol0 + kh)),
                  pl.BlockSpec((1, ATT_HD), lambda b, kh: (0, 0)),
                  pl.BlockSpec((1, ATT_HD), lambda b, kh: (0, 0))],
        out_specs=[pl.BlockSpec((SEQ, ATT_GROUP * ATT_HD), lambda b, kh: (b, kh)), kv_spec, kv_spec],
        out_shape=[jax.ShapeDtypeStruct((T_ALL, ATT_HEADS * ATT_HD), BF16), kv_shape, kv_shape],
        compiler_params=_cparams("arbitrary", "arbitrary"),
        name="attention_prompt",
    )(sink, proj, proj, proj, q_norm.reshape(1, ATT_HD), k_norm.reshape(1, ATT_HD))


def _split_dot(m01, a):
    hi = a.astype(BF16)
    r1 = a - hi.astype(F32)
    mid = r1.astype(BF16)
    lo = (r1 - mid.astype(F32)).astype(BF16)
    return (jnp.dot(m01, hi, preferred_element_type=F32) + jnp.dot(m01, mid, preferred_element_type=F32)
            + jnp.dot(m01, lo, preferred_element_type=F32))


PREP_CHUNKS = ROW_TILE // BLOCK


def _ssd_prep_kernel(raw_ref, bias_ref, alog_ref, cum_ref, dt_ref, w_ref, tot_ref, ecum_ref):
    ii = lax.broadcasted_iota(jnp.int32, (BLOCK, BLOCK), 0)
    jj = lax.broadcasted_iota(jnp.int32, (BLOCK, BLOCK), 1)
    lower = jnp.where(jj <= ii, 1.0, 0.0).astype(BF16)
    upper = jnp.where(jj >= ii, 1.0, 0.0).astype(BF16)
    fwd_lane = lax.broadcasted_iota(jnp.int32, (BLOCK, LANES), 1) < SSD_HEADS
    neg_a = -jnp.exp(alog_ref[...])

    def chunk(k, carry):
        x = raw_ref[_rows(k), :] + bias_ref[...]
        dt = jnp.maximum(x, 0.0) + jnp.log1p(jnp.exp(-jnp.abs(x)))
        a = dt * neg_a
        incl = _split_dot(lower, a)
        rincl = _split_dot(upper, a)
        cum = jnp.where(fwd_lane, incl, rincl)
        tot = jnp.where(fwd_lane[:1], incl[BLOCK - 1:BLOCK, :], rincl[0:1, :])
        cum_ref[k] = cum.T
        dt_ref[k] = dt.T
        w_ref[k] = (dt * jnp.exp(tot - cum)).T
        tot_ref[k] = jnp.broadcast_to(jnp.exp(tot), (BLOCK, LANES)).T
        ecum_ref[k] = jnp.exp(cum).T
        return carry

    lax.fori_loop(0, PREP_CHUNKS, chunk, 0)


def _ssd_prep(dt_raw, dt_bias, a_log):
    nc = T_ALL // BLOCK
    spec = pl.BlockSpec((PREP_CHUNKS, 2 * SSD_HEADS, BLOCK), lambda c: (c, 0, 0))
    shape = jax.ShapeDtypeStruct((nc, 2 * SSD_HEADS, BLOCK), F32)
    return pl.pallas_call(
        _ssd_prep_kernel,
        grid=(nc // PREP_CHUNKS,),
        in_specs=[pl.BlockSpec((ROW_TILE, 2 * SSD_HEADS), lambda c: (c, 0)),
                  pl.BlockSpec((1, 2 * SSD_HEADS), lambda c: (0, 0)),
                  pl.BlockSpec((1, 2 * SSD_HEADS), lambda c: (0, 0))],
        out_specs=[spec] * 5,
        out_shape=[shape] * 5,
        compiler_params=_cparams("arbitrary"),
        name="ssd_prep",
    )(dt_raw, dt_bias.reshape(1, 2 * SSD_HEADS), a_log.reshape(1, 2 * SSD_HEADS))


GW = SSD_R * SSD_P
HALO = 8


def _pair_tiles(per_head):
    low = lax.broadcasted_iota(jnp.int32, per_head[0].shape, 1) < SSD_P
    return jnp.concatenate([jnp.where(low, per_head[2 * t], per_head[2 * t + 1]) for t in range(SSD_R // 2)],
                           axis=1)


def _row_bcast(ref, c, r):
    return jnp.broadcast_to(ref[c, r:r + 1, :], (BLOCK, BLOCK))


def _pair_cols(ref, c):
    top = lax.broadcasted_iota(jnp.int32, (BLOCK, BLOCK), 0) < SSD_P
    return jnp.concatenate(
        [jnp.where(top, _row_bcast(ref, c, 2 * t), _row_bcast(ref, c, 2 * t + 1)).T for t in range(SSD_R // 2)],
        axis=1)


def _ssd_kernel(n_chunks, has_s0, emit_state, has_fill, *refs):
    it = iter(refs)
    z_ref, x_ref, b_ref, c_ref = (next(it) for _ in range(4))
    cumf_ref, cumb_ref, dtf_ref, dtb_ref, wf_ref, wb_ref, totf_ref, totb_ref, ecf_ref, ecb_ref = (
        next(it) for _ in range(10))
    cwx_ref, cwb_ref, cwc_ref, cbx_ref, cbb_ref, cbc_ref, d_ref, ng_ref = (next(it) for _ in range(8))
    s0_ref = next(it) if has_s0 else None
    if has_fill:
        next(it), next(it)
    o_ref, ss_ref = next(it), next(it)
    sfin_ref = next(it) if emit_state else None
    pad_ref, xc_ref, bc_ref, cc_ref, sfs_ref, sf_ref, sb_ref = (next(it) for _ in range(7))
    seq_len = n_chunks * BLOCK

    pad_ref[0:HALO, :] = jnp.zeros((HALO, GW + 2 * SSD_N), F32)
    pad_ref[HALO + seq_len:2 * HALO + seq_len, :] = jnp.zeros((HALO, GW + 2 * SSD_N), F32)

    def fill(c, carry):
        dst = pl.ds(pl.multiple_of(c * BLOCK, BLOCK) + HALO, BLOCK)
        r = _rows(c)
        pad_ref[dst, 0:GW] = x_ref[r, :].astype(F32)
        pad_ref[dst, GW:GW + SSD_N] = b_ref[r, :].astype(F32)
        pad_ref[dst, GW + SSD_N:GW + 2 * SSD_N] = c_ref[r, :].astype(F32)
        return carry

    lax.fori_loop(0, n_chunks, fill, 0)

    def conv(c):
        r = _rows(c)
        src = pl.ds(pl.multiple_of(c * BLOCK, BLOCK), BLOCK + 2 * HALO)
        for col0, width, cw_ref, cb_ref, dst in ((0, GW, cwx_ref, cbx_ref, xc_ref),
                                                 (GW, SSD_N, cwb_ref, cbb_ref, bc_ref),
                                                 (GW + SSD_N, SSD_N, cwc_ref, cbc_ref, cc_ref)):
            for t in range(width // LANES):
                tl = slice(t * LANES, (t + 1) * LANES)
                win = pad_ref[src, col0 + t * LANES:col0 + (t + 1) * LANES]
                acc = jnp.broadcast_to(cb_ref[:, tl], (BLOCK, LANES))
                for w in range(CONV_W):
                    off = HALO - CONV_W // 2 + w
                    acc = acc + win[off:off + BLOCK, :] * cw_ref[w:w + 1, tl]
                dst[r, tl] = _silu(acc).astype(dst.dtype)

    if has_s0:
        for r in range(SSD_R):
            sf_ref[:, r * SSD_P:(r + 1) * SSD_P] = s0_ref[0, r]
            sb_ref[:, r * SSD_P:(r + 1) * SSD_P] = s0_ref[1, r]
    else:
        sf_ref[...] = jnp.zeros_like(sf_ref)
        sb_ref[...] = jnp.zeros_like(sb_ref)

    def state_update(s_ref, c, w_ref, tot_ref, bm, xs):
        tot = _pair_tiles([tot_ref[c, r:r + 1, :] for r in range(SSD_R)])
        s_ref[...] = s_ref[...] * tot + _bdot_tn(bm, xs * _pair_cols(w_ref, c))

    def fwd_step(c):
        r = _rows(c)
        sfs_ref[c] = sf_ref[...].astype(BF16)
        state_update(sf_ref, c, wf_ref, totf_ref, bc_ref[r, :], xc_ref[r, :])

    def fwd(c, carry):
        fwd_step(c)
        conv(c + 1)
        return carry

    conv(0)
    lax.fori_loop(0, n_chunks - 1, fwd, 0)
    fwd_step(n_chunks - 1)
    if emit_state:
        for r in range(SSD_R):
            sfin_ref[0, r] = sf_ref[:, r * SSD_P:(r + 1) * SSD_P]

    ii = lax.broadcasted_iota(jnp.int32, (BLOCK, BLOCK), 0)
    jj = lax.broadcasted_iota(jnp.int32, (BLOCK, BLOCK), 1)
    causal = jj <= ii
    anti = ii <= jj
    low = lax.broadcasted_iota(jnp.int32, (BLOCK, LANES), 1) < SSD_P

    def bwd(t, carry):
        c = n_chunks - 1 - t
        r = _rows(c)
        cm, bm, xs = cc_ref[r, :], bc_ref[r, :], xc_ref[r, :]
        xb = xs.astype(BF16)
        sc = _bdot_nt(cm, bm)
        yf = _bdot(cm, sfs_ref[c])
        yb = _bdot(cm, sb_ref[...])
        tiles = []
        for t2 in range(SSD_R // 2):
            xt = xb[:, t2 * LANES:(t2 + 1) * LANES]
            acc = None
            for hh in range(2):
                hr = 2 * t2 + hh
                rf, rb = _row_bcast(cumf_ref, c, hr), _row_bcast(cumb_ref, c, hr)
                e = jnp.exp(jnp.where(anti, rf, rb).T - jnp.where(causal, rf, rb))
                e = e * jnp.where(causal, _row_bcast(dtf_ref, c, hr), _row_bcast(dtb_ref, c, hr))
                xh = jnp.where(low, xt, 0.0) if hh == 0 else jnp.where(low, 0.0, xt)
                part = _bdot(sc * e, xh)
                acc = part if acc is None else acc + part
            tiles.append(acc)
        y = (jnp.concatenate(tiles, axis=1) + _pair_cols(ecf_ref, c) * yf + _pair_cols(ecb_ref, c) * yb
             + d_ref[...] * xs)
        yg = y * _silu(z_ref[r, :].astype(F32))
        ss_ref[r, :] = jnp.broadcast_to(jnp.sum(yg * yg, axis=-1, keepdims=True), (BLOCK, LANES))
        o_ref[r, :] = (yg * ng_ref[...]).astype(o_ref.dtype)
        state_update(sb_ref, c, wb_ref, totb_ref, bm, xs)
        return carry

    lax.fori_loop(0, n_chunks, bwd, 0, unroll=2)
    if emit_state:
        for r in range(SSD_R):
            sfin_ref[1, r] = sb_ref[:, r * SSD_P:(r + 1) * SSD_P]


def _ssd_scan(zx, prep, conv_w, conv_b, d_exp, norm_gain, seq_len, n_seq, row_block0, s0=None, emit_state=False,
              fill=None):
    n_chunks = seq_len // BLOCK
    has_s0 = s0 is not None
    xcol0 = D_INNER // GW
    bcol0 = 2 * D_INNER // SSD_N
    ccol0 = bcol0 + SSD_GROUPS
    cwb0 = D_INNER // SSD_N

    def rowcol(width, col0):
        return pl.BlockSpec((seq_len, width), lambda b, g, col0=col0: (b + row_block0, col0 + g))

    def headrows(direction):
        return pl.BlockSpec((n_chunks, SSD_R, BLOCK),
                            lambda b, g, direction=direction: (b + row_block0, direction * SSD_GROUPS + g, 0))

    in_specs = [rowcol(GW, 0), rowcol(GW, xcol0), rowcol(SSD_N, bcol0), rowcol(SSD_N, ccol0)]
    args = [zx, zx, zx, zx]
    for arr in prep:
        in_specs += [headrows(0), headrows(1)]
        args += [arr, arr]
    in_specs += [pl.BlockSpec((CONV_W, GW), lambda b, g: (0, g)),
                 pl.BlockSpec((CONV_W, SSD_N), lambda b, g: (0, cwb0 + g)),
                 pl.BlockSpec((CONV_W, SSD_N), lambda b, g: (0, cwb0 + SSD_GROUPS + g)),
                 pl.BlockSpec((1, GW), lambda b, g: (0, g)),
                 pl.BlockSpec((1, SSD_N), lambda b, g: (0, cwb0 + g)),
                 pl.BlockSpec((1, SSD_N), lambda b, g: (0, cwb0 + SSD_GROUPS + g)),
                 pl.BlockSpec((1, GW), lambda b, g: (0, g)),
                 pl.BlockSpec((1, GW), lambda b, g: (0, g))]
    args += [conv_w, conv_w, conv_w, conv_b, conv_b, conv_b, d_exp, norm_gain.reshape(1, D_INNER)]
    state_spec = pl.BlockSpec((None, None, 2, SSD_R, SSD_N, SSD_P), lambda b, g: (b, 0, 0, g, 0, 0))
    if has_s0:
        in_specs.append(state_spec)
        args.append(s0)
    aliases = _fill_alias(fill, in_specs, args)
    out_specs = [pl.BlockSpec((seq_len, GW), lambda b, g: (b + row_block0, g)),
                 pl.BlockSpec((seq_len, LANES), lambda b, g: (b + row_block0, g))]
    out_shape = [jax.ShapeDtypeStruct((T_ALL, D_INNER), BF16),
                 jax.ShapeDtypeStruct((T_ALL, SSD_GROUPS * LANES), F32)]
    if emit_state:
        out_specs.append(state_spec)
        out_shape.append(jax.ShapeDtypeStruct((n_seq, 1, 2, SSD_HEADS, SSD_N, SSD_P), F32))
    return pl.pallas_call(
        functools.partial(_ssd_kernel, n_chunks, has_s0, emit_state, bool(fill)),
        grid=(n_seq, SSD_GROUPS),
        in_specs=in_specs,
        out_specs=out_specs,
        out_shape=out_shape,
        input_output_aliases=aliases,
        scratch_shapes=[pltpu.VMEM((seq_len + 2 * HALO, GW + 2 * SSD_N), F32),
                        pltpu.VMEM((seq_len, GW), F32), pltpu.VMEM((seq_len, SSD_N), BF16),
                        pltpu.VMEM((seq_len, SSD_N), BF16), pltpu.VMEM((n_chunks, SSD_N, GW), BF16),
                        pltpu.VMEM((SSD_N, GW), F32), pltpu.VMEM((SSD_N, GW), F32)],
        compiler_params=_cparams("arbitrary", "arbitrary"),
        name="ssd_scan_latent" if has_s0 else "ssd_scan_prompt",
    )(*args)


def _router_kernel(x_ref, g_ref, mod_ref, rw_ref, hn_ref, idx_ref, wgt_ref, hf_ref):
    _adaln_to(x_ref, g_ref, mod_ref, 3, 4, hf_ref)
    hf = hf_ref[...]
    h_hi = hf.astype(BF16)
    h_lo = (hf - h_hi.astype(F32)).astype(BF16)
    hn_ref[...] = h_hi
    rw = rw_ref[...]
    r_hi = rw.astype(BF16)
    r_lo = (rw - r_hi.astype(F32)).astype(BF16)
    t1 = jnp.dot(h_hi, jnp.concatenate([r_hi, r_lo], axis=1), preferred_element_type=F32)
    logits = t1[:, :LANES] + t1[:, LANES:] + jnp.dot(h_lo, r_hi, preferred_element_type=F32)
    lane = lax.broadcasted_iota(jnp.int32, logits.shape, 1)
    lg = jnp.where(lane < N_EXPERTS, logits, NEG_INF)
    m1 = jnp.max(lg, axis=-1, keepdims=True)
    i1 = jnp.min(jnp.where(lg == m1, lane, LANES), axis=-1, keepdims=True)
    lg2 = jnp.where(lane == i1, NEG_INF, lg)
    m2 = jnp.max(lg2, axis=-1, keepdims=True)
    i2 = jnp.min(jnp.where(lg2 == m2, lane, LANES), axis=-1, keepdims=True)
    e2 = jnp.exp(m2 - m1)
    w1 = 1.0 / (1.0 + e2)
    idx_ref[...] = jnp.where(lane == 0, i1, jnp.where(lane == 1, i2, 0))
    wgt_ref[...] = jnp.where(lane == 0, w1, jnp.where(lane == 1, e2 * w1, 0.0))


def _router(x, gain, modt, router_w):
    t, d = x.shape
    rw = jnp.pad(router_w, ((0, 0), (0, LANES - N_EXPERTS)))
    return pl.pallas_call(
        _router_kernel,
        grid=(t // ROW_TILE,),
        in_specs=[pl.BlockSpec((ROW_TILE, d), lambda i: (i, 0)),
                  pl.BlockSpec((1, d), lambda i: (0, 0)),
                  pl.BlockSpec((None, 8, d), lambda i: (i, 0, 0)),
                  pl.BlockSpec((d, LANES), lambda i: (0, 0))],
        out_specs=[pl.BlockSpec((ROW_TILE, d), lambda i: (i, 0)),
                   pl.BlockSpec((ROW_TILE, LANES), lambda i: (i, 0)),
                   pl.BlockSpec((ROW_TILE, LANES), lambda i: (i, 0))],
        out_shape=[jax.ShapeDtypeStruct((t, d), BF16), jax.ShapeDtypeStruct((t, LANES), jnp.int32),
                   jax.ShapeDtypeStruct((t, LANES), F32)],
        scratch_shapes=[pltpu.VMEM((ROW_TILE, d), F32)],
        compiler_params=_cparams("arbitrary"),
        name="moe_router",
    )(x, gain.reshape(1, d), modt, rw)


DOWN_ROWS = 512
DOWN_TN = 512


def _expert_changed(be_ref, blk, prev_blk, step):
    return jnp.logical_or(step == 0, be_ref[blk] != be_ref[prev_blk])


def _expert_up_kernel(be_ref, nu_ref, nv_ref, xs_ref, wg_ref, wu_ref, h_ref, wgb_ref, wub_ref):
    i = pl.program_id(1)
    used = nv_ref[i] > 0

    @pl.when(jnp.logical_and(used, _expert_changed(be_ref, i, jnp.maximum(i - 1, 0), i)))
    def _():
        wgb_ref[...] = wg_ref[...].astype(BF16)
        wub_ref[...] = wu_ref[...].astype(BF16)

    @pl.when(used)
    def _():
        xs = xs_ref[...]
        h_ref[...] = (_silu(_bdot(xs, wgb_ref[...])) * _bdot(xs, wub_ref[...])).astype(h_ref.dtype)

    @pl.when(jnp.logical_not(used))
    def _():
        h_ref[...] = jnp.zeros_like(h_ref)


def _expert_down_kernel(be_ref, nu_ref, nv_ref, h_ref, wd_ref, o_ref, wdb_ref):
    i = pl.program_id(1)
    per = MOE_ROWS // DOWN_ROWS
    blk = i // per
    used = nv_ref[blk] > (i % per) * DOWN_ROWS

    @pl.when(jnp.logical_and(nv_ref[blk] > 0, _expert_changed(be_ref, blk, jnp.maximum(i - 1, 0) // per, i)))
    def _():
        wdb_ref[...] = wd_ref[...].astype(BF16)

    @pl.when(used)
    def _():
        o_ref[...] = _bdot(h_ref[...], wdb_ref[...]).astype(o_ref.dtype)

    @pl.when(jnp.logical_not(used))
    def _():
        o_ref[...] = jnp.zeros_like(o_ref)


def _experts(xs_sorted, block_e, n_used, n_valid, wg, wu, wd):
    cap, d = xs_sorted.shape
    ff = wg.shape[2]
    per = MOE_ROWS // DOWN_ROWS

    def expert_of(blk, be, nu):
        return be[jnp.minimum(blk, jnp.maximum(nu[0] - 1, 0))]

    h = pl.pallas_call(
        _expert_up_kernel,
        grid_spec=pltpu.PrefetchScalarGridSpec(
            num_scalar_prefetch=3,
            grid=(ff // FFN_TF, cap // MOE_ROWS),
            in_specs=[pl.BlockSpec((MOE_ROWS, d), lambda f, i, be, nu, nv: (i, 0)),
                      pl.BlockSpec((None, d, FFN_TF), lambda f, i, be, nu, nv: (expert_of(i, be, nu), 0, f)),
                      pl.BlockSpec((None, d, FFN_TF), lambda f, i, be, nu, nv: (expert_of(i, be, nu), 0, f))],
            out_specs=pl.BlockSpec((MOE_ROWS, FFN_TF), lambda f, i, be, nu, nv: (i, f)),
            scratch_shapes=[pltpu.VMEM((d, FFN_TF), BF16), pltpu.VMEM((d, FFN_TF), BF16)],
        ),
        out_shape=jax.ShapeDtypeStruct((cap, ff), BF16),
        compiler_params=_cparams("arbitrary", "arbitrary"),
        name="moe_expert_up",
    )(block_e, n_used, n_valid, xs_sorted, wg, wu)
    return pl.pallas_call(
        _expert_down_kernel,
        grid_spec=pltpu.PrefetchScalarGridSpec(
            num_scalar_prefetch=3,
            grid=(d // DOWN_TN, cap // DOWN_ROWS),
            in_specs=[pl.BlockSpec((DOWN_ROWS, ff), lambda n, i, be, nu, nv: (i, 0)),
                      pl.BlockSpec((None, ff, DOWN_TN),
                                   lambda n, i, be, nu, nv: (expert_of(i // per, be, nu), 0, n))],
            out_specs=pl.BlockSpec((DOWN_ROWS, DOWN_TN), lambda n, i, be, nu, nv: (i, n)),
            scratch_shapes=[pltpu.VMEM((ff, DOWN_TN), BF16)],
        ),
        out_shape=jax.ShapeDtypeStruct((cap, d), BF16),
        compiler_params=_cparams("arbitrary", "arbitrary"),
        name="moe_expert_down",
    )(block_e, n_used, n_valid, h, wd)


def _combine_kernel(x_ref, g0_ref, g1_ref, w_ref, mod_ref, o_ref):
    w = w_ref[...]
    y = w[:, 0:1] * g0_ref[...].astype(F32) + w[:, 1:2] * g1_ref[...].astype(F32)
    o_ref[...] = x_ref[...] + mod_ref[5:6, :] * y


COMB_ROWS = 512


def _combine(x, g, wgt, modt, row0, n_rows):
    t, d = x.shape
    b0 = row0 // COMB_ROWS
    per = ROW_TILE // COMB_ROWS
    return pl.pallas_call(
        _combine_kernel,
        grid=(n_rows // COMB_ROWS,),
        in_specs=[pl.BlockSpec((COMB_ROWS, d), lambda i: (i + b0, 0)),
                  pl.BlockSpec((COMB_ROWS, d), lambda i: (i + b0, 0)),
                  pl.BlockSpec((COMB_ROWS, d), lambda i: (i + b0 + t // COMB_ROWS, 0)),
                  pl.BlockSpec((COMB_ROWS, LANES), lambda i: (i + b0, 0)),
                  pl.BlockSpec((None, 8, d), lambda i: ((i + b0) // per, 0, 0))],
        out_specs=pl.BlockSpec((COMB_ROWS, d), lambda i: (i, 0)),
        out_shape=jax.ShapeDtypeStruct((n_rows, d), F32),
        compiler_params=_cparams("arbitrary"),
        name="moe_combine",
    )(x, g, g, wgt, modt)


def _moe(x, gain, modt, router_w, wg, wu, wd):
    t, d = x.shape
    hn, idx, wgt = _router(x, gain, modt, router_w)
    top_idx = idx[:, :TOP_K]
    n_slots = t * TOP_K
    flat_e = top_idx.reshape(-1)
    onehot = (flat_e[:, None] == jnp.arange(N_EXPERTS, dtype=jnp.int32)[None, :]).astype(jnp.int32)
    incl = jnp.cumsum(onehot, axis=0)
    counts = incl[-1]
    rank = jnp.sum((incl - onehot) * onehot, axis=1)
    padded = (counts + MOE_ROWS - 1) // MOE_ROWS * MOE_ROWS
    pend = jnp.cumsum(padded)
    pstart = pend - padded
    dest = pstart[flat_e] + rank
    n_blocks = n_slots // MOE_ROWS + N_EXPERTS
    cap = n_blocks * MOE_ROWS
    row_tok = (jnp.arange(cap, dtype=jnp.int32) % t).at[dest].set(jnp.arange(n_slots, dtype=jnp.int32) // TOP_K)
    block_e = jnp.clip(jnp.searchsorted(pend, jnp.arange(n_blocks, dtype=jnp.int32) * MOE_ROWS, side='right'),
                       0, N_EXPERTS - 1).astype(jnp.int32)
    n_used = (pend[-1:] // MOE_ROWS).astype(jnp.int32)
    blk = jnp.arange(n_blocks, dtype=jnp.int32)
    n_valid = jnp.where(blk < n_used[0],
                        jnp.clip((pstart + counts)[block_e] - blk * MOE_ROWS, 0, MOE_ROWS), 0).astype(jnp.int32)
    out = _experts(hn[row_tok], block_e, n_used, n_valid, wg, wu, wd)
    g = out[dest.reshape(t, TOP_K).T.reshape(-1)]
    return _combine(x, g, wgt, modt, 0, T_PROMPT), _combine(x, g, wgt, modt, T_PROMPT, T_SAMPLE)


def _rope_tables(n_tokens, dim):
    n_rows = n_tokens // GRID_W
    row = jnp.repeat(jnp.arange(n_rows), GRID_W).astype(F32)
    col = jnp.tile(jnp.arange(GRID_W), n_rows).astype(F32)
    n_freq = dim // 4
    inv = ROPE_BASE ** (-jnp.arange(n_freq, dtype=F32) / n_freq)
    ang = jnp.concatenate([row[:, None] * inv, col[:, None] * inv], axis=-1)
    return jnp.cos(ang), jnp.sin(ang)


def kernel(x_prompt, x_sample, state_ret, cache_k, cache_v, state_ssd, c, c_ctx, ada_w, ada_b, norm_mix, norm_ffn, ev_w_in, ev_w_out, ret_decay_logit, ret_norm, att_q_norm, att_k_norm, att_sink, ffn_w_gate, ffn_w_up, ffn_w_down, ssd_w_in, ssd_conv_w, ssd_conv_b, ssd_a_log, ssd_dt_bias, ssd_d, ssd_norm, ssd_w_out, moe_router, moe_w_gate, moe_w_up, moe_w_down):
    d = D_MODEL
    x = (x_prompt.reshape(T_PROMPT, d), x_sample.reshape(T_SAMPLE, d))

    cvecs = jnp.concatenate([c_ctx[None, :], c, jnp.zeros((MOD_ROWS - 1 - DEC_BATCH, d), F32)], axis=0)
    mods = _modulation(cvecs, ada_w, ada_b).reshape(2, MOD_ROWS, 6, d)
    tiles_per_seq = DEC_SEQ // ROW_TILE
    tile_row = jnp.concatenate([jnp.zeros((T_PROMPT // ROW_TILE,), jnp.int32),
                                1 + jnp.arange(T_SAMPLE // ROW_TILE, dtype=jnp.int32) // tiles_per_seq])
    modt = jnp.pad(mods[:, tile_row], ((0, 0), (0, 0), (0, 2), (0, 0)))

    ev_w_in_b, ev_w_out_b = ev_w_in[0].astype(BF16), ev_w_out[0].astype(BF16)
    ffn_wg_b, ffn_wu_b, ffn_wd_b = ffn_w_gate[0].astype(BF16), ffn_w_up[0].astype(BF16), ffn_w_down[0].astype(BF16)
    ssd_w_in_b, ssd_w_out_b = ssd_w_in[0].astype(BF16), ssd_w_out[0].astype(BF16)

    proj, = _adaln_matmul(x, norm_mix[0], modt[0], 0, 1, ev_w_in_b, tn=512, out_dtype=BF16, name="even_in_proj")
    lg = jax.nn.log_sigmoid(ret_decay_logit[0].astype(F32))
    cos_r, sin_r = _rope_tables(DEC_SEQ, RET_DK)
    cos_a, sin_a = _rope_tables(DEC_SEQ, ATT_HD)
    cos_a2 = jnp.concatenate([cos_a, cos_a], axis=-1)
    sin_a2 = jnp.concatenate([-sin_a, sin_a], axis=-1)
    mix_ret, new_state_ret = _retention(proj, lg, ret_norm[0], SEQ, BATCH, 0, emit_state=True)
    mix_ret, = _retention(proj, lg, ret_norm[0], DEC_SEQ, DEC_BATCH, T_PROMPT // DEC_SEQ,
                          ropes=(cos_r, sin_r), s0=state_ret, fill=(mix_ret,))
    mix_att, new_k, new_v = _attention_prompt(proj, att_sink[0], att_q_norm[0], att_k_norm[0])
    mix_att = _attention_latent(proj, cache_k[:, 0], cache_v[:, 0], att_sink[0], cos_a2, sin_a2,
                                att_q_norm[0], att_k_norm[0], mix_att)
    x = _proj_residual([mix_ret, mix_att], ev_w_out_b, x, modt[0], 2, tn=1024, name="even_out_proj")
    h = _ffn_gateup(x, norm_ffn[0], modt[0], ffn_wg_b, ffn_wu_b)
    x = _proj_residual([h], ffn_wd_b, x, modt[0], 5, tn=512, name="ffn_down")

    zx, dt_raw = _adaln_matmul(x, norm_mix[1], modt[1], 0, 1, ssd_w_in_b, tn=1024, n_out=SSD_ZX,
                               tail=2 * SSD_HEADS, out_dtype=BF16, name="ssd_in_proj")
    prep = _ssd_prep(dt_raw, ssd_dt_bias[0], ssd_a_log[0])
    d_exp = jnp.repeat(ssd_d[0], SSD_P)[None, :]
    conv_b = ssd_conv_b[0][None, :]
    yg, yss, new_state_ssd = _ssd_scan(zx, prep, ssd_conv_w[0], conv_b, d_exp, ssd_norm[0], SEQ, BATCH, 0,
                                       emit_state=True)
    yg, yss = _ssd_scan(zx, prep, ssd_conv_w[0], conv_b, d_exp, ssd_norm[0], DEC_SEQ, DEC_BATCH,
                        T_PROMPT // DEC_SEQ, s0=state_ssd, fill=(yg, yss))
    x = _proj_residual([yg], ssd_w_out_b, x, modt[1], 2, tn=512, row_ss=yss, name="ssd_out_proj")
    y_p, y_s = _moe(x, norm_ffn[1], modt[1], moe_router[0], moe_w_gate[0], moe_w_up[0], moe_w_down[0])

    y_prompt = y_p.reshape(BATCH, SEQ, d)
    y_sample = y_s.reshape(DEC_BATCH, DEC_SEQ, d)
    new_cache_k = new_k.reshape(BATCH, 1, SEQ, ATT_KV_HEADS, ATT_HD)
    new_cache_v = new_v.reshape(BATCH, 1, SEQ, ATT_KV_HEADS, ATT_HD)
    return (y_prompt, y_sample, new_state_ret, new_cache_k, new_cache_v, new_state_ssd)
```

```python
import functools

import jax
import jax.numpy as jnp
from jax import lax
from jax.experimental import pallas as pl
from jax.experimental.pallas import tpu as pltpu

F32 = jnp.float32
BF16 = jnp.bfloat16

D_MODEL = 2048
BATCH = 16
SEQ = 256
DEC_BATCH = 8
DEC_SEQ = 2048
PAST_LEN = 512
GRID_W = 64
BLOCK = 128
WINDOW = 128
EPS = 1e-6
ROPE_BASE = 10000.0
RET_HEADS = 4
RET_DK = 256
RET_DV = 256
ATT_HEADS = 8
ATT_KV_HEADS = 2
ATT_HD = 128
ATT_GROUP = ATT_HEADS // ATT_KV_HEADS
EVEN_IN = 5632
D_INNER = 2 * D_MODEL
SSD_P = 64
SSD_HEADS = D_INNER // SSD_P
SSD_N = 128
SSD_GROUPS = 8
SSD_R = SSD_HEADS // SSD_GROUPS
CONV_W = 5
CONV_CH = D_INNER + 2 * SSD_GROUPS * SSD_N
SSD_ZX = D_INNER + CONV_CH
D_FF = 5632
N_EXPERTS = 8
TOP_K = 2

T_PROMPT = BATCH * SEQ
T_SAMPLE = DEC_BATCH * DEC_SEQ
T_ALL = T_PROMPT + T_SAMPLE

LANES = 128
ROW_TILE = 1024
VMEM_LIMIT = 56 * 1024 * 1024
N_ROW_TILES = T_ALL // ROW_TILE
MOE_ROWS = 1024
NEG_INF = float("-inf")


def _cparams(*sem):
    return pltpu.CompilerParams(dimension_semantics=sem, vmem_limit_bytes=VMEM_LIMIT)


def _silu(x):
    return x * jax.nn.sigmoid(x)


def _bdot(a, b):
    return jnp.dot(a.astype(BF16), b.astype(BF16), preferred_element_type=F32)


def _bdot_nt(a, b):
    return lax.dot_general(a.astype(BF16), b.astype(BF16), (((1,), (1,)), ((), ())),
                           preferred_element_type=F32)


def _bdot_tn(a, b):
    return lax.dot_general(a.astype(BF16), b.astype(BF16), (((0,), (0,)), ((), ())),
                           preferred_element_type=F32)


def _rows(c, n=BLOCK):
    return pl.ds(pl.multiple_of(c * n, n), n)


MOD_ROWS = 16
MOD_TN = 1024


def _mod_kernel(c_ref, w_ref, b_ref, o_ref):
    o_ref[...] = _bdot(_silu(c_ref[...]), w_ref[...]) + b_ref[...]


def _modulation(cvecs, ada_w, ada_b):
    depth, d, n = ada_w.shape
    return pl.pallas_call(
        _mod_kernel,
        grid=(depth, n // MOD_TN),
        in_specs=[pl.BlockSpec((MOD_ROWS, d), lambda l, j: (0, 0)),
                  pl.BlockSpec((None, d, MOD_TN), lambda l, j: (l, 0, j)),
                  pl.BlockSpec((None, 1, MOD_TN), lambda l, j: (l, 0, j))],
        out_specs=pl.BlockSpec((None, MOD_ROWS, MOD_TN), lambda l, j: (l, 0, j)),
        out_shape=jax.ShapeDtypeStruct((depth, MOD_ROWS, n), F32),
        compiler_params=_cparams("arbitrary", "arbitrary"),
        name="modulation",
    )(cvecs, ada_w, ada_b.reshape(depth, 1, n))


ADALN_CHUNK = 64


def _adaln_to(x_ref, g_ref, mod_ref, shift_row, scale_row, hn_ref):
    mult = g_ref[...] * (1.0 + mod_ref[scale_row:scale_row + 1, :])
    shift = mod_ref[shift_row:shift_row + 1, :]

    def body(i, carry):
        r = _rows(i, ADALN_CHUNK)
        x = x_ref[r, :]
        ms = jnp.mean(x * x, axis=-1, keepdims=True)
        hn_ref[r, :] = (x * lax.rsqrt(ms + EPS) * mult + shift).astype(hn_ref.dtype)
        return carry

    lax.fori_loop(0, x_ref.shape[0] // ADALN_CHUNK, body, 0)


PROMPT_TILES = T_PROMPT // ROW_TILE


def _x_rows_specs(x, cols, col_of_j):
    if not isinstance(x, tuple):
        return [pl.BlockSpec((ROW_TILE, cols), lambda i, j: (i, col_of_j(j)))], [x]
    return ([pl.BlockSpec((ROW_TILE, cols),
                          lambda i, j: (jnp.minimum(i, PROMPT_TILES - 1), jnp.where(i < PROMPT_TILES, col_of_j(j), 0))),
             pl.BlockSpec((ROW_TILE, cols),
                          lambda i, j: (jnp.maximum(i - PROMPT_TILES, 0), jnp.where(i < PROMPT_TILES, 0, col_of_j(j))))],
            list(x))


def _for_x_rows(x_refs, fn):
    if len(x_refs) == 1:
        fn(x_refs[0])
        return
    in_prompt = pl.program_id(0) < PROMPT_TILES
    pl.when(in_prompt)(lambda: fn(x_refs[0]))
    pl.when(jnp.logical_not(in_prompt))(lambda: fn(x_refs[1]))


def _adaln_mm_kernel(shift_row, scale_row, has_tail, n_x, *refs):
    x_refs, (g_ref, mod_ref, w_ref), refs = refs[:n_x], refs[n_x:n_x + 3], refs[n_x + 3:]
    if has_tail:
        wt_ref, o_ref, ot_ref, hn_ref = refs
    else:
        o_ref, hn_ref = refs

    @pl.when(pl.program_id(1) == 0)
    def _():
        _for_x_rows(x_refs, lambda x_ref: _adaln_to(x_ref, g_ref, mod_ref, shift_row, scale_row, hn_ref))
        if has_tail:
            ot_ref[...] = _bdot(hn_ref[...], wt_ref[...])

    o_ref[...] = _bdot(hn_ref[...], w_ref[...]).astype(o_ref.dtype)


def _adaln_matmul(x, gain, modt, shift_row, scale_row, w, tn, n_out=None, tail=0, out_dtype=F32, name="adaln_mm"):
    t, d = T_ALL, w.shape[0]
    n_out = w.shape[1] if n_out is None else n_out
    in_specs, args = _x_rows_specs(x, d, lambda j: 0)
    n_x = len(args)
    in_specs += [pl.BlockSpec((1, d), lambda i, j: (0, 0)),
                 pl.BlockSpec((None, 8, d), lambda i, j: (i, 0, 0)),
                 pl.BlockSpec((d, tn), lambda i, j: (0, j))]
    args += [gain.reshape(1, d), modt, w]
    out_specs = [pl.BlockSpec((ROW_TILE, tn), lambda i, j: (i, j))]
    out_shape = [jax.ShapeDtypeStruct((t, n_out), out_dtype)]
    if tail:
        in_specs.append(pl.BlockSpec((d, tail), lambda i, j: (0, n_out // tail)))
        args.append(w)
        out_specs.append(pl.BlockSpec((ROW_TILE, tail), lambda i, j: (i, 0)))
        out_shape.append(jax.ShapeDtypeStruct((t, tail), F32))
    return pl.pallas_call(
        functools.partial(_adaln_mm_kernel, shift_row, scale_row, bool(tail), n_x),
        grid=(t // ROW_TILE, n_out // tn),
        in_specs=in_specs,
        out_specs=out_specs,
        out_shape=out_shape,
        scratch_shapes=[pltpu.VMEM((ROW_TILE, d), BF16)],
        compiler_params=_cparams("arbitrary", "arbitrary"),
        name=name,
    )(*args)


FFN_TF = 512


def _gateup_kernel(x_ref, g_ref, mod_ref, wg_ref, wu_ref, h_ref, hn_ref):
    @pl.when(pl.program_id(1) == 0)
    def _():
        _adaln_to(x_ref, g_ref, mod_ref, 3, 4, hn_ref)

    hn = hn_ref[...]
    h_ref[...] = (_silu(_bdot(hn, wg_ref[...])) * _bdot(hn, wu_ref[...])).astype(h_ref.dtype)


def _ffn_gateup(x, gain, modt, wg, wu):
    t, d = x.shape
    ff = wg.shape[1]
    return pl.pallas_call(
        _gateup_kernel,
        grid=(t // ROW_TILE, ff // FFN_TF),
        in_specs=[pl.BlockSpec((ROW_TILE, d), lambda i, f: (i, 0)),
                  pl.BlockSpec((1, d), lambda i, f: (0, 0)),
                  pl.BlockSpec((None, 8, d), lambda i, f: (i, 0, 0)),
                  pl.BlockSpec((d, FFN_TF), lambda i, f: (0, f)),
                  pl.BlockSpec((d, FFN_TF), lambda i, f: (0, f))],
        out_specs=pl.BlockSpec((ROW_TILE, FFN_TF), lambda i, f: (i, f)),
        out_shape=jax.ShapeDtypeStruct((t, ff), BF16),
        scratch_shapes=[pltpu.VMEM((ROW_TILE, d), BF16)],
        compiler_params=_cparams("arbitrary", "arbitrary"),
        name="ffn_gateup",
    )(x, gain.reshape(1, d), modt, wg, wu)


def _proj_res_kernel(n_a, n_x, gate_row, norm, *refs):
    a_refs = refs[:n_a]
    w_refs = refs[n_a:2 * n_a]
    x_refs, mod_ref = refs[2 * n_a:2 * n_a + n_x], refs[2 * n_a + n_x]
    pos = 2 * n_a + n_x + 1
    o_ref = refs[pos + 1] if norm else refs[pos]
    acc = _bdot(a_refs[0][...], w_refs[0][...])
    for k in range(1, n_a):
        acc = acc + _bdot(a_refs[k][...], w_refs[k][...])
    if norm:
        ss_ref = refs[pos]
        k_total = sum(a.shape[1] for a in a_refs)
        ss = ss_ref[:, 0:LANES]
        for k in range(1, ss_ref.shape[1] // LANES):
            ss = ss + ss_ref[:, k * LANES:(k + 1) * LANES]
        rs = lax.rsqrt(ss * (1.0 / k_total) + EPS)
        acc = acc * jnp.concatenate([rs] * (acc.shape[1] // LANES), axis=1)
    upd = mod_ref[gate_row:gate_row + 1, :] * acc

    def finish(x_ref):
        o_ref[...] = x_ref[...] + upd

    _for_x_rows(x_refs, finish)


def _proj_residual(a_list, w, x, modt, gate_row, tn, row_ss=None, name="proj_res"):
    t, d = T_ALL, w.shape[1]
    n_a = len(a_list)
    norm = row_ss is not None
    in_specs, args, k0 = [], [], 0
    for a in a_list:
        in_specs.append(pl.BlockSpec((ROW_TILE, a.shape[1]), lambda i, j: (i, 0)))
        args.append(a)
    for a in a_list:
        ka = a.shape[1]
        assert k0 % ka == 0
        in_specs.append(pl.BlockSpec((ka, tn), lambda i, j, kb=k0 // ka: (kb, j)))
        args.append(w)
        k0 += ka
    x_specs, x_args = _x_rows_specs(x, tn, lambda j: j)
    in_specs += x_specs + [pl.BlockSpec((None, 8, tn), lambda i, j: (i, 0, j))]
    args += x_args + [modt]
    if norm:
        in_specs.append(pl.BlockSpec((ROW_TILE, row_ss.shape[1]), lambda i, j: (i, 0)))
        args.append(row_ss)
    return pl.pallas_call(
        functools.partial(_proj_res_kernel, n_a, len(x_args), gate_row, norm),
        grid=(t // ROW_TILE, d // tn),
        in_specs=in_specs,
        out_specs=pl.BlockSpec((ROW_TILE, tn), lambda i, j: (i, j)),
        out_shape=jax.ShapeDtypeStruct((t, d), F32),
        compiler_params=_cparams("arbitrary", "arbitrary"),
        name=name,
    )(*args)


def _ret_kernel(n_chunks, rope, has_s0, emit_state, has_fill, lg_ref, *refs):
    it = iter(refs)
    q_ref, k_ref, v_ref, gt_ref = next(it), next(it), next(it), next(it)
    cos_ref, sin_ref = (next(it), next(it)) if rope else (None, None)
    s0_ref = next(it) if has_s0 else None
    gain_ref = next(it)
    if has_fill:
        next(it)
    o_ref = next(it)
    sfin_ref = next(it) if emit_state else None
    qs_ref, ks_ref, sfs_ref, dm_ref, dec_ref, sf_ref, sb_ref = (next(it) for _ in range(7))

    h = pl.program_id(1)
    lgf = lg_ref[0, h]
    lgb = lg_ref[1, h]
    half = RET_DK // 2

    def prep(c, carry):
        r = _rows(c)
        q = q_ref[r, :].astype(F32)
        k = k_ref[r, :].astype(F32) * (RET_DK ** -0.5)
        if rope:
            cs, sn = cos_ref[r, :], sin_ref[r, :]
            for src, dst in ((q, qs_ref), (k, ks_ref)):
                x1, x2 = src[:, :half], src[:, half:]
                dst[r, :half] = (x1 * cs - x2 * sn).astype(BF16)
                dst[r, half:] = (x1 * sn + x2 * cs).astype(BF16)
        else:
            qs_ref[r, :] = q.astype(BF16)
            ks_ref[r, :] = k.astype(BF16)
        return carry

    lax.fori_loop(0, n_chunks, prep, 0)

    ii = lax.broadcasted_iota(jnp.int32, (BLOCK, BLOCK), 0)
    jj = lax.broadcasted_iota(jnp.int32, (BLOCK, BLOCK), 1)
    diff = (ii - jj).astype(F32)
    dm_ref[...] = jnp.exp(jnp.where(jj <= ii, diff * lgf, -diff * lgb))
    pos = lax.broadcasted_iota(jnp.int32, (BLOCK, RET_DV), 0).astype(F32)
    dec_ref[0] = jnp.exp((pos + 1.0) * lgf)
    dec_ref[1] = jnp.exp((BLOCK - pos) * lgb)
    dec_ref[2] = jnp.exp((BLOCK - 1.0 - pos) * lgf)
    dec_ref[3] = jnp.exp(pos * lgb)
    tot_f = jnp.exp(jnp.full((1, RET_DV), BLOCK * lgf, F32))
    tot_b = jnp.exp(jnp.full((1, RET_DV), BLOCK * lgb, F32))

    if has_s0:
        sf_ref[...] = s0_ref[0]
        sb_ref[...] = s0_ref[1]
    else:
        sf_ref[...] = jnp.zeros_like(sf_ref)
        sb_ref[...] = jnp.zeros_like(sb_ref)

    def fwd(c, carry):
        r = _rows(c)
        sfs_ref[c] = sf_ref[...].astype(BF16)
        kd = ks_ref[r, :].astype(F32) * dec_ref[2]
        sf_ref[...] = sf_ref[...] * tot_f + _bdot_tn(kd, v_ref[r, :])
        return carry

    lax.fori_loop(0, n_chunks, fwd, 0, unroll=min(4, n_chunks))
    if emit_state:
        sfin_ref[0] = sf_ref[...]

    def bwd(t, carry):
        c = n_chunks - 1 - t
        r = _rows(c)
        q = qs_ref[r, :]
        k = ks_ref[r, :]
        v = v_ref[r, :].astype(BF16)
        p = _bdot_nt(q, k) * dm_ref[...]
        o = _bdot(p, v)
        o = o + _bdot(q, sfs_ref[c]) * dec_ref[0]
        o = o + _bdot(q, sb_ref[...]) * dec_ref[1]
        ms = jnp.mean(o * o, axis=-1, keepdims=True)
        y = o * lax.rsqrt(ms + EPS) * gain_ref[...]
        o_ref[r, :] = (y * _silu(gt_ref[r, :].astype(F32))).astype(o_ref.dtype)
        kd = k.astype(F32) * dec_ref[3]
        sb_ref[...] = sb_ref[...] * tot_b + _bdot_tn(kd, v)
        return carry

    lax.fori_loop(0, n_chunks, bwd, 0, unroll=min(4, n_chunks))
    if emit_state:
        sfin_ref[1] = sb_ref[...]


def _fill_alias(fill, in_specs, args):
    aliases = {}
    for k, arr in enumerate(fill or ()):
        in_specs.append(pl.BlockSpec(memory_space=pl.ANY))
        args.append(arr)
        aliases[len(args) - 1] = k
    return aliases


def _retention(proj, lg, ret_norm, seq_len, n_seq, row_block0, ropes=None, s0=None, emit_state=False, fill=None):
    n_chunks = seq_len // BLOCK
    rope, has_s0 = ropes is not None, s0 is not None

    def col(cb):
        return pl.BlockSpec((seq_len, RET_DK), lambda b, h, cb=cb: (b + row_block0, cb * RET_HEADS + h))

    in_specs = [pl.BlockSpec(memory_space=pltpu.SMEM), col(0), col(1), col(2), col(3)]
    args = [lg, proj, proj, proj, proj]
    if rope:
        in_specs += [pl.BlockSpec((seq_len, RET_DK // 2), lambda b, h: (0, 0))] * 2
        args += list(ropes)
    if has_s0:
        in_specs.append(pl.BlockSpec((None, None, 2, None, RET_DK, RET_DV), lambda b, h: (b, 0, 0, h, 0, 0)))
        args.append(s0)
    in_specs.append(pl.BlockSpec((1, RET_DV), lambda b, h: (0, h)))
    args.append(ret_norm.reshape(1, RET_HEADS * RET_DV))
    aliases = _fill_alias(fill, in_specs, args)
    out_specs = [pl.BlockSpec((seq_len, RET_DV), lambda b, h: (b + row_block0, h))]
    out_shape = [jax.ShapeDtypeStruct((T_ALL, RET_HEADS * RET_DV), BF16)]
    if emit_state:
        out_specs.append(pl.BlockSpec((None, None, 2, None, RET_DK, RET_DV), lambda b, h: (b, 0, 0, h, 0, 0)))
        out_shape.append(jax.ShapeDtypeStruct((n_seq, 1, 2, RET_HEADS, RET_DK, RET_DV), F32))
    return pl.pallas_call(
        functools.partial(_ret_kernel, n_chunks, rope, has_s0, emit_state, bool(fill)),
        grid=(n_seq, RET_HEADS),
        in_specs=in_specs,
        out_specs=out_specs,
        out_shape=out_shape,
        input_output_aliases=aliases,
        scratch_shapes=[pltpu.VMEM((seq_len, RET_DK), BF16), pltpu.VMEM((seq_len, RET_DK), BF16),
                        pltpu.VMEM((n_chunks, RET_DK, RET_DV), BF16), pltpu.VMEM((BLOCK, BLOCK), F32),
                        pltpu.VMEM((4, BLOCK, RET_DV), F32), pltpu.VMEM((RET_DK, RET_DV), F32),
                        pltpu.VMEM((RET_DK, RET_DV), F32)],
        compiler_params=_cparams("arbitrary", "arbitrary"),
        name="retention_latent" if rope else "retention_prompt",
    )(*args)


def _head_norm(x, gain):
    x = x.astype(F32)
    return x * lax.rsqrt(jnp.mean(x * x, axis=-1, keepdims=True) + EPS) * gain


def _rope_full(x, cs, sn):
    return x * cs + pltpu.roll(x, ATT_HD // 2, 1) * sn


ATT_SPLIT = 2


def _sink_column(sink_ref, head0, n_heads, rows_per_head):
    n = n_heads * rows_per_head
    head = lax.broadcasted_iota(jnp.int32, (n, 1), 0) // rows_per_head
    col = jnp.full((n, 1), sink_ref[head0], F32)
    for g in range(1, n_heads):
        col = jnp.where(head == g, sink_ref[head0 + g], col)
    return col


def _att_latent_kernel(sink_ref, q_ref, k_ref, v_ref, ck_ref, cv_ref, cos_ref, sin_ref, qn_ref, kn_ref,
                       fill_ref, o_ref, kp_ref, vp_ref, ckp_ref, cvp_ref):
    del fill_ref
    kh, qb = pl.program_id(1), pl.program_id(2)
    n_chunks = DEC_SEQ // BLOCK
    loc = 3 * BLOCK

    @pl.when(qb == 0)
    def _():
        def prep(c, carry):
            r = _rows(c)
            kp_ref[r, :] = _rope_full(_head_norm(k_ref[r, :], kn_ref[...]), cos_ref[r, :], sin_ref[r, :]).astype(BF16)
            vp_ref[r, :ATT_HD] = v_ref[r, :].astype(BF16)
            vp_ref[r, ATT_HD:] = jnp.ones((BLOCK, ATT_HD), BF16)
            return carry

        lax.fori_loop(0, n_chunks, prep, 0)
        ckp_ref[...] = ck_ref[...].astype(BF16)
        cvp_ref[:, :ATT_HD] = cv_ref[...].astype(BF16)
        cvp_ref[:, ATT_HD:] = jnp.ones((PAST_LEN, ATT_HD), BF16)

    rq = _rows(qb)
    cs, sn = cos_ref[rq, :], sin_ref[rq, :]
    qg = qn_ref[...] * (ATT_HD ** -0.5)
    start = pl.multiple_of(jnp.clip((qb - 1) * BLOCK, 0, DEC_SEQ - loc), BLOCK)
    kl, vl = kp_ref[pl.ds(start, loc), :], vp_ref[pl.ds(start, loc), :]
    qpos = qb * BLOCK + (lax.broadcasted_iota(jnp.int32, (ATT_SPLIT * BLOCK, loc), 0) & (BLOCK - 1))
    kpos = start + lax.broadcasted_iota(jnp.int32, (ATT_SPLIT * BLOCK, loc), 1)
    in_window = jnp.abs(qpos - kpos) <= WINDOW
    for g0 in range(0, ATT_GROUP, ATT_SPLIT):
        q = jnp.concatenate(
            [_rope_full(_head_norm(q_ref[:, g * ATT_HD:(g + 1) * ATT_HD], qg), cs, sn).astype(BF16)
             for g in range(g0, g0 + ATT_SPLIT)], axis=0)
        s_loc = jnp.where(in_window, _bdot_nt(q, kl), NEG_INF)
        s_ctx = _bdot_nt(q, ckp_ref[...])
        sink = _sink_column(sink_ref, kh * ATT_GROUP + g0, ATT_SPLIT, BLOCK)
        m = jnp.maximum(jnp.maximum(jnp.max(s_loc, axis=-1, keepdims=True),
                                    jnp.max(s_ctx, axis=-1, keepdims=True)), sink)
        pv = _bdot(jnp.exp(s_loc - m), vl) + _bdot(jnp.exp(s_ctx - m), cvp_ref[...])
        o = pv[:, :ATT_HD] / (pv[:, ATT_HD:] + jnp.exp(sink - m))
        for g in range(ATT_SPLIT):
            o_ref[:, (g0 + g) * ATT_HD:(g0 + g + 1) * ATT_HD] = o[g * BLOCK:(g + 1) * BLOCK, :].astype(o_ref.dtype)


def _attention_latent(proj, cache_k, cache_v, sink, cos_a, sin_a, q_norm, k_norm, fill):
    nqb = DEC_SEQ // BLOCK
    rb0 = T_PROMPT // DEC_SEQ
    qcol0 = 4 * RET_HEADS * RET_DK // (ATT_GROUP * ATT_HD)
    kcol0 = (4 * RET_HEADS * RET_DK + ATT_HEADS * ATT_HD) // ATT_HD
    vcol0 = kcol0 + ATT_KV_HEADS
    ck = cache_k.reshape(DEC_BATCH, PAST_LEN, ATT_KV_HEADS * ATT_HD)
    cv = cache_v.reshape(DEC_BATCH, PAST_LEN, ATT_KV_HEADS * ATT_HD)
    return pl.pallas_call(
        _att_latent_kernel,
        grid=(DEC_BATCH, ATT_KV_HEADS, nqb),
        in_specs=[pl.BlockSpec(memory_space=pltpu.SMEM),
                  pl.BlockSpec((BLOCK, ATT_GROUP * ATT_HD),
                               lambda b, kh, qb: (T_PROMPT // BLOCK + b * nqb + qb, qcol0 + kh)),
                  pl.BlockSpec((DEC_SEQ, ATT_HD), lambda b, kh, qb: (rb0 + b, kcol0 + kh)),
                  pl.BlockSpec((DEC_SEQ, ATT_HD), lambda b, kh, qb: (rb0 + b, vcol0 + kh)),
                  pl.BlockSpec((None, PAST_LEN, ATT_HD), lambda b, kh, qb: (b, 0, kh)),
                  pl.BlockSpec((None, PAST_LEN, ATT_HD), lambda b, kh, qb: (b, 0, kh)),
                  pl.BlockSpec((DEC_SEQ, ATT_HD), lambda b, kh, qb: (0, 0)),
                  pl.BlockSpec((DEC_SEQ, ATT_HD), lambda b, kh, qb: (0, 0)),
                  pl.BlockSpec((1, ATT_HD), lambda b, kh, qb: (0, 0)),
                  pl.BlockSpec((1, ATT_HD), lambda b, kh, qb: (0, 0)),
                  pl.BlockSpec(memory_space=pl.ANY)],
        out_specs=pl.BlockSpec((BLOCK, ATT_GROUP * ATT_HD),
                               lambda b, kh, qb: (T_PROMPT // BLOCK + b * nqb + qb, kh)),
        out_shape=jax.ShapeDtypeStruct((T_ALL, ATT_HEADS * ATT_HD), BF16),
        input_output_aliases={10: 0},
        scratch_shapes=[pltpu.VMEM((DEC_SEQ, ATT_HD), BF16), pltpu.VMEM((DEC_SEQ, 2 * ATT_HD), BF16),
                        pltpu.VMEM((PAST_LEN, ATT_HD), BF16), pltpu.VMEM((PAST_LEN, 2 * ATT_HD), BF16)],
        compiler_params=_cparams("arbitrary", "arbitrary", "arbitrary"),
        name="attention_latent",
    )(sink, proj, proj, proj, ck, cv, cos_a, sin_a, q_norm.reshape(1, ATT_HD), k_norm.reshape(1, ATT_HD), fill)


def _att_prompt_kernel(sink_ref, q_ref, k_ref, v_ref, qn_ref, kn_ref, o_ref, nk_ref, nv_ref):
    kh = pl.program_id(1)
    kn = _head_norm(k_ref[...], kn_ref[...])
    v = v_ref[...]
    nk_ref[...] = kn
    nv_ref[...] = v.astype(F32)
    qg = qn_ref[...] * (ATT_HD ** -0.5)
    q = jnp.concatenate([_head_norm(q_ref[:, g * ATT_HD:(g + 1) * ATT_HD], qg).astype(BF16)
                         for g in range(ATT_GROUP)], axis=0)
    s = _bdot_nt(q, kn)
    sink = _sink_column(sink_ref, kh * ATT_GROUP, ATT_GROUP, SEQ)
    m = jnp.maximum(jnp.max(s, axis=-1, keepdims=True), sink)
    p = jnp.exp(s - m)
    den = jnp.sum(p, axis=-1, keepdims=True) + jnp.exp(sink - m)
    o = _bdot(p, v) / den
    for g in range(ATT_GROUP):
        o_ref[:, g * ATT_HD:(g + 1) * ATT_HD] = o[g * SEQ:(g + 1) * SEQ, :].astype(o_ref.dtype)


def _attention_prompt(proj, sink, q_norm, k_norm):
    qcol0 = 4 * RET_HEADS * RET_DK // (ATT_GROUP * ATT_HD)
    kcol0 = (4 * RET_HEADS * RET_DK + ATT_HEADS * ATT_HD) // ATT_HD
    vcol0 = kcol0 + ATT_KV_HEADS
    kv_spec = pl.BlockSpec((None, SEQ, ATT_HD), lambda b, kh: (b, 0, kh))
    kv_shape = jax.ShapeDtypeStruct((BATCH, SEQ, ATT_KV_HEADS * ATT_HD), F32)
    return pl.pallas_call(
        _att_prompt_kernel,
        grid=(BATCH, ATT_KV_HEADS),
        in_specs=[pl.BlockSpec(memory_space=pltpu.SMEM),
                  pl.BlockSpec((SEQ, ATT_GROUP * ATT_HD), lambda b, kh: (b, qcol0 + kh)),
                  pl.BlockSpec((SEQ, ATT_HD), lambda b, kh: (b, kcol0 + kh)),
                  pl.BlockSpec((SEQ, ATT_HD), lambda b, kh: (b, vcol0 + kh)),
                  pl.BlockSpec((1, ATT_HD), lambda b, kh: (0, 0)),
                  pl.BlockSpec((1, ATT_HD), lambda b, kh: (0, 0))],
        out_specs=[pl.BlockSpec((SEQ, ATT_GROUP * ATT_HD), lambda b, kh: (b, kh)), kv_spec, kv_spec],
        out_shape=[jax.ShapeDtypeStruct((T_ALL, ATT_HEADS * ATT_HD), BF16), kv_shape, kv_shape],
        compiler_params=_cparams("arbitrary", "arbitrary"),
        name="attention_prompt",
    )(sink, proj, proj, proj, q_norm.reshape(1, ATT_HD), k_norm.reshape(1, ATT_HD))


def _split_dot(m01, a):
    hi = a.astype(BF16)
    r1 = a - hi.astype(F32)
    mid = r1.astype(BF16)
    lo = (r1 - mid.astype(F32)).astype(BF16)
    return (jnp.dot(m01, hi, preferred_element_type=F32) + jnp.dot(m01, mid, preferred_element_type=F32)
            + jnp.dot(m01, lo, preferred_element_type=F32))


PREP_CHUNKS = ROW_TILE // BLOCK


def _ssd_prep_kernel(raw_ref, bias_ref, alog_ref, cum_ref, dt_ref, w_ref, tot_ref, ecum_ref):
    ii = lax.broadcasted_iota(jnp.int32, (BLOCK, BLOCK), 0)
    jj = lax.broadcasted_iota(jnp.int32, (BLOCK, BLOCK), 1)
    lower = jnp.where(jj <= ii, 1.0, 0.0).astype(BF16)
    upper = jnp.where(jj >= ii, 1.0, 0.0).astype(BF16)
    fwd_lane = lax.broadcasted_iota(jnp.int32, (BLOCK, LANES), 1) < SSD_HEADS
    neg_a = -jnp.exp(alog_ref[...])

    def chunk(k, carry):
        x = raw_ref[_rows(k), :] + bias_ref[...]
        dt = jnp.maximum(x, 0.0) + jnp.log1p(jnp.exp(-jnp.abs(x)))
        a = dt * neg_a
        incl = _split_dot(lower, a)
        rincl = _split_dot(upper, a)
        cum = jnp.where(fwd_lane, incl, rincl)
        tot = jnp.where(fwd_lane[:1], incl[BLOCK - 1:BLOCK, :], rincl[0:1, :])
        cum_ref[k] = cum.T
        dt_ref[k] = dt.T
        w_ref[k] = (dt * jnp.exp(tot - cum)).T
        tot_ref[k] = jnp.broadcast_to(jnp.exp(tot), (BLOCK, LANES)).T
        ecum_ref[k] = jnp.exp(cum).T
        return carry

    lax.fori_loop(0, PREP_CHUNKS, chunk, 0)


def _ssd_prep(dt_raw, dt_bias, a_log):
    nc = T_ALL // BLOCK
    spec = pl.BlockSpec((PREP_CHUNKS, 2 * SSD_HEADS, BLOCK), lambda c: (c, 0, 0))
    shape = jax.ShapeDtypeStruct((nc, 2 * SSD_HEADS, BLOCK), F32)
    return pl.pallas_call(
        _ssd_prep_kernel,
        grid=(nc // PREP_CHUNKS,),
        in_specs=[pl.BlockSpec((ROW_TILE, 2 * SSD_HEADS), lambda c: (c, 0)),
                  pl.BlockSpec((1, 2 * SSD_HEADS), lambda c: (0, 0)),
                  pl.BlockSpec((1, 2 * SSD_HEADS), lambda c: (0, 0))],
        out_specs=[spec] * 5,
        out_shape=[shape] * 5,
        compiler_params=_cparams("arbitrary"),
        name="ssd_prep",
    )(dt_raw, dt_bias.reshape(1, 2 * SSD_HEADS), a_log.reshape(1, 2 * SSD_HEADS))


GW = SSD_R * SSD_P
HALO = 8


def _pair_tiles(per_head):
    low = lax.broadcasted_iota(jnp.int32, per_head[0].shape, 1) < SSD_P
    return jnp.concatenate([jnp.where(low, per_head[2 * t], per_head[2 * t + 1]) for t in range(SSD_R // 2)],
                           axis=1)


def _row_bcast(ref, c, r):
    return jnp.broadcast_to(ref[c, r:r + 1, :], (BLOCK, BLOCK))


def _pair_cols(ref, c):
    top = lax.broadcasted_iota(jnp.int32, (BLOCK, BLOCK), 0) < SSD_P
    return jnp.concatenate(
        [jnp.where(top, _row_bcast(ref, c, 2 * t), _row_bcast(ref, c, 2 * t + 1)).T for t in range(SSD_R // 2)],
        axis=1)


def _ssd_kernel(n_chunks, has_s0, emit_state, has_fill, *refs):
    it = iter(refs)
    z_ref, x_ref, b_ref, c_ref = (next(it) for _ in range(4))
    cumf_ref, cumb_ref, dtf_ref, dtb_ref, wf_ref, wb_ref, totf_ref, totb_ref, ecf_ref, ecb_ref = (
        next(it) for _ in range(10))
    cwx_ref, cwb_ref, cwc_ref, cbx_ref, cbb_ref, cbc_ref, d_ref, ng_ref = (next(it) for _ in range(8))
    s0_ref = next(it) if has_s0 else None
    if has_fill:
        next(it), next(it)
    o_ref, ss_ref = next(it), next(it)
    sfin_ref = next(it) if emit_state else None
    pad_ref, xc_ref, bc_ref, cc_ref, sfs_ref, sf_ref, sb_ref = (next(it) for _ in range(7))
    seq_len = n_chunks * BLOCK

    pad_ref[0:HALO, :] = jnp.zeros((HALO, GW + 2 * SSD_N), F32)
    pad_ref[HALO + seq_len:2 * HALO + seq_len, :] = jnp.zeros((HALO, GW + 2 * SSD_N), F32)

    def fill(c, carry):
        dst = pl.ds(pl.multiple_of(c * BLOCK, BLOCK) + HALO, BLOCK)
        r = _rows(c)
        pad_ref[dst, 0:GW] = x_ref[r, :].astype(F32)
        pad_ref[dst, GW:GW + SSD_N] = b_ref[r, :].astype(F32)
        pad_ref[dst, GW + SSD_N:GW + 2 * SSD_N] = c_ref[r, :].astype(F32)
        return carry

    lax.fori_loop(0, n_chunks, fill, 0)

    def conv(c):
        r = _rows(c)
        src = pl.ds(pl.multiple_of(c * BLOCK, BLOCK), BLOCK + 2 * HALO)
        for col0, width, cw_ref, cb_ref, dst in ((0, GW, cwx_ref, cbx_ref, xc_ref),
                                                 (GW, SSD_N, cwb_ref, cbb_ref, bc_ref),
                                                 (GW + SSD_N, SSD_N, cwc_ref, cbc_ref, cc_ref)):
            for t in range(width // LANES):
                tl = slice(t * LANES, (t + 1) * LANES)
                win = pad_ref[src, col0 + t * LANES:col0 + (t + 1) * LANES]
                acc = jnp.broadcast_to(cb_ref[:, tl], (BLOCK, LANES))
                for w in range(CONV_W):
                    off = HALO - CONV_W // 2 + w
                    acc = acc + win[off:off + BLOCK, :] * cw_ref[w:w + 1, tl]
                dst[r, tl] = _silu(acc).astype(dst.dtype)

    if has_s0:
        for r in range(SSD_R):
            sf_ref[:, r * SSD_P:(r + 1) * SSD_P] = s0_ref[0, r]
            sb_ref[:, r * SSD_P:(r + 1) * SSD_P] = s0_ref[1, r]
    else:
        sf_ref[...] = jnp.zeros_like(sf_ref)
        sb_ref[...] = jnp.zeros_like(sb_ref)

    def state_update(s_ref, c, w_ref, tot_ref, bm, xs):
        tot = _pair_tiles([tot_ref[c, r:r + 1, :] for r in range(SSD_R)])
        s_ref[...] = s_ref[...] * tot + _bdot_tn(bm, xs * _pair_cols(w_ref, c))

    def fwd_step(c):
        r = _rows(c)
        sfs_ref[c] = sf_ref[...].astype(BF16)
        state_update(sf_ref, c, wf_ref, totf_ref, bc_ref[r, :], xc_ref[r, :])

    def fwd(c, carry):
        fwd_step(c)
        conv(c + 1)
        return carry

    conv(0)
    lax.fori_loop(0, n_chunks - 1, fwd, 0, unroll=3 if (n_chunks - 1) % 3 == 0 else 1)
    fwd_step(n_chunks - 1)
    if emit_state:
        for r in range(SSD_R):
            sfin_ref[0, r] = sf_ref[:, r * SSD_P:(r + 1) * SSD_P]

    ii = lax.broadcasted_iota(jnp.int32, (BLOCK, BLOCK), 0)
    jj = lax.broadcasted_iota(jnp.int32, (BLOCK, BLOCK), 1)
    causal = jj <= ii
    anti = ii <= jj
    low = lax.broadcasted_iota(jnp.int32, (BLOCK, LANES), 1) < SSD_P

    def bwd(t, carry):
        c = n_chunks - 1 - t
        r = _rows(c)
        cm, bm, xs = cc_ref[r, :], bc_ref[r, :], xc_ref[r, :]
        xb = xs.astype(BF16)
        sc = _bdot_nt(cm, bm)
        yf = _bdot(cm, sfs_ref[c])
        yb = _bdot(cm, sb_ref[...])
        tiles = []
        for t2 in range(SSD_R // 2):
            xt = xb[:, t2 * LANES:(t2 + 1) * LANES]
            acc = None
            for hh in range(2):
                hr = 2 * t2 + hh
                rf, rb = _row_bcast(cumf_ref, c, hr), _row_bcast(cumb_ref, c, hr)
                e = jnp.exp(jnp.where(anti, rf, rb).T - jnp.where(causal, rf, rb))
                e = e * jnp.where(causal, _row_bcast(dtf_ref, c, hr), _row_bcast(dtb_ref, c, hr))
                xh = jnp.where(low, xt, 0.0) if hh == 0 else jnp.where(low, 0.0, xt)
                part = _bdot(sc * e, xh)
                acc = part if acc is None else acc + part
            tiles.append(acc)
        y = (jnp.concatenate(tiles, axis=1) + _pair_cols(ecf_ref, c) * yf + _pair_cols(ecb_ref, c) * yb
             + d_ref[...] * xs)
        yg = y * _silu(z_ref[r, :].astype(F32))
        ss_ref[r, :] = jnp.broadcast_to(jnp.sum(yg * yg, axis=-1, keepdims=True), (BLOCK, LANES))
        o_ref[r, :] = (yg * ng_ref[...]).astype(o_ref.dtype)
        state_update(sb_ref, c, wb_ref, totb_ref, bm, xs)
        return carry

    lax.fori_loop(0, n_chunks, bwd, 0, unroll=2)
    if emit_state:
        for r in range(SSD_R):
            sfin_ref[1, r] = sb_ref[:, r * SSD_P:(r + 1) * SSD_P]


def _ssd_scan(zx, prep, conv_w, conv_b, d_exp, norm_gain, seq_len, n_seq, row_block0, s0=None, emit_state=False,
              fill=None):
    n_chunks = seq_len // BLOCK
    has_s0 = s0 is not None
    xcol0 = D_INNER // GW
    bcol0 = 2 * D_INNER // SSD_N
    ccol0 = bcol0 + SSD_GROUPS
    cwb0 = D_INNER // SSD_N

    def rowcol(width, col0):
        return pl.BlockSpec((seq_len, width), lambda b, g, col0=col0: (b + row_block0, col0 + g))

    def headrows(direction):
        return pl.BlockSpec((n_chunks, SSD_R, BLOCK),
                            lambda b, g, direction=direction: (b + row_block0, direction * SSD_GROUPS + g, 0))

    in_specs = [rowcol(GW, 0), rowcol(GW, xcol0), rowcol(SSD_N, bcol0), rowcol(SSD_N, ccol0)]
    args = [zx, zx, zx, zx]
    for arr in prep:
        in_specs += [headrows(0), headrows(1)]
        args += [arr, arr]
    in_specs += [pl.BlockSpec((CONV_W, GW), lambda b, g: (0, g)),
                 pl.BlockSpec((CONV_W, SSD_N), lambda b, g: (0, cwb0 + g)),
                 pl.BlockSpec((CONV_W, SSD_N), lambda b, g: (0, cwb0 + SSD_GROUPS + g)),
                 pl.BlockSpec((1, GW), lambda b, g: (0, g)),
                 pl.BlockSpec((1, SSD_N), lambda b, g: (0, cwb0 + g)),
                 pl.BlockSpec((1, SSD_N), lambda b, g: (0, cwb0 + SSD_GROUPS + g)),
                 pl.BlockSpec((1, GW), lambda b, g: (0, g)),
                 pl.BlockSpec((1, GW), lambda b, g: (0, g))]
    args += [conv_w, conv_w, conv_w, conv_b, conv_b, conv_b, d_exp, norm_gain.reshape(1, D_INNER)]
    state_spec = pl.BlockSpec((None, None, 2, SSD_R, SSD_N, SSD_P), lambda b, g: (b, 0, 0, g, 0, 0))
    if has_s0:
        in_specs.append(state_spec)
        args.append(s0)
    aliases = _fill_alias(fill, in_specs, args)
    out_specs = [pl.BlockSpec((seq_len, GW), lambda b, g: (b + row_block0, g)),
                 pl.BlockSpec((seq_len, LANES), lambda b, g: (b + row_block0, g))]
    out_shape = [jax.ShapeDtypeStruct((T_ALL, D_INNER), BF16),
                 jax.ShapeDtypeStruct((T_ALL, SSD_GROUPS * LANES), F32)]
    if emit_state:
        out_specs.append(state_spec)
        out_shape.append(jax.ShapeDtypeStruct((n_seq, 1, 2, SSD_HEADS, SSD_N, SSD_P), F32))
    return pl.pallas_call(
        functools.partial(_ssd_kernel, n_chunks, has_s0, emit_state, bool(fill)),
        grid=(n_seq, SSD_GROUPS),
        in_specs=in_specs,
        out_specs=out_specs,
        out_shape=out_shape,
        input_output_aliases=aliases,
        scratch_shapes=[pltpu.VMEM((seq_len + 2 * HALO, GW + 2 * SSD_N), F32),
                        pltpu.VMEM((seq_len, GW), F32), pltpu.VMEM((seq_len, SSD_N), BF16),
                        pltpu.VMEM((seq_len, SSD_N), BF16), pltpu.VMEM((n_chunks, SSD_N, GW), BF16),
                        pltpu.VMEM((SSD_N, GW), F32), pltpu.VMEM((SSD_N, GW), F32)],
        compiler_params=_cparams("arbitrary", "arbitrary"),
        name="ssd_scan_latent" if has_s0 else "ssd_scan_prompt",
    )(*args)


def _router_kernel(x_ref, g_ref, mod_ref, rw_ref, hn_ref, idx_ref, wgt_ref, hf_ref):
    _adaln_to(x_ref, g_ref, mod_ref, 3, 4, hf_ref)
    hf = hf_ref[...]
    h_hi = hf.astype(BF16)
    h_lo = (hf - h_hi.astype(F32)).astype(BF16)
    hn_ref[...] = h_hi
    rw = rw_ref[...]
    r_hi = rw.astype(BF16)
    r_lo = (rw - r_hi.astype(F32)).astype(BF16)
    t1 = jnp.dot(h_hi, jnp.concatenate([r_hi, r_lo], axis=1), preferred_element_type=F32)
    logits = t1[:, :LANES] + t1[:, LANES:] + jnp.dot(h_lo, r_hi, preferred_element_type=F32)
    lane = lax.broadcasted_iota(jnp.int32, logits.shape, 1)
    lg = jnp.where(lane < N_EXPERTS, logits, NEG_INF)
    m1 = jnp.max(lg, axis=-1, keepdims=True)
    i1 = jnp.min(jnp.where(lg == m1, lane, LANES), axis=-1, keepdims=True)
    lg2 = jnp.where(lane == i1, NEG_INF, lg)
    m2 = jnp.max(lg2, axis=-1, keepdims=True)
    i2 = jnp.min(jnp.where(lg2 == m2, lane, LANES), axis=-1, keepdims=True)
    e2 = jnp.exp(m2 - m1)
    w1 = 1.0 / (1.0 + e2)
    idx_ref[...] = jnp.where(lane == 0, i1, jnp.where(lane == 1, i2, 0))
    wgt_ref[...] = jnp.where(lane == 0, w1, jnp.where(lane == 1, e2 * w1, 0.0))


def _router(x, gain, modt, router_w):
    t, d = x.shape
    rw = jnp.pad(router_w, ((0, 0), (0, LANES - N_EXPERTS)))
    return pl.pallas_call(
        _router_kernel,
        grid=(t // ROW_TILE,),
        in_specs=[pl.BlockSpec((ROW_TILE, d), lambda i: (i, 0)),
                  pl.BlockSpec((1, d), lambda i: (0, 0)),
                  pl.BlockSpec((None, 8, d), lambda i: (i, 0, 0)),
                  pl.BlockSpec((d, LANES), lambda i: (0, 0))],
        out_specs=[pl.BlockSpec((ROW_TILE, d), lambda i: (i, 0)),
                   pl.BlockSpec((ROW_TILE, LANES), lambda i: (i, 0)),
                   pl.BlockSpec((ROW_TILE, LANES), lambda i: (i, 0))],
        out_shape=[jax.ShapeDtypeStruct((t, d), BF16), jax.ShapeDtypeStruct((t, LANES), jnp.int32),
                   jax.ShapeDtypeStruct((t, LANES), F32)],
        scratch_shapes=[pltpu.VMEM((ROW_TILE, d), F32)],
        compiler_params=_cparams("arbitrary"),
        name="moe_router",
    )(x, gain.reshape(1, d), modt, rw)


DOWN_ROWS = 512
DOWN_TN = 512


def _expert_changed(be_ref, blk, prev_blk, step):
    return jnp.logical_or(step == 0, be_ref[blk] != be_ref[prev_blk])


def _expert_up_kernel(be_ref, nu_ref, nv_ref, xs_ref, wg_ref, wu_ref, h_ref, wgb_ref, wub_ref):
    i = pl.program_id(1)
    used = nv_ref[i] > 0

    @pl.when(jnp.logical_and(used, _expert_changed(be_ref, i, jnp.maximum(i - 1, 0), i)))
    def _():
        wgb_ref[...] = wg_ref[...].astype(BF16)
        wub_ref[...] = wu_ref[...].astype(BF16)

    @pl.when(used)
    def _():
        xs = xs_ref[...]
        h_ref[...] = (_silu(_bdot(xs, wgb_ref[...])) * _bdot(xs, wub_ref[...])).astype(h_ref.dtype)

    @pl.when(jnp.logical_not(used))
    def _():
        h_ref[...] = jnp.zeros_like(h_ref)


def _expert_down_kernel(be_ref, nu_ref, nv_ref, h_ref, wd_ref, o_ref, wdb_ref):
    i = pl.program_id(1)
    per = MOE_ROWS // DOWN_ROWS
    blk = i // per
    used = nv_ref[blk] > (i % per) * DOWN_ROWS

    @pl.when(jnp.logical_and(nv_ref[blk] > 0, _expert_changed(be_ref, blk, jnp.maximum(i - 1, 0) // per, i)))
    def _():
        wdb_ref[...] = wd_ref[...].astype(BF16)

    @pl.when(used)
    def _():
        o_ref[...] = _bdot(h_ref[...], wdb_ref[...]).astype(o_ref.dtype)

    @pl.when(jnp.logical_not(used))
    def _():
        o_ref[...] = jnp.zeros_like(o_ref)


def _experts(xs_sorted, block_e, n_used, n_valid, wg, wu, wd):
    cap, d = xs_sorted.shape
    ff = wg.shape[2]
    per = MOE_ROWS // DOWN_ROWS

    def expert_of(blk, be, nu):
        return be[jnp.minimum(blk, jnp.maximum(nu[0] - 1, 0))]

    h = pl.pallas_call(
        _expert_up_kernel,
        grid_spec=pltpu.PrefetchScalarGridSpec(
            num_scalar_prefetch=3,
            grid=(ff // FFN_TF, cap // MOE_ROWS),
            in_specs=[pl.BlockSpec((MOE_ROWS, d), lambda f, i, be, nu, nv: (i, 0)),
                      pl.BlockSpec((None, d, FFN_TF), lambda f, i, be, nu, nv: (expert_of(i, be, nu), 0, f)),
                      pl.BlockSpec((None, d, FFN_TF), lambda f, i, be, nu, nv: (expert_of(i, be, nu), 0, f))],
            out_specs=pl.BlockSpec((MOE_ROWS, FFN_TF), lambda f, i, be, nu, nv: (i, f)),
            scratch_shapes=[pltpu.VMEM((d, FFN_TF), BF16), pltpu.VMEM((d, FFN_TF), BF16)],
        ),
        out_shape=jax.ShapeDtypeStruct((cap, ff), BF16),
        compiler_params=_cparams("arbitrary", "arbitrary"),
        name="moe_expert_up",
    )(block_e, n_used, n_valid, xs_sorted, wg, wu)
    return pl.pallas_call(
        _expert_down_kernel,
        grid_spec=pltpu.PrefetchScalarGridSpec(
            num_scalar_prefetch=3,
            grid=(d // DOWN_TN, cap // DOWN_ROWS),
            in_specs=[pl.BlockSpec((DOWN_ROWS, ff), lambda n, i, be, nu, nv: (i, 0)),
                      pl.BlockSpec((None, ff, DOWN_TN),
                                   lambda n, i, be, nu, nv: (expert_of(i // per, be, nu), 0, n))],
            out_specs=pl.BlockSpec((DOWN_ROWS, DOWN_TN), lambda n, i, be, nu, nv: (i, n)),
            scratch_shapes=[pltpu.VMEM((ff, DOWN_TN), BF16)],
        ),
        out_shape=jax.ShapeDtypeStruct((cap, d), BF16),
        compiler_params=_cparams("arbitrary", "arbitrary"),
        name="moe_expert_down",
    )(block_e, n_used, n_valid, h, wd)


def _combine_kernel(x_ref, g0_ref, g1_ref, w_ref, mod_ref, o_ref):
    w = w_ref[...]
    y = w[:, 0:1] * g0_ref[...].astype(F32) + w[:, 1:2] * g1_ref[...].astype(F32)
    o_ref[...] = x_ref[...] + mod_ref[5:6, :] * y


COMB_ROWS = 512


def _combine(x, g, wgt, modt, row0, n_rows):
    t, d = x.shape
    b0 = row0 // COMB_ROWS
    per = ROW_TILE // COMB_ROWS
    return pl.pallas_call(
        _combine_kernel,
        grid=(n_rows // COMB_ROWS,),
        in_specs=[pl.BlockSpec((COMB_ROWS, d), lambda i: (i + b0, 0)),
                  pl.BlockSpec((COMB_ROWS, d), lambda i: (i + b0, 0)),
                  pl.BlockSpec((COMB_ROWS, d), lambda i: (i + b0 + t // COMB_ROWS, 0)),
                  pl.BlockSpec((COMB_ROWS, LANES), lambda i: (i + b0, 0)),
                  pl.BlockSpec((None, 8, d), lambda i: ((i + b0) // per, 0, 0))],
        out_specs=pl.BlockSpec((COMB_ROWS, d), lambda i: (i, 0)),
        out_shape=jax.ShapeDtypeStruct((n_rows, d), F32),
        compiler_params=_cparams("arbitrary"),
        name="moe_combine",
    )(x, g, g, wgt, modt)


def _moe(x, gain, modt, router_w, wg, wu, wd):
    t, d = x.shape
    hn, idx, wgt = _router(x, gain, modt, router_w)
    top_idx = idx[:, :TOP_K]
    n_slots = t * TOP_K
    flat_e = top_idx.reshape(-1)
    onehot = (flat_e[:, None] == jnp.arange(N_EXPERTS, dtype=jnp.int32)[None, :]).astype(jnp.int32)
    incl = jnp.cumsum(onehot, axis=0)
    counts = incl[-1]
    rank = jnp.sum((incl - onehot) * onehot, axis=1)
    padded = (counts + MOE_ROWS - 1) // MOE_ROWS * MOE_ROWS
    pend = jnp.cumsum(padded)
    pstart = pend - padded
    dest = pstart[flat_e] + rank
    n_blocks = n_slots // MOE_ROWS + N_EXPERTS
    cap = n_blocks * MOE_ROWS
    row_tok = (jnp.arange(cap, dtype=jnp.int32) % t).at[dest].set(jnp.arange(n_slots, dtype=jnp.int32) // TOP_K)
    block_e = jnp.clip(jnp.searchsorted(pend, jnp.arange(n_blocks, dtype=jnp.int32) * MOE_ROWS, side='right'),
                       0, N_EXPERTS - 1).astype(jnp.int32)
    n_used = (pend[-1:] // MOE_ROWS).astype(jnp.int32)
    blk = jnp.arange(n_blocks, dtype=jnp.int32)
    n_valid = jnp.where(blk < n_used[0],
                        jnp.clip((pstart + counts)[block_e] - blk * MOE_ROWS, 0, MOE_ROWS), 0).astype(jnp.int32)
    out = _experts(hn[row_tok], block_e, n_used, n_valid, wg, wu, wd)
    g = out[dest.reshape(t, TOP_K).T.reshape(-1)]
    return _combine(x, g, wgt, modt, 0, T_PROMPT), _combine(x, g, wgt, modt, T_PROMPT, T_SAMPLE)


def _rope_tables(n_tokens, dim):
    n_rows = n_tokens // GRID_W
    row = jnp.repeat(jnp.arange(n_rows), GRID_W).astype(F32)
    col = jnp.tile(jnp.arange(GRID_W), n_rows).astype(F32)
    n_freq = dim // 4
    inv = ROPE_BASE ** (-jnp.arange(n_freq, dtype=F32) / n_freq)
    ang = jnp.concatenate([row[:, None] * inv, col[:, None] * inv], axis=-1)
    return jnp.cos(ang), jnp.sin(ang)


def kernel(x_prompt, x_sample, state_ret, cache_k, cache_v, state_ssd, c, c_ctx, ada_w, ada_b, norm_mix, norm_ffn, ev_w_in, ev_w_out, ret_decay_logit, ret_norm, att_q_norm, att_k_norm, att_sink, ffn_w_gate, ffn_w_up, ffn_w_down, ssd_w_in, ssd_conv_w, ssd_conv_b, ssd_a_log, ssd_dt_bias, ssd_d, ssd_norm, ssd_w_out, moe_router, moe_w_gate, moe_w_up, moe_w_down):
    d = D_MODEL
    x = (x_prompt.reshape(T_PROMPT, d), x_sample.reshape(T_SAMPLE, d))

    cvecs = jnp.concatenate([c_ctx[None, :], c, jnp.zeros((MOD_ROWS - 1 - DEC_BATCH, d), F32)], axis=0)
    mods = _modulation(cvecs, ada_w, ada_b).reshape(2, MOD_ROWS, 6, d)
    tiles_per_seq = DEC_SEQ // ROW_TILE
    tile_row = jnp.concatenate([jnp.zeros((T_PROMPT // ROW_TILE,), jnp.int32),
                                1 + jnp.arange(T_SAMPLE // ROW_TILE, dtype=jnp.int32) // tiles_per_seq])
    modt = jnp.pad(mods[:, tile_row], ((0, 0), (0, 0), (0, 2), (0, 0)))

    ev_w_in_b, ev_w_out_b = ev_w_in[0].astype(BF16), ev_w_out[0].astype(BF16)
    ffn_wg_b, ffn_wu_b, ffn_wd_b = ffn_w_gate[0].astype(BF16), ffn_w_up[0].astype(BF16), ffn_w_down[0].astype(BF16)
    ssd_w_in_b, ssd_w_out_b = ssd_w_in[0].astype(BF16), ssd_w_out[0].astype(BF16)

    proj, = _adaln_matmul(x, norm_mix[0], modt[0], 0, 1, ev_w_in_b, tn=512, out_dtype=BF16, name="even_in_proj")
    lg = jax.nn.log_sigmoid(ret_decay_logit[0].astype(F32))
    cos_r, sin_r = _rope_tables(DEC_SEQ, RET_DK)
    cos_a, sin_a = _rope_tables(DEC_SEQ, ATT_HD)
    cos_a2 = jnp.concatenate([cos_a, cos_a], axis=-1)
    sin_a2 = jnp.concatenate([-sin_a, sin_a], axis=-1)
    mix_ret, new_state_ret = _retention(proj, lg, ret_norm[0], SEQ, BATCH, 0, emit_state=True)
    mix_ret, = _retention(proj, lg, ret_norm[0], DEC_SEQ, DEC_BATCH, T_PROMPT // DEC_SEQ,
                          ropes=(cos_r, sin_r), s0=state_ret, fill=(mix_ret,))
    mix_att, new_k, new_v = _attention_prompt(proj, att_sink[0], att_q_norm[0], att_k_norm[0])
    mix_att = _attention_latent(proj, cache_k[:, 0], cache_v[:, 0], att_sink[0], cos_a2, sin_a2,
                                att_q_norm[0], att_k_norm[0], mix_att)
    x = _proj_residual([mix_ret, mix_att], ev_w_out_b, x, modt[0], 2, tn=1024, name="even_out_proj")
    h = _ffn_gateup(x, norm_ffn[0], modt[0], ffn_wg_b, ffn_wu_b)
    x = _proj_residual([h], ffn_wd_b, x, modt[0], 5, tn=512, name="ffn_down")

    zx, dt_raw = _adaln_matmul(x, norm_mix[1], modt[1], 0, 1, ssd_w_in_b, tn=1024, n_out=SSD_ZX,
                               tail=2 * SSD_HEADS, out_dtype=BF16, name="ssd_in_proj")
    prep = _ssd_prep(dt_raw, ssd_dt_bias[0], ssd_a_log[0])
    d_exp = jnp.repeat(ssd_d[0], SSD_P)[None, :]
    conv_b = ssd_conv_b[0][None, :]
    yg, yss, new_state_ssd = _ssd_scan(zx, prep, ssd_conv_w[0], conv_b, d_exp, ssd_norm[0], SEQ, BATCH, 0,
                                       emit_state=True)
    yg, yss = _ssd_scan(zx, prep, ssd_conv_w[0], conv_b, d_exp, ssd_norm[0], DEC_SEQ, DEC_BATCH,
                        T_PROMPT // DEC_SEQ, s0=state_ssd, fill=(yg, yss))
    x = _proj_residual([yg], ssd_w_out_b, x, modt[1], 2, tn=512, row_ss=yss, name="ssd_out_proj")
    y_p, y_s = _moe(x, norm_ffn[1], modt[1], moe_router[0], moe_w_gate[0], moe_w_up[0], moe_w_down[0])

    y_prompt = y_p.reshape(BATCH, SEQ, d)
    y_sample = y_s.reshape(DEC_BATCH, DEC_SEQ, d)
    new_cache_k = new_k.reshape(BATCH, 1, SEQ, ATT_KV_HEADS, ATT_HD)
    new_cache_v = new_v.reshape(BATCH, 1, SEQ, ATT_KV_HEADS, ATT_HD)
    return (y_prompt, y_sample, new_state_ret, new_cache_k, new_cache_v, new_state_ssd)
```

```python
import functools

import jax
import jax.numpy as jnp
from jax import lax
from jax.experimental import pallas as pl
from jax.experimental.pallas import tpu as pltpu

F32 = jnp.float32
BF16 = jnp.bfloat16

D_MODEL = 2048
BATCH = 16
SEQ = 256
DEC_BATCH = 8
DEC_SEQ = 2048
PAST_LEN = 512
GRID_W = 64
BLOCK = 128
WINDOW = 128
EPS = 1e-6
ROPE_BASE = 10000.0
RET_HEADS = 4
RET_DK = 256
RET_DV = 256
ATT_HEADS = 8
ATT_KV_HEADS = 2
ATT_HD = 128
ATT_GROUP = ATT_HEADS // ATT_KV_HEADS
EVEN_IN = 5632
D_INNER = 2 * D_MODEL
SSD_P = 64
SSD_HEADS = D_INNER // SSD_P
SSD_N = 128
SSD_GROUPS = 8
SSD_R = SSD_HEADS // SSD_GROUPS
CONV_W = 5
CONV_CH = D_INNER + 2 * SSD_GROUPS * SSD_N
SSD_ZX = D_INNER + CONV_CH
D_FF = 5632
N_EXPERTS = 8
TOP_K = 2

T_PROMPT = BATCH * SEQ
T_SAMPLE = DEC_BATCH * DEC_SEQ
T_ALL = T_PROMPT + T_SAMPLE

LANES = 128
ROW_TILE = 1024
VMEM_LIMIT = 56 * 1024 * 1024
N_ROW_TILES = T_ALL // ROW_TILE
MOE_ROWS = 1024
NEG_INF = float("-inf")


def _cparams(*sem):
    return pltpu.CompilerParams(dimension_semantics=sem, vmem_limit_bytes=VMEM_LIMIT)


def _silu(x):
    return x * jax.nn.sigmoid(x)


def _bdot(a, b):
    return jnp.dot(a.astype(BF16), b.astype(BF16), preferred_element_type=F32)


def _bdot_nt(a, b):
    return lax.dot_general(a.astype(BF16), b.astype(BF16), (((1,), (1,)), ((), ())),
                           preferred_element_type=F32)


def _bdot_tn(a, b):
    return lax.dot_general(a.astype(BF16), b.astype(BF16), (((0,), (0,)), ((), ())),
                           preferred_element_type=F32)


def _rows(c, n=BLOCK):
    return pl.ds(pl.multiple_of(c * n, n), n)


MOD_ROWS = 16
MOD_TN = 1024


def _mod_kernel(c_ref, w_ref, b_ref, o_ref):
    o_ref[...] = _bdot(_silu(c_ref[...]), w_ref[...]) + b_ref[...]


def _modulation(cvecs, ada_w, ada_b):
    depth, d, n = ada_w.shape
    return pl.pallas_call(
        _mod_kernel,
        grid=(depth, n // MOD_TN),
        in_specs=[pl.BlockSpec((MOD_ROWS, d), lambda l, j: (0, 0)),
                  pl.BlockSpec((None, d, MOD_TN), lambda l, j: (l, 0, j)),
                  pl.BlockSpec((None, 1, MOD_TN), lambda l, j: (l, 0, j))],
        out_specs=pl.BlockSpec((None, MOD_ROWS, MOD_TN), lambda l, j: (l, 0, j)),
        out_shape=jax.ShapeDtypeStruct((depth, MOD_ROWS, n), F32),
        compiler_params=_cparams("arbitrary", "arbitrary"),
        name="modulation",
    )(cvecs, ada_w, ada_b.reshape(depth, 1, n))


ADALN_CHUNK = 64


def _adaln_to(x_ref, g_ref, mod_ref, shift_row, scale_row, hn_ref):
    mult = g_ref[...] * (1.0 + mod_ref[scale_row:scale_row + 1, :])
    shift = mod_ref[shift_row:shift_row + 1, :]

    def body(i, carry):
        r = _rows(i, ADALN_CHUNK)
        x = x_ref[r, :]
        ms = jnp.mean(x * x, axis=-1, keepdims=True)
        hn_ref[r, :] = (x * lax.rsqrt(ms + EPS) * mult + shift).astype(hn_ref.dtype)
        return carry

    lax.fori_loop(0, x_ref.shape[0] // ADALN_CHUNK, body, 0)


ADALN_DOT_ROWS = 256


def _adaln_then(x_ref, g_ref, mod_ref, shift_row, scale_row, hn_ref, consume):
    mult = g_ref[...] * (1.0 + mod_ref[scale_row:scale_row + 1, :])
    shift = mod_ref[shift_row:shift_row + 1, :]
    for r0 in range(0, x_ref.shape[0], ADALN_DOT_ROWS):
        for s0 in range(r0, r0 + ADALN_DOT_ROWS, ADALN_CHUNK):
            r = slice(s0, s0 + ADALN_CHUNK)
            x = x_ref[r, :]
            ms = jnp.mean(x * x, axis=-1, keepdims=True)
            hn_ref[r, :] = (x * lax.rsqrt(ms + EPS) * mult + shift).astype(hn_ref.dtype)
        consume(slice(r0, r0 + ADALN_DOT_ROWS))


PROMPT_TILES = T_PROMPT // ROW_TILE


def _x_rows_specs(x, cols, col_of_j):
    if not isinstance(x, tuple):
        return [pl.BlockSpec((ROW_TILE, cols), lambda i, j: (i, col_of_j(j)))], [x]
    return ([pl.BlockSpec((ROW_TILE, cols),
                          lambda i, j: (jnp.minimum(i, PROMPT_TILES - 1), jnp.where(i < PROMPT_TILES, col_of_j(j), 0))),
             pl.BlockSpec((ROW_TILE, cols),
                          lambda i, j: (jnp.maximum(i - PROMPT_TILES, 0), jnp.where(i < PROMPT_TILES, 0, col_of_j(j))))],
            list(x))


def _for_x_rows(x_refs, fn):
    if len(x_refs) == 1:
        fn(x_refs[0])
        return
    in_prompt = pl.program_id(0) < PROMPT_TILES
    pl.when(in_prompt)(lambda: fn(x_refs[0]))
    pl.when(jnp.logical_not(in_prompt))(lambda: fn(x_refs[1]))


def _adaln_mm_kernel(shift_row, scale_row, has_tail, n_x, *refs):
    x_refs, (g_ref, mod_ref, w_ref), refs = refs[:n_x], refs[n_x:n_x + 3], refs[n_x + 3:]
    if has_tail:
        wt_ref, o_ref, ot_ref, hn_ref = refs
    else:
        o_ref, hn_ref = refs

    def project(r):
        hn = hn_ref[r, :]
        o_ref[r, :] = _bdot(hn, w_ref[...]).astype(o_ref.dtype)
        if has_tail:
            ot_ref[r, :] = _bdot(hn, wt_ref[...])

    first = pl.program_id(1) == 0

    @pl.when(first)
    def _():
        _for_x_rows(x_refs, lambda x_ref: _adaln_then(x_ref, g_ref, mod_ref, shift_row, scale_row, hn_ref, project))

    @pl.when(jnp.logical_not(first))
    def _():
        o_ref[...] = _bdot(hn_ref[...], w_ref[...]).astype(o_ref.dtype)


def _adaln_matmul(x, gain, modt, shift_row, scale_row, w, tn, n_out=None, tail=0, out_dtype=F32, name="adaln_mm"):
    t, d = T_ALL, w.shape[0]
    n_out = w.shape[1] if n_out is None else n_out
    in_specs, args = _x_rows_specs(x, d, lambda j: 0)
    n_x = len(args)
    in_specs += [pl.BlockSpec((1, d), lambda i, j: (0, 0)),
                 pl.BlockSpec((None, 8, d), lambda i, j: (i, 0, 0)),
                 pl.BlockSpec((d, tn), lambda i, j: (0, j))]
    args += [gain.reshape(1, d), modt, w]
    out_specs = [pl.BlockSpec((ROW_TILE, tn), lambda i, j: (i, j))]
    out_shape = [jax.ShapeDtypeStruct((t, n_out), out_dtype)]
    if tail:
        in_specs.append(pl.BlockSpec((d, tail), lambda i, j: (0, n_out // tail)))
        args.append(w)
        out_specs.append(pl.BlockSpec((ROW_TILE, tail), lambda i, j: (i, 0)))
        out_shape.append(jax.ShapeDtypeStruct((t, tail), F32))
    return pl.pallas_call(
        functools.partial(_adaln_mm_kernel, shift_row, scale_row, bool(tail), n_x),
        grid=(t // ROW_TILE, n_out // tn),
        in_specs=in_specs,
        out_specs=out_specs,
        out_shape=out_shape,
        scratch_shapes=[pltpu.VMEM((ROW_TILE, d), BF16)],
        compiler_params=_cparams("arbitrary", "arbitrary"),
        name=name,
    )(*args)


FFN_TF = 512


def _gateup_kernel(x_ref, g_ref, mod_ref, wg_ref, wu_ref, h_ref, hn_ref):
    def gate_up(r):
        hn = hn_ref[r, :]
        h_ref[r, :] = (_silu(_bdot(hn, wg_ref[...])) * _bdot(hn, wu_ref[...])).astype(h_ref.dtype)

    first = pl.program_id(1) == 0

    @pl.when(first)
    def _():
        _adaln_then(x_ref, g_ref, mod_ref, 3, 4, hn_ref, gate_up)

    @pl.when(jnp.logical_not(first))
    def _():
        gate_up(slice(None))


def _ffn_gateup(x, gain, modt, wg, wu):
    t, d = x.shape
    ff = wg.shape[1]
    return pl.pallas_call(
        _gateup_kernel,
        grid=(t // ROW_TILE, ff // FFN_TF),
        in_specs=[pl.BlockSpec((ROW_TILE, d), lambda i, f: (i, 0)),
                  pl.BlockSpec((1, d), lambda i, f: (0, 0)),
                  pl.BlockSpec((None, 8, d), lambda i, f: (i, 0, 0)),
                  pl.BlockSpec((d, FFN_TF), lambda i, f: (0, f)),
                  pl.BlockSpec((d, FFN_TF), lambda i, f: (0, f))],
        out_specs=pl.BlockSpec((ROW_TILE, FFN_TF), lambda i, f: (i, f)),
        out_shape=jax.ShapeDtypeStruct((t, ff), BF16),
        scratch_shapes=[pltpu.VMEM((ROW_TILE, d), BF16)],
        compiler_params=_cparams("arbitrary", "arbitrary"),
        name="ffn_gateup",
    )(x, gain.reshape(1, d), modt, wg, wu)


def _proj_res_kernel(n_a, n_x, gate_row, norm, *refs):
    a_refs = refs[:n_a]
    w_refs = refs[n_a:2 * n_a]
    x_refs, mod_ref = refs[2 * n_a:2 * n_a + n_x], refs[2 * n_a + n_x]
    pos = 2 * n_a + n_x + 1
    o_ref = refs[pos + 1] if norm else refs[pos]
    acc = _bdot(a_refs[0][...], w_refs[0][...])
    for k in range(1, n_a):
        acc = acc + _bdot(a_refs[k][...], w_refs[k][...])
    if norm:
        ss_ref = refs[pos]
        k_total = sum(a.shape[1] for a in a_refs)
        ss = ss_ref[:, 0:LANES]
        for k in range(1, ss_ref.shape[1] // LANES):
            ss = ss + ss_ref[:, k * LANES:(k + 1) * LANES]
        rs = lax.rsqrt(ss * (1.0 / k_total) + EPS)
        acc = acc * jnp.concatenate([rs] * (acc.shape[1] // LANES), axis=1)
    upd = mod_ref[gate_row:gate_row + 1, :] * acc

    def finish(x_ref):
        o_ref[...] = x_ref[...] + upd

    _for_x_rows(x_refs, finish)


def _proj_residual(a_list, w, x, modt, gate_row, tn, row_ss=None, name="proj_res"):
    t, d = T_ALL, w.shape[1]
    n_a = len(a_list)
    norm = row_ss is not None
    in_specs, args, k0 = [], [], 0
    for a in a_list:
        in_specs.append(pl.BlockSpec((ROW_TILE, a.shape[1]), lambda i, j: (i, 0)))
        args.append(a)
    for a in a_list:
        ka = a.shape[1]
        assert k0 % ka == 0
        in_specs.append(pl.BlockSpec((ka, tn), lambda i, j, kb=k0 // ka: (kb, j)))
        args.append(w)
        k0 += ka
    x_specs, x_args = _x_rows_specs(x, tn, lambda j: j)
    in_specs += x_specs + [pl.BlockSpec((None, 8, tn), lambda i, j: (i, 0, j))]
    args += x_args + [modt]
    if norm:
        in_specs.append(pl.BlockSpec((ROW_TILE, row_ss.shape[1]), lambda i, j: (i, 0)))
        args.append(row_ss)
    return pl.pallas_call(
        functools.partial(_proj_res_kernel, n_a, len(x_args), gate_row, norm),
        grid=(t // ROW_TILE, d // tn),
        in_specs=in_specs,
        out_specs=pl.BlockSpec((ROW_TILE, tn), lambda i, j: (i, j)),
        out_shape=jax.ShapeDtypeStruct((t, d), F32),
        compiler_params=_cparams("arbitrary", "arbitrary"),
        name=name,
    )(*args)


def _ret_kernel(n_chunks, rope, has_s0, emit_state, has_fill, lg_ref, *refs):
    it = iter(refs)
    q_ref, k_ref, v_ref, gt_ref = next(it), next(it), next(it), next(it)
    cos_ref, sin_ref = (next(it), next(it)) if rope else (None, None)
    s0_ref = next(it) if has_s0 else None
    gain_ref = next(it)
    if has_fill:
        next(it)
    o_ref = next(it)
    sfin_ref = next(it) if emit_state else None
    qs_ref, ks_ref, sfs_ref, dm_ref, dec_ref, sf_ref, sb_ref = (next(it) for _ in range(7))

    h = pl.program_id(1)
    lgf = lg_ref[0, h]
    lgb = lg_ref[1, h]
    half = RET_DK // 2

    def prep(c, carry):
        r = _rows(c)
        q = q_ref[r, :].astype(F32)
        k = k_ref[r, :].astype(F32) * (RET_DK ** -0.5)
        if rope:
            cs, sn = cos_ref[r, :], sin_ref[r, :]
            for src, dst in ((q, qs_ref), (k, ks_ref)):
                x1, x2 = src[:, :half], src[:, half:]
                dst[r, :half] = (x1 * cs - x2 * sn).astype(BF16)
                dst[r, half:] = (x1 * sn + x2 * cs).astype(BF16)
        else:
            qs_ref[r, :] = q.astype(BF16)
            ks_ref[r, :] = k.astype(BF16)
        return carry

    lax.fori_loop(0, n_chunks, prep, 0)

    ii = lax.broadcasted_iota(jnp.int32, (BLOCK, BLOCK), 0)
    jj = lax.broadcasted_iota(jnp.int32, (BLOCK, BLOCK), 1)
    diff = (ii - jj).astype(F32)
    dm_ref[...] = jnp.exp(jnp.where(jj <= ii, diff * lgf, -diff * lgb))
    pos = lax.broadcasted_iota(jnp.int32, (BLOCK, RET_DV), 0).astype(F32)
    dec_ref[0] = jnp.exp((pos + 1.0) * lgf)
    dec_ref[1] = jnp.exp((BLOCK - pos) * lgb)
    dec_ref[2] = jnp.exp((BLOCK - 1.0 - pos) * lgf)
    dec_ref[3] = jnp.exp(pos * lgb)
    tot_f = jnp.exp(jnp.full((1, RET_DV), BLOCK * lgf, F32))
    tot_b = jnp.exp(jnp.full((1, RET_DV), BLOCK * lgb, F32))

    if has_s0:
        sf_ref[...] = s0_ref[0]
        sb_ref[...] = s0_ref[1]
    else:
        sf_ref[...] = jnp.zeros_like(sf_ref)
        sb_ref[...] = jnp.zeros_like(sb_ref)

    def fwd(c, carry):
        r = _rows(c)
        sfs_ref[c] = sf_ref[...].astype(BF16)
        kd = ks_ref[r, :].astype(F32) * dec_ref[2]
        sf_ref[...] = sf_ref[...] * tot_f + _bdot_tn(kd, v_ref[r, :])
        return carry

    lax.fori_loop(0, n_chunks, fwd, 0, unroll=min(4, n_chunks))
    if emit_state:
        sfin_ref[0] = sf_ref[...]

    def bwd(t, carry):
        c = n_chunks - 1 - t
        r = _rows(c)
        q = qs_ref[r, :]
        k = ks_ref[r, :]
        v = v_ref[r, :].astype(BF16)
        p = _bdot_nt(q, k) * dm_ref[...]
        o = _bdot(p, v)
        o = o + _bdot(q, sfs_ref[c]) * dec_ref[0]
        o = o + _bdot(q, sb_ref[...]) * dec_ref[1]
        ms = jnp.mean(o * o, axis=-1, keepdims=True)
        y = o * lax.rsqrt(ms + EPS) * gain_ref[...]
        o_ref[r, :] = (y * _silu(gt_ref[r, :].astype(F32))).astype(o_ref.dtype)
        kd = k.astype(F32) * dec_ref[3]
        sb_ref[...] = sb_ref[...] * tot_b + _bdot_tn(kd, v)
        return carry

    lax.fori_loop(0, n_chunks, bwd, 0, unroll=min(4, n_chunks))
    if emit_state:
        sfin_ref[1] = sb_ref[...]


def _fill_alias(fill, in_specs, args):
    aliases = {}
    for k, arr in enumerate(fill or ()):
        in_specs.append(pl.BlockSpec(memory_space=pl.ANY))
        args.append(arr)
        aliases[len(args) - 1] = k
    return aliases


def _retention(proj, lg, ret_norm, seq_len, n_seq, row_block0, ropes=None, s0=None, emit_state=False, fill=None):
    n_chunks = seq_len // BLOCK
    rope, has_s0 = ropes is not None, s0 is not None

    def col(cb):
        return pl.BlockSpec((seq_len, RET_DK), lambda b, h, cb=cb: (b + row_block0, cb * RET_HEADS + h))

    in_specs = [pl.BlockSpec(memory_space=pltpu.SMEM), col(0), col(1), col(2), col(3)]
    args = [lg, proj, proj, proj, proj]
    if rope:
        in_specs += [pl.BlockSpec((seq_len, RET_DK // 2), lambda b, h: (0, 0))] * 2
        args += list(ropes)
    if has_s0:
        in_specs.append(pl.BlockSpec((None, None, 2, None, RET_DK, RET_DV), lambda b, h: (b, 0, 0, h, 0, 0)))
        args.append(s0)
    in_specs.append(pl.BlockSpec((1, RET_DV), lambda b, h: (0, h)))
    args.append(ret_norm.reshape(1, RET_HEADS * RET_DV))
    aliases = _fill_alias(fill, in_specs, args)
    out_specs = [pl.BlockSpec((seq_len, RET_DV), lambda b, h: (b + row_block0, h))]
    out_shape = [jax.ShapeDtypeStruct((T_ALL, RET_HEADS * RET_DV), BF16)]
    if emit_state:
        out_specs.append(pl.BlockSpec((None, None, 2, None, RET_DK, RET_DV), lambda b, h: (b, 0, 0, h, 0, 0)))
        out_shape.append(jax.ShapeDtypeStruct((n_seq, 1, 2, RET_HEADS, RET_DK, RET_DV), F32))
    return pl.pallas_call(
        functools.partial(_ret_kernel, n_chunks, rope, has_s0, emit_state, bool(fill)),
        grid=(n_seq, RET_HEADS),
        in_specs=in_specs,
        out_specs=out_specs,
        out_shape=out_shape,
        input_output_aliases=aliases,
        scratch_shapes=[pltpu.VMEM((seq_len, RET_DK), BF16), pltpu.VMEM((seq_len, RET_DK), BF16),
                        pltpu.VMEM((n_chunks, RET_DK, RET_DV), BF16), pltpu.VMEM((BLOCK, BLOCK), F32),
                        pltpu.VMEM((4, BLOCK, RET_DV), F32), pltpu.VMEM((RET_DK, RET_DV), F32),
                        pltpu.VMEM((RET_DK, RET_DV), F32)],
        compiler_params=_cparams("arbitrary", "arbitrary"),
        name="retention_latent" if rope else "retention_prompt",
    )(*args)


def _head_norm(x, gain):
    x = x.astype(F32)
    return x * lax.rsqrt(jnp.mean(x * x, axis=-1, keepdims=True) + EPS) * gain


def _rope_full(x, cs, sn):
    return x * cs + pltpu.roll(x, ATT_HD // 2, 1) * sn


ATT_SPLIT = 2


def _sink_column(sink_ref, head0, n_heads, rows_per_head):
    n = n_heads * rows_per_head
    head = lax.broadcasted_iota(jnp.int32, (n, 1), 0) // rows_per_head
    col = jnp.full((n, 1), sink_ref[head0], F32)
    for g in range(1, n_heads):
        col = jnp.where(head == g, sink_ref[head0 + g], col)
    return col


def _att_latent_kernel(sink_ref, q_ref, k_ref, v_ref, ck_ref, cv_ref, cos_ref, sin_ref, qn_ref, kn_ref,
                       fill_ref, o_ref, kp_ref, vp_ref, ckp_ref, cvp_ref):
    del fill_ref
    kh, qb = pl.program_id(1), pl.program_id(2)
    n_chunks = DEC_SEQ // BLOCK
    loc = 3 * BLOCK

    @pl.when(qb == 0)
    def _():
        def prep(c, carry):
            r = _rows(c)
            kp_ref[r, :] = _rope_full(_head_norm(k_ref[r, :], kn_ref[...]), cos_ref[r, :], sin_ref[r, :]).astype(BF16)
            vp_ref[r, :ATT_HD] = v_ref[r, :].astype(BF16)
            vp_ref[r, ATT_HD:] = jnp.ones((BLOCK, ATT_HD), BF16)
            return carry

        lax.fori_loop(0, n_chunks, prep, 0)
        ckp_ref[...] = ck_ref[...].astype(BF16)
        cvp_ref[:, :ATT_HD] = cv_ref[...].astype(BF16)
        cvp_ref[:, ATT_HD:] = jnp.ones((PAST_LEN, ATT_HD), BF16)

    rq = _rows(qb)
    cs, sn = cos_ref[rq, :], sin_ref[rq, :]
    qg = qn_ref[...] * (ATT_HD ** -0.5)
    start = pl.multiple_of(jnp.clip((qb - 1) * BLOCK, 0, DEC_SEQ - loc), BLOCK)
    kl, vl = kp_ref[pl.ds(start, loc), :], vp_ref[pl.ds(start, loc), :]
    qpos = qb * BLOCK + (lax.broadcasted_iota(jnp.int32, (ATT_SPLIT * BLOCK, loc), 0) & (BLOCK - 1))
    kpos = start + lax.broadcasted_iota(jnp.int32, (ATT_SPLIT * BLOCK, loc), 1)
    in_window = jnp.abs(qpos - kpos) <= WINDOW
    for g0 in range(0, ATT_GROUP, ATT_SPLIT):
        q = jnp.concatenate(
            [_rope_full(_head_norm(q_ref[:, g * ATT_HD:(g + 1) * ATT_HD], qg), cs, sn).astype(BF16)
             for g in range(g0, g0 + ATT_SPLIT)], axis=0)
        s_loc = jnp.where(in_window, _bdot_nt(q, kl), NEG_INF)
        s_ctx = _bdot_nt(q, ckp_ref[...])
        sink = _sink_column(sink_ref, kh * ATT_GROUP + g0, ATT_SPLIT, BLOCK)
        m = jnp.maximum(jnp.maximum(jnp.max(s_loc, axis=-1, keepdims=True),
                                    jnp.max(s_ctx, axis=-1, keepdims=True)), sink)
        pv = _bdot(jnp.exp(s_loc - m), vl) + _bdot(jnp.exp(s_ctx - m), cvp_ref[...])
        o = pv[:, :ATT_HD] / (pv[:, ATT_HD:] + jnp.exp(sink - m))
        for g in range(ATT_SPLIT):
            o_ref[:, (g0 + g) * ATT_HD:(g0 + g + 1) * ATT_HD] = o[g * BLOCK:(g + 1) * BLOCK, :].astype(o_ref.dtype)


def _attention_latent(proj, cache_k, cache_v, sink, cos_a, sin_a, q_norm, k_norm, fill):
    nqb = DEC_SEQ // BLOCK
    rb0 = T_PROMPT // DEC_SEQ
    qcol0 = 4 * RET_HEADS * RET_DK // (ATT_GROUP * ATT_HD)
    kcol0 = (4 * RET_HEADS * RET_DK + ATT_HEADS * ATT_HD) // ATT_HD
    vcol0 = kcol0 + ATT_KV_HEADS
    ck = cache_k.reshape(DEC_BATCH, PAST_LEN, ATT_KV_HEADS * ATT_HD)
    cv = cache_v.reshape(DEC_BATCH, PAST_LEN, ATT_KV_HEADS * ATT_HD)
    return pl.pallas_call(
        _att_latent_kernel,
        grid=(DEC_BATCH, ATT_KV_HEADS, nqb),
        in_specs=[pl.BlockSpec(memory_space=pltpu.SMEM),
                  pl.BlockSpec((BLOCK, ATT_GROUP * ATT_HD),
                               lambda b, kh, qb: (T_PROMPT // BLOCK + b * nqb + qb, qcol0 + kh)),
                  pl.BlockSpec((DEC_SEQ, ATT_HD), lambda b, kh, qb: (rb0 + b, kcol0 + kh)),
                  pl.BlockSpec((DEC_SEQ, ATT_HD), lambda b, kh, qb: (rb0 + b, vcol0 + kh)),
                  pl.BlockSpec((None, PAST_LEN, ATT_HD), lambda b, kh, qb: (b, 0, kh)),
                  pl.BlockSpec((None, PAST_LEN, ATT_HD), lambda b, kh, qb: (b, 0, kh)),
                  pl.BlockSpec((DEC_SEQ, ATT_HD), lambda b, kh, qb: (0, 0)),
                  pl.BlockSpec((DEC_SEQ, ATT_HD), lambda b, kh, qb: (0, 0)),
                  pl.BlockSpec((1, ATT_HD), lambda b, kh, qb: (0, 0)),
                  pl.BlockSpec((1, ATT_HD), lambda b, kh, qb: (0, 0)),
                  pl.BlockSpec(memory_space=pl.ANY)],
        out_specs=pl.BlockSpec((BLOCK, ATT_GROUP * ATT_HD),
                               lambda b, kh, qb: (T_PROMPT // BLOCK + b * nqb + qb, kh)),
        out_shape=jax.ShapeDtypeStruct((T_ALL, ATT_HEADS * ATT_HD), BF16),
        input_output_aliases={10: 0},
        scratch_shapes=[pltpu.VMEM((DEC_SEQ, ATT_HD), BF16), pltpu.VMEM((DEC_SEQ, 2 * ATT_HD), BF16),
                        pltpu.VMEM((PAST_LEN, ATT_HD), BF16), pltpu.VMEM((PAST_LEN, 2 * ATT_HD), BF16)],
        compiler_params=_cparams("arbitrary", "arbitrary", "arbitrary"),
        name="attention_latent",
    )(sink, proj, proj, proj, ck, cv, cos_a, sin_a, q_norm.reshape(1, ATT_HD), k_norm.reshape(1, ATT_HD), fill)


def _att_prompt_kernel(sink_ref, q_ref, k_ref, v_ref, qn_ref, kn_ref, o_ref, nk_ref, nv_ref):
    kh = pl.program_id(1)
    kn = _head_norm(k_ref[...], kn_ref[...])
    v = v_ref[...]
    nk_ref[...] = kn
    nv_ref[...] = v.astype(F32)
    qg = qn_ref[...] * (ATT_HD ** -0.5)
    q = jnp.concatenate([_head_norm(q_ref[:, g * ATT_HD:(g + 1) * ATT_HD], qg).astype(BF16)
                         for g in range(ATT_GROUP)], axis=0)
    s = _bdot_nt(q, kn)
    sink = _sink_column(sink_ref, kh * ATT_GROUP, ATT_GROUP, SEQ)
    m = jnp.maximum(jnp.max(s, axis=-1, keepdims=True), sink)
    p = jnp.exp(s - m)
    den = jnp.sum(p, axis=-1, keepdims=True) + jnp.exp(sink - m)
    o = _bdot(p, v) / den
    for g in range(ATT_GROUP):
        o_ref[:, g * ATT_HD:(g + 1) * ATT_HD] = o[g * SEQ:(g + 1) * SEQ, :].astype(o_ref.dtype)


def _attention_prompt(proj, sink, q_norm, k_norm):
    qcol0 = 4 * RET_HEADS * RET_DK // (ATT_GROUP * ATT_HD)
    kcol0 = (4 * RET_HEADS * RET_DK + ATT_HEADS * ATT_HD) // ATT_HD
    vcol0 = kcol0 + ATT_KV_HEADS
    kv_spec = pl.BlockSpec((None, SEQ, ATT_HD), lambda b, kh: (b, 0, kh))
    kv_shape = jax.ShapeDtypeStruct((BATCH, SEQ, ATT_KV_HEADS * ATT_HD), F32)
    return pl.pallas_call(
        _att_prompt_kernel,
        grid=(BATCH, ATT_KV_HEADS),
        in_specs=[pl.BlockSpec(memory_space=pltpu.SMEM),
                  pl.BlockSpec((SEQ, ATT_GROUP * ATT_HD), lambda b, kh: (b, qcol0 + kh)),
                  pl.BlockSpec((SEQ, ATT_HD), lambda b, kh: (b, kcol0 + kh)),
                  pl.BlockSpec((SEQ, ATT_HD), lambda b, kh: (b, vcol0 + kh)),
                  pl.BlockSpec((1, ATT_HD), lambda b, kh: (0, 0)),
                  pl.BlockSpec((1, ATT_HD), lambda b, kh: (0, 0))],
        out_specs=[pl.BlockSpec((SEQ, ATT_GROUP * ATT_HD), lambda b, kh: (b, kh)), kv_spec, kv_spec],
        out_shape=[jax.ShapeDtypeStruct((T_ALL, ATT_HEADS * ATT_HD), BF16), kv_shape, kv_shape],
        compiler_params=_cparams("arbitrary", "arbitrary"),
        name="attention_prompt",
    )(sink, proj, proj, proj, q_norm.reshape(1, ATT_HD), k_norm.reshape(1, ATT_HD))


def _split_dot(m01, a):
    hi = a.astype(BF16)
    r1 = a - hi.astype(F32)
    mid = r1.astype(BF16)
    lo = (r1 - mid.astype(F32)).astype(BF16)
    return (jnp.dot(m01, hi, preferred_element_type=F32) + jnp.dot(m01, mid, preferred_element_type=F32)
            + jnp.dot(m01, lo, preferred_element_type=F32))


PREP_CHUNKS = ROW_TILE // BLOCK


def _ssd_prep_kernel(raw_ref, bias_ref, alog_ref, cum_ref, dt_ref, w_ref, tot_ref, ecum_ref):
    ii = lax.broadcasted_iota(jnp.int32, (BLOCK, BLOCK), 0)
    jj = lax.broadcasted_iota(jnp.int32, (BLOCK, BLOCK), 1)
    lower = jnp.where(jj <= ii, 1.0, 0.0).astype(BF16)
    upper = jnp.where(jj >= ii, 1.0, 0.0).astype(BF16)
    fwd_lane = lax.broadcasted_iota(jnp.int32, (BLOCK, LANES), 1) < SSD_HEADS
    neg_a = -jnp.exp(alog_ref[...])

    def chunk(k, carry):
        x = raw_ref[_rows(k), :] + bias_ref[...]
        dt = jnp.maximum(x, 0.0) + jnp.log1p(jnp.exp(-jnp.abs(x)))
        a = dt * neg_a
        incl = _split_dot(lower, a)
        rincl = _split_dot(upper, a)
        cum = jnp.where(fwd_lane, incl, rincl)
        tot = jnp.where(fwd_lane[:1], incl[BLOCK - 1:BLOCK, :], rincl[0:1, :])
        cum_ref[k] = cum.T
        dt_ref[k] = dt.T
        w_ref[k] = (dt * jnp.exp(tot - cum)).T
        tot_ref[k] = jnp.broadcast_to(jnp.exp(tot), (BLOCK, LANES)).T
        ecum_ref[k] = jnp.exp(cum).T
        return carry

    lax.fori_loop(0, PREP_CHUNKS, chunk, 0)


def _ssd_prep(dt_raw, dt_bias, a_log):
    nc = T_ALL // BLOCK
    spec = pl.BlockSpec((PREP_CHUNKS, 2 * SSD_HEADS, BLOCK), lambda c: (c, 0, 0))
    shape = jax.ShapeDtypeStruct((nc, 2 * SSD_HEADS, BLOCK), F32)
    return pl.pallas_call(
        _ssd_prep_kernel,
        grid=(nc // PREP_CHUNKS,),
        in_specs=[pl.BlockSpec((ROW_TILE, 2 * SSD_HEADS), lambda c: (c, 0)),
                  pl.BlockSpec((1, 2 * SSD_HEADS), lambda c: (0, 0)),
                  pl.BlockSpec((1, 2 * SSD_HEADS), lambda c: (0, 0))],
        out_specs=[spec] * 5,
        out_shape=[shape] * 5,
        compiler_params=_cparams("arbitrary"),
        name="ssd_prep",
    )(dt_raw, dt_bias.reshape(1, 2 * SSD_HEADS), a_log.reshape(1, 2 * SSD_HEADS))


GW = SSD_R * SSD_P
HALO = 8


def _pair_tiles(per_head):
    low = lax.broadcasted_iota(jnp.int32, per_head[0].shape, 1) < SSD_P
    return jnp.concatenate([jnp.where(low, per_head[2 * t], per_head[2 * t + 1]) for t in range(SSD_R // 2)],
                           axis=1)


def _row_bcast(ref, c, r):
    return jnp.broadcast_to(ref[c, r:r + 1, :], (BLOCK, BLOCK))


def _pair_cols(ref, c):
    top = lax.broadcasted_iota(jnp.int32, (BLOCK, BLOCK), 0) < SSD_P
    return jnp.concatenate(
        [jnp.where(top, _row_bcast(ref, c, 2 * t), _row_bcast(ref, c, 2 * t + 1)).T for t in range(SSD_R // 2)],
        axis=1)


def _ssd_kernel(n_chunks, has_s0, emit_state, has_fill, *refs):
    it = iter(refs)
    z_ref, x_ref, b_ref, c_ref = (next(it) for _ in range(4))
    cumf_ref, cumb_ref, dtf_ref, dtb_ref, wf_ref, wb_ref, totf_ref, totb_ref, ecf_ref, ecb_ref = (
        next(it) for _ in range(10))
    cwx_ref, cwb_ref, cwc_ref, cbx_ref, cbb_ref, cbc_ref, d_ref, ng_ref = (next(it) for _ in range(8))
    s0_ref = next(it) if has_s0 else None
    if has_fill:
        next(it), next(it)
    o_ref, ss_ref = next(it), next(it)
    sfin_ref = next(it) if emit_state else None
    pad_ref, xc_ref, bc_ref, cc_ref, sfs_ref, sf_ref, sb_ref = (next(it) for _ in range(7))
    seq_len = n_chunks * BLOCK

    pad_ref[0:HALO, :] = jnp.zeros((HALO, GW + 2 * SSD_N), F32)
    pad_ref[HALO + seq_len:2 * HALO + seq_len, :] = jnp.zeros((HALO, GW + 2 * SSD_N), F32)

    def fill(c, carry):
        dst = pl.ds(pl.multiple_of(c * BLOCK, BLOCK) + HALO, BLOCK)
        r = _rows(c)
        pad_ref[dst, 0:GW] = x_ref[r, :].astype(F32)
        pad_ref[dst, GW:GW + SSD_N] = b_ref[r, :].astype(F32)
        pad_ref[dst, GW + SSD_N:GW + 2 * SSD_N] = c_ref[r, :].astype(F32)
        return carry

    lax.fori_loop(0, n_chunks, fill, 0)

    def conv(c):
        r = _rows(c)
        src = pl.ds(pl.multiple_of(c * BLOCK, BLOCK), BLOCK + 2 * HALO)
        for col0, width, cw_ref, cb_ref, dst in ((0, GW, cwx_ref, cbx_ref, xc_ref),
                                                 (GW, SSD_N, cwb_ref, cbb_ref, bc_ref),
                                                 (GW + SSD_N, SSD_N, cwc_ref, cbc_ref, cc_ref)):
            for t in range(width // LANES):
                tl = slice(t * LANES, (t + 1) * LANES)
                win = pad_ref[src, col0 + t * LANES:col0 + (t + 1) * LANES]
                acc = jnp.broadcast_to(cb_ref[:, tl], (BLOCK, LANES))
                for w in range(CONV_W):
                    off = HALO - CONV_W // 2 + w
                    acc = acc + win[off:off + BLOCK, :] * cw_ref[w:w + 1, tl]
                dst[r, tl] = _silu(acc).astype(dst.dtype)

    if has_s0:
        for r in range(SSD_R):
            sf_ref[:, r * SSD_P:(r + 1) * SSD_P] = s0_ref[0, r]
            sb_ref[:, r * SSD_P:(r + 1) * SSD_P] = s0_ref[1, r]
    else:
        sf_ref[...] = jnp.zeros_like(sf_ref)
        sb_ref[...] = jnp.zeros_like(sb_ref)

    def state_update(s_ref, c, w_ref, tot_ref, bm, xs):
        tot = _pair_tiles([tot_ref[c, r:r + 1, :] for r in range(SSD_R)])
        s_ref[...] = s_ref[...] * tot + _bdot_tn(bm, xs * _pair_cols(w_ref, c))

    def fwd_step(c):
        r = _rows(c)
        sfs_ref[c] = sf_ref[...].astype(BF16)
        state_update(sf_ref, c, wf_ref, totf_ref, bc_ref[r, :], xc_ref[r, :])

    def fwd(c, carry):
        fwd_step(c)
        conv(c + 1)
        return carry

    conv(0)
    lax.fori_loop(0, n_chunks - 1, fwd, 0, unroll=3 if (n_chunks - 1) % 3 == 0 else 1)
    fwd_step(n_chunks - 1)
    if emit_state:
        for r in range(SSD_R):
            sfin_ref[0, r] = sf_ref[:, r * SSD_P:(r + 1) * SSD_P]

    ii = lax.broadcasted_iota(jnp.int32, (BLOCK, BLOCK), 0)
    jj = lax.broadcasted_iota(jnp.int32, (BLOCK, BLOCK), 1)
    causal = jj <= ii
    anti = ii <= jj
    low = lax.broadcasted_iota(jnp.int32, (BLOCK, LANES), 1) < SSD_P

    def bwd(t, carry):
        c = n_chunks - 1 - t
        r = _rows(c)
        cm, bm, xs = cc_ref[r, :], bc_ref[r, :], xc_ref[r, :]
        xb = xs.astype(BF16)
        sc = _bdot_nt(cm, bm)
        yf = _bdot(cm, sfs_ref[c])
        yb = _bdot(cm, sb_ref[...])
        tiles = []
        for t2 in range(SSD_R // 2):
            xt = xb[:, t2 * LANES:(t2 + 1) * LANES]
            acc = None
            for hh in range(2):
                hr = 2 * t2 + hh
                rf, rb = _row_bcast(cumf_ref, c, hr), _row_bcast(cumb_ref, c, hr)
                e = jnp.exp(jnp.where(anti, rf, rb).T - jnp.where(causal, rf, rb))
                e = e * jnp.where(causal, _row_bcast(dtf_ref, c, hr), _row_bcast(dtb_ref, c, hr))
                xh = jnp.where(low, xt, 0.0) if hh == 0 else jnp.where(low, 0.0, xt)
                part = _bdot(sc * e, xh)
                acc = part if acc is None else acc + part
            tiles.append(acc)
        y = (jnp.concatenate(tiles, axis=1) + _pair_cols(ecf_ref, c) * yf + _pair_cols(ecb_ref, c) * yb
             + d_ref[...] * xs)
        yg = y * _silu(z_ref[r, :].astype(F32))
        ss_ref[r, :] = jnp.broadcast_to(jnp.sum(yg * yg, axis=-1, keepdims=True), (BLOCK, LANES))
        o_ref[r, :] = (yg * ng_ref[...]).astype(o_ref.dtype)
        state_update(sb_ref, c, wb_ref, totb_ref, bm, xs)
        return carry

    lax.fori_loop(0, n_chunks, bwd, 0, unroll=2)
    if emit_state:
        for r in range(SSD_R):
            sfin_ref[1, r] = sb_ref[:, r * SSD_P:(r + 1) * SSD_P]


def _ssd_scan(zx, prep, conv_w, conv_b, d_exp, norm_gain, seq_len, n_seq, row_block0, s0=None, emit_state=False,
              fill=None):
    n_chunks = seq_len // BLOCK
    has_s0 = s0 is not None
    xcol0 = D_INNER // GW
    bcol0 = 2 * D_INNER // SSD_N
    ccol0 = bcol0 + SSD_GROUPS
    cwb0 = D_INNER // SSD_N

    def rowcol(width, col0):
        return pl.BlockSpec((seq_len, width), lambda b, g, col0=col0: (b + row_block0, col0 + g))

    def headrows(direction):
        return pl.BlockSpec((n_chunks, SSD_R, BLOCK),
                            lambda b, g, direction=direction: (b + row_block0, direction * SSD_GROUPS + g, 0))

    in_specs = [rowcol(GW, 0), rowcol(GW, xcol0), rowcol(SSD_N, bcol0), rowcol(SSD_N, ccol0)]
    args = [zx, zx, zx, zx]
    for arr in prep:
        in_specs += [headrows(0), headrows(1)]
        args += [arr, arr]
    in_specs += [pl.BlockSpec((CONV_W, GW), lambda b, g: (0, g)),
                 pl.BlockSpec((CONV_W, SSD_N), lambda b, g: (0, cwb0 + g)),
                 pl.BlockSpec((CONV_W, SSD_N), lambda b, g: (0, cwb0 + SSD_GROUPS + g)),
                 pl.BlockSpec((1, GW), lambda b, g: (0, g)),
                 pl.BlockSpec((1, SSD_N), lambda b, g: (0, cwb0 + g)),
                 pl.BlockSpec((1, SSD_N), lambda b, g: (0, cwb0 + SSD_GROUPS + g)),
                 pl.BlockSpec((1, GW), lambda b, g: (0, g)),
                 pl.BlockSpec((1, GW), lambda b, g: (0, g))]
    args += [conv_w, conv_w, conv_w, conv_b, conv_b, conv_b, d_exp, norm_gain.reshape(1, D_INNER)]
    state_spec = pl.BlockSpec((None, None, 2, SSD_R, SSD_N, SSD_P), lambda b, g: (b, 0, 0, g, 0, 0))
    if has_s0:
        in_specs.append(state_spec)
        args.append(s0)
    aliases = _fill_alias(fill, in_specs, args)
    out_specs = [pl.BlockSpec((seq_len, GW), lambda b, g: (b + row_block0, g)),
                 pl.BlockSpec((seq_len, LANES), lambda b, g: (b + row_block0, g))]
    out_shape = [jax.ShapeDtypeStruct((T_ALL, D_INNER), BF16),
                 jax.ShapeDtypeStruct((T_ALL, SSD_GROUPS * LANES), F32)]
    if emit_state:
        out_specs.append(state_spec)
        out_shape.append(jax.ShapeDtypeStruct((n_seq, 1, 2, SSD_HEADS, SSD_N, SSD_P), F32))
    return pl.pallas_call(
        functools.partial(_ssd_kernel, n_chunks, has_s0, emit_state, bool(fill)),
        grid=(n_seq, SSD_GROUPS),
        in_specs=in_specs,
        out_specs=out_specs,
        out_shape=out_shape,
        input_output_aliases=aliases,
        scratch_shapes=[pltpu.VMEM((seq_len + 2 * HALO, GW + 2 * SSD_N), F32),
                        pltpu.VMEM((seq_len, GW), F32), pltpu.VMEM((seq_len, SSD_N), BF16),
                        pltpu.VMEM((seq_len, SSD_N), BF16), pltpu.VMEM((n_chunks, SSD_N, GW), BF16),
                        pltpu.VMEM((SSD_N, GW), F32), pltpu.VMEM((SSD_N, GW), F32)],
        compiler_params=_cparams("arbitrary", "arbitrary"),
        name="ssd_scan_latent" if has_s0 else "ssd_scan_prompt",
    )(*args)


def _router_kernel(x_ref, g_ref, mod_ref, rw_ref, hn_ref, idx_ref, wgt_ref, hf_ref):
    _adaln_to(x_ref, g_ref, mod_ref, 3, 4, hf_ref)
    hf = hf_ref[...]
    h_hi = hf.astype(BF16)
    h_lo = (hf - h_hi.astype(F32)).astype(BF16)
    hn_ref[...] = h_hi
    rw = rw_ref[...]
    r_hi = rw.astype(BF16)
    r_lo = (rw - r_hi.astype(F32)).astype(BF16)
    t1 = jnp.dot(h_hi, jnp.concatenate([r_hi, r_lo], axis=1), preferred_element_type=F32)
    logits = t1[:, :LANES] + t1[:, LANES:] + jnp.dot(h_lo, r_hi, preferred_element_type=F32)
    lane = lax.broadcasted_iota(jnp.int32, logits.shape, 1)
    lg = jnp.where(lane < N_EXPERTS, logits, NEG_INF)
    m1 = jnp.max(lg, axis=-1, keepdims=True)
    i1 = jnp.min(jnp.where(lg == m1, lane, LANES), axis=-1, keepdims=True)
    lg2 = jnp.where(lane == i1, NEG_INF, lg)
    m2 = jnp.max(lg2, axis=-1, keepdims=True)
    i2 = jnp.min(jnp.where(lg2 == m2, lane, LANES), axis=-1, keepdims=True)
    e2 = jnp.exp(m2 - m1)
    w1 = 1.0 / (1.0 + e2)
    idx_ref[...] = jnp.where(lane == 0, i1, jnp.where(lane == 1, i2, 0))
    wgt_ref[...] = jnp.where(lane == 0, w1, jnp.where(lane == 1, e2 * w1, 0.0))


def _router(x, gain, modt, router_w):
    t, d = x.shape
    rw = jnp.pad(router_w, ((0, 0), (0, LANES - N_EXPERTS)))
    return pl.pallas_call(
        _router_kernel,
        grid=(t // ROW_TILE,),
        in_specs=[pl.BlockSpec((ROW_TILE, d), lambda i: (i, 0)),
                  pl.BlockSpec((1, d), lambda i: (0, 0)),
                  pl.BlockSpec((None, 8, d), lambda i: (i, 0, 0)),
                  pl.BlockSpec((d, LANES), lambda i: (0, 0))],
        out_specs=[pl.BlockSpec((ROW_TILE, d), lambda i: (i, 0)),
                   pl.BlockSpec((ROW_TILE, LANES), lambda i: (i, 0)),
                   pl.BlockSpec((ROW_TILE, LANES), lambda i: (i, 0))],
        out_shape=[jax.ShapeDtypeStruct((t, d), BF16), jax.ShapeDtypeStruct((t, LANES), jnp.int32),
                   jax.ShapeDtypeStruct((t, LANES), F32)],
        scratch_shapes=[pltpu.VMEM((ROW_TILE, d), F32)],
        compiler_params=_cparams("arbitrary"),
        name="moe_router",
    )(x, gain.reshape(1, d), modt, rw)


DOWN_ROWS = 512
DOWN_TN = 512


def _expert_changed(be_ref, blk, prev_blk, step):
    return jnp.logical_or(step == 0, be_ref[blk] != be_ref[prev_blk])


def _expert_up_kernel(be_ref, nu_ref, nv_ref, xs_ref, wg_ref, wu_ref, h_ref, wgb_ref, wub_ref):
    i = pl.program_id(1)
    used = nv_ref[i] > 0

    @pl.when(jnp.logical_and(used, _expert_changed(be_ref, i, jnp.maximum(i - 1, 0), i)))
    def _():
        wgb_ref[...] = wg_ref[...].astype(BF16)
        wub_ref[...] = wu_ref[...].astype(BF16)

    @pl.when(used)
    def _():
        xs = xs_ref[...]
        h_ref[...] = (_silu(_bdot(xs, wgb_ref[...])) * _bdot(xs, wub_ref[...])).astype(h_ref.dtype)

    @pl.when(jnp.logical_not(used))
    def _():
        h_ref[...] = jnp.zeros_like(h_ref)


def _expert_down_kernel(be_ref, nu_ref, nv_ref, h_ref, wd_ref, o_ref, wdb_ref):
    i = pl.program_id(1)
    per = MOE_ROWS // DOWN_ROWS
    blk = i // per
    used = nv_ref[blk] > (i % per) * DOWN_ROWS

    @pl.when(jnp.logical_and(nv_ref[blk] > 0, _expert_changed(be_ref, blk, jnp.maximum(i - 1, 0) // per, i)))
    def _():
        wdb_ref[...] = wd_ref[...].astype(BF16)

    @pl.when(used)
    def _():
        o_ref[...] = _bdot(h_ref[...], wdb_ref[...]).astype(o_ref.dtype)

    @pl.when(jnp.logical_not(used))
    def _():
        o_ref[...] = jnp.zeros_like(o_ref)


def _experts(xs_sorted, block_e, n_used, n_valid, wg, wu, wd):
    cap, d = xs_sorted.shape
    ff = wg.shape[2]
    per = MOE_ROWS // DOWN_ROWS

    def expert_of(blk, be, nu):
        return be[jnp.minimum(blk, jnp.maximum(nu[0] - 1, 0))]

    h = pl.pallas_call(
        _expert_up_kernel,
        grid_spec=pltpu.PrefetchScalarGridSpec(
            num_scalar_prefetch=3,
            grid=(ff // FFN_TF, cap // MOE_ROWS),
            in_specs=[pl.BlockSpec((MOE_ROWS, d), lambda f, i, be, nu, nv: (i, 0)),
                      pl.BlockSpec((None, d, FFN_TF), lambda f, i, be, nu, nv: (expert_of(i, be, nu), 0, f)),
                      pl.BlockSpec((None, d, FFN_TF), lambda f, i, be, nu, nv: (expert_of(i, be, nu), 0, f))],
            out_specs=pl.BlockSpec((MOE_ROWS, FFN_TF), lambda f, i, be, nu, nv: (i, f)),
            scratch_shapes=[pltpu.VMEM((d, FFN_TF), BF16), pltpu.VMEM((d, FFN_TF), BF16)],
        ),
        out_shape=jax.ShapeDtypeStruct((cap, ff), BF16),
        compiler_params=_cparams("arbitrary", "arbitrary"),
        name="moe_expert_up",
    )(block_e, n_used, n_valid, xs_sorted, wg, wu)
    return pl.pallas_call(
        _expert_down_kernel,
        grid_spec=pltpu.PrefetchScalarGridSpec(
            num_scalar_prefetch=3,
            grid=(d // DOWN_TN, cap // DOWN_ROWS),
            in_specs=[pl.BlockSpec((DOWN_ROWS, ff), lambda n, i, be, nu, nv: (i, 0)),
                      pl.BlockSpec((None, ff, DOWN_TN),
                                   lambda n, i, be, nu, nv: (expert_of(i // per, be, nu), 0, n))],
            out_specs=pl.BlockSpec((DOWN_ROWS, DOWN_TN), lambda n, i, be, nu, nv: (i, n)),
            scratch_shapes=[pltpu.VMEM((ff, DOWN_TN), BF16)],
        ),
        out_shape=jax.ShapeDtypeStruct((cap, d), BF16),
        compiler_params=_cparams("arbitrary", "arbitrary"),
        name="moe_expert_down",
    )(block_e, n_used, n_valid, h, wd)


def _combine_kernel(x_ref, g0_ref, g1_ref, w_ref, mod_ref, o_ref):
    w = w_ref[...]
    y = w[:, 0:1] * g0_ref[...].astype(F32) + w[:, 1:2] * g1_ref[...].astype(F32)
    o_ref[...] = x_ref[...] + mod_ref[5:6, :] * y


COMB_ROWS = 512


def _combine(x, g, wgt, modt, row0, n_rows):
    t, d = x.shape
    b0 = row0 // COMB_ROWS
    per = ROW_TILE // COMB_ROWS
    return pl.pallas_call(
        _combine_kernel,
        grid=(n_rows // COMB_ROWS,),
        in_specs=[pl.BlockSpec((COMB_ROWS, d), lambda i: (i + b0, 0)),
                  pl.BlockSpec((COMB_ROWS, d), lambda i: (i + b0, 0)),
                  pl.BlockSpec((COMB_ROWS, d), lambda i: (i + b0 + t // COMB_ROWS, 0)),
                  pl.BlockSpec((COMB_ROWS, LANES), lambda i: (i + b0, 0)),
                  pl.BlockSpec((None, 8, d), lambda i: ((i + b0) // per, 0, 0))],
        out_specs=pl.BlockSpec((COMB_ROWS, d), lambda i: (i, 0)),
        out_shape=jax.ShapeDtypeStruct((n_rows, d), F32),
        compiler_params=_cparams("arbitrary"),
        name="moe_combine",
    )(x, g, g, wgt, modt)


def _moe(x, gain, modt, router_w, wg, wu, wd):
    t, d = x.shape
    hn, idx, wgt = _router(x, gain, modt, router_w)
    top_idx = idx[:, :TOP_K]
    n_slots = t * TOP_K
    flat_e = top_idx.reshape(-1)
    onehot = (flat_e[:, None] == jnp.arange(N_EXPERTS, dtype=jnp.int32)[None, :]).astype(jnp.int32)
    incl = jnp.cumsum(onehot, axis=0)
    counts = incl[-1]
    rank = jnp.sum((incl - onehot) * onehot, axis=1)
    padded = (counts + MOE_ROWS - 1) // MOE_ROWS * MOE_ROWS
    pend = jnp.cumsum(padded)
    pstart = pend - padded
    dest = pstart[flat_e] + rank
    n_blocks = n_slots // MOE_ROWS + N_EXPERTS
    cap = n_blocks * MOE_ROWS
    row_tok = (jnp.arange(cap, dtype=jnp.int32) % t).at[dest].set(jnp.arange(n_slots, dtype=jnp.int32) // TOP_K)
    block_e = jnp.clip(jnp.searchsorted(pend, jnp.arange(n_blocks, dtype=jnp.int32) * MOE_ROWS, side='right'),
                       0, N_EXPERTS - 1).astype(jnp.int32)
    n_used = (pend[-1:] // MOE_ROWS).astype(jnp.int32)
    blk = jnp.arange(n_blocks, dtype=jnp.int32)
    n_valid = jnp.where(blk < n_used[0],
                        jnp.clip((pstart + counts)[block_e] - blk * MOE_ROWS, 0, MOE_ROWS), 0).astype(jnp.int32)
    out = _experts(hn[row_tok], block_e, n_used, n_valid, wg, wu, wd)
    g = out[dest.reshape(t, TOP_K).T.reshape(-1)]
    return _combine(x, g, wgt, modt, 0, T_PROMPT), _combine(x, g, wgt, modt, T_PROMPT, T_SAMPLE)


def _rope_tables(n_tokens, dim):
    n_rows = n_tokens // GRID_W
    row = jnp.repeat(jnp.arange(n_rows), GRID_W).astype(F32)
    col = jnp.tile(jnp.arange(GRID_W), n_rows).astype(F32)
    n_freq = dim // 4
    inv = ROPE_BASE ** (-jnp.arange(n_freq, dtype=F32) / n_freq)
    ang = jnp.concatenate([row[:, None] * inv, col[:, None] * inv], axis=-1)
    return jnp.cos(ang), jnp.sin(ang)


def kernel(x_prompt, x_sample, state_ret, cache_k, cache_v, state_ssd, c, c_ctx, ada_w, ada_b, norm_mix, norm_ffn, ev_w_in, ev_w_out, ret_decay_logit, ret_norm, att_q_norm, att_k_norm, att_sink, ffn_w_gate, ffn_w_up, ffn_w_down, ssd_w_in, ssd_conv_w, ssd_conv_b, ssd_a_log, ssd_dt_bias, ssd_d, ssd_norm, ssd_w_out, moe_router, moe_w_gate, moe_w_up, moe_w_down):
    d = D_MODEL
    x = (x_prompt.reshape(T_PROMPT, d), x_sample.reshape(T_SAMPLE, d))

    cvecs = jnp.concatenate([c_ctx[None, :], c, jnp.zeros((MOD_ROWS - 1 - DEC_BATCH, d), F32)], axis=0)
    mods = _modulation(cvecs, ada_w, ada_b).reshape(2, MOD_ROWS, 6, d)
    tiles_per_seq = DEC_SEQ // ROW_TILE
    tile_row = jnp.concatenate([jnp.zeros((T_PROMPT // ROW_TILE,), jnp.int32),
                                1 + jnp.arange(T_SAMPLE // ROW_TILE, dtype=jnp.int32) // tiles_per_seq])
    modt = jnp.pad(mods[:, tile_row], ((0, 0), (0, 0), (0, 2), (0, 0)))

    ev_w_in_b, ev_w_out_b = ev_w_in[0].astype(BF16), ev_w_out[0].astype(BF16)
    ffn_wg_b, ffn_wu_b, ffn_wd_b = ffn_w_gate[0].astype(BF16), ffn_w_up[0].astype(BF16), ffn_w_down[0].astype(BF16)
    ssd_w_in_b, ssd_w_out_b = ssd_w_in[0].astype(BF16), ssd_w_out[0].astype(BF16)

    proj, = _adaln_matmul(x, norm_mix[0], modt[0], 0, 1, ev_w_in_b, tn=512, out_dtype=BF16, name="even_in_proj")
    lg = jax.nn.log_sigmoid(ret_decay_logit[0].astype(F32))
    cos_r, sin_r = _rope_tables(DEC_SEQ, RET_DK)
    cos_a, sin_a = _rope_tables(DEC_SEQ, ATT_HD)
    cos_a2 = jnp.concatenate([cos_a, cos_a], axis=-1)
    sin_a2 = jnp.concatenate([-sin_a, sin_a], axis=-1)
    mix_ret, new_state_ret = _retention(proj, lg, ret_norm[0], SEQ, BATCH, 0, emit_state=True)
    mix_ret, = _retention(proj, lg, ret_norm[0], DEC_SEQ, DEC_BATCH, T_PROMPT // DEC_SEQ,
                          ropes=(cos_r, sin_r), s0=state_ret, fill=(mix_ret,))
    mix_att, new_k, new_v = _attention_prompt(proj, att_sink[0], att_q_norm[0], att_k_norm[0])
    mix_att = _attention_latent(proj, cache_k[:, 0], cache_v[:, 0], att_sink[0], cos_a2, sin_a2,
                                att_q_norm[0], att_k_norm[0], mix_att)
    x = _proj_residual([mix_ret, mix_att], ev_w_out_b, x, modt[0], 2, tn=1024, name="even_out_proj")
    h = _ffn_gateup(x, norm_ffn[0], modt[0], ffn_wg_b, ffn_wu_b)
    x = _proj_residual([h], ffn_wd_b, x, modt[0], 5, tn=512, name="ffn_down")

    zx, dt_raw = _adaln_matmul(x, norm_mix[1], modt[1], 0, 1, ssd_w_in_b, tn=1024, n_out=SSD_ZX,
                               tail=2 * SSD_HEADS, out_dtype=BF16, name="ssd_in_proj")
    prep = _ssd_prep(dt_raw, ssd_dt_bias[0], ssd_a_log[0])
    d_exp = jnp.repeat(ssd_d[0], SSD_P)[None, :]
    conv_b = ssd_conv_b[0][None, :]
    yg, yss, new_state_ssd = _ssd_scan(zx, prep, ssd_conv_w[0], conv_b, d_exp, ssd_norm[0], SEQ, BATCH, 0,
                                       emit_state=True)
    yg, yss = _ssd_scan(zx, prep, ssd_conv_w[0], conv_b, d_exp, ssd_norm[0], DEC_SEQ, DEC_BATCH,
                        T_PROMPT // DEC_SEQ, s0=state_ssd, fill=(yg, yss))
    x = _proj_residual([yg], ssd_w_out_b, x, modt[1], 2, tn=512, row_ss=yss, name="ssd_out_proj")
    y_p, y_s = _moe(x, norm_ffn[1], modt[1], moe_router[0], moe_w_gate[0], moe_w_up[0], moe_w_down[0])

    y_prompt = y_p.reshape(BATCH, SEQ, d)
    y_sample = y_s.reshape(DEC_BATCH, DEC_SEQ, d)
    new_cache_k = new_k.reshape(BATCH, 1, SEQ, ATT_KV_HEADS, ATT_HD)
    new_cache_v = new_v.reshape(BATCH, 1, SEQ, ATT_KV_HEADS, ATT_HD)
    return (y_prompt, y_sample, new_state_ret, new_cache_k, new_cache_v, new_state_ssd)
```

```python
import functools

import jax
import jax.numpy as jnp
from jax import lax
from jax.experimental import pallas as pl
from jax.experimental.pallas import tpu as pltpu

F32 = jnp.float32
BF16 = jnp.bfloat16

D_MODEL = 2048
BATCH = 16
SEQ = 256
DEC_BATCH = 8
DEC_SEQ = 2048
PAST_LEN = 512
GRID_W = 64
BLOCK = 128
WINDOW = 128
EPS = 1e-6
ROPE_BASE = 10000.0
RET_HEADS = 4
RET_DK = 256
RET_DV = 256
ATT_HEADS = 8
ATT_KV_HEADS = 2
ATT_HD = 128
ATT_GROUP = ATT_HEADS // ATT_KV_HEADS
EVEN_IN = 5632
D_INNER = 2 * D_MODEL
SSD_P = 64
SSD_HEADS = D_INNER // SSD_P
SSD_N = 128
SSD_GROUPS = 8
SSD_R = SSD_HEADS // SSD_GROUPS
CONV_W = 5
CONV_CH = D_INNER + 2 * SSD_GROUPS * SSD_N
SSD_ZX = D_INNER + CONV_CH
D_FF = 5632
N_EXPERTS = 8
TOP_K = 2

T_PROMPT = BATCH * SEQ
T_SAMPLE = DEC_BATCH * DEC_SEQ
T_ALL = T_PROMPT + T_SAMPLE

LANES = 128
ROW_TILE = 1024
VMEM_LIMIT = 56 * 1024 * 1024
N_ROW_TILES = T_ALL // ROW_TILE
MOE_ROWS = 1024
NEG_INF = float("-inf")


def _cparams(*sem):
    return pltpu.CompilerParams(dimension_semantics=sem, vmem_limit_bytes=VMEM_LIMIT)


def _silu(x):
    return x * jax.nn.sigmoid(x)


def _bdot(a, b):
    return jnp.dot(a.astype(BF16), b.astype(BF16), preferred_element_type=F32)


def _bdot_nt(a, b):
    return lax.dot_general(a.astype(BF16), b.astype(BF16), (((1,), (1,)), ((), ())),
                           preferred_element_type=F32)


def _bdot_tn(a, b):
    return lax.dot_general(a.astype(BF16), b.astype(BF16), (((0,), (0,)), ((), ())),
                           preferred_element_type=F32)


def _rows(c, n=BLOCK):
    return pl.ds(pl.multiple_of(c * n, n), n)


MOD_ROWS = 16
MOD_TN = 1024


def _mod_kernel(c_ref, w_ref, b_ref, o_ref):
    o_ref[...] = _bdot(_silu(c_ref[...]), w_ref[...]) + b_ref[...]


def _modulation(cvecs, ada_w, ada_b):
    depth, d, n = ada_w.shape
    return pl.pallas_call(
        _mod_kernel,
        grid=(depth, n // MOD_TN),
        in_specs=[pl.BlockSpec((MOD_ROWS, d), lambda l, j: (0, 0)),
                  pl.BlockSpec((None, d, MOD_TN), lambda l, j: (l, 0, j)),
                  pl.BlockSpec((None, 1, MOD_TN), lambda l, j: (l, 0, j))],
        out_specs=pl.BlockSpec((None, MOD_ROWS, MOD_TN), lambda l, j: (l, 0, j)),
        out_shape=jax.ShapeDtypeStruct((depth, MOD_ROWS, n), F32),
        compiler_params=_cparams("arbitrary", "arbitrary"),
        name="modulation",
    )(cvecs, ada_w, ada_b.reshape(depth, 1, n))


ADALN_CHUNK = 64
ADALN_DOT_ROWS = 256


def _adaln_then(x_ref, g_ref, mod_ref, shift_row, scale_row, hn_ref, consume):
    mult = g_ref[...] * (1.0 + mod_ref[scale_row:scale_row + 1, :])
    shift = mod_ref[shift_row:shift_row + 1, :]
    for r0 in range(0, x_ref.shape[0], ADALN_DOT_ROWS):
        for s0 in range(r0, r0 + ADALN_DOT_ROWS, ADALN_CHUNK):
            r = slice(s0, s0 + ADALN_CHUNK)
            x = x_ref[r, :]
            ms = jnp.mean(x * x, axis=-1, keepdims=True)
            hn_ref[r, :] = (x * lax.rsqrt(ms + EPS) * mult + shift).astype(hn_ref.dtype)
        consume(slice(r0, r0 + ADALN_DOT_ROWS))


PROMPT_TILES = T_PROMPT // ROW_TILE


def _x_rows_specs(x, cols, col_of_j):
    if not isinstance(x, tuple):
        return [pl.BlockSpec((ROW_TILE, cols), lambda i, j: (i, col_of_j(j)))], [x]
    return ([pl.BlockSpec((ROW_TILE, cols),
                          lambda i, j: (jnp.minimum(i, PROMPT_TILES - 1), jnp.where(i < PROMPT_TILES, col_of_j(j), 0))),
             pl.BlockSpec((ROW_TILE, cols),
                          lambda i, j: (jnp.maximum(i - PROMPT_TILES, 0), jnp.where(i < PROMPT_TILES, 0, col_of_j(j))))],
            list(x))


def _for_x_rows(x_refs, fn):
    if len(x_refs) == 1:
        fn(x_refs[0])
        return
    in_prompt = pl.program_id(0) < PROMPT_TILES
    pl.when(in_prompt)(lambda: fn(x_refs[0]))
    pl.when(jnp.logical_not(in_prompt))(lambda: fn(x_refs[1]))


def _adaln_mm_kernel(shift_row, scale_row, has_tail, n_x, *refs):
    x_refs, (g_ref, mod_ref, w_ref), refs = refs[:n_x], refs[n_x:n_x + 3], refs[n_x + 3:]
    if has_tail:
        wt_ref, o_ref, ot_ref, hn_ref = refs
    else:
        o_ref, hn_ref = refs

    def project(r):
        hn = hn_ref[r, :]
        o_ref[r, :] = _bdot(hn, w_ref[...]).astype(o_ref.dtype)
        if has_tail:
            ot_ref[r, :] = _bdot(hn, wt_ref[...])

    first = pl.program_id(1) == 0

    @pl.when(first)
    def _():
        _for_x_rows(x_refs, lambda x_ref: _adaln_then(x_ref, g_ref, mod_ref, shift_row, scale_row, hn_ref, project))

    @pl.when(jnp.logical_not(first))
    def _():
        o_ref[...] = _bdot(hn_ref[...], w_ref[...]).astype(o_ref.dtype)


def _adaln_matmul(x, gain, modt, shift_row, scale_row, w, tn, n_out=None, tail=0, out_dtype=F32, name="adaln_mm"):
    t, d = T_ALL, w.shape[0]
    n_out = w.shape[1] if n_out is None else n_out
    in_specs, args = _x_rows_specs(x, d, lambda j: 0)
    n_x = len(args)
    in_specs += [pl.BlockSpec((1, d), lambda i, j: (0, 0)),
                 pl.BlockSpec((None, 8, d), lambda i, j: (i, 0, 0)),
                 pl.BlockSpec((d, tn), lambda i, j: (0, j))]
    args += [gain.reshape(1, d), modt, w]
    out_specs = [pl.BlockSpec((ROW_TILE, tn), lambda i, j: (i, j))]
    out_shape = [jax.ShapeDtypeStruct((t, n_out), out_dtype)]
    if tail:
        in_specs.append(pl.BlockSpec((d, tail), lambda i, j: (0, n_out // tail)))
        args.append(w)
        out_specs.append(pl.BlockSpec((ROW_TILE, tail), lambda i, j: (i, 0)))
        out_shape.append(jax.ShapeDtypeStruct((t, tail), F32))
    return pl.pallas_call(
        functools.partial(_adaln_mm_kernel, shift_row, scale_row, bool(tail), n_x),
        grid=(t // ROW_TILE, n_out // tn),
        in_specs=in_specs,
        out_specs=out_specs,
        out_shape=out_shape,
        scratch_shapes=[pltpu.VMEM((ROW_TILE, d), BF16)],
        compiler_params=_cparams("arbitrary", "arbitrary"),
        name=name,
    )(*args)


FFN_TF = 512


def _gateup_kernel(x_ref, g_ref, mod_ref, wg_ref, wu_ref, h_ref, hn_ref):
    def gate_up(r):
        hn = hn_ref[r, :]
        h_ref[r, :] = (_silu(_bdot(hn, wg_ref[...])) * _bdot(hn, wu_ref[...])).astype(h_ref.dtype)

    first = pl.program_id(1) == 0

    @pl.when(first)
    def _():
        _adaln_then(x_ref, g_ref, mod_ref, 3, 4, hn_ref, gate_up)

    @pl.when(jnp.logical_not(first))
    def _():
        gate_up(slice(None))


def _ffn_gateup(x, gain, modt, wg, wu):
    t, d = x.shape
    ff = wg.shape[1]
    return pl.pallas_call(
        _gateup_kernel,
        grid=(t // ROW_TILE, ff // FFN_TF),
        in_specs=[pl.BlockSpec((ROW_TILE, d), lambda i, f: (i, 0)),
                  pl.BlockSpec((1, d), lambda i, f: (0, 0)),
                  pl.BlockSpec((None, 8, d), lambda i, f: (i, 0, 0)),
                  pl.BlockSpec((d, FFN_TF), lambda i, f: (0, f)),
                  pl.BlockSpec((d, FFN_TF), lambda i, f: (0, f))],
        out_specs=pl.BlockSpec((ROW_TILE, FFN_TF), lambda i, f: (i, f)),
        out_shape=jax.ShapeDtypeStruct((t, ff), BF16),
        scratch_shapes=[pltpu.VMEM((ROW_TILE, d), BF16)],
        compiler_params=_cparams("arbitrary", "arbitrary"),
        name="ffn_gateup",
    )(x, gain.reshape(1, d), modt, wg, wu)


def _proj_res_kernel(n_a, n_x, gate_row, norm, *refs):
    a_refs = refs[:n_a]
    w_refs = refs[n_a:2 * n_a]
    x_refs, mod_ref = refs[2 * n_a:2 * n_a + n_x], refs[2 * n_a + n_x]
    pos = 2 * n_a + n_x + 1
    o_ref = refs[pos + 1] if norm else refs[pos]
    acc = _bdot(a_refs[0][...], w_refs[0][...])
    for k in range(1, n_a):
        acc = acc + _bdot(a_refs[k][...], w_refs[k][...])
    if norm:
        ss_ref = refs[pos]
        k_total = sum(a.shape[1] for a in a_refs)
        ss = ss_ref[:, 0:LANES]
        for k in range(1, ss_ref.shape[1] // LANES):
            ss = ss + ss_ref[:, k * LANES:(k + 1) * LANES]
        rs = lax.rsqrt(ss * (1.0 / k_total) + EPS)
        acc = acc * jnp.concatenate([rs] * (acc.shape[1] // LANES), axis=1)
    upd = mod_ref[gate_row:gate_row + 1, :] * acc

    def finish(x_ref):
        o_ref[...] = x_ref[...] + upd

    _for_x_rows(x_refs, finish)


def _proj_residual(a_list, w, x, modt, gate_row, tn, row_ss=None, name="proj_res"):
    t, d = T_ALL, w.shape[1]
    n_a = len(a_list)
    norm = row_ss is not None
    in_specs, args, k0 = [], [], 0
    for a in a_list:
        in_specs.append(pl.BlockSpec((ROW_TILE, a.shape[1]), lambda i, j: (i, 0)))
        args.append(a)
    for a in a_list:
        ka = a.shape[1]
        assert k0 % ka == 0
        in_specs.append(pl.BlockSpec((ka, tn), lambda i, j, kb=k0 // ka: (kb, j)))
        args.append(w)
        k0 += ka
    x_specs, x_args = _x_rows_specs(x, tn, lambda j: j)
    in_specs += x_specs + [pl.BlockSpec((None, 8, tn), lambda i, j: (i, 0, j))]
    args += x_args + [modt]
    if norm:
        in_specs.append(pl.BlockSpec((ROW_TILE, row_ss.shape[1]), lambda i, j: (i, 0)))
        args.append(row_ss)
    return pl.pallas_call(
        functools.partial(_proj_res_kernel, n_a, len(x_args), gate_row, norm),
        grid=(t // ROW_TILE, d // tn),
        in_specs=in_specs,
        out_specs=pl.BlockSpec((ROW_TILE, tn), lambda i, j: (i, j)),
        out_shape=jax.ShapeDtypeStruct((t, d), F32),
        compiler_params=_cparams("arbitrary", "arbitrary"),
        name=name,
    )(*args)


def _ret_kernel(n_chunks, rope, has_s0, emit_state, has_fill, lg_ref, *refs):
    it = iter(refs)
    q_ref, k_ref, v_ref, gt_ref = next(it), next(it), next(it), next(it)
    cos_ref, sin_ref = (next(it), next(it)) if rope else (None, None)
    s0_ref = next(it) if has_s0 else None
    gain_ref = next(it)
    if has_fill:
        next(it)
    o_ref = next(it)
    sfin_ref = next(it) if emit_state else None
    qs_ref, ks_ref, sfs_ref, dm_ref, dec_ref, sf_ref, sb_ref = (next(it) for _ in range(7))

    h = pl.program_id(1)
    lgf = lg_ref[0, h]
    lgb = lg_ref[1, h]
    half = RET_DK // 2

    def prep(c, carry):
        r = _rows(c)
        q = q_ref[r, :].astype(F32)
        k = k_ref[r, :].astype(F32) * (RET_DK ** -0.5)
        if rope:
            cs, sn = cos_ref[r, :], sin_ref[r, :]
            for src, dst in ((q, qs_ref), (k, ks_ref)):
                x1, x2 = src[:, :half], src[:, half:]
                dst[r, :half] = (x1 * cs - x2 * sn).astype(BF16)
                dst[r, half:] = (x1 * sn + x2 * cs).astype(BF16)
        else:
            qs_ref[r, :] = q.astype(BF16)
            ks_ref[r, :] = k.astype(BF16)
        return carry

    lax.fori_loop(0, n_chunks, prep, 0)

    ii = lax.broadcasted_iota(jnp.int32, (BLOCK, BLOCK), 0)
    jj = lax.broadcasted_iota(jnp.int32, (BLOCK, BLOCK), 1)
    diff = (ii - jj).astype(F32)
    dm_ref[...] = jnp.exp(jnp.where(jj <= ii, diff * lgf, -diff * lgb))
    pos = lax.broadcasted_iota(jnp.int32, (BLOCK, RET_DV), 0).astype(F32)
    dec_ref[0] = jnp.exp((pos + 1.0) * lgf)
    dec_ref[1] = jnp.exp((BLOCK - pos) * lgb)
    dec_ref[2] = jnp.exp((BLOCK - 1.0 - pos) * lgf)
    dec_ref[3] = jnp.exp(pos * lgb)
    tot_f = jnp.exp(jnp.full((1, RET_DV), BLOCK * lgf, F32))
    tot_b = jnp.exp(jnp.full((1, RET_DV), BLOCK * lgb, F32))

    if has_s0:
        sf_ref[...] = s0_ref[0]
        sb_ref[...] = s0_ref[1]
    else:
        sf_ref[...] = jnp.zeros_like(sf_ref)
        sb_ref[...] = jnp.zeros_like(sb_ref)

    def fwd(c, carry):
        r = _rows(c)
        sfs_ref[c] = sf_ref[...].astype(BF16)
        kd = ks_ref[r, :].astype(F32) * dec_ref[2]
        sf_ref[...] = sf_ref[...] * tot_f + _bdot_tn(kd, v_ref[r, :])
        return carry

    lax.fori_loop(0, n_chunks, fwd, 0, unroll=min(4, n_chunks))
    if emit_state:
        sfin_ref[0] = sf_ref[...]

    def bwd(t, carry):
        c = n_chunks - 1 - t
        r = _rows(c)
        q = qs_ref[r, :]
        k = ks_ref[r, :]
        v = v_ref[r, :].astype(BF16)
        p = _bdot_nt(q, k) * dm_ref[...]
        o = _bdot(p, v)
        o = o + _bdot(q, sfs_ref[c]) * dec_ref[0]
        o = o + _bdot(q, sb_ref[...]) * dec_ref[1]
        ms = jnp.mean(o * o, axis=-1, keepdims=True)
        y = o * lax.rsqrt(ms + EPS) * gain_ref[...]
        o_ref[r, :] = (y * _silu(gt_ref[r, :].astype(F32))).astype(o_ref.dtype)
        kd = k.astype(F32) * dec_ref[3]
        sb_ref[...] = sb_ref[...] * tot_b + _bdot_tn(kd, v)
        return carry

    lax.fori_loop(0, n_chunks, bwd, 0, unroll=min(4, n_chunks))
    if emit_state:
        sfin_ref[1] = sb_ref[...]


def _fill_alias(fill, in_specs, args):
    aliases = {}
    for k, arr in enumerate(fill or ()):
        in_specs.append(pl.BlockSpec(memory_space=pl.ANY))
        args.append(arr)
        aliases[len(args) - 1] = k
    return aliases


def _retention(proj, lg, ret_norm, seq_len, n_seq, row_block0, ropes=None, s0=None, emit_state=False, fill=None):
    n_chunks = seq_len // BLOCK
    rope, has_s0 = ropes is not None, s0 is not None

    def col(cb):
        return pl.BlockSpec((seq_len, RET_DK), lambda b, h, cb=cb: (b + row_block0, cb * RET_HEADS + h))

    in_specs = [pl.BlockSpec(memory_space=pltpu.SMEM), col(0), col(1), col(2), col(3)]
    args = [lg, proj, proj, proj, proj]
    if rope:
        in_specs += [pl.BlockSpec((seq_len, RET_DK // 2), lambda b, h: (0, 0))] * 2
        args += list(ropes)
    if has_s0:
        in_specs.append(pl.BlockSpec((None, None, 2, None, RET_DK, RET_DV), lambda b, h: (b, 0, 0, h, 0, 0)))
        args.append(s0)
    in_specs.append(pl.BlockSpec((1, RET_DV), lambda b, h: (0, h)))
    args.append(ret_norm.reshape(1, RET_HEADS * RET_DV))
    aliases = _fill_alias(fill, in_specs, args)
    out_specs = [pl.BlockSpec((seq_len, RET_DV), lambda b, h: (b + row_block0, h))]
    out_shape = [jax.ShapeDtypeStruct((T_ALL, RET_HEADS * RET_DV), BF16)]
    if emit_state:
        out_specs.append(pl.BlockSpec((None, None, 2, None, RET_DK, RET_DV), lambda b, h: (b, 0, 0, h, 0, 0)))
        out_shape.append(jax.ShapeDtypeStruct((n_seq, 1, 2, RET_HEADS, RET_DK, RET_DV), F32))
    return pl.pallas_call(
        functools.partial(_ret_kernel, n_chunks, rope, has_s0, emit_state, bool(fill)),
        grid=(n_seq, RET_HEADS),
        in_specs=in_specs,
        out_specs=out_specs,
        out_shape=out_shape,
        input_output_aliases=aliases,
        scratch_shapes=[pltpu.VMEM((seq_len, RET_DK), BF16), pltpu.VMEM((seq_len, RET_DK), BF16),
                        pltpu.VMEM((n_chunks, RET_DK, RET_DV), BF16), pltpu.VMEM((BLOCK, BLOCK), F32),
                        pltpu.VMEM((4, BLOCK, RET_DV), F32), pltpu.VMEM((RET_DK, RET_DV), F32),
                        pltpu.VMEM((RET_DK, RET_DV), F32)],
        compiler_params=_cparams("arbitrary", "arbitrary"),
        name="retention_latent" if rope else "retention_prompt",
    )(*args)


def _head_norm(x, gain):
    x = x.astype(F32)
    return x * lax.rsqrt(jnp.mean(x * x, axis=-1, keepdims=True) + EPS) * gain


def _rope_full(x, cs, sn):
    return x * cs + pltpu.roll(x, ATT_HD // 2, 1) * sn


ATT_SPLIT = 2


def _sink_column(sink_ref, head0, n_heads, rows_per_head):
    n = n_heads * rows_per_head
    head = lax.broadcasted_iota(jnp.int32, (n, 1), 0) // rows_per_head
    col = jnp.full((n, 1), sink_ref[head0], F32)
    for g in range(1, n_heads):
        col = jnp.where(head == g, sink_ref[head0 + g], col)
    return col


def _att_latent_kernel(sink_ref, q_ref, k_ref, v_ref, ck_ref, cv_ref, cos_ref, sin_ref, qn_ref, kn_ref,
                       fill_ref, o_ref, kp_ref, vp_ref, ckp_ref, cvp_ref):
    del fill_ref
    kh, qb = pl.program_id(1), pl.program_id(2)
    n_chunks = DEC_SEQ // BLOCK
    loc = 3 * BLOCK

    @pl.when(qb == 0)
    def _():
        def prep(c, carry):
            r = _rows(c)
            kp_ref[r, :] = _rope_full(_head_norm(k_ref[r, :], kn_ref[...]), cos_ref[r, :], sin_ref[r, :]).astype(BF16)
            vp_ref[r, :ATT_HD] = v_ref[r, :].astype(BF16)
            vp_ref[r, ATT_HD:] = jnp.ones((BLOCK, ATT_HD), BF16)
            return carry

        lax.fori_loop(0, n_chunks, prep, 0)
        ckp_ref[...] = ck_ref[...].astype(BF16)
        cvp_ref[:, :ATT_HD] = cv_ref[...].astype(BF16)
        cvp_ref[:, ATT_HD:] = jnp.ones((PAST_LEN, ATT_HD), BF16)

    rq = _rows(qb)
    cs, sn = cos_ref[rq, :], sin_ref[rq, :]
    qg = qn_ref[...] * (ATT_HD ** -0.5)
    start = pl.multiple_of(jnp.clip((qb - 1) * BLOCK, 0, DEC_SEQ - loc), BLOCK)
    kl, vl = kp_ref[pl.ds(start, loc), :], vp_ref[pl.ds(start, loc), :]
    qpos = qb * BLOCK + (lax.broadcasted_iota(jnp.int32, (ATT_SPLIT * BLOCK, loc), 0) & (BLOCK - 1))
    kpos = start + lax.broadcasted_iota(jnp.int32, (ATT_SPLIT * BLOCK, loc), 1)
    in_window = jnp.abs(qpos - kpos) <= WINDOW
    for g0 in range(0, ATT_GROUP, ATT_SPLIT):
        q = jnp.concatenate(
            [_rope_full(_head_norm(q_ref[:, g * ATT_HD:(g + 1) * ATT_HD], qg), cs, sn).astype(BF16)
             for g in range(g0, g0 + ATT_SPLIT)], axis=0)
        s_loc = jnp.where(in_window, _bdot_nt(q, kl), NEG_INF)
        s_ctx = _bdot_nt(q, ckp_ref[...])
        sink = _sink_column(sink_ref, kh * ATT_GROUP + g0, ATT_SPLIT, BLOCK)
        m = jnp.maximum(jnp.maximum(jnp.max(s_loc, axis=-1, keepdims=True),
                                    jnp.max(s_ctx, axis=-1, keepdims=True)), sink)
        pv = _bdot(jnp.exp(s_loc - m), vl) + _bdot(jnp.exp(s_ctx - m), cvp_ref[...])
        o = pv[:, :ATT_HD] / (pv[:, ATT_HD:] + jnp.exp(sink - m))
        for g in range(ATT_SPLIT):
            o_ref[:, (g0 + g) * ATT_HD:(g0 + g + 1) * ATT_HD] = o[g * BLOCK:(g + 1) * BLOCK, :].astype(o_ref.dtype)


def _attention_latent(proj, cache_k, cache_v, sink, cos_a, sin_a, q_norm, k_norm, fill):
    nqb = DEC_SEQ // BLOCK
    rb0 = T_PROMPT // DEC_SEQ
    qcol0 = 4 * RET_HEADS * RET_DK // (ATT_GROUP * ATT_HD)
    kcol0 = (4 * RET_HEADS * RET_DK + ATT_HEADS * ATT_HD) // ATT_HD
    vcol0 = kcol0 + ATT_KV_HEADS
    ck = cache_k.reshape(DEC_BATCH, PAST_LEN, ATT_KV_HEADS * ATT_HD)
    cv = cache_v.reshape(DEC_BATCH, PAST_LEN, ATT_KV_HEADS * ATT_HD)
    return pl.pallas_call(
        _att_latent_kernel,
        grid=(DEC_BATCH, ATT_KV_HEADS, nqb),
        in_specs=[pl.BlockSpec(memory_space=pltpu.SMEM),
                  pl.BlockSpec((BLOCK, ATT_GROUP * ATT_HD),
                               lambda b, kh, qb: (T_PROMPT // BLOCK + b * nqb + qb, qcol0 + kh)),
                  pl.BlockSpec((DEC_SEQ, ATT_HD), lambda b, kh, qb: (rb0 + b, kcol0 + kh)),
                  pl.BlockSpec((DEC_SEQ, ATT_HD), lambda b, kh, qb: (rb0 + b, vcol0 + kh)),
                  pl.BlockSpec((None, PAST_LEN, ATT_HD), lambda b, kh, qb: (b, 0, kh)),
                  pl.BlockSpec((None, PAST_LEN, ATT_HD), lambda b, kh, qb: (b, 0, kh)),
                  pl.BlockSpec((DEC_SEQ, ATT_HD), lambda b, kh, qb: (0, 0)),
                  pl.BlockSpec((DEC_SEQ, ATT_HD), lambda b, kh, qb: (0, 0)),
                  pl.BlockSpec((1, ATT_HD), lambda b, kh, qb: (0, 0)),
                  pl.BlockSpec((1, ATT_HD), lambda b, kh, qb: (0, 0)),
                  pl.BlockSpec(memory_space=pl.ANY)],
        out_specs=pl.BlockSpec((BLOCK, ATT_GROUP * ATT_HD),
                               lambda b, kh, qb: (T_PROMPT // BLOCK + b * nqb + qb, kh)),
        out_shape=jax.ShapeDtypeStruct((T_ALL, ATT_HEADS * ATT_HD), BF16),
        input_output_aliases={10: 0},
        scratch_shapes=[pltpu.VMEM((DEC_SEQ, ATT_HD), BF16), pltpu.VMEM((DEC_SEQ, 2 * ATT_HD), BF16),
                        pltpu.VMEM((PAST_LEN, ATT_HD), BF16), pltpu.VMEM((PAST_LEN, 2 * ATT_HD), BF16)],
        compiler_params=_cparams("arbitrary", "arbitrary", "arbitrary"),
        name="attention_latent",
    )(sink, proj, proj, proj, ck, cv, cos_a, sin_a, q_norm.reshape(1, ATT_HD), k_norm.reshape(1, ATT_HD), fill)


def _att_prompt_kernel(sink_ref, q_ref, k_ref, v_ref, qn_ref, kn_ref, o_ref, nk_ref, nv_ref):
    kh = pl.program_id(1)
    kn = _head_norm(k_ref[...], kn_ref[...])
    v = v_ref[...]
    nk_ref[...] = kn
    nv_ref[...] = v.astype(F32)
    qg = qn_ref[...] * (ATT_HD ** -0.5)
    q = jnp.concatenate([_head_norm(q_ref[:, g * ATT_HD:(g + 1) * ATT_HD], qg).astype(BF16)
                         for g in range(ATT_GROUP)], axis=0)
    s = _bdot_nt(q, kn)
    sink = _sink_column(sink_ref, kh * ATT_GROUP, ATT_GROUP, SEQ)
    m = jnp.maximum(jnp.max(s, axis=-1, keepdims=True), sink)
    p = jnp.exp(s - m)
    den = jnp.sum(p, axis=-1, keepdims=True) + jnp.exp(sink - m)
    o = _bdot(p, v) / den
    for g in range(ATT_GROUP):
        o_ref[:, g * ATT_HD:(g + 1) * ATT_HD] = o[g * SEQ:(g + 1) * SEQ, :].astype(o_ref.dtype)


def _attention_prompt(proj, sink, q_norm, k_norm):
    qcol0 = 4 * RET_HEADS * RET_DK // (ATT_GROUP * ATT_HD)
    kcol0 = (4 * RET_HEADS * RET_DK + ATT_HEADS * ATT_HD) // ATT_HD
    vcol0 = kcol0 + ATT_KV_HEADS
    kv_spec = pl.BlockSpec((None, SEQ, ATT_HD), lambda b, kh: (b, 0, kh))
    kv_shape = jax.ShapeDtypeStruct((BATCH, SEQ, ATT_KV_HEADS * ATT_HD), F32)
    return pl.pallas_call(
        _att_prompt_kernel,
        grid=(BATCH, ATT_KV_HEADS),
        in_specs=[pl.BlockSpec(memory_space=pltpu.SMEM),
                  pl.BlockSpec((SEQ, ATT_GROUP * ATT_HD), lambda b, kh: (b, qcol0 + kh)),
                  pl.BlockSpec((SEQ, ATT_HD), lambda b, kh: (b, kcol0 + kh)),
                  pl.BlockSpec((SEQ, ATT_HD), lambda b, kh: (b, vcol0 + kh)),
                  pl.BlockSpec((1, ATT_HD), lambda b, kh: (0, 0)),
                  pl.BlockSpec((1, ATT_HD), lambda b, kh: (0, 0))],
        out_specs=[pl.BlockSpec((SEQ, ATT_GROUP * ATT_HD), lambda b, kh: (b, kh)), kv_spec, kv_spec],
        out_shape=[jax.ShapeDtypeStruct((T_ALL, ATT_HEADS * ATT_HD), BF16), kv_shape, kv_shape],
        compiler_params=_cparams("arbitrary", "arbitrary"),
        name="attention_prompt",
    )(sink, proj, proj, proj, q_norm.reshape(1, ATT_HD), k_norm.reshape(1, ATT_HD))


def _split_dot(m01, a):
    hi = a.astype(BF16)
    r1 = a - hi.astype(F32)
    mid = r1.astype(BF16)
    lo = (r1 - mid.astype(F32)).astype(BF16)
    return (jnp.dot(m01, hi, preferred_element_type=F32) + jnp.dot(m01, mid, preferred_element_type=F32)
            + jnp.dot(m01, lo, preferred_element_type=F32))


PREP_CHUNKS = ROW_TILE // BLOCK


def _ssd_prep_kernel(raw_ref, bias_ref, alog_ref, cum_ref, dt_ref, w_ref, tot_ref, ecum_ref):
    ii = lax.broadcasted_iota(jnp.int32, (BLOCK, BLOCK), 0)
    jj = lax.broadcasted_iota(jnp.int32, (BLOCK, BLOCK), 1)
    lower = jnp.where(jj <= ii, 1.0, 0.0).astype(BF16)
    upper = jnp.where(jj >= ii, 1.0, 0.0).astype(BF16)
    fwd_lane = lax.broadcasted_iota(jnp.int32, (BLOCK, LANES), 1) < SSD_HEADS
    neg_a = -jnp.exp(alog_ref[...])

    def chunk(k, carry):
        x = raw_ref[_rows(k), :] + bias_ref[...]
        dt = jnp.maximum(x, 0.0) + jnp.log1p(jnp.exp(-jnp.abs(x)))
        a = dt * neg_a
        incl = _split_dot(lower, a)
        rincl = _split_dot(upper, a)
        cum = jnp.where(fwd_lane, incl, rincl)
        tot = jnp.where(fwd_lane[:1], incl[BLOCK - 1:BLOCK, :], rincl[0:1, :])
        cum_ref[k] = cum.T
        dt_ref[k] = dt.T
        w_ref[k] = (dt * jnp.exp(tot - cum)).T
        tot_ref[k] = jnp.broadcast_to(jnp.exp(tot), (BLOCK, LANES)).T
        ecum_ref[k] = jnp.exp(cum).T
        return carry

    lax.fori_loop(0, PREP_CHUNKS, chunk, 0)


def _ssd_prep(dt_raw, dt_bias, a_log):
    nc = T_ALL // BLOCK
    spec = pl.BlockSpec((PREP_CHUNKS, 2 * SSD_HEADS, BLOCK), lambda c: (c, 0, 0))
    shape = jax.ShapeDtypeStruct((nc, 2 * SSD_HEADS, BLOCK), F32)
    return pl.pallas_call(
        _ssd_prep_kernel,
        grid=(nc // PREP_CHUNKS,),
        in_specs=[pl.BlockSpec((ROW_TILE, 2 * SSD_HEADS), lambda c: (c, 0)),
                  pl.BlockSpec((1, 2 * SSD_HEADS), lambda c: (0, 0)),
                  pl.BlockSpec((1, 2 * SSD_HEADS), lambda c: (0, 0))],
        out_specs=[spec] * 5,
        out_shape=[shape] * 5,
        compiler_params=_cparams("arbitrary"),
        name="ssd_prep",
    )(dt_raw, dt_bias.reshape(1, 2 * SSD_HEADS), a_log.reshape(1, 2 * SSD_HEADS))


GW = SSD_R * SSD_P
HALO = 8


def _pair_tiles(per_head):
    low = lax.broadcasted_iota(jnp.int32, per_head[0].shape, 1) < SSD_P
    return jnp.concatenate([jnp.where(low, per_head[2 * t], per_head[2 * t + 1]) for t in range(SSD_R // 2)],
                           axis=1)


def _row_bcast(ref, c, r):
    return jnp.broadcast_to(ref[c, r:r + 1, :], (BLOCK, BLOCK))


def _pair_cols(ref, c):
    top = lax.broadcasted_iota(jnp.int32, (BLOCK, BLOCK), 0) < SSD_P
    return jnp.concatenate(
        [jnp.where(top, _row_bcast(ref, c, 2 * t), _row_bcast(ref, c, 2 * t + 1)).T for t in range(SSD_R // 2)],
        axis=1)


def _ssd_kernel(n_chunks, has_s0, emit_state, has_fill, *refs):
    it = iter(refs)
    z_ref, x_ref, b_ref, c_ref = (next(it) for _ in range(4))
    cumf_ref, cumb_ref, dtf_ref, dtb_ref, wf_ref, wb_ref, totf_ref, totb_ref, ecf_ref, ecb_ref = (
        next(it) for _ in range(10))
    cwx_ref, cwb_ref, cwc_ref, cbx_ref, cbb_ref, cbc_ref, d_ref, ng_ref = (next(it) for _ in range(8))
    s0_ref = next(it) if has_s0 else None
    if has_fill:
        next(it), next(it)
    o_ref, ss_ref = next(it), next(it)
    sfin_ref = next(it) if emit_state else None
    pad_ref, xc_ref, bc_ref, cc_ref, sfs_ref, sf_ref, sb_ref = (next(it) for _ in range(7))
    seq_len = n_chunks * BLOCK

    pad_ref[0:HALO, :] = jnp.zeros((HALO, GW + 2 * SSD_N), F32)
    pad_ref[HALO + seq_len:2 * HALO + seq_len, :] = jnp.zeros((HALO, GW + 2 * SSD_N), F32)

    def fill(c, carry):
        dst = pl.ds(pl.multiple_of(c * BLOCK, BLOCK) + HALO, BLOCK)
        r = _rows(c)
        pad_ref[dst, 0:GW] = x_ref[r, :].astype(F32)
        pad_ref[dst, GW:GW + SSD_N] = b_ref[r, :].astype(F32)
        pad_ref[dst, GW + SSD_N:GW + 2 * SSD_N] = c_ref[r, :].astype(F32)
        return carry

    lax.fori_loop(0, n_chunks, fill, 0)

    def conv(c):
        r = _rows(c)
        src = pl.ds(pl.multiple_of(c * BLOCK, BLOCK), BLOCK + 2 * HALO)
        for col0, width, cw_ref, cb_ref, dst in ((0, GW, cwx_ref, cbx_ref, xc_ref),
                                                 (GW, SSD_N, cwb_ref, cbb_ref, bc_ref),
                                                 (GW + SSD_N, SSD_N, cwc_ref, cbc_ref, cc_ref)):
            for t in range(width // LANES):
                tl = slice(t * LANES, (t + 1) * LANES)
                win = pad_ref[src, col0 + t * LANES:col0 + (t + 1) * LANES]
                acc = jnp.broadcast_to(cb_ref[:, tl], (BLOCK, LANES))
                for w in range(CONV_W):
                    off = HALO - CONV_W // 2 + w
                    acc = acc + win[off:off + BLOCK, :] * cw_ref[w:w + 1, tl]
                dst[r, tl] = _silu(acc).astype(dst.dtype)

    if has_s0:
        for r in range(SSD_R):
            sf_ref[:, r * SSD_P:(r + 1) * SSD_P] = s0_ref[0, r]
            sb_ref[:, r * SSD_P:(r + 1) * SSD_P] = s0_ref[1, r]
    else:
        sf_ref[...] = jnp.zeros_like(sf_ref)
        sb_ref[...] = jnp.zeros_like(sb_ref)

    def state_update(s_ref, c, w_ref, tot_ref, bm, xs):
        tot = _pair_tiles([tot_ref[c, r:r + 1, :] for r in range(SSD_R)])
        s_ref[...] = s_ref[...] * tot + _bdot_tn(bm, xs * _pair_cols(w_ref, c))

    def fwd_step(c):
        r = _rows(c)
        sfs_ref[c] = sf_ref[...].astype(BF16)
        state_update(sf_ref, c, wf_ref, totf_ref, bc_ref[r, :], xc_ref[r, :])

    def fwd(c, carry):
        fwd_step(c)
        conv(c + 1)
        return carry

    conv(0)
    lax.fori_loop(0, n_chunks - 1, fwd, 0, unroll=3 if (n_chunks - 1) % 3 == 0 else 1)
    fwd_step(n_chunks - 1)
    if emit_state:
        for r in range(SSD_R):
            sfin_ref[0, r] = sf_ref[:, r * SSD_P:(r + 1) * SSD_P]

    ii = lax.broadcasted_iota(jnp.int32, (BLOCK, BLOCK), 0)
    jj = lax.broadcasted_iota(jnp.int32, (BLOCK, BLOCK), 1)
    causal = jj <= ii
    anti = ii <= jj
    low = lax.broadcasted_iota(jnp.int32, (BLOCK, LANES), 1) < SSD_P

    def bwd(t, carry):
        c = n_chunks - 1 - t
        r = _rows(c)
        cm, bm, xs = cc_ref[r, :], bc_ref[r, :], xc_ref[r, :]
        xb = xs.astype(BF16)
        sc = _bdot_nt(cm, bm)
        yf = _bdot(cm, sfs_ref[c])
        yb = _bdot(cm, sb_ref[...])
        tiles = []
        for t2 in range(SSD_R // 2):
            xt = xb[:, t2 * LANES:(t2 + 1) * LANES]
            acc = None
            for hh in range(2):
                hr = 2 * t2 + hh
                rf, rb = _row_bcast(cumf_ref, c, hr), _row_bcast(cumb_ref, c, hr)
                e = jnp.exp(jnp.where(anti, rf, rb).T - jnp.where(causal, rf, rb))
                e = e * jnp.where(causal, _row_bcast(dtf_ref, c, hr), _row_bcast(dtb_ref, c, hr))
                xh = jnp.where(low, xt, 0.0) if hh == 0 else jnp.where(low, 0.0, xt)
                part = _bdot(sc * e, xh)
                acc = part if acc is None else acc + part
            tiles.append(acc)
        y = (jnp.concatenate(tiles, axis=1) + _pair_cols(ecf_ref, c) * yf + _pair_cols(ecb_ref, c) * yb
             + d_ref[...] * xs)
        yg = y * _silu(z_ref[r, :].astype(F32))
        ss_ref[r, :] = jnp.broadcast_to(jnp.sum(yg * yg, axis=-1, keepdims=True), (BLOCK, LANES))
        o_ref[r, :] = (yg * ng_ref[...]).astype(o_ref.dtype)
        state_update(sb_ref, c, wb_ref, totb_ref, bm, xs)
        return carry

    lax.fori_loop(0, n_chunks, bwd, 0, unroll=min(4, n_chunks))
    if emit_state:
        for r in range(SSD_R):
            sfin_ref[1, r] = sb_ref[:, r * SSD_P:(r + 1) * SSD_P]


def _ssd_scan(zx, prep, conv_w, conv_b, d_exp, norm_gain, seq_len, n_seq, row_block0, s0=None, emit_state=False,
              fill=None):
    n_chunks = seq_len // BLOCK
    has_s0 = s0 is not None
    xcol0 = D_INNER // GW
    bcol0 = 2 * D_INNER // SSD_N
    ccol0 = bcol0 + SSD_GROUPS
    cwb0 = D_INNER // SSD_N

    def rowcol(width, col0):
        return pl.BlockSpec((seq_len, width), lambda b, g, col0=col0: (b + row_block0, col0 + g))

    def headrows(direction):
        return pl.BlockSpec((n_chunks, SSD_R, BLOCK),
                            lambda b, g, direction=direction: (b + row_block0, direction * SSD_GROUPS + g, 0))

    in_specs = [rowcol(GW, 0), rowcol(GW, xcol0), rowcol(SSD_N, bcol0), rowcol(SSD_N, ccol0)]
    args = [zx, zx, zx, zx]
    for arr in prep:
        in_specs += [headrows(0), headrows(1)]
        args += [arr, arr]
    in_specs += [pl.BlockSpec((CONV_W, GW), lambda b, g: (0, g)),
                 pl.BlockSpec((CONV_W, SSD_N), lambda b, g: (0, cwb0 + g)),
                 pl.BlockSpec((CONV_W, SSD_N), lambda b, g: (0, cwb0 + SSD_GROUPS + g)),
                 pl.BlockSpec((1, GW), lambda b, g: (0, g)),
                 pl.BlockSpec((1, SSD_N), lambda b, g: (0, cwb0 + g)),
                 pl.BlockSpec((1, SSD_N), lambda b, g: (0, cwb0 + SSD_GROUPS + g)),
                 pl.BlockSpec((1, GW), lambda b, g: (0, g)),
                 pl.BlockSpec((1, GW), lambda b, g: (0, g))]
    args += [conv_w, conv_w, conv_w, conv_b, conv_b, conv_b, d_exp, norm_gain.reshape(1, D_INNER)]
    state_spec = pl.BlockSpec((None, None, 2, SSD_R, SSD_N, SSD_P), lambda b, g: (b, 0, 0, g, 0, 0))
    if has_s0:
        in_specs.append(state_spec)
        args.append(s0)
    aliases = _fill_alias(fill, in_specs, args)
    out_specs = [pl.BlockSpec((seq_len, GW), lambda b, g: (b + row_block0, g)),
                 pl.BlockSpec((seq_len, LANES), lambda b, g: (b + row_block0, g))]
    out_shape = [jax.ShapeDtypeStruct((T_ALL, D_INNER), BF16),
                 jax.ShapeDtypeStruct((T_ALL, SSD_GROUPS * LANES), F32)]
    if emit_state:
        out_specs.append(state_spec)
        out_shape.append(jax.ShapeDtypeStruct((n_seq, 1, 2, SSD_HEADS, SSD_N, SSD_P), F32))
    return pl.pallas_call(
        functools.partial(_ssd_kernel, n_chunks, has_s0, emit_state, bool(fill)),
        grid=(n_seq, SSD_GROUPS),
        in_specs=in_specs,
        out_specs=out_specs,
        out_shape=out_shape,
        input_output_aliases=aliases,
        scratch_shapes=[pltpu.VMEM((seq_len + 2 * HALO, GW + 2 * SSD_N), F32),
                        pltpu.VMEM((seq_len, GW), F32), pltpu.VMEM((seq_len, SSD_N), BF16),
                        pltpu.VMEM((seq_len, SSD_N), BF16), pltpu.VMEM((n_chunks, SSD_N, GW), BF16),
                        pltpu.VMEM((SSD_N, GW), F32), pltpu.VMEM((SSD_N, GW), F32)],
        compiler_params=_cparams("arbitrary", "arbitrary"),
        name="ssd_scan_latent" if has_s0 else "ssd_scan_prompt",
    )(*args)


def _router_kernel(x_ref, g_ref, mod_ref, rw_ref, hn_ref, idx_ref, wgt_ref, hf_ref):
    rw = rw_ref[...]
    r_hi = rw.astype(BF16)
    r_lo = (rw - r_hi.astype(F32)).astype(BF16)
    r_both = jnp.concatenate([r_hi, r_lo], axis=1)

    def route(r):
        hf = hf_ref[r, :]
        h_hi = hf.astype(BF16)
        h_lo = (hf - h_hi.astype(F32)).astype(BF16)
        hn_ref[r, :] = h_hi
        t1 = jnp.dot(h_hi, r_both, preferred_element_type=F32)
        logits = t1[:, :LANES] + t1[:, LANES:] + jnp.dot(h_lo, r_hi, preferred_element_type=F32)
        lane = lax.broadcasted_iota(jnp.int32, logits.shape, 1)
        lg = jnp.where(lane < N_EXPERTS, logits, NEG_INF)
        m1 = jnp.max(lg, axis=-1, keepdims=True)
        i1 = jnp.min(jnp.where(lg == m1, lane, LANES), axis=-1, keepdims=True)
        lg2 = jnp.where(lane == i1, NEG_INF, lg)
        m2 = jnp.max(lg2, axis=-1, keepdims=True)
        i2 = jnp.min(jnp.where(lg2 == m2, lane, LANES), axis=-1, keepdims=True)
        e2 = jnp.exp(m2 - m1)
        w1 = 1.0 / (1.0 + e2)
        idx_ref[r, :] = jnp.where(lane == 0, i1, jnp.where(lane == 1, i2, 0))
        wgt_ref[r, :] = jnp.where(lane == 0, w1, jnp.where(lane == 1, e2 * w1, 0.0))

    _adaln_then(x_ref, g_ref, mod_ref, 3, 4, hf_ref, route)


def _router(x, gain, modt, router_w):
    t, d = x.shape
    rw = jnp.pad(router_w, ((0, 0), (0, LANES - N_EXPERTS)))
    return pl.pallas_call(
        _router_kernel,
        grid=(t // ROW_TILE,),
        in_specs=[pl.BlockSpec((ROW_TILE, d), lambda i: (i, 0)),
                  pl.BlockSpec((1, d), lambda i: (0, 0)),
                  pl.BlockSpec((None, 8, d), lambda i: (i, 0, 0)),
                  pl.BlockSpec((d, LANES), lambda i: (0, 0))],
        out_specs=[pl.BlockSpec((ROW_TILE, d), lambda i: (i, 0)),
                   pl.BlockSpec((ROW_TILE, LANES), lambda i: (i, 0)),
                   pl.BlockSpec((ROW_TILE, LANES), lambda i: (i, 0))],
        out_shape=[jax.ShapeDtypeStruct((t, d), BF16), jax.ShapeDtypeStruct((t, LANES), jnp.int32),
                   jax.ShapeDtypeStruct((t, LANES), F32)],
        scratch_shapes=[pltpu.VMEM((ROW_TILE, d), F32)],
        compiler_params=_cparams("arbitrary"),
        name="moe_router",
    )(x, gain.reshape(1, d), modt, rw)


DOWN_ROWS = 512
DOWN_TN = 512


def _expert_changed(be_ref, blk, prev_blk, step):
    return jnp.logical_or(step == 0, be_ref[blk] != be_ref[prev_blk])


def _expert_up_kernel(be_ref, nu_ref, nv_ref, xs_ref, wg_ref, wu_ref, h_ref, wgb_ref, wub_ref):
    i = pl.program_id(1)
    used = nv_ref[i] > 0

    @pl.when(jnp.logical_and(used, _expert_changed(be_ref, i, jnp.maximum(i - 1, 0), i)))
    def _():
        wgb_ref[...] = wg_ref[...].astype(BF16)
        wub_ref[...] = wu_ref[...].astype(BF16)

    @pl.when(used)
    def _():
        xs = xs_ref[...]
        h_ref[...] = (_silu(_bdot(xs, wgb_ref[...])) * _bdot(xs, wub_ref[...])).astype(h_ref.dtype)

    @pl.when(jnp.logical_not(used))
    def _():
        h_ref[...] = jnp.zeros_like(h_ref)


def _expert_down_kernel(be_ref, nu_ref, nv_ref, h_ref, wd_ref, o_ref, wdb_ref):
    i = pl.program_id(1)
    per = MOE_ROWS // DOWN_ROWS
    blk = i // per
    used = nv_ref[blk] > (i % per) * DOWN_ROWS

    @pl.when(jnp.logical_and(nv_ref[blk] > 0, _expert_changed(be_ref, blk, jnp.maximum(i - 1, 0) // per, i)))
    def _():
        wdb_ref[...] = wd_ref[...].astype(BF16)

    @pl.when(used)
    def _():
        o_ref[...] = _bdot(h_ref[...], wdb_ref[...]).astype(o_ref.dtype)

    @pl.when(jnp.logical_not(used))
    def _():
        o_ref[...] = jnp.zeros_like(o_ref)


def _experts(xs_sorted, block_e, n_used, n_valid, wg, wu, wd):
    cap, d = xs_sorted.shape
    ff = wg.shape[2]
    per = MOE_ROWS // DOWN_ROWS

    def expert_of(blk, be, nu):
        return be[jnp.minimum(blk, jnp.maximum(nu[0] - 1, 0))]

    h = pl.pallas_call(
        _expert_up_kernel,
        grid_spec=pltpu.PrefetchScalarGridSpec(
            num_scalar_prefetch=3,
            grid=(ff // FFN_TF, cap // MOE_ROWS),
            in_specs=[pl.BlockSpec((MOE_ROWS, d), lambda f, i, be, nu, nv: (i, 0)),
                      pl.BlockSpec((None, d, FFN_TF), lambda f, i, be, nu, nv: (expert_of(i, be, nu), 0, f)),
                      pl.BlockSpec((None, d, FFN_TF), lambda f, i, be, nu, nv: (expert_of(i, be, nu), 0, f))],
            out_specs=pl.BlockSpec((MOE_ROWS, FFN_TF), lambda f, i, be, nu, nv: (i, f)),
            scratch_shapes=[pltpu.VMEM((d, FFN_TF), BF16), pltpu.VMEM((d, FFN_TF), BF16)],
        ),
        out_shape=jax.ShapeDtypeStruct((cap, ff), BF16),
        compiler_params=_cparams("arbitrary", "arbitrary"),
        name="moe_expert_up",
    )(block_e, n_used, n_valid, xs_sorted, wg, wu)
    return pl.pallas_call(
        _expert_down_kernel,
        grid_spec=pltpu.PrefetchScalarGridSpec(
            num_scalar_prefetch=3,
            grid=(d // DOWN_TN, cap // DOWN_ROWS),
            in_specs=[pl.BlockSpec((DOWN_ROWS, ff), lambda n, i, be, nu, nv: (i, 0)),
                      pl.BlockSpec((None, ff, DOWN_TN),
                                   lambda n, i, be, nu, nv: (expert_of(i // per, be, nu), 0, n))],
            out_specs=pl.BlockSpec((DOWN_ROWS, DOWN_TN), lambda n, i, be, nu, nv: (i, n)),
            scratch_shapes=[pltpu.VMEM((ff, DOWN_TN), BF16)],
        ),
        out_shape=jax.ShapeDtypeStruct((cap, d), BF16),
        compiler_params=_cparams("arbitrary", "arbitrary"),
        name="moe_expert_down",
    )(block_e, n_used, n_valid, h, wd)


def _combine_kernel(x_ref, g0_ref, g1_ref, w_ref, mod_ref, o_ref):
    w = w_ref[...]
    y = w[:, 0:1] * g0_ref[...].astype(F32) + w[:, 1:2] * g1_ref[...].astype(F32)
    o_ref[...] = x_ref[...] + mod_ref[5:6, :] * y


COMB_ROWS = 512


def _combine(x, g, wgt, modt, row0, n_rows):
    t, d = x.shape
    b0 = row0 // COMB_ROWS
    per = ROW_TILE // COMB_ROWS
    return pl.pallas_call(
        _combine_kernel,
        grid=(n_rows // COMB_ROWS,),
        in_specs=[pl.BlockSpec((COMB_ROWS, d), lambda i: (i + b0, 0)),
                  pl.BlockSpec((COMB_ROWS, d), lambda i: (i + b0, 0)),
                  pl.BlockSpec((COMB_ROWS, d), lambda i: (i + b0 + t // COMB_ROWS, 0)),
                  pl.BlockSpec((COMB_ROWS, LANES), lambda i: (i + b0, 0)),
                  pl.BlockSpec((None, 8, d), lambda i: ((i + b0) // per, 0, 0))],
        out_specs=pl.BlockSpec((COMB_ROWS, d), lambda i: (i, 0)),
        out_shape=jax.ShapeDtypeStruct((n_rows, d), F32),
        compiler_params=_cparams("arbitrary"),
        name="moe_combine",
    )(x, g, g, wgt, modt)


def _moe(x, gain, modt, router_w, wg, wu, wd):
    t, d = x.shape
    hn, idx, wgt = _router(x, gain, modt, router_w)
    top_idx = idx[:, :TOP_K]
    n_slots = t * TOP_K
    flat_e = top_idx.reshape(-1)
    onehot = (flat_e[:, None] == jnp.arange(N_EXPERTS, dtype=jnp.int32)[None, :]).astype(jnp.int32)
    incl = jnp.cumsum(onehot, axis=0)
    counts = incl[-1]
    rank = jnp.sum((incl - onehot) * onehot, axis=1)
    padded = (counts + MOE_ROWS - 1) // MOE_ROWS * MOE_ROWS
    pend = jnp.cumsum(padded)
    pstart = pend - padded
    dest = pstart[flat_e] + rank
    n_blocks = n_slots // MOE_ROWS + N_EXPERTS
    cap = n_blocks * MOE_ROWS
    row_tok = (jnp.arange(cap, dtype=jnp.int32) % t).at[dest].set(jnp.arange(n_slots, dtype=jnp.int32) // TOP_K)
    block_e = jnp.clip(jnp.searchsorted(pend, jnp.arange(n_blocks, dtype=jnp.int32) * MOE_ROWS, side='right'),
                       0, N_EXPERTS - 1).astype(jnp.int32)
    n_used = (pend[-1:] // MOE_ROWS).astype(jnp.int32)
    blk = jnp.arange(n_blocks, dtype=jnp.int32)
    n_valid = jnp.where(blk < n_used[0],
                        jnp.clip((pstart + counts)[block_e] - blk * MOE_ROWS, 0, MOE_ROWS), 0).astype(jnp.int32)
    out = _experts(hn[row_tok], block_e, n_used, n_valid, wg, wu, wd)
    g = out[dest.reshape(t, TOP_K).T.reshape(-1)]
    return _combine(x, g, wgt, modt, 0, T_PROMPT), _combine(x, g, wgt, modt, T_PROMPT, T_SAMPLE)


def _rope_tables(n_tokens, dim):
    n_rows = n_tokens // GRID_W
    row = jnp.repeat(jnp.arange(n_rows), GRID_W).astype(F32)
    col = jnp.tile(jnp.arange(GRID_W), n_rows).astype(F32)
    n_freq = dim // 4
    inv = ROPE_BASE ** (-jnp.arange(n_freq, dtype=F32) / n_freq)
    ang = jnp.concatenate([row[:, None] * inv, col[:, None] * inv], axis=-1)
    return jnp.cos(ang), jnp.sin(ang)


def kernel(x_prompt, x_sample, state_ret, cache_k, cache_v, state_ssd, c, c_ctx, ada_w, ada_b, norm_mix, norm_ffn, ev_w_in, ev_w_out, ret_decay_logit, ret_norm, att_q_norm, att_k_norm, att_sink, ffn_w_gate, ffn_w_up, ffn_w_down, ssd_w_in, ssd_conv_w, ssd_conv_b, ssd_a_log, ssd_dt_bias, ssd_d, ssd_norm, ssd_w_out, moe_router, moe_w_gate, moe_w_up, moe_w_down):
    d = D_MODEL
    x = (x_prompt.reshape(T_PROMPT, d), x_sample.reshape(T_SAMPLE, d))

    cvecs = jnp.concatenate([c_ctx[None, :], c, jnp.zeros((MOD_ROWS - 1 - DEC_BATCH, d), F32)], axis=0)
    mods = _modulation(cvecs, ada_w, ada_b).reshape(2, MOD_ROWS, 6, d)
    tiles_per_seq = DEC_SEQ // ROW_TILE
    tile_row = jnp.concatenate([jnp.zeros((T_PROMPT // ROW_TILE,), jnp.int32),
                                1 + jnp.arange(T_SAMPLE // ROW_TILE, dtype=jnp.int32) // tiles_per_seq])
    modt = jnp.pad(mods[:, tile_row], ((0, 0), (0, 0), (0, 2), (0, 0)))

    ev_w_in_b, ev_w_out_b = ev_w_in[0].astype(BF16), ev_w_out[0].astype(BF16)
    ffn_wg_b, ffn_wu_b, ffn_wd_b = ffn_w_gate[0].astype(BF16), ffn_w_up[0].astype(BF16), ffn_w_down[0].astype(BF16)
    ssd_w_in_b, ssd_w_out_b = ssd_w_in[0].astype(BF16), ssd_w_out[0].astype(BF16)

    proj, = _adaln_matmul(x, norm_mix[0], modt[0], 0, 1, ev_w_in_b, tn=512, out_dtype=BF16, name="even_in_proj")
    lg = jax.nn.log_sigmoid(ret_decay_logit[0].astype(F32))
    cos_r, sin_r = _rope_tables(DEC_SEQ, RET_DK)
    cos_a, sin_a = _rope_tables(DEC_SEQ, ATT_HD)
    cos_a2 = jnp.concatenate([cos_a, cos_a], axis=-1)
    sin_a2 = jnp.concatenate([-sin_a, sin_a], axis=-1)
    mix_ret, new_state_ret = _retention(proj, lg, ret_norm[0], SEQ, BATCH, 0, emit_state=True)
    mix_ret, = _retention(proj, lg, ret_norm[0], DEC_SEQ, DEC_BATCH, T_PROMPT // DEC_SEQ,
                          ropes=(cos_r, sin_r), s0=state_ret, fill=(mix_ret,))
    mix_att, new_k, new_v = _attention_prompt(proj, att_sink[0], att_q_norm[0], att_k_norm[0])
    mix_att = _attention_latent(proj, cache_k[:, 0], cache_v[:, 0], att_sink[0], cos_a2, sin_a2,
                                att_q_norm[0], att_k_norm[0], mix_att)
    x = _proj_residual([mix_ret, mix_att], ev_w_out_b, x, modt[0], 2, tn=1024, name="even_out_proj")
    h = _ffn_gateup(x, norm_ffn[0], modt[0], ffn_wg_b, ffn_wu_b)
    x = _proj_residual([h], ffn_wd_b, x, modt[0], 5, tn=512, name="ffn_down")

    zx, dt_raw = _adaln_matmul(x, norm_mix[1], modt[1], 0, 1, ssd_w_in_b, tn=1024, n_out=SSD_ZX,
                               tail=2 * SSD_HEADS, out_dtype=BF16, name="ssd_in_proj")
    prep = _ssd_prep(dt_raw, ssd_dt_bias[0], ssd_a_log[0])
    d_exp = jnp.repeat(ssd_d[0], SSD_P)[None, :]
    conv_b = ssd_conv_b[0][None, :]
    yg, yss, new_state_ssd = _ssd_scan(zx, prep, ssd_conv_w[0], conv_b, d_exp, ssd_norm[0], SEQ, BATCH, 0,
                                       emit_state=True)
    yg, yss = _ssd_scan(zx, prep, ssd_conv_w[0], conv_b, d_exp, ssd_norm[0], DEC_SEQ, DEC_BATCH,
                        T_PROMPT // DEC_SEQ, s0=state_ssd, fill=(yg, yss))
    x = _proj_residual([yg], ssd_w_out_b, x, modt[1], 2, tn=512, row_ss=yss, name="ssd_out_proj")
    y_p, y_s = _moe(x, norm_ffn[1], modt[1], moe_router[0], moe_w_gate[0], moe_w_up[0], moe_w_down[0])

    y_prompt = y_p.reshape(BATCH, SEQ, d)
    y_sample = y_s.reshape(DEC_BATCH, DEC_SEQ, d)
    new_cache_k = new_k.reshape(BATCH, 1, SEQ, ATT_KV_HEADS, ATT_HD)
    new_cache_v = new_v.reshape(BATCH, 1, SEQ, ATT_KV_HEADS, ATT_HD)
    return (y_prompt, y_sample, new_state_ret, new_cache_k, new_cache_v, new_state_ssd)
```

```python
import functools

import jax
import jax.numpy as jnp
from jax import lax
from jax.experimental import pallas as pl
from jax.experimental.pallas import tpu as pltpu

F32 = jnp.float32
BF16 = jnp.bfloat16

D_MODEL = 2048
BATCH = 16
SEQ = 256
DEC_BATCH = 8
DEC_SEQ = 2048
PAST_LEN = 512
GRID_W = 64
BLOCK = 128
WINDOW = 128
EPS = 1e-6
ROPE_BASE = 10000.0
RET_HEADS = 4
RET_DK = 256
RET_DV = 256
ATT_HEADS = 8
ATT_KV_HEADS = 2
ATT_HD = 128
ATT_GROUP = ATT_HEADS // ATT_KV_HEADS
EVEN_IN = 5632
D_INNER = 2 * D_MODEL
SSD_P = 64
SSD_HEADS = D_INNER // SSD_P
SSD_N = 128
SSD_GROUPS = 8
SSD_R = SSD_HEADS // SSD_GROUPS
CONV_W = 5
CONV_CH = D_INNER + 2 * SSD_GROUPS * SSD_N
SSD_ZX = D_INNER + CONV_CH
D_FF = 5632
N_EXPERTS = 8
TOP_K = 2

T_PROMPT = BATCH * SEQ
T_SAMPLE = DEC_BATCH * DEC_SEQ
T_ALL = T_PROMPT + T_SAMPLE

LANES = 128
ROW_TILE = 1024
VMEM_LIMIT = 56 * 1024 * 1024
N_ROW_TILES = T_ALL // ROW_TILE
MOE_ROWS = 1024
NEG_INF = float("-inf")


def _cparams(*sem):
    return pltpu.CompilerParams(dimension_semantics=sem, vmem_limit_bytes=VMEM_LIMIT)


def _silu(x):
    return x * jax.nn.sigmoid(x)


def _bdot(a, b):
    return jnp.dot(a.astype(BF16), b.astype(BF16), preferred_element_type=F32)


def _bdot_nt(a, b):
    return lax.dot_general(a.astype(BF16), b.astype(BF16), (((1,), (1,)), ((), ())),
                           preferred_element_type=F32)


def _bdot_tn(a, b):
    return lax.dot_general(a.astype(BF16), b.astype(BF16), (((0,), (0,)), ((), ())),
                           preferred_element_type=F32)


def _rows(c, n=BLOCK):
    return pl.ds(pl.multiple_of(c * n, n), n)


MOD_ROWS = 16
MOD_TN = 1024


def _mod_kernel(c_ref, w_ref, b_ref, o_ref):
    o_ref[...] = _bdot(_silu(c_ref[...]), w_ref[...]) + b_ref[...]


def _modulation(cvecs, ada_w, ada_b):
    depth, d, n = ada_w.shape
    return pl.pallas_call(
        _mod_kernel,
        grid=(depth, n // MOD_TN),
        in_specs=[pl.BlockSpec((MOD_ROWS, d), lambda l, j: (0, 0)),
                  pl.BlockSpec((None, d, MOD_TN), lambda l, j: (l, 0, j)),
                  pl.BlockSpec((None, 1, MOD_TN), lambda l, j: (l, 0, j))],
        out_specs=pl.BlockSpec((None, MOD_ROWS, MOD_TN), lambda l, j: (l, 0, j)),
        out_shape=jax.ShapeDtypeStruct((depth, MOD_ROWS, n), F32),
        compiler_params=_cparams("arbitrary", "arbitrary"),
        name="modulation",
    )(cvecs, ada_w, ada_b.reshape(depth, 1, n))


ADALN_CHUNK = 64
ADALN_DOT_ROWS = 256


def _adaln_then(x_ref, g_ref, mod_ref, shift_row, scale_row, hn_ref, consume):
    mult = g_ref[...] * (1.0 + mod_ref[scale_row:scale_row + 1, :])
    shift = mod_ref[shift_row:shift_row + 1, :]
    for r0 in range(0, x_ref.shape[0], ADALN_DOT_ROWS):
        for s0 in range(r0, r0 + ADALN_DOT_ROWS, ADALN_CHUNK):
            r = slice(s0, s0 + ADALN_CHUNK)
            x = x_ref[r, :]
            ms = jnp.mean(x * x, axis=-1, keepdims=True)
            hn_ref[r, :] = (x * lax.rsqrt(ms + EPS) * mult + shift).astype(hn_ref.dtype)
        consume(slice(r0, r0 + ADALN_DOT_ROWS))


PROMPT_TILES = T_PROMPT // ROW_TILE


def _x_rows_specs(x, cols, col_of_j):
    if not isinstance(x, tuple):
        return [pl.BlockSpec((ROW_TILE, cols), lambda i, j: (i, col_of_j(j)))], [x]
    return ([pl.BlockSpec((ROW_TILE, cols),
                          lambda i, j: (jnp.minimum(i, PROMPT_TILES - 1), jnp.where(i < PROMPT_TILES, col_of_j(j), 0))),
             pl.BlockSpec((ROW_TILE, cols),
                          lambda i, j: (jnp.maximum(i - PROMPT_TILES, 0), jnp.where(i < PROMPT_TILES, 0, col_of_j(j))))],
            list(x))


def _for_x_rows(x_refs, fn):
    if len(x_refs) == 1:
        fn(x_refs[0])
        return
    in_prompt = pl.program_id(0) < PROMPT_TILES
    pl.when(in_prompt)(lambda: fn(x_refs[0]))
    pl.when(jnp.logical_not(in_prompt))(lambda: fn(x_refs[1]))


def _adaln_mm_kernel(shift_row, scale_row, has_tail, n_x, *refs):
    x_refs, (g_ref, mod_ref, w_ref), refs = refs[:n_x], refs[n_x:n_x + 3], refs[n_x + 3:]
    if has_tail:
        wt_ref, o_ref, ot_ref, hn_ref = refs
    else:
        o_ref, hn_ref = refs

    def project(r):
        hn = hn_ref[r, :]
        o_ref[r, :] = _bdot(hn, w_ref[...]).astype(o_ref.dtype)
        if has_tail:
            ot_ref[r, :] = _bdot(hn, wt_ref[...])

    first = pl.program_id(1) == 0

    @pl.when(first)
    def _():
        _for_x_rows(x_refs, lambda x_ref: _adaln_then(x_ref, g_ref, mod_ref, shift_row, scale_row, hn_ref, project))

    @pl.when(jnp.logical_not(first))
    def _():
        o_ref[...] = _bdot(hn_ref[...], w_ref[...]).astype(o_ref.dtype)


def _adaln_matmul(x, gain, modt, shift_row, scale_row, w, tn, n_out=None, tail=0, out_dtype=F32, name="adaln_mm"):
    t, d = T_ALL, w.shape[0]
    n_out = w.shape[1] if n_out is None else n_out
    in_specs, args = _x_rows_specs(x, d, lambda j: 0)
    n_x = len(args)
    in_specs += [pl.BlockSpec((1, d), lambda i, j: (0, 0)),
                 pl.BlockSpec((None, 8, d), lambda i, j: (i, 0, 0)),
                 pl.BlockSpec((d, tn), lambda i, j: (0, j))]
    args += [gain.reshape(1, d), modt, w]
    out_specs = [pl.BlockSpec((ROW_TILE, tn), lambda i, j: (i, j))]
    out_shape = [jax.ShapeDtypeStruct((t, n_out), out_dtype)]
    if tail:
        in_specs.append(pl.BlockSpec((d, tail), lambda i, j: (0, n_out // tail)))
        args.append(w)
        out_specs.append(pl.BlockSpec((ROW_TILE, tail), lambda i, j: (i, 0)))
        out_shape.append(jax.ShapeDtypeStruct((t, tail), F32))
    return pl.pallas_call(
        functools.partial(_adaln_mm_kernel, shift_row, scale_row, bool(tail), n_x),
        grid=(t // ROW_TILE, n_out // tn),
        in_specs=in_specs,
        out_specs=out_specs,
        out_shape=out_shape,
        scratch_shapes=[pltpu.VMEM((ROW_TILE, d), BF16)],
        compiler_params=_cparams("arbitrary", "arbitrary"),
        name=name,
    )(*args)


FFN_TF = 512


def _gateup_kernel(x_ref, g_ref, mod_ref, wg_ref, wu_ref, h_ref, hn_ref):
    def gate_up(r):
        hn = hn_ref[r, :]
        h_ref[r, :] = (_silu(_bdot(hn, wg_ref[...])) * _bdot(hn, wu_ref[...])).astype(h_ref.dtype)

    first = pl.program_id(1) == 0

    @pl.when(first)
    def _():
        _adaln_then(x_ref, g_ref, mod_ref, 3, 4, hn_ref, gate_up)

    @pl.when(jnp.logical_not(first))
    def _():
        gate_up(slice(None))


def _ffn_gateup(x, gain, modt, wg, wu):
    t, d = x.shape
    ff = wg.shape[1]
    return pl.pallas_call(
        _gateup_kernel,
        grid=(t // ROW_TILE, ff // FFN_TF),
        in_specs=[pl.BlockSpec((ROW_TILE, d), lambda i, f: (i, 0)),
                  pl.BlockSpec((1, d), lambda i, f: (0, 0)),
                  pl.BlockSpec((None, 8, d), lambda i, f: (i, 0, 0)),
                  pl.BlockSpec((d, FFN_TF), lambda i, f: (0, f)),
                  pl.BlockSpec((d, FFN_TF), lambda i, f: (0, f))],
        out_specs=pl.BlockSpec((ROW_TILE, FFN_TF), lambda i, f: (i, f)),
        out_shape=jax.ShapeDtypeStruct((t, ff), BF16),
        scratch_shapes=[pltpu.VMEM((ROW_TILE, d), BF16)],
        compiler_params=_cparams("arbitrary", "arbitrary"),
        name="ffn_gateup",
    )(x, gain.reshape(1, d), modt, wg, wu)


def _proj_res_kernel(n_a, n_x, gate_row, norm, *refs):
    a_refs = refs[:n_a]
    w_refs = refs[n_a:2 * n_a]
    x_refs, mod_ref = refs[2 * n_a:2 * n_a + n_x], refs[2 * n_a + n_x]
    pos = 2 * n_a + n_x + 1
    o_ref = refs[pos + 1] if norm else refs[pos]
    acc = _bdot(a_refs[0][...], w_refs[0][...])
    for k in range(1, n_a):
        acc = acc + _bdot(a_refs[k][...], w_refs[k][...])
    if norm:
        ss_ref = refs[pos]
        k_total = sum(a.shape[1] for a in a_refs)
        ss = ss_ref[:, 0:LANES]
        for k in range(1, ss_ref.shape[1] // LANES):
            ss = ss + ss_ref[:, k * LANES:(k + 1) * LANES]
        rs = lax.rsqrt(ss * (1.0 / k_total) + EPS)
        acc = acc * jnp.concatenate([rs] * (acc.shape[1] // LANES), axis=1)
    upd = mod_ref[gate_row:gate_row + 1, :] * acc

    def finish(x_ref):
        o_ref[...] = x_ref[...] + upd

    _for_x_rows(x_refs, finish)


def _proj_residual(a_list, w, x, modt, gate_row, tn, row_ss=None, name="proj_res"):
    t, d = T_ALL, w.shape[1]
    n_a = len(a_list)
    norm = row_ss is not None
    in_specs, args, k0 = [], [], 0
    for a in a_list:
        in_specs.append(pl.BlockSpec((ROW_TILE, a.shape[1]), lambda i, j: (i, 0)))
        args.append(a)
    for a in a_list:
        ka = a.shape[1]
        assert k0 % ka == 0
        in_specs.append(pl.BlockSpec((ka, tn), lambda i, j, kb=k0 // ka: (kb, j)))
        args.append(w)
        k0 += ka
    x_specs, x_args = _x_rows_specs(x, tn, lambda j: j)
    in_specs += x_specs + [pl.BlockSpec((None, 8, tn), lambda i, j: (i, 0, j))]
    args += x_args + [modt]
    if norm:
        in_specs.append(pl.BlockSpec((ROW_TILE, row_ss.shape[1]), lambda i, j: (i, 0)))
        args.append(row_ss)
    return pl.pallas_call(
        functools.partial(_proj_res_kernel, n_a, len(x_args), gate_row, norm),
        grid=(t // ROW_TILE, d // tn),
        in_specs=in_specs,
        out_specs=pl.BlockSpec((ROW_TILE, tn), lambda i, j: (i, j)),
        out_shape=jax.ShapeDtypeStruct((t, d), F32),
        compiler_params=_cparams("arbitrary", "arbitrary"),
        name=name,
    )(*args)


def _ret_kernel(n_chunks, rope, has_s0, emit_state, has_fill, lg_ref, *refs):
    it = iter(refs)
    q_ref, k_ref, v_ref, gt_ref = next(it), next(it), next(it), next(it)
    cos_ref, sin_ref = (next(it), next(it)) if rope else (None, None)
    s0_ref = next(it) if has_s0 else None
    gain_ref = next(it)
    if has_fill:
        next(it)
    o_ref = next(it)
    sfin_ref = next(it) if emit_state else None
    qs_ref, ks_ref, sfs_ref, dm_ref, dec_ref, sf_ref, sb_ref = (next(it) for _ in range(7))

    h = pl.program_id(1)
    lgf = lg_ref[0, h]
    lgb = lg_ref[1, h]
    half = RET_DK // 2

    def prep(c, carry):
        r = _rows(c)
        q = q_ref[r, :].astype(F32)
        k = k_ref[r, :].astype(F32) * (RET_DK ** -0.5)
        if rope:
            cs, sn = cos_ref[r, :], sin_ref[r, :]
            for src, dst in ((q, qs_ref), (k, ks_ref)):
                x1, x2 = src[:, :half], src[:, half:]
                dst[r, :half] = (x1 * cs - x2 * sn).astype(BF16)
                dst[r, half:] = (x1 * sn + x2 * cs).astype(BF16)
        else:
            qs_ref[r, :] = q.astype(BF16)
            ks_ref[r, :] = k.astype(BF16)
        return carry

    lax.fori_loop(0, n_chunks, prep, 0)

    ii = lax.broadcasted_iota(jnp.int32, (BLOCK, BLOCK), 0)
    jj = lax.broadcasted_iota(jnp.int32, (BLOCK, BLOCK), 1)
    diff = (ii - jj).astype(F32)
    dm_ref[...] = jnp.exp(jnp.where(jj <= ii, diff * lgf, -diff * lgb))
    pos = lax.broadcasted_iota(jnp.int32, (BLOCK, RET_DV), 0).astype(F32)
    dec_ref[0] = jnp.exp((pos + 1.0) * lgf)
    dec_ref[1] = jnp.exp((BLOCK - pos) * lgb)
    dec_ref[2] = jnp.exp((BLOCK - 1.0 - pos) * lgf)
    dec_ref[3] = jnp.exp(pos * lgb)
    tot_f = jnp.exp(jnp.full((1, RET_DV), BLOCK * lgf, F32))
    tot_b = jnp.exp(jnp.full((1, RET_DV), BLOCK * lgb, F32))

    if has_s0:
        sf_ref[...] = s0_ref[0]
        sb_ref[...] = s0_ref[1]
    else:
        sf_ref[...] = jnp.zeros_like(sf_ref)
        sb_ref[...] = jnp.zeros_like(sb_ref)

    def fwd(c, carry):
        r = _rows(c)
        sfs_ref[c] = sf_ref[...].astype(BF16)
        kd = ks_ref[r, :].astype(F32) * dec_ref[2]
        sf_ref[...] = sf_ref[...] * tot_f + _bdot_tn(kd, v_ref[r, :])
        return carry

    lax.fori_loop(0, n_chunks, fwd, 0, unroll=min(4, n_chunks))
    if emit_state:
        sfin_ref[0] = sf_ref[...]

    def bwd(t, carry):
        c = n_chunks - 1 - t
        r = _rows(c)
        q = qs_ref[r, :]
        k = ks_ref[r, :]
        v = v_ref[r, :].astype(BF16)
        p = _bdot_nt(q, k) * dm_ref[...]
        o = _bdot(p, v)
        o = o + _bdot(q, sfs_ref[c]) * dec_ref[0]
        o = o + _bdot(q, sb_ref[...]) * dec_ref[1]
        ms = jnp.mean(o * o, axis=-1, keepdims=True)
        y = o * lax.rsqrt(ms + EPS) * gain_ref[...]
        o_ref[r, :] = (y * _silu(gt_ref[r, :].astype(F32))).astype(o_ref.dtype)
        kd = k.astype(F32) * dec_ref[3]
        sb_ref[...] = sb_ref[...] * tot_b + _bdot_tn(kd, v)
        return carry

    lax.fori_loop(0, n_chunks, bwd, 0, unroll=min(4, n_chunks))
    if emit_state:
        sfin_ref[1] = sb_ref[...]


def _fill_alias(fill, in_specs, args):
    aliases = {}
    for k, arr in enumerate(fill or ()):
        in_specs.append(pl.BlockSpec(memory_space=pl.ANY))
        args.append(arr)
        aliases[len(args) - 1] = k
    return aliases


def _retention(proj, lg, ret_norm, seq_len, n_seq, row_block0, ropes=None, s0=None, emit_state=False, fill=None):
    n_chunks = seq_len // BLOCK
    rope, has_s0 = ropes is not None, s0 is not None

    def col(cb):
        return pl.BlockSpec((seq_len, RET_DK), lambda b, h, cb=cb: (b + row_block0, cb * RET_HEADS + h))

    in_specs = [pl.BlockSpec(memory_space=pltpu.SMEM), col(0), col(1), col(2), col(3)]
    args = [lg, proj, proj, proj, proj]
    if rope:
        in_specs += [pl.BlockSpec((seq_len, RET_DK // 2), lambda b, h: (0, 0))] * 2
        args += list(ropes)
    if has_s0:
        in_specs.append(pl.BlockSpec((None, None, 2, None, RET_DK, RET_DV), lambda b, h: (b, 0, 0, h, 0, 0)))
        args.append(s0)
    in_specs.append(pl.BlockSpec((1, RET_DV), lambda b, h: (0, h)))
    args.append(ret_norm.reshape(1, RET_HEADS * RET_DV))
    aliases = _fill_alias(fill, in_specs, args)
    out_specs = [pl.BlockSpec((seq_len, RET_DV), lambda b, h: (b + row_block0, h))]
    out_shape = [jax.ShapeDtypeStruct((T_ALL, RET_HEADS * RET_DV), BF16)]
    if emit_state:
        out_specs.append(pl.BlockSpec((None, None, 2, None, RET_DK, RET_DV), lambda b, h: (b, 0, 0, h, 0, 0)))
        out_shape.append(jax.ShapeDtypeStruct((n_seq, 1, 2, RET_HEADS, RET_DK, RET_DV), F32))
    return pl.pallas_call(
        functools.partial(_ret_kernel, n_chunks, rope, has_s0, emit_state, bool(fill)),
        grid=(n_seq, RET_HEADS),
        in_specs=in_specs,
        out_specs=out_specs,
        out_shape=out_shape,
        input_output_aliases=aliases,
        scratch_shapes=[pltpu.VMEM((seq_len, RET_DK), BF16), pltpu.VMEM((seq_len, RET_DK), BF16),
                        pltpu.VMEM((n_chunks, RET_DK, RET_DV), BF16), pltpu.VMEM((BLOCK, BLOCK), F32),
                        pltpu.VMEM((4, BLOCK, RET_DV), F32), pltpu.VMEM((RET_DK, RET_DV), F32),
                        pltpu.VMEM((RET_DK, RET_DV), F32)],
        compiler_params=_cparams("arbitrary", "arbitrary"),
        name="retention_latent" if rope else "retention_prompt",
    )(*args)


def _head_norm(x, gain):
    x = x.astype(F32)
    return x * lax.rsqrt(jnp.mean(x * x, axis=-1, keepdims=True) + EPS) * gain


def _rope_full(x, cs, sn):
    return x * cs + pltpu.roll(x, ATT_HD // 2, 1) * sn


ATT_SPLIT = 2


def _sink_column(sink_ref, head0, n_heads, rows_per_head):
    n = n_heads * rows_per_head
    head = lax.broadcasted_iota(jnp.int32, (n, 1), 0) // rows_per_head
    col = jnp.full((n, 1), sink_ref[head0], F32)
    for g in range(1, n_heads):
        col = jnp.where(head == g, sink_ref[head0 + g], col)
    return col


def _att_latent_kernel(sink_ref, q_ref, k_ref, v_ref, ck_ref, cv_ref, cos_ref, sin_ref, qn_ref, kn_ref,
                       fill_ref, o_ref, kp_ref, vp_ref, ckp_ref, cvp_ref):
    del fill_ref
    kh, qb = pl.program_id(1), pl.program_id(2)
    n_chunks = DEC_SEQ // BLOCK
    loc = 3 * BLOCK

    @pl.when(qb == 0)
    def _():
        def prep(c, carry):
            r = _rows(c)
            kp_ref[r, :] = _rope_full(_head_norm(k_ref[r, :], kn_ref[...]), cos_ref[r, :], sin_ref[r, :]).astype(BF16)
            vp_ref[r, :ATT_HD] = v_ref[r, :].astype(BF16)
            vp_ref[r, ATT_HD:] = jnp.ones((BLOCK, ATT_HD), BF16)
            return carry

        lax.fori_loop(0, n_chunks, prep, 0)
        ckp_ref[...] = ck_ref[...].astype(BF16)
        cvp_ref[:, :ATT_HD] = cv_ref[...].astype(BF16)
        cvp_ref[:, ATT_HD:] = jnp.ones((PAST_LEN, ATT_HD), BF16)

    rq = _rows(qb)
    cs, sn = cos_ref[rq, :], sin_ref[rq, :]
    qg = qn_ref[...] * (ATT_HD ** -0.5)
    start = pl.multiple_of(jnp.clip((qb - 1) * BLOCK, 0, DEC_SEQ - loc), BLOCK)
    kl, vl = kp_ref[pl.ds(start, loc), :], vp_ref[pl.ds(start, loc), :]
    qpos = qb * BLOCK + (lax.broadcasted_iota(jnp.int32, (ATT_SPLIT * BLOCK, loc), 0) & (BLOCK - 1))
    kpos = start + lax.broadcasted_iota(jnp.int32, (ATT_SPLIT * BLOCK, loc), 1)
    in_window = jnp.abs(qpos - kpos) <= WINDOW
    for g0 in range(0, ATT_GROUP, ATT_SPLIT):
        q = jnp.concatenate(
            [_rope_full(_head_norm(q_ref[:, g * ATT_HD:(g + 1) * ATT_HD], qg), cs, sn).astype(BF16)
             for g in range(g0, g0 + ATT_SPLIT)], axis=0)
        s_loc = jnp.where(in_window, _bdot_nt(q, kl), NEG_INF)
        s_ctx = _bdot_nt(q, ckp_ref[...])
        sink = _sink_column(sink_ref, kh * ATT_GROUP + g0, ATT_SPLIT, BLOCK)
        m = jnp.maximum(jnp.maximum(jnp.max(s_loc, axis=-1, keepdims=True),
                                    jnp.max(s_ctx, axis=-1, keepdims=True)), sink)
        pv = _bdot(jnp.exp(s_loc - m), vl) + _bdot(jnp.exp(s_ctx - m), cvp_ref[...])
        o = pv[:, :ATT_HD] / (pv[:, ATT_HD:] + jnp.exp(sink - m))
        for g in range(ATT_SPLIT):
            o_ref[:, (g0 + g) * ATT_HD:(g0 + g + 1) * ATT_HD] = o[g * BLOCK:(g + 1) * BLOCK, :].astype(o_ref.dtype)


def _attention_latent(proj, cache_k, cache_v, sink, cos_a, sin_a, q_norm, k_norm, fill):
    nqb = DEC_SEQ // BLOCK
    rb0 = T_PROMPT // DEC_SEQ
    qcol0 = 4 * RET_HEADS * RET_DK // (ATT_GROUP * ATT_HD)
    kcol0 = (4 * RET_HEADS * RET_DK + ATT_HEADS * ATT_HD) // ATT_HD
    vcol0 = kcol0 + ATT_KV_HEADS
    ck = cache_k.reshape(DEC_BATCH, PAST_LEN, ATT_KV_HEADS * ATT_HD)
    cv = cache_v.reshape(DEC_BATCH, PAST_LEN, ATT_KV_HEADS * ATT_HD)
    return pl.pallas_call(
        _att_latent_kernel,
        grid=(DEC_BATCH, ATT_KV_HEADS, nqb),
        in_specs=[pl.BlockSpec(memory_space=pltpu.SMEM),
                  pl.BlockSpec((BLOCK, ATT_GROUP * ATT_HD),
                               lambda b, kh, qb: (T_PROMPT // BLOCK + b * nqb + qb, qcol0 + kh)),
                  pl.BlockSpec((DEC_SEQ, ATT_HD), lambda b, kh, qb: (rb0 + b, kcol0 + kh)),
                  pl.BlockSpec((DEC_SEQ, ATT_HD), lambda b, kh, qb: (rb0 + b, vcol0 + kh)),
                  pl.BlockSpec((None, PAST_LEN, ATT_HD), lambda b, kh, qb: (b, 0, kh)),
                  pl.BlockSpec((None, PAST_LEN, ATT_HD), lambda b, kh, qb: (b, 0, kh)),
                  pl.BlockSpec((DEC_SEQ, ATT_HD), lambda b, kh, qb: (0, 0)),
                  pl.BlockSpec((DEC_SEQ, ATT_HD), lambda b, kh, qb: (0, 0)),
                  pl.BlockSpec((1, ATT_HD), lambda b, kh, qb: (0, 0)),
                  pl.BlockSpec((1, ATT_HD), lambda b, kh, qb: (0, 0)),
                  pl.BlockSpec(memory_space=pl.ANY)],
        out_specs=pl.BlockSpec((BLOCK, ATT_GROUP * ATT_HD),
                               lambda b, kh, qb: (T_PROMPT // BLOCK + b * nqb + qb, kh)),
        out_shape=jax.ShapeDtypeStruct((T_ALL, ATT_HEADS * ATT_HD), BF16),
        input_output_aliases={10: 0},
        scratch_shapes=[pltpu.VMEM((DEC_SEQ, ATT_HD), BF16), pltpu.VMEM((DEC_SEQ, 2 * ATT_HD), BF16),
                        pltpu.VMEM((PAST_LEN, ATT_HD), BF16), pltpu.VMEM((PAST_LEN, 2 * ATT_HD), BF16)],
        compiler_params=_cparams("arbitrary", "arbitrary", "arbitrary"),
        name="attention_latent",
    )(sink, proj, proj, proj, ck, cv, cos_a, sin_a, q_norm.reshape(1, ATT_HD), k_norm.reshape(1, ATT_HD), fill)


def _att_prompt_kernel(sink_ref, q_ref, k_ref, v_ref, qn_ref, kn_ref, o_ref, nk_ref, nv_ref):
    kh = pl.program_id(1)
    kn = _head_norm(k_ref[...], kn_ref[...])
    v = v_ref[...]
    nk_ref[...] = kn
    nv_ref[...] = v.astype(F32)
    qg = qn_ref[...] * (ATT_HD ** -0.5)
    q = jnp.concatenate([_head_norm(q_ref[:, g * ATT_HD:(g + 1) * ATT_HD], qg).astype(BF16)
                         for g in range(ATT_GROUP)], axis=0)
    s = _bdot_nt(q, kn)
    sink = _sink_column(sink_ref, kh * ATT_GROUP, ATT_GROUP, SEQ)
    m = jnp.maximum(jnp.max(s, axis=-1, keepdims=True), sink)
    p = jnp.exp(s - m)
    den = jnp.sum(p, axis=-1, keepdims=True) + jnp.exp(sink - m)
    o = _bdot(p, v) / den
    for g in range(ATT_GROUP):
        o_ref[:, g * ATT_HD:(g + 1) * ATT_HD] = o[g * SEQ:(g + 1) * SEQ, :].astype(o_ref.dtype)


def _attention_prompt(proj, sink, q_norm, k_norm):
    qcol0 = 4 * RET_HEADS * RET_DK // (ATT_GROUP * ATT_HD)
    kcol0 = (4 * RET_HEADS * RET_DK + ATT_HEADS * ATT_HD) // ATT_HD
    vcol0 = kcol0 + ATT_KV_HEADS
    kv_spec = pl.BlockSpec((None, SEQ, ATT_HD), lambda b, kh: (b, 0, kh))
    kv_shape = jax.ShapeDtypeStruct((BATCH, SEQ, ATT_KV_HEADS * ATT_HD), F32)
    return pl.pallas_call(
        _att_prompt_kernel,
        grid=(BATCH, ATT_KV_HEADS),
        in_specs=[pl.BlockSpec(memory_space=pltpu.SMEM),
                  pl.BlockSpec((SEQ, ATT_GROUP * ATT_HD), lambda b, kh: (b, qcol0 + kh)),
                  pl.BlockSpec((SEQ, ATT_HD), lambda b, kh: (b, kcol0 + kh)),
                  pl.BlockSpec((SEQ, ATT_HD), lambda b, kh: (b, vcol0 + kh)),
                  pl.BlockSpec((1, ATT_HD), lambda b, kh: (0, 0)),
                  pl.BlockSpec((1, ATT_HD), lambda b, kh: (0, 0))],
        out_specs=[pl.BlockSpec((SEQ, ATT_GROUP * ATT_HD), lambda b, kh: (b, kh)), kv_spec, kv_spec],
        out_shape=[jax.ShapeDtypeStruct((T_ALL, ATT_HEADS * ATT_HD), BF16), kv_shape, kv_shape],
        compiler_params=_cparams("arbitrary", "arbitrary"),
        name="attention_prompt",
    )(sink, proj, proj, proj, q_norm.reshape(1, ATT_HD), k_norm.reshape(1, ATT_HD))


def _split_dot(m01, a):
    hi = a.astype(BF16)
    r1 = a - hi.astype(F32)
    mid = r1.astype(BF16)
    lo = (r1 - mid.astype(F32)).astype(BF16)
    return (jnp.dot(m01, hi, preferred_element_type=F32) + jnp.dot(m01, mid, preferred_element_type=F32)
            + jnp.dot(m01, lo, preferred_element_type=F32))


PREP_CHUNKS = ROW_TILE // BLOCK


def _ssd_prep_kernel(raw_ref, bias_ref, alog_ref, cum_ref, dt_ref, w_ref, tot_ref, ecum_ref):
    ii = lax.broadcasted_iota(jnp.int32, (BLOCK, BLOCK), 0)
    jj = lax.broadcasted_iota(jnp.int32, (BLOCK, BLOCK), 1)
    lower = jnp.where(jj <= ii, 1.0, 0.0).astype(BF16)
    upper = jnp.where(jj >= ii, 1.0, 0.0).astype(BF16)
    fwd_lane = lax.broadcasted_iota(jnp.int32, (BLOCK, LANES), 1) < SSD_HEADS
    neg_a = -jnp.exp(alog_ref[...])

    def chunk(k, carry):
        x = raw_ref[_rows(k), :] + bias_ref[...]
        dt = jnp.maximum(x, 0.0) + jnp.log1p(jnp.exp(-jnp.abs(x)))
        a = dt * neg_a
        incl = _split_dot(lower, a)
        rincl = _split_dot(upper, a)
        cum = jnp.where(fwd_lane, incl, rincl)
        tot = jnp.where(fwd_lane[:1], incl[BLOCK - 1:BLOCK, :], rincl[0:1, :])
        cum_ref[k] = cum.T
        dt_ref[k] = dt.T
        w_ref[k] = (dt * jnp.exp(tot - cum)).T
        tot_ref[k] = jnp.broadcast_to(jnp.exp(tot), (BLOCK, LANES)).T
        ecum_ref[k] = jnp.exp(cum).T
        return carry

    lax.fori_loop(0, PREP_CHUNKS, chunk, 0)


def _ssd_prep(dt_raw, dt_bias, a_log):
    nc = T_ALL // BLOCK
    spec = pl.BlockSpec((PREP_CHUNKS, 2 * SSD_HEADS, BLOCK), lambda c: (c, 0, 0))
    shape = jax.ShapeDtypeStruct((nc, 2 * SSD_HEADS, BLOCK), F32)
    return pl.pallas_call(
        _ssd_prep_kernel,
        grid=(nc // PREP_CHUNKS,),
        in_specs=[pl.BlockSpec((ROW_TILE, 2 * SSD_HEADS), lambda c: (c, 0)),
                  pl.BlockSpec((1, 2 * SSD_HEADS), lambda c: (0, 0)),
                  pl.BlockSpec((1, 2 * SSD_HEADS), lambda c: (0, 0))],
        out_specs=[spec] * 5,
        out_shape=[shape] * 5,
        compiler_params=_cparams("arbitrary"),
        name="ssd_prep",
    )(dt_raw, dt_bias.reshape(1, 2 * SSD_HEADS), a_log.reshape(1, 2 * SSD_HEADS))


GW = SSD_R * SSD_P
HALO = 8


def _pair_tiles(per_head):
    low = lax.broadcasted_iota(jnp.int32, per_head[0].shape, 1) < SSD_P
    return jnp.concatenate([jnp.where(low, per_head[2 * t], per_head[2 * t + 1]) for t in range(SSD_R // 2)],
                           axis=1)


def _row_bcast(ref, c, r):
    return jnp.broadcast_to(ref[c, r:r + 1, :], (BLOCK, BLOCK))


def _pair_cols(ref, c):
    top = lax.broadcasted_iota(jnp.int32, (BLOCK, BLOCK), 0) < SSD_P
    return jnp.concatenate(
        [jnp.where(top, _row_bcast(ref, c, 2 * t), _row_bcast(ref, c, 2 * t + 1)).T for t in range(SSD_R // 2)],
        axis=1)


def _ssd_kernel(n_chunks, has_s0, emit_state, has_fill, *refs):
    it = iter(refs)
    z_ref, x_ref, b_ref, c_ref = (next(it) for _ in range(4))
    cumf_ref, cumb_ref, dtf_ref, dtb_ref, wf_ref, wb_ref, totf_ref, totb_ref, ecf_ref, ecb_ref = (
        next(it) for _ in range(10))
    cwx_ref, cwb_ref, cwc_ref, cbx_ref, cbb_ref, cbc_ref, d_ref, ng_ref = (next(it) for _ in range(8))
    s0_ref = next(it) if has_s0 else None
    if has_fill:
        next(it), next(it)
    o_ref, ss_ref = next(it), next(it)
    sfin_ref = next(it) if emit_state else None
    pad_ref, xc_ref, bc_ref, cc_ref, sfs_ref, sf_ref, sb_ref = (next(it) for _ in range(7))
    seq_len = n_chunks * BLOCK

    pad_ref[0:HALO, :] = jnp.zeros((HALO, GW + 2 * SSD_N), F32)
    pad_ref[HALO + seq_len:2 * HALO + seq_len, :] = jnp.zeros((HALO, GW + 2 * SSD_N), F32)

    def fill(c, carry):
        dst = pl.ds(pl.multiple_of(c * BLOCK, BLOCK) + HALO, BLOCK)
        r = _rows(c)
        pad_ref[dst, 0:GW] = x_ref[r, :].astype(F32)
        pad_ref[dst, GW:GW + SSD_N] = b_ref[r, :].astype(F32)
        pad_ref[dst, GW + SSD_N:GW + 2 * SSD_N] = c_ref[r, :].astype(F32)
        return carry

    lax.fori_loop(0, n_chunks, fill, 0)

    def conv(c):
        r = _rows(c)
        src = pl.ds(pl.multiple_of(c * BLOCK, BLOCK), BLOCK + 2 * HALO)
        for col0, width, cw_ref, cb_ref, dst in ((0, GW, cwx_ref, cbx_ref, xc_ref),
                                                 (GW, SSD_N, cwb_ref, cbb_ref, bc_ref),
                                                 (GW + SSD_N, SSD_N, cwc_ref, cbc_ref, cc_ref)):
            for t in range(width // LANES):
                tl = slice(t * LANES, (t + 1) * LANES)
                win = pad_ref[src, col0 + t * LANES:col0 + (t + 1) * LANES]
                acc = jnp.broadcast_to(cb_ref[:, tl], (BLOCK, LANES))
                for w in range(CONV_W):
                    off = HALO - CONV_W // 2 + w
                    acc = acc + win[off:off + BLOCK, :] * cw_ref[w:w + 1, tl]
                dst[r, tl] = _silu(acc).astype(dst.dtype)

    if has_s0:
        for r in range(SSD_R):
            sf_ref[:, r * SSD_P:(r + 1) * SSD_P] = s0_ref[0, r]
            sb_ref[:, r * SSD_P:(r + 1) * SSD_P] = s0_ref[1, r]
    else:
        sf_ref[...] = jnp.zeros_like(sf_ref)
        sb_ref[...] = jnp.zeros_like(sb_ref)

    def state_update(s_ref, c, w_ref, tot_ref, bm, xs):
        tot = _pair_tiles([tot_ref[c, r:r + 1, :] for r in range(SSD_R)])
        s_ref[...] = s_ref[...] * tot + _bdot_tn(bm, xs * _pair_cols(w_ref, c))

    def fwd_step(c):
        r = _rows(c)
        sfs_ref[c] = sf_ref[...].astype(BF16)
        state_update(sf_ref, c, wf_ref, totf_ref, bc_ref[r, :], xc_ref[r, :])

    def fwd(c, carry):
        fwd_step(c)
        conv(c + 1)
        return carry

    conv(0)
    lax.fori_loop(0, n_chunks - 1, fwd, 0, unroll=3 if (n_chunks - 1) % 3 == 0 else 1)
    fwd_step(n_chunks - 1)
    if emit_state:
        for r in range(SSD_R):
            sfin_ref[0, r] = sf_ref[:, r * SSD_P:(r + 1) * SSD_P]

    ii = lax.broadcasted_iota(jnp.int32, (BLOCK, BLOCK), 0)
    jj = lax.broadcasted_iota(jnp.int32, (BLOCK, BLOCK), 1)
    causal = jj <= ii
    anti = ii <= jj
    low = lax.broadcasted_iota(jnp.int32, (BLOCK, LANES), 1) < SSD_P

    def bwd(t, carry):
        c = n_chunks - 1 - t
        r = _rows(c)
        cm, bm, xs = cc_ref[r, :], bc_ref[r, :], xc_ref[r, :]
        xb = xs.astype(BF16)
        sc = _bdot_nt(cm, bm)
        yf = _bdot(cm, sfs_ref[c])
        yb = _bdot(cm, sb_ref[...])
        tiles = []
        for t2 in range(SSD_R // 2):
            xt = xb[:, t2 * LANES:(t2 + 1) * LANES]
            acc = None
            for hh in range(2):
                hr = 2 * t2 + hh
                rf, rb = _row_bcast(cumf_ref, c, hr), _row_bcast(cumb_ref, c, hr)
                e = jnp.exp(jnp.where(anti, rf, rb).T - jnp.where(causal, rf, rb))
                e = e * jnp.where(causal, _row_bcast(dtf_ref, c, hr), _row_bcast(dtb_ref, c, hr))
                xh = jnp.where(low, xt, 0.0) if hh == 0 else jnp.where(low, 0.0, xt)
                part = _bdot(sc * e, xh)
                acc = part if acc is None else acc + part
            tiles.append(acc)
        y = (jnp.concatenate(tiles, axis=1) + _pair_cols(ecf_ref, c) * yf + _pair_cols(ecb_ref, c) * yb
             + d_ref[...] * xs)
        yg = y * _silu(z_ref[r, :].astype(F32))
        ss_ref[r, :] = jnp.broadcast_to(jnp.sum(yg * yg, axis=-1, keepdims=True), (BLOCK, LANES))
        o_ref[r, :] = (yg * ng_ref[...]).astype(o_ref.dtype)
        state_update(sb_ref, c, wb_ref, totb_ref, bm, xs)
        return carry

    lax.fori_loop(0, n_chunks, bwd, 0, unroll=min(4, n_chunks))
    if emit_state:
        for r in range(SSD_R):
            sfin_ref[1, r] = sb_ref[:, r * SSD_P:(r + 1) * SSD_P]


def _ssd_scan(zx, prep, conv_w, conv_b, d_exp, norm_gain, seq_len, n_seq, row_block0, s0=None, emit_state=False,
              fill=None):
    n_chunks = seq_len // BLOCK
    has_s0 = s0 is not None
    xcol0 = D_INNER // GW
    bcol0 = 2 * D_INNER // SSD_N
    ccol0 = bcol0 + SSD_GROUPS
    cwb0 = D_INNER // SSD_N

    def rowcol(width, col0):
        return pl.BlockSpec((seq_len, width), lambda b, g, col0=col0: (b + row_block0, col0 + g))

    def headrows(direction):
        return pl.BlockSpec((n_chunks, SSD_R, BLOCK),
                            lambda b, g, direction=direction: (b + row_block0, direction * SSD_GROUPS + g, 0))

    in_specs = [rowcol(GW, 0), rowcol(GW, xcol0), rowcol(SSD_N, bcol0), rowcol(SSD_N, ccol0)]
    args = [zx, zx, zx, zx]
    for arr in prep:
        in_specs += [headrows(0), headrows(1)]
        args += [arr, arr]
    in_specs += [pl.BlockSpec((CONV_W, GW), lambda b, g: (0, g)),
                 pl.BlockSpec((CONV_W, SSD_N), lambda b, g: (0, cwb0 + g)),
                 pl.BlockSpec((CONV_W, SSD_N), lambda b, g: (0, cwb0 + SSD_GROUPS + g)),
                 pl.BlockSpec((1, GW), lambda b, g: (0, g)),
                 pl.BlockSpec((1, SSD_N), lambda b, g: (0, cwb0 + g)),
                 pl.BlockSpec((1, SSD_N), lambda b, g: (0, cwb0 + SSD_GROUPS + g)),
                 pl.BlockSpec((1, GW), lambda b, g: (0, g)),
                 pl.BlockSpec((1, GW), lambda b, g: (0, g))]
    args += [conv_w, conv_w, conv_w, conv_b, conv_b, conv_b, d_exp, norm_gain.reshape(1, D_INNER)]
    state_spec = pl.BlockSpec((None, None, 2, SSD_R, SSD_N, SSD_P), lambda b, g: (b, 0, 0, g, 0, 0))
    if has_s0:
        in_specs.append(state_spec)
        args.append(s0)
    aliases = _fill_alias(fill, in_specs, args)
    out_specs = [pl.BlockSpec((seq_len, GW), lambda b, g: (b + row_block0, g)),
                 pl.BlockSpec((seq_len, LANES), lambda b, g: (b + row_block0, g))]
    out_shape = [jax.ShapeDtypeStruct((T_ALL, D_INNER), BF16),
                 jax.ShapeDtypeStruct((T_ALL, SSD_GROUPS * LANES), F32)]
    if emit_state:
        out_specs.append(state_spec)
        out_shape.append(jax.ShapeDtypeStruct((n_seq, 1, 2, SSD_HEADS, SSD_N, SSD_P), F32))
    return pl.pallas_call(
        functools.partial(_ssd_kernel, n_chunks, has_s0, emit_state, bool(fill)),
        grid=(n_seq, SSD_GROUPS),
        in_specs=in_specs,
        out_specs=out_specs,
        out_shape=out_shape,
        input_output_aliases=aliases,
        scratch_shapes=[pltpu.VMEM((seq_len + 2 * HALO, GW + 2 * SSD_N), F32),
                        pltpu.VMEM((seq_len, GW), F32), pltpu.VMEM((seq_len, SSD_N), BF16),
                        pltpu.VMEM((seq_len, SSD_N), BF16), pltpu.VMEM((n_chunks, SSD_N, GW), BF16),
                        pltpu.VMEM((SSD_N, GW), F32), pltpu.VMEM((SSD_N, GW), F32)],
        compiler_params=_cparams("arbitrary", "arbitrary"),
        name="ssd_scan_latent" if has_s0 else "ssd_scan_prompt",
    )(*args)


def _router_kernel(x_ref, g_ref, mod_ref, rw_ref, hn_ref, idx_ref, wgt_ref, hf_ref):
    rw = rw_ref[...]
    r_hi = rw.astype(BF16)
    r_lo = (rw - r_hi.astype(F32)).astype(BF16)
    r_both = jnp.concatenate([r_hi, r_lo], axis=1)

    def route(r):
        hf = hf_ref[r, :]
        h_hi = hf.astype(BF16)
        h_lo = (hf - h_hi.astype(F32)).astype(BF16)
        hn_ref[r, :] = h_hi
        t1 = jnp.dot(h_hi, r_both, preferred_element_type=F32)
        logits = t1[:, :LANES] + t1[:, LANES:] + jnp.dot(h_lo, r_hi, preferred_element_type=F32)
        lane = lax.broadcasted_iota(jnp.int32, logits.shape, 1)
        lg = jnp.where(lane < N_EXPERTS, logits, NEG_INF)
        m1 = jnp.max(lg, axis=-1, keepdims=True)
        i1 = jnp.min(jnp.where(lg == m1, lane, LANES), axis=-1, keepdims=True)
        lg2 = jnp.where(lane == i1, NEG_INF, lg)
        m2 = jnp.max(lg2, axis=-1, keepdims=True)
        i2 = jnp.min(jnp.where(lg2 == m2, lane, LANES), axis=-1, keepdims=True)
        e2 = jnp.exp(m2 - m1)
        w1 = 1.0 / (1.0 + e2)
        idx_ref[r, :] = jnp.where(lane == 0, i1, jnp.where(lane == 1, i2, 0))
        wgt_ref[r, :] = jnp.where(lane == 0, w1, jnp.where(lane == 1, e2 * w1, 0.0))

    _adaln_then(x_ref, g_ref, mod_ref, 3, 4, hf_ref, route)


def _router(x, gain, modt, router_w):
    t, d = x.shape
    rw = jnp.pad(router_w, ((0, 0), (0, LANES - N_EXPERTS)))
    return pl.pallas_call(
        _router_kernel,
        grid=(t // ROW_TILE,),
        in_specs=[pl.BlockSpec((ROW_TILE, d), lambda i: (i, 0)),
                  pl.BlockSpec((1, d), lambda i: (0, 0)),
                  pl.BlockSpec((None, 8, d), lambda i: (i, 0, 0)),
                  pl.BlockSpec((d, LANES), lambda i: (0, 0))],
        out_specs=[pl.BlockSpec((ROW_TILE, d), lambda i: (i, 0)),
                   pl.BlockSpec((ROW_TILE, LANES), lambda i: (i, 0)),
                   pl.BlockSpec((ROW_TILE, LANES), lambda i: (i, 0))],
        out_shape=[jax.ShapeDtypeStruct((t, d), BF16), jax.ShapeDtypeStruct((t, LANES), jnp.int32),
                   jax.ShapeDtypeStruct((t, LANES), F32)],
        scratch_shapes=[pltpu.VMEM((ROW_TILE, d), F32)],
        compiler_params=_cparams("arbitrary"),
        name="moe_router",
    )(x, gain.reshape(1, d), modt, rw)


DOWN_ROWS = 512
DOWN_TN = 512
PAD_QUANTUM = 256


def _expert_changed(be_ref, blk, prev_blk, step):
    return jnp.logical_or(step == 0, be_ref[blk] != be_ref[prev_blk])


def _expert_up_kernel(be_ref, nu_ref, nv_ref, xs_ref, wg_ref, wu_ref, h_ref, wgb_ref, wub_ref):
    i = pl.program_id(1)
    n_rows = h_ref.shape[0]
    quanta = (nv_ref[i] + PAD_QUANTUM - 1) // PAD_QUANTUM

    @pl.when(jnp.logical_and(quanta > 0, _expert_changed(be_ref, i, jnp.maximum(i - 1, 0), i)))
    def _():
        wgb_ref[...] = wg_ref[...].astype(BF16)
        wub_ref[...] = wu_ref[...].astype(BF16)

    for m in range(n_rows // PAD_QUANTUM + 1):
        @pl.when(quanta == m)
        def _():
            live = m * PAD_QUANTUM
            if live:
                xs = xs_ref[:live, :]
                h_ref[:live, :] = (_silu(_bdot(xs, wgb_ref[...])) * _bdot(xs, wub_ref[...])).astype(h_ref.dtype)
            if live < n_rows:
                h_ref[live:, :] = jnp.zeros((n_rows - live, h_ref.shape[1]), h_ref.dtype)


def _expert_down_kernel(be_ref, nu_ref, nv_ref, h_ref, wd_ref, o_ref, wdb_ref):
    i = pl.program_id(1)
    per = MOE_ROWS // DOWN_ROWS
    blk = i // per
    n_rows = o_ref.shape[0]
    quanta = (jnp.clip(nv_ref[blk] - (i % per) * DOWN_ROWS, 0, DOWN_ROWS) + PAD_QUANTUM - 1) // PAD_QUANTUM

    @pl.when(jnp.logical_and(nv_ref[blk] > 0, _expert_changed(be_ref, blk, jnp.maximum(i - 1, 0) // per, i)))
    def _():
        wdb_ref[...] = wd_ref[...].astype(BF16)

    for m in range(n_rows // PAD_QUANTUM + 1):
        @pl.when(quanta == m)
        def _():
            live = m * PAD_QUANTUM
            if live:
                o_ref[:live, :] = _bdot(h_ref[:live, :], wdb_ref[...]).astype(o_ref.dtype)
            if live < n_rows:
                o_ref[live:, :] = jnp.zeros((n_rows - live, o_ref.shape[1]), o_ref.dtype)


def _experts(xs_sorted, block_e, n_used, n_valid, wg, wu, wd):
    cap, d = xs_sorted.shape
    ff = wg.shape[2]
    per = MOE_ROWS // DOWN_ROWS

    def expert_of(blk, be, nu):
        return be[jnp.minimum(blk, jnp.maximum(nu[0] - 1, 0))]

    h = pl.pallas_call(
        _expert_up_kernel,
        grid_spec=pltpu.PrefetchScalarGridSpec(
            num_scalar_prefetch=3,
            grid=(ff // FFN_TF, cap // MOE_ROWS),
            in_specs=[pl.BlockSpec((MOE_ROWS, d), lambda f, i, be, nu, nv: (i, 0)),
                      pl.BlockSpec((None, d, FFN_TF), lambda f, i, be, nu, nv: (expert_of(i, be, nu), 0, f)),
                      pl.BlockSpec((None, d, FFN_TF), lambda f, i, be, nu, nv: (expert_of(i, be, nu), 0, f))],
            out_specs=pl.BlockSpec((MOE_ROWS, FFN_TF), lambda f, i, be, nu, nv: (i, f)),
            scratch_shapes=[pltpu.VMEM((d, FFN_TF), BF16), pltpu.VMEM((d, FFN_TF), BF16)],
        ),
        out_shape=jax.ShapeDtypeStruct((cap, ff), BF16),
        compiler_params=_cparams("arbitrary", "arbitrary"),
        name="moe_expert_up",
    )(block_e, n_used, n_valid, xs_sorted, wg, wu)
    return pl.pallas_call(
        _expert_down_kernel,
        grid_spec=pltpu.PrefetchScalarGridSpec(
            num_scalar_prefetch=3,
            grid=(d // DOWN_TN, cap // DOWN_ROWS),
            in_specs=[pl.BlockSpec((DOWN_ROWS, ff), lambda n, i, be, nu, nv: (i, 0)),
                      pl.BlockSpec((None, ff, DOWN_TN),
                                   lambda n, i, be, nu, nv: (expert_of(i // per, be, nu), 0, n))],
            out_specs=pl.BlockSpec((DOWN_ROWS, DOWN_TN), lambda n, i, be, nu, nv: (i, n)),
            scratch_shapes=[pltpu.VMEM((ff, DOWN_TN), BF16)],
        ),
        out_shape=jax.ShapeDtypeStruct((cap, d), BF16),
        compiler_params=_cparams("arbitrary", "arbitrary"),
        name="moe_expert_down",
    )(block_e, n_used, n_valid, h, wd)


def _combine_kernel(x_ref, g0_ref, g1_ref, w_ref, mod_ref, o_ref):
    w = w_ref[...]
    y = w[:, 0:1] * g0_ref[...].astype(F32) + w[:, 1:2] * g1_ref[...].astype(F32)
    o_ref[...] = x_ref[...] + mod_ref[5:6, :] * y


COMB_ROWS = 512


def _combine(x, g, wgt, modt, row0, n_rows):
    t, d = x.shape
    b0 = row0 // COMB_ROWS
    per = ROW_TILE // COMB_ROWS
    return pl.pallas_call(
        _combine_kernel,
        grid=(n_rows // COMB_ROWS,),
        in_specs=[pl.BlockSpec((COMB_ROWS, d), lambda i: (i + b0, 0)),
                  pl.BlockSpec((COMB_ROWS, d), lambda i: (i + b0, 0)),
                  pl.BlockSpec((COMB_ROWS, d), lambda i: (i + b0 + t // COMB_ROWS, 0)),
                  pl.BlockSpec((COMB_ROWS, LANES), lambda i: (i + b0, 0)),
                  pl.BlockSpec((None, 8, d), lambda i: ((i + b0) // per, 0, 0))],
        out_specs=pl.BlockSpec((COMB_ROWS, d), lambda i: (i, 0)),
        out_shape=jax.ShapeDtypeStruct((n_rows, d), F32),
        compiler_params=_cparams("arbitrary"),
        name="moe_combine",
    )(x, g, g, wgt, modt)


def _moe(x, gain, modt, router_w, wg, wu, wd):
    t, d = x.shape
    hn, idx, wgt = _router(x, gain, modt, router_w)
    top_idx = idx[:, :TOP_K]
    n_slots = t * TOP_K
    flat_e = top_idx.reshape(-1)
    onehot = (flat_e[:, None] == jnp.arange(N_EXPERTS, dtype=jnp.int32)[None, :]).astype(jnp.int32)
    incl = jnp.cumsum(onehot, axis=0)
    counts = incl[-1]
    rank = jnp.sum((incl - onehot) * onehot, axis=1)
    padded = (counts + MOE_ROWS - 1) // MOE_ROWS * MOE_ROWS
    pend = jnp.cumsum(padded)
    pstart = pend - padded
    dest = pstart[flat_e] + rank
    n_blocks = n_slots // MOE_ROWS + N_EXPERTS
    cap = n_blocks * MOE_ROWS
    row_tok = (jnp.arange(cap, dtype=jnp.int32) % t).at[dest].set(jnp.arange(n_slots, dtype=jnp.int32) // TOP_K)
    block_e = jnp.clip(jnp.searchsorted(pend, jnp.arange(n_blocks, dtype=jnp.int32) * MOE_ROWS, side='right'),
                       0, N_EXPERTS - 1).astype(jnp.int32)
    n_used = (pend[-1:] // MOE_ROWS).astype(jnp.int32)
    blk = jnp.arange(n_blocks, dtype=jnp.int32)
    n_valid = jnp.where(blk < n_used[0],
                        jnp.clip((pstart + counts)[block_e] - blk * MOE_ROWS, 0, MOE_ROWS), 0).astype(jnp.int32)
    out = _experts(hn[row_tok], block_e, n_used, n_valid, wg, wu, wd)
    g = out[dest.reshape(t, TOP_K).T.reshape(-1)]
    return _combine(x, g, wgt, modt, 0, T_PROMPT), _combine(x, g, wgt, modt, T_PROMPT, T_SAMPLE)


def _rope_tables(n_tokens, dim):
    n_rows = n_tokens // GRID_W
    row = jnp.repeat(jnp.arange(n_rows), GRID_W).astype(F32)
    col = jnp.tile(jnp.arange(GRID_W), n_rows).astype(F32)
    n_freq = dim // 4
    inv = ROPE_BASE ** (-jnp.arange(n_freq, dtype=F32) / n_freq)
    ang = jnp.concatenate([row[:, None] * inv, col[:, None] * inv], axis=-1)
    return jnp.cos(ang), jnp.sin(ang)


def kernel(x_prompt, x_sample, state_ret, cache_k, cache_v, state_ssd, c, c_ctx, ada_w, ada_b, norm_mix, norm_ffn, ev_w_in, ev_w_out, ret_decay_logit, ret_norm, att_q_norm, att_k_norm, att_sink, ffn_w_gate, ffn_w_up, ffn_w_down, ssd_w_in, ssd_conv_w, ssd_conv_b, ssd_a_log, ssd_dt_bias, ssd_d, ssd_norm, ssd_w_out, moe_router, moe_w_gate, moe_w_up, moe_w_down):
    d = D_MODEL
    x = (x_prompt.reshape(T_PROMPT, d), x_sample.reshape(T_SAMPLE, d))

    cvecs = jnp.concatenate([c_ctx[None, :], c, jnp.zeros((MOD_ROWS - 1 - DEC_BATCH, d), F32)], axis=0)
    mods = _modulation(cvecs, ada_w, ada_b).reshape(2, MOD_ROWS, 6, d)
    tiles_per_seq = DEC_SEQ // ROW_TILE
    tile_row = jnp.concatenate([jnp.zeros((T_PROMPT // ROW_TILE,), jnp.int32),
                                1 + jnp.arange(T_SAMPLE // ROW_TILE, dtype=jnp.int32) // tiles_per_seq])
    modt = jnp.pad(mods[:, tile_row], ((0, 0), (0, 0), (0, 2), (0, 0)))

    ev_w_in_b, ev_w_out_b = ev_w_in[0].astype(BF16), ev_w_out[0].astype(BF16)
    ffn_wg_b, ffn_wu_b, ffn_wd_b = ffn_w_gate[0].astype(BF16), ffn_w_up[0].astype(BF16), ffn_w_down[0].astype(BF16)
    ssd_w_in_b, ssd_w_out_b = ssd_w_in[0].astype(BF16), ssd_w_out[0].astype(BF16)

    proj, = _adaln_matmul(x, norm_mix[0], modt[0], 0, 1, ev_w_in_b, tn=512, out_dtype=BF16, name="even_in_proj")
    lg = jax.nn.log_sigmoid(ret_decay_logit[0].astype(F32))
    cos_r, sin_r = _rope_tables(DEC_SEQ, RET_DK)
    cos_a, sin_a = _rope_tables(DEC_SEQ, ATT_HD)
    cos_a2 = jnp.concatenate([cos_a, cos_a], axis=-1)
    sin_a2 = jnp.concatenate([-sin_a, sin_a], axis=-1)
    mix_ret, new_state_ret = _retention(proj, lg, ret_norm[0], SEQ, BATCH, 0, emit_state=True)
    mix_ret, = _retention(proj, lg, ret_norm[0], DEC_SEQ, DEC_BATCH, T_PROMPT // DEC_SEQ,
                          ropes=(cos_r, sin_r), s0=state_ret, fill=(mix_ret,))
    mix_att, new_k, new_v = _attention_prompt(proj, att_sink[0], att_q_norm[0], att_k_norm[0])
    mix_att = _attention_latent(proj, cache_k[:, 0], cache_v[:, 0], att_sink[0], cos_a2, sin_a2,
                                att_q_norm[0], att_k_norm[0], mix_att)
    x = _proj_residual([mix_ret, mix_att], ev_w_out_b, x, modt[0], 2, tn=1024, name="even_out_proj")
    h = _ffn_gateup(x, norm_ffn[0], modt[0], ffn_wg_b, ffn_wu_b)
    x = _proj_residual([h], ffn_wd_b, x, modt[0], 5, tn=512, name="ffn_down")

    zx, dt_raw = _adaln_matmul(x, norm_mix[1], modt[1], 0, 1, ssd_w_in_b, tn=1024, n_out=SSD_ZX,
                               tail=2 * SSD_HEADS, out_dtype=BF16, name="ssd_in_proj")
    prep = _ssd_prep(dt_raw, ssd_dt_bias[0], ssd_a_log[0])
    d_exp = jnp.repeat(ssd_d[0], SSD_P)[None, :]
    conv_b = ssd_conv_b[0][None, :]
    yg, yss, new_state_ssd = _ssd_scan(zx, prep, ssd_conv_w[0], conv_b, d_exp, ssd_norm[0], SEQ, BATCH, 0,
                                       emit_state=True)
    yg, yss = _ssd_scan(zx, prep, ssd_conv_w[0], conv_b, d_exp, ssd_norm[0], DEC_SEQ, DEC_BATCH,
                        T_PROMPT // DEC_SEQ, s0=state_ssd, fill=(yg, yss))
    x = _proj_residual([yg], ssd_w_out_b, x, modt[1], 2, tn=512, row_ss=yss, name="ssd_out_proj")
    y_p, y_s = _moe(x, norm_ffn[1], modt[1], moe_router[0], moe_w_gate[0], moe_w_up[0], moe_w_down[0])

    y_prompt = y_p.reshape(BATCH, SEQ, d)
    y_sample = y_s.reshape(DEC_BATCH, DEC_SEQ, d)
    new_cache_k = new_k.reshape(BATCH, 1, SEQ, ATT_KV_HEADS, ATT_HD)
    new_cache_v = new_v.reshape(BATCH, 1, SEQ, ATT_KV_HEADS, ATT_HD)
    return (y_prompt, y_sample, new_state_ret, new_cache_k, new_cache_v, new_state_ssd)
```

```python
import functools

import jax
import jax.numpy as jnp
from jax import lax
from jax.experimental import pallas as pl
from jax.experimental.pallas import tpu as pltpu

F32 = jnp.float32
BF16 = jnp.bfloat16

D_MODEL = 2048
BATCH = 16
SEQ = 256
DEC_BATCH = 8
DEC_SEQ = 2048
PAST_LEN = 512
GRID_W = 64
BLOCK = 128
WINDOW = 128
EPS = 1e-6
ROPE_BASE = 10000.0
RET_HEADS = 4
RET_DK = 256
RET_DV = 256
ATT_HEADS = 8
ATT_KV_HEADS = 2
ATT_HD = 128
ATT_GROUP = ATT_HEADS // ATT_KV_HEADS
EVEN_IN = 5632
D_INNER = 2 * D_MODEL
SSD_P = 64
SSD_HEADS = D_INNER // SSD_P
SSD_N = 128
SSD_GROUPS = 8
SSD_R = SSD_HEADS // SSD_GROUPS
CONV_W = 5
CONV_CH = D_INNER + 2 * SSD_GROUPS * SSD_N
SSD_ZX = D_INNER + CONV_CH
D_FF = 5632
N_EXPERTS = 8
TOP_K = 2

T_PROMPT = BATCH * SEQ
T_SAMPLE = DEC_BATCH * DEC_SEQ
T_ALL = T_PROMPT + T_SAMPLE

LANES = 128
ROW_TILE = 1024
VMEM_LIMIT = 56 * 1024 * 1024
MOE_ROWS = 1024
NEG_INF = float("-inf")


def _cparams(*sem):
    return pltpu.CompilerParams(dimension_semantics=sem, vmem_limit_bytes=VMEM_LIMIT)


def _silu(x):
    return x * jax.nn.sigmoid(x)


def _bdot(a, b):
    return jnp.dot(a.astype(BF16), b.astype(BF16), preferred_element_type=F32)


def _bdot_nt(a, b):
    return lax.dot_general(a.astype(BF16), b.astype(BF16), (((1,), (1,)), ((), ())),
                           preferred_element_type=F32)


def _bdot_tn(a, b):
    return lax.dot_general(a.astype(BF16), b.astype(BF16), (((0,), (0,)), ((), ())),
                           preferred_element_type=F32)


def _rows(c, n=BLOCK):
    return pl.ds(pl.multiple_of(c * n, n), n)


MOD_ROWS = 16
MOD_TN = 1024


def _mod_kernel(c_ref, w_ref, b_ref, o_ref):
    o_ref[...] = _bdot(_silu(c_ref[...]), w_ref[...]) + b_ref[...]


def _modulation(cvecs, ada_w, ada_b):
    depth, d, n = ada_w.shape
    return pl.pallas_call(
        _mod_kernel,
        grid=(depth, n // MOD_TN),
        in_specs=[pl.BlockSpec((MOD_ROWS, d), lambda l, j: (0, 0)),
                  pl.BlockSpec((None, d, MOD_TN), lambda l, j: (l, 0, j)),
                  pl.BlockSpec((None, 1, MOD_TN), lambda l, j: (l, 0, j))],
        out_specs=pl.BlockSpec((None, MOD_ROWS, MOD_TN), lambda l, j: (l, 0, j)),
        out_shape=jax.ShapeDtypeStruct((depth, MOD_ROWS, n), F32),
        compiler_params=_cparams("arbitrary", "arbitrary"),
        name="modulation",
    )(cvecs, ada_w, ada_b.reshape(depth, 1, n))


ADALN_CHUNK = 64
ADALN_DOT_ROWS = 256


def _adaln_then(x_ref, g_ref, mod_ref, shift_row, scale_row, hn_ref, consume):
    mult = g_ref[...] * (1.0 + mod_ref[scale_row:scale_row + 1, :])
    shift = mod_ref[shift_row:shift_row + 1, :]
    for r0 in range(0, x_ref.shape[0], ADALN_DOT_ROWS):
        for s0 in range(r0, r0 + ADALN_DOT_ROWS, ADALN_CHUNK):
            r = slice(s0, s0 + ADALN_CHUNK)
            x = x_ref[r, :]
            ms = jnp.mean(x * x, axis=-1, keepdims=True)
            hn_ref[r, :] = (x * lax.rsqrt(ms + EPS) * mult + shift).astype(hn_ref.dtype)
        consume(slice(r0, r0 + ADALN_DOT_ROWS))


PROMPT_TILES = T_PROMPT // ROW_TILE


def _x_rows_specs(x, cols, col_of_j):
    if not isinstance(x, tuple):
        return [pl.BlockSpec((ROW_TILE, cols), lambda i, j: (i, col_of_j(j)))], [x]
    return ([pl.BlockSpec((ROW_TILE, cols),
                          lambda i, j: (jnp.minimum(i, PROMPT_TILES - 1), jnp.where(i < PROMPT_TILES, col_of_j(j), 0))),
             pl.BlockSpec((ROW_TILE, cols),
                          lambda i, j: (jnp.maximum(i - PROMPT_TILES, 0), jnp.where(i < PROMPT_TILES, 0, col_of_j(j))))],
            list(x))


def _for_x_rows(x_refs, fn):
    if len(x_refs) == 1:
        fn(x_refs[0])
        return
    in_prompt = pl.program_id(0) < PROMPT_TILES
    pl.when(in_prompt)(lambda: fn(x_refs[0]))
    pl.when(jnp.logical_not(in_prompt))(lambda: fn(x_refs[1]))


def _adaln_mm_kernel(shift_row, scale_row, has_tail, n_x, *refs):
    x_refs, (g_ref, mod_ref, w_ref), refs = refs[:n_x], refs[n_x:n_x + 3], refs[n_x + 3:]
    if has_tail:
        wt_ref, o_ref, ot_ref, hn_ref = refs
    else:
        o_ref, hn_ref = refs

    def project(r):
        hn = hn_ref[r, :]
        o_ref[r, :] = _bdot(hn, w_ref[...]).astype(o_ref.dtype)
        if has_tail:
            ot_ref[r, :] = _bdot(hn, wt_ref[...])

    first = pl.program_id(1) == 0

    @pl.when(first)
    def _():
        _for_x_rows(x_refs, lambda x_ref: _adaln_then(x_ref, g_ref, mod_ref, shift_row, scale_row, hn_ref, project))

    @pl.when(jnp.logical_not(first))
    def _():
        o_ref[...] = _bdot(hn_ref[...], w_ref[...]).astype(o_ref.dtype)


def _adaln_matmul(x, gain, modt, shift_row, scale_row, w, tn, n_out=None, tail=0, out_dtype=F32, name="adaln_mm"):
    t, d = T_ALL, w.shape[0]
    n_out = w.shape[1] if n_out is None else n_out
    in_specs, args = _x_rows_specs(x, d, lambda j: 0)
    n_x = len(args)
    in_specs += [pl.BlockSpec((1, d), lambda i, j: (0, 0)),
                 pl.BlockSpec((None, 8, d), lambda i, j: (i, 0, 0)),
                 pl.BlockSpec((d, tn), lambda i, j: (0, j))]
    args += [gain.reshape(1, d), modt, w]
    out_specs = [pl.BlockSpec((ROW_TILE, tn), lambda i, j: (i, j))]
    out_shape = [jax.ShapeDtypeStruct((t, n_out), out_dtype)]
    if tail:
        in_specs.append(pl.BlockSpec((d, tail), lambda i, j: (0, n_out // tail)))
        args.append(w)
        out_specs.append(pl.BlockSpec((ROW_TILE, tail), lambda i, j: (i, 0)))
        out_shape.append(jax.ShapeDtypeStruct((t, tail), F32))
    return pl.pallas_call(
        functools.partial(_adaln_mm_kernel, shift_row, scale_row, bool(tail), n_x),
        grid=(t // ROW_TILE, n_out // tn),
        in_specs=in_specs,
        out_specs=out_specs,
        out_shape=out_shape,
        scratch_shapes=[pltpu.VMEM((ROW_TILE, d), BF16)],
        compiler_params=_cparams("arbitrary", "arbitrary"),
        name=name,
    )(*args)


FFN_TF = 512


def _gateup_kernel(x_ref, g_ref, mod_ref, wg_ref, wu_ref, h_ref, hn_ref):
    def gate_up(r):
        hn = hn_ref[r, :]
        h_ref[r, :] = (_silu(_bdot(hn, wg_ref[...])) * _bdot(hn, wu_ref[...])).astype(h_ref.dtype)

    first = pl.program_id(1) == 0

    @pl.when(first)
    def _():
        _adaln_then(x_ref, g_ref, mod_ref, 3, 4, hn_ref, gate_up)

    @pl.when(jnp.logical_not(first))
    def _():
        gate_up(slice(None))


def _ffn_gateup(x, gain, modt, wg, wu):
    t, d = x.shape
    ff = wg.shape[1]
    return pl.pallas_call(
        _gateup_kernel,
        grid=(t // ROW_TILE, ff // FFN_TF),
        in_specs=[pl.BlockSpec((ROW_TILE, d), lambda i, f: (i, 0)),
                  pl.BlockSpec((1, d), lambda i, f: (0, 0)),
                  pl.BlockSpec((None, 8, d), lambda i, f: (i, 0, 0)),
                  pl.BlockSpec((d, FFN_TF), lambda i, f: (0, f)),
                  pl.BlockSpec((d, FFN_TF), lambda i, f: (0, f))],
        out_specs=pl.BlockSpec((ROW_TILE, FFN_TF), lambda i, f: (i, f)),
        out_shape=jax.ShapeDtypeStruct((t, ff), BF16),
        scratch_shapes=[pltpu.VMEM((ROW_TILE, d), BF16)],
        compiler_params=_cparams("arbitrary", "arbitrary"),
        name="ffn_gateup",
    )(x, gain.reshape(1, d), modt, wg, wu)


def _proj_res_kernel(n_a, n_x, gate_row, norm, *refs):
    a_refs = refs[:n_a]
    w_refs = refs[n_a:2 * n_a]
    x_refs, mod_ref = refs[2 * n_a:2 * n_a + n_x], refs[2 * n_a + n_x]
    pos = 2 * n_a + n_x + 1
    o_ref = refs[pos + 1] if norm else refs[pos]
    acc = _bdot(a_refs[0][...], w_refs[0][...])
    for k in range(1, n_a):
        acc = acc + _bdot(a_refs[k][...], w_refs[k][...])
    if norm:
        ss_ref = refs[pos]
        k_total = sum(a.shape[1] for a in a_refs)
        ss = ss_ref[:, 0:LANES]
        for k in range(1, ss_ref.shape[1] // LANES):
            ss = ss + ss_ref[:, k * LANES:(k + 1) * LANES]
        rs = lax.rsqrt(ss * (1.0 / k_total) + EPS)
        acc = acc * jnp.concatenate([rs] * (acc.shape[1] // LANES), axis=1)
    upd = mod_ref[gate_row:gate_row + 1, :] * acc

    def finish(x_ref):
        o_ref[...] = x_ref[...] + upd

    _for_x_rows(x_refs, finish)


def _proj_residual(a_list, w, x, modt, gate_row, tn, row_ss=None, name="proj_res"):
    t, d = T_ALL, w.shape[1]
    n_a = len(a_list)
    norm = row_ss is not None
    in_specs, args, k0 = [], [], 0
    for a in a_list:
        in_specs.append(pl.BlockSpec((ROW_TILE, a.shape[1]), lambda i, j: (i, 0)))
        args.append(a)
    for a in a_list:
        ka = a.shape[1]
        assert k0 % ka == 0
        in_specs.append(pl.BlockSpec((ka, tn), lambda i, j, kb=k0 // ka: (kb, j)))
        args.append(w)
        k0 += ka
    x_specs, x_args = _x_rows_specs(x, tn, lambda j: j)
    in_specs += x_specs + [pl.BlockSpec((None, 8, tn), lambda i, j: (i, 0, j))]
    args += x_args + [modt]
    if norm:
        in_specs.append(pl.BlockSpec((ROW_TILE, row_ss.shape[1]), lambda i, j: (i, 0)))
        args.append(row_ss)
    return pl.pallas_call(
        functools.partial(_proj_res_kernel, n_a, len(x_args), gate_row, norm),
        grid=(t // ROW_TILE, d // tn),
        in_specs=in_specs,
        out_specs=pl.BlockSpec((ROW_TILE, tn), lambda i, j: (i, j)),
        out_shape=jax.ShapeDtypeStruct((t, d), F32),
        compiler_params=_cparams("arbitrary", "arbitrary"),
        name=name,
    )(*args)


def _ret_kernel(n_chunks, rope, has_s0, emit_state, has_fill, lg_ref, *refs):
    it = iter(refs)
    q_ref, k_ref, v_ref, gt_ref = next(it), next(it), next(it), next(it)
    cos_ref, sin_ref = (next(it), next(it)) if rope else (None, None)
    s0_ref = next(it) if has_s0 else None
    gain_ref = next(it)
    if has_fill:
        next(it)
    o_ref = next(it)
    sfin_ref = next(it) if emit_state else None
    qs_ref, ks_ref, sfs_ref, dm_ref, dec_ref, sf_ref, sb_ref = (next(it) for _ in range(7))

    h = pl.program_id(1)
    lgf = lg_ref[0, h]
    lgb = lg_ref[1, h]
    half = RET_DK // 2

    def prep(c, carry):
        r = _rows(c)
        q = q_ref[r, :].astype(F32)
        k = k_ref[r, :].astype(F32) * (RET_DK ** -0.5)
        if rope:
            cs, sn = cos_ref[r, :], sin_ref[r, :]
            for src, dst in ((q, qs_ref), (k, ks_ref)):
                x1, x2 = src[:, :half], src[:, half:]
                dst[r, :half] = (x1 * cs - x2 * sn).astype(BF16)
                dst[r, half:] = (x1 * sn + x2 * cs).astype(BF16)
        else:
            qs_ref[r, :] = q.astype(BF16)
            ks_ref[r, :] = k.astype(BF16)
        return carry

    lax.fori_loop(0, n_chunks, prep, 0)

    ii = lax.broadcasted_iota(jnp.int32, (BLOCK, BLOCK), 0)
    jj = lax.broadcasted_iota(jnp.int32, (BLOCK, BLOCK), 1)
    diff = (ii - jj).astype(F32)
    dm_ref[...] = jnp.exp(jnp.where(jj <= ii, diff * lgf, -diff * lgb))
    pos = lax.broadcasted_iota(jnp.int32, (BLOCK, RET_DV), 0).astype(F32)
    dec_ref[0] = jnp.exp((pos + 1.0) * lgf)
    dec_ref[1] = jnp.exp((BLOCK - pos) * lgb)
    dec_ref[2] = jnp.exp((BLOCK - 1.0 - pos) * lgf)
    dec_ref[3] = jnp.exp(pos * lgb)
    tot_f = jnp.exp(jnp.full((1, RET_DV), BLOCK * lgf, F32))
    tot_b = jnp.exp(jnp.full((1, RET_DV), BLOCK * lgb, F32))

    if has_s0:
        sf_ref[...] = s0_ref[0]
        sb_ref[...] = s0_ref[1]
    else:
        sf_ref[...] = jnp.zeros_like(sf_ref)
        sb_ref[...] = jnp.zeros_like(sb_ref)

    def fwd(c, carry):
        r = _rows(c)
        sfs_ref[c] = sf_ref[...].astype(BF16)
        kd = ks_ref[r, :].astype(F32) * dec_ref[2]
        sf_ref[...] = sf_ref[...] * tot_f + _bdot_tn(kd, v_ref[r, :])
        return carry

    lax.fori_loop(0, n_chunks, fwd, 0, unroll=min(4, n_chunks))
    if emit_state:
        sfin_ref[0] = sf_ref[...]

    def bwd(t, carry):
        c = n_chunks - 1 - t
        r = _rows(c)
        q = qs_ref[r, :]
        k = ks_ref[r, :]
        v = v_ref[r, :].astype(BF16)
        p = _bdot_nt(q, k) * dm_ref[...]
        o = _bdot(p, v)
        o = o + _bdot(q, sfs_ref[c]) * dec_ref[0]
        o = o + _bdot(q, sb_ref[...]) * dec_ref[1]
        ms = jnp.mean(o * o, axis=-1, keepdims=True)
        y = o * lax.rsqrt(ms + EPS) * gain_ref[...]
        o_ref[r, :] = (y * _silu(gt_ref[r, :].astype(F32))).astype(o_ref.dtype)
        kd = k.astype(F32) * dec_ref[3]
        sb_ref[...] = sb_ref[...] * tot_b + _bdot_tn(kd, v)
        return carry

    lax.fori_loop(0, n_chunks, bwd, 0, unroll=min(4, n_chunks))
    if emit_state:
        sfin_ref[1] = sb_ref[...]


def _fill_alias(fill, in_specs, args):
    aliases = {}
    for k, arr in enumerate(fill or ()):
        in_specs.append(pl.BlockSpec(memory_space=pl.ANY))
        args.append(arr)
        aliases[len(args) - 1] = k
    return aliases


def _retention(proj, lg, ret_norm, seq_len, n_seq, row_block0, ropes=None, s0=None, emit_state=False, fill=None):
    n_chunks = seq_len // BLOCK
    rope, has_s0 = ropes is not None, s0 is not None

    def col(cb):
        return pl.BlockSpec((seq_len, RET_DK), lambda b, h, cb=cb: (b + row_block0, cb * RET_HEADS + h))

    in_specs = [pl.BlockSpec(memory_space=pltpu.SMEM), col(0), col(1), col(2), col(3)]
    args = [lg, proj, proj, proj, proj]
    if rope:
        in_specs += [pl.BlockSpec((seq_len, RET_DK // 2), lambda b, h: (0, 0))] * 2
        args += list(ropes)
    if has_s0:
        in_specs.append(pl.BlockSpec((None, None, 2, None, RET_DK, RET_DV), lambda b, h: (b, 0, 0, h, 0, 0)))
        args.append(s0)
    in_specs.append(pl.BlockSpec((1, RET_DV), lambda b, h: (0, h)))
    args.append(ret_norm.reshape(1, RET_HEADS * RET_DV))
    aliases = _fill_alias(fill, in_specs, args)
    out_specs = [pl.BlockSpec((seq_len, RET_DV), lambda b, h: (b + row_block0, h))]
    out_shape = [jax.ShapeDtypeStruct((T_ALL, RET_HEADS * RET_DV), BF16)]
    if emit_state:
        out_specs.append(pl.BlockSpec((None, None, 2, None, RET_DK, RET_DV), lambda b, h: (b, 0, 0, h, 0, 0)))
        out_shape.append(jax.ShapeDtypeStruct((n_seq, 1, 2, RET_HEADS, RET_DK, RET_DV), F32))
    return pl.pallas_call(
        functools.partial(_ret_kernel, n_chunks, rope, has_s0, emit_state, bool(fill)),
        grid=(n_seq, RET_HEADS),
        in_specs=in_specs,
        out_specs=out_specs,
        out_shape=out_shape,
        input_output_aliases=aliases,
        scratch_shapes=[pltpu.VMEM((seq_len, RET_DK), BF16), pltpu.VMEM((seq_len, RET_DK), BF16),
                        pltpu.VMEM((n_chunks, RET_DK, RET_DV), BF16), pltpu.VMEM((BLOCK, BLOCK), F32),
                        pltpu.VMEM((4, BLOCK, RET_DV), F32), pltpu.VMEM((RET_DK, RET_DV), F32),
                        pltpu.VMEM((RET_DK, RET_DV), F32)],
        compiler_params=_cparams("arbitrary", "arbitrary"),
        name="retention_latent" if rope else "retention_prompt",
    )(*args)


def _head_norm(x, gain):
    x = x.astype(F32)
    return x * lax.rsqrt(jnp.mean(x * x, axis=-1, keepdims=True) + EPS) * gain


def _rope_full(x, cs, sn):
    return x * cs + pltpu.roll(x, ATT_HD // 2, 1) * sn


ATT_SPLIT = 2


def _sink_column(sink_ref, head0, n_heads, rows_per_head):
    n = n_heads * rows_per_head
    head = lax.broadcasted_iota(jnp.int32, (n, 1), 0) // rows_per_head
    col = jnp.full((n, 1), sink_ref[head0], F32)
    for g in range(1, n_heads):
        col = jnp.where(head == g, sink_ref[head0 + g], col)
    return col


def _att_latent_kernel(sink_ref, q_ref, k_ref, v_ref, ck_ref, cv_ref, cos_ref, sin_ref, qn_ref, kn_ref,
                       fill_ref, o_ref, kp_ref, vp_ref, ckp_ref, cvp_ref):
    del fill_ref
    kh, qb = pl.program_id(1), pl.program_id(2)
    n_chunks = DEC_SEQ // BLOCK
    loc = 3 * BLOCK

    @pl.when(qb == 0)
    def _():
        def prep(c, carry):
            r = _rows(c)
            kp_ref[r, :] = _rope_full(_head_norm(k_ref[r, :], kn_ref[...]), cos_ref[r, :], sin_ref[r, :]).astype(BF16)
            vp_ref[r, :ATT_HD] = v_ref[r, :].astype(BF16)
            vp_ref[r, ATT_HD:] = jnp.ones((BLOCK, ATT_HD), BF16)
            return carry

        lax.fori_loop(0, n_chunks, prep, 0)
        ckp_ref[...] = ck_ref[...].astype(BF16)
        cvp_ref[:, :ATT_HD] = cv_ref[...].astype(BF16)
        cvp_ref[:, ATT_HD:] = jnp.ones((PAST_LEN, ATT_HD), BF16)

    rq = _rows(qb)
    cs, sn = cos_ref[rq, :], sin_ref[rq, :]
    qg = qn_ref[...] * (ATT_HD ** -0.5)
    start = pl.multiple_of(jnp.clip((qb - 1) * BLOCK, 0, DEC_SEQ - loc), BLOCK)
    kl, vl = kp_ref[pl.ds(start, loc), :], vp_ref[pl.ds(start, loc), :]
    qpos = qb * BLOCK + (lax.broadcasted_iota(jnp.int32, (ATT_SPLIT * BLOCK, loc), 0) & (BLOCK - 1))
    kpos = start + lax.broadcasted_iota(jnp.int32, (ATT_SPLIT * BLOCK, loc), 1)
    in_window = jnp.abs(qpos - kpos) <= WINDOW
    for g0 in range(0, ATT_GROUP, ATT_SPLIT):
        q = jnp.concatenate(
            [_rope_full(_head_norm(q_ref[:, g * ATT_HD:(g + 1) * ATT_HD], qg), cs, sn).astype(BF16)
             for g in range(g0, g0 + ATT_SPLIT)], axis=0)
        s_loc = jnp.where(in_window, _bdot_nt(q, kl), NEG_INF)
        s_ctx = _bdot_nt(q, ckp_ref[...])
        sink = _sink_column(sink_ref, kh * ATT_GROUP + g0, ATT_SPLIT, BLOCK)
        m = jnp.maximum(jnp.maximum(jnp.max(s_loc, axis=-1, keepdims=True),
                                    jnp.max(s_ctx, axis=-1, keepdims=True)), sink)
        pv = _bdot(jnp.exp(s_loc - m), vl) + _bdot(jnp.exp(s_ctx - m), cvp_ref[...])
        o = pv[:, :ATT_HD] / (pv[:, ATT_HD:] + jnp.exp(sink - m))
        for g in range(ATT_SPLIT):
            o_ref[:, (g0 + g) * ATT_HD:(g0 + g + 1) * ATT_HD] = o[g * BLOCK:(g + 1) * BLOCK, :].astype(o_ref.dtype)


def _attention_latent(proj, cache_k, cache_v, sink, cos_a, sin_a, q_norm, k_norm, fill):
    nqb = DEC_SEQ // BLOCK
    rb0 = T_PROMPT // DEC_SEQ
    qcol0 = 4 * RET_HEADS * RET_DK // (ATT_GROUP * ATT_HD)
    kcol0 = (4 * RET_HEADS * RET_DK + ATT_HEADS * ATT_HD) // ATT_HD
    vcol0 = kcol0 + ATT_KV_HEADS
    ck = cache_k.reshape(DEC_BATCH, PAST_LEN, ATT_KV_HEADS * ATT_HD)
    cv = cache_v.reshape(DEC_BATCH, PAST_LEN, ATT_KV_HEADS * ATT_HD)
    return pl.pallas_call(
        _att_latent_kernel,
        grid=(DEC_BATCH, ATT_KV_HEADS, nqb),
        in_specs=[pl.BlockSpec(memory_space=pltpu.SMEM),
                  pl.BlockSpec((BLOCK, ATT_GROUP * ATT_HD),
                               lambda b, kh, qb: (T_PROMPT // BLOCK + b * nqb + qb, qcol0 + kh)),
                  pl.BlockSpec((DEC_SEQ, ATT_HD), lambda b, kh, qb: (rb0 + b, kcol0 + kh)),
                  pl.BlockSpec((DEC_SEQ, ATT_HD), lambda b, kh, qb: (rb0 + b, vcol0 + kh)),
                  pl.BlockSpec((None, PAST_LEN, ATT_HD), lambda b, kh, qb: (b, 0, kh)),
                  pl.BlockSpec((None, PAST_LEN, ATT_HD), lambda b, kh, qb: (b, 0, kh)),
                  pl.BlockSpec((DEC_SEQ, ATT_HD), lambda b, kh, qb: (0, 0)),
                  pl.BlockSpec((DEC_SEQ, ATT_HD), lambda b, kh, qb: (0, 0)),
                  pl.BlockSpec((1, ATT_HD), lambda b, kh, qb: (0, 0)),
                  pl.BlockSpec((1, ATT_HD), lambda b, kh, qb: (0, 0)),
                  pl.BlockSpec(memory_space=pl.ANY)],
        out_specs=pl.BlockSpec((BLOCK, ATT_GROUP * ATT_HD),
                               lambda b, kh, qb: (T_PROMPT // BLOCK + b * nqb + qb, kh)),
        out_shape=jax.ShapeDtypeStruct((T_ALL, ATT_HEADS * ATT_HD), BF16),
        input_output_aliases={10: 0},
        scratch_shapes=[pltpu.VMEM((DEC_SEQ, ATT_HD), BF16), pltpu.VMEM((DEC_SEQ, 2 * ATT_HD), BF16),
                        pltpu.VMEM((PAST_LEN, ATT_HD), BF16), pltpu.VMEM((PAST_LEN, 2 * ATT_HD), BF16)],
        compiler_params=_cparams("arbitrary", "arbitrary", "arbitrary"),
        name="attention_latent",
    )(sink, proj, proj, proj, ck, cv, cos_a, sin_a, q_norm.reshape(1, ATT_HD), k_norm.reshape(1, ATT_HD), fill)


def _att_prompt_kernel(sink_ref, q_ref, k_ref, v_ref, qn_ref, kn_ref, o_ref, nk_ref, nv_ref):
    kh = pl.program_id(1)
    kn = _head_norm(k_ref[...], kn_ref[...])
    v = v_ref[...]
    nk_ref[...] = kn
    nv_ref[...] = v.astype(F32)
    qg = qn_ref[...] * (ATT_HD ** -0.5)
    q = jnp.concatenate([_head_norm(q_ref[:, g * ATT_HD:(g + 1) * ATT_HD], qg).astype(BF16)
                         for g in range(ATT_GROUP)], axis=0)
    s = _bdot_nt(q, kn)
    sink = _sink_column(sink_ref, kh * ATT_GROUP, ATT_GROUP, SEQ)
    m = jnp.maximum(jnp.max(s, axis=-1, keepdims=True), sink)
    pv = _bdot(jnp.exp(s - m), jnp.concatenate([v.astype(BF16), jnp.ones((SEQ, ATT_HD), BF16)], axis=1))
    o = pv[:, :ATT_HD] / (pv[:, ATT_HD:] + jnp.exp(sink - m))
    for g in range(ATT_GROUP):
        o_ref[:, g * ATT_HD:(g + 1) * ATT_HD] = o[g * SEQ:(g + 1) * SEQ, :].astype(o_ref.dtype)


def _attention_prompt(proj, sink, q_norm, k_norm):
    qcol0 = 4 * RET_HEADS * RET_DK // (ATT_GROUP * ATT_HD)
    kcol0 = (4 * RET_HEADS * RET_DK + ATT_HEADS * ATT_HD) // ATT_HD
    vcol0 = kcol0 + ATT_KV_HEADS
    kv_spec = pl.BlockSpec((None, SEQ, ATT_HD), lambda b, kh: (b, 0, kh))
    kv_shape = jax.ShapeDtypeStruct((BATCH, SEQ, ATT_KV_HEADS * ATT_HD), F32)
    return pl.pallas_call(
        _att_prompt_kernel,
        grid=(BATCH, ATT_KV_HEADS),
        in_specs=[pl.BlockSpec(memory_space=pltpu.SMEM),
                  pl.BlockSpec((SEQ, ATT_GROUP * ATT_HD), lambda b, kh: (b, qcol0 + kh)),
                  pl.BlockSpec((SEQ, ATT_HD), lambda b, kh: (b, kcol0 + kh)),
                  pl.BlockSpec((SEQ, ATT_HD), lambda b, kh: (b, vcol0 + kh)),
                  pl.BlockSpec((1, ATT_HD), lambda b, kh: (0, 0)),
                  pl.BlockSpec((1, ATT_HD), lambda b, kh: (0, 0))],
        out_specs=[pl.BlockSpec((SEQ, ATT_GROUP * ATT_HD), lambda b, kh: (b, kh)), kv_spec, kv_spec],
        out_shape=[jax.ShapeDtypeStruct((T_ALL, ATT_HEADS * ATT_HD), BF16), kv_shape, kv_shape],
        compiler_params=_cparams("arbitrary", "arbitrary"),
        name="attention_prompt",
    )(sink, proj, proj, proj, q_norm.reshape(1, ATT_HD), k_norm.reshape(1, ATT_HD))


def _split_dot(m01, a):
    hi = a.astype(BF16)
    r1 = a - hi.astype(F32)
    mid = r1.astype(BF16)
    lo = (r1 - mid.astype(F32)).astype(BF16)
    return (jnp.dot(m01, hi, preferred_element_type=F32) + jnp.dot(m01, mid, preferred_element_type=F32)
            + jnp.dot(m01, lo, preferred_element_type=F32))


PREP_CHUNKS = ROW_TILE // BLOCK


def _ssd_prep_kernel(raw_ref, bias_ref, alog_ref, cum_ref, dt_ref, w_ref, tot_ref, ecum_ref):
    ii = lax.broadcasted_iota(jnp.int32, (BLOCK, BLOCK), 0)
    jj = lax.broadcasted_iota(jnp.int32, (BLOCK, BLOCK), 1)
    lower = jnp.where(jj <= ii, 1.0, 0.0).astype(BF16)
    upper = jnp.where(jj >= ii, 1.0, 0.0).astype(BF16)
    fwd_lane = lax.broadcasted_iota(jnp.int32, (BLOCK, LANES), 1) < SSD_HEADS
    neg_a = -jnp.exp(alog_ref[...])

    def chunk(k, carry):
        x = raw_ref[_rows(k), :] + bias_ref[...]
        dt = jnp.maximum(x, 0.0) + jnp.log1p(jnp.exp(-jnp.abs(x)))
        a = dt * neg_a
        incl = _split_dot(lower, a)
        rincl = _split_dot(upper, a)
        cum = jnp.where(fwd_lane, incl, rincl)
        tot = jnp.where(fwd_lane[:1], incl[BLOCK - 1:BLOCK, :], rincl[0:1, :])
        cum_ref[k] = cum.T
        dt_ref[k] = dt.T
        w_ref[k] = (dt * jnp.exp(tot - cum)).T
        tot_ref[k] = jnp.broadcast_to(jnp.exp(tot), (BLOCK, LANES)).T
        ecum_ref[k] = jnp.exp(cum).T
        return carry

    lax.fori_loop(0, PREP_CHUNKS, chunk, 0)


def _ssd_prep(dt_raw, dt_bias, a_log):
    nc = T_ALL // BLOCK
    spec = pl.BlockSpec((PREP_CHUNKS, 2 * SSD_HEADS, BLOCK), lambda c: (c, 0, 0))
    shape = jax.ShapeDtypeStruct((nc, 2 * SSD_HEADS, BLOCK), F32)
    return pl.pallas_call(
        _ssd_prep_kernel,
        grid=(nc // PREP_CHUNKS,),
        in_specs=[pl.BlockSpec((ROW_TILE, 2 * SSD_HEADS), lambda c: (c, 0)),
                  pl.BlockSpec((1, 2 * SSD_HEADS), lambda c: (0, 0)),
                  pl.BlockSpec((1, 2 * SSD_HEADS), lambda c: (0, 0))],
        out_specs=[spec] * 5,
        out_shape=[shape] * 5,
        compiler_params=_cparams("arbitrary"),
        name="ssd_prep",
    )(dt_raw, dt_bias.reshape(1, 2 * SSD_HEADS), a_log.reshape(1, 2 * SSD_HEADS))


GW = SSD_R * SSD_P
HALO = 8


def _pair_tiles(per_head):
    low = lax.broadcasted_iota(jnp.int32, per_head[0].shape, 1) < SSD_P
    return jnp.concatenate([jnp.where(low, per_head[2 * t], per_head[2 * t + 1]) for t in range(SSD_R // 2)],
                           axis=1)


def _row_bcast(ref, c, r):
    return jnp.broadcast_to(ref[c, r:r + 1, :], (BLOCK, BLOCK))


def _pair_cols(ref, c):
    top = lax.broadcasted_iota(jnp.int32, (BLOCK, BLOCK), 0) < SSD_P
    return jnp.concatenate(
        [jnp.where(top, _row_bcast(ref, c, 2 * t), _row_bcast(ref, c, 2 * t + 1)).T for t in range(SSD_R // 2)],
        axis=1)


def _ssd_kernel(n_chunks, has_s0, emit_state, has_fill, *refs):
    it = iter(refs)
    z_ref, x_ref, b_ref, c_ref = (next(it) for _ in range(4))
    cumf_ref, cumb_ref, dtf_ref, dtb_ref, wf_ref, wb_ref, totf_ref, totb_ref, ecf_ref, ecb_ref = (
        next(it) for _ in range(10))
    cwx_ref, cwb_ref, cwc_ref, cbx_ref, cbb_ref, cbc_ref, d_ref, ng_ref = (next(it) for _ in range(8))
    s0_ref = next(it) if has_s0 else None
    if has_fill:
        next(it), next(it)
    o_ref, ss_ref = next(it), next(it)
    sfin_ref = next(it) if emit_state else None
    pad_ref, xc_ref, bc_ref, cc_ref, sfs_ref, sf_ref, sb_ref = (next(it) for _ in range(7))
    seq_len = n_chunks * BLOCK

    pad_ref[0:HALO, :] = jnp.zeros((HALO, GW + 2 * SSD_N), F32)
    pad_ref[HALO + seq_len:2 * HALO + seq_len, :] = jnp.zeros((HALO, GW + 2 * SSD_N), F32)

    def fill(c, carry):
        dst = pl.ds(pl.multiple_of(c * BLOCK, BLOCK) + HALO, BLOCK)
        r = _rows(c)
        pad_ref[dst, 0:GW] = x_ref[r, :].astype(F32)
        pad_ref[dst, GW:GW + SSD_N] = b_ref[r, :].astype(F32)
        pad_ref[dst, GW + SSD_N:GW + 2 * SSD_N] = c_ref[r, :].astype(F32)
        return carry

    lax.fori_loop(0, n_chunks, fill, 0)

    def conv(c):
        r = _rows(c)
        src = pl.ds(pl.multiple_of(c * BLOCK, BLOCK), BLOCK + 2 * HALO)
        for col0, width, cw_ref, cb_ref, dst in ((0, GW, cwx_ref, cbx_ref, xc_ref),
                                                 (GW, SSD_N, cwb_ref, cbb_ref, bc_ref),
                                                 (GW + SSD_N, SSD_N, cwc_ref, cbc_ref, cc_ref)):
            for t in range(width // LANES):
                tl = slice(t * LANES, (t + 1) * LANES)
                win = pad_ref[src, col0 + t * LANES:col0 + (t + 1) * LANES]
                acc = jnp.broadcast_to(cb_ref[:, tl], (BLOCK, LANES))
                for w in range(CONV_W):
                    off = HALO - CONV_W // 2 + w
                    acc = acc + win[off:off + BLOCK, :] * cw_ref[w:w + 1, tl]
                dst[r, tl] = _silu(acc).astype(dst.dtype)

    if has_s0:
        for r in range(SSD_R):
            sf_ref[:, r * SSD_P:(r + 1) * SSD_P] = s0_ref[0, r]
            sb_ref[:, r * SSD_P:(r + 1) * SSD_P] = s0_ref[1, r]
    else:
        sf_ref[...] = jnp.zeros_like(sf_ref)
        sb_ref[...] = jnp.zeros_like(sb_ref)

    def state_update(s_ref, c, w_ref, tot_ref, bm, xs):
        tot = _pair_tiles([tot_ref[c, r:r + 1, :] for r in range(SSD_R)])
        s_ref[...] = s_ref[...] * tot + _bdot_tn(bm, xs * _pair_cols(w_ref, c))

    def fwd_step(c):
        r = _rows(c)
        sfs_ref[c] = sf_ref[...].astype(BF16)
        state_update(sf_ref, c, wf_ref, totf_ref, bc_ref[r, :], xc_ref[r, :])

    def fwd(c, carry):
        fwd_step(c)
        conv(c + 1)
        return carry

    conv(0)
    lax.fori_loop(0, n_chunks - 1, fwd, 0, unroll=3 if (n_chunks - 1) % 3 == 0 else 1)
    fwd_step(n_chunks - 1)
    if emit_state:
        for r in range(SSD_R):
            sfin_ref[0, r] = sf_ref[:, r * SSD_P:(r + 1) * SSD_P]

    ii = lax.broadcasted_iota(jnp.int32, (BLOCK, BLOCK), 0)
    jj = lax.broadcasted_iota(jnp.int32, (BLOCK, BLOCK), 1)
    causal = jj <= ii
    anti = ii <= jj
    low = lax.broadcasted_iota(jnp.int32, (BLOCK, LANES), 1) < SSD_P

    def bwd(t, carry):
        c = n_chunks - 1 - t
        r = _rows(c)
        cm, bm, xs = cc_ref[r, :], bc_ref[r, :], xc_ref[r, :]
        xb = xs.astype(BF16)
        sc = _bdot_nt(cm, bm)
        yf = _bdot(cm, sfs_ref[c])
        yb = _bdot(cm, sb_ref[...])
        tiles = []
        for t2 in range(SSD_R // 2):
            xt = xb[:, t2 * LANES:(t2 + 1) * LANES]
            acc = None
            for hh in range(2):
                hr = 2 * t2 + hh
                rf, rb = _row_bcast(cumf_ref, c, hr), _row_bcast(cumb_ref, c, hr)
                e = jnp.exp(jnp.where(anti, rf, rb).T - jnp.where(causal, rf, rb))
                e = e * jnp.where(causal, _row_bcast(dtf_ref, c, hr), _row_bcast(dtb_ref, c, hr))
                xh = jnp.where(low, xt, 0.0) if hh == 0 else jnp.where(low, 0.0, xt)
                part = _bdot(sc * e, xh)
                acc = part if acc is None else acc + part
            tiles.append(acc)
        y = (jnp.concatenate(tiles, axis=1) + _pair_cols(ecf_ref, c) * yf + _pair_cols(ecb_ref, c) * yb
             + d_ref[...] * xs)
        yg = y * _silu(z_ref[r, :].astype(F32))
        ss_ref[r, :] = jnp.broadcast_to(jnp.sum(yg * yg, axis=-1, keepdims=True), (BLOCK, LANES))
        o_ref[r, :] = (yg * ng_ref[...]).astype(o_ref.dtype)
        state_update(sb_ref, c, wb_ref, totb_ref, bm, xs)
        return carry

    lax.fori_loop(0, n_chunks, bwd, 0, unroll=min(4, n_chunks))
    if emit_state:
        for r in range(SSD_R):
            sfin_ref[1, r] = sb_ref[:, r * SSD_P:(r + 1) * SSD_P]


def _ssd_scan(zx, prep, conv_w, conv_b, d_exp, norm_gain, seq_len, n_seq, row_block0, s0=None, emit_state=False,
              fill=None):
    n_chunks = seq_len // BLOCK
    has_s0 = s0 is not None
    xcol0 = D_INNER // GW
    bcol0 = 2 * D_INNER // SSD_N
    ccol0 = bcol0 + SSD_GROUPS
    cwb0 = D_INNER // SSD_N

    def rowcol(width, col0):
        return pl.BlockSpec((seq_len, width), lambda b, g, col0=col0: (b + row_block0, col0 + g))

    def headrows(direction):
        return pl.BlockSpec((n_chunks, SSD_R, BLOCK),
                            lambda b, g, direction=direction: (b + row_block0, direction * SSD_GROUPS + g, 0))

    in_specs = [rowcol(GW, 0), rowcol(GW, xcol0), rowcol(SSD_N, bcol0), rowcol(SSD_N, ccol0)]
    args = [zx, zx, zx, zx]
    for arr in prep:
        in_specs += [headrows(0), headrows(1)]
        args += [arr, arr]
    in_specs += [pl.BlockSpec((CONV_W, GW), lambda b, g: (0, g)),
                 pl.BlockSpec((CONV_W, SSD_N), lambda b, g: (0, cwb0 + g)),
                 pl.BlockSpec((CONV_W, SSD_N), lambda b, g: (0, cwb0 + SSD_GROUPS + g)),
                 pl.BlockSpec((1, GW), lambda b, g: (0, g)),
                 pl.BlockSpec((1, SSD_N), lambda b, g: (0, cwb0 + g)),
                 pl.BlockSpec((1, SSD_N), lambda b, g: (0, cwb0 + SSD_GROUPS + g)),
                 pl.BlockSpec((1, GW), lambda b, g: (0, g)),
                 pl.BlockSpec((1, GW), lambda b, g: (0, g))]
    args += [conv_w, conv_w, conv_w, conv_b, conv_b, conv_b, d_exp, norm_gain.reshape(1, D_INNER)]
    state_spec = pl.BlockSpec((None, None, 2, SSD_R, SSD_N, SSD_P), lambda b, g: (b, 0, 0, g, 0, 0))
    if has_s0:
        in_specs.append(state_spec)
        args.append(s0)
    aliases = _fill_alias(fill, in_specs, args)
    out_specs = [pl.BlockSpec((seq_len, GW), lambda b, g: (b + row_block0, g)),
                 pl.BlockSpec((seq_len, LANES), lambda b, g: (b + row_block0, g))]
    out_shape = [jax.ShapeDtypeStruct((T_ALL, D_INNER), BF16),
                 jax.ShapeDtypeStruct((T_ALL, SSD_GROUPS * LANES), F32)]
    if emit_state:
        out_specs.append(state_spec)
        out_shape.append(jax.ShapeDtypeStruct((n_seq, 1, 2, SSD_HEADS, SSD_N, SSD_P), F32))
    return pl.pallas_call(
        functools.partial(_ssd_kernel, n_chunks, has_s0, emit_state, bool(fill)),
        grid=(n_seq, SSD_GROUPS),
        in_specs=in_specs,
        out_specs=out_specs,
        out_shape=out_shape,
        input_output_aliases=aliases,
        scratch_shapes=[pltpu.VMEM((seq_len + 2 * HALO, GW + 2 * SSD_N), F32),
                        pltpu.VMEM((seq_len, GW), F32), pltpu.VMEM((seq_len, SSD_N), BF16),
                        pltpu.VMEM((seq_len, SSD_N), BF16), pltpu.VMEM((n_chunks, SSD_N, GW), BF16),
                        pltpu.VMEM((SSD_N, GW), F32), pltpu.VMEM((SSD_N, GW), F32)],
        compiler_params=_cparams("arbitrary", "arbitrary"),
        name="ssd_scan_latent" if has_s0 else "ssd_scan_prompt",
    )(*args)


def _router_kernel(x_ref, g_ref, mod_ref, rw_ref, hn_ref, idx_ref, wgt_ref, hf_ref):
    rw = rw_ref[...]
    r_hi = rw.astype(BF16)
    r_lo = (rw - r_hi.astype(F32)).astype(BF16)
    r_both = jnp.concatenate([r_hi, r_lo], axis=1)

    def route(r):
        hf = hf_ref[r, :]
        h_hi = hf.astype(BF16)
        h_lo = (hf - h_hi.astype(F32)).astype(BF16)
        hn_ref[r, :] = h_hi
        t1 = jnp.dot(h_hi, r_both, preferred_element_type=F32)
        logits = t1[:, :LANES] + t1[:, LANES:] + jnp.dot(h_lo, r_hi, preferred_element_type=F32)
        lane = lax.broadcasted_iota(jnp.int32, logits.shape, 1)
        lg = jnp.where(lane < N_EXPERTS, logits, NEG_INF)
        m1 = jnp.max(lg, axis=-1, keepdims=True)
        i1 = jnp.min(jnp.where(lg == m1, lane, LANES), axis=-1, keepdims=True)
        lg2 = jnp.where(lane == i1, NEG_INF, lg)
        m2 = jnp.max(lg2, axis=-1, keepdims=True)
        i2 = jnp.min(jnp.where(lg2 == m2, lane, LANES), axis=-1, keepdims=True)
        e2 = jnp.exp(m2 - m1)
        w1 = 1.0 / (1.0 + e2)
        idx_ref[r, :] = jnp.where(lane == 0, i1, jnp.where(lane == 1, i2, 0))
        wgt_ref[r, :] = jnp.where(lane == 0, w1, jnp.where(lane == 1, e2 * w1, 0.0))

    _adaln_then(x_ref, g_ref, mod_ref, 3, 4, hf_ref, route)


def _router(x, gain, modt, router_w):
    t, d = x.shape
    rw = jnp.pad(router_w, ((0, 0), (0, LANES - N_EXPERTS)))
    return pl.pallas_call(
        _router_kernel,
        grid=(t // ROW_TILE,),
        in_specs=[pl.BlockSpec((ROW_TILE, d), lambda i: (i, 0)),
                  pl.BlockSpec((1, d), lambda i: (0, 0)),
                  pl.BlockSpec((None, 8, d), lambda i: (i, 0, 0)),
                  pl.BlockSpec((d, LANES), lambda i: (0, 0))],
        out_specs=[pl.BlockSpec((ROW_TILE, d), lambda i: (i, 0)),
                   pl.BlockSpec((ROW_TILE, LANES), lambda i: (i, 0)),
                   pl.BlockSpec((ROW_TILE, LANES), lambda i: (i, 0))],
        out_shape=[jax.ShapeDtypeStruct((t, d), BF16), jax.ShapeDtypeStruct((t, LANES), jnp.int32),
                   jax.ShapeDtypeStruct((t, LANES), F32)],
        scratch_shapes=[pltpu.VMEM((ROW_TILE, d), F32)],
        compiler_params=_cparams("arbitrary"),
        name="moe_router",
    )(x, gain.reshape(1, d), modt, rw)


DOWN_ROWS = 512
DOWN_TN = 512


def _expert_changed(be_ref, blk, prev_blk, step):
    return jnp.logical_or(step == 0, be_ref[blk] != be_ref[prev_blk])


def _expert_up_kernel(be_ref, nu_ref, nv_ref, xs_ref, wg_ref, wu_ref, h_ref, wgb_ref, wub_ref):
    i = pl.program_id(1)
    used = nv_ref[i] > 0

    @pl.when(jnp.logical_and(used, _expert_changed(be_ref, i, jnp.maximum(i - 1, 0), i)))
    def _():
        wgb_ref[...] = wg_ref[...].astype(BF16)
        wub_ref[...] = wu_ref[...].astype(BF16)

    @pl.when(used)
    def _():
        xs = xs_ref[...]
        h_ref[...] = (_silu(_bdot(xs, wgb_ref[...])) * _bdot(xs, wub_ref[...])).astype(h_ref.dtype)

    @pl.when(jnp.logical_not(used))
    def _():
        h_ref[...] = jnp.zeros_like(h_ref)


def _expert_down_kernel(be_ref, nu_ref, nv_ref, h_ref, wd_ref, o_ref, wdb_ref):
    i = pl.program_id(1)
    per = MOE_ROWS // DOWN_ROWS
    blk = i // per
    used = nv_ref[blk] > (i % per) * DOWN_ROWS

    @pl.when(jnp.logical_and(nv_ref[blk] > 0, _expert_changed(be_ref, blk, jnp.maximum(i - 1, 0) // per, i)))
    def _():
        wdb_ref[...] = wd_ref[...].astype(BF16)

    @pl.when(used)
    def _():
        o_ref[...] = _bdot(h_ref[...], wdb_ref[...]).astype(o_ref.dtype)

    @pl.when(jnp.logical_not(used))
    def _():
        o_ref[...] = jnp.zeros_like(o_ref)


def _experts(xs_sorted, block_e, n_used, n_valid, wg, wu, wd):
    cap, d = xs_sorted.shape
    ff = wg.shape[2]
    per = MOE_ROWS // DOWN_ROWS

    def expert_of(blk, be, nu):
        return be[jnp.minimum(blk, jnp.maximum(nu[0] - 1, 0))]

    h = pl.pallas_call(
        _expert_up_kernel,
        grid_spec=pltpu.PrefetchScalarGridSpec(
            num_scalar_prefetch=3,
            grid=(ff // FFN_TF, cap // MOE_ROWS),
            in_specs=[pl.BlockSpec((MOE_ROWS, d), lambda f, i, be, nu, nv: (i, 0)),
                      pl.BlockSpec((None, d, FFN_TF), lambda f, i, be, nu, nv: (expert_of(i, be, nu), 0, f)),
                      pl.BlockSpec((None, d, FFN_TF), lambda f, i, be, nu, nv: (expert_of(i, be, nu), 0, f))],
            out_specs=pl.BlockSpec((MOE_ROWS, FFN_TF), lambda f, i, be, nu, nv: (i, f)),
            scratch_shapes=[pltpu.VMEM((d, FFN_TF), BF16), pltpu.VMEM((d, FFN_TF), BF16)],
        ),
        out_shape=jax.ShapeDtypeStruct((cap, ff), BF16),
        compiler_params=_cparams("arbitrary", "arbitrary"),
        name="moe_expert_up",
    )(block_e, n_used, n_valid, xs_sorted, wg, wu)
    return pl.pallas_call(
        _expert_down_kernel,
        grid_spec=pltpu.PrefetchScalarGridSpec(
            num_scalar_prefetch=3,
            grid=(d // DOWN_TN, cap // DOWN_ROWS),
            in_specs=[pl.BlockSpec((DOWN_ROWS, ff), lambda n, i, be, nu, nv: (i, 0)),
                      pl.BlockSpec((None, ff, DOWN_TN),
                                   lambda n, i, be, nu, nv: (expert_of(i // per, be, nu), 0, n))],
            out_specs=pl.BlockSpec((DOWN_ROWS, DOWN_TN), lambda n, i, be, nu, nv: (i, n)),
            scratch_shapes=[pltpu.VMEM((ff, DOWN_TN), BF16)],
        ),
        out_shape=jax.ShapeDtypeStruct((cap, d), BF16),
        compiler_params=_cparams("arbitrary", "arbitrary"),
        name="moe_expert_down",
    )(block_e, n_used, n_valid, h, wd)


def _combine_kernel(x_ref, g0_ref, g1_ref, w_ref, mod_ref, o_ref):
    w = w_ref[...]
    y = w[:, 0:1] * g0_ref[...].astype(F32) + w[:, 1:2] * g1_ref[...].astype(F32)
    o_ref[...] = x_ref[...] + mod_ref[5:6, :] * y


COMB_ROWS = ROW_TILE


def _combine(x, g, wgt, modt, row0, n_rows):
    t, d = x.shape
    b0 = row0 // COMB_ROWS
    per = ROW_TILE // COMB_ROWS
    return pl.pallas_call(
        _combine_kernel,
        grid=(n_rows // COMB_ROWS,),
        in_specs=[pl.BlockSpec((COMB_ROWS, d), lambda i: (i + b0, 0)),
                  pl.BlockSpec((COMB_ROWS, d), lambda i: (i + b0, 0)),
                  pl.BlockSpec((COMB_ROWS, d), lambda i: (i + b0 + t // COMB_ROWS, 0)),
                  pl.BlockSpec((COMB_ROWS, LANES), lambda i: (i + b0, 0)),
                  pl.BlockSpec((None, 8, d), lambda i: ((i + b0) // per, 0, 0))],
        out_specs=pl.BlockSpec((COMB_ROWS, d), lambda i: (i, 0)),
        out_shape=jax.ShapeDtypeStruct((n_rows, d), F32),
        compiler_params=_cparams("arbitrary"),
        name="moe_combine",
    )(x, g, g, wgt, modt)


def _moe(x, gain, modt, router_w, wg, wu, wd):
    t, d = x.shape
    hn, idx, wgt = _router(x, gain, modt, router_w)
    top_idx = idx[:, :TOP_K]
    n_slots = t * TOP_K
    flat_e = top_idx.reshape(-1)
    onehot = (flat_e[:, None] == jnp.arange(N_EXPERTS, dtype=jnp.int32)[None, :]).astype(jnp.int32)
    incl = jnp.cumsum(onehot, axis=0)
    counts = incl[-1]
    rank = jnp.sum((incl - onehot) * onehot, axis=1)
    padded = (counts + MOE_ROWS - 1) // MOE_ROWS * MOE_ROWS
    pend = jnp.cumsum(padded)
    pstart = pend - padded
    dest = pstart[flat_e] + rank
    n_blocks = n_slots // MOE_ROWS + N_EXPERTS
    cap = n_blocks * MOE_ROWS
    row_tok = (jnp.arange(cap, dtype=jnp.int32) % t).at[dest].set(jnp.arange(n_slots, dtype=jnp.int32) // TOP_K)
    block_e = jnp.clip(jnp.searchsorted(pend, jnp.arange(n_blocks, dtype=jnp.int32) * MOE_ROWS, side='right'),
                       0, N_EXPERTS - 1).astype(jnp.int32)
    n_used = (pend[-1:] // MOE_ROWS).astype(jnp.int32)
    blk = jnp.arange(n_blocks, dtype=jnp.int32)
    n_valid = jnp.where(blk < n_used[0],
                        jnp.clip((pstart + counts)[block_e] - blk * MOE_ROWS, 0, MOE_ROWS), 0).astype(jnp.int32)
    out = _experts(hn[row_tok], block_e, n_used, n_valid, wg, wu, wd)
    g = out[dest.reshape(t, TOP_K).T.reshape(-1)]
    return _combine(x, g, wgt, modt, 0, T_PROMPT), _combine(x, g, wgt, modt, T_PROMPT, T_SAMPLE)


def _rope_tables(n_tokens, dim):
    n_rows = n_tokens // GRID_W
    row = jnp.repeat(jnp.arange(n_rows), GRID_W).astype(F32)
    col = jnp.tile(jnp.arange(GRID_W), n_rows).astype(F32)
    n_freq = dim // 4
    inv = ROPE_BASE ** (-jnp.arange(n_freq, dtype=F32) / n_freq)
    ang = jnp.concatenate([row[:, None] * inv, col[:, None] * inv], axis=-1)
    return jnp.cos(ang), jnp.sin(ang)


def kernel(x_prompt, x_sample, state_ret, cache_k, cache_v, state_ssd, c, c_ctx, ada_w, ada_b, norm_mix, norm_ffn, ev_w_in, ev_w_out, ret_decay_logit, ret_norm, att_q_norm, att_k_norm, att_sink, ffn_w_gate, ffn_w_up, ffn_w_down, ssd_w_in, ssd_conv_w, ssd_conv_b, ssd_a_log, ssd_dt_bias, ssd_d, ssd_norm, ssd_w_out, moe_router, moe_w_gate, moe_w_up, moe_w_down):
    d = D_MODEL
    x = (x_prompt.reshape(T_PROMPT, d), x_sample.reshape(T_SAMPLE, d))

    cvecs = jnp.concatenate([c_ctx[None, :], c, jnp.zeros((MOD_ROWS - 1 - DEC_BATCH, d), F32)], axis=0)
    mods = _modulation(cvecs, ada_w, ada_b).reshape(2, MOD_ROWS, 6, d)
    tiles_per_seq = DEC_SEQ // ROW_TILE
    tile_row = jnp.concatenate([jnp.zeros((T_PROMPT // ROW_TILE,), jnp.int32),
                                1 + jnp.arange(T_SAMPLE // ROW_TILE, dtype=jnp.int32) // tiles_per_seq])
    modt = jnp.pad(mods[:, tile_row], ((0, 0), (0, 0), (0, 2), (0, 0)))

    ev_w_in_b, ev_w_out_b = ev_w_in[0].astype(BF16), ev_w_out[0].astype(BF16)
    ffn_wg_b, ffn_wu_b, ffn_wd_b = ffn_w_gate[0].astype(BF16), ffn_w_up[0].astype(BF16), ffn_w_down[0].astype(BF16)
    ssd_w_in_b, ssd_w_out_b = ssd_w_in[0].astype(BF16), ssd_w_out[0].astype(BF16)

    proj, = _adaln_matmul(x, norm_mix[0], modt[0], 0, 1, ev_w_in_b, tn=512, out_dtype=BF16, name="even_in_proj")
    lg = jax.nn.log_sigmoid(ret_decay_logit[0].astype(F32))
    cos_r, sin_r = _rope_tables(DEC_SEQ, RET_DK)
    cos_a, sin_a = _rope_tables(DEC_SEQ, ATT_HD)
    cos_a2 = jnp.concatenate([cos_a, cos_a], axis=-1)
    sin_a2 = jnp.concatenate([-sin_a, sin_a], axis=-1)
    mix_ret, new_state_ret = _retention(proj, lg, ret_norm[0], SEQ, BATCH, 0, emit_state=True)
    mix_ret, = _retention(proj, lg, ret_norm[0], DEC_SEQ, DEC_BATCH, T_PROMPT // DEC_SEQ,
                          ropes=(cos_r, sin_r), s0=state_ret, fill=(mix_ret,))
    mix_att, new_k, new_v = _attention_prompt(proj, att_sink[0], att_q_norm[0], att_k_norm[0])
    mix_att = _attention_latent(proj, cache_k[:, 0], cache_v[:, 0], att_sink[0], cos_a2, sin_a2,
                                att_q_norm[0], att_k_norm[0], mix_att)
    x = _proj_residual([mix_ret, mix_att], ev_w_out_b, x, modt[0], 2, tn=1024, name="even_out_proj")
    h = _ffn_gateup(x, norm_ffn[0], modt[0], ffn_wg_b, ffn_wu_b)
    x = _proj_residual([h], ffn_wd_b, x, modt[0], 5, tn=512, name="ffn_down")

    zx, dt_raw = _adaln_matmul(x, norm_mix[1], modt[1], 0, 1, ssd_w_in_b, tn=1024, n_out=SSD_ZX,
                               tail=2 * SSD_HEADS, out_dtype=BF16, name="ssd_in_proj")
    prep = _ssd_prep(dt_raw, ssd_dt_bias[0], ssd_a_log[0])
    d_exp = jnp.repeat(ssd_d[0], SSD_P)[None, :]
    conv_b = ssd_conv_b[0][None, :]
    yg, yss, new_state_ssd = _ssd_scan(zx, prep, ssd_conv_w[0], conv_b, d_exp, ssd_norm[0], SEQ, BATCH, 0,
                                       emit_state=True)
    yg, yss = _ssd_scan(zx, prep, ssd_conv_w[0], conv_b, d_exp, ssd_norm[0], DEC_SEQ, DEC_BATCH,
                        T_PROMPT // DEC_SEQ, s0=state_ssd, fill=(yg, yss))
    x = _proj_residual([yg], ssd_w_out_b, x, modt[1], 2, tn=512, row_ss=yss, name="ssd_out_proj")
    y_p, y_s = _moe(x, norm_ffn[1], modt[1], moe_router[0], moe_w_gate[0], moe_w_up[0], moe_w_down[0])

    y_prompt = y_p.reshape(BATCH, SEQ, d)
    y_sample = y_s.reshape(DEC_BATCH, DEC_SEQ, d)
    new_cache_k = new_k.reshape(BATCH, 1, SEQ, ATT_KV_HEADS, ATT_HD)
    new_cache_v = new_v.reshape(BATCH, 1, SEQ, ATT_KV_HEADS, ATT_HD)
    return (y_prompt, y_sample, new_state_ret, new_cache_k, new_cache_v, new_state_ssd)
```

```python
import functools

import jax
import jax.numpy as jnp
from jax import lax
from jax.experimental import pallas as pl
from jax.experimental.pallas import tpu as pltpu

F32 = jnp.float32
BF16 = jnp.bfloat16

D_MODEL = 2048
BATCH = 16
SEQ = 256
DEC_BATCH = 8
DEC_SEQ = 2048
PAST_LEN = 512
GRID_W = 64
BLOCK = 128
WINDOW = 128
EPS = 1e-6
ROPE_BASE = 10000.0
RET_HEADS = 4
RET_DK = 256
RET_DV = 256
ATT_HEADS = 8
ATT_KV_HEADS = 2
ATT_HD = 128
ATT_GROUP = ATT_HEADS // ATT_KV_HEADS
EVEN_IN = 5632
D_INNER = 2 * D_MODEL
SSD_P = 64
SSD_HEADS = D_INNER // SSD_P
SSD_N = 128
SSD_GROUPS = 8
SSD_R = SSD_HEADS // SSD_GROUPS
CONV_W = 5
CONV_CH = D_INNER + 2 * SSD_GROUPS * SSD_N
SSD_ZX = D_INNER + CONV_CH
D_FF = 5632
N_EXPERTS = 8
TOP_K = 2

T_PROMPT = BATCH * SEQ
T_SAMPLE = DEC_BATCH * DEC_SEQ
T_ALL = T_PROMPT + T_SAMPLE

LANES = 128
ROW_TILE = 1024
VMEM_LIMIT = 56 * 1024 * 1024
MOE_ROWS = 1024
NEG_INF = float("-inf")


def _cparams(*sem):
    return pltpu.CompilerParams(dimension_semantics=sem, vmem_limit_bytes=VMEM_LIMIT)


def _silu(x):
    return x * jax.nn.sigmoid(x)


def _bdot(a, b):
    return jnp.dot(a.astype(BF16), b.astype(BF16), preferred_element_type=F32)


def _bdot_nt(a, b):
    return lax.dot_general(a.astype(BF16), b.astype(BF16), (((1,), (1,)), ((), ())),
                           preferred_element_type=F32)


def _bdot_tn(a, b):
    return lax.dot_general(a.astype(BF16), b.astype(BF16), (((0,), (0,)), ((), ())),
                           preferred_element_type=F32)


def _rows(c, n=BLOCK):
    return pl.ds(pl.multiple_of(c * n, n), n)


MOD_ROWS = 16
MOD_TN = 1024


def _mod_kernel(c_ref, w_ref, b_ref, o_ref):
    o_ref[...] = _bdot(_silu(c_ref[...]), w_ref[...]) + b_ref[...]


def _modulation(cvecs, ada_w, ada_b):
    depth, d, n = ada_w.shape
    return pl.pallas_call(
        _mod_kernel,
        grid=(depth, n // MOD_TN),
        in_specs=[pl.BlockSpec((MOD_ROWS, d), lambda l, j: (0, 0)),
                  pl.BlockSpec((None, d, MOD_TN), lambda l, j: (l, 0, j)),
                  pl.BlockSpec((None, 1, MOD_TN), lambda l, j: (l, 0, j))],
        out_specs=pl.BlockSpec((None, MOD_ROWS, MOD_TN), lambda l, j: (l, 0, j)),
        out_shape=jax.ShapeDtypeStruct((depth, MOD_ROWS, n), F32),
        compiler_params=_cparams("arbitrary", "arbitrary"),
        name="modulation",
    )(cvecs, ada_w, ada_b.reshape(depth, 1, n))


ADALN_CHUNK = 64
ADALN_DOT_ROWS = 256


def _adaln_then(x_ref, g_ref, mod_ref, shift_row, scale_row, hn_ref, consume):
    mult = g_ref[...] * (1.0 + mod_ref[scale_row:scale_row + 1, :])
    shift = mod_ref[shift_row:shift_row + 1, :]
    for r0 in range(0, x_ref.shape[0], ADALN_DOT_ROWS):
        for s0 in range(r0, r0 + ADALN_DOT_ROWS, ADALN_CHUNK):
            r = slice(s0, s0 + ADALN_CHUNK)
            x = x_ref[r, :]
            ms = jnp.mean(x * x, axis=-1, keepdims=True)
            hn_ref[r, :] = (x * lax.rsqrt(ms + EPS) * mult + shift).astype(hn_ref.dtype)
        consume(slice(r0, r0 + ADALN_DOT_ROWS))


PROMPT_TILES = T_PROMPT // ROW_TILE


def _x_rows_specs(x, cols, col_of_j):
    if not isinstance(x, tuple):
        return [pl.BlockSpec((ROW_TILE, cols), lambda i, j: (i, col_of_j(j)))], [x]
    return ([pl.BlockSpec((ROW_TILE, cols),
                          lambda i, j: (jnp.minimum(i, PROMPT_TILES - 1), jnp.where(i < PROMPT_TILES, col_of_j(j), 0))),
             pl.BlockSpec((ROW_TILE, cols),
                          lambda i, j: (jnp.maximum(i - PROMPT_TILES, 0), jnp.where(i < PROMPT_TILES, 0, col_of_j(j))))],
            list(x))


def _for_x_rows(x_refs, fn):
    if len(x_refs) == 1:
        fn(x_refs[0])
        return
    in_prompt = pl.program_id(0) < PROMPT_TILES
    pl.when(in_prompt)(lambda: fn(x_refs[0]))
    pl.when(jnp.logical_not(in_prompt))(lambda: fn(x_refs[1]))


def _adaln_mm_kernel(shift_row, scale_row, has_tail, n_x, *refs):
    x_refs, (g_ref, mod_ref, w_ref), refs = refs[:n_x], refs[n_x:n_x + 3], refs[n_x + 3:]
    if has_tail:
        wt_ref, o_ref, ot_ref, hn_ref = refs
    else:
        o_ref, hn_ref = refs

    def project(r):
        hn = hn_ref[r, :]
        o_ref[r, :] = _bdot(hn, w_ref[...]).astype(o_ref.dtype)
        if has_tail:
            ot_ref[r, :] = _bdot(hn, wt_ref[...])

    first = pl.program_id(1) == 0

    @pl.when(first)
    def _():
        _for_x_rows(x_refs, lambda x_ref: _adaln_then(x_ref, g_ref, mod_ref, shift_row, scale_row, hn_ref, project))

    @pl.when(jnp.logical_not(first))
    def _():
        o_ref[...] = _bdot(hn_ref[...], w_ref[...]).astype(o_ref.dtype)


def _adaln_matmul(x, gain, modt, shift_row, scale_row, w, tn, n_out=None, tail=0, out_dtype=F32, name="adaln_mm"):
    t, d = T_ALL, w.shape[0]
    n_out = w.shape[1] if n_out is None else n_out
    in_specs, args = _x_rows_specs(x, d, lambda j: 0)
    n_x = len(args)
    in_specs += [pl.BlockSpec((1, d), lambda i, j: (0, 0)),
                 pl.BlockSpec((None, 8, d), lambda i, j: (i, 0, 0)),
                 pl.BlockSpec((d, tn), lambda i, j: (0, j))]
    args += [gain.reshape(1, d), modt, w]
    out_specs = [pl.BlockSpec((ROW_TILE, tn), lambda i, j: (i, j))]
    out_shape = [jax.ShapeDtypeStruct((t, n_out), out_dtype)]
    if tail:
        in_specs.append(pl.BlockSpec((d, tail), lambda i, j: (0, n_out // tail)))
        args.append(w)
        out_specs.append(pl.BlockSpec((ROW_TILE, tail), lambda i, j: (i, 0)))
        out_shape.append(jax.ShapeDtypeStruct((t, tail), F32))
    return pl.pallas_call(
        functools.partial(_adaln_mm_kernel, shift_row, scale_row, bool(tail), n_x),
        grid=(t // ROW_TILE, n_out // tn),
        in_specs=in_specs,
        out_specs=out_specs,
        out_shape=out_shape,
        scratch_shapes=[pltpu.VMEM((ROW_TILE, d), BF16)],
        compiler_params=_cparams("arbitrary", "arbitrary"),
        name=name,
    )(*args)


FFN_TF = 512


def _gateup_kernel(x_ref, g_ref, mod_ref, wg_ref, wu_ref, h_ref, hn_ref):
    def gate_up(r):
        hn = hn_ref[r, :]
        h_ref[r, :] = (_silu(_bdot(hn, wg_ref[...])) * _bdot(hn, wu_ref[...])).astype(h_ref.dtype)

    first = pl.program_id(1) == 0

    @pl.when(first)
    def _():
        _adaln_then(x_ref, g_ref, mod_ref, 3, 4, hn_ref, gate_up)

    @pl.when(jnp.logical_not(first))
    def _():
        gate_up(slice(None))


def _ffn_gateup(x, gain, modt, wg, wu):
    t, d = x.shape
    ff = wg.shape[1]
    return pl.pallas_call(
        _gateup_kernel,
        grid=(t // ROW_TILE, ff // FFN_TF),
        in_specs=[pl.BlockSpec((ROW_TILE, d), lambda i, f: (i, 0)),
                  pl.BlockSpec((1, d), lambda i, f: (0, 0)),
                  pl.BlockSpec((None, 8, d), lambda i, f: (i, 0, 0)),
                  pl.BlockSpec((d, FFN_TF), lambda i, f: (0, f)),
                  pl.BlockSpec((d, FFN_TF), lambda i, f: (0, f))],
        out_specs=pl.BlockSpec((ROW_TILE, FFN_TF), lambda i, f: (i, f)),
        out_shape=jax.ShapeDtypeStruct((t, ff), BF16),
        scratch_shapes=[pltpu.VMEM((ROW_TILE, d), BF16)],
        compiler_params=_cparams("arbitrary", "arbitrary"),
        name="ffn_gateup",
    )(x, gain.reshape(1, d), modt, wg, wu)


def _proj_res_kernel(n_a, n_x, gate_row, norm, *refs):
    a_refs = refs[:n_a]
    w_refs = refs[n_a:2 * n_a]
    x_refs, mod_ref = refs[2 * n_a:2 * n_a + n_x], refs[2 * n_a + n_x]
    pos = 2 * n_a + n_x + 1
    o_ref = refs[pos + 1] if norm else refs[pos]
    acc = _bdot(a_refs[0][...], w_refs[0][...])
    for k in range(1, n_a):
        acc = acc + _bdot(a_refs[k][...], w_refs[k][...])
    if norm:
        ss_ref = refs[pos]
        k_total = sum(a.shape[1] for a in a_refs)
        ss = ss_ref[:, 0:LANES]
        for k in range(1, ss_ref.shape[1] // LANES):
            ss = ss + ss_ref[:, k * LANES:(k + 1) * LANES]
        rs = lax.rsqrt(ss * (1.0 / k_total) + EPS)
        acc = acc * jnp.concatenate([rs] * (acc.shape[1] // LANES), axis=1)
    upd = mod_ref[gate_row:gate_row + 1, :] * acc

    def finish(x_ref):
        o_ref[...] = x_ref[...] + upd

    _for_x_rows(x_refs, finish)


def _proj_residual(a_list, w, x, modt, gate_row, tn, row_ss=None, name="proj_res"):
    t, d = T_ALL, w.shape[1]
    n_a = len(a_list)
    norm = row_ss is not None
    in_specs, args, k0 = [], [], 0
    for a in a_list:
        in_specs.append(pl.BlockSpec((ROW_TILE, a.shape[1]), lambda i, j: (i, 0)))
        args.append(a)
    for a in a_list:
        ka = a.shape[1]
        assert k0 % ka == 0
        in_specs.append(pl.BlockSpec((ka, tn), lambda i, j, kb=k0 // ka: (kb, j)))
        args.append(w)
        k0 += ka
    x_specs, x_args = _x_rows_specs(x, tn, lambda j: j)
    in_specs += x_specs + [pl.BlockSpec((None, 8, tn), lambda i, j: (i, 0, j))]
    args += x_args + [modt]
    if norm:
        in_specs.append(pl.BlockSpec((ROW_TILE, row_ss.shape[1]), lambda i, j: (i, 0)))
        args.append(row_ss)
    return pl.pallas_call(
        functools.partial(_proj_res_kernel, n_a, len(x_args), gate_row, norm),
        grid=(t // ROW_TILE, d // tn),
        in_specs=in_specs,
        out_specs=pl.BlockSpec((ROW_TILE, tn), lambda i, j: (i, j)),
        out_shape=jax.ShapeDtypeStruct((t, d), F32),
        compiler_params=_cparams("arbitrary", "arbitrary"),
        name=name,
    )(*args)


def _ret_kernel(n_chunks, rope, has_s0, emit_state, has_fill, lg_ref, *refs):
    it = iter(refs)
    q_ref, k_ref, v_ref, gt_ref = next(it), next(it), next(it), next(it)
    cos_ref, sin_ref = (next(it), next(it)) if rope else (None, None)
    s0_ref = next(it) if has_s0 else None
    gain_ref = next(it)
    if has_fill:
        next(it)
    o_ref = next(it)
    sfin_ref = next(it) if emit_state else None
    qs_ref, ks_ref, sfs_ref, dm_ref, dec_ref, sf_ref, sb_ref = (next(it) for _ in range(7))

    h = pl.program_id(1)
    lgf = lg_ref[0, h]
    lgb = lg_ref[1, h]
    half = RET_DK // 2

    def prep(c, carry):
        r = _rows(c)
        q = q_ref[r, :].astype(F32)
        k = k_ref[r, :].astype(F32) * (RET_DK ** -0.5)
        if rope:
            cs, sn = cos_ref[r, :], sin_ref[r, :]
            for src, dst in ((q, qs_ref), (k, ks_ref)):
                x1, x2 = src[:, :half], src[:, half:]
                dst[r, :half] = (x1 * cs - x2 * sn).astype(BF16)
                dst[r, half:] = (x1 * sn + x2 * cs).astype(BF16)
        else:
            qs_ref[r, :] = q.astype(BF16)
            ks_ref[r, :] = k.astype(BF16)
        return carry

    lax.fori_loop(0, n_chunks, prep, 0)

    ii = lax.broadcasted_iota(jnp.int32, (BLOCK, BLOCK), 0)
    jj = lax.broadcasted_iota(jnp.int32, (BLOCK, BLOCK), 1)
    diff = (ii - jj).astype(F32)
    dm_ref[...] = jnp.exp(jnp.where(jj <= ii, diff * lgf, -diff * lgb))
    pos = lax.broadcasted_iota(jnp.int32, (BLOCK, RET_DV), 0).astype(F32)
    dec_ref[0] = jnp.exp((pos + 1.0) * lgf)
    dec_ref[1] = jnp.exp((BLOCK - pos) * lgb)
    dec_ref[2] = jnp.exp((BLOCK - 1.0 - pos) * lgf)
    dec_ref[3] = jnp.exp(pos * lgb)
    tot_f = jnp.exp(jnp.full((1, RET_DV), BLOCK * lgf, F32))
    tot_b = jnp.exp(jnp.full((1, RET_DV), BLOCK * lgb, F32))

    if has_s0:
        sf_ref[...] = s0_ref[0]
        sb_ref[...] = s0_ref[1]
    else:
        sf_ref[...] = jnp.zeros_like(sf_ref)
        sb_ref[...] = jnp.zeros_like(sb_ref)

    def fwd(c, carry):
        r = _rows(c)
        sfs_ref[c] = sf_ref[...].astype(BF16)
        kd = ks_ref[r, :].astype(F32) * dec_ref[2]
        sf_ref[...] = sf_ref[...] * tot_f + _bdot_tn(kd, v_ref[r, :])
        return carry

    lax.fori_loop(0, n_chunks, fwd, 0, unroll=min(4, n_chunks))
    if emit_state:
        sfin_ref[0] = sf_ref[...]

    def bwd(t, carry):
        c = n_chunks - 1 - t
        r = _rows(c)
        q = qs_ref[r, :]
        k = ks_ref[r, :]
        v = v_ref[r, :].astype(BF16)
        p = _bdot_nt(q, k) * dm_ref[...]
        o = _bdot(p, v)
        o = o + _bdot(q, sfs_ref[c]) * dec_ref[0]
        o = o + _bdot(q, sb_ref[...]) * dec_ref[1]
        ms = jnp.mean(o * o, axis=-1, keepdims=True)
        y = o * lax.rsqrt(ms + EPS) * gain_ref[...]
        o_ref[r, :] = (y * _silu(gt_ref[r, :].astype(F32))).astype(o_ref.dtype)
        kd = k.astype(F32) * dec_ref[3]
        sb_ref[...] = sb_ref[...] * tot_b + _bdot_tn(kd, v)
        return carry

    lax.fori_loop(0, n_chunks, bwd, 0, unroll=min(4, n_chunks))
    if emit_state:
        sfin_ref[1] = sb_ref[...]


def _fill_alias(fill, in_specs, args):
    aliases = {}
    for k, arr in enumerate(fill or ()):
        in_specs.append(pl.BlockSpec(memory_space=pl.ANY))
        args.append(arr)
        aliases[len(args) - 1] = k
    return aliases


def _retention(proj, lg, ret_norm, seq_len, n_seq, row_block0, ropes=None, s0=None, emit_state=False, fill=None):
    n_chunks = seq_len // BLOCK
    rope, has_s0 = ropes is not None, s0 is not None

    def col(cb):
        return pl.BlockSpec((seq_len, RET_DK), lambda b, h, cb=cb: (b + row_block0, cb * RET_HEADS + h))

    in_specs = [pl.BlockSpec(memory_space=pltpu.SMEM), col(0), col(1), col(2), col(3)]
    args = [lg, proj, proj, proj, proj]
    if rope:
        in_specs += [pl.BlockSpec((seq_len, RET_DK // 2), lambda b, h: (0, 0))] * 2
        args += list(ropes)
    if has_s0:
        in_specs.append(pl.BlockSpec((None, None, 2, None, RET_DK, RET_DV), lambda b, h: (b, 0, 0, h, 0, 0)))
        args.append(s0)
    in_specs.append(pl.BlockSpec((1, RET_DV), lambda b, h: (0, h)))
    args.append(ret_norm.reshape(1, RET_HEADS * RET_DV))
    aliases = _fill_alias(fill, in_specs, args)
    out_specs = [pl.BlockSpec((seq_len, RET_DV), lambda b, h: (b + row_block0, h))]
    out_shape = [jax.ShapeDtypeStruct((T_ALL, RET_HEADS * RET_DV), BF16)]
    if emit_state:
        out_specs.append(pl.BlockSpec((None, None, 2, None, RET_DK, RET_DV), lambda b, h: (b, 0, 0, h, 0, 0)))
        out_shape.append(jax.ShapeDtypeStruct((n_seq, 1, 2, RET_HEADS, RET_DK, RET_DV), F32))
    return pl.pallas_call(
        functools.partial(_ret_kernel, n_chunks, rope, has_s0, emit_state, bool(fill)),
        grid=(n_seq, RET_HEADS),
        in_specs=in_specs,
        out_specs=out_specs,
        out_shape=out_shape,
        input_output_aliases=aliases,
        scratch_shapes=[pltpu.VMEM((seq_len, RET_DK), BF16), pltpu.VMEM((seq_len, RET_DK), BF16),
                        pltpu.VMEM((n_chunks, RET_DK, RET_DV), BF16), pltpu.VMEM((BLOCK, BLOCK), F32),
                        pltpu.VMEM((4, BLOCK, RET_DV), F32), pltpu.VMEM((RET_DK, RET_DV), F32),
                        pltpu.VMEM((RET_DK, RET_DV), F32)],
        compiler_params=_cparams("arbitrary", "arbitrary"),
        name="retention_latent" if rope else "retention_prompt",
    )(*args)


def _head_norm(x, gain):
    x = x.astype(F32)
    return x * lax.rsqrt(jnp.mean(x * x, axis=-1, keepdims=True) + EPS) * gain


def _rope_full(x, cs, sn):
    return x * cs + pltpu.roll(x, ATT_HD // 2, 1) * sn


ATT_SPLIT = 2


def _sink_column(sink_ref, head0, n_heads, rows_per_head):
    n = n_heads * rows_per_head
    head = lax.broadcasted_iota(jnp.int32, (n, 1), 0) // rows_per_head
    col = jnp.full((n, 1), sink_ref[head0], F32)
    for g in range(1, n_heads):
        col = jnp.where(head == g, sink_ref[head0 + g], col)
    return col


def _att_latent_kernel(sink_ref, q_ref, k_ref, v_ref, ck_ref, cv_ref, cos_ref, sin_ref, qn_ref, kn_ref,
                       fill_ref, o_ref, kp_ref, vp_ref, ckp_ref, cvp_ref):
    del fill_ref
    kh, qb = pl.program_id(1), pl.program_id(2)
    n_chunks = DEC_SEQ // BLOCK
    loc = 3 * BLOCK

    @pl.when(qb == 0)
    def _():
        def prep(c, carry):
            r = _rows(c)
            kp_ref[r, :] = _rope_full(_head_norm(k_ref[r, :], kn_ref[...]), cos_ref[r, :], sin_ref[r, :]).astype(BF16)
            vp_ref[r, :ATT_HD] = v_ref[r, :].astype(BF16)
            vp_ref[r, ATT_HD:] = jnp.ones((BLOCK, ATT_HD), BF16)
            return carry

        lax.fori_loop(0, n_chunks, prep, 0)
        ckp_ref[...] = ck_ref[...].astype(BF16)
        cvp_ref[:, :ATT_HD] = cv_ref[...].astype(BF16)
        cvp_ref[:, ATT_HD:] = jnp.ones((PAST_LEN, ATT_HD), BF16)

    rq = _rows(qb)
    cs, sn = cos_ref[rq, :], sin_ref[rq, :]
    qg = qn_ref[...] * (ATT_HD ** -0.5)
    start = pl.multiple_of(jnp.clip((qb - 1) * BLOCK, 0, DEC_SEQ - loc), BLOCK)
    kl, vl = kp_ref[pl.ds(start, loc), :], vp_ref[pl.ds(start, loc), :]
    qpos = qb * BLOCK + (lax.broadcasted_iota(jnp.int32, (ATT_SPLIT * BLOCK, loc), 0) & (BLOCK - 1))
    kpos = start + lax.broadcasted_iota(jnp.int32, (ATT_SPLIT * BLOCK, loc), 1)
    in_window = jnp.abs(qpos - kpos) <= WINDOW
    for g0 in range(0, ATT_GROUP, ATT_SPLIT):
        q = jnp.concatenate(
            [_rope_full(_head_norm(q_ref[:, g * ATT_HD:(g + 1) * ATT_HD], qg), cs, sn).astype(BF16)
             for g in range(g0, g0 + ATT_SPLIT)], axis=0)
        s_loc = jnp.where(in_window, _bdot_nt(q, kl), NEG_INF)
        s_ctx = _bdot_nt(q, ckp_ref[...])
        sink = _sink_column(sink_ref, kh * ATT_GROUP + g0, ATT_SPLIT, BLOCK)
        m = jnp.maximum(jnp.maximum(jnp.max(s_loc, axis=-1, keepdims=True),
                                    jnp.max(s_ctx, axis=-1, keepdims=True)), sink)
        pv = _bdot(jnp.exp(s_loc - m), vl) + _bdot(jnp.exp(s_ctx - m), cvp_ref[...])
        o = pv[:, :ATT_HD] / (pv[:, ATT_HD:] + jnp.exp(sink - m))
        for g in range(ATT_SPLIT):
            o_ref[:, (g0 + g) * ATT_HD:(g0 + g + 1) * ATT_HD] = o[g * BLOCK:(g + 1) * BLOCK, :].astype(o_ref.dtype)


def _attention_latent(proj, cache_k, cache_v, sink, cos_a, sin_a, q_norm, k_norm, fill):
    nqb = DEC_SEQ // BLOCK
    rb0 = T_PROMPT // DEC_SEQ
    qcol0 = 4 * RET_HEADS * RET_DK // (ATT_GROUP * ATT_HD)
    kcol0 = (4 * RET_HEADS * RET_DK + ATT_HEADS * ATT_HD) // ATT_HD
    vcol0 = kcol0 + ATT_KV_HEADS
    ck = cache_k.reshape(DEC_BATCH, PAST_LEN, ATT_KV_HEADS * ATT_HD)
    cv = cache_v.reshape(DEC_BATCH, PAST_LEN, ATT_KV_HEADS * ATT_HD)
    return pl.pallas_call(
        _att_latent_kernel,
        grid=(DEC_BATCH, ATT_KV_HEADS, nqb),
        in_specs=[pl.BlockSpec(memory_space=pltpu.SMEM),
                  pl.BlockSpec((BLOCK, ATT_GROUP * ATT_HD),
                               lambda b, kh, qb: (T_PROMPT // BLOCK + b * nqb + qb, qcol0 + kh)),
                  pl.BlockSpec((DEC_SEQ, ATT_HD), lambda b, kh, qb: (rb0 + b, kcol0 + kh)),
                  pl.BlockSpec((DEC_SEQ, ATT_HD), lambda b, kh, qb: (rb0 + b, vcol0 + kh)),
                  pl.BlockSpec((None, PAST_LEN, ATT_HD), lambda b, kh, qb: (b, 0, kh)),
                  pl.BlockSpec((None, PAST_LEN, ATT_HD), lambda b, kh, qb: (b, 0, kh)),
                  pl.BlockSpec((DEC_SEQ, ATT_HD), lambda b, kh, qb: (0, 0)),
                  pl.BlockSpec((DEC_SEQ, ATT_HD), lambda b, kh, qb: (0, 0)),
                  pl.BlockSpec((1, ATT_HD), lambda b, kh, qb: (0, 0)),
                  pl.BlockSpec((1, ATT_HD), lambda b, kh, qb: (0, 0)),
                  pl.BlockSpec(memory_space=pl.ANY)],
        out_specs=pl.BlockSpec((BLOCK, ATT_GROUP * ATT_HD),
                               lambda b, kh, qb: (T_PROMPT // BLOCK + b * nqb + qb, kh)),
        out_shape=jax.ShapeDtypeStruct((T_ALL, ATT_HEADS * ATT_HD), BF16),
        input_output_aliases={10: 0},
        scratch_shapes=[pltpu.VMEM((DEC_SEQ, ATT_HD), BF16), pltpu.VMEM((DEC_SEQ, 2 * ATT_HD), BF16),
                        pltpu.VMEM((PAST_LEN, ATT_HD), BF16), pltpu.VMEM((PAST_LEN, 2 * ATT_HD), BF16)],
        compiler_params=_cparams("arbitrary", "arbitrary", "arbitrary"),
        name="attention_latent",
    )(sink, proj, proj, proj, ck, cv, cos_a, sin_a, q_norm.reshape(1, ATT_HD), k_norm.reshape(1, ATT_HD), fill)


def _att_prompt_kernel(sink_ref, q_ref, k_ref, v_ref, qn_ref, kn_ref, o_ref, nk_ref, nv_ref):
    kh = pl.program_id(1)
    kn = _head_norm(k_ref[...], kn_ref[...])
    v = v_ref[...]
    nk_ref[...] = kn
    nv_ref[...] = v.astype(F32)
    qg = qn_ref[...] * (ATT_HD ** -0.5)
    q = jnp.concatenate([_head_norm(q_ref[:, g * ATT_HD:(g + 1) * ATT_HD], qg).astype(BF16)
                         for g in range(ATT_GROUP)], axis=0)
    s = _bdot_nt(q, kn)
    sink = _sink_column(sink_ref, kh * ATT_GROUP, ATT_GROUP, SEQ)
    m = jnp.maximum(jnp.max(s, axis=-1, keepdims=True), sink)
    pv = _bdot(jnp.exp(s - m), jnp.concatenate([v.astype(BF16), jnp.ones((SEQ, ATT_HD), BF16)], axis=1))
    o = pv[:, :ATT_HD] / (pv[:, ATT_HD:] + jnp.exp(sink - m))
    for g in range(ATT_GROUP):
        o_ref[:, g * ATT_HD:(g + 1) * ATT_HD] = o[g * SEQ:(g + 1) * SEQ, :].astype(o_ref.dtype)


def _attention_prompt(proj, sink, q_norm, k_norm):
    qcol0 = 4 * RET_HEADS * RET_DK // (ATT_GROUP * ATT_HD)
    kcol0 = (4 * RET_HEADS * RET_DK + ATT_HEADS * ATT_HD) // ATT_HD
    vcol0 = kcol0 + ATT_KV_HEADS
    kv_spec = pl.BlockSpec((None, SEQ, ATT_HD), lambda b, kh: (b, 0, kh))
    kv_shape = jax.ShapeDtypeStruct((BATCH, SEQ, ATT_KV_HEADS * ATT_HD), F32)
    return pl.pallas_call(
        _att_prompt_kernel,
        grid=(BATCH, ATT_KV_HEADS),
        in_specs=[pl.BlockSpec(memory_space=pltpu.SMEM),
                  pl.BlockSpec((SEQ, ATT_GROUP * ATT_HD), lambda b, kh: (b, qcol0 + kh)),
                  pl.BlockSpec((SEQ, ATT_HD), lambda b, kh: (b, kcol0 + kh)),
                  pl.BlockSpec((SEQ, ATT_HD), lambda b, kh: (b, vcol0 + kh)),
                  pl.BlockSpec((1, ATT_HD), lambda b, kh: (0, 0)),
                  pl.BlockSpec((1, ATT_HD), lambda b, kh: (0, 0))],
        out_specs=[pl.BlockSpec((SEQ, ATT_GROUP * ATT_HD), lambda b, kh: (b, kh)), kv_spec, kv_spec],
        out_shape=[jax.ShapeDtypeStruct((T_ALL, ATT_HEADS * ATT_HD), BF16), kv_shape, kv_shape],
        compiler_params=_cparams("arbitrary", "arbitrary"),
        name="attention_prompt",
    )(sink, proj, proj, proj, q_norm.reshape(1, ATT_HD), k_norm.reshape(1, ATT_HD))


def _split_dot(m01, a):
    hi = a.astype(BF16)
    r1 = a - hi.astype(F32)
    mid = r1.astype(BF16)
    lo = (r1 - mid.astype(F32)).astype(BF16)
    return (jnp.dot(m01, hi, preferred_element_type=F32) + jnp.dot(m01, mid, preferred_element_type=F32)
            + jnp.dot(m01, lo, preferred_element_type=F32))


PREP_CHUNKS = ROW_TILE // BLOCK


def _ssd_prep_kernel(raw_ref, bias_ref, alog_ref, cum_ref, dt_ref, w_ref, tot_ref, ecum_ref):
    ii = lax.broadcasted_iota(jnp.int32, (BLOCK, BLOCK), 0)
    jj = lax.broadcasted_iota(jnp.int32, (BLOCK, BLOCK), 1)
    lower = jnp.where(jj <= ii, 1.0, 0.0).astype(BF16)
    upper = jnp.where(jj >= ii, 1.0, 0.0).astype(BF16)
    fwd_lane = lax.broadcasted_iota(jnp.int32, (BLOCK, LANES), 1) < SSD_HEADS
    neg_a = -jnp.exp(alog_ref[...])

    def chunk(k, carry):
        x = raw_ref[_rows(k), :] + bias_ref[...]
        dt = jnp.maximum(x, 0.0) + jnp.log1p(jnp.exp(-jnp.abs(x)))
        a = dt * neg_a
        incl = _split_dot(lower, a)
        rincl = _split_dot(upper, a)
        cum = jnp.where(fwd_lane, incl, rincl)
        tot = jnp.where(fwd_lane[:1], incl[BLOCK - 1:BLOCK, :], rincl[0:1, :])
        cum_ref[k] = cum.T
        dt_ref[k] = dt.T
        w_ref[k] = (dt * jnp.exp(tot - cum)).T
        tot_ref[k] = jnp.broadcast_to(jnp.exp(tot), (BLOCK, LANES)).T
        ecum_ref[k] = jnp.exp(cum).T
        return carry

    lax.fori_loop(0, PREP_CHUNKS, chunk, 0)


def _ssd_prep(dt_raw, dt_bias, a_log):
    nc = T_ALL // BLOCK
    spec = pl.BlockSpec((PREP_CHUNKS, 2 * SSD_HEADS, BLOCK), lambda c: (c, 0, 0))
    shape = jax.ShapeDtypeStruct((nc, 2 * SSD_HEADS, BLOCK), F32)
    return pl.pallas_call(
        _ssd_prep_kernel,
        grid=(nc // PREP_CHUNKS,),
        in_specs=[pl.BlockSpec((ROW_TILE, 2 * SSD_HEADS), lambda c: (c, 0)),
                  pl.BlockSpec((1, 2 * SSD_HEADS), lambda c: (0, 0)),
                  pl.BlockSpec((1, 2 * SSD_HEADS), lambda c: (0, 0))],
        out_specs=[spec] * 5,
        out_shape=[shape] * 5,
        compiler_params=_cparams("arbitrary"),
        name="ssd_prep",
    )(dt_raw, dt_bias.reshape(1, 2 * SSD_HEADS), a_log.reshape(1, 2 * SSD_HEADS))


GW = SSD_R * SSD_P
HALO = 8


def _pair_tiles(per_head):
    low = lax.broadcasted_iota(jnp.int32, per_head[0].shape, 1) < SSD_P
    return jnp.concatenate([jnp.where(low, per_head[2 * t], per_head[2 * t + 1]) for t in range(SSD_R // 2)],
                           axis=1)


def _row_bcast(ref, c, r):
    return jnp.broadcast_to(ref[c, r:r + 1, :], (BLOCK, BLOCK))


def _pair_cols(ref, c):
    top = lax.broadcasted_iota(jnp.int32, (BLOCK, BLOCK), 0) < SSD_P
    return jnp.concatenate(
        [jnp.where(top, _row_bcast(ref, c, 2 * t), _row_bcast(ref, c, 2 * t + 1)).T for t in range(SSD_R // 2)],
        axis=1)


def _ssd_kernel(n_chunks, has_s0, emit_state, has_fill, *refs):
    it = iter(refs)
    z_ref, x_ref, b_ref, c_ref = (next(it) for _ in range(4))
    cumf_ref, cumb_ref, dtf_ref, dtb_ref, wf_ref, wb_ref, totf_ref, totb_ref, ecf_ref, ecb_ref = (
        next(it) for _ in range(10))
    cwx_ref, cwb_ref, cwc_ref, cbx_ref, cbb_ref, cbc_ref, d_ref, ng_ref = (next(it) for _ in range(8))
    s0_ref = next(it) if has_s0 else None
    if has_fill:
        next(it), next(it)
    o_ref, ss_ref = next(it), next(it)
    sfin_ref = next(it) if emit_state else None
    pad_ref, xc_ref, bc_ref, cc_ref, sfs_ref, sf_ref, sb_ref = (next(it) for _ in range(7))
    seq_len = n_chunks * BLOCK

    pad_ref[0:HALO, :] = jnp.zeros((HALO, GW + 2 * SSD_N), F32)
    pad_ref[HALO + seq_len:2 * HALO + seq_len, :] = jnp.zeros((HALO, GW + 2 * SSD_N), F32)

    def fill(c, carry):
        dst = pl.ds(pl.multiple_of(c * BLOCK, BLOCK) + HALO, BLOCK)
        r = _rows(c)
        pad_ref[dst, 0:GW] = x_ref[r, :].astype(F32)
        pad_ref[dst, GW:GW + SSD_N] = b_ref[r, :].astype(F32)
        pad_ref[dst, GW + SSD_N:GW + 2 * SSD_N] = c_ref[r, :].astype(F32)
        return carry

    lax.fori_loop(0, n_chunks, fill, 0)

    def conv(c):
        r = _rows(c)
        src = pl.ds(pl.multiple_of(c * BLOCK, BLOCK), BLOCK + 2 * HALO)
        for col0, width, cw_ref, cb_ref, dst in ((0, GW, cwx_ref, cbx_ref, xc_ref),
                                                 (GW, SSD_N, cwb_ref, cbb_ref, bc_ref),
                                                 (GW + SSD_N, SSD_N, cwc_ref, cbc_ref, cc_ref)):
            for t in range(width // LANES):
                tl = slice(t * LANES, (t + 1) * LANES)
                win = pad_ref[src, col0 + t * LANES:col0 + (t + 1) * LANES]
                acc = jnp.broadcast_to(cb_ref[:, tl], (BLOCK, LANES))
                for w in range(CONV_W):
                    off = HALO - CONV_W // 2 + w
                    acc = acc + win[off:off + BLOCK, :] * cw_ref[w:w + 1, tl]
                dst[r, tl] = _silu(acc).astype(dst.dtype)

    if has_s0:
        for r in range(SSD_R):
            sf_ref[:, r * SSD_P:(r + 1) * SSD_P] = s0_ref[0, r]
            sb_ref[:, r * SSD_P:(r + 1) * SSD_P] = s0_ref[1, r]
    else:
        sf_ref[...] = jnp.zeros_like(sf_ref)
        sb_ref[...] = jnp.zeros_like(sb_ref)

    def state_update(s_ref, c, w_ref, tot_ref, bm, xs):
        tot = _pair_tiles([tot_ref[c, r:r + 1, :] for r in range(SSD_R)])
        s_ref[...] = s_ref[...] * tot + _bdot_tn(bm, xs * _pair_cols(w_ref, c))

    def fwd_step(c):
        r = _rows(c)
        sfs_ref[c] = sf_ref[...].astype(BF16)
        state_update(sf_ref, c, wf_ref, totf_ref, bc_ref[r, :], xc_ref[r, :])

    def fwd(c, carry):
        fwd_step(c)
        conv(c + 1)
        return carry

    conv(0)
    lax.fori_loop(0, n_chunks - 1, fwd, 0, unroll=3 if (n_chunks - 1) % 3 == 0 else 1)
    fwd_step(n_chunks - 1)
    if emit_state:
        for r in range(SSD_R):
            sfin_ref[0, r] = sf_ref[:, r * SSD_P:(r + 1) * SSD_P]

    ii = lax.broadcasted_iota(jnp.int32, (BLOCK, BLOCK), 0)
    jj = lax.broadcasted_iota(jnp.int32, (BLOCK, BLOCK), 1)
    causal = jj <= ii
    anti = ii <= jj
    low = lax.broadcasted_iota(jnp.int32, (BLOCK, LANES), 1) < SSD_P

    def bwd(t, carry):
        c = n_chunks - 1 - t
        r = _rows(c)
        cm, bm, xs = cc_ref[r, :], bc_ref[r, :], xc_ref[r, :]
        xb = xs.astype(BF16)
        sc = _bdot_nt(cm, bm)
        yf = _bdot(cm, sfs_ref[c])
        yb = _bdot(cm, sb_ref[...])
        tiles = []
        for t2 in range(SSD_R // 2):
            xt = xb[:, t2 * LANES:(t2 + 1) * LANES]
            acc = None
            for hh in range(2):
                hr = 2 * t2 + hh
                rf, rb = _row_bcast(cumf_ref, c, hr), _row_bcast(cumb_ref, c, hr)
                e = jnp.exp(jnp.where(anti, rf, rb).T - jnp.where(causal, rf, rb))
                e = e * jnp.where(causal, _row_bcast(dtf_ref, c, hr), _row_bcast(dtb_ref, c, hr))
                xh = jnp.where(low, xt, 0.0) if hh == 0 else jnp.where(low, 0.0, xt)
                part = _bdot(sc * e, xh)
                acc = part if acc is None else acc + part
            tiles.append(acc)
        y = (jnp.concatenate(tiles, axis=1) + _pair_cols(ecf_ref, c) * yf + _pair_cols(ecb_ref, c) * yb
             + d_ref[...] * xs)
        yg = y * _silu(z_ref[r, :].astype(F32))
        ss_ref[r, :] = jnp.broadcast_to(jnp.sum(yg * yg, axis=-1, keepdims=True), (BLOCK, LANES))
        o_ref[r, :] = (yg * ng_ref[...]).astype(o_ref.dtype)
        state_update(sb_ref, c, wb_ref, totb_ref, bm, xs)
        return carry

    lax.fori_loop(0, n_chunks, bwd, 0, unroll=min(4, n_chunks))
    if emit_state:
        for r in range(SSD_R):
            sfin_ref[1, r] = sb_ref[:, r * SSD_P:(r + 1) * SSD_P]


def _ssd_scan(zx, prep, conv_w, conv_b, d_exp, norm_gain, seq_len, n_seq, row_block0, s0=None, emit_state=False,
              fill=None):
    n_chunks = seq_len // BLOCK
    has_s0 = s0 is not None
    xcol0 = D_INNER // GW
    bcol0 = 2 * D_INNER // SSD_N
    ccol0 = bcol0 + SSD_GROUPS
    cwb0 = D_INNER // SSD_N

    def rowcol(width, col0):
        return pl.BlockSpec((seq_len, width), lambda b, g, col0=col0: (b + row_block0, col0 + g))

    def headrows(direction):
        return pl.BlockSpec((n_chunks, SSD_R, BLOCK),
                            lambda b, g, direction=direction: (b + row_block0, direction * SSD_GROUPS + g, 0))

    in_specs = [rowcol(GW, 0), rowcol(GW, xcol0), rowcol(SSD_N, bcol0), rowcol(SSD_N, ccol0)]
    args = [zx, zx, zx, zx]
    for arr in prep:
        in_specs += [headrows(0), headrows(1)]
        args += [arr, arr]
    in_specs += [pl.BlockSpec((CONV_W, GW), lambda b, g: (0, g)),
                 pl.BlockSpec((CONV_W, SSD_N), lambda b, g: (0, cwb0 + g)),
                 pl.BlockSpec((CONV_W, SSD_N), lambda b, g: (0, cwb0 + SSD_GROUPS + g)),
                 pl.BlockSpec((1, GW), lambda b, g: (0, g)),
                 pl.BlockSpec((1, SSD_N), lambda b, g: (0, cwb0 + g)),
                 pl.BlockSpec((1, SSD_N), lambda b, g: (0, cwb0 + SSD_GROUPS + g)),
                 pl.BlockSpec((1, GW), lambda b, g: (0, g)),
                 pl.BlockSpec((1, GW), lambda b, g: (0, g))]
    args += [conv_w, conv_w, conv_w, conv_b, conv_b, conv_b, d_exp, norm_gain.reshape(1, D_INNER)]
    state_spec = pl.BlockSpec((None, None, 2, SSD_R, SSD_N, SSD_P), lambda b, g: (b, 0, 0, g, 0, 0))
    if has_s0:
        in_specs.append(state_spec)
        args.append(s0)
    aliases = _fill_alias(fill, in_specs, args)
    out_specs = [pl.BlockSpec((seq_len, GW), lambda b, g: (b + row_block0, g)),
                 pl.BlockSpec((seq_len, LANES), lambda b, g: (b + row_block0, g))]
    out_shape = [jax.ShapeDtypeStruct((T_ALL, D_INNER), BF16),
                 jax.ShapeDtypeStruct((T_ALL, SSD_GROUPS * LANES), F32)]
    if emit_state:
        out_specs.append(state_spec)
        out_shape.append(jax.ShapeDtypeStruct((n_seq, 1, 2, SSD_HEADS, SSD_N, SSD_P), F32))
    return pl.pallas_call(
        functools.partial(_ssd_kernel, n_chunks, has_s0, emit_state, bool(fill)),
        grid=(n_seq, SSD_GROUPS),
        in_specs=in_specs,
        out_specs=out_specs,
        out_shape=out_shape,
        input_output_aliases=aliases,
        scratch_shapes=[pltpu.VMEM((seq_len + 2 * HALO, GW + 2 * SSD_N), F32),
                        pltpu.VMEM((seq_len, GW), F32), pltpu.VMEM((seq_len, SSD_N), BF16),
                        pltpu.VMEM((seq_len, SSD_N), BF16), pltpu.VMEM((n_chunks, SSD_N, GW), BF16),
                        pltpu.VMEM((SSD_N, GW), F32), pltpu.VMEM((SSD_N, GW), F32)],
        compiler_params=_cparams("arbitrary", "arbitrary"),
        name="ssd_scan_latent" if has_s0 else "ssd_scan_prompt",
    )(*args)


def _router_kernel(x_ref, g_ref, mod_ref, rw_ref, hn_ref, idx_ref, wgt_ref, hf_ref):
    rw = rw_ref[...]
    r_hi = rw.astype(BF16)
    r_lo = (rw - r_hi.astype(F32)).astype(BF16)
    r_both = jnp.concatenate([r_hi, r_lo], axis=1)

    def route(r):
        hf = hf_ref[r, :]
        h_hi = hf.astype(BF16)
        h_lo = (hf - h_hi.astype(F32)).astype(BF16)
        hn_ref[r, :] = h_hi
        t1 = jnp.dot(h_hi, r_both, preferred_element_type=F32)
        logits = t1[:, :LANES] + t1[:, LANES:] + jnp.dot(h_lo, r_hi, preferred_element_type=F32)
        lane = lax.broadcasted_iota(jnp.int32, logits.shape, 1)
        lg = jnp.where(lane < N_EXPERTS, logits, NEG_INF)
        m1 = jnp.max(lg, axis=-1, keepdims=True)
        i1 = jnp.min(jnp.where(lg == m1, lane, LANES), axis=-1, keepdims=True)
        lg2 = jnp.where(lane == i1, NEG_INF, lg)
        m2 = jnp.max(lg2, axis=-1, keepdims=True)
        i2 = jnp.min(jnp.where(lg2 == m2, lane, LANES), axis=-1, keepdims=True)
        e2 = jnp.exp(m2 - m1)
        w1 = 1.0 / (1.0 + e2)
        idx_ref[r, :] = jnp.where(lane == 0, i1, jnp.where(lane == 1, i2, 0))
        wgt_ref[r, :] = jnp.where(lane == 0, w1, jnp.where(lane == 1, e2 * w1, 0.0))

    _adaln_then(x_ref, g_ref, mod_ref, 3, 4, hf_ref, route)


def _router(x, gain, modt, router_w):
    t, d = x.shape
    rw = jnp.pad(router_w, ((0, 0), (0, LANES - N_EXPERTS)))
    return pl.pallas_call(
        _router_kernel,
        grid=(t // ROW_TILE,),
        in_specs=[pl.BlockSpec((ROW_TILE, d), lambda i: (i, 0)),
                  pl.BlockSpec((1, d), lambda i: (0, 0)),
                  pl.BlockSpec((None, 8, d), lambda i: (i, 0, 0)),
                  pl.BlockSpec((d, LANES), lambda i: (0, 0))],
        out_specs=[pl.BlockSpec((ROW_TILE, d), lambda i: (i, 0)),
                   pl.BlockSpec((ROW_TILE, LANES), lambda i: (i, 0)),
                   pl.BlockSpec((ROW_TILE, LANES), lambda i: (i, 0))],
        out_shape=[jax.ShapeDtypeStruct((t, d), BF16), jax.ShapeDtypeStruct((t, LANES), jnp.int32),
                   jax.ShapeDtypeStruct((t, LANES), F32)],
        scratch_shapes=[pltpu.VMEM((ROW_TILE, d), F32)],
        compiler_params=_cparams("arbitrary"),
        name="moe_router",
    )(x, gain.reshape(1, d), modt, rw)


DOWN_ROWS = 512
DOWN_TN = 1024


def _expert_changed(be_ref, blk, prev_blk, step):
    return jnp.logical_or(step == 0, be_ref[blk] != be_ref[prev_blk])


def _expert_up_kernel(be_ref, nu_ref, nv_ref, xs_ref, wg_ref, wu_ref, h_ref, wgb_ref, wub_ref):
    i = pl.program_id(1)
    used = nv_ref[i] > 0

    @pl.when(jnp.logical_and(used, _expert_changed(be_ref, i, jnp.maximum(i - 1, 0), i)))
    def _():
        wgb_ref[...] = wg_ref[...].astype(BF16)
        wub_ref[...] = wu_ref[...].astype(BF16)

    @pl.when(used)
    def _():
        xs = xs_ref[...]
        h_ref[...] = (_silu(_bdot(xs, wgb_ref[...])) * _bdot(xs, wub_ref[...])).astype(h_ref.dtype)

    @pl.when(jnp.logical_not(used))
    def _():
        h_ref[...] = jnp.zeros_like(h_ref)


def _expert_down_kernel(be_ref, nu_ref, nv_ref, h_ref, wd_ref, o_ref, wdb_ref):
    i = pl.program_id(1)
    per = MOE_ROWS // DOWN_ROWS
    blk = i // per
    used = nv_ref[blk] > (i % per) * DOWN_ROWS

    @pl.when(jnp.logical_and(nv_ref[blk] > 0, _expert_changed(be_ref, blk, jnp.maximum(i - 1, 0) // per, i)))
    def _():
        wdb_ref[...] = wd_ref[...].astype(BF16)

    @pl.when(used)
    def _():
        o_ref[...] = _bdot(h_ref[...], wdb_ref[...]).astype(o_ref.dtype)

    @pl.when(jnp.logical_not(used))
    def _():
        o_ref[...] = jnp.zeros_like(o_ref)


def _experts(xs_sorted, block_e, n_used, n_valid, wg, wu, wd):
    cap, d = xs_sorted.shape
    ff = wg.shape[2]
    per = MOE_ROWS // DOWN_ROWS

    def expert_of(blk, be, nu):
        return be[jnp.minimum(blk, jnp.maximum(nu[0] - 1, 0))]

    h = pl.pallas_call(
        _expert_up_kernel,
        grid_spec=pltpu.PrefetchScalarGridSpec(
            num_scalar_prefetch=3,
            grid=(ff // FFN_TF, cap // MOE_ROWS),
            in_specs=[pl.BlockSpec((MOE_ROWS, d), lambda f, i, be, nu, nv: (i, 0)),
                      pl.BlockSpec((None, d, FFN_TF), lambda f, i, be, nu, nv: (expert_of(i, be, nu), 0, f)),
                      pl.BlockSpec((None, d, FFN_TF), lambda f, i, be, nu, nv: (expert_of(i, be, nu), 0, f))],
            out_specs=pl.BlockSpec((MOE_ROWS, FFN_TF), lambda f, i, be, nu, nv: (i, f)),
            scratch_shapes=[pltpu.VMEM((d, FFN_TF), BF16), pltpu.VMEM((d, FFN_TF), BF16)],
        ),
        out_shape=jax.ShapeDtypeStruct((cap, ff), BF16),
        compiler_params=_cparams("arbitrary", "arbitrary"),
        name="moe_expert_up",
    )(block_e, n_used, n_valid, xs_sorted, wg, wu)
    return pl.pallas_call(
        _expert_down_kernel,
        grid_spec=pltpu.PrefetchScalarGridSpec(
            num_scalar_prefetch=3,
            grid=(d // DOWN_TN, cap // DOWN_ROWS),
            in_specs=[pl.BlockSpec((DOWN_ROWS, ff), lambda n, i, be, nu, nv: (i, 0)),
                      pl.BlockSpec((None, ff, DOWN_TN),
                                   lambda n, i, be, nu, nv: (expert_of(i // per, be, nu), 0, n),
                                   pipeline_mode=pl.Buffered(1))],
            out_specs=pl.BlockSpec((DOWN_ROWS, DOWN_TN), lambda n, i, be, nu, nv: (i, n)),
            scratch_shapes=[pltpu.VMEM((ff, DOWN_TN), BF16)],
        ),
        out_shape=jax.ShapeDtypeStruct((cap, d), BF16),
        compiler_params=_cparams("arbitrary", "arbitrary"),
        name="moe_expert_down",
    )(block_e, n_used, n_valid, h, wd)


def _combine_kernel(x_ref, g0_ref, g1_ref, w_ref, mod_ref, o_ref):
    w = w_ref[...]
    y = w[:, 0:1] * g0_ref[...].astype(F32) + w[:, 1:2] * g1_ref[...].astype(F32)
    o_ref[...] = x_ref[...] + mod_ref[5:6, :] * y


COMB_ROWS = ROW_TILE


def _combine(x, g, wgt, modt, row0, n_rows):
    t, d = x.shape
    b0 = row0 // COMB_ROWS
    per = ROW_TILE // COMB_ROWS
    return pl.pallas_call(
        _combine_kernel,
        grid=(n_rows // COMB_ROWS,),
        in_specs=[pl.BlockSpec((COMB_ROWS, d), lambda i: (i + b0, 0)),
                  pl.BlockSpec((COMB_ROWS, d), lambda i: (i + b0, 0)),
                  pl.BlockSpec((COMB_ROWS, d), lambda i: (i + b0 + t // COMB_ROWS, 0)),
                  pl.BlockSpec((COMB_ROWS, LANES), lambda i: (i + b0, 0)),
                  pl.BlockSpec((None, 8, d), lambda i: ((i + b0) // per, 0, 0))],
        out_specs=pl.BlockSpec((COMB_ROWS, d), lambda i: (i, 0)),
        out_shape=jax.ShapeDtypeStruct((n_rows, d), F32),
        compiler_params=_cparams("arbitrary"),
        name="moe_combine",
    )(x, g, g, wgt, modt)


def _moe(x, gain, modt, router_w, wg, wu, wd):
    t, d = x.shape
    hn, idx, wgt = _router(x, gain, modt, router_w)
    top_idx = idx[:, :TOP_K]
    n_slots = t * TOP_K
    flat_e = top_idx.reshape(-1)
    onehot = (flat_e[:, None] == jnp.arange(N_EXPERTS, dtype=jnp.int32)[None, :]).astype(jnp.int32)
    incl = jnp.cumsum(onehot, axis=0)
    counts = incl[-1]
    rank = jnp.sum((incl - onehot) * onehot, axis=1)
    padded = (counts + MOE_ROWS - 1) // MOE_ROWS * MOE_ROWS
    pend = jnp.cumsum(padded)
    pstart = pend - padded
    dest = pstart[flat_e] + rank
    n_blocks = n_slots // MOE_ROWS + N_EXPERTS
    cap = n_blocks * MOE_ROWS
    row_tok = (jnp.arange(cap, dtype=jnp.int32) % t).at[dest].set(
        jnp.arange(n_slots, dtype=jnp.int32) // TOP_K, unique_indices=True, mode='promise_in_bounds')
    block_e = jnp.clip(jnp.searchsorted(pend, jnp.arange(n_blocks, dtype=jnp.int32) * MOE_ROWS, side='right'),
                       0, N_EXPERTS - 1).astype(jnp.int32)
    n_used = (pend[-1:] // MOE_ROWS).astype(jnp.int32)
    blk = jnp.arange(n_blocks, dtype=jnp.int32)
    n_valid = jnp.where(blk < n_used[0],
                        jnp.clip((pstart + counts)[block_e] - blk * MOE_ROWS, 0, MOE_ROWS), 0).astype(jnp.int32)
    out = _experts(hn.at[row_tok].get(mode='promise_in_bounds'), block_e, n_used, n_valid, wg, wu, wd)
    g = out.at[dest.reshape(t, TOP_K).T.reshape(-1)].get(mode='promise_in_bounds', unique_indices=True)
    return _combine(x, g, wgt, modt, 0, T_PROMPT), _combine(x, g, wgt, modt, T_PROMPT, T_SAMPLE)


def _rope_tables(n_tokens, dim):
    n_rows = n_tokens // GRID_W
    row = jnp.repeat(jnp.arange(n_rows), GRID_W).astype(F32)
    col = jnp.tile(jnp.arange(GRID_W), n_rows).astype(F32)
    n_freq = dim // 4
    inv = ROPE_BASE ** (-jnp.arange(n_freq, dtype=F32) / n_freq)
    ang = jnp.concatenate([row[:, None] * inv, col[:, None] * inv], axis=-1)
    return jnp.cos(ang), jnp.sin(ang)


def kernel(x_prompt, x_sample, state_ret, cache_k, cache_v, state_ssd, c, c_ctx, ada_w, ada_b, norm_mix, norm_ffn, ev_w_in, ev_w_out, ret_decay_logit, ret_norm, att_q_norm, att_k_norm, att_sink, ffn_w_gate, ffn_w_up, ffn_w_down, ssd_w_in, ssd_conv_w, ssd_conv_b, ssd_a_log, ssd_dt_bias, ssd_d, ssd_norm, ssd_w_out, moe_router, moe_w_gate, moe_w_up, moe_w_down):
    d = D_MODEL
    x = (x_prompt.reshape(T_PROMPT, d), x_sample.reshape(T_SAMPLE, d))

    cvecs = jnp.concatenate([c_ctx[None, :], c, jnp.zeros((MOD_ROWS - 1 - DEC_BATCH, d), F32)], axis=0)
    mods = _modulation(cvecs, ada_w, ada_b).reshape(2, MOD_ROWS, 6, d)
    tiles_per_seq = DEC_SEQ // ROW_TILE
    tile_row = jnp.concatenate([jnp.zeros((T_PROMPT // ROW_TILE,), jnp.int32),
                                1 + jnp.arange(T_SAMPLE // ROW_TILE, dtype=jnp.int32) // tiles_per_seq])
    modt = jnp.pad(mods[:, tile_row], ((0, 0), (0, 0), (0, 2), (0, 0)))

    ev_w_in_b, ev_w_out_b = ev_w_in[0].astype(BF16), ev_w_out[0].astype(BF16)
    ffn_wg_b, ffn_wu_b, ffn_wd_b = ffn_w_gate[0].astype(BF16), ffn_w_up[0].astype(BF16), ffn_w_down[0].astype(BF16)
    ssd_w_in_b, ssd_w_out_b = ssd_w_in[0].astype(BF16), ssd_w_out[0].astype(BF16)

    proj, = _adaln_matmul(x, norm_mix[0], modt[0], 0, 1, ev_w_in_b, tn=512, out_dtype=BF16, name="even_in_proj")
    lg = jax.nn.log_sigmoid(ret_decay_logit[0].astype(F32))
    cos_r, sin_r = _rope_tables(DEC_SEQ, RET_DK)
    cos_a, sin_a = _rope_tables(DEC_SEQ, ATT_HD)
    cos_a2 = jnp.concatenate([cos_a, cos_a], axis=-1)
    sin_a2 = jnp.concatenate([-sin_a, sin_a], axis=-1)
    mix_ret, new_state_ret = _retention(proj, lg, ret_norm[0], SEQ, BATCH, 0, emit_state=True)
    mix_ret, = _retention(proj, lg, ret_norm[0], DEC_SEQ, DEC_BATCH, T_PROMPT // DEC_SEQ,
                          ropes=(cos_r, sin_r), s0=state_ret, fill=(mix_ret,))
    mix_att, new_k, new_v = _attention_prompt(proj, att_sink[0], att_q_norm[0], att_k_norm[0])
    mix_att = _attention_latent(proj, cache_k[:, 0], cache_v[:, 0], att_sink[0], cos_a2, sin_a2,
                                att_q_norm[0], att_k_norm[0], mix_att)
    x = _proj_residual([mix_ret, mix_att], ev_w_out_b, x, modt[0], 2, tn=1024, name="even_out_proj")
    h = _ffn_gateup(x, norm_ffn[0], modt[0], ffn_wg_b, ffn_wu_b)
    x = _proj_residual([h], ffn_wd_b, x, modt[0], 5, tn=512, name="ffn_down")

    zx, dt_raw = _adaln_matmul(x, norm_mix[1], modt[1], 0, 1, ssd_w_in_b, tn=1024, n_out=SSD_ZX,
                               tail=2 * SSD_HEADS, out_dtype=BF16, name="ssd_in_proj")
    prep = _ssd_prep(dt_raw, ssd_dt_bias[0], ssd_a_log[0])
    d_exp = jnp.repeat(ssd_d[0], SSD_P)[None, :]
    conv_b = ssd_conv_b[0][None, :]
    yg, yss, new_state_ssd = _ssd_scan(zx, prep, ssd_conv_w[0], conv_b, d_exp, ssd_norm[0], SEQ, BATCH, 0,
                                       emit_state=True)
    yg, yss = _ssd_scan(zx, prep, ssd_conv_w[0], conv_b, d_exp, ssd_norm[0], DEC_SEQ, DEC_BATCH,
                        T_PROMPT // DEC_SEQ, s0=state_ssd, fill=(yg, yss))
    x = _proj_residual([yg], ssd_w_out_b, x, modt[1], 2, tn=512, row_ss=yss, name="ssd_out_proj")
    y_p, y_s = _moe(x, norm_ffn[1], modt[1], moe_router[0], moe_w_gate[0], moe_w_up[0], moe_w_down[0])

    y_prompt = y_p.reshape(BATCH, SEQ, d)
    y_sample = y_s.reshape(DEC_BATCH, DEC_SEQ, d)
    new_cache_k = new_k.reshape(BATCH, 1, SEQ, ATT_KV_HEADS, ATT_HD)
    new_cache_v = new_v.reshape(BATCH, 1, SEQ, ATT_KV_HEADS, ATT_HD)
    return (y_prompt, y_sample, new_state_ret, new_cache_k, new_cache_v, new_state_ssd)
```

```python
import functools

import jax
import jax.numpy as jnp
from jax import lax
from jax.experimental import pallas as pl
from jax.experimental.pallas import tpu as pltpu

F32 = jnp.float32
BF16 = jnp.bfloat16

D_MODEL = 2048
BATCH = 16
SEQ = 256
DEC_BATCH = 8
DEC_SEQ = 2048
PAST_LEN = 512
GRID_W = 64
BLOCK = 128
WINDOW = 128
EPS = 1e-6
ROPE_BASE = 10000.0
RET_HEADS = 4
RET_DK = 256
RET_DV = 256
ATT_HEADS = 8
ATT_KV_HEADS = 2
ATT_HD = 128
ATT_GROUP = ATT_HEADS // ATT_KV_HEADS
EVEN_IN = 5632
D_INNER = 2 * D_MODEL
SSD_P = 64
SSD_HEADS = D_INNER // SSD_P
SSD_N = 128
SSD_GROUPS = 8
SSD_R = SSD_HEADS // SSD_GROUPS
CONV_W = 5
CONV_CH = D_INNER + 2 * SSD_GROUPS * SSD_N
SSD_ZX = D_INNER + CONV_CH
D_FF = 5632
N_EXPERTS = 8
TOP_K = 2

T_PROMPT = BATCH * SEQ
T_SAMPLE = DEC_BATCH * DEC_SEQ
T_ALL = T_PROMPT + T_SAMPLE

LANES = 128
ROW_TILE = 1024
VMEM_LIMIT = 56 * 1024 * 1024
MOE_ROWS = 1024
NEG_INF = float("-inf")


def _cparams(*sem):
    return pltpu.CompilerParams(dimension_semantics=sem, vmem_limit_bytes=VMEM_LIMIT)


def _silu(x):
    return x * jax.nn.sigmoid(x)


def _bdot(a, b):
    return jnp.dot(a.astype(BF16), b.astype(BF16), preferred_element_type=F32)


def _bdot_nt(a, b):
    return lax.dot_general(a.astype(BF16), b.astype(BF16), (((1,), (1,)), ((), ())),
                           preferred_element_type=F32)


def _bdot_tn(a, b):
    return lax.dot_general(a.astype(BF16), b.astype(BF16), (((0,), (0,)), ((), ())),
                           preferred_element_type=F32)


def _rows(c, n=BLOCK):
    return pl.ds(pl.multiple_of(c * n, n), n)


MOD_ROWS = 16
MOD_TN = 1024


def _mod_kernel(c_ref, w_ref, b_ref, o_ref):
    o_ref[...] = _bdot(_silu(c_ref[...]), w_ref[...]) + b_ref[...]


def _modulation(cvecs, ada_w, ada_b):
    depth, d, n = ada_w.shape
    return pl.pallas_call(
        _mod_kernel,
        grid=(depth, n // MOD_TN),
        in_specs=[pl.BlockSpec((MOD_ROWS, d), lambda l, j: (0, 0)),
                  pl.BlockSpec((None, d, MOD_TN), lambda l, j: (l, 0, j)),
                  pl.BlockSpec((None, 1, MOD_TN), lambda l, j: (l, 0, j))],
        out_specs=pl.BlockSpec((None, MOD_ROWS, MOD_TN), lambda l, j: (l, 0, j)),
        out_shape=jax.ShapeDtypeStruct((depth, MOD_ROWS, n), F32),
        compiler_params=_cparams("arbitrary", "arbitrary"),
        name="modulation",
    )(cvecs, ada_w, ada_b.reshape(depth, 1, n))


ADALN_CHUNK = 64
ADALN_DOT_ROWS = 256


def _adaln_then(x_ref, g_ref, mod_ref, shift_row, scale_row, hn_ref, consume):
    mult = g_ref[...] * (1.0 + mod_ref[scale_row:scale_row + 1, :])
    shift = mod_ref[shift_row:shift_row + 1, :]
    for r0 in range(0, x_ref.shape[0], ADALN_DOT_ROWS):
        for s0 in range(r0, r0 + ADALN_DOT_ROWS, ADALN_CHUNK):
            r = slice(s0, s0 + ADALN_CHUNK)
            x = x_ref[r, :]
            ms = jnp.mean(x * x, axis=-1, keepdims=True)
            hn_ref[r, :] = (x * lax.rsqrt(ms + EPS) * mult + shift).astype(hn_ref.dtype)
        consume(slice(r0, r0 + ADALN_DOT_ROWS))


PROMPT_TILES = T_PROMPT // ROW_TILE


def _x_rows_specs(x, cols, col_of_j):
    if not isinstance(x, tuple):
        return [pl.BlockSpec((ROW_TILE, cols), lambda i, j: (i, col_of_j(j)))], [x]
    return ([pl.BlockSpec((ROW_TILE, cols),
                          lambda i, j: (jnp.minimum(i, PROMPT_TILES - 1), jnp.where(i < PROMPT_TILES, col_of_j(j), 0))),
             pl.BlockSpec((ROW_TILE, cols),
                          lambda i, j: (jnp.maximum(i - PROMPT_TILES, 0), jnp.where(i < PROMPT_TILES, 0, col_of_j(j))))],
            list(x))


def _for_x_rows(x_refs, fn):
    if len(x_refs) == 1:
        fn(x_refs[0])
        return
    in_prompt = pl.program_id(0) < PROMPT_TILES
    pl.when(in_prompt)(lambda: fn(x_refs[0]))
    pl.when(jnp.logical_not(in_prompt))(lambda: fn(x_refs[1]))


def _adaln_mm_kernel(shift_row, scale_row, has_tail, n_x, *refs):
    x_refs, (g_ref, mod_ref, w_ref), refs = refs[:n_x], refs[n_x:n_x + 3], refs[n_x + 3:]
    if has_tail:
        wt_ref, o_ref, ot_ref, hn_ref = refs
    else:
        o_ref, hn_ref = refs

    def project(r):
        hn = hn_ref[r, :]
        o_ref[r, :] = _bdot(hn, w_ref[...]).astype(o_ref.dtype)
        if has_tail:
            ot_ref[r, :] = _bdot(hn, wt_ref[...])

    first = pl.program_id(1) == 0

    @pl.when(first)
    def _():
        _for_x_rows(x_refs, lambda x_ref: _adaln_then(x_ref, g_ref, mod_ref, shift_row, scale_row, hn_ref, project))

    @pl.when(jnp.logical_not(first))
    def _():
        o_ref[...] = _bdot(hn_ref[...], w_ref[...]).astype(o_ref.dtype)


def _adaln_matmul(x, gain, modt, shift_row, scale_row, w, tn, n_out=None, tail=0, out_dtype=F32, name="adaln_mm"):
    t, d = T_ALL, w.shape[0]
    n_out = w.shape[1] if n_out is None else n_out
    in_specs, args = _x_rows_specs(x, d, lambda j: 0)
    n_x = len(args)
    in_specs += [pl.BlockSpec((1, d), lambda i, j: (0, 0)),
                 pl.BlockSpec((None, 8, d), lambda i, j: (i, 0, 0)),
                 pl.BlockSpec((d, tn), lambda i, j: (0, j))]
    args += [gain.reshape(1, d), modt, w]
    out_specs = [pl.BlockSpec((ROW_TILE, tn), lambda i, j: (i, j))]
    out_shape = [jax.ShapeDtypeStruct((t, n_out), out_dtype)]
    if tail:
        in_specs.append(pl.BlockSpec((d, tail), lambda i, j: (0, n_out // tail)))
        args.append(w)
        out_specs.append(pl.BlockSpec((ROW_TILE, tail), lambda i, j: (i, 0)))
        out_shape.append(jax.ShapeDtypeStruct((t, tail), F32))
    return pl.pallas_call(
        functools.partial(_adaln_mm_kernel, shift_row, scale_row, bool(tail), n_x),
        grid=(t // ROW_TILE, n_out // tn),
        in_specs=in_specs,
        out_specs=out_specs,
        out_shape=out_shape,
        scratch_shapes=[pltpu.VMEM((ROW_TILE, d), BF16)],
        compiler_params=_cparams("arbitrary", "arbitrary"),
        name=name,
    )(*args)


FFN_TF = 512


def _gateup_kernel(x_ref, g_ref, mod_ref, wg_ref, wu_ref, h_ref, hn_ref):
    def gate_up(r):
        hn = hn_ref[r, :]
        h_ref[r, :] = (_silu(_bdot(hn, wg_ref[...])) * _bdot(hn, wu_ref[...])).astype(h_ref.dtype)

    first = pl.program_id(1) == 0

    @pl.when(first)
    def _():
        _adaln_then(x_ref, g_ref, mod_ref, 3, 4, hn_ref, gate_up)

    @pl.when(jnp.logical_not(first))
    def _():
        gate_up(slice(None))


def _ffn_gateup(x, gain, modt, wg, wu):
    t, d = x.shape
    ff = wg.shape[1]
    return pl.pallas_call(
        _gateup_kernel,
        grid=(t // ROW_TILE, ff // FFN_TF),
        in_specs=[pl.BlockSpec((ROW_TILE, d), lambda i, f: (i, 0)),
                  pl.BlockSpec((1, d), lambda i, f: (0, 0)),
                  pl.BlockSpec((None, 8, d), lambda i, f: (i, 0, 0)),
                  pl.BlockSpec((d, FFN_TF), lambda i, f: (0, f)),
                  pl.BlockSpec((d, FFN_TF), lambda i, f: (0, f))],
        out_specs=pl.BlockSpec((ROW_TILE, FFN_TF), lambda i, f: (i, f)),
        out_shape=jax.ShapeDtypeStruct((t, ff), BF16),
        scratch_shapes=[pltpu.VMEM((ROW_TILE, d), BF16)],
        compiler_params=_cparams("arbitrary", "arbitrary"),
        name="ffn_gateup",
    )(x, gain.reshape(1, d), modt, wg, wu)


def _proj_res_kernel(n_a, n_x, gate_row, norm, *refs):
    a_refs = refs[:n_a]
    w_refs = refs[n_a:2 * n_a]
    x_refs, mod_ref = refs[2 * n_a:2 * n_a + n_x], refs[2 * n_a + n_x]
    pos = 2 * n_a + n_x + 1
    o_ref = refs[pos + 1] if norm else refs[pos]
    acc = _bdot(a_refs[0][...], w_refs[0][...])
    for k in range(1, n_a):
        acc = acc + _bdot(a_refs[k][...], w_refs[k][...])
    if norm:
        ss_ref = refs[pos]
        k_total = sum(a.shape[1] for a in a_refs)
        ss = ss_ref[:, 0:LANES]
        for k in range(1, ss_ref.shape[1] // LANES):
            ss = ss + ss_ref[:, k * LANES:(k + 1) * LANES]
        rs = lax.rsqrt(ss * (1.0 / k_total) + EPS)
        acc = acc * jnp.concatenate([rs] * (acc.shape[1] // LANES), axis=1)
    upd = mod_ref[gate_row:gate_row + 1, :] * acc

    def finish(x_ref):
        o_ref[...] = x_ref[...] + upd

    _for_x_rows(x_refs, finish)


def _proj_residual(a_list, w, x, modt, gate_row, tn, row_ss=None, name="proj_res"):
    t, d = T_ALL, w.shape[1]
    n_a = len(a_list)
    norm = row_ss is not None
    in_specs, args, k0 = [], [], 0
    for a in a_list:
        in_specs.append(pl.BlockSpec((ROW_TILE, a.shape[1]), lambda i, j: (i, 0)))
        args.append(a)
    for a in a_list:
        ka = a.shape[1]
        assert k0 % ka == 0
        in_specs.append(pl.BlockSpec((ka, tn), lambda i, j, kb=k0 // ka: (kb, j)))
        args.append(w)
        k0 += ka
    x_specs, x_args = _x_rows_specs(x, tn, lambda j: j)
    in_specs += x_specs + [pl.BlockSpec((None, 8, tn), lambda i, j: (i, 0, j))]
    args += x_args + [modt]
    if norm:
        in_specs.append(pl.BlockSpec((ROW_TILE, row_ss.shape[1]), lambda i, j: (i, 0)))
        args.append(row_ss)
    return pl.pallas_call(
        functools.partial(_proj_res_kernel, n_a, len(x_args), gate_row, norm),
        grid=(t // ROW_TILE, d // tn),
        in_specs=in_specs,
        out_specs=pl.BlockSpec((ROW_TILE, tn), lambda i, j: (i, j)),
        out_shape=jax.ShapeDtypeStruct((t, d), F32),
        compiler_params=_cparams("arbitrary", "arbitrary"),
        name=name,
    )(*args)


def _ret_kernel(n_chunks, rope, has_s0, emit_state, has_fill, lg_ref, *refs):
    it = iter(refs)
    q_ref, k_ref, v_ref, gt_ref = next(it), next(it), next(it), next(it)
    cos_ref, sin_ref = (next(it), next(it)) if rope else (None, None)
    s0_ref = next(it) if has_s0 else None
    gain_ref = next(it)
    if has_fill:
        next(it)
    o_ref = next(it)
    sfin_ref = next(it) if emit_state else None
    qs_ref, ks_ref, sfs_ref, dm_ref, dec_ref, sf_ref, sb_ref = (next(it) for _ in range(7))

    h = pl.program_id(1)
    lgf = lg_ref[0, h]
    lgb = lg_ref[1, h]
    half = RET_DK // 2

    def prep(c, carry):
        r = _rows(c)
        q = q_ref[r, :].astype(F32)
        k = k_ref[r, :].astype(F32) * (RET_DK ** -0.5)
        if rope:
            cs, sn = cos_ref[r, :], sin_ref[r, :]
            for src, dst in ((q, qs_ref), (k, ks_ref)):
                x1, x2 = src[:, :half], src[:, half:]
                dst[r, :half] = (x1 * cs - x2 * sn).astype(BF16)
                dst[r, half:] = (x1 * sn + x2 * cs).astype(BF16)
        else:
            qs_ref[r, :] = q.astype(BF16)
            ks_ref[r, :] = k.astype(BF16)
        return carry

    lax.fori_loop(0, n_chunks, prep, 0)

    ii = lax.broadcasted_iota(jnp.int32, (BLOCK, BLOCK), 0)
    jj = lax.broadcasted_iota(jnp.int32, (BLOCK, BLOCK), 1)
    diff = (ii - jj).astype(F32)
    dm_ref[...] = jnp.exp(jnp.where(jj <= ii, diff * lgf, -diff * lgb))
    pos = lax.broadcasted_iota(jnp.int32, (BLOCK, RET_DV), 0).astype(F32)
    dec_ref[0] = jnp.exp((pos + 1.0) * lgf)
    dec_ref[1] = jnp.exp((BLOCK - pos) * lgb)
    dec_ref[2] = jnp.exp((BLOCK - 1.0 - pos) * lgf)
    dec_ref[3] = jnp.exp(pos * lgb)
    tot_f = jnp.exp(jnp.full((1, RET_DV), BLOCK * lgf, F32))
    tot_b = jnp.exp(jnp.full((1, RET_DV), BLOCK * lgb, F32))

    if has_s0:
        sf_ref[...] = s0_ref[0]
        sb_ref[...] = s0_ref[1]
    else:
        sf_ref[...] = jnp.zeros_like(sf_ref)
        sb_ref[...] = jnp.zeros_like(sb_ref)

    def fwd(c, carry):
        r = _rows(c)
        sfs_ref[c] = sf_ref[...].astype(BF16)
        kd = ks_ref[r, :].astype(F32) * dec_ref[2]
        sf_ref[...] = sf_ref[...] * tot_f + _bdot_tn(kd, v_ref[r, :])
        return carry

    lax.fori_loop(0, n_chunks, fwd, 0, unroll=min(4, n_chunks))
    if emit_state:
        sfin_ref[0] = sf_ref[...]

    def bwd(t, carry):
        c = n_chunks - 1 - t
        r = _rows(c)
        q = qs_ref[r, :]
        k = ks_ref[r, :]
        v = v_ref[r, :].astype(BF16)
        p = _bdot_nt(q, k) * dm_ref[...]
        o = _bdot(p, v)
        o = o + _bdot(q, sfs_ref[c]) * dec_ref[0]
        o = o + _bdot(q, sb_ref[...]) * dec_ref[1]
        ms = jnp.mean(o * o, axis=-1, keepdims=True)
        y = o * lax.rsqrt(ms + EPS) * gain_ref[...]
        o_ref[r, :] = (y * _silu(gt_ref[r, :].astype(F32))).astype(o_ref.dtype)
        kd = k.astype(F32) * dec_ref[3]
        sb_ref[...] = sb_ref[...] * tot_b + _bdot_tn(kd, v)
        return carry

    lax.fori_loop(0, n_chunks, bwd, 0, unroll=min(4, n_chunks))
    if emit_state:
        sfin_ref[1] = sb_ref[...]


def _fill_alias(fill, in_specs, args):
    aliases = {}
    for k, arr in enumerate(fill or ()):
        in_specs.append(pl.BlockSpec(memory_space=pl.ANY))
        args.append(arr)
        aliases[len(args) - 1] = k
    return aliases


def _retention(proj, lg, ret_norm, seq_len, n_seq, row_block0, ropes=None, s0=None, emit_state=False, fill=None):
    n_chunks = seq_len // BLOCK
    rope, has_s0 = ropes is not None, s0 is not None

    def col(cb):
        return pl.BlockSpec((seq_len, RET_DK), lambda b, h, cb=cb: (b + row_block0, cb * RET_HEADS + h))

    in_specs = [pl.BlockSpec(memory_space=pltpu.SMEM), col(0), col(1), col(2), col(3)]
    args = [lg, proj, proj, proj, proj]
    if rope:
        in_specs += [pl.BlockSpec((seq_len, RET_DK // 2), lambda b, h: (0, 0))] * 2
        args += list(ropes)
    if has_s0:
        in_specs.append(pl.BlockSpec((None, None, 2, None, RET_DK, RET_DV), lambda b, h: (b, 0, 0, h, 0, 0)))
        args.append(s0)
    in_specs.append(pl.BlockSpec((1, RET_DV), lambda b, h: (0, h)))
    args.append(ret_norm.reshape(1, RET_HEADS * RET_DV))
    aliases = _fill_alias(fill, in_specs, args)
    out_specs = [pl.BlockSpec((seq_len, RET_DV), lambda b, h: (b + row_block0, h))]
    out_shape = [jax.ShapeDtypeStruct((T_ALL, RET_HEADS * RET_DV), BF16)]
    if emit_state:
        out_specs.append(pl.BlockSpec((None, None, 2, None, RET_DK, RET_DV), lambda b, h: (b, 0, 0, h, 0, 0)))
        out_shape.append(jax.ShapeDtypeStruct((n_seq, 1, 2, RET_HEADS, RET_DK, RET_DV), F32))
    return pl.pallas_call(
        functools.partial(_ret_kernel, n_chunks, rope, has_s0, emit_state, bool(fill)),
        grid=(n_seq, RET_HEADS),
        in_specs=in_specs,
        out_specs=out_specs,
        out_shape=out_shape,
        input_output_aliases=aliases,
        scratch_shapes=[pltpu.VMEM((seq_len, RET_DK), BF16), pltpu.VMEM((seq_len, RET_DK), BF16),
                        pltpu.VMEM((n_chunks, RET_DK, RET_DV), BF16), pltpu.VMEM((BLOCK, BLOCK), F32),
                        pltpu.VMEM((4, BLOCK, RET_DV), F32), pltpu.VMEM((RET_DK, RET_DV), F32),
                        pltpu.VMEM((RET_DK, RET_DV), F32)],
        compiler_params=_cparams("arbitrary", "arbitrary"),
        name="retention_latent" if rope else "retention_prompt",
    )(*args)


def _head_norm(x, gain):
    x = x.astype(F32)
    return x * lax.rsqrt(jnp.mean(x * x, axis=-1, keepdims=True) + EPS) * gain


def _rope_full(x, cs, sn):
    return x * cs + pltpu.roll(x, ATT_HD // 2, 1) * sn


ATT_SPLIT = 2


def _sink_column(sink_ref, head0, n_heads, rows_per_head):
    n = n_heads * rows_per_head
    head = lax.broadcasted_iota(jnp.int32, (n, 1), 0) // rows_per_head
    col = jnp.full((n, 1), sink_ref[head0], F32)
    for g in range(1, n_heads):
        col = jnp.where(head == g, sink_ref[head0 + g], col)
    return col


def _att_latent_kernel(sink_ref, q_ref, k_ref, v_ref, ck_ref, cv_ref, cos_ref, sin_ref, qn_ref, kn_ref,
                       fill_ref, o_ref, kp_ref, vp_ref, ckp_ref, cvp_ref):
    del fill_ref
    kh, qb = pl.program_id(1), pl.program_id(2)
    n_chunks = DEC_SEQ // BLOCK
    loc = 3 * BLOCK

    @pl.when(qb == 0)
    def _():
        def prep(c, carry):
            r = _rows(c)
            kp_ref[r, :] = _rope_full(_head_norm(k_ref[r, :], kn_ref[...]), cos_ref[r, :], sin_ref[r, :]).astype(BF16)
            vp_ref[r, :ATT_HD] = v_ref[r, :].astype(BF16)
            vp_ref[r, ATT_HD:] = jnp.ones((BLOCK, ATT_HD), BF16)
            return carry

        lax.fori_loop(0, n_chunks, prep, 0)
        ckp_ref[...] = ck_ref[...].astype(BF16)
        cvp_ref[:, :ATT_HD] = cv_ref[...].astype(BF16)
        cvp_ref[:, ATT_HD:] = jnp.ones((PAST_LEN, ATT_HD), BF16)

    rq = _rows(qb)
    cs, sn = cos_ref[rq, :], sin_ref[rq, :]
    qg = qn_ref[...] * (ATT_HD ** -0.5)
    start = pl.multiple_of(jnp.clip((qb - 1) * BLOCK, 0, DEC_SEQ - loc), BLOCK)
    kl, vl = kp_ref[pl.ds(start, loc), :], vp_ref[pl.ds(start, loc), :]
    qpos = qb * BLOCK + (lax.broadcasted_iota(jnp.int32, (ATT_SPLIT * BLOCK, loc), 0) & (BLOCK - 1))
    kpos = start + lax.broadcasted_iota(jnp.int32, (ATT_SPLIT * BLOCK, loc), 1)
    in_window = jnp.abs(qpos - kpos) <= WINDOW
    for g0 in range(0, ATT_GROUP, ATT_SPLIT):
        q = jnp.concatenate(
            [_rope_full(_head_norm(q_ref[:, g * ATT_HD:(g + 1) * ATT_HD], qg), cs, sn).astype(BF16)
             for g in range(g0, g0 + ATT_SPLIT)], axis=0)
        s_loc = jnp.where(in_window, _bdot_nt(q, kl), NEG_INF)
        s_ctx = _bdot_nt(q, ckp_ref[...])
        sink = _sink_column(sink_ref, kh * ATT_GROUP + g0, ATT_SPLIT, BLOCK)
        m = jnp.maximum(jnp.maximum(jnp.max(s_loc, axis=-1, keepdims=True),
                                    jnp.max(s_ctx, axis=-1, keepdims=True)), sink)
        pv = _bdot(jnp.exp(s_loc - m), vl) + _bdot(jnp.exp(s_ctx - m), cvp_ref[...])
        o = pv[:, :ATT_HD] / (pv[:, ATT_HD:] + jnp.exp(sink - m))
        for g in range(ATT_SPLIT):
            o_ref[:, (g0 + g) * ATT_HD:(g0 + g + 1) * ATT_HD] = o[g * BLOCK:(g + 1) * BLOCK, :].astype(o_ref.dtype)


def _attention_latent(proj, cache_k, cache_v, sink, cos_a, sin_a, q_norm, k_norm, fill):
    nqb = DEC_SEQ // BLOCK
    rb0 = T_PROMPT // DEC_SEQ
    qcol0 = 4 * RET_HEADS * RET_DK // (ATT_GROUP * ATT_HD)
    kcol0 = (4 * RET_HEADS * RET_DK + ATT_HEADS * ATT_HD) // ATT_HD
    vcol0 = kcol0 + ATT_KV_HEADS
    ck = cache_k.reshape(DEC_BATCH, PAST_LEN, ATT_KV_HEADS * ATT_HD)
    cv = cache_v.reshape(DEC_BATCH, PAST_LEN, ATT_KV_HEADS * ATT_HD)
    return pl.pallas_call(
        _att_latent_kernel,
        grid=(DEC_BATCH, ATT_KV_HEADS, nqb),
        in_specs=[pl.BlockSpec(memory_space=pltpu.SMEM),
                  pl.BlockSpec((BLOCK, ATT_GROUP * ATT_HD),
                               lambda b, kh, qb: (T_PROMPT // BLOCK + b * nqb + qb, qcol0 + kh)),
                  pl.BlockSpec((DEC_SEQ, ATT_HD), lambda b, kh, qb: (rb0 + b, kcol0 + kh)),
                  pl.BlockSpec((DEC_SEQ, ATT_HD), lambda b, kh, qb: (rb0 + b, vcol0 + kh)),
                  pl.BlockSpec((None, PAST_LEN, ATT_HD), lambda b, kh, qb: (b, 0, kh)),
                  pl.BlockSpec((None, PAST_LEN, ATT_HD), lambda b, kh, qb: (b, 0, kh)),
                  pl.BlockSpec((DEC_SEQ, ATT_HD), lambda b, kh, qb: (0, 0)),
                  pl.BlockSpec((DEC_SEQ, ATT_HD), lambda b, kh, qb: (0, 0)),
                  pl.BlockSpec((1, ATT_HD), lambda b, kh, qb: (0, 0)),
                  pl.BlockSpec((1, ATT_HD), lambda b, kh, qb: (0, 0)),
                  pl.BlockSpec(memory_space=pl.ANY)],
        out_specs=pl.BlockSpec((BLOCK, ATT_GROUP * ATT_HD),
                               lambda b, kh, qb: (T_PROMPT // BLOCK + b * nqb + qb, kh)),
        out_shape=jax.ShapeDtypeStruct((T_ALL, ATT_HEADS * ATT_HD), BF16),
        input_output_aliases={10: 0},
        scratch_shapes=[pltpu.VMEM((DEC_SEQ, ATT_HD), BF16), pltpu.VMEM((DEC_SEQ, 2 * ATT_HD), BF16),
                        pltpu.VMEM((PAST_LEN, ATT_HD), BF16), pltpu.VMEM((PAST_LEN, 2 * ATT_HD), BF16)],
        compiler_params=_cparams("arbitrary", "arbitrary", "arbitrary"),
        name="attention_latent",
    )(sink, proj, proj, proj, ck, cv, cos_a, sin_a, q_norm.reshape(1, ATT_HD), k_norm.reshape(1, ATT_HD), fill)


def _att_prompt_kernel(sink_ref, q_ref, k_ref, v_ref, qn_ref, kn_ref, o_ref, nk_ref, nv_ref):
    kh = pl.program_id(1)
    kn = _head_norm(k_ref[...], kn_ref[...])
    v = v_ref[...]
    nk_ref[...] = kn
    nv_ref[...] = v.astype(F32)
    qg = qn_ref[...] * (ATT_HD ** -0.5)
    q = jnp.concatenate([_head_norm(q_ref[:, g * ATT_HD:(g + 1) * ATT_HD], qg).astype(BF16)
                         for g in range(ATT_GROUP)], axis=0)
    s = _bdot_nt(q, kn)
    sink = _sink_column(sink_ref, kh * ATT_GROUP, ATT_GROUP, SEQ)
    m = jnp.maximum(jnp.max(s, axis=-1, keepdims=True), sink)
    pv = _bdot(jnp.exp(s - m), jnp.concatenate([v.astype(BF16), jnp.ones((SEQ, ATT_HD), BF16)], axis=1))
    o = pv[:, :ATT_HD] / (pv[:, ATT_HD:] + jnp.exp(sink - m))
    for g in range(ATT_GROUP):
        o_ref[:, g * ATT_HD:(g + 1) * ATT_HD] = o[g * SEQ:(g + 1) * SEQ, :].astype(o_ref.dtype)


def _attention_prompt(proj, sink, q_norm, k_norm):
    qcol0 = 4 * RET_HEADS * RET_DK // (ATT_GROUP * ATT_HD)
    kcol0 = (4 * RET_HEADS * RET_DK + ATT_HEADS * ATT_HD) // ATT_HD
    vcol0 = kcol0 + ATT_KV_HEADS
    kv_spec = pl.BlockSpec((None, SEQ, ATT_HD), lambda b, kh: (b, 0, kh))
    kv_shape = jax.ShapeDtypeStruct((BATCH, SEQ, ATT_KV_HEADS * ATT_HD), F32)
    return pl.pallas_call(
        _att_prompt_kernel,
        grid=(BATCH, ATT_KV_HEADS),
        in_specs=[pl.BlockSpec(memory_space=pltpu.SMEM),
                  pl.BlockSpec((SEQ, ATT_GROUP * ATT_HD), lambda b, kh: (b, qcol0 + kh)),
                  pl.BlockSpec((SEQ, ATT_HD), lambda b, kh: (b, kcol0 + kh)),
                  pl.BlockSpec((SEQ, ATT_HD), lambda b, kh: (b, vcol0 + kh)),
                  pl.BlockSpec((1, ATT_HD), lambda b, kh: (0, 0)),
                  pl.BlockSpec((1, ATT_HD), lambda b, kh: (0, 0))],
        out_specs=[pl.BlockSpec((SEQ, ATT_GROUP * ATT_HD), lambda b, kh: (b, kh)), kv_spec, kv_spec],
        out_shape=[jax.ShapeDtypeStruct((T_ALL, ATT_HEADS * ATT_HD), BF16), kv_shape, kv_shape],
        compiler_params=_cparams("arbitrary", "arbitrary"),
        name="attention_prompt",
    )(sink, proj, proj, proj, q_norm.reshape(1, ATT_HD), k_norm.reshape(1, ATT_HD))


def _split_dot(m01, a):
    hi = a.astype(BF16)
    r1 = a - hi.astype(F32)
    mid = r1.astype(BF16)
    lo = (r1 - mid.astype(F32)).astype(BF16)
    return (jnp.dot(m01, hi, preferred_element_type=F32) + jnp.dot(m01, mid, preferred_element_type=F32)
            + jnp.dot(m01, lo, preferred_element_type=F32))


PREP_CHUNKS = ROW_TILE // BLOCK


def _ssd_prep_kernel(raw_ref, bias_ref, alog_ref, cum_ref, dt_ref, w_ref, tot_ref, ecum_ref):
    ii = lax.broadcasted_iota(jnp.int32, (BLOCK, BLOCK), 0)
    jj = lax.broadcasted_iota(jnp.int32, (BLOCK, BLOCK), 1)
    lower = jnp.where(jj <= ii, 1.0, 0.0).astype(BF16)
    upper = jnp.where(jj >= ii, 1.0, 0.0).astype(BF16)
    fwd_lane = lax.broadcasted_iota(jnp.int32, (BLOCK, LANES), 1) < SSD_HEADS
    neg_a = -jnp.exp(alog_ref[...])

    def chunk(k, carry):
        x = raw_ref[_rows(k), :] + bias_ref[...]
        dt = jnp.maximum(x, 0.0) + jnp.log1p(jnp.exp(-jnp.abs(x)))
        a = dt * neg_a
        incl = _split_dot(lower, a)
        rincl = _split_dot(upper, a)
        cum = jnp.where(fwd_lane, incl, rincl)
        tot = jnp.where(fwd_lane[:1], incl[BLOCK - 1:BLOCK, :], rincl[0:1, :])
        cum_ref[k] = cum.T
        dt_ref[k] = dt.T
        w_ref[k] = (dt * jnp.exp(tot - cum)).T
        tot_ref[k] = jnp.broadcast_to(jnp.exp(tot), (BLOCK, LANES)).T
        ecum_ref[k] = jnp.exp(cum).T
        return carry

    lax.fori_loop(0, PREP_CHUNKS, chunk, 0)


def _ssd_prep(dt_raw, dt_bias, a_log):
    nc = T_ALL // BLOCK
    spec = pl.BlockSpec((PREP_CHUNKS, 2 * SSD_HEADS, BLOCK), lambda c: (c, 0, 0))
    shape = jax.ShapeDtypeStruct((nc, 2 * SSD_HEADS, BLOCK), F32)
    return pl.pallas_call(
        _ssd_prep_kernel,
        grid=(nc // PREP_CHUNKS,),
        in_specs=[pl.BlockSpec((ROW_TILE, 2 * SSD_HEADS), lambda c: (c, 0)),
                  pl.BlockSpec((1, 2 * SSD_HEADS), lambda c: (0, 0)),
                  pl.BlockSpec((1, 2 * SSD_HEADS), lambda c: (0, 0))],
        out_specs=[spec] * 5,
        out_shape=[shape] * 5,
        compiler_params=_cparams("arbitrary"),
        name="ssd_prep",
    )(dt_raw, dt_bias.reshape(1, 2 * SSD_HEADS), a_log.reshape(1, 2 * SSD_HEADS))


GW = SSD_R * SSD_P
HALO = 8


def _pair_tiles(per_head):
    low = lax.broadcasted_iota(jnp.int32, per_head[0].shape, 1) < SSD_P
    return jnp.concatenate([jnp.where(low, per_head[2 * t], per_head[2 * t + 1]) for t in range(SSD_R // 2)],
                           axis=1)


def _row_bcast(ref, c, r):
    return jnp.broadcast_to(ref[c, r:r + 1, :], (BLOCK, BLOCK))


def _pair_cols(ref, c):
    top = lax.broadcasted_iota(jnp.int32, (BLOCK, BLOCK), 0) < SSD_P
    return jnp.concatenate(
        [jnp.where(top, _row_bcast(ref, c, 2 * t), _row_bcast(ref, c, 2 * t + 1)).T for t in range(SSD_R // 2)],
        axis=1)


def _ssd_kernel(n_chunks, has_s0, emit_state, has_fill, *refs):
    it = iter(refs)
    z_ref, x_ref, b_ref, c_ref = (next(it) for _ in range(4))
    cumf_ref, cumb_ref, dtf_ref, dtb_ref, wf_ref, wb_ref, totf_ref, totb_ref, ecf_ref, ecb_ref = (
        next(it) for _ in range(10))
    cwx_ref, cwb_ref, cwc_ref, cbx_ref, cbb_ref, cbc_ref, d_ref, ng_ref = (next(it) for _ in range(8))
    s0_ref = next(it) if has_s0 else None
    if has_fill:
        next(it), next(it)
    o_ref, ss_ref = next(it), next(it)
    sfin_ref = next(it) if emit_state else None
    pad_ref, xc_ref, bc_ref, cc_ref, sfs_ref, sf_ref, sb_ref = (next(it) for _ in range(7))
    seq_len = n_chunks * BLOCK

    pad_ref[0:HALO, :] = jnp.zeros((HALO, GW + 2 * SSD_N), F32)
    pad_ref[HALO + seq_len:2 * HALO + seq_len, :] = jnp.zeros((HALO, GW + 2 * SSD_N), F32)

    def fill(c, carry):
        dst = pl.ds(pl.multiple_of(c * BLOCK, BLOCK) + HALO, BLOCK)
        r = _rows(c)
        pad_ref[dst, 0:GW] = x_ref[r, :].astype(F32)
        pad_ref[dst, GW:GW + SSD_N] = b_ref[r, :].astype(F32)
        pad_ref[dst, GW + SSD_N:GW + 2 * SSD_N] = c_ref[r, :].astype(F32)
        return carry

    lax.fori_loop(0, n_chunks, fill, 0)

    def conv(c):
        r = _rows(c)
        src = pl.ds(pl.multiple_of(c * BLOCK, BLOCK), BLOCK + 2 * HALO)
        for col0, width, cw_ref, cb_ref, dst in ((0, GW, cwx_ref, cbx_ref, xc_ref),
                                                 (GW, SSD_N, cwb_ref, cbb_ref, bc_ref),
                                                 (GW + SSD_N, SSD_N, cwc_ref, cbc_ref, cc_ref)):
            for t in range(width // LANES):
                tl = slice(t * LANES, (t + 1) * LANES)
                win = pad_ref[src, col0 + t * LANES:col0 + (t + 1) * LANES]
                acc = jnp.broadcast_to(cb_ref[:, tl], (BLOCK, LANES))
                for w in range(CONV_W):
                    off = HALO - CONV_W // 2 + w
                    acc = acc + win[off:off + BLOCK, :] * cw_ref[w:w + 1, tl]
                dst[r, tl] = _silu(acc).astype(dst.dtype)

    if has_s0:
        for r in range(SSD_R):
            sf_ref[:, r * SSD_P:(r + 1) * SSD_P] = s0_ref[0, r]
            sb_ref[:, r * SSD_P:(r + 1) * SSD_P] = s0_ref[1, r]
    else:
        sf_ref[...] = jnp.zeros_like(sf_ref)
        sb_ref[...] = jnp.zeros_like(sb_ref)

    def state_update(s_ref, c, w_ref, tot_ref, bm, xs):
        tot = _pair_tiles([tot_ref[c, r:r + 1, :] for r in range(SSD_R)])
        s_ref[...] = s_ref[...] * tot + _bdot_tn(bm, xs * _pair_cols(w_ref, c))

    def fwd_step(c):
        r = _rows(c)
        sfs_ref[c] = sf_ref[...].astype(BF16)
        state_update(sf_ref, c, wf_ref, totf_ref, bc_ref[r, :], xc_ref[r, :])

    def fwd(c, carry):
        fwd_step(c)
        conv(c + 1)
        return carry

    conv(0)
    lax.fori_loop(0, n_chunks - 1, fwd, 0, unroll=3 if (n_chunks - 1) % 3 == 0 else 1)
    fwd_step(n_chunks - 1)
    if emit_state:
        for r in range(SSD_R):
            sfin_ref[0, r] = sf_ref[:, r * SSD_P:(r + 1) * SSD_P]

    ii = lax.broadcasted_iota(jnp.int32, (BLOCK, BLOCK), 0)
    jj = lax.broadcasted_iota(jnp.int32, (BLOCK, BLOCK), 1)
    causal = jj <= ii
    anti = ii <= jj
    low = lax.broadcasted_iota(jnp.int32, (BLOCK, LANES), 1) < SSD_P

    def bwd(t, carry):
        c = n_chunks - 1 - t
        r = _rows(c)
        cm, bm, xs = cc_ref[r, :], bc_ref[r, :], xc_ref[r, :]
        xb = xs.astype(BF16)
        sc = _bdot_nt(cm, bm)
        yf = _bdot(cm, sfs_ref[c])
        yb = _bdot(cm, sb_ref[...])
        tiles = []
        for t2 in range(SSD_R // 2):
            xt = xb[:, t2 * LANES:(t2 + 1) * LANES]
            acc = None
            for hh in range(2):
                hr = 2 * t2 + hh
                rf, rb = _row_bcast(cumf_ref, c, hr), _row_bcast(cumb_ref, c, hr)
                e = jnp.exp(jnp.where(anti, rf, rb).T - jnp.where(causal, rf, rb))
                e = e * jnp.where(causal, _row_bcast(dtf_ref, c, hr), _row_bcast(dtb_ref, c, hr))
                xh = jnp.where(low, xt, 0.0) if hh == 0 else jnp.where(low, 0.0, xt)
                part = _bdot(sc * e, xh)
                acc = part if acc is None else acc + part
            tiles.append(acc)
        y = (jnp.concatenate(tiles, axis=1) + _pair_cols(ecf_ref, c) * yf + _pair_cols(ecb_ref, c) * yb
             + d_ref[...] * xs)
        yg = y * _silu(z_ref[r, :].astype(F32))
        ss_ref[r, :] = jnp.broadcast_to(jnp.sum(yg * yg, axis=-1, keepdims=True), (BLOCK, LANES))
        o_ref[r, :] = (yg * ng_ref[...]).astype(o_ref.dtype)
        state_update(sb_ref, c, wb_ref, totb_ref, bm, xs)
        return carry

    lax.fori_loop(0, n_chunks, bwd, 0, unroll=min(4, n_chunks))
    if emit_state:
        for r in range(SSD_R):
            sfin_ref[1, r] = sb_ref[:, r * SSD_P:(r + 1) * SSD_P]


def _ssd_scan(zx, prep, conv_w, conv_b, d_exp, norm_gain, seq_len, n_seq, row_block0, s0=None, emit_state=False,
              fill=None):
    n_chunks = seq_len // BLOCK
    has_s0 = s0 is not None
    xcol0 = D_INNER // GW
    bcol0 = 2 * D_INNER // SSD_N
    ccol0 = bcol0 + SSD_GROUPS
    cwb0 = D_INNER // SSD_N

    def rowcol(width, col0):
        return pl.BlockSpec((seq_len, width), lambda b, g, col0=col0: (b + row_block0, col0 + g))

    def headrows(direction):
        return pl.BlockSpec((n_chunks, SSD_R, BLOCK),
                            lambda b, g, direction=direction: (b + row_block0, direction * SSD_GROUPS + g, 0))

    in_specs = [rowcol(GW, 0), rowcol(GW, xcol0), rowcol(SSD_N, bcol0), rowcol(SSD_N, ccol0)]
    args = [zx, zx, zx, zx]
    for arr in prep:
        in_specs += [headrows(0), headrows(1)]
        args += [arr, arr]
    in_specs += [pl.BlockSpec((CONV_W, GW), lambda b, g: (0, g)),
                 pl.BlockSpec((CONV_W, SSD_N), lambda b, g: (0, cwb0 + g)),
                 pl.BlockSpec((CONV_W, SSD_N), lambda b, g: (0, cwb0 + SSD_GROUPS + g)),
                 pl.BlockSpec((1, GW), lambda b, g: (0, g)),
                 pl.BlockSpec((1, SSD_N), lambda b, g: (0, cwb0 + g)),
                 pl.BlockSpec((1, SSD_N), lambda b, g: (0, cwb0 + SSD_GROUPS + g)),
                 pl.BlockSpec((1, GW), lambda b, g: (0, g)),
                 pl.BlockSpec((1, GW), lambda b, g: (0, g))]
    args += [conv_w, conv_w, conv_w, conv_b, conv_b, conv_b, d_exp, norm_gain.reshape(1, D_INNER)]
    state_spec = pl.BlockSpec((None, None, 2, SSD_R, SSD_N, SSD_P), lambda b, g: (b, 0, 0, g, 0, 0))
    if has_s0:
        in_specs.append(state_spec)
        args.append(s0)
    aliases = _fill_alias(fill, in_specs, args)
    out_specs = [pl.BlockSpec((seq_len, GW), lambda b, g: (b + row_block0, g)),
                 pl.BlockSpec((seq_len, LANES), lambda b, g: (b + row_block0, g))]
    out_shape = [jax.ShapeDtypeStruct((T_ALL, D_INNER), BF16),
                 jax.ShapeDtypeStruct((T_ALL, SSD_GROUPS * LANES), F32)]
    if emit_state:
        out_specs.append(state_spec)
        out_shape.append(jax.ShapeDtypeStruct((n_seq, 1, 2, SSD_HEADS, SSD_N, SSD_P), F32))
    return pl.pallas_call(
        functools.partial(_ssd_kernel, n_chunks, has_s0, emit_state, bool(fill)),
        grid=(n_seq, SSD_GROUPS),
        in_specs=in_specs,
        out_specs=out_specs,
        out_shape=out_shape,
        input_output_aliases=aliases,
        scratch_shapes=[pltpu.VMEM((seq_len + 2 * HALO, GW + 2 * SSD_N), F32),
                        pltpu.VMEM((seq_len, GW), F32), pltpu.VMEM((seq_len, SSD_N), BF16),
                        pltpu.VMEM((seq_len, SSD_N), BF16), pltpu.VMEM((n_chunks, SSD_N, GW), BF16),
                        pltpu.VMEM((SSD_N, GW), F32), pltpu.VMEM((SSD_N, GW), F32)],
        compiler_params=_cparams("arbitrary", "arbitrary"),
        name="ssd_scan_latent" if has_s0 else "ssd_scan_prompt",
    )(*args)


def _router_kernel(x_ref, g_ref, mod_ref, rw_ref, hn_ref, idx_ref, wgt_ref, hf_ref):
    rw = rw_ref[...]
    r_hi = rw.astype(BF16)
    r_lo = (rw - r_hi.astype(F32)).astype(BF16)
    r_both = jnp.concatenate([r_hi, r_lo], axis=1)

    def route(r):
        hf = hf_ref[r, :]
        h_hi = hf.astype(BF16)
        h_lo = (hf - h_hi.astype(F32)).astype(BF16)
        hn_ref[r, :] = h_hi
        t1 = jnp.dot(h_hi, r_both, preferred_element_type=F32)
        logits = t1[:, :LANES] + t1[:, LANES:] + jnp.dot(h_lo, r_hi, preferred_element_type=F32)
        lane = lax.broadcasted_iota(jnp.int32, logits.shape, 1)
        lg = jnp.where(lane < N_EXPERTS, logits, NEG_INF)
        m1 = jnp.max(lg, axis=-1, keepdims=True)
        i1 = jnp.min(jnp.where(lg == m1, lane, LANES), axis=-1, keepdims=True)
        lg2 = jnp.where(lane == i1, NEG_INF, lg)
        m2 = jnp.max(lg2, axis=-1, keepdims=True)
        i2 = jnp.min(jnp.where(lg2 == m2, lane, LANES), axis=-1, keepdims=True)
        e2 = jnp.exp(m2 - m1)
        w1 = 1.0 / (1.0 + e2)
        idx_ref[r, :] = jnp.where(lane == 0, i1, jnp.where(lane == 1, i2, 0))
        wgt_ref[r, :] = jnp.where(lane == 0, w1, jnp.where(lane == 1, e2 * w1, 0.0))

    _adaln_then(x_ref, g_ref, mod_ref, 3, 4, hf_ref, route)


def _router(x, gain, modt, router_w):
    t, d = x.shape
    rw = jnp.pad(router_w, ((0, 0), (0, LANES - N_EXPERTS)))
    return pl.pallas_call(
        _router_kernel,
        grid=(t // ROW_TILE,),
        in_specs=[pl.BlockSpec((ROW_TILE, d), lambda i: (i, 0)),
                  pl.BlockSpec((1, d), lambda i: (0, 0)),
                  pl.BlockSpec((None, 8, d), lambda i: (i, 0, 0)),
                  pl.BlockSpec((d, LANES), lambda i: (0, 0))],
        out_specs=[pl.BlockSpec((ROW_TILE, d), lambda i: (i, 0)),
                   pl.BlockSpec((ROW_TILE, LANES), lambda i: (i, 0)),
                   pl.BlockSpec((ROW_TILE, LANES), lambda i: (i, 0))],
        out_shape=[jax.ShapeDtypeStruct((t, d), BF16), jax.ShapeDtypeStruct((t, LANES), jnp.int32),
                   jax.ShapeDtypeStruct((t, LANES), F32)],
        scratch_shapes=[pltpu.VMEM((ROW_TILE, d), F32)],
        compiler_params=_cparams("arbitrary"),
        name="moe_router",
    )(x, gain.reshape(1, d), modt, rw)


DOWN_ROWS = 512
DOWN_TN = 1024


def _expert_changed(be_ref, blk, prev_blk, step):
    return jnp.logical_or(step == 0, be_ref[blk] != be_ref[prev_blk])


def _expert_up_kernel(be_ref, nu_ref, nv_ref, xs_ref, wg_ref, wu_ref, h_ref, wgb_ref, wub_ref):
    i = pl.program_id(1)
    used = nv_ref[i] > 0

    @pl.when(jnp.logical_and(used, _expert_changed(be_ref, i, jnp.maximum(i - 1, 0), i)))
    def _():
        wgb_ref[...] = wg_ref[...].astype(BF16)
        wub_ref[...] = wu_ref[...].astype(BF16)

    @pl.when(used)
    def _():
        xs = xs_ref[...]
        h_ref[...] = (_silu(_bdot(xs, wgb_ref[...])) * _bdot(xs, wub_ref[...])).astype(h_ref.dtype)

    @pl.when(jnp.logical_not(used))
    def _():
        h_ref[...] = jnp.zeros_like(h_ref)


def _expert_down_kernel(be_ref, nu_ref, nv_ref, h_ref, wd_ref, o_ref, wdb_ref):
    i = pl.program_id(1)
    per = MOE_ROWS // DOWN_ROWS
    blk = i // per
    used = nv_ref[blk] > (i % per) * DOWN_ROWS

    @pl.when(jnp.logical_and(nv_ref[blk] > 0, _expert_changed(be_ref, blk, jnp.maximum(i - 1, 0) // per, i)))
    def _():
        wdb_ref[...] = wd_ref[...].astype(BF16)

    @pl.when(used)
    def _():
        o_ref[...] = _bdot(h_ref[...], wdb_ref[...]).astype(o_ref.dtype)

    @pl.when(jnp.logical_not(used))
    def _():
        o_ref[...] = jnp.zeros_like(o_ref)


def _experts(xs_sorted, block_e, n_used, n_valid, wg, wu, wd):
    cap, d = xs_sorted.shape
    ff = wg.shape[2]
    per = MOE_ROWS // DOWN_ROWS

    def expert_of(blk, be, nu):
        return be[jnp.minimum(blk, jnp.maximum(nu[0] - 1, 0))]

    h = pl.pallas_call(
        _expert_up_kernel,
        grid_spec=pltpu.PrefetchScalarGridSpec(
            num_scalar_prefetch=3,
            grid=(ff // FFN_TF, cap // MOE_ROWS),
            in_specs=[pl.BlockSpec((MOE_ROWS, d), lambda f, i, be, nu, nv: (i, 0)),
                      pl.BlockSpec((None, d, FFN_TF), lambda f, i, be, nu, nv: (expert_of(i, be, nu), 0, f)),
                      pl.BlockSpec((None, d, FFN_TF), lambda f, i, be, nu, nv: (expert_of(i, be, nu), 0, f))],
            out_specs=pl.BlockSpec((MOE_ROWS, FFN_TF), lambda f, i, be, nu, nv: (i, f)),
            scratch_shapes=[pltpu.VMEM((d, FFN_TF), BF16), pltpu.VMEM((d, FFN_TF), BF16)],
        ),
        out_shape=jax.ShapeDtypeStruct((cap, ff), BF16),
        compiler_params=_cparams("arbitrary", "arbitrary"),
        name="moe_expert_up",
    )(block_e, n_used, n_valid, xs_sorted, wg, wu)
    return pl.pallas_call(
        _expert_down_kernel,
        grid_spec=pltpu.PrefetchScalarGridSpec(
            num_scalar_prefetch=3,
            grid=(d // DOWN_TN, cap // DOWN_ROWS),
            in_specs=[pl.BlockSpec((DOWN_ROWS, ff), lambda n, i, be, nu, nv: (i, 0)),
                      pl.BlockSpec((None, ff, DOWN_TN),
                                   lambda n, i, be, nu, nv: (expert_of(i // per, be, nu), 0, n),
                                   pipeline_mode=pl.Buffered(1))],
            out_specs=pl.BlockSpec((DOWN_ROWS, DOWN_TN), lambda n, i, be, nu, nv: (i, n)),
            scratch_shapes=[pltpu.VMEM((ff, DOWN_TN), BF16)],
        ),
        out_shape=jax.ShapeDtypeStruct((cap, d), BF16),
        compiler_params=_cparams("arbitrary", "arbitrary"),
        name="moe_expert_down",
    )(block_e, n_used, n_valid, h, wd)


def _combine_kernel(x_ref, g0_ref, g1_ref, w_ref, mod_ref, o_ref):
    w = w_ref[...]
    y = w[:, 0:1] * g0_ref[...].astype(F32) + w[:, 1:2] * g1_ref[...].astype(F32)
    o_ref[...] = x_ref[...] + mod_ref[5:6, :] * y


COMB_ROWS = ROW_TILE


def _combine(x, g, wgt, modt, row0, n_rows):
    t, d = x.shape
    b0 = row0 // COMB_ROWS
    per = ROW_TILE // COMB_ROWS
    return pl.pallas_call(
        _combine_kernel,
        grid=(n_rows // COMB_ROWS,),
        in_specs=[pl.BlockSpec((COMB_ROWS, d), lambda i: (i + b0, 0)),
                  pl.BlockSpec((COMB_ROWS, d), lambda i: (i + b0, 0)),
                  pl.BlockSpec((COMB_ROWS, d), lambda i: (i + b0 + t // COMB_ROWS, 0)),
                  pl.BlockSpec((COMB_ROWS, LANES), lambda i: (i + b0, 0)),
                  pl.BlockSpec((None, 8, d), lambda i: ((i + b0) // per, 0, 0))],
        out_specs=pl.BlockSpec((COMB_ROWS, d), lambda i: (i, 0)),
        out_shape=jax.ShapeDtypeStruct((n_rows, d), F32),
        compiler_params=_cparams("arbitrary"),
        name="moe_combine",
    )(x, g, g, wgt, modt)


def _moe(x, gain, modt, router_w, wg, wu, wd):
    t, d = x.shape
    hn, idx, wgt = _router(x, gain, modt, router_w)
    top_idx = idx[:, :TOP_K]
    n_slots = t * TOP_K
    flat_e = top_idx.reshape(-1)
    onehot = (flat_e[:, None] == jnp.arange(N_EXPERTS, dtype=jnp.int32)[None, :]).astype(jnp.int32)
    incl = jnp.cumsum(onehot, axis=0)
    counts = incl[-1]
    rank = jnp.sum((incl - onehot) * onehot, axis=1)
    padded = (counts + MOE_ROWS - 1) // MOE_ROWS * MOE_ROWS
    pend = jnp.cumsum(padded)
    pstart = pend - padded
    dest = pstart[flat_e] + rank
    n_blocks = n_slots // MOE_ROWS + N_EXPERTS
    cap = n_blocks * MOE_ROWS
    row_tok = (jnp.arange(cap, dtype=jnp.int32) % t).at[dest].set(
        jnp.arange(n_slots, dtype=jnp.int32) // TOP_K, unique_indices=True, mode='promise_in_bounds')
    block_e = jnp.clip(jnp.searchsorted(pend, jnp.arange(n_blocks, dtype=jnp.int32) * MOE_ROWS, side='right'),
                       0, N_EXPERTS - 1).astype(jnp.int32)
    n_used = (pend[-1:] // MOE_ROWS).astype(jnp.int32)
    blk = jnp.arange(n_blocks, dtype=jnp.int32)
    n_valid = jnp.where(blk < n_used[0],
                        jnp.clip((pstart + counts)[block_e] - blk * MOE_ROWS, 0, MOE_ROWS), 0).astype(jnp.int32)
    out = _experts(hn.at[row_tok].get(mode='promise_in_bounds'), block_e, n_used, n_valid, wg, wu, wd)
    g = out.at[dest.reshape(t, TOP_K).T.reshape(-1)].get(mode='promise_in_bounds', unique_indices=True)
    return _combine(x, g, wgt, modt, 0, T_PROMPT), _combine(x, g, wgt, modt, T_PROMPT, T_SAMPLE)


def _rope_tables(n_tokens, dim):
    n_rows = n_tokens // GRID_W
    row = jnp.repeat(jnp.arange(n_rows), GRID_W).astype(F32)
    col = jnp.tile(jnp.arange(GRID_W), n_rows).astype(F32)
    n_freq = dim // 4
    inv = ROPE_BASE ** (-jnp.arange(n_freq, dtype=F32) / n_freq)
    ang = jnp.concatenate([row[:, None] * inv, col[:, None] * inv], axis=-1)
    return jnp.cos(ang), jnp.sin(ang)


def kernel(x_prompt, x_sample, state_ret, cache_k, cache_v, state_ssd, c, c_ctx, ada_w, ada_b, norm_mix, norm_ffn, ev_w_in, ev_w_out, ret_decay_logit, ret_norm, att_q_norm, att_k_norm, att_sink, ffn_w_gate, ffn_w_up, ffn_w_down, ssd_w_in, ssd_conv_w, ssd_conv_b, ssd_a_log, ssd_dt_bias, ssd_d, ssd_norm, ssd_w_out, moe_router, moe_w_gate, moe_w_up, moe_w_down):
    d = D_MODEL
    x = (x_prompt.reshape(T_PROMPT, d), x_sample.reshape(T_SAMPLE, d))

    cvecs = jnp.concatenate([c_ctx[None, :], c, jnp.zeros((MOD_ROWS - 1 - DEC_BATCH, d), F32)], axis=0)
    mods = _modulation(cvecs, ada_w, ada_b).reshape(2, MOD_ROWS, 6, d)
    tiles_per_seq = DEC_SEQ // ROW_TILE
    tile_row = jnp.concatenate([jnp.zeros((T_PROMPT // ROW_TILE,), jnp.int32),
                                1 + jnp.arange(T_SAMPLE // ROW_TILE, dtype=jnp.int32) // tiles_per_seq])
    modt = jnp.pad(mods[:, tile_row], ((0, 0), (0, 0), (0, 2), (0, 0)))

    ev_w_in_b, ev_w_out_b = ev_w_in[0].astype(BF16), ev_w_out[0].astype(BF16)
    ffn_wd_b = ffn_w_down[0].astype(BF16)
    ssd_w_in_b, ssd_w_out_b = ssd_w_in[0].astype(BF16), ssd_w_out[0].astype(BF16)

    proj, = _adaln_matmul(x, norm_mix[0], modt[0], 0, 1, ev_w_in_b, tn=512, out_dtype=BF16, name="even_in_proj")
    lg = jax.nn.log_sigmoid(ret_decay_logit[0].astype(F32))
    cos_r, sin_r = _rope_tables(DEC_SEQ, RET_DK)
    cos_a, sin_a = _rope_tables(DEC_SEQ, ATT_HD)
    cos_a2 = jnp.concatenate([cos_a, cos_a], axis=-1)
    sin_a2 = jnp.concatenate([-sin_a, sin_a], axis=-1)
    mix_ret, new_state_ret = _retention(proj, lg, ret_norm[0], SEQ, BATCH, 0, emit_state=True)
    mix_ret, = _retention(proj, lg, ret_norm[0], DEC_SEQ, DEC_BATCH, T_PROMPT // DEC_SEQ,
                          ropes=(cos_r, sin_r), s0=state_ret, fill=(mix_ret,))
    mix_att, new_k, new_v = _attention_prompt(proj, att_sink[0], att_q_norm[0], att_k_norm[0])
    mix_att = _attention_latent(proj, cache_k[:, 0], cache_v[:, 0], att_sink[0], cos_a2, sin_a2,
                                att_q_norm[0], att_k_norm[0], mix_att)
    x = _proj_residual([mix_ret, mix_att], ev_w_out_b, x, modt[0], 2, tn=1024, name="even_out_proj")
    h = _ffn_gateup(x, norm_ffn[0], modt[0], ffn_w_gate[0], ffn_w_up[0])
    x = _proj_residual([h], ffn_wd_b, x, modt[0], 5, tn=512, name="ffn_down")

    zx, dt_raw = _adaln_matmul(x, norm_mix[1], modt[1], 0, 1, ssd_w_in_b, tn=1024, n_out=SSD_ZX,
                               tail=2 * SSD_HEADS, out_dtype=BF16, name="ssd_in_proj")
    prep = _ssd_prep(dt_raw, ssd_dt_bias[0], ssd_a_log[0])
    d_exp = jnp.repeat(ssd_d[0], SSD_P)[None, :]
    conv_b = ssd_conv_b[0][None, :]
    yg, yss, new_state_ssd = _ssd_scan(zx, prep, ssd_conv_w[0], conv_b, d_exp, ssd_norm[0], SEQ, BATCH, 0,
                                       emit_state=True)
    yg, yss = _ssd_scan(zx, prep, ssd_conv_w[0], conv_b, d_exp, ssd_norm[0], DEC_SEQ, DEC_BATCH,
                        T_PROMPT // DEC_SEQ, s0=state_ssd, fill=(yg, yss))
    x = _proj_residual([yg], ssd_w_out_b, x, modt[1], 2, tn=512, row_ss=yss, name="ssd_out_proj")
    y_p, y_s = _moe(x, norm_ffn[1], modt[1], moe_router[0], moe_w_gate[0], moe_w_up[0], moe_w_down[0])

    y_prompt = y_p.reshape(BATCH, SEQ, d)
    y_sample = y_s.reshape(DEC_BATCH, DEC_SEQ, d)
    new_cache_k = new_k.reshape(BATCH, 1, SEQ, ATT_KV_HEADS, ATT_HD)
    new_cache_v = new_v.reshape(BATCH, 1, SEQ, ATT_KV_HEADS, ATT_HD)
    return (y_prompt, y_sample, new_state_ret, new_cache_k, new_cache_v, new_state_ssd)
```

```python
import functools

import jax
import jax.numpy as jnp
from jax import lax
from jax.experimental import pallas as pl
from jax.experimental.pallas import tpu as pltpu

F32 = jnp.float32
BF16 = jnp.bfloat16

D_MODEL = 2048
BATCH = 16
SEQ = 256
DEC_BATCH = 8
DEC_SEQ = 2048
PAST_LEN = 512
GRID_W = 64
BLOCK = 128
WINDOW = 128
EPS = 1e-6
ROPE_BASE = 10000.0
RET_HEADS = 4
RET_DK = 256
RET_DV = 256
ATT_HEADS = 8
ATT_KV_HEADS = 2
ATT_HD = 128
ATT_GROUP = ATT_HEADS // ATT_KV_HEADS
EVEN_IN = 5632
D_INNER = 2 * D_MODEL
SSD_P = 64
SSD_HEADS = D_INNER // SSD_P
SSD_N = 128
SSD_GROUPS = 8
SSD_R = SSD_HEADS // SSD_GROUPS
CONV_W = 5
CONV_CH = D_INNER + 2 * SSD_GROUPS * SSD_N
SSD_ZX = D_INNER + CONV_CH
D_FF = 5632
N_EXPERTS = 8
TOP_K = 2

T_PROMPT = BATCH * SEQ
T_SAMPLE = DEC_BATCH * DEC_SEQ
T_ALL = T_PROMPT + T_SAMPLE

LANES = 128
ROW_TILE = 1024
VMEM_LIMIT = 56 * 1024 * 1024
MOE_ROWS = 1024
NEG_INF = float("-inf")


def _cparams(*sem):
    return pltpu.CompilerParams(dimension_semantics=sem, vmem_limit_bytes=VMEM_LIMIT)


def _silu(x):
    return x * jax.nn.sigmoid(x)


def _bdot(a, b):
    return jnp.dot(a.astype(BF16), b.astype(BF16), preferred_element_type=F32)


def _bdot_nt(a, b):
    return lax.dot_general(a.astype(BF16), b.astype(BF16), (((1,), (1,)), ((), ())),
                           preferred_element_type=F32)


def _bdot_tn(a, b):
    return lax.dot_general(a.astype(BF16), b.astype(BF16), (((0,), (0,)), ((), ())),
                           preferred_element_type=F32)


def _rows(c, n=BLOCK):
    return pl.ds(pl.multiple_of(c * n, n), n)


MOD_ROWS = 16
MOD_TN = 1024


def _mod_kernel(c_ref, w_ref, b_ref, o_ref):
    o_ref[...] = _bdot(_silu(c_ref[...]), w_ref[...]) + b_ref[...]


def _modulation(cvecs, ada_w, ada_b):
    depth, d, n = ada_w.shape
    return pl.pallas_call(
        _mod_kernel,
        grid=(depth, n // MOD_TN),
        in_specs=[pl.BlockSpec((MOD_ROWS, d), lambda l, j: (0, 0)),
                  pl.BlockSpec((None, d, MOD_TN), lambda l, j: (l, 0, j)),
                  pl.BlockSpec((None, 1, MOD_TN), lambda l, j: (l, 0, j))],
        out_specs=pl.BlockSpec((None, MOD_ROWS, MOD_TN), lambda l, j: (l, 0, j)),
        out_shape=jax.ShapeDtypeStruct((depth, MOD_ROWS, n), F32),
        compiler_params=_cparams("arbitrary", "arbitrary"),
        name="modulation",
    )(cvecs, ada_w, ada_b.reshape(depth, 1, n))


ADALN_CHUNK = 64
ADALN_DOT_ROWS = 256


def _adaln_then(x_ref, g_ref, mod_ref, shift_row, scale_row, hn_ref, consume):
    mult = g_ref[...] * (1.0 + mod_ref[scale_row:scale_row + 1, :])
    shift = mod_ref[shift_row:shift_row + 1, :]
    for r0 in range(0, x_ref.shape[0], ADALN_DOT_ROWS):
        for s0 in range(r0, r0 + ADALN_DOT_ROWS, ADALN_CHUNK):
            r = slice(s0, s0 + ADALN_CHUNK)
            x = x_ref[r, :]
            ms = jnp.mean(x * x, axis=-1, keepdims=True)
            hn_ref[r, :] = (x * lax.rsqrt(ms + EPS) * mult + shift).astype(hn_ref.dtype)
        consume(slice(r0, r0 + ADALN_DOT_ROWS))


PROMPT_TILES = T_PROMPT // ROW_TILE


def _x_rows_specs(x, cols, col_of_j):
    if not isinstance(x, tuple):
        return [pl.BlockSpec((ROW_TILE, cols), lambda i, j: (i, col_of_j(j)))], [x]
    return ([pl.BlockSpec((ROW_TILE, cols),
                          lambda i, j: (jnp.minimum(i, PROMPT_TILES - 1), jnp.where(i < PROMPT_TILES, col_of_j(j), 0))),
             pl.BlockSpec((ROW_TILE, cols),
                          lambda i, j: (jnp.maximum(i - PROMPT_TILES, 0), jnp.where(i < PROMPT_TILES, 0, col_of_j(j))))],
            list(x))


def _for_x_rows(x_refs, fn):
    if len(x_refs) == 1:
        fn(x_refs[0])
        return
    in_prompt = pl.program_id(0) < PROMPT_TILES
    pl.when(in_prompt)(lambda: fn(x_refs[0]))
    pl.when(jnp.logical_not(in_prompt))(lambda: fn(x_refs[1]))


def _adaln_mm_kernel(shift_row, scale_row, has_tail, n_x, *refs):
    x_refs, (g_ref, mod_ref, w_ref), refs = refs[:n_x], refs[n_x:n_x + 3], refs[n_x + 3:]
    if has_tail:
        wt_ref, o_ref, ot_ref, hn_ref = refs
    else:
        o_ref, hn_ref = refs

    def project(r):
        hn = hn_ref[r, :]
        o_ref[r, :] = _bdot(hn, w_ref[...]).astype(o_ref.dtype)
        if has_tail:
            ot_ref[r, :] = _bdot(hn, wt_ref[...])

    first = pl.program_id(1) == 0

    @pl.when(first)
    def _():
        _for_x_rows(x_refs, lambda x_ref: _adaln_then(x_ref, g_ref, mod_ref, shift_row, scale_row, hn_ref, project))

    @pl.when(jnp.logical_not(first))
    def _():
        o_ref[...] = _bdot(hn_ref[...], w_ref[...]).astype(o_ref.dtype)


def _adaln_matmul(x, gain, modt, shift_row, scale_row, w, tn, n_out=None, tail=0, out_dtype=F32, name="adaln_mm"):
    t, d = T_ALL, w.shape[0]
    n_out = w.shape[1] if n_out is None else n_out
    in_specs, args = _x_rows_specs(x, d, lambda j: 0)
    n_x = len(args)
    in_specs += [pl.BlockSpec((1, d), lambda i, j: (0, 0)),
                 pl.BlockSpec((None, 8, d), lambda i, j: (i, 0, 0)),
                 pl.BlockSpec((d, tn), lambda i, j: (0, j))]
    args += [gain.reshape(1, d), modt, w]
    out_specs = [pl.BlockSpec((ROW_TILE, tn), lambda i, j: (i, j))]
    out_shape = [jax.ShapeDtypeStruct((t, n_out), out_dtype)]
    if tail:
        in_specs.append(pl.BlockSpec((d, tail), lambda i, j: (0, n_out // tail)))
        args.append(w)
        out_specs.append(pl.BlockSpec((ROW_TILE, tail), lambda i, j: (i, 0)))
        out_shape.append(jax.ShapeDtypeStruct((t, tail), F32))
    return pl.pallas_call(
        functools.partial(_adaln_mm_kernel, shift_row, scale_row, bool(tail), n_x),
        grid=(t // ROW_TILE, n_out // tn),
        in_specs=in_specs,
        out_specs=out_specs,
        out_shape=out_shape,
        scratch_shapes=[pltpu.VMEM((ROW_TILE, d), BF16)],
        compiler_params=_cparams("arbitrary", "arbitrary"),
        name=name,
    )(*args)


FFN_TF = 512


def _gateup_kernel(x_ref, g_ref, mod_ref, wg_ref, wu_ref, h_ref, hn_ref):
    def gate_up(r):
        hn = hn_ref[r, :]
        h_ref[r, :] = (_silu(_bdot(hn, wg_ref[...])) * _bdot(hn, wu_ref[...])).astype(h_ref.dtype)

    first = pl.program_id(1) == 0

    @pl.when(first)
    def _():
        _adaln_then(x_ref, g_ref, mod_ref, 3, 4, hn_ref, gate_up)

    @pl.when(jnp.logical_not(first))
    def _():
        gate_up(slice(None))


def _ffn_gateup(x, gain, modt, wg, wu):
    t, d = x.shape
    ff = wg.shape[1]
    return pl.pallas_call(
        _gateup_kernel,
        grid=(t // ROW_TILE, ff // FFN_TF),
        in_specs=[pl.BlockSpec((ROW_TILE, d), lambda i, f: (i, 0)),
                  pl.BlockSpec((1, d), lambda i, f: (0, 0)),
                  pl.BlockSpec((None, 8, d), lambda i, f: (i, 0, 0)),
                  pl.BlockSpec((d, FFN_TF), lambda i, f: (0, f)),
                  pl.BlockSpec((d, FFN_TF), lambda i, f: (0, f))],
        out_specs=pl.BlockSpec((ROW_TILE, FFN_TF), lambda i, f: (i, f)),
        out_shape=jax.ShapeDtypeStruct((t, ff), BF16),
        scratch_shapes=[pltpu.VMEM((ROW_TILE, d), BF16)],
        compiler_params=_cparams("arbitrary", "arbitrary"),
        name="ffn_gateup",
    )(x, gain.reshape(1, d), modt, wg, wu)


def _proj_res_kernel(n_a, n_x, gate_row, norm, *refs):
    a_refs = refs[:n_a]
    w_refs = refs[n_a:2 * n_a]
    x_refs, mod_ref = refs[2 * n_a:2 * n_a + n_x], refs[2 * n_a + n_x]
    pos = 2 * n_a + n_x + 1
    o_ref = refs[pos + 1] if norm else refs[pos]
    acc = _bdot(a_refs[0][...], w_refs[0][...])
    for k in range(1, n_a):
        acc = acc + _bdot(a_refs[k][...], w_refs[k][...])
    if norm:
        ss_ref = refs[pos]
        k_total = sum(a.shape[1] for a in a_refs)
        ss = ss_ref[:, 0:LANES]
        for k in range(1, ss_ref.shape[1] // LANES):
            ss = ss + ss_ref[:, k * LANES:(k + 1) * LANES]
        rs = lax.rsqrt(ss * (1.0 / k_total) + EPS)
        acc = acc * jnp.concatenate([rs] * (acc.shape[1] // LANES), axis=1)
    upd = mod_ref[gate_row:gate_row + 1, :] * acc

    def finish(x_ref):
        o_ref[...] = x_ref[...] + upd

    _for_x_rows(x_refs, finish)


def _proj_residual(a_list, w, x, modt, gate_row, tn, row_ss=None, name="proj_res"):
    t, d = T_ALL, w.shape[1]
    n_a = len(a_list)
    norm = row_ss is not None
    in_specs, args, k0 = [], [], 0
    for a in a_list:
        in_specs.append(pl.BlockSpec((ROW_TILE, a.shape[1]), lambda i, j: (i, 0)))
        args.append(a)
    for a in a_list:
        ka = a.shape[1]
        assert k0 % ka == 0
        in_specs.append(pl.BlockSpec((ka, tn), lambda i, j, kb=k0 // ka: (kb, j)))
        args.append(w)
        k0 += ka
    x_specs, x_args = _x_rows_specs(x, tn, lambda j: j)
    in_specs += x_specs + [pl.BlockSpec((None, 8, tn), lambda i, j: (i, 0, j))]
    args += x_args + [modt]
    if norm:
        in_specs.append(pl.BlockSpec((ROW_TILE, row_ss.shape[1]), lambda i, j: (i, 0)))
        args.append(row_ss)
    return pl.pallas_call(
        functools.partial(_proj_res_kernel, n_a, len(x_args), gate_row, norm),
        grid=(t // ROW_TILE, d // tn),
        in_specs=in_specs,
        out_specs=pl.BlockSpec((ROW_TILE, tn), lambda i, j: (i, j)),
        out_shape=jax.ShapeDtypeStruct((t, d), F32),
        compiler_params=_cparams("arbitrary", "arbitrary"),
        name=name,
    )(*args)


def _ret_kernel(n_chunks, rope, has_s0, emit_state, has_fill, lg_ref, *refs):
    it = iter(refs)
    q_ref, k_ref, v_ref, gt_ref = next(it), next(it), next(it), next(it)
    cos_ref, sin_ref = (next(it), next(it)) if rope else (None, None)
    s0_ref = next(it) if has_s0 else None
    gain_ref = next(it)
    if has_fill:
        next(it)
    o_ref = next(it)
    sfin_ref = next(it) if emit_state else None
    qs_ref, ks_ref, sfs_ref, dm_ref, dec_ref, sf_ref, sb_ref = (next(it) for _ in range(7))

    h = pl.program_id(1)
    lgf = lg_ref[0, h]
    lgb = lg_ref[1, h]
    half = RET_DK // 2

    def prep(c, carry):
        r = _rows(c)
        q = q_ref[r, :].astype(F32)
        k = k_ref[r, :].astype(F32) * (RET_DK ** -0.5)
        if rope:
            cs, sn = cos_ref[r, :], sin_ref[r, :]
            for src, dst in ((q, qs_ref), (k, ks_ref)):
                x1, x2 = src[:, :half], src[:, half:]
                dst[r, :half] = (x1 * cs - x2 * sn).astype(BF16)
                dst[r, half:] = (x1 * sn + x2 * cs).astype(BF16)
        else:
            qs_ref[r, :] = q.astype(BF16)
            ks_ref[r, :] = k.astype(BF16)
        return carry

    lax.fori_loop(0, n_chunks, prep, 0)

    ii = lax.broadcasted_iota(jnp.int32, (BLOCK, BLOCK), 0)
    jj = lax.broadcasted_iota(jnp.int32, (BLOCK, BLOCK), 1)
    diff = (ii - jj).astype(F32)
    dm_ref[...] = jnp.exp(jnp.where(jj <= ii, diff * lgf, -diff * lgb))
    pos = lax.broadcasted_iota(jnp.int32, (BLOCK, RET_DV), 0).astype(F32)
    dec_ref[0] = jnp.exp((pos + 1.0) * lgf)
    dec_ref[1] = jnp.exp((BLOCK - pos) * lgb)
    dec_ref[2] = jnp.exp((BLOCK - 1.0 - pos) * lgf)
    dec_ref[3] = jnp.exp(pos * lgb)
    tot_f = jnp.exp(jnp.full((1, RET_DV), BLOCK * lgf, F32))
    tot_b = jnp.exp(jnp.full((1, RET_DV), BLOCK * lgb, F32))

    if has_s0:
        sf_ref[...] = s0_ref[0]
        sb_ref[...] = s0_ref[1]
    else:
        sf_ref[...] = jnp.zeros_like(sf_ref)
        sb_ref[...] = jnp.zeros_like(sb_ref)

    def fwd(c, carry):
        r = _rows(c)
        sfs_ref[c] = sf_ref[...].astype(BF16)
        kd = ks_ref[r, :].astype(F32) * dec_ref[2]
        sf_ref[...] = sf_ref[...] * tot_f + _bdot_tn(kd, v_ref[r, :])
        return carry

    lax.fori_loop(0, n_chunks, fwd, 0, unroll=min(4, n_chunks))
    if emit_state:
        sfin_ref[0] = sf_ref[...]

    def bwd(t, carry):
        c = n_chunks - 1 - t
        r = _rows(c)
        q = qs_ref[r, :]
        k = ks_ref[r, :]
        v = v_ref[r, :].astype(BF16)
        p = _bdot_nt(q, k) * dm_ref[...]
        o = _bdot(p, v)
        o = o + _bdot(q, sfs_ref[c]) * dec_ref[0]
        o = o + _bdot(q, sb_ref[...]) * dec_ref[1]
        ms = jnp.mean(o * o, axis=-1, keepdims=True)
        y = o * lax.rsqrt(ms + EPS) * gain_ref[...]
        o_ref[r, :] = (y * _silu(gt_ref[r, :].astype(F32))).astype(o_ref.dtype)
        kd = k.astype(F32) * dec_ref[3]
        sb_ref[...] = sb_ref[...] * tot_b + _bdot_tn(kd, v)
        return carry

    lax.fori_loop(0, n_chunks, bwd, 0, unroll=min(4, n_chunks))
    if emit_state:
        sfin_ref[1] = sb_ref[...]


def _fill_alias(fill, in_specs, args):
    aliases = {}
    for k, arr in enumerate(fill or ()):
        in_specs.append(pl.BlockSpec(memory_space=pl.ANY))
        args.append(arr)
        aliases[len(args) - 1] = k
    return aliases


def _retention(proj, lg, ret_norm, seq_len, n_seq, row_block0, ropes=None, s0=None, emit_state=False, fill=None):
    n_chunks = seq_len // BLOCK
    rope, has_s0 = ropes is not None, s0 is not None

    def col(cb):
        return pl.BlockSpec((seq_len, RET_DK), lambda b, h, cb=cb: (b + row_block0, cb * RET_HEADS + h))

    in_specs = [pl.BlockSpec(memory_space=pltpu.SMEM), col(0), col(1), col(2), col(3)]
    args = [lg, proj, proj, proj, proj]
    if rope:
        in_specs += [pl.BlockSpec((seq_len, RET_DK // 2), lambda b, h: (0, 0))] * 2
        args += list(ropes)
    if has_s0:
        in_specs.append(pl.BlockSpec((None, None, 2, None, RET_DK, RET_DV), lambda b, h: (b, 0, 0, h, 0, 0)))
        args.append(s0)
    in_specs.append(pl.BlockSpec((1, RET_DV), lambda b, h: (0, h)))
    args.append(ret_norm.reshape(1, RET_HEADS * RET_DV))
    aliases = _fill_alias(fill, in_specs, args)
    out_specs = [pl.BlockSpec((seq_len, RET_DV), lambda b, h: (b + row_block0, h))]
    out_shape = [jax.ShapeDtypeStruct((T_ALL, RET_HEADS * RET_DV), BF16)]
    if emit_state:
        out_specs.append(pl.BlockSpec((None, None, 2, None, RET_DK, RET_DV), lambda b, h: (b, 0, 0, h, 0, 0)))
        out_shape.append(jax.ShapeDtypeStruct((n_seq, 1, 2, RET_HEADS, RET_DK, RET_DV), F32))
    return pl.pallas_call(
        functools.partial(_ret_kernel, n_chunks, rope, has_s0, emit_state, bool(fill)),
        grid=(n_seq, RET_HEADS),
        in_specs=in_specs,
        out_specs=out_specs,
        out_shape=out_shape,
        input_output_aliases=aliases,
        scratch_shapes=[pltpu.VMEM((seq_len, RET_DK), BF16), pltpu.VMEM((seq_len, RET_DK), BF16),
                        pltpu.VMEM((n_chunks, RET_DK, RET_DV), BF16), pltpu.VMEM((BLOCK, BLOCK), F32),
                        pltpu.VMEM((4, BLOCK, RET_DV), F32), pltpu.VMEM((RET_DK, RET_DV), F32),
                        pltpu.VMEM((RET_DK, RET_DV), F32)],
        compiler_params=_cparams("arbitrary", "arbitrary"),
        name="retention_latent" if rope else "retention_prompt",
    )(*args)


def _head_norm(x, gain):
    x = x.astype(F32)
    return x * lax.rsqrt(jnp.mean(x * x, axis=-1, keepdims=True) + EPS) * gain


def _rope_full(x, cs, sn):
    return x * cs + pltpu.roll(x, ATT_HD // 2, 1) * sn


ATT_SPLIT = 2


def _sink_column(sink_ref, head0, n_heads, rows_per_head):
    n = n_heads * rows_per_head
    head = lax.broadcasted_iota(jnp.int32, (n, 1), 0) // rows_per_head
    col = jnp.full((n, 1), sink_ref[head0], F32)
    for g in range(1, n_heads):
        col = jnp.where(head == g, sink_ref[head0 + g], col)
    return col


def _att_latent_kernel(sink_ref, q_ref, k_ref, v_ref, ck_ref, cv_ref, cos_ref, sin_ref, qn_ref, kn_ref,
                       fill_ref, o_ref, kp_ref, vp_ref, ckp_ref, cvp_ref):
    del fill_ref
    kh, qb = pl.program_id(1), pl.program_id(2)
    n_chunks = DEC_SEQ // BLOCK
    loc = 3 * BLOCK

    @pl.when(qb == 0)
    def _():
        def prep(c, carry):
            r = _rows(c)
            kp_ref[r, :] = _rope_full(_head_norm(k_ref[r, :], kn_ref[...]), cos_ref[r, :], sin_ref[r, :]).astype(BF16)
            vp_ref[r, :ATT_HD] = v_ref[r, :].astype(BF16)
            vp_ref[r, ATT_HD:] = jnp.ones((BLOCK, ATT_HD), BF16)
            return carry

        lax.fori_loop(0, n_chunks, prep, 0)
        ckp_ref[...] = ck_ref[...].astype(BF16)
        cvp_ref[:, :ATT_HD] = cv_ref[...].astype(BF16)
        cvp_ref[:, ATT_HD:] = jnp.ones((PAST_LEN, ATT_HD), BF16)

    rq = _rows(qb)
    cs, sn = cos_ref[rq, :], sin_ref[rq, :]
    qg = qn_ref[...] * (ATT_HD ** -0.5)
    start = pl.multiple_of(jnp.clip((qb - 1) * BLOCK, 0, DEC_SEQ - loc), BLOCK)
    kl, vl = kp_ref[pl.ds(start, loc), :], vp_ref[pl.ds(start, loc), :]
    qpos = qb * BLOCK + (lax.broadcasted_iota(jnp.int32, (ATT_SPLIT * BLOCK, loc), 0) & (BLOCK - 1))
    kpos = start + lax.broadcasted_iota(jnp.int32, (ATT_SPLIT * BLOCK, loc), 1)
    in_window = jnp.abs(qpos - kpos) <= WINDOW
    for g0 in range(0, ATT_GROUP, ATT_SPLIT):
        q = jnp.concatenate(
            [_rope_full(_head_norm(q_ref[:, g * ATT_HD:(g + 1) * ATT_HD], qg), cs, sn).astype(BF16)
             for g in range(g0, g0 + ATT_SPLIT)], axis=0)
        s_loc = jnp.where(in_window, _bdot_nt(q, kl), NEG_INF)
        s_ctx = _bdot_nt(q, ckp_ref[...])
        sink = _sink_column(sink_ref, kh * ATT_GROUP + g0, ATT_SPLIT, BLOCK)
        m = jnp.maximum(jnp.maximum(jnp.max(s_loc, axis=-1, keepdims=True),
                                    jnp.max(s_ctx, axis=-1, keepdims=True)), sink)
        pv = _bdot(jnp.exp(s_loc - m), vl) + _bdot(jnp.exp(s_ctx - m), cvp_ref[...])
        o = pv[:, :ATT_HD] / (pv[:, ATT_HD:] + jnp.exp(sink - m))
        for g in range(ATT_SPLIT):
            o_ref[:, (g0 + g) * ATT_HD:(g0 + g + 1) * ATT_HD] = o[g * BLOCK:(g + 1) * BLOCK, :].astype(o_ref.dtype)


def _attention_latent(proj, cache_k, cache_v, sink, cos_a, sin_a, q_norm, k_norm, fill):
    nqb = DEC_SEQ // BLOCK
    rb0 = T_PROMPT // DEC_SEQ
    qcol0 = 4 * RET_HEADS * RET_DK // (ATT_GROUP * ATT_HD)
    kcol0 = (4 * RET_HEADS * RET_DK + ATT_HEADS * ATT_HD) // ATT_HD
    vcol0 = kcol0 + ATT_KV_HEADS
    ck = cache_k.reshape(DEC_BATCH, PAST_LEN, ATT_KV_HEADS * ATT_HD)
    cv = cache_v.reshape(DEC_BATCH, PAST_LEN, ATT_KV_HEADS * ATT_HD)
    return pl.pallas_call(
        _att_latent_kernel,
        grid=(DEC_BATCH, ATT_KV_HEADS, nqb),
        in_specs=[pl.BlockSpec(memory_space=pltpu.SMEM),
                  pl.BlockSpec((BLOCK, ATT_GROUP * ATT_HD),
                               lambda b, kh, qb: (T_PROMPT // BLOCK + b * nqb + qb, qcol0 + kh)),
                  pl.BlockSpec((DEC_SEQ, ATT_HD), lambda b, kh, qb: (rb0 + b, kcol0 + kh)),
                  pl.BlockSpec((DEC_SEQ, ATT_HD), lambda b, kh, qb: (rb0 + b, vcol0 + kh)),
                  pl.BlockSpec((None, PAST_LEN, ATT_HD), lambda b, kh, qb: (b, 0, kh)),
                  pl.BlockSpec((None, PAST_LEN, ATT_HD), lambda b, kh, qb: (b, 0, kh)),
                  pl.BlockSpec((DEC_SEQ, ATT_HD), lambda b, kh, qb: (0, 0)),
                  pl.BlockSpec((DEC_SEQ, ATT_HD), lambda b, kh, qb: (0, 0)),
                  pl.BlockSpec((1, ATT_HD), lambda b, kh, qb: (0, 0)),
                  pl.BlockSpec((1, ATT_HD), lambda b, kh, qb: (0, 0)),
                  pl.BlockSpec(memory_space=pl.ANY)],
        out_specs=pl.BlockSpec((BLOCK, ATT_GROUP * ATT_HD),
                               lambda b, kh, qb: (T_PROMPT // BLOCK + b * nqb + qb, kh)),
        out_shape=jax.ShapeDtypeStruct((T_ALL, ATT_HEADS * ATT_HD), BF16),
        input_output_aliases={10: 0},
        scratch_shapes=[pltpu.VMEM((DEC_SEQ, ATT_HD), BF16), pltpu.VMEM((DEC_SEQ, 2 * ATT_HD), BF16),
                        pltpu.VMEM((PAST_LEN, ATT_HD), BF16), pltpu.VMEM((PAST_LEN, 2 * ATT_HD), BF16)],
        compiler_params=_cparams("arbitrary", "arbitrary", "arbitrary"),
        name="attention_latent",
    )(sink, proj, proj, proj, ck, cv, cos_a, sin_a, q_norm.reshape(1, ATT_HD), k_norm.reshape(1, ATT_HD), fill)


def _att_prompt_kernel(sink_ref, q_ref, k_ref, v_ref, qn_ref, kn_ref, fill_ref, o_ref, nk_ref, nv_ref):
    del fill_ref
    kh = pl.program_id(1)
    kn = _head_norm(k_ref[...], kn_ref[...])
    v = v_ref[...]
    nk_ref[...] = kn
    nv_ref[...] = v.astype(F32)
    qg = qn_ref[...] * (ATT_HD ** -0.5)
    q = jnp.concatenate([_head_norm(q_ref[:, g * ATT_HD:(g + 1) * ATT_HD], qg).astype(BF16)
                         for g in range(ATT_GROUP)], axis=0)
    s = _bdot_nt(q, kn)
    sink = _sink_column(sink_ref, kh * ATT_GROUP, ATT_GROUP, SEQ)
    m = jnp.maximum(jnp.max(s, axis=-1, keepdims=True), sink)
    pv = _bdot(jnp.exp(s - m), jnp.concatenate([v.astype(BF16), jnp.ones((SEQ, ATT_HD), BF16)], axis=1))
    o = pv[:, :ATT_HD] / (pv[:, ATT_HD:] + jnp.exp(sink - m))
    for g in range(ATT_GROUP):
        o_ref[:, g * ATT_HD:(g + 1) * ATT_HD] = o[g * SEQ:(g + 1) * SEQ, :].astype(o_ref.dtype)


def _attention_prompt(proj, sink, q_norm, k_norm, fill):
    qcol0 = 4 * RET_HEADS * RET_DK // (ATT_GROUP * ATT_HD)
    kcol0 = (4 * RET_HEADS * RET_DK + ATT_HEADS * ATT_HD) // ATT_HD
    vcol0 = kcol0 + ATT_KV_HEADS
    kv_spec = pl.BlockSpec((None, SEQ, ATT_HD), lambda b, kh: (b, 0, kh))
    kv_shape = jax.ShapeDtypeStruct((BATCH, SEQ, ATT_KV_HEADS * ATT_HD), F32)
    return pl.pallas_call(
        _att_prompt_kernel,
        grid=(BATCH, ATT_KV_HEADS),
        in_specs=[pl.BlockSpec(memory_space=pltpu.SMEM),
                  pl.BlockSpec((SEQ, ATT_GROUP * ATT_HD), lambda b, kh: (b, qcol0 + kh)),
                  pl.BlockSpec((SEQ, ATT_HD), lambda b, kh: (b, kcol0 + kh)),
                  pl.BlockSpec((SEQ, ATT_HD), lambda b, kh: (b, vcol0 + kh)),
                  pl.BlockSpec((1, ATT_HD), lambda b, kh: (0, 0)),
                  pl.BlockSpec((1, ATT_HD), lambda b, kh: (0, 0)),
                  pl.BlockSpec(memory_space=pl.ANY)],
        out_specs=[pl.BlockSpec((SEQ, ATT_GROUP * ATT_HD), lambda b, kh: (b, kh)), kv_spec, kv_spec],
        out_shape=[jax.ShapeDtypeStruct((T_ALL, ATT_HEADS * ATT_HD), BF16), kv_shape, kv_shape],
        input_output_aliases={6: 0},
        compiler_params=_cparams("arbitrary", "arbitrary"),
        name="attention_prompt",
    )(sink, proj, proj, proj, q_norm.reshape(1, ATT_HD), k_norm.reshape(1, ATT_HD), fill)


def _split_dot(m01, a):
    hi = a.astype(BF16)
    r1 = a - hi.astype(F32)
    mid = r1.astype(BF16)
    lo = (r1 - mid.astype(F32)).astype(BF16)
    return (jnp.dot(m01, hi, preferred_element_type=F32) + jnp.dot(m01, mid, preferred_element_type=F32)
            + jnp.dot(m01, lo, preferred_element_type=F32))


PREP_CHUNKS = ROW_TILE // BLOCK


def _ssd_prep_kernel(raw_ref, bias_ref, alog_ref, cum_ref, dt_ref, w_ref, tot_ref, ecum_ref):
    ii = lax.broadcasted_iota(jnp.int32, (BLOCK, BLOCK), 0)
    jj = lax.broadcasted_iota(jnp.int32, (BLOCK, BLOCK), 1)
    lower = jnp.where(jj <= ii, 1.0, 0.0).astype(BF16)
    upper = jnp.where(jj >= ii, 1.0, 0.0).astype(BF16)
    fwd_lane = lax.broadcasted_iota(jnp.int32, (BLOCK, LANES), 1) < SSD_HEADS
    neg_a = -jnp.exp(alog_ref[...])

    def chunk(k, carry):
        x = raw_ref[_rows(k), :] + bias_ref[...]
        dt = jnp.maximum(x, 0.0) + jnp.log1p(jnp.exp(-jnp.abs(x)))
        a = dt * neg_a
        incl = _split_dot(lower, a)
        rincl = _split_dot(upper, a)
        cum = jnp.where(fwd_lane, incl, rincl)
        tot = jnp.where(fwd_lane[:1], incl[BLOCK - 1:BLOCK, :], rincl[0:1, :])
        cum_ref[k] = cum.T
        dt_ref[k] = dt.T
        w_ref[k] = (dt * jnp.exp(tot - cum)).T
        tot_ref[k] = jnp.broadcast_to(jnp.exp(tot), (BLOCK, LANES)).T
        ecum_ref[k] = jnp.exp(cum).T
        return carry

    lax.fori_loop(0, PREP_CHUNKS, chunk, 0)


def _ssd_prep(dt_raw, dt_bias, a_log):
    nc = T_ALL // BLOCK
    spec = pl.BlockSpec((PREP_CHUNKS, 2 * SSD_HEADS, BLOCK), lambda c: (c, 0, 0))
    shape = jax.ShapeDtypeStruct((nc, 2 * SSD_HEADS, BLOCK), F32)
    return pl.pallas_call(
        _ssd_prep_kernel,
        grid=(nc // PREP_CHUNKS,),
        in_specs=[pl.BlockSpec((ROW_TILE, 2 * SSD_HEADS), lambda c: (c, 0)),
                  pl.BlockSpec((1, 2 * SSD_HEADS), lambda c: (0, 0)),
                  pl.BlockSpec((1, 2 * SSD_HEADS), lambda c: (0, 0))],
        out_specs=[spec] * 5,
        out_shape=[shape] * 5,
        compiler_params=_cparams("arbitrary"),
        name="ssd_prep",
    )(dt_raw, dt_bias.reshape(1, 2 * SSD_HEADS), a_log.reshape(1, 2 * SSD_HEADS))


GW = SSD_R * SSD_P
HALO = 8


def _pair_tiles(per_head):
    low = lax.broadcasted_iota(jnp.int32, per_head[0].shape, 1) < SSD_P
    return jnp.concatenate([jnp.where(low, per_head[2 * t], per_head[2 * t + 1]) for t in range(SSD_R // 2)],
                           axis=1)


def _row_bcast(ref, c, r):
    return jnp.broadcast_to(ref[c, r:r + 1, :], (BLOCK, BLOCK))


def _pair_cols(ref, c):
    top = lax.broadcasted_iota(jnp.int32, (BLOCK, BLOCK), 0) < SSD_P
    return jnp.concatenate(
        [jnp.where(top, _row_bcast(ref, c, 2 * t), _row_bcast(ref, c, 2 * t + 1)).T for t in range(SSD_R // 2)],
        axis=1)


def _ssd_kernel(n_chunks, has_s0, emit_state, has_fill, *refs):
    it = iter(refs)
    z_ref, x_ref, b_ref, c_ref = (next(it) for _ in range(4))
    cumf_ref, cumb_ref, dtf_ref, dtb_ref, wf_ref, wb_ref, totf_ref, totb_ref, ecf_ref, ecb_ref = (
        next(it) for _ in range(10))
    cwx_ref, cwb_ref, cwc_ref, cbx_ref, cbb_ref, cbc_ref, d_ref, ng_ref = (next(it) for _ in range(8))
    s0_ref = next(it) if has_s0 else None
    if has_fill:
        next(it), next(it)
    o_ref, ss_ref = next(it), next(it)
    sfin_ref = next(it) if emit_state else None
    pad_ref, xc_ref, bc_ref, cc_ref, sfs_ref, sf_ref, sb_ref = (next(it) for _ in range(7))
    seq_len = n_chunks * BLOCK

    pad_ref[0:HALO, :] = jnp.zeros((HALO, GW + 2 * SSD_N), F32)
    pad_ref[HALO + seq_len:2 * HALO + seq_len, :] = jnp.zeros((HALO, GW + 2 * SSD_N), F32)

    def fill(c, carry):
        dst = pl.ds(pl.multiple_of(c * BLOCK, BLOCK) + HALO, BLOCK)
        r = _rows(c)
        pad_ref[dst, 0:GW] = x_ref[r, :].astype(F32)
        pad_ref[dst, GW:GW + SSD_N] = b_ref[r, :].astype(F32)
        pad_ref[dst, GW + SSD_N:GW + 2 * SSD_N] = c_ref[r, :].astype(F32)
        return carry

    lax.fori_loop(0, n_chunks, fill, 0)

    def conv(c):
        r = _rows(c)
        src = pl.ds(pl.multiple_of(c * BLOCK, BLOCK), BLOCK + 2 * HALO)
        for col0, width, cw_ref, cb_ref, dst in ((0, GW, cwx_ref, cbx_ref, xc_ref),
                                                 (GW, SSD_N, cwb_ref, cbb_ref, bc_ref),
                                                 (GW + SSD_N, SSD_N, cwc_ref, cbc_ref, cc_ref)):
            for t in range(width // LANES):
                tl = slice(t * LANES, (t + 1) * LANES)
                win = pad_ref[src, col0 + t * LANES:col0 + (t + 1) * LANES]
                acc = jnp.broadcast_to(cb_ref[:, tl], (BLOCK, LANES))
                for w in range(CONV_W):
                    off = HALO - CONV_W // 2 + w
                    acc = acc + win[off:off + BLOCK, :] * cw_ref[w:w + 1, tl]
                dst[r, tl] = _silu(acc).astype(dst.dtype)

    if has_s0:
        for r in range(SSD_R):
            sf_ref[:, r * SSD_P:(r + 1) * SSD_P] = s0_ref[0, r]
            sb_ref[:, r * SSD_P:(r + 1) * SSD_P] = s0_ref[1, r]
    else:
        sf_ref[...] = jnp.zeros_like(sf_ref)
        sb_ref[...] = jnp.zeros_like(sb_ref)

    def state_update(s_ref, c, w_ref, tot_ref, bm, xs):
        tot = _pair_tiles([tot_ref[c, r:r + 1, :] for r in range(SSD_R)])
        s_ref[...] = s_ref[...] * tot + _bdot_tn(bm, xs * _pair_cols(w_ref, c))

    def fwd_step(c):
        r = _rows(c)
        sfs_ref[c] = sf_ref[...].astype(BF16)
        state_update(sf_ref, c, wf_ref, totf_ref, bc_ref[r, :], xc_ref[r, :])

    def fwd(c, carry):
        fwd_step(c)
        conv(c + 1)
        return carry

    conv(0)
    lax.fori_loop(0, n_chunks - 1, fwd, 0, unroll=3 if (n_chunks - 1) % 3 == 0 else 1)
    fwd_step(n_chunks - 1)
    if emit_state:
        for r in range(SSD_R):
            sfin_ref[0, r] = sf_ref[:, r * SSD_P:(r + 1) * SSD_P]

    ii = lax.broadcasted_iota(jnp.int32, (BLOCK, BLOCK), 0)
    jj = lax.broadcasted_iota(jnp.int32, (BLOCK, BLOCK), 1)
    causal = jj <= ii
    anti = ii <= jj
    low = lax.broadcasted_iota(jnp.int32, (BLOCK, LANES), 1) < SSD_P

    def bwd(t, carry):
        c = n_chunks - 1 - t
        r = _rows(c)
        cm, bm, xs = cc_ref[r, :], bc_ref[r, :], xc_ref[r, :]
        xb = xs.astype(BF16)
        sc = _bdot_nt(cm, bm)
        yf = _bdot(cm, sfs_ref[c])
        yb = _bdot(cm, sb_ref[...])
        tiles = []
        for t2 in range(SSD_R // 2):
            xt = xb[:, t2 * LANES:(t2 + 1) * LANES]
            acc = None
            for hh in range(2):
                hr = 2 * t2 + hh
                rf, rb = _row_bcast(cumf_ref, c, hr), _row_bcast(cumb_ref, c, hr)
                e = jnp.exp(jnp.where(anti, rf, rb).T - jnp.where(causal, rf, rb))
                e = e * jnp.where(causal, _row_bcast(dtf_ref, c, hr), _row_bcast(dtb_ref, c, hr))
                xh = jnp.where(low, xt, 0.0) if hh == 0 else jnp.where(low, 0.0, xt)
                part = _bdot(sc * e, xh)
                acc = part if acc is None else acc + part
            tiles.append(acc)
        y = (jnp.concatenate(tiles, axis=1) + _pair_cols(ecf_ref, c) * yf + _pair_cols(ecb_ref, c) * yb
             + d_ref[...] * xs)
        yg = y * _silu(z_ref[r, :].astype(F32))
        ss_ref[r, :] = jnp.broadcast_to(jnp.sum(yg * yg, axis=-1, keepdims=True), (BLOCK, LANES))
        o_ref[r, :] = (yg * ng_ref[...]).astype(o_ref.dtype)
        state_update(sb_ref, c, wb_ref, totb_ref, bm, xs)
        return carry

    lax.fori_loop(0, n_chunks, bwd, 0, unroll=min(4, n_chunks))
    if emit_state:
        for r in range(SSD_R):
            sfin_ref[1, r] = sb_ref[:, r * SSD_P:(r + 1) * SSD_P]


def _ssd_scan(zx, prep, conv_w, conv_b, d_exp, norm_gain, seq_len, n_seq, row_block0, s0=None, emit_state=False,
              fill=None):
    n_chunks = seq_len // BLOCK
    has_s0 = s0 is not None
    xcol0 = D_INNER // GW
    bcol0 = 2 * D_INNER // SSD_N
    ccol0 = bcol0 + SSD_GROUPS
    cwb0 = D_INNER // SSD_N

    def rowcol(width, col0):
        return pl.BlockSpec((seq_len, width), lambda b, g, col0=col0: (b + row_block0, col0 + g))

    def headrows(direction):
        return pl.BlockSpec((n_chunks, SSD_R, BLOCK),
                            lambda b, g, direction=direction: (b + row_block0, direction * SSD_GROUPS + g, 0))

    in_specs = [rowcol(GW, 0), rowcol(GW, xcol0), rowcol(SSD_N, bcol0), rowcol(SSD_N, ccol0)]
    args = [zx, zx, zx, zx]
    for arr in prep:
        in_specs += [headrows(0), headrows(1)]
        args += [arr, arr]
    in_specs += [pl.BlockSpec((CONV_W, GW), lambda b, g: (0, g)),
                 pl.BlockSpec((CONV_W, SSD_N), lambda b, g: (0, cwb0 + g)),
                 pl.BlockSpec((CONV_W, SSD_N), lambda b, g: (0, cwb0 + SSD_GROUPS + g)),
                 pl.BlockSpec((1, GW), lambda b, g: (0, g)),
                 pl.BlockSpec((1, SSD_N), lambda b, g: (0, cwb0 + g)),
                 pl.BlockSpec((1, SSD_N), lambda b, g: (0, cwb0 + SSD_GROUPS + g)),
                 pl.BlockSpec((1, GW), lambda b, g: (0, g)),
                 pl.BlockSpec((1, GW), lambda b, g: (0, g))]
    args += [conv_w, conv_w, conv_w, conv_b, conv_b, conv_b, d_exp, norm_gain.reshape(1, D_INNER)]
    state_spec = pl.BlockSpec((None, None, 2, SSD_R, SSD_N, SSD_P), lambda b, g: (b, 0, 0, g, 0, 0))
    if has_s0:
        in_specs.append(state_spec)
        args.append(s0)
    aliases = _fill_alias(fill, in_specs, args)
    out_specs = [pl.BlockSpec((seq_len, GW), lambda b, g: (b + row_block0, g)),
                 pl.BlockSpec((seq_len, LANES), lambda b, g: (b + row_block0, g))]
    out_shape = [jax.ShapeDtypeStruct((T_ALL, D_INNER), BF16),
                 jax.ShapeDtypeStruct((T_ALL, SSD_GROUPS * LANES), F32)]
    if emit_state:
        out_specs.append(state_spec)
        out_shape.append(jax.ShapeDtypeStruct((n_seq, 1, 2, SSD_HEADS, SSD_N, SSD_P), F32))
    return pl.pallas_call(
        functools.partial(_ssd_kernel, n_chunks, has_s0, emit_state, bool(fill)),
        grid=(n_seq, SSD_GROUPS),
        in_specs=in_specs,
        out_specs=out_specs,
        out_shape=out_shape,
        input_output_aliases=aliases,
        scratch_shapes=[pltpu.VMEM((seq_len + 2 * HALO, GW + 2 * SSD_N), F32),
                        pltpu.VMEM((seq_len, GW), F32), pltpu.VMEM((seq_len, SSD_N), BF16),
                        pltpu.VMEM((seq_len, SSD_N), BF16), pltpu.VMEM((n_chunks, SSD_N, GW), BF16),
                        pltpu.VMEM((SSD_N, GW), F32), pltpu.VMEM((SSD_N, GW), F32)],
        compiler_params=_cparams("arbitrary", "arbitrary"),
        name="ssd_scan_latent" if has_s0 else "ssd_scan_prompt",
    )(*args)


def _router_kernel(x_ref, g_ref, mod_ref, rw_ref, hn_ref, idx_ref, wgt_ref, hf_ref):
    rw = rw_ref[...]
    r_hi = rw.astype(BF16)
    r_lo = (rw - r_hi.astype(F32)).astype(BF16)
    r_both = jnp.concatenate([r_hi, r_lo], axis=1)

    def route(r):
        hf = hf_ref[r, :]
        h_hi = hf.astype(BF16)
        h_lo = (hf - h_hi.astype(F32)).astype(BF16)
        hn_ref[r, :] = h_hi
        t1 = jnp.dot(h_hi, r_both, preferred_element_type=F32)
        logits = t1[:, :LANES] + t1[:, LANES:] + jnp.dot(h_lo, r_hi, preferred_element_type=F32)
        lane = lax.broadcasted_iota(jnp.int32, logits.shape, 1)
        lg = jnp.where(lane < N_EXPERTS, logits, NEG_INF)
        m1 = jnp.max(lg, axis=-1, keepdims=True)
        i1 = jnp.min(jnp.where(lg == m1, lane, LANES), axis=-1, keepdims=True)
        lg2 = jnp.where(lane == i1, NEG_INF, lg)
        m2 = jnp.max(lg2, axis=-1, keepdims=True)
        i2 = jnp.min(jnp.where(lg2 == m2, lane, LANES), axis=-1, keepdims=True)
        e2 = jnp.exp(m2 - m1)
        w1 = 1.0 / (1.0 + e2)
        idx_ref[r, :] = jnp.where(lane == 0, i1, jnp.where(lane == 1, i2, 0))
        wgt_ref[r, :] = jnp.where(lane == 0, w1, jnp.where(lane == 1, e2 * w1, 0.0))

    _adaln_then(x_ref, g_ref, mod_ref, 3, 4, hf_ref, route)


def _router(x, gain, modt, router_w):
    t, d = x.shape
    rw = jnp.pad(router_w, ((0, 0), (0, LANES - N_EXPERTS)))
    return pl.pallas_call(
        _router_kernel,
        grid=(t // ROW_TILE,),
        in_specs=[pl.BlockSpec((ROW_TILE, d), lambda i: (i, 0)),
                  pl.BlockSpec((1, d), lambda i: (0, 0)),
                  pl.BlockSpec((None, 8, d), lambda i: (i, 0, 0)),
                  pl.BlockSpec((d, LANES), lambda i: (0, 0))],
        out_specs=[pl.BlockSpec((ROW_TILE, d), lambda i: (i, 0)),
                   pl.BlockSpec((ROW_TILE, LANES), lambda i: (i, 0)),
                   pl.BlockSpec((ROW_TILE, LANES), lambda i: (i, 0))],
        out_shape=[jax.ShapeDtypeStruct((t, d), BF16), jax.ShapeDtypeStruct((t, LANES), jnp.int32),
                   jax.ShapeDtypeStruct((t, LANES), F32)],
        scratch_shapes=[pltpu.VMEM((ROW_TILE, d), F32)],
        compiler_params=_cparams("arbitrary"),
        name="moe_router",
    )(x, gain.reshape(1, d), modt, rw)


DOWN_ROWS = 512
DOWN_TN = 1024


def _expert_changed(be_ref, blk, prev_blk, step):
    return jnp.logical_or(step == 0, be_ref[blk] != be_ref[prev_blk])


def _expert_up_kernel(be_ref, nu_ref, nv_ref, xs_ref, wg_ref, wu_ref, h_ref, wgb_ref, wub_ref):
    i = pl.program_id(1)
    used = nv_ref[i] > 0

    @pl.when(jnp.logical_and(used, _expert_changed(be_ref, i, jnp.maximum(i - 1, 0), i)))
    def _():
        wgb_ref[...] = wg_ref[...].astype(BF16)
        wub_ref[...] = wu_ref[...].astype(BF16)

    @pl.when(used)
    def _():
        xs = xs_ref[...]
        h_ref[...] = (_silu(_bdot(xs, wgb_ref[...])) * _bdot(xs, wub_ref[...])).astype(h_ref.dtype)

    @pl.when(jnp.logical_not(used))
    def _():
        h_ref[...] = jnp.zeros_like(h_ref)


def _expert_down_kernel(be_ref, nu_ref, nv_ref, h_ref, wd_ref, o_ref, wdb_ref):
    i = pl.program_id(1)
    per = MOE_ROWS // DOWN_ROWS
    blk = i // per
    used = nv_ref[blk] > (i % per) * DOWN_ROWS

    @pl.when(jnp.logical_and(nv_ref[blk] > 0, _expert_changed(be_ref, blk, jnp.maximum(i - 1, 0) // per, i)))
    def _():
        wdb_ref[...] = wd_ref[...].astype(BF16)

    @pl.when(used)
    def _():
        o_ref[...] = _bdot(h_ref[...], wdb_ref[...]).astype(o_ref.dtype)

    @pl.when(jnp.logical_not(used))
    def _():
        o_ref[...] = jnp.zeros_like(o_ref)


def _experts(xs_sorted, block_e, n_used, n_valid, wg, wu, wd):
    cap, d = xs_sorted.shape
    ff = wg.shape[2]
    per = MOE_ROWS // DOWN_ROWS

    def expert_of(blk, be, nu):
        return be[jnp.minimum(blk, jnp.maximum(nu[0] - 1, 0))]

    h = pl.pallas_call(
        _expert_up_kernel,
        grid_spec=pltpu.PrefetchScalarGridSpec(
            num_scalar_prefetch=3,
            grid=(ff // FFN_TF, cap // MOE_ROWS),
            in_specs=[pl.BlockSpec((MOE_ROWS, d), lambda f, i, be, nu, nv: (i, 0)),
                      pl.BlockSpec((None, d, FFN_TF), lambda f, i, be, nu, nv: (expert_of(i, be, nu), 0, f)),
                      pl.BlockSpec((None, d, FFN_TF), lambda f, i, be, nu, nv: (expert_of(i, be, nu), 0, f))],
            out_specs=pl.BlockSpec((MOE_ROWS, FFN_TF), lambda f, i, be, nu, nv: (i, f)),
            scratch_shapes=[pltpu.VMEM((d, FFN_TF), BF16), pltpu.VMEM((d, FFN_TF), BF16)],
        ),
        out_shape=jax.ShapeDtypeStruct((cap, ff), BF16),
        compiler_params=_cparams("arbitrary", "arbitrary"),
        name="moe_expert_up",
    )(block_e, n_used, n_valid, xs_sorted, wg, wu)
    return pl.pallas_call(
        _expert_down_kernel,
        grid_spec=pltpu.PrefetchScalarGridSpec(
            num_scalar_prefetch=3,
            grid=(d // DOWN_TN, cap // DOWN_ROWS),
            in_specs=[pl.BlockSpec((DOWN_ROWS, ff), lambda n, i, be, nu, nv: (i, 0)),
                      pl.BlockSpec((None, ff, DOWN_TN),
                                   lambda n, i, be, nu, nv: (expert_of(i // per, be, nu), 0, n),
                                   pipeline_mode=pl.Buffered(1))],
            out_specs=pl.BlockSpec((DOWN_ROWS, DOWN_TN), lambda n, i, be, nu, nv: (i, n)),
            scratch_shapes=[pltpu.VMEM((ff, DOWN_TN), BF16)],
        ),
        out_shape=jax.ShapeDtypeStruct((cap, d), BF16),
        compiler_params=_cparams("arbitrary", "arbitrary"),
        name="moe_expert_down",
    )(block_e, n_used, n_valid, h, wd)


def _combine_kernel(x_ref, g0_ref, g1_ref, w_ref, mod_ref, o_ref):
    w = w_ref[...]
    y = w[:, 0:1] * g0_ref[...].astype(F32) + w[:, 1:2] * g1_ref[...].astype(F32)
    o_ref[...] = x_ref[...] + mod_ref[5:6, :] * y


COMB_ROWS = ROW_TILE


def _combine(x, g, wgt, modt, row0, n_rows):
    t, d = x.shape
    b0 = row0 // COMB_ROWS
    per = ROW_TILE // COMB_ROWS
    return pl.pallas_call(
        _combine_kernel,
        grid=(n_rows // COMB_ROWS,),
        in_specs=[pl.BlockSpec((COMB_ROWS, d), lambda i: (i + b0, 0)),
                  pl.BlockSpec((COMB_ROWS, d), lambda i: (i + b0, 0)),
                  pl.BlockSpec((COMB_ROWS, d), lambda i: (i + b0 + t // COMB_ROWS, 0)),
                  pl.BlockSpec((COMB_ROWS, LANES), lambda i: (i + b0, 0)),
                  pl.BlockSpec((None, 8, d), lambda i: ((i + b0) // per, 0, 0))],
        out_specs=pl.BlockSpec((COMB_ROWS, d), lambda i: (i, 0)),
        out_shape=jax.ShapeDtypeStruct((n_rows, d), F32),
        compiler_params=_cparams("arbitrary"),
        name="moe_combine",
    )(x, g, g, wgt, modt)


def _moe(x, gain, modt, router_w, wg, wu, wd):
    t, d = x.shape
    hn, idx, wgt = _router(x, gain, modt, router_w)
    top_idx = idx[:, :TOP_K]
    n_slots = t * TOP_K
    flat_e = top_idx.reshape(-1)
    onehot = (flat_e[:, None] == jnp.arange(N_EXPERTS, dtype=jnp.int32)[None, :]).astype(jnp.int32)
    incl = jnp.cumsum(onehot, axis=0)
    counts = incl[-1]
    rank = jnp.sum((incl - onehot) * onehot, axis=1)
    padded = (counts + MOE_ROWS - 1) // MOE_ROWS * MOE_ROWS
    pend = jnp.cumsum(padded)
    pstart = pend - padded
    dest = pstart[flat_e] + rank
    n_blocks = n_slots // MOE_ROWS + N_EXPERTS
    cap = n_blocks * MOE_ROWS
    row_tok = (jnp.arange(cap, dtype=jnp.int32) % t).at[dest].set(
        jnp.arange(n_slots, dtype=jnp.int32) // TOP_K, unique_indices=True, mode='promise_in_bounds')
    block_e = jnp.clip(jnp.searchsorted(pend, jnp.arange(n_blocks, dtype=jnp.int32) * MOE_ROWS, side='right'),
                       0, N_EXPERTS - 1).astype(jnp.int32)
    n_used = (pend[-1:] // MOE_ROWS).astype(jnp.int32)
    blk = jnp.arange(n_blocks, dtype=jnp.int32)
    n_valid = jnp.where(blk < n_used[0],
                        jnp.clip((pstart + counts)[block_e] - blk * MOE_ROWS, 0, MOE_ROWS), 0).astype(jnp.int32)
    out = _experts(hn.at[row_tok].get(mode='promise_in_bounds'), block_e, n_used, n_valid, wg, wu, wd)
    g = out.at[dest.reshape(t, TOP_K).T.reshape(-1)].get(mode='promise_in_bounds', unique_indices=True)
    return _combine(x, g, wgt, modt, 0, T_PROMPT), _combine(x, g, wgt, modt, T_PROMPT, T_SAMPLE)


def _rope_tables(n_tokens, dim):
    n_rows = n_tokens // GRID_W
    row = jnp.repeat(jnp.arange(n_rows), GRID_W).astype(F32)
    col = jnp.tile(jnp.arange(GRID_W), n_rows).astype(F32)
    n_freq = dim // 4
    inv = ROPE_BASE ** (-jnp.arange(n_freq, dtype=F32) / n_freq)
    ang = jnp.concatenate([row[:, None] * inv, col[:, None] * inv], axis=-1)
    return jnp.cos(ang), jnp.sin(ang)


def kernel(x_prompt, x_sample, state_ret, cache_k, cache_v, state_ssd, c, c_ctx, ada_w, ada_b, norm_mix, norm_ffn, ev_w_in, ev_w_out, ret_decay_logit, ret_norm, att_q_norm, att_k_norm, att_sink, ffn_w_gate, ffn_w_up, ffn_w_down, ssd_w_in, ssd_conv_w, ssd_conv_b, ssd_a_log, ssd_dt_bias, ssd_d, ssd_norm, ssd_w_out, moe_router, moe_w_gate, moe_w_up, moe_w_down):
    d = D_MODEL
    x = (x_prompt.reshape(T_PROMPT, d), x_sample.reshape(T_SAMPLE, d))

    cvecs = jnp.concatenate([c_ctx[None, :], c, jnp.zeros((MOD_ROWS - 1 - DEC_BATCH, d), F32)], axis=0)
    mods = _modulation(cvecs, ada_w, ada_b).reshape(2, MOD_ROWS, 6, d)
    tiles_per_seq = DEC_SEQ // ROW_TILE
    tile_row = jnp.concatenate([jnp.zeros((T_PROMPT // ROW_TILE,), jnp.int32),
                                1 + jnp.arange(T_SAMPLE // ROW_TILE, dtype=jnp.int32) // tiles_per_seq])
    modt = jnp.pad(mods[:, tile_row], ((0, 0), (0, 0), (0, 2), (0, 0)))

    ev_w_in_b, ev_w_out_b = ev_w_in[0].astype(BF16), ev_w_out[0].astype(BF16)
    ffn_wd_b = ffn_w_down[0].astype(BF16)
    ssd_w_in_b, ssd_w_out_b = ssd_w_in[0].astype(BF16), ssd_w_out[0].astype(BF16)

    proj, = _adaln_matmul(x, norm_mix[0], modt[0], 0, 1, ev_w_in_b, tn=512, out_dtype=BF16, name="even_in_proj")
    lg = jax.nn.log_sigmoid(ret_decay_logit[0].astype(F32))
    cos_r, sin_r = _rope_tables(DEC_SEQ, RET_DK)
    cos_a, sin_a = _rope_tables(DEC_SEQ, ATT_HD)
    cos_a2 = jnp.concatenate([cos_a, cos_a], axis=-1)
    sin_a2 = jnp.concatenate([-sin_a, sin_a], axis=-1)
    mix_ret, new_state_ret = _retention(proj, lg, ret_norm[0], SEQ, BATCH, 0, emit_state=True,
                                        fill=(jnp.zeros((T_ALL, RET_HEADS * RET_DV), BF16),))
    mix_ret, = _retention(proj, lg, ret_norm[0], DEC_SEQ, DEC_BATCH, T_PROMPT // DEC_SEQ,
                          ropes=(cos_r, sin_r), s0=state_ret, fill=(mix_ret,))
    mix_att, new_k, new_v = _attention_prompt(proj, att_sink[0], att_q_norm[0], att_k_norm[0],
                                              jnp.zeros((T_ALL, ATT_HEADS * ATT_HD), BF16))
    mix_att = _attention_latent(proj, cache_k[:, 0], cache_v[:, 0], att_sink[0], cos_a2, sin_a2,
                                att_q_norm[0], att_k_norm[0], mix_att)
    x = _proj_residual([mix_ret, mix_att], ev_w_out_b, x, modt[0], 2, tn=1024, name="even_out_proj")
    h = _ffn_gateup(x, norm_ffn[0], modt[0], ffn_w_gate[0], ffn_w_up[0])
    x = _proj_residual([h], ffn_wd_b, x, modt[0], 5, tn=512, name="ffn_down")

    zx, dt_raw = _adaln_matmul(x, norm_mix[1], modt[1], 0, 1, ssd_w_in_b, tn=1024, n_out=SSD_ZX,
                               tail=2 * SSD_HEADS, out_dtype=BF16, name="ssd_in_proj")
    prep = _ssd_prep(dt_raw, ssd_dt_bias[0], ssd_a_log[0])
    d_exp = jnp.repeat(ssd_d[0], SSD_P)[None, :]
    conv_b = ssd_conv_b[0][None, :]
    yg, yss, new_state_ssd = _ssd_scan(zx, prep, ssd_conv_w[0], conv_b, d_exp, ssd_norm[0], SEQ, BATCH, 0,
                                       emit_state=True, fill=(jnp.zeros((T_ALL, D_INNER), BF16),
                                                              jnp.zeros((T_ALL, SSD_GROUPS * LANES), F32)))
    yg, yss = _ssd_scan(zx, prep, ssd_conv_w[0], conv_b, d_exp, ssd_norm[0], DEC_SEQ, DEC_BATCH,
                        T_PROMPT // DEC_SEQ, s0=state_ssd, fill=(yg, yss))
    x = _proj_residual([yg], ssd_w_out_b, x, modt[1], 2, tn=512, row_ss=yss, name="ssd_out_proj")
    y_p, y_s = _moe(x, norm_ffn[1], modt[1], moe_router[0], moe_w_gate[0], moe_w_up[0], moe_w_down[0])

    y_prompt = y_p.reshape(BATCH, SEQ, d)
    y_sample = y_s.reshape(DEC_BATCH, DEC_SEQ, d)
    new_cache_k = new_k.reshape(BATCH, 1, SEQ, ATT_KV_HEADS, ATT_HD)
    new_cache_v = new_v.reshape(BATCH, 1, SEQ, ATT_KV_HEADS, ATT_HD)
    return (y_prompt, y_sample, new_state_ret, new_cache_k, new_cache_v, new_state_ssd)
```

```python
import functools

import jax
import jax.numpy as jnp
from jax import lax
from jax.experimental import pallas as pl
from jax.experimental.pallas import tpu as pltpu

F32 = jnp.float32
BF16 = jnp.bfloat16

D_MODEL = 2048
BATCH = 16
SEQ = 256
DEC_BATCH = 8
DEC_SEQ = 2048
PAST_LEN = 512
GRID_W = 64
BLOCK = 128
WINDOW = 128
EPS = 1e-6
ROPE_BASE = 10000.0
RET_HEADS = 4
RET_DK = 256
RET_DV = 256
ATT_HEADS = 8
ATT_KV_HEADS = 2
ATT_HD = 128
ATT_GROUP = ATT_HEADS // ATT_KV_HEADS
EVEN_IN = 5632
D_INNER = 2 * D_MODEL
SSD_P = 64
SSD_HEADS = D_INNER // SSD_P
SSD_N = 128
SSD_GROUPS = 8
SSD_R = SSD_HEADS // SSD_GROUPS
CONV_W = 5
CONV_CH = D_INNER + 2 * SSD_GROUPS * SSD_N
SSD_ZX = D_INNER + CONV_CH
D_FF = 5632
N_EXPERTS = 8
TOP_K = 2

T_PROMPT = BATCH * SEQ
T_SAMPLE = DEC_BATCH * DEC_SEQ
T_ALL = T_PROMPT + T_SAMPLE

LANES = 128
ROW_TILE = 1024
VMEM_LIMIT = 56 * 1024 * 1024
MOE_ROWS = 1024
NEG_INF = float("-inf")


def _cparams(*sem):
    return pltpu.CompilerParams(dimension_semantics=sem, vmem_limit_bytes=VMEM_LIMIT)


def _silu(x):
    return x * jax.nn.sigmoid(x)


def _bdot(a, b):
    return jnp.dot(a.astype(BF16), b.astype(BF16), preferred_element_type=F32)


def _bdot_nt(a, b):
    return lax.dot_general(a.astype(BF16), b.astype(BF16), (((1,), (1,)), ((), ())),
                           preferred_element_type=F32)


def _bdot_tn(a, b):
    return lax.dot_general(a.astype(BF16), b.astype(BF16), (((0,), (0,)), ((), ())),
                           preferred_element_type=F32)


def _rows(c, n=BLOCK):
    return pl.ds(pl.multiple_of(c * n, n), n)


MOD_ROWS = 16
MOD_TN = 1024


def _mod_kernel(c_ref, w_ref, b_ref, o_ref):
    o_ref[...] = _bdot(_silu(c_ref[...]), w_ref[...]) + b_ref[...]


def _modulation(cvecs, ada_w, ada_b):
    depth, d, n = ada_w.shape
    return pl.pallas_call(
        _mod_kernel,
        grid=(depth, n // MOD_TN),
        in_specs=[pl.BlockSpec((MOD_ROWS, d), lambda l, j: (0, 0)),
                  pl.BlockSpec((None, d, MOD_TN), lambda l, j: (l, 0, j)),
                  pl.BlockSpec((None, 1, MOD_TN), lambda l, j: (l, 0, j))],
        out_specs=pl.BlockSpec((None, MOD_ROWS, MOD_TN), lambda l, j: (l, 0, j)),
        out_shape=jax.ShapeDtypeStruct((depth, MOD_ROWS, n), F32),
        compiler_params=_cparams("arbitrary", "arbitrary"),
        name="modulation",
    )(cvecs, ada_w, ada_b.reshape(depth, 1, n))


ADALN_CHUNK = 64
ADALN_DOT_ROWS = 256


def _adaln_then(x_ref, g_ref, mod_ref, shift_row, scale_row, hn_ref, consume):
    mult = g_ref[...] * (1.0 + mod_ref[scale_row:scale_row + 1, :])
    shift = mod_ref[shift_row:shift_row + 1, :]
    for r0 in range(0, x_ref.shape[0], ADALN_DOT_ROWS):
        for s0 in range(r0, r0 + ADALN_DOT_ROWS, ADALN_CHUNK):
            r = slice(s0, s0 + ADALN_CHUNK)
            x = x_ref[r, :]
            ms = jnp.mean(x * x, axis=-1, keepdims=True)
            hn_ref[r, :] = (x * lax.rsqrt(ms + EPS) * mult + shift).astype(hn_ref.dtype)
        consume(slice(r0, r0 + ADALN_DOT_ROWS))


PROMPT_TILES = T_PROMPT // ROW_TILE


def _x_rows_specs(x, cols, col_of_j):
    if not isinstance(x, tuple):
        return [pl.BlockSpec((ROW_TILE, cols), lambda i, j: (i, col_of_j(j)))], [x]
    return ([pl.BlockSpec((ROW_TILE, cols),
                          lambda i, j: (jnp.minimum(i, PROMPT_TILES - 1), jnp.where(i < PROMPT_TILES, col_of_j(j), 0))),
             pl.BlockSpec((ROW_TILE, cols),
                          lambda i, j: (jnp.maximum(i - PROMPT_TILES, 0), jnp.where(i < PROMPT_TILES, 0, col_of_j(j))))],
            list(x))


def _for_x_rows(x_refs, fn):
    if len(x_refs) == 1:
        fn(x_refs[0])
        return
    in_prompt = pl.program_id(0) < PROMPT_TILES
    pl.when(in_prompt)(lambda: fn(x_refs[0]))
    pl.when(jnp.logical_not(in_prompt))(lambda: fn(x_refs[1]))


def _adaln_mm_kernel(shift_row, scale_row, has_tail, n_x, *refs):
    x_refs, (g_ref, mod_ref, w_ref), refs = refs[:n_x], refs[n_x:n_x + 3], refs[n_x + 3:]
    if has_tail:
        wt_ref, o_ref, ot_ref, hn_ref = refs
    else:
        o_ref, hn_ref = refs

    def project(r):
        hn = hn_ref[r, :]
        o_ref[r, :] = _bdot(hn, w_ref[...]).astype(o_ref.dtype)
        if has_tail:
            ot_ref[r, :] = _bdot(hn, wt_ref[...])

    first = pl.program_id(1) == 0

    @pl.when(first)
    def _():
        _for_x_rows(x_refs, lambda x_ref: _adaln_then(x_ref, g_ref, mod_ref, shift_row, scale_row, hn_ref, project))

    @pl.when(jnp.logical_not(first))
    def _():
        o_ref[...] = _bdot(hn_ref[...], w_ref[...]).astype(o_ref.dtype)


def _adaln_matmul(x, gain, modt, shift_row, scale_row, w, tn, n_out=None, tail=0, out_dtype=F32, name="adaln_mm"):
    t, d = T_ALL, w.shape[0]
    n_out = w.shape[1] if n_out is None else n_out
    in_specs, args = _x_rows_specs(x, d, lambda j: 0)
    n_x = len(args)
    in_specs += [pl.BlockSpec((1, d), lambda i, j: (0, 0)),
                 pl.BlockSpec((None, 8, d), lambda i, j: (i, 0, 0)),
                 pl.BlockSpec((d, tn), lambda i, j: (0, j))]
    args += [gain.reshape(1, d), modt, w]
    out_specs = [pl.BlockSpec((ROW_TILE, tn), lambda i, j: (i, j))]
    out_shape = [jax.ShapeDtypeStruct((t, n_out), out_dtype)]
    if tail:
        in_specs.append(pl.BlockSpec((d, tail), lambda i, j: (0, n_out // tail)))
        args.append(w)
        out_specs.append(pl.BlockSpec((ROW_TILE, tail), lambda i, j: (i, 0)))
        out_shape.append(jax.ShapeDtypeStruct((t, tail), F32))
    return pl.pallas_call(
        functools.partial(_adaln_mm_kernel, shift_row, scale_row, bool(tail), n_x),
        grid=(t // ROW_TILE, n_out // tn),
        in_specs=in_specs,
        out_specs=out_specs,
        out_shape=out_shape,
        scratch_shapes=[pltpu.VMEM((ROW_TILE, d), BF16)],
        compiler_params=_cparams("arbitrary", "arbitrary"),
        name=name,
    )(*args)


FFN_TF = 512


def _gateup_kernel(x_ref, g_ref, mod_ref, wg_ref, wu_ref, h_ref, hn_ref):
    def gate_up(r):
        hn = hn_ref[r, :]
        h_ref[r, :] = (_silu(_bdot(hn, wg_ref[...])) * _bdot(hn, wu_ref[...])).astype(h_ref.dtype)

    first = pl.program_id(1) == 0

    @pl.when(first)
    def _():
        _adaln_then(x_ref, g_ref, mod_ref, 3, 4, hn_ref, gate_up)

    @pl.when(jnp.logical_not(first))
    def _():
        gate_up(slice(None))


def _ffn_gateup(x, gain, modt, wg, wu):
    t, d = x.shape
    ff = wg.shape[1]
    return pl.pallas_call(
        _gateup_kernel,
        grid=(t // ROW_TILE, ff // FFN_TF),
        in_specs=[pl.BlockSpec((ROW_TILE, d), lambda i, f: (i, 0)),
                  pl.BlockSpec((1, d), lambda i, f: (0, 0)),
                  pl.BlockSpec((None, 8, d), lambda i, f: (i, 0, 0)),
                  pl.BlockSpec((d, FFN_TF), lambda i, f: (0, f)),
                  pl.BlockSpec((d, FFN_TF), lambda i, f: (0, f))],
        out_specs=pl.BlockSpec((ROW_TILE, FFN_TF), lambda i, f: (i, f)),
        out_shape=jax.ShapeDtypeStruct((t, ff), BF16),
        scratch_shapes=[pltpu.VMEM((ROW_TILE, d), BF16)],
        compiler_params=_cparams("arbitrary", "arbitrary"),
        name="ffn_gateup",
    )(x, gain.reshape(1, d), modt, wg, wu)


def _proj_res_kernel(group_sizes, gate_row, norm, *refs):
    n_a = len(group_sizes) - 1 - int(norm)
    it = iter(refs)
    a_groups = [[next(it) for _ in range(group_sizes[k])] for k in range(n_a)]
    w_refs = [next(it) for _ in range(n_a)]
    x_group = [next(it) for _ in range(group_sizes[n_a])]
    mod_ref = next(it)
    ss_group = [next(it) for _ in range(group_sizes[n_a + 1])] if norm else None
    o_ref = next(it)

    def run(which):
        def pick(group):
            return group[min(which, len(group) - 1)]

        acc = _bdot(pick(a_groups[0])[...], w_refs[0][...])
        for k in range(1, n_a):
            acc = acc + _bdot(pick(a_groups[k])[...], w_refs[k][...])
        if norm:
            ss_ref = pick(ss_group)
            k_total = sum(g[0].shape[1] for g in a_groups)
            ss = ss_ref[:, 0:LANES]
            for k in range(1, ss_ref.shape[1] // LANES):
                ss = ss + ss_ref[:, k * LANES:(k + 1) * LANES]
            rs = lax.rsqrt(ss * (1.0 / k_total) + EPS)
            acc = acc * jnp.concatenate([rs] * (acc.shape[1] // LANES), axis=1)
        o_ref[...] = pick(x_group)[...] + mod_ref[gate_row:gate_row + 1, :] * acc

    if max(group_sizes) == 1:
        run(0)
    else:
        in_prompt = pl.program_id(0) < PROMPT_TILES
        pl.when(in_prompt)(lambda: run(0))
        pl.when(jnp.logical_not(in_prompt))(lambda: run(1))


def _proj_residual(a_list, w, x, modt, gate_row, tn, row_ss=None, name="proj_res"):
    t, d = T_ALL, w.shape[1]
    norm = row_ss is not None
    in_specs, args, group_sizes, k0 = [], [], [], 0

    def add_rows(arr, cols, col_of_j):
        specs, arrs = _x_rows_specs(arr, cols, col_of_j)
        in_specs.extend(specs)
        args.extend(arrs)
        group_sizes.append(len(arrs))

    widths = [(a[0] if isinstance(a, tuple) else a).shape[1] for a in a_list]
    for a, ka in zip(a_list, widths):
        add_rows(a, ka, lambda j: 0)
    for ka in widths:
        assert k0 % ka == 0
        in_specs.append(pl.BlockSpec((ka, tn), lambda i, j, kb=k0 // ka: (kb, j)))
        args.append(w)
        k0 += ka
    add_rows(x, tn, lambda j: j)
    in_specs.append(pl.BlockSpec((None, 8, tn), lambda i, j: (i, 0, j)))
    args.append(modt)
    if norm:
        add_rows(row_ss, (row_ss[0] if isinstance(row_ss, tuple) else row_ss).shape[1], lambda j: 0)
    return pl.pallas_call(
        functools.partial(_proj_res_kernel, tuple(group_sizes), gate_row, norm),
        grid=(t // ROW_TILE, d // tn),
        in_specs=in_specs,
        out_specs=pl.BlockSpec((ROW_TILE, tn), lambda i, j: (i, j)),
        out_shape=jax.ShapeDtypeStruct((t, d), F32),
        compiler_params=_cparams("arbitrary", "arbitrary"),
        name=name,
    )(*args)


def _ret_kernel(n_chunks, rope, has_s0, emit_state, lg_ref, *refs):
    it = iter(refs)
    q_ref, k_ref, v_ref, gt_ref = next(it), next(it), next(it), next(it)
    cos_ref, sin_ref = (next(it), next(it)) if rope else (None, None)
    s0_ref = next(it) if has_s0 else None
    gain_ref = next(it)
    o_ref = next(it)
    sfin_ref = next(it) if emit_state else None
    qs_ref, ks_ref, sfs_ref, dm_ref, dec_ref, sf_ref, sb_ref = (next(it) for _ in range(7))

    h = pl.program_id(1)
    lgf = lg_ref[0, h]
    lgb = lg_ref[1, h]
    half = RET_DK // 2

    def prep(c, carry):
        r = _rows(c)
        q = q_ref[r, :].astype(F32)
        k = k_ref[r, :].astype(F32) * (RET_DK ** -0.5)
        if rope:
            cs, sn = cos_ref[r, :], sin_ref[r, :]
            for src, dst in ((q, qs_ref), (k, ks_ref)):
                x1, x2 = src[:, :half], src[:, half:]
                dst[r, :half] = (x1 * cs - x2 * sn).astype(BF16)
                dst[r, half:] = (x1 * sn + x2 * cs).astype(BF16)
        else:
            qs_ref[r, :] = q.astype(BF16)
            ks_ref[r, :] = k.astype(BF16)
        return carry

    lax.fori_loop(0, n_chunks, prep, 0)

    ii = lax.broadcasted_iota(jnp.int32, (BLOCK, BLOCK), 0)
    jj = lax.broadcasted_iota(jnp.int32, (BLOCK, BLOCK), 1)
    diff = (ii - jj).astype(F32)
    dm_ref[...] = jnp.exp(jnp.where(jj <= ii, diff * lgf, -diff * lgb))
    pos = lax.broadcasted_iota(jnp.int32, (BLOCK, RET_DV), 0).astype(F32)
    dec_ref[0] = jnp.exp((pos + 1.0) * lgf)
    dec_ref[1] = jnp.exp((BLOCK - pos) * lgb)
    dec_ref[2] = jnp.exp((BLOCK - 1.0 - pos) * lgf)
    dec_ref[3] = jnp.exp(pos * lgb)
    tot_f = jnp.exp(jnp.full((1, RET_DV), BLOCK * lgf, F32))
    tot_b = jnp.exp(jnp.full((1, RET_DV), BLOCK * lgb, F32))

    if has_s0:
        sf_ref[...] = s0_ref[0]
        sb_ref[...] = s0_ref[1]
    else:
        sf_ref[...] = jnp.zeros_like(sf_ref)
        sb_ref[...] = jnp.zeros_like(sb_ref)

    def fwd(c, carry):
        r = _rows(c)
        sfs_ref[c] = sf_ref[...].astype(BF16)
        kd = ks_ref[r, :].astype(F32) * dec_ref[2]
        sf_ref[...] = sf_ref[...] * tot_f + _bdot_tn(kd, v_ref[r, :])
        return carry

    lax.fori_loop(0, n_chunks, fwd, 0, unroll=min(4, n_chunks))
    if emit_state:
        sfin_ref[0] = sf_ref[...]

    def bwd(t, carry):
        c = n_chunks - 1 - t
        r = _rows(c)
        q = qs_ref[r, :]
        k = ks_ref[r, :]
        v = v_ref[r, :].astype(BF16)
        p = _bdot_nt(q, k) * dm_ref[...]
        o = _bdot(p, v)
        o = o + _bdot(q, sfs_ref[c]) * dec_ref[0]
        o = o + _bdot(q, sb_ref[...]) * dec_ref[1]
        ms = jnp.mean(o * o, axis=-1, keepdims=True)
        y = o * lax.rsqrt(ms + EPS) * gain_ref[...]
        o_ref[r, :] = (y * _silu(gt_ref[r, :].astype(F32))).astype(o_ref.dtype)
        kd = k.astype(F32) * dec_ref[3]
        sb_ref[...] = sb_ref[...] * tot_b + _bdot_tn(kd, v)
        return carry

    lax.fori_loop(0, n_chunks, bwd, 0, unroll=min(4, n_chunks))
    if emit_state:
        sfin_ref[1] = sb_ref[...]


def _retention(proj, lg, ret_norm, seq_len, n_seq, row_block0, ropes=None, s0=None, emit_state=False):
    n_chunks = seq_len // BLOCK
    rope, has_s0 = ropes is not None, s0 is not None

    def col(cb):
        return pl.BlockSpec((seq_len, RET_DK), lambda b, h, cb=cb: (b + row_block0, cb * RET_HEADS + h))

    in_specs = [pl.BlockSpec(memory_space=pltpu.SMEM), col(0), col(1), col(2), col(3)]
    args = [lg, proj, proj, proj, proj]
    if rope:
        in_specs += [pl.BlockSpec((seq_len, RET_DK // 2), lambda b, h: (0, 0))] * 2
        args += list(ropes)
    if has_s0:
        in_specs.append(pl.BlockSpec((None, None, 2, None, RET_DK, RET_DV), lambda b, h: (b, 0, 0, h, 0, 0)))
        args.append(s0)
    in_specs.append(pl.BlockSpec((1, RET_DV), lambda b, h: (0, h)))
    args.append(ret_norm.reshape(1, RET_HEADS * RET_DV))
    out_specs = [pl.BlockSpec((seq_len, RET_DV), lambda b, h: (b, h))]
    out_shape = [jax.ShapeDtypeStruct((n_seq * seq_len, RET_HEADS * RET_DV), BF16)]
    if emit_state:
        out_specs.append(pl.BlockSpec((None, None, 2, None, RET_DK, RET_DV), lambda b, h: (b, 0, 0, h, 0, 0)))
        out_shape.append(jax.ShapeDtypeStruct((n_seq, 1, 2, RET_HEADS, RET_DK, RET_DV), F32))
    return pl.pallas_call(
        functools.partial(_ret_kernel, n_chunks, rope, has_s0, emit_state),
        grid=(n_seq, RET_HEADS),
        in_specs=in_specs,
        out_specs=out_specs,
        out_shape=out_shape,
        scratch_shapes=[pltpu.VMEM((seq_len, RET_DK), BF16), pltpu.VMEM((seq_len, RET_DK), BF16),
                        pltpu.VMEM((n_chunks, RET_DK, RET_DV), BF16), pltpu.VMEM((BLOCK, BLOCK), F32),
                        pltpu.VMEM((4, BLOCK, RET_DV), F32), pltpu.VMEM((RET_DK, RET_DV), F32),
                        pltpu.VMEM((RET_DK, RET_DV), F32)],
        compiler_params=_cparams("arbitrary", "arbitrary"),
        name="retention_latent" if rope else "retention_prompt",
    )(*args)


def _head_norm(x, gain):
    x = x.astype(F32)
    return x * lax.rsqrt(jnp.mean(x * x, axis=-1, keepdims=True) + EPS) * gain


def _rope_full(x, cs, sn):
    return x * cs + pltpu.roll(x, ATT_HD // 2, 1) * sn


ATT_SPLIT = 2


def _sink_column(sink_ref, head0, n_heads, rows_per_head):
    n = n_heads * rows_per_head
    head = lax.broadcasted_iota(jnp.int32, (n, 1), 0) // rows_per_head
    col = jnp.full((n, 1), sink_ref[head0], F32)
    for g in range(1, n_heads):
        col = jnp.where(head == g, sink_ref[head0 + g], col)
    return col


def _att_latent_kernel(sink_ref, q_ref, k_ref, v_ref, ck_ref, cv_ref, cos_ref, sin_ref, qn_ref, kn_ref,
                       o_ref, kp_ref, vp_ref, ckp_ref, cvp_ref):
    kh, qb = pl.program_id(1), pl.program_id(2)
    n_chunks = DEC_SEQ // BLOCK
    loc = 3 * BLOCK

    @pl.when(qb == 0)
    def _():
        def prep(c, carry):
            r = _rows(c)
            kp_ref[r, :] = _rope_full(_head_norm(k_ref[r, :], kn_ref[...]), cos_ref[r, :], sin_ref[r, :]).astype(BF16)
            vp_ref[r, :ATT_HD] = v_ref[r, :].astype(BF16)
            vp_ref[r, ATT_HD:] = jnp.ones((BLOCK, ATT_HD), BF16)
            return carry

        lax.fori_loop(0, n_chunks, prep, 0)
        ckp_ref[...] = ck_ref[...].astype(BF16)
        cvp_ref[:, :ATT_HD] = cv_ref[...].astype(BF16)
        cvp_ref[:, ATT_HD:] = jnp.ones((PAST_LEN, ATT_HD), BF16)

    rq = _rows(qb)
    cs, sn = cos_ref[rq, :], sin_ref[rq, :]
    qg = qn_ref[...] * (ATT_HD ** -0.5)
    start = pl.multiple_of(jnp.clip((qb - 1) * BLOCK, 0, DEC_SEQ - loc), BLOCK)
    kl, vl = kp_ref[pl.ds(start, loc), :], vp_ref[pl.ds(start, loc), :]
    qpos = qb * BLOCK + (lax.broadcasted_iota(jnp.int32, (ATT_SPLIT * BLOCK, loc), 0) & (BLOCK - 1))
    kpos = start + lax.broadcasted_iota(jnp.int32, (ATT_SPLIT * BLOCK, loc), 1)
    in_window = jnp.abs(qpos - kpos) <= WINDOW
    for g0 in range(0, ATT_GROUP, ATT_SPLIT):
        q = jnp.concatenate(
            [_rope_full(_head_norm(q_ref[:, g * ATT_HD:(g + 1) * ATT_HD], qg), cs, sn).astype(BF16)
             for g in range(g0, g0 + ATT_SPLIT)], axis=0)
        s_loc = jnp.where(in_window, _bdot_nt(q, kl), NEG_INF)
        s_ctx = _bdot_nt(q, ckp_ref[...])
        sink = _sink_column(sink_ref, kh * ATT_GROUP + g0, ATT_SPLIT, BLOCK)
        m = jnp.maximum(jnp.maximum(jnp.max(s_loc, axis=-1, keepdims=True),
                                    jnp.max(s_ctx, axis=-1, keepdims=True)), sink)
        pv = _bdot(jnp.exp(s_loc - m), vl) + _bdot(jnp.exp(s_ctx - m), cvp_ref[...])
        o = pv[:, :ATT_HD] / (pv[:, ATT_HD:] + jnp.exp(sink - m))
        for g in range(ATT_SPLIT):
            o_ref[:, (g0 + g) * ATT_HD:(g0 + g + 1) * ATT_HD] = o[g * BLOCK:(g + 1) * BLOCK, :].astype(o_ref.dtype)


def _attention_latent(proj, cache_k, cache_v, sink, cos_a, sin_a, q_norm, k_norm):
    nqb = DEC_SEQ // BLOCK
    rb0 = T_PROMPT // DEC_SEQ
    qcol0 = 4 * RET_HEADS * RET_DK // (ATT_GROUP * ATT_HD)
    kcol0 = (4 * RET_HEADS * RET_DK + ATT_HEADS * ATT_HD) // ATT_HD
    vcol0 = kcol0 + ATT_KV_HEADS
    ck = cache_k.reshape(DEC_BATCH, PAST_LEN, ATT_KV_HEADS * ATT_HD)
    cv = cache_v.reshape(DEC_BATCH, PAST_LEN, ATT_KV_HEADS * ATT_HD)
    return pl.pallas_call(
        _att_latent_kernel,
        grid=(DEC_BATCH, ATT_KV_HEADS, nqb),
        in_specs=[pl.BlockSpec(memory_space=pltpu.SMEM),
                  pl.BlockSpec((BLOCK, ATT_GROUP * ATT_HD),
                               lambda b, kh, qb: (T_PROMPT // BLOCK + b * nqb + qb, qcol0 + kh)),
                  pl.BlockSpec((DEC_SEQ, ATT_HD), lambda b, kh, qb: (rb0 + b, kcol0 + kh)),
                  pl.BlockSpec((DEC_SEQ, ATT_HD), lambda b, kh, qb: (rb0 + b, vcol0 + kh)),
                  pl.BlockSpec((None, PAST_LEN, ATT_HD), lambda b, kh, qb: (b, 0, kh)),
                  pl.BlockSpec((None, PAST_LEN, ATT_HD), lambda b, kh, qb: (b, 0, kh)),
                  pl.BlockSpec((DEC_SEQ, ATT_HD), lambda b, kh, qb: (0, 0)),
                  pl.BlockSpec((DEC_SEQ, ATT_HD), lambda b, kh, qb: (0, 0)),
                  pl.BlockSpec((1, ATT_HD), lambda b, kh, qb: (0, 0)),
                  pl.BlockSpec((1, ATT_HD), lambda b, kh, qb: (0, 0))],
        out_specs=pl.BlockSpec((BLOCK, ATT_GROUP * ATT_HD), lambda b, kh, qb: (b * nqb + qb, kh)),
        out_shape=jax.ShapeDtypeStruct((T_SAMPLE, ATT_HEADS * ATT_HD), BF16),
        scratch_shapes=[pltpu.VMEM((DEC_SEQ, ATT_HD), BF16), pltpu.VMEM((DEC_SEQ, 2 * ATT_HD), BF16),
                        pltpu.VMEM((PAST_LEN, ATT_HD), BF16), pltpu.VMEM((PAST_LEN, 2 * ATT_HD), BF16)],
        compiler_params=_cparams("arbitrary", "arbitrary", "arbitrary"),
        name="attention_latent",
    )(sink, proj, proj, proj, ck, cv, cos_a, sin_a, q_norm.reshape(1, ATT_HD), k_norm.reshape(1, ATT_HD))


def _att_prompt_kernel(sink_ref, q_ref, k_ref, v_ref, qn_ref, kn_ref, o_ref, nk_ref, nv_ref):
    kh = pl.program_id(1)
    kn = _head_norm(k_ref[...], kn_ref[...])
    v = v_ref[...]
    nk_ref[...] = kn
    nv_ref[...] = v.astype(F32)
    qg = qn_ref[...] * (ATT_HD ** -0.5)
    q = jnp.concatenate([_head_norm(q_ref[:, g * ATT_HD:(g + 1) * ATT_HD], qg).astype(BF16)
                         for g in range(ATT_GROUP)], axis=0)
    s = _bdot_nt(q, kn)
    sink = _sink_column(sink_ref, kh * ATT_GROUP, ATT_GROUP, SEQ)
    m = jnp.maximum(jnp.max(s, axis=-1, keepdims=True), sink)
    pv = _bdot(jnp.exp(s - m), jnp.concatenate([v.astype(BF16), jnp.ones((SEQ, ATT_HD), BF16)], axis=1))
    o = pv[:, :ATT_HD] / (pv[:, ATT_HD:] + jnp.exp(sink - m))
    for g in range(ATT_GROUP):
        o_ref[:, g * ATT_HD:(g + 1) * ATT_HD] = o[g * SEQ:(g + 1) * SEQ, :].astype(o_ref.dtype)


def _attention_prompt(proj, sink, q_norm, k_norm):
    qcol0 = 4 * RET_HEADS * RET_DK // (ATT_GROUP * ATT_HD)
    kcol0 = (4 * RET_HEADS * RET_DK + ATT_HEADS * ATT_HD) // ATT_HD
    vcol0 = kcol0 + ATT_KV_HEADS
    kv_spec = pl.BlockSpec((None, SEQ, ATT_HD), lambda b, kh: (b, 0, kh))
    kv_shape = jax.ShapeDtypeStruct((BATCH, SEQ, ATT_KV_HEADS * ATT_HD), F32)
    return pl.pallas_call(
        _att_prompt_kernel,
        grid=(BATCH, ATT_KV_HEADS),
        in_specs=[pl.BlockSpec(memory_space=pltpu.SMEM),
                  pl.BlockSpec((SEQ, ATT_GROUP * ATT_HD), lambda b, kh: (b, qcol0 + kh)),
                  pl.BlockSpec((SEQ, ATT_HD), lambda b, kh: (b, kcol0 + kh)),
                  pl.BlockSpec((SEQ, ATT_HD), lambda b, kh: (b, vcol0 + kh)),
                  pl.BlockSpec((1, ATT_HD), lambda b, kh: (0, 0)),
                  pl.BlockSpec((1, ATT_HD), lambda b, kh: (0, 0))],
        out_specs=[pl.BlockSpec((SEQ, ATT_GROUP * ATT_HD), lambda b, kh: (b, kh)), kv_spec, kv_spec],
        out_shape=[jax.ShapeDtypeStruct((T_PROMPT, ATT_HEADS * ATT_HD), BF16), kv_shape, kv_shape],
        compiler_params=_cparams("arbitrary", "arbitrary"),
        name="attention_prompt",
    )(sink, proj, proj, proj, q_norm.reshape(1, ATT_HD), k_norm.reshape(1, ATT_HD))


def _split_dot(m01, a):
    hi = a.astype(BF16)
    r1 = a - hi.astype(F32)
    mid = r1.astype(BF16)
    lo = (r1 - mid.astype(F32)).astype(BF16)
    return (jnp.dot(m01, hi, preferred_element_type=F32) + jnp.dot(m01, mid, preferred_element_type=F32)
            + jnp.dot(m01, lo, preferred_element_type=F32))


PREP_CHUNKS = ROW_TILE // BLOCK


def _ssd_prep_kernel(raw_ref, bias_ref, alog_ref, cum_ref, dt_ref, w_ref, tot_ref, ecum_ref):
    ii = lax.broadcasted_iota(jnp.int32, (BLOCK, BLOCK), 0)
    jj = lax.broadcasted_iota(jnp.int32, (BLOCK, BLOCK), 1)
    lower = jnp.where(jj <= ii, 1.0, 0.0).astype(BF16)
    upper = jnp.where(jj >= ii, 1.0, 0.0).astype(BF16)
    fwd_lane = lax.broadcasted_iota(jnp.int32, (BLOCK, LANES), 1) < SSD_HEADS
    neg_a = -jnp.exp(alog_ref[...])

    def chunk(k, carry):
        x = raw_ref[_rows(k), :] + bias_ref[...]
        dt = jnp.maximum(x, 0.0) + jnp.log1p(jnp.exp(-jnp.abs(x)))
        a = dt * neg_a
        incl = _split_dot(lower, a)
        rincl = _split_dot(upper, a)
        cum = jnp.where(fwd_lane, incl, rincl)
        tot = jnp.where(fwd_lane[:1], incl[BLOCK - 1:BLOCK, :], rincl[0:1, :])
        cum_ref[k] = cum.T
        dt_ref[k] = dt.T
        w_ref[k] = (dt * jnp.exp(tot - cum)).T
        tot_ref[k] = jnp.broadcast_to(jnp.exp(tot), (BLOCK, LANES)).T
        ecum_ref[k] = jnp.exp(cum).T
        return carry

    lax.fori_loop(0, PREP_CHUNKS, chunk, 0)


def _ssd_prep(dt_raw, dt_bias, a_log):
    nc = T_ALL // BLOCK
    spec = pl.BlockSpec((PREP_CHUNKS, 2 * SSD_HEADS, BLOCK), lambda c: (c, 0, 0))
    shape = jax.ShapeDtypeStruct((nc, 2 * SSD_HEADS, BLOCK), F32)
    return pl.pallas_call(
        _ssd_prep_kernel,
        grid=(nc // PREP_CHUNKS,),
        in_specs=[pl.BlockSpec((ROW_TILE, 2 * SSD_HEADS), lambda c: (c, 0)),
                  pl.BlockSpec((1, 2 * SSD_HEADS), lambda c: (0, 0)),
                  pl.BlockSpec((1, 2 * SSD_HEADS), lambda c: (0, 0))],
        out_specs=[spec] * 5,
        out_shape=[shape] * 5,
        compiler_params=_cparams("arbitrary"),
        name="ssd_prep",
    )(dt_raw, dt_bias.reshape(1, 2 * SSD_HEADS), a_log.reshape(1, 2 * SSD_HEADS))


GW = SSD_R * SSD_P
HALO = 8


def _pair_tiles(per_head):
    low = lax.broadcasted_iota(jnp.int32, per_head[0].shape, 1) < SSD_P
    return jnp.concatenate([jnp.where(low, per_head[2 * t], per_head[2 * t + 1]) for t in range(SSD_R // 2)],
                           axis=1)


def _row_bcast(ref, c, r):
    return jnp.broadcast_to(ref[c, r:r + 1, :], (BLOCK, BLOCK))


def _pair_cols(ref, c):
    top = lax.broadcasted_iota(jnp.int32, (BLOCK, BLOCK), 0) < SSD_P
    return jnp.concatenate(
        [jnp.where(top, _row_bcast(ref, c, 2 * t), _row_bcast(ref, c, 2 * t + 1)).T for t in range(SSD_R // 2)],
        axis=1)


def _ssd_kernel(n_chunks, has_s0, emit_state, *refs):
    it = iter(refs)
    z_ref, x_ref, b_ref, c_ref = (next(it) for _ in range(4))
    cumf_ref, cumb_ref, dtf_ref, dtb_ref, wf_ref, wb_ref, totf_ref, totb_ref, ecf_ref, ecb_ref = (
        next(it) for _ in range(10))
    cwx_ref, cwb_ref, cwc_ref, cbx_ref, cbb_ref, cbc_ref, d_ref, ng_ref = (next(it) for _ in range(8))
    s0_ref = next(it) if has_s0 else None
    o_ref, ss_ref = next(it), next(it)
    sfin_ref = next(it) if emit_state else None
    pad_ref, xc_ref, bc_ref, cc_ref, sfs_ref, sf_ref, sb_ref = (next(it) for _ in range(7))
    seq_len = n_chunks * BLOCK

    pad_ref[0:HALO, :] = jnp.zeros((HALO, GW + 2 * SSD_N), F32)
    pad_ref[HALO + seq_len:2 * HALO + seq_len, :] = jnp.zeros((HALO, GW + 2 * SSD_N), F32)

    def fill(c, carry):
        dst = pl.ds(pl.multiple_of(c * BLOCK, BLOCK) + HALO, BLOCK)
        r = _rows(c)
        pad_ref[dst, 0:GW] = x_ref[r, :].astype(F32)
        pad_ref[dst, GW:GW + SSD_N] = b_ref[r, :].astype(F32)
        pad_ref[dst, GW + SSD_N:GW + 2 * SSD_N] = c_ref[r, :].astype(F32)
        return carry

    lax.fori_loop(0, n_chunks, fill, 0)

    def conv(c):
        r = _rows(c)
        src = pl.ds(pl.multiple_of(c * BLOCK, BLOCK), BLOCK + 2 * HALO)
        for col0, width, cw_ref, cb_ref, dst in ((0, GW, cwx_ref, cbx_ref, xc_ref),
                                                 (GW, SSD_N, cwb_ref, cbb_ref, bc_ref),
                                                 (GW + SSD_N, SSD_N, cwc_ref, cbc_ref, cc_ref)):
            for t in range(width // LANES):
                tl = slice(t * LANES, (t + 1) * LANES)
                win = pad_ref[src, col0 + t * LANES:col0 + (t + 1) * LANES]
                acc = jnp.broadcast_to(cb_ref[:, tl], (BLOCK, LANES))
                for w in range(CONV_W):
                    off = HALO - CONV_W // 2 + w
                    acc = acc + win[off:off + BLOCK, :] * cw_ref[w:w + 1, tl]
                dst[r, tl] = _silu(acc).astype(dst.dtype)

    if has_s0:
        for r in range(SSD_R):
            sf_ref[:, r * SSD_P:(r + 1) * SSD_P] = s0_ref[0, r]
            sb_ref[:, r * SSD_P:(r + 1) * SSD_P] = s0_ref[1, r]
    else:
        sf_ref[...] = jnp.zeros_like(sf_ref)
        sb_ref[...] = jnp.zeros_like(sb_ref)

    def state_update(s_ref, c, w_ref, tot_ref, bm, xs):
        tot = _pair_tiles([tot_ref[c, r:r + 1, :] for r in range(SSD_R)])
        s_ref[...] = s_ref[...] * tot + _bdot_tn(bm, xs * _pair_cols(w_ref, c))

    def fwd_step(c):
        r = _rows(c)
        sfs_ref[c] = sf_ref[...].astype(BF16)
        state_update(sf_ref, c, wf_ref, totf_ref, bc_ref[r, :], xc_ref[r, :])

    def fwd(c, carry):
        fwd_step(c)
        conv(c + 1)
        return carry

    conv(0)
    lax.fori_loop(0, n_chunks - 1, fwd, 0, unroll=3 if (n_chunks - 1) % 3 == 0 else 1)
    fwd_step(n_chunks - 1)
    if emit_state:
        for r in range(SSD_R):
            sfin_ref[0, r] = sf_ref[:, r * SSD_P:(r + 1) * SSD_P]

    ii = lax.broadcasted_iota(jnp.int32, (BLOCK, BLOCK), 0)
    jj = lax.broadcasted_iota(jnp.int32, (BLOCK, BLOCK), 1)
    causal = jj <= ii
    anti = ii <= jj
    low = lax.broadcasted_iota(jnp.int32, (BLOCK, LANES), 1) < SSD_P

    @pl.when(pl.program_id(1) == 0)
    def _():
        ss_ref[...] = jnp.zeros_like(ss_ref)

    def bwd(t, carry):
        c = n_chunks - 1 - t
        r = _rows(c)
        cm, bm, xs = cc_ref[r, :], bc_ref[r, :], xc_ref[r, :]
        xb = xs.astype(BF16)
        sc = _bdot_nt(cm, bm)
        yf = _bdot(cm, sfs_ref[c])
        yb = _bdot(cm, sb_ref[...])
        tiles = []
        for t2 in range(SSD_R // 2):
            xt = xb[:, t2 * LANES:(t2 + 1) * LANES]
            acc = None
            for hh in range(2):
                hr = 2 * t2 + hh
                rf, rb = _row_bcast(cumf_ref, c, hr), _row_bcast(cumb_ref, c, hr)
                e = jnp.exp(jnp.where(anti, rf, rb).T - jnp.where(causal, rf, rb))
                e = e * jnp.where(causal, _row_bcast(dtf_ref, c, hr), _row_bcast(dtb_ref, c, hr))
                xh = jnp.where(low, xt, 0.0) if hh == 0 else jnp.where(low, 0.0, xt)
                part = _bdot(sc * e, xh)
                acc = part if acc is None else acc + part
            tiles.append(acc)
        y = (jnp.concatenate(tiles, axis=1) + _pair_cols(ecf_ref, c) * yf + _pair_cols(ecb_ref, c) * yb
             + d_ref[...] * xs)
        yg = y * _silu(z_ref[r, :].astype(F32))
        ss_ref[r, :] += jnp.broadcast_to(jnp.sum(yg * yg, axis=-1, keepdims=True), (BLOCK, LANES))
        o_ref[r, :] = (yg * ng_ref[...]).astype(o_ref.dtype)
        state_update(sb_ref, c, wb_ref, totb_ref, bm, xs)
        return carry

    lax.fori_loop(0, n_chunks, bwd, 0, unroll=min(4, n_chunks))
    if emit_state:
        for r in range(SSD_R):
            sfin_ref[1, r] = sb_ref[:, r * SSD_P:(r + 1) * SSD_P]


def _ssd_scan(zx, prep, conv_w, conv_b, d_exp, norm_gain, seq_len, n_seq, row_block0, s0=None, emit_state=False):
    n_chunks = seq_len // BLOCK
    has_s0 = s0 is not None
    xcol0 = D_INNER // GW
    bcol0 = 2 * D_INNER // SSD_N
    ccol0 = bcol0 + SSD_GROUPS
    cwb0 = D_INNER // SSD_N

    def rowcol(width, col0):
        return pl.BlockSpec((seq_len, width), lambda b, g, col0=col0: (b + row_block0, col0 + g))

    def headrows(direction):
        return pl.BlockSpec((n_chunks, SSD_R, BLOCK),
                            lambda b, g, direction=direction: (b + row_block0, direction * SSD_GROUPS + g, 0))

    in_specs = [rowcol(GW, 0), rowcol(GW, xcol0), rowcol(SSD_N, bcol0), rowcol(SSD_N, ccol0)]
    args = [zx, zx, zx, zx]
    for arr in prep:
        in_specs += [headrows(0), headrows(1)]
        args += [arr, arr]
    in_specs += [pl.BlockSpec((CONV_W, GW), lambda b, g: (0, g)),
                 pl.BlockSpec((CONV_W, SSD_N), lambda b, g: (0, cwb0 + g)),
                 pl.BlockSpec((CONV_W, SSD_N), lambda b, g: (0, cwb0 + SSD_GROUPS + g)),
                 pl.BlockSpec((1, GW), lambda b, g: (0, g)),
                 pl.BlockSpec((1, SSD_N), lambda b, g: (0, cwb0 + g)),
                 pl.BlockSpec((1, SSD_N), lambda b, g: (0, cwb0 + SSD_GROUPS + g)),
                 pl.BlockSpec((1, GW), lambda b, g: (0, g)),
                 pl.BlockSpec((1, GW), lambda b, g: (0, g))]
    args += [conv_w, conv_w, conv_w, conv_b, conv_b, conv_b, d_exp, norm_gain.reshape(1, D_INNER)]
    state_spec = pl.BlockSpec((None, None, 2, SSD_R, SSD_N, SSD_P), lambda b, g: (b, 0, 0, g, 0, 0))
    if has_s0:
        in_specs.append(state_spec)
        args.append(s0)
    out_specs = [pl.BlockSpec((seq_len, GW), lambda b, g: (b, g)),
                 pl.BlockSpec((seq_len, LANES), lambda b, g: (b, 0))]
    out_shape = [jax.ShapeDtypeStruct((n_seq * seq_len, D_INNER), BF16),
                 jax.ShapeDtypeStruct((n_seq * seq_len, LANES), F32)]
    if emit_state:
        out_specs.append(state_spec)
        out_shape.append(jax.ShapeDtypeStruct((n_seq, 1, 2, SSD_HEADS, SSD_N, SSD_P), F32))
    return pl.pallas_call(
        functools.partial(_ssd_kernel, n_chunks, has_s0, emit_state),
        grid=(n_seq, SSD_GROUPS),
        in_specs=in_specs,
        out_specs=out_specs,
        out_shape=out_shape,
        scratch_shapes=[pltpu.VMEM((seq_len + 2 * HALO, GW + 2 * SSD_N), F32),
                        pltpu.VMEM((seq_len, GW), F32), pltpu.VMEM((seq_len, SSD_N), BF16),
                        pltpu.VMEM((seq_len, SSD_N), BF16), pltpu.VMEM((n_chunks, SSD_N, GW), BF16),
                        pltpu.VMEM((SSD_N, GW), F32), pltpu.VMEM((SSD_N, GW), F32)],
        compiler_params=_cparams("arbitrary", "arbitrary"),
        name="ssd_scan_latent" if has_s0 else "ssd_scan_prompt",
    )(*args)


def _router_kernel(x_ref, g_ref, mod_ref, rw_ref, hn_ref, idx_ref, wgt_ref, hf_ref):
    rw = rw_ref[...]
    r_hi = rw.astype(BF16)
    r_lo = (rw - r_hi.astype(F32)).astype(BF16)
    r_both = jnp.concatenate([r_hi, r_lo], axis=1)

    def route(r):
        hf = hf_ref[r, :]
        h_hi = hf.astype(BF16)
        h_lo = (hf - h_hi.astype(F32)).astype(BF16)
        hn_ref[r, :] = h_hi
        t1 = jnp.dot(h_hi, r_both, preferred_element_type=F32)
        logits = t1[:, :LANES] + t1[:, LANES:] + jnp.dot(h_lo, r_hi, preferred_element_type=F32)
        lane = lax.broadcasted_iota(jnp.int32, logits.shape, 1)
        lg = jnp.where(lane < N_EXPERTS, logits, NEG_INF)
        m1 = jnp.max(lg, axis=-1, keepdims=True)
        i1 = jnp.min(jnp.where(lg == m1, lane, LANES), axis=-1, keepdims=True)
        lg2 = jnp.where(lane == i1, NEG_INF, lg)
        m2 = jnp.max(lg2, axis=-1, keepdims=True)
        i2 = jnp.min(jnp.where(lg2 == m2, lane, LANES), axis=-1, keepdims=True)
        e2 = jnp.exp(m2 - m1)
        w1 = 1.0 / (1.0 + e2)
        idx_ref[r, :] = jnp.where(lane == 0, i1, jnp.where(lane == 1, i2, 0))
        wgt_ref[r, :] = jnp.where(lane == 0, w1, jnp.where(lane == 1, e2 * w1, 0.0))

    _adaln_then(x_ref, g_ref, mod_ref, 3, 4, hf_ref, route)


def _router(x, gain, modt, router_w):
    t, d = x.shape
    rw = jnp.pad(router_w, ((0, 0), (0, LANES - N_EXPERTS)))
    return pl.pallas_call(
        _router_kernel,
        grid=(t // ROW_TILE,),
        in_specs=[pl.BlockSpec((ROW_TILE, d), lambda i: (i, 0)),
                  pl.BlockSpec((1, d), lambda i: (0, 0)),
                  pl.BlockSpec((None, 8, d), lambda i: (i, 0, 0)),
                  pl.BlockSpec((d, LANES), lambda i: (0, 0))],
        out_specs=[pl.BlockSpec((ROW_TILE, d), lambda i: (i, 0)),
                   pl.BlockSpec((ROW_TILE, LANES), lambda i: (i, 0)),
                   pl.BlockSpec((ROW_TILE, LANES), lambda i: (i, 0))],
        out_shape=[jax.ShapeDtypeStruct((t, d), BF16), jax.ShapeDtypeStruct((t, LANES), jnp.int32),
                   jax.ShapeDtypeStruct((t, LANES), F32)],
        scratch_shapes=[pltpu.VMEM((ROW_TILE, d), F32)],
        compiler_params=_cparams("arbitrary"),
        name="moe_router",
    )(x, gain.reshape(1, d), modt, rw)


DOWN_ROWS = 512
DOWN_TN = 1024


def _expert_changed(be_ref, blk, prev_blk, step):
    return jnp.logical_or(step == 0, be_ref[blk] != be_ref[prev_blk])


def _expert_up_kernel(be_ref, nu_ref, nv_ref, xs_ref, wg_ref, wu_ref, h_ref, wgb_ref, wub_ref):
    i = pl.program_id(1)
    used = nv_ref[i] > 0

    @pl.when(jnp.logical_and(used, _expert_changed(be_ref, i, jnp.maximum(i - 1, 0), i)))
    def _():
        wgb_ref[...] = wg_ref[...].astype(BF16)
        wub_ref[...] = wu_ref[...].astype(BF16)

    @pl.when(used)
    def _():
        xs = xs_ref[...]
        h_ref[...] = (_silu(_bdot(xs, wgb_ref[...])) * _bdot(xs, wub_ref[...])).astype(h_ref.dtype)

    @pl.when(jnp.logical_not(used))
    def _():
        h_ref[...] = jnp.zeros_like(h_ref)


def _expert_down_kernel(be_ref, nu_ref, nv_ref, h_ref, wd_ref, o_ref, wdb_ref):
    i = pl.program_id(1)
    per = MOE_ROWS // DOWN_ROWS
    blk = i // per
    used = nv_ref[blk] > (i % per) * DOWN_ROWS

    @pl.when(jnp.logical_and(nv_ref[blk] > 0, _expert_changed(be_ref, blk, jnp.maximum(i - 1, 0) // per, i)))
    def _():
        wdb_ref[...] = wd_ref[...].astype(BF16)

    @pl.when(used)
    def _():
        o_ref[...] = _bdot(h_ref[...], wdb_ref[...]).astype(o_ref.dtype)

    @pl.when(jnp.logical_not(used))
    def _():
        o_ref[...] = jnp.zeros_like(o_ref)


def _experts(xs_sorted, block_e, n_used, n_valid, wg, wu, wd):
    cap, d = xs_sorted.shape
    ff = wg.shape[2]
    per = MOE_ROWS // DOWN_ROWS

    def expert_of(blk, be, nu):
        return be[jnp.minimum(blk, jnp.maximum(nu[0] - 1, 0))]

    h = pl.pallas_call(
        _expert_up_kernel,
        grid_spec=pltpu.PrefetchScalarGridSpec(
            num_scalar_prefetch=3,
            grid=(ff // FFN_TF, cap // MOE_ROWS),
            in_specs=[pl.BlockSpec((MOE_ROWS, d), lambda f, i, be, nu, nv: (i, 0)),
                      pl.BlockSpec((None, d, FFN_TF), lambda f, i, be, nu, nv: (expert_of(i, be, nu), 0, f)),
                      pl.BlockSpec((None, d, FFN_TF), lambda f, i, be, nu, nv: (expert_of(i, be, nu), 0, f))],
            out_specs=pl.BlockSpec((MOE_ROWS, FFN_TF), lambda f, i, be, nu, nv: (i, f)),
            scratch_shapes=[pltpu.VMEM((d, FFN_TF), BF16), pltpu.VMEM((d, FFN_TF), BF16)],
        ),
        out_shape=jax.ShapeDtypeStruct((cap, ff), BF16),
        compiler_params=_cparams("arbitrary", "arbitrary"),
        name="moe_expert_up",
    )(block_e, n_used, n_valid, xs_sorted, wg, wu)
    return pl.pallas_call(
        _expert_down_kernel,
        grid_spec=pltpu.PrefetchScalarGridSpec(
            num_scalar_prefetch=3,
            grid=(d // DOWN_TN, cap // DOWN_ROWS),
            in_specs=[pl.BlockSpec((DOWN_ROWS, ff), lambda n, i, be, nu, nv: (i, 0)),
                      pl.BlockSpec((None, ff, DOWN_TN),
                                   lambda n, i, be, nu, nv: (expert_of(i // per, be, nu), 0, n),
                                   pipeline_mode=pl.Buffered(1))],
            out_specs=pl.BlockSpec((DOWN_ROWS, DOWN_TN), lambda n, i, be, nu, nv: (i, n)),
            scratch_shapes=[pltpu.VMEM((ff, DOWN_TN), BF16)],
        ),
        out_shape=jax.ShapeDtypeStruct((cap, d), BF16),
        compiler_params=_cparams("arbitrary", "arbitrary"),
        name="moe_expert_down",
    )(block_e, n_used, n_valid, h, wd)


def _combine_kernel(x_ref, g0_ref, g1_ref, w_ref, mod_ref, o_ref):
    w = w_ref[...]
    y = w[:, 0:1] * g0_ref[...].astype(F32) + w[:, 1:2] * g1_ref[...].astype(F32)
    o_ref[...] = x_ref[...] + mod_ref[5:6, :] * y


COMB_ROWS = ROW_TILE


def _combine(x, g, wgt, modt, row0, n_rows):
    t, d = x.shape
    b0 = row0 // COMB_ROWS
    per = ROW_TILE // COMB_ROWS
    return pl.pallas_call(
        _combine_kernel,
        grid=(n_rows // COMB_ROWS,),
        in_specs=[pl.BlockSpec((COMB_ROWS, d), lambda i: (i + b0, 0)),
                  pl.BlockSpec((COMB_ROWS, d), lambda i: (i + b0, 0)),
                  pl.BlockSpec((COMB_ROWS, d), lambda i: (i + b0 + t // COMB_ROWS, 0)),
                  pl.BlockSpec((COMB_ROWS, LANES), lambda i: (i + b0, 0)),
                  pl.BlockSpec((None, 8, d), lambda i: ((i + b0) // per, 0, 0))],
        out_specs=pl.BlockSpec((COMB_ROWS, d), lambda i: (i, 0)),
        out_shape=jax.ShapeDtypeStruct((n_rows, d), F32),
        compiler_params=_cparams("arbitrary"),
        name="moe_combine",
    )(x, g, g, wgt, modt)


def _moe(x, gain, modt, router_w, wg, wu, wd):
    t, d = x.shape
    hn, idx, wgt = _router(x, gain, modt, router_w)
    top_idx = idx[:, :TOP_K]
    n_slots = t * TOP_K
    flat_e = top_idx.reshape(-1)
    onehot = (flat_e[:, None] == jnp.arange(N_EXPERTS, dtype=jnp.int32)[None, :]).astype(jnp.int32)
    incl = jnp.cumsum(onehot, axis=0)
    counts = incl[-1]
    rank = jnp.sum((incl - onehot) * onehot, axis=1)
    padded = (counts + MOE_ROWS - 1) // MOE_ROWS * MOE_ROWS
    pend = jnp.cumsum(padded)
    pstart = pend - padded
    dest = pstart[flat_e] + rank
    n_blocks = n_slots // MOE_ROWS + N_EXPERTS
    cap = n_blocks * MOE_ROWS
    row_tok = (jnp.arange(cap, dtype=jnp.int32) % t).at[dest].set(
        jnp.arange(n_slots, dtype=jnp.int32) // TOP_K, unique_indices=True, mode='promise_in_bounds')
    block_e = jnp.clip(jnp.searchsorted(pend, jnp.arange(n_blocks, dtype=jnp.int32) * MOE_ROWS, side='right'),
                       0, N_EXPERTS - 1).astype(jnp.int32)
    n_used = (pend[-1:] // MOE_ROWS).astype(jnp.int32)
    blk = jnp.arange(n_blocks, dtype=jnp.int32)
    n_valid = jnp.where(blk < n_used[0],
                        jnp.clip((pstart + counts)[block_e] - blk * MOE_ROWS, 0, MOE_ROWS), 0).astype(jnp.int32)
    out = _experts(hn.at[row_tok].get(mode='promise_in_bounds'), block_e, n_used, n_valid, wg, wu, wd)
    g = out.at[dest.reshape(t, TOP_K).T.reshape(-1)].get(mode='promise_in_bounds', unique_indices=True)
    return _combine(x, g, wgt, modt, 0, T_PROMPT), _combine(x, g, wgt, modt, T_PROMPT, T_SAMPLE)


def _rope_tables(n_tokens, dim):
    n_rows = n_tokens // GRID_W
    row = jnp.repeat(jnp.arange(n_rows), GRID_W).astype(F32)
    col = jnp.tile(jnp.arange(GRID_W), n_rows).astype(F32)
    n_freq = dim // 4
    inv = ROPE_BASE ** (-jnp.arange(n_freq, dtype=F32) / n_freq)
    ang = jnp.concatenate([row[:, None] * inv, col[:, None] * inv], axis=-1)
    return jnp.cos(ang), jnp.sin(ang)


def kernel(x_prompt, x_sample, state_ret, cache_k, cache_v, state_ssd, c, c_ctx, ada_w, ada_b, norm_mix, norm_ffn, ev_w_in, ev_w_out, ret_decay_logit, ret_norm, att_q_norm, att_k_norm, att_sink, ffn_w_gate, ffn_w_up, ffn_w_down, ssd_w_in, ssd_conv_w, ssd_conv_b, ssd_a_log, ssd_dt_bias, ssd_d, ssd_norm, ssd_w_out, moe_router, moe_w_gate, moe_w_up, moe_w_down):
    d = D_MODEL
    x = (x_prompt.reshape(T_PROMPT, d), x_sample.reshape(T_SAMPLE, d))

    cvecs = jnp.concatenate([c_ctx[None, :], c, jnp.zeros((MOD_ROWS - 1 - DEC_BATCH, d), F32)], axis=0)
    mods = _modulation(cvecs, ada_w, ada_b).reshape(2, MOD_ROWS, 6, d)
    tiles_per_seq = DEC_SEQ // ROW_TILE
    tile_row = jnp.concatenate([jnp.zeros((T_PROMPT // ROW_TILE,), jnp.int32),
                                1 + jnp.arange(T_SAMPLE // ROW_TILE, dtype=jnp.int32) // tiles_per_seq])
    modt = jnp.pad(mods[:, tile_row], ((0, 0), (0, 0), (0, 2), (0, 0)))

    ev_w_in_b, ev_w_out_b = ev_w_in[0].astype(BF16), ev_w_out[0].astype(BF16)
    ffn_wd_b = ffn_w_down[0].astype(BF16)
    ssd_w_in_b, ssd_w_out_b = ssd_w_in[0].astype(BF16), ssd_w_out[0].astype(BF16)

    proj, = _adaln_matmul(x, norm_mix[0], modt[0], 0, 1, ev_w_in_b, tn=512, out_dtype=BF16, name="even_in_proj")
    lg = jax.nn.log_sigmoid(ret_decay_logit[0].astype(F32))
    cos_r, sin_r = _rope_tables(DEC_SEQ, RET_DK)
    cos_a, sin_a = _rope_tables(DEC_SEQ, ATT_HD)
    cos_a2 = jnp.concatenate([cos_a, cos_a], axis=-1)
    sin_a2 = jnp.concatenate([-sin_a, sin_a], axis=-1)
    ret_p, new_state_ret = _retention(proj, lg, ret_norm[0], SEQ, BATCH, 0, emit_state=True)
    ret_s, = _retention(proj, lg, ret_norm[0], DEC_SEQ, DEC_BATCH, T_PROMPT // DEC_SEQ,
                        ropes=(cos_r, sin_r), s0=state_ret)
    att_p, new_k, new_v = _attention_prompt(proj, att_sink[0], att_q_norm[0], att_k_norm[0])
    att_s = _attention_latent(proj, cache_k[:, 0], cache_v[:, 0], att_sink[0], cos_a2, sin_a2,
                              att_q_norm[0], att_k_norm[0])
    x = _proj_residual([(ret_p, ret_s), (att_p, att_s)], ev_w_out_b, x, modt[0], 2, tn=1024, name="even_out_proj")
    h = _ffn_gateup(x, norm_ffn[0], modt[0], ffn_w_gate[0], ffn_w_up[0])
    x = _proj_residual([h], ffn_wd_b, x, modt[0], 5, tn=512, name="ffn_down")

    zx, dt_raw = _adaln_matmul(x, norm_mix[1], modt[1], 0, 1, ssd_w_in_b, tn=1024, n_out=SSD_ZX,
                               tail=2 * SSD_HEADS, out_dtype=BF16, name="ssd_in_proj")
    prep = _ssd_prep(dt_raw, ssd_dt_bias[0], ssd_a_log[0])
    d_exp = jnp.repeat(ssd_d[0], SSD_P)[None, :]
    conv_b = ssd_conv_b[0][None, :]
    yg_p, yss_p, new_state_ssd = _ssd_scan(zx, prep, ssd_conv_w[0], conv_b, d_exp, ssd_norm[0], SEQ, BATCH, 0,
                                           emit_state=True)
    yg_s, yss_s = _ssd_scan(zx, prep, ssd_conv_w[0], conv_b, d_exp, ssd_norm[0], DEC_SEQ, DEC_BATCH,
                            T_PROMPT // DEC_SEQ, s0=state_ssd)
    x = _proj_residual([(yg_p, yg_s)], ssd_w_out_b, x, modt[1], 2, tn=512, row_ss=(yss_p, yss_s),
                       name="ssd_out_proj")
    y_p, y_s = _moe(x, norm_ffn[1], modt[1], moe_router[0], moe_w_gate[0], moe_w_up[0], moe_w_down[0])

    y_prompt = y_p.reshape(BATCH, SEQ, d)
    y_sample = y_s.reshape(DEC_BATCH, DEC_SEQ, d)
    new_cache_k = new_k.reshape(BATCH, 1, SEQ, ATT_KV_HEADS, ATT_HD)
    new_cache_v = new_v.reshape(BATCH, 1, SEQ, ATT_KV_HEADS, ATT_HD)
    return (y_prompt, y_sample, new_state_ret, new_cache_k, new_cache_v, new_state_ssd)
```

```python
import functools

import jax
import jax.numpy as jnp
from jax import lax
from jax.experimental import pallas as pl
from jax.experimental.pallas import tpu as pltpu

F32 = jnp.float32
BF16 = jnp.bfloat16

D_MODEL = 2048
BATCH = 16
SEQ = 256
DEC_BATCH = 8
DEC_SEQ = 2048
PAST_LEN = 512
GRID_W = 64
BLOCK = 128
WINDOW = 128
EPS = 1e-6
ROPE_BASE = 10000.0
RET_HEADS = 4
RET_DK = 256
RET_DV = 256
ATT_HEADS = 8
ATT_KV_HEADS = 2
ATT_HD = 128
ATT_GROUP = ATT_HEADS // ATT_KV_HEADS
EVEN_IN = 5632
D_INNER = 2 * D_MODEL
SSD_P = 64
SSD_HEADS = D_INNER // SSD_P
SSD_N = 128
SSD_GROUPS = 8
SSD_R = SSD_HEADS // SSD_GROUPS
CONV_W = 5
CONV_CH = D_INNER + 2 * SSD_GROUPS * SSD_N
SSD_ZX = D_INNER + CONV_CH
D_FF = 5632
N_EXPERTS = 8
TOP_K = 2

T_PROMPT = BATCH * SEQ
T_SAMPLE = DEC_BATCH * DEC_SEQ
T_ALL = T_PROMPT + T_SAMPLE

LANES = 128
ROW_TILE = 1024
VMEM_LIMIT = 56 * 1024 * 1024
MOE_ROWS = 1024
NEG_INF = float("-inf")


def _cparams(*sem):
    return pltpu.CompilerParams(dimension_semantics=sem, vmem_limit_bytes=VMEM_LIMIT)


def _silu(x):
    return x * jax.nn.sigmoid(x)


def _bdot(a, b):
    return jnp.dot(a.astype(BF16), b.astype(BF16), preferred_element_type=F32)


def _bdot_nt(a, b):
    return lax.dot_general(a.astype(BF16), b.astype(BF16), (((1,), (1,)), ((), ())),
                           preferred_element_type=F32)


def _bdot_tn(a, b):
    return lax.dot_general(a.astype(BF16), b.astype(BF16), (((0,), (0,)), ((), ())),
                           preferred_element_type=F32)


def _rows(c, n=BLOCK):
    return pl.ds(pl.multiple_of(c * n, n), n)


MOD_ROWS = 16
MOD_TN = 1024


def _mod_kernel(c_ref, w_ref, b_ref, o_ref):
    o_ref[...] = _bdot(_silu(c_ref[...]), w_ref[...]) + b_ref[...]


def _modulation(cvecs, ada_w, ada_b):
    depth, d, n = ada_w.shape
    return pl.pallas_call(
        _mod_kernel,
        grid=(depth, n // MOD_TN),
        in_specs=[pl.BlockSpec((MOD_ROWS, d), lambda l, j: (0, 0)),
                  pl.BlockSpec((None, d, MOD_TN), lambda l, j: (l, 0, j)),
                  pl.BlockSpec((None, 1, MOD_TN), lambda l, j: (l, 0, j))],
        out_specs=pl.BlockSpec((None, MOD_ROWS, MOD_TN), lambda l, j: (l, 0, j)),
        out_shape=jax.ShapeDtypeStruct((depth, MOD_ROWS, n), F32),
        compiler_params=_cparams("arbitrary", "arbitrary"),
        name="modulation",
    )(cvecs, ada_w, ada_b.reshape(depth, 1, n))


ADALN_CHUNK = 64
ADALN_DOT_ROWS = 256


def _adaln_then(x_ref, g_ref, mod_ref, shift_row, scale_row, hn_ref, consume):
    mult = g_ref[...] * (1.0 + mod_ref[scale_row:scale_row + 1, :])
    shift = mod_ref[shift_row:shift_row + 1, :]
    for r0 in range(0, x_ref.shape[0], ADALN_DOT_ROWS):
        for s0 in range(r0, r0 + ADALN_DOT_ROWS, ADALN_CHUNK):
            r = slice(s0, s0 + ADALN_CHUNK)
            x = x_ref[r, :]
            ms = jnp.mean(x * x, axis=-1, keepdims=True)
            hn_ref[r, :] = (x * lax.rsqrt(ms + EPS) * mult + shift).astype(hn_ref.dtype)
        consume(slice(r0, r0 + ADALN_DOT_ROWS))


PROMPT_TILES = T_PROMPT // ROW_TILE


def _x_rows_specs(x, cols, col_of_j):
    if not isinstance(x, tuple):
        return [pl.BlockSpec((ROW_TILE, cols), lambda i, j: (i, col_of_j(j)))], [x]
    return ([pl.BlockSpec((ROW_TILE, cols),
                          lambda i, j: (jnp.minimum(i, PROMPT_TILES - 1), jnp.where(i < PROMPT_TILES, col_of_j(j), 0))),
             pl.BlockSpec((ROW_TILE, cols),
                          lambda i, j: (jnp.maximum(i - PROMPT_TILES, 0), jnp.where(i < PROMPT_TILES, 0, col_of_j(j))))],
            list(x))


def _for_x_rows(x_refs, fn):
    if len(x_refs) == 1:
        fn(x_refs[0])
        return
    in_prompt = pl.program_id(0) < PROMPT_TILES
    pl.when(in_prompt)(lambda: fn(x_refs[0]))
    pl.when(jnp.logical_not(in_prompt))(lambda: fn(x_refs[1]))


W_SLOTS = 3


def _adaln_mm_kernel(shift_row, scale_row, has_tail, n_x, *refs):
    x_refs, (g_ref, mod_ref, w_ref), refs = refs[:n_x], refs[n_x:n_x + 3], refs[n_x + 3:]
    if has_tail:
        wt_ref, o_ref, ot_ref, hn_ref, wbuf_ref, wsem = refs
    else:
        o_ref, hn_ref, wbuf_ref, wsem = refs

    nj = pl.num_programs(1)
    tn = wbuf_ref.shape[2]
    step = pl.program_id(0) * nj + pl.program_id(1)
    total = pl.num_programs(0) * nj

    def w_copy(s):
        col = pl.multiple_of((s % nj) * tn, tn)
        slot = s % W_SLOTS
        return pltpu.make_async_copy(w_ref.at[:, pl.ds(col, tn)], wbuf_ref.at[slot], wsem.at[slot])

    @pl.when(step == 0)
    def _():
        w_copy(step).start()
        w_copy(step + 1).start()

    @pl.when(step + 2 < total)
    def _():
        w_copy(step + 2).start()

    w_copy(step).wait()
    w_tile = wbuf_ref.at[step % W_SLOTS]

    def project(r):
        hn = hn_ref[r, :]
        o_ref[r, :] = _bdot(hn, w_tile[...]).astype(o_ref.dtype)
        if has_tail:
            ot_ref[r, :] = _bdot(hn, wt_ref[...])

    first = pl.program_id(1) == 0

    @pl.when(first)
    def _():
        _for_x_rows(x_refs, lambda x_ref: _adaln_then(x_ref, g_ref, mod_ref, shift_row, scale_row, hn_ref, project))

    @pl.when(jnp.logical_not(first))
    def _():
        o_ref[...] = _bdot(hn_ref[...], w_tile[...]).astype(o_ref.dtype)


def _adaln_matmul(x, gain, modt, shift_row, scale_row, w, tn, n_out=None, tail=0, out_dtype=F32, name="adaln_mm"):
    t, d = T_ALL, w.shape[0]
    n_out = w.shape[1] if n_out is None else n_out
    in_specs, args = _x_rows_specs(x, d, lambda j: 0)
    n_x = len(args)
    in_specs += [pl.BlockSpec((1, d), lambda i, j: (0, 0)),
                 pl.BlockSpec((None, 8, d), lambda i, j: (i, 0, 0)),
                 pl.BlockSpec(memory_space=pl.ANY)]
    args += [gain.reshape(1, d), modt, w]
    out_specs = [pl.BlockSpec((ROW_TILE, tn), lambda i, j: (i, j))]
    out_shape = [jax.ShapeDtypeStruct((t, n_out), out_dtype)]
    if tail:
        in_specs.append(pl.BlockSpec((d, tail), lambda i, j: (0, n_out // tail)))
        args.append(w)
        out_specs.append(pl.BlockSpec((ROW_TILE, tail), lambda i, j: (i, 0)))
        out_shape.append(jax.ShapeDtypeStruct((t, tail), F32))
    return pl.pallas_call(
        functools.partial(_adaln_mm_kernel, shift_row, scale_row, bool(tail), n_x),
        grid=(t // ROW_TILE, n_out // tn),
        in_specs=in_specs,
        out_specs=out_specs,
        out_shape=out_shape,
        scratch_shapes=[pltpu.VMEM((ROW_TILE, d), BF16), pltpu.VMEM((W_SLOTS, d, tn), w.dtype),
                        pltpu.SemaphoreType.DMA((W_SLOTS,))],
        compiler_params=_cparams("arbitrary", "arbitrary"),
        name=name,
    )(*args)


FFN_TF = 512


def _gateup_kernel(x_ref, g_ref, mod_ref, wg_ref, wu_ref, h_ref, hn_ref):
    def gate_up(r):
        hn = hn_ref[r, :]
        h_ref[r, :] = (_silu(_bdot(hn, wg_ref[...])) * _bdot(hn, wu_ref[...])).astype(h_ref.dtype)

    first = pl.program_id(1) == 0

    @pl.when(first)
    def _():
        _adaln_then(x_ref, g_ref, mod_ref, 3, 4, hn_ref, gate_up)

    @pl.when(jnp.logical_not(first))
    def _():
        gate_up(slice(None))


def _ffn_gateup(x, gain, modt, wg, wu):
    t, d = x.shape
    ff = wg.shape[1]
    return pl.pallas_call(
        _gateup_kernel,
        grid=(t // ROW_TILE, ff // FFN_TF),
        in_specs=[pl.BlockSpec((ROW_TILE, d), lambda i, f: (i, 0)),
                  pl.BlockSpec((1, d), lambda i, f: (0, 0)),
                  pl.BlockSpec((None, 8, d), lambda i, f: (i, 0, 0)),
                  pl.BlockSpec((d, FFN_TF), lambda i, f: (0, f)),
                  pl.BlockSpec((d, FFN_TF), lambda i, f: (0, f))],
        out_specs=pl.BlockSpec((ROW_TILE, FFN_TF), lambda i, f: (i, f)),
        out_shape=jax.ShapeDtypeStruct((t, ff), BF16),
        scratch_shapes=[pltpu.VMEM((ROW_TILE, d), BF16)],
        compiler_params=_cparams("arbitrary", "arbitrary"),
        name="ffn_gateup",
    )(x, gain.reshape(1, d), modt, wg, wu)


def _proj_res_kernel(group_sizes, gate_row, norm, *refs):
    n_a = len(group_sizes) - 1 - int(norm)
    it = iter(refs)
    a_groups = [[next(it) for _ in range(group_sizes[k])] for k in range(n_a)]
    w_refs = [next(it) for _ in range(n_a)]
    x_group = [next(it) for _ in range(group_sizes[n_a])]
    mod_ref = next(it)
    ss_group = [next(it) for _ in range(group_sizes[n_a + 1])] if norm else None
    o_ref = next(it)

    def run(which):
        def pick(group):
            return group[min(which, len(group) - 1)]

        acc = _bdot(pick(a_groups[0])[...], w_refs[0][...])
        for k in range(1, n_a):
            acc = acc + _bdot(pick(a_groups[k])[...], w_refs[k][...])
        if norm:
            ss_ref = pick(ss_group)
            k_total = sum(g[0].shape[1] for g in a_groups)
            ss = ss_ref[:, 0:LANES]
            for k in range(1, ss_ref.shape[1] // LANES):
                ss = ss + ss_ref[:, k * LANES:(k + 1) * LANES]
            rs = lax.rsqrt(ss * (1.0 / k_total) + EPS)
            acc = acc * jnp.concatenate([rs] * (acc.shape[1] // LANES), axis=1)
        o_ref[...] = pick(x_group)[...] + mod_ref[gate_row:gate_row + 1, :] * acc

    if max(group_sizes) == 1:
        run(0)
    else:
        in_prompt = pl.program_id(0) < PROMPT_TILES
        pl.when(in_prompt)(lambda: run(0))
        pl.when(jnp.logical_not(in_prompt))(lambda: run(1))


def _proj_residual(a_list, w, x, modt, gate_row, tn, row_ss=None, name="proj_res"):
    t, d = T_ALL, w.shape[1]
    norm = row_ss is not None
    in_specs, args, group_sizes, k0 = [], [], [], 0

    def add_rows(arr, cols, col_of_j):
        specs, arrs = _x_rows_specs(arr, cols, col_of_j)
        in_specs.extend(specs)
        args.extend(arrs)
        group_sizes.append(len(arrs))

    widths = [(a[0] if isinstance(a, tuple) else a).shape[1] for a in a_list]
    for a, ka in zip(a_list, widths):
        add_rows(a, ka, lambda j: 0)
    for ka in widths:
        assert k0 % ka == 0
        in_specs.append(pl.BlockSpec((ka, tn), lambda i, j, kb=k0 // ka: (kb, j)))
        args.append(w)
        k0 += ka
    add_rows(x, tn, lambda j: j)
    in_specs.append(pl.BlockSpec((None, 8, tn), lambda i, j: (i, 0, j)))
    args.append(modt)
    if norm:
        add_rows(row_ss, (row_ss[0] if isinstance(row_ss, tuple) else row_ss).shape[1], lambda j: 0)
    return pl.pallas_call(
        functools.partial(_proj_res_kernel, tuple(group_sizes), gate_row, norm),
        grid=(t // ROW_TILE, d // tn),
        in_specs=in_specs,
        out_specs=pl.BlockSpec((ROW_TILE, tn), lambda i, j: (i, j)),
        out_shape=jax.ShapeDtypeStruct((t, d), F32),
        compiler_params=_cparams("arbitrary", "arbitrary"),
        name=name,
    )(*args)


def _ret_kernel(n_chunks, rope, has_s0, emit_state, lg_ref, *refs):
    it = iter(refs)
    q_ref, k_ref, v_ref, gt_ref = next(it), next(it), next(it), next(it)
    cos_ref, sin_ref = (next(it), next(it)) if rope else (None, None)
    s0_ref = next(it) if has_s0 else None
    gain_ref = next(it)
    o_ref = next(it)
    sfin_ref = next(it) if emit_state else None
    qs_ref, ks_ref, sfs_ref, dm_ref, dec_ref, sf_ref, sb_ref = (next(it) for _ in range(7))

    h = pl.program_id(1)
    lgf = lg_ref[0, h]
    lgb = lg_ref[1, h]
    half = RET_DK // 2

    def prep(c, carry):
        r = _rows(c)
        q = q_ref[r, :].astype(F32)
        k = k_ref[r, :].astype(F32) * (RET_DK ** -0.5)
        if rope:
            cs, sn = cos_ref[r, :], sin_ref[r, :]
            for src, dst in ((q, qs_ref), (k, ks_ref)):
                x1, x2 = src[:, :half], src[:, half:]
                dst[r, :half] = (x1 * cs - x2 * sn).astype(BF16)
                dst[r, half:] = (x1 * sn + x2 * cs).astype(BF16)
        else:
            qs_ref[r, :] = q.astype(BF16)
            ks_ref[r, :] = k.astype(BF16)
        return carry

    lax.fori_loop(0, n_chunks, prep, 0)

    ii = lax.broadcasted_iota(jnp.int32, (BLOCK, BLOCK), 0)
    jj = lax.broadcasted_iota(jnp.int32, (BLOCK, BLOCK), 1)
    diff = (ii - jj).astype(F32)
    dm_ref[...] = jnp.exp(jnp.where(jj <= ii, diff * lgf, -diff * lgb))
    pos = lax.broadcasted_iota(jnp.int32, (BLOCK, RET_DV), 0).astype(F32)
    dec_ref[0] = jnp.exp((pos + 1.0) * lgf)
    dec_ref[1] = jnp.exp((BLOCK - pos) * lgb)
    dec_ref[2] = jnp.exp((BLOCK - 1.0 - pos) * lgf)
    dec_ref[3] = jnp.exp(pos * lgb)
    tot_f = jnp.exp(jnp.full((1, RET_DV), BLOCK * lgf, F32))
    tot_b = jnp.exp(jnp.full((1, RET_DV), BLOCK * lgb, F32))

    if has_s0:
        sf_ref[...] = s0_ref[0]
        sb_ref[...] = s0_ref[1]
    else:
        sf_ref[...] = jnp.zeros_like(sf_ref)
        sb_ref[...] = jnp.zeros_like(sb_ref)

    def fwd(c, carry):
        r = _rows(c)
        sfs_ref[c] = sf_ref[...].astype(BF16)
        kd = ks_ref[r, :].astype(F32) * dec_ref[2]
        sf_ref[...] = sf_ref[...] * tot_f + _bdot_tn(kd, v_ref[r, :])
        return carry

    lax.fori_loop(0, n_chunks, fwd, 0, unroll=min(4, n_chunks))
    if emit_state:
        sfin_ref[0] = sf_ref[...]

    def bwd(t, carry):
        c = n_chunks - 1 - t
        r = _rows(c)
        q = qs_ref[r, :]
        k = ks_ref[r, :]
        v = v_ref[r, :].astype(BF16)
        p = _bdot_nt(q, k) * dm_ref[...]
        o = _bdot(p, v)
        o = o + _bdot(q, sfs_ref[c]) * dec_ref[0]
        o = o + _bdot(q, sb_ref[...]) * dec_ref[1]
        ms = jnp.mean(o * o, axis=-1, keepdims=True)
        y = o * lax.rsqrt(ms + EPS) * gain_ref[...]
        o_ref[r, :] = (y * _silu(gt_ref[r, :].astype(F32))).astype(o_ref.dtype)
        kd = k.astype(F32) * dec_ref[3]
        sb_ref[...] = sb_ref[...] * tot_b + _bdot_tn(kd, v)
        return carry

    lax.fori_loop(0, n_chunks, bwd, 0, unroll=min(4, n_chunks))
    if emit_state:
        sfin_ref[1] = sb_ref[...]


def _retention(proj, lg, ret_norm, seq_len, n_seq, row_block0, ropes=None, s0=None, emit_state=False):
    n_chunks = seq_len // BLOCK
    rope, has_s0 = ropes is not None, s0 is not None

    def col(cb):
        return pl.BlockSpec((seq_len, RET_DK), lambda b, h, cb=cb: (b + row_block0, cb * RET_HEADS + h))

    in_specs = [pl.BlockSpec(memory_space=pltpu.SMEM), col(0), col(1), col(2), col(3)]
    args = [lg, proj, proj, proj, proj]
    if rope:
        in_specs += [pl.BlockSpec((seq_len, RET_DK // 2), lambda b, h: (0, 0))] * 2
        args += list(ropes)
    if has_s0:
        in_specs.append(pl.BlockSpec((None, None, 2, None, RET_DK, RET_DV), lambda b, h: (b, 0, 0, h, 0, 0)))
        args.append(s0)
    in_specs.append(pl.BlockSpec((1, RET_DV), lambda b, h: (0, h)))
    args.append(ret_norm.reshape(1, RET_HEADS * RET_DV))
    out_specs = [pl.BlockSpec((seq_len, RET_DV), lambda b, h: (b, h))]
    out_shape = [jax.ShapeDtypeStruct((n_seq * seq_len, RET_HEADS * RET_DV), BF16)]
    if emit_state:
        out_specs.append(pl.BlockSpec((None, None, 2, None, RET_DK, RET_DV), lambda b, h: (b, 0, 0, h, 0, 0)))
        out_shape.append(jax.ShapeDtypeStruct((n_seq, 1, 2, RET_HEADS, RET_DK, RET_DV), F32))
    return pl.pallas_call(
        functools.partial(_ret_kernel, n_chunks, rope, has_s0, emit_state),
        grid=(n_seq, RET_HEADS),
        in_specs=in_specs,
        out_specs=out_specs,
        out_shape=out_shape,
        scratch_shapes=[pltpu.VMEM((seq_len, RET_DK), BF16), pltpu.VMEM((seq_len, RET_DK), BF16),
                        pltpu.VMEM((n_chunks, RET_DK, RET_DV), BF16), pltpu.VMEM((BLOCK, BLOCK), F32),
                        pltpu.VMEM((4, BLOCK, RET_DV), F32), pltpu.VMEM((RET_DK, RET_DV), F32),
                        pltpu.VMEM((RET_DK, RET_DV), F32)],
        compiler_params=_cparams("arbitrary", "arbitrary"),
        name="retention_latent" if rope else "retention_prompt",
    )(*args)


def _head_norm(x, gain):
    x = x.astype(F32)
    return x * lax.rsqrt(jnp.mean(x * x, axis=-1, keepdims=True) + EPS) * gain


def _rope_full(x, cs, sn):
    return x * cs + pltpu.roll(x, ATT_HD // 2, 1) * sn


ATT_SPLIT = 2


def _sink_column(sink_ref, head0, n_heads, rows_per_head):
    n = n_heads * rows_per_head
    head = lax.broadcasted_iota(jnp.int32, (n, 1), 0) // rows_per_head
    col = jnp.full((n, 1), sink_ref[head0], F32)
    for g in range(1, n_heads):
        col = jnp.where(head == g, sink_ref[head0 + g], col)
    return col


def _att_latent_kernel(sink_ref, q_ref, k_ref, v_ref, ck_ref, cv_ref, cos_ref, sin_ref, qn_ref, kn_ref,
                       o_ref, kp_ref, vp_ref, ckp_ref, cvp_ref):
    kh, qb = pl.program_id(1), pl.program_id(2)
    n_chunks = DEC_SEQ // BLOCK
    loc = 3 * BLOCK

    @pl.when(qb == 0)
    def _():
        def prep(c, carry):
            r = _rows(c)
            kp_ref[r, :] = _rope_full(_head_norm(k_ref[r, :], kn_ref[...]), cos_ref[r, :], sin_ref[r, :]).astype(BF16)
            vp_ref[r, :ATT_HD] = v_ref[r, :].astype(BF16)
            vp_ref[r, ATT_HD:] = jnp.ones((BLOCK, ATT_HD), BF16)
            return carry

        lax.fori_loop(0, n_chunks, prep, 0)
        ckp_ref[...] = ck_ref[...].astype(BF16)
        cvp_ref[:, :ATT_HD] = cv_ref[...].astype(BF16)
        cvp_ref[:, ATT_HD:] = jnp.ones((PAST_LEN, ATT_HD), BF16)

    rq = _rows(qb)
    cs, sn = cos_ref[rq, :], sin_ref[rq, :]
    qg = qn_ref[...] * (ATT_HD ** -0.5)
    start = pl.multiple_of(jnp.clip((qb - 1) * BLOCK, 0, DEC_SEQ - loc), BLOCK)
    kl, vl = kp_ref[pl.ds(start, loc), :], vp_ref[pl.ds(start, loc), :]
    qpos = qb * BLOCK + (lax.broadcasted_iota(jnp.int32, (ATT_SPLIT * BLOCK, loc), 0) & (BLOCK - 1))
    kpos = start + lax.broadcasted_iota(jnp.int32, (ATT_SPLIT * BLOCK, loc), 1)
    in_window = jnp.abs(qpos - kpos) <= WINDOW
    for g0 in range(0, ATT_GROUP, ATT_SPLIT):
        q = jnp.concatenate(
            [_rope_full(_head_norm(q_ref[:, g * ATT_HD:(g + 1) * ATT_HD], qg), cs, sn).astype(BF16)
             for g in range(g0, g0 + ATT_SPLIT)], axis=0)
        s_loc = jnp.where(in_window, _bdot_nt(q, kl), NEG_INF)
        s_ctx = _bdot_nt(q, ckp_ref[...])
        sink = _sink_column(sink_ref, kh * ATT_GROUP + g0, ATT_SPLIT, BLOCK)
        m = jnp.maximum(jnp.maximum(jnp.max(s_loc, axis=-1, keepdims=True),
                                    jnp.max(s_ctx, axis=-1, keepdims=True)), sink)
        pv = _bdot(jnp.exp(s_loc - m), vl) + _bdot(jnp.exp(s_ctx - m), cvp_ref[...])
        o = pv[:, :ATT_HD] / (pv[:, ATT_HD:] + jnp.exp(sink - m))
        for g in range(ATT_SPLIT):
            o_ref[:, (g0 + g) * ATT_HD:(g0 + g + 1) * ATT_HD] = o[g * BLOCK:(g + 1) * BLOCK, :].astype(o_ref.dtype)


def _attention_latent(proj, cache_k, cache_v, sink, cos_a, sin_a, q_norm, k_norm):
    nqb = DEC_SEQ // BLOCK
    rb0 = T_PROMPT // DEC_SEQ
    qcol0 = 4 * RET_HEADS * RET_DK // (ATT_GROUP * ATT_HD)
    kcol0 = (4 * RET_HEADS * RET_DK + ATT_HEADS * ATT_HD) // ATT_HD
    vcol0 = kcol0 + ATT_KV_HEADS
    ck = cache_k.reshape(DEC_BATCH, PAST_LEN, ATT_KV_HEADS * ATT_HD)
    cv = cache_v.reshape(DEC_BATCH, PAST_LEN, ATT_KV_HEADS * ATT_HD)
    return pl.pallas_call(
        _att_latent_kernel,
        grid=(DEC_BATCH, ATT_KV_HEADS, nqb),
        in_specs=[pl.BlockSpec(memory_space=pltpu.SMEM),
                  pl.BlockSpec((BLOCK, ATT_GROUP * ATT_HD),
                               lambda b, kh, qb: (T_PROMPT // BLOCK + b * nqb + qb, qcol0 + kh)),
                  pl.BlockSpec((DEC_SEQ, ATT_HD), lambda b, kh, qb: (rb0 + b, kcol0 + kh)),
                  pl.BlockSpec((DEC_SEQ, ATT_HD), lambda b, kh, qb: (rb0 + b, vcol0 + kh)),
                  pl.BlockSpec((None, PAST_LEN, ATT_HD), lambda b, kh, qb: (b, 0, kh)),
                  pl.BlockSpec((None, PAST_LEN, ATT_HD), lambda b, kh, qb: (b, 0, kh)),
                  pl.BlockSpec((DEC_SEQ, ATT_HD), lambda b, kh, qb: (0, 0)),
                  pl.BlockSpec((DEC_SEQ, ATT_HD), lambda b, kh, qb: (0, 0)),
                  pl.BlockSpec((1, ATT_HD), lambda b, kh, qb: (0, 0)),
                  pl.BlockSpec((1, ATT_HD), lambda b, kh, qb: (0, 0))],
        out_specs=pl.BlockSpec((BLOCK, ATT_GROUP * ATT_HD), lambda b, kh, qb: (b * nqb + qb, kh)),
        out_shape=jax.ShapeDtypeStruct((T_SAMPLE, ATT_HEADS * ATT_HD), BF16),
        scratch_shapes=[pltpu.VMEM((DEC_SEQ, ATT_HD), BF16), pltpu.VMEM((DEC_SEQ, 2 * ATT_HD), BF16),
                        pltpu.VMEM((PAST_LEN, ATT_HD), BF16), pltpu.VMEM((PAST_LEN, 2 * ATT_HD), BF16)],
        compiler_params=_cparams("arbitrary", "arbitrary", "arbitrary"),
        name="attention_latent",
    )(sink, proj, proj, proj, ck, cv, cos_a, sin_a, q_norm.reshape(1, ATT_HD), k_norm.reshape(1, ATT_HD))


def _att_prompt_kernel(sink_ref, q_ref, k_ref, v_ref, qn_ref, kn_ref, o_ref, nk_ref, nv_ref):
    kh = pl.program_id(1)
    kn = _head_norm(k_ref[...], kn_ref[...])
    v = v_ref[...]
    nk_ref[...] = kn
    nv_ref[...] = v.astype(F32)
    qg = qn_ref[...] * (ATT_HD ** -0.5)
    q = jnp.concatenate([_head_norm(q_ref[:, g * ATT_HD:(g + 1) * ATT_HD], qg).astype(BF16)
                         for g in range(ATT_GROUP)], axis=0)
    s = _bdot_nt(q, kn)
    sink = _sink_column(sink_ref, kh * ATT_GROUP, ATT_GROUP, SEQ)
    m = jnp.maximum(jnp.max(s, axis=-1, keepdims=True), sink)
    pv = _bdot(jnp.exp(s - m), jnp.concatenate([v.astype(BF16), jnp.ones((SEQ, ATT_HD), BF16)], axis=1))
    o = pv[:, :ATT_HD] / (pv[:, ATT_HD:] + jnp.exp(sink - m))
    for g in range(ATT_GROUP):
        o_ref[:, g * ATT_HD:(g + 1) * ATT_HD] = o[g * SEQ:(g + 1) * SEQ, :].astype(o_ref.dtype)


def _attention_prompt(proj, sink, q_norm, k_norm):
    qcol0 = 4 * RET_HEADS * RET_DK // (ATT_GROUP * ATT_HD)
    kcol0 = (4 * RET_HEADS * RET_DK + ATT_HEADS * ATT_HD) // ATT_HD
    vcol0 = kcol0 + ATT_KV_HEADS
    kv_spec = pl.BlockSpec((None, SEQ, ATT_HD), lambda b, kh: (b, 0, kh))
    kv_shape = jax.ShapeDtypeStruct((BATCH, SEQ, ATT_KV_HEADS * ATT_HD), F32)
    return pl.pallas_call(
        _att_prompt_kernel,
        grid=(BATCH, ATT_KV_HEADS),
        in_specs=[pl.BlockSpec(memory_space=pltpu.SMEM),
                  pl.BlockSpec((SEQ, ATT_GROUP * ATT_HD), lambda b, kh: (b, qcol0 + kh)),
                  pl.BlockSpec((SEQ, ATT_HD), lambda b, kh: (b, kcol0 + kh)),
                  pl.BlockSpec((SEQ, ATT_HD), lambda b, kh: (b, vcol0 + kh)),
                  pl.BlockSpec((1, ATT_HD), lambda b, kh: (0, 0)),
                  pl.BlockSpec((1, ATT_HD), lambda b, kh: (0, 0))],
        out_specs=[pl.BlockSpec((SEQ, ATT_GROUP * ATT_HD), lambda b, kh: (b, kh)), kv_spec, kv_spec],
        out_shape=[jax.ShapeDtypeStruct((T_PROMPT, ATT_HEADS * ATT_HD), BF16), kv_shape, kv_shape],
        compiler_params=_cparams("arbitrary", "arbitrary"),
        name="attention_prompt",
    )(sink, proj, proj, proj, q_norm.reshape(1, ATT_HD), k_norm.reshape(1, ATT_HD))


def _split_dot(m01, a):
    hi = a.astype(BF16)
    r1 = a - hi.astype(F32)
    mid = r1.astype(BF16)
    lo = (r1 - mid.astype(F32)).astype(BF16)
    return (jnp.dot(m01, hi, preferred_element_type=F32) + jnp.dot(m01, mid, preferred_element_type=F32)
            + jnp.dot(m01, lo, preferred_element_type=F32))


PREP_CHUNKS = ROW_TILE // BLOCK


def _ssd_prep_kernel(raw_ref, bias_ref, alog_ref, cum_ref, dt_ref, w_ref, tot_ref, ecum_ref):
    ii = lax.broadcasted_iota(jnp.int32, (BLOCK, BLOCK), 0)
    jj = lax.broadcasted_iota(jnp.int32, (BLOCK, BLOCK), 1)
    lower = jnp.where(jj <= ii, 1.0, 0.0).astype(BF16)
    upper = jnp.where(jj >= ii, 1.0, 0.0).astype(BF16)
    fwd_lane = lax.broadcasted_iota(jnp.int32, (BLOCK, LANES), 1) < SSD_HEADS
    neg_a = -jnp.exp(alog_ref[...])

    def chunk(k, carry):
        x = raw_ref[_rows(k), :] + bias_ref[...]
        dt = jnp.maximum(x, 0.0) + jnp.log1p(jnp.exp(-jnp.abs(x)))
        a = dt * neg_a
        incl = _split_dot(lower, a)
        rincl = _split_dot(upper, a)
        cum = jnp.where(fwd_lane, incl, rincl)
        tot = jnp.where(fwd_lane[:1], incl[BLOCK - 1:BLOCK, :], rincl[0:1, :])
        cum_ref[k] = cum.T
        dt_ref[k] = dt.T
        w_ref[k] = (dt * jnp.exp(tot - cum)).T
        tot_ref[k] = jnp.broadcast_to(jnp.exp(tot), (BLOCK, LANES)).T
        ecum_ref[k] = jnp.exp(cum).T
        return carry

    lax.fori_loop(0, PREP_CHUNKS, chunk, 0)


def _ssd_prep(dt_raw, dt_bias, a_log):
    nc = T_ALL // BLOCK
    spec = pl.BlockSpec((PREP_CHUNKS, 2 * SSD_HEADS, BLOCK), lambda c: (c, 0, 0))
    shape = jax.ShapeDtypeStruct((nc, 2 * SSD_HEADS, BLOCK), F32)
    return pl.pallas_call(
        _ssd_prep_kernel,
        grid=(nc // PREP_CHUNKS,),
        in_specs=[pl.BlockSpec((ROW_TILE, 2 * SSD_HEADS), lambda c: (c, 0)),
                  pl.BlockSpec((1, 2 * SSD_HEADS), lambda c: (0, 0)),
                  pl.BlockSpec((1, 2 * SSD_HEADS), lambda c: (0, 0))],
        out_specs=[spec] * 5,
        out_shape=[shape] * 5,
        compiler_params=_cparams("arbitrary"),
        name="ssd_prep",
    )(dt_raw, dt_bias.reshape(1, 2 * SSD_HEADS), a_log.reshape(1, 2 * SSD_HEADS))


GW = SSD_R * SSD_P
HALO = 8


def _pair_tiles(per_head):
    low = lax.broadcasted_iota(jnp.int32, per_head[0].shape, 1) < SSD_P
    return jnp.concatenate([jnp.where(low, per_head[2 * t], per_head[2 * t + 1]) for t in range(SSD_R // 2)],
                           axis=1)


def _row_bcast(ref, c, r):
    return jnp.broadcast_to(ref[c, r:r + 1, :], (BLOCK, BLOCK))


def _pair_cols(ref, c):
    top = lax.broadcasted_iota(jnp.int32, (BLOCK, BLOCK), 0) < SSD_P
    return jnp.concatenate(
        [jnp.where(top, _row_bcast(ref, c, 2 * t), _row_bcast(ref, c, 2 * t + 1)).T for t in range(SSD_R // 2)],
        axis=1)


def _ssd_kernel(n_chunks, has_s0, emit_state, *refs):
    it = iter(refs)
    z_ref, x_ref, b_ref, c_ref = (next(it) for _ in range(4))
    cumf_ref, cumb_ref, dtf_ref, dtb_ref, wf_ref, wb_ref, totf_ref, totb_ref, ecf_ref, ecb_ref = (
        next(it) for _ in range(10))
    cwx_ref, cwb_ref, cwc_ref, cbx_ref, cbb_ref, cbc_ref, d_ref, ng_ref = (next(it) for _ in range(8))
    s0_ref = next(it) if has_s0 else None
    o_ref, ss_ref = next(it), next(it)
    sfin_ref = next(it) if emit_state else None
    pad_ref, xc_ref, bc_ref, cc_ref, sfs_ref, sf_ref, sb_ref = (next(it) for _ in range(7))
    seq_len = n_chunks * BLOCK

    pad_ref[0:HALO, :] = jnp.zeros((HALO, GW + 2 * SSD_N), F32)
    pad_ref[HALO + seq_len:2 * HALO + seq_len, :] = jnp.zeros((HALO, GW + 2 * SSD_N), F32)

    def fill(c, carry):
        dst = pl.ds(pl.multiple_of(c * BLOCK, BLOCK) + HALO, BLOCK)
        r = _rows(c)
        pad_ref[dst, 0:GW] = x_ref[r, :].astype(F32)
        pad_ref[dst, GW:GW + SSD_N] = b_ref[r, :].astype(F32)
        pad_ref[dst, GW + SSD_N:GW + 2 * SSD_N] = c_ref[r, :].astype(F32)
        return carry

    lax.fori_loop(0, n_chunks, fill, 0)

    def conv(c):
        r = _rows(c)
        src = pl.ds(pl.multiple_of(c * BLOCK, BLOCK), BLOCK + 2 * HALO)
        for col0, width, cw_ref, cb_ref, dst in ((0, GW, cwx_ref, cbx_ref, xc_ref),
                                                 (GW, SSD_N, cwb_ref, cbb_ref, bc_ref),
                                                 (GW + SSD_N, SSD_N, cwc_ref, cbc_ref, cc_ref)):
            for t in range(width // LANES):
                tl = slice(t * LANES, (t + 1) * LANES)
                win = pad_ref[src, col0 + t * LANES:col0 + (t + 1) * LANES]
                acc = jnp.broadcast_to(cb_ref[:, tl], (BLOCK, LANES))
                for w in range(CONV_W):
                    off = HALO - CONV_W // 2 + w
                    acc = acc + win[off:off + BLOCK, :] * cw_ref[w:w + 1, tl]
                dst[r, tl] = _silu(acc).astype(dst.dtype)

    if has_s0:
        for r in range(SSD_R):
            sf_ref[:, r * SSD_P:(r + 1) * SSD_P] = s0_ref[0, r]
            sb_ref[:, r * SSD_P:(r + 1) * SSD_P] = s0_ref[1, r]
    else:
        sf_ref[...] = jnp.zeros_like(sf_ref)
        sb_ref[...] = jnp.zeros_like(sb_ref)

    def state_update(s_ref, c, w_ref, tot_ref, bm, xs):
        tot = _pair_tiles([tot_ref[c, r:r + 1, :] for r in range(SSD_R)])
        s_ref[...] = s_ref[...] * tot + _bdot_tn(bm, xs * _pair_cols(w_ref, c))

    def fwd_step(c):
        r = _rows(c)
        sfs_ref[c] = sf_ref[...].astype(BF16)
        state_update(sf_ref, c, wf_ref, totf_ref, bc_ref[r, :], xc_ref[r, :])

    def fwd(c, carry):
        fwd_step(c)
        conv(c + 1)
        return carry

    conv(0)
    lax.fori_loop(0, n_chunks - 1, fwd, 0, unroll=3 if (n_chunks - 1) % 3 == 0 else 1)
    fwd_step(n_chunks - 1)
    if emit_state:
        for r in range(SSD_R):
            sfin_ref[0, r] = sf_ref[:, r * SSD_P:(r + 1) * SSD_P]

    ii = lax.broadcasted_iota(jnp.int32, (BLOCK, BLOCK), 0)
    jj = lax.broadcasted_iota(jnp.int32, (BLOCK, BLOCK), 1)
    causal = jj <= ii
    anti = ii <= jj
    low = lax.broadcasted_iota(jnp.int32, (BLOCK, LANES), 1) < SSD_P

    @pl.when(pl.program_id(1) == 0)
    def _():
        ss_ref[...] = jnp.zeros_like(ss_ref)

    def bwd(t, carry):
        c = n_chunks - 1 - t
        r = _rows(c)
        cm, bm, xs = cc_ref[r, :], bc_ref[r, :], xc_ref[r, :]
        xb = xs.astype(BF16)
        sc = _bdot_nt(cm, bm)
        yf = _bdot(cm, sfs_ref[c])
        yb = _bdot(cm, sb_ref[...])
        tiles = []
        for t2 in range(SSD_R // 2):
            xt = xb[:, t2 * LANES:(t2 + 1) * LANES]
            acc = None
            for hh in range(2):
                hr = 2 * t2 + hh
                rf, rb = _row_bcast(cumf_ref, c, hr), _row_bcast(cumb_ref, c, hr)
                e = jnp.exp(jnp.where(anti, rf, rb).T - jnp.where(causal, rf, rb))
                e = e * jnp.where(causal, _row_bcast(dtf_ref, c, hr), _row_bcast(dtb_ref, c, hr))
                xh = jnp.where(low, xt, 0.0) if hh == 0 else jnp.where(low, 0.0, xt)
                part = _bdot(sc * e, xh)
                acc = part if acc is None else acc + part
            tiles.append(acc)
        y = (jnp.concatenate(tiles, axis=1) + _pair_cols(ecf_ref, c) * yf + _pair_cols(ecb_ref, c) * yb
             + d_ref[...] * xs)
        yg = y * _silu(z_ref[r, :].astype(F32))
        ss_ref[r, :] += jnp.broadcast_to(jnp.sum(yg * yg, axis=-1, keepdims=True), (BLOCK, LANES))
        o_ref[r, :] = (yg * ng_ref[...]).astype(o_ref.dtype)
        state_update(sb_ref, c, wb_ref, totb_ref, bm, xs)
        return carry

    lax.fori_loop(0, n_chunks, bwd, 0, unroll=min(4, n_chunks))
    if emit_state:
        for r in range(SSD_R):
            sfin_ref[1, r] = sb_ref[:, r * SSD_P:(r + 1) * SSD_P]


def _ssd_scan(zx, prep, conv_w, conv_b, d_exp, norm_gain, seq_len, n_seq, row_block0, s0=None, emit_state=False):
    n_chunks = seq_len // BLOCK
    has_s0 = s0 is not None
    xcol0 = D_INNER // GW
    bcol0 = 2 * D_INNER // SSD_N
    ccol0 = bcol0 + SSD_GROUPS
    cwb0 = D_INNER // SSD_N

    def rowcol(width, col0):
        return pl.BlockSpec((seq_len, width), lambda b, g, col0=col0: (b + row_block0, col0 + g))

    def headrows(direction):
        return pl.BlockSpec((n_chunks, SSD_R, BLOCK),
                            lambda b, g, direction=direction: (b + row_block0, direction * SSD_GROUPS + g, 0))

    in_specs = [rowcol(GW, 0), rowcol(GW, xcol0), rowcol(SSD_N, bcol0), rowcol(SSD_N, ccol0)]
    args = [zx, zx, zx, zx]
    for arr in prep:
        in_specs += [headrows(0), headrows(1)]
        args += [arr, arr]
    in_specs += [pl.BlockSpec((CONV_W, GW), lambda b, g: (0, g)),
                 pl.BlockSpec((CONV_W, SSD_N), lambda b, g: (0, cwb0 + g)),
                 pl.BlockSpec((CONV_W, SSD_N), lambda b, g: (0, cwb0 + SSD_GROUPS + g)),
                 pl.BlockSpec((1, GW), lambda b, g: (0, g)),
                 pl.BlockSpec((1, SSD_N), lambda b, g: (0, cwb0 + g)),
                 pl.BlockSpec((1, SSD_N), lambda b, g: (0, cwb0 + SSD_GROUPS + g)),
                 pl.BlockSpec((1, GW), lambda b, g: (0, g)),
                 pl.BlockSpec((1, GW), lambda b, g: (0, g))]
    args += [conv_w, conv_w, conv_w, conv_b, conv_b, conv_b, d_exp, norm_gain.reshape(1, D_INNER)]
    state_spec = pl.BlockSpec((None, None, 2, SSD_R, SSD_N, SSD_P), lambda b, g: (b, 0, 0, g, 0, 0))
    if has_s0:
        in_specs.append(state_spec)
        args.append(s0)
    out_specs = [pl.BlockSpec((seq_len, GW), lambda b, g: (b, g)),
                 pl.BlockSpec((seq_len, LANES), lambda b, g: (b, 0))]
    out_shape = [jax.ShapeDtypeStruct((n_seq * seq_len, D_INNER), BF16),
                 jax.ShapeDtypeStruct((n_seq * seq_len, LANES), F32)]
    if emit_state:
        out_specs.append(state_spec)
        out_shape.append(jax.ShapeDtypeStruct((n_seq, 1, 2, SSD_HEADS, SSD_N, SSD_P), F32))
    return pl.pallas_call(
        functools.partial(_ssd_kernel, n_chunks, has_s0, emit_state),
        grid=(n_seq, SSD_GROUPS),
        in_specs=in_specs,
        out_specs=out_specs,
        out_shape=out_shape,
        scratch_shapes=[pltpu.VMEM((seq_len + 2 * HALO, GW + 2 * SSD_N), F32),
                        pltpu.VMEM((seq_len, GW), F32), pltpu.VMEM((seq_len, SSD_N), BF16),
                        pltpu.VMEM((seq_len, SSD_N), BF16), pltpu.VMEM((n_chunks, SSD_N, GW), BF16),
                        pltpu.VMEM((SSD_N, GW), F32), pltpu.VMEM((SSD_N, GW), F32)],
        compiler_params=_cparams("arbitrary", "arbitrary"),
        name="ssd_scan_latent" if has_s0 else "ssd_scan_prompt",
    )(*args)


def _router_kernel(x_ref, g_ref, mod_ref, rw_ref, hn_ref, idx_ref, wgt_ref, hf_ref):
    rw = rw_ref[...]
    r_hi = rw.astype(BF16)
    r_lo = (rw - r_hi.astype(F32)).astype(BF16)
    r_both = jnp.concatenate([r_hi, r_lo], axis=1)

    def route(r):
        hf = hf_ref[r, :]
        h_hi = hf.astype(BF16)
        h_lo = (hf - h_hi.astype(F32)).astype(BF16)
        hn_ref[r, :] = h_hi
        t1 = jnp.dot(h_hi, r_both, preferred_element_type=F32)
        logits = t1[:, :LANES] + t1[:, LANES:] + jnp.dot(h_lo, r_hi, preferred_element_type=F32)
        lane = lax.broadcasted_iota(jnp.int32, logits.shape, 1)
        lg = jnp.where(lane < N_EXPERTS, logits, NEG_INF)
        m1 = jnp.max(lg, axis=-1, keepdims=True)
        i1 = jnp.min(jnp.where(lg == m1, lane, LANES), axis=-1, keepdims=True)
        lg2 = jnp.where(lane == i1, NEG_INF, lg)
        m2 = jnp.max(lg2, axis=-1, keepdims=True)
        i2 = jnp.min(jnp.where(lg2 == m2, lane, LANES), axis=-1, keepdims=True)
        e2 = jnp.exp(m2 - m1)
        w1 = 1.0 / (1.0 + e2)
        idx_ref[r, :] = jnp.where(lane == 0, i1, jnp.where(lane == 1, i2, 0))
        wgt_ref[r, :] = jnp.where(lane == 0, w1, jnp.where(lane == 1, e2 * w1, 0.0))

    _adaln_then(x_ref, g_ref, mod_ref, 3, 4, hf_ref, route)


def _router(x, gain, modt, router_w):
    t, d = x.shape
    rw = jnp.pad(router_w, ((0, 0), (0, LANES - N_EXPERTS)))
    return pl.pallas_call(
        _router_kernel,
        grid=(t // ROW_TILE,),
        in_specs=[pl.BlockSpec((ROW_TILE, d), lambda i: (i, 0)),
                  pl.BlockSpec((1, d), lambda i: (0, 0)),
                  pl.BlockSpec((None, 8, d), lambda i: (i, 0, 0)),
                  pl.BlockSpec((d, LANES), lambda i: (0, 0))],
        out_specs=[pl.BlockSpec((ROW_TILE, d), lambda i: (i, 0)),
                   pl.BlockSpec((ROW_TILE, LANES), lambda i: (i, 0)),
                   pl.BlockSpec((ROW_TILE, LANES), lambda i: (i, 0))],
        out_shape=[jax.ShapeDtypeStruct((t, d), BF16), jax.ShapeDtypeStruct((t, LANES), jnp.int32),
                   jax.ShapeDtypeStruct((t, LANES), F32)],
        scratch_shapes=[pltpu.VMEM((ROW_TILE, d), F32)],
        compiler_params=_cparams("arbitrary"),
        name="moe_router",
    )(x, gain.reshape(1, d), modt, rw)


DOWN_ROWS = 512
DOWN_TN = 1024


def _expert_changed(be_ref, blk, prev_blk, step):
    return jnp.logical_or(step == 0, be_ref[blk] != be_ref[prev_blk])


def _expert_up_kernel(be_ref, nu_ref, nv_ref, xs_ref, wg_ref, wu_ref, h_ref, wgb_ref, wub_ref):
    i = pl.program_id(1)
    used = nv_ref[i] > 0

    @pl.when(jnp.logical_and(used, _expert_changed(be_ref, i, jnp.maximum(i - 1, 0), i)))
    def _():
        wgb_ref[...] = wg_ref[...].astype(BF16)
        wub_ref[...] = wu_ref[...].astype(BF16)

    @pl.when(used)
    def _():
        xs = xs_ref[...]
        h_ref[...] = (_silu(_bdot(xs, wgb_ref[...])) * _bdot(xs, wub_ref[...])).astype(h_ref.dtype)

    @pl.when(jnp.logical_not(used))
    def _():
        h_ref[...] = jnp.zeros_like(h_ref)


def _expert_down_kernel(be_ref, nu_ref, nv_ref, h_ref, wd_ref, o_ref, wdb_ref):
    i = pl.program_id(1)
    per = MOE_ROWS // DOWN_ROWS
    blk = i // per
    used = nv_ref[blk] > (i % per) * DOWN_ROWS

    @pl.when(jnp.logical_and(nv_ref[blk] > 0, _expert_changed(be_ref, blk, jnp.maximum(i - 1, 0) // per, i)))
    def _():
        wdb_ref[...] = wd_ref[...].astype(BF16)

    @pl.when(used)
    def _():
        o_ref[...] = _bdot(h_ref[...], wdb_ref[...]).astype(o_ref.dtype)

    @pl.when(jnp.logical_not(used))
    def _():
        o_ref[...] = jnp.zeros_like(o_ref)


def _experts(xs_sorted, block_e, n_used, n_valid, wg, wu, wd):
    cap, d = xs_sorted.shape
    ff = wg.shape[2]
    per = MOE_ROWS // DOWN_ROWS

    def expert_of(blk, be, nu):
        return be[jnp.minimum(blk, jnp.maximum(nu[0] - 1, 0))]

    h = pl.pallas_call(
        _expert_up_kernel,
        grid_spec=pltpu.PrefetchScalarGridSpec(
            num_scalar_prefetch=3,
            grid=(ff // FFN_TF, cap // MOE_ROWS),
            in_specs=[pl.BlockSpec((MOE_ROWS, d), lambda f, i, be, nu, nv: (i, 0)),
                      pl.BlockSpec((None, d, FFN_TF), lambda f, i, be, nu, nv: (expert_of(i, be, nu), 0, f)),
                      pl.BlockSpec((None, d, FFN_TF), lambda f, i, be, nu, nv: (expert_of(i, be, nu), 0, f))],
            out_specs=pl.BlockSpec((MOE_ROWS, FFN_TF), lambda f, i, be, nu, nv: (i, f)),
            scratch_shapes=[pltpu.VMEM((d, FFN_TF), BF16), pltpu.VMEM((d, FFN_TF), BF16)],
        ),
        out_shape=jax.ShapeDtypeStruct((cap, ff), BF16),
        compiler_params=_cparams("arbitrary", "arbitrary"),
        name="moe_expert_up",
    )(block_e, n_used, n_valid, xs_sorted, wg, wu)
    return pl.pallas_call(
        _expert_down_kernel,
        grid_spec=pltpu.PrefetchScalarGridSpec(
            num_scalar_prefetch=3,
            grid=(d // DOWN_TN, cap // DOWN_ROWS),
            in_specs=[pl.BlockSpec((DOWN_ROWS, ff), lambda n, i, be, nu, nv: (i, 0)),
                      pl.BlockSpec((None, ff, DOWN_TN),
                                   lambda n, i, be, nu, nv: (expert_of(i // per, be, nu), 0, n),
                                   pipeline_mode=pl.Buffered(1))],
            out_specs=pl.BlockSpec((DOWN_ROWS, DOWN_TN), lambda n, i, be, nu, nv: (i, n)),
            scratch_shapes=[pltpu.VMEM((ff, DOWN_TN), BF16)],
        ),
        out_shape=jax.ShapeDtypeStruct((cap, d), BF16),
        compiler_params=_cparams("arbitrary", "arbitrary"),
        name="moe_expert_down",
    )(block_e, n_used, n_valid, h, wd)


def _combine_kernel(x_ref, g0_ref, g1_ref, w_ref, mod_ref, o_ref):
    w = w_ref[...]
    y = w[:, 0:1] * g0_ref[...].astype(F32) + w[:, 1:2] * g1_ref[...].astype(F32)
    o_ref[...] = x_ref[...] + mod_ref[5:6, :] * y


COMB_ROWS = ROW_TILE


def _combine(x, g, wgt, modt, row0, n_rows):
    t, d = x.shape
    b0 = row0 // COMB_ROWS
    per = ROW_TILE // COMB_ROWS
    return pl.pallas_call(
        _combine_kernel,
        grid=(n_rows // COMB_ROWS,),
        in_specs=[pl.BlockSpec((COMB_ROWS, d), lambda i: (i + b0, 0)),
                  pl.BlockSpec((COMB_ROWS, d), lambda i: (i + b0, 0)),
                  pl.BlockSpec((COMB_ROWS, d), lambda i: (i + b0 + t // COMB_ROWS, 0)),
                  pl.BlockSpec((COMB_ROWS, LANES), lambda i: (i + b0, 0)),
                  pl.BlockSpec((None, 8, d), lambda i: ((i + b0) // per, 0, 0))],
        out_specs=pl.BlockSpec((COMB_ROWS, d), lambda i: (i, 0)),
        out_shape=jax.ShapeDtypeStruct((n_rows, d), F32),
        compiler_params=_cparams("arbitrary"),
        name="moe_combine",
    )(x, g, g, wgt, modt)


def _moe(x, gain, modt, router_w, wg, wu, wd):
    t, d = x.shape
    hn, idx, wgt = _router(x, gain, modt, router_w)
    top_idx = idx[:, :TOP_K]
    n_slots = t * TOP_K
    flat_e = top_idx.reshape(-1)
    onehot = (flat_e[:, None] == jnp.arange(N_EXPERTS, dtype=jnp.int32)[None, :]).astype(jnp.int32)
    incl = jnp.cumsum(onehot, axis=0)
    counts = incl[-1]
    rank = jnp.sum((incl - onehot) * onehot, axis=1)
    padded = (counts + MOE_ROWS - 1) // MOE_ROWS * MOE_ROWS
    pend = jnp.cumsum(padded)
    pstart = pend - padded
    dest = pstart[flat_e] + rank
    n_blocks = n_slots // MOE_ROWS + N_EXPERTS
    cap = n_blocks * MOE_ROWS
    row_tok = (jnp.arange(cap, dtype=jnp.int32) % t).at[dest].set(
        jnp.arange(n_slots, dtype=jnp.int32) // TOP_K, unique_indices=True, mode='promise_in_bounds')
    block_e = jnp.clip(jnp.searchsorted(pend, jnp.arange(n_blocks, dtype=jnp.int32) * MOE_ROWS, side='right'),
                       0, N_EXPERTS - 1).astype(jnp.int32)
    n_used = (pend[-1:] // MOE_ROWS).astype(jnp.int32)
    blk = jnp.arange(n_blocks, dtype=jnp.int32)
    n_valid = jnp.where(blk < n_used[0],
                        jnp.clip((pstart + counts)[block_e] - blk * MOE_ROWS, 0, MOE_ROWS), 0).astype(jnp.int32)
    out = _experts(hn.at[row_tok].get(mode='promise_in_bounds'), block_e, n_used, n_valid, wg, wu, wd)
    g = out.at[dest.reshape(t, TOP_K).T.reshape(-1)].get(mode='promise_in_bounds', unique_indices=True)
    return _combine(x, g, wgt, modt, 0, T_PROMPT), _combine(x, g, wgt, modt, T_PROMPT, T_SAMPLE)


def _rope_tables(n_tokens, dim):
    n_rows = n_tokens // GRID_W
    row = jnp.repeat(jnp.arange(n_rows), GRID_W).astype(F32)
    col = jnp.tile(jnp.arange(GRID_W), n_rows).astype(F32)
    n_freq = dim // 4
    inv = ROPE_BASE ** (-jnp.arange(n_freq, dtype=F32) / n_freq)
    ang = jnp.concatenate([row[:, None] * inv, col[:, None] * inv], axis=-1)
    return jnp.cos(ang), jnp.sin(ang)


def kernel(x_prompt, x_sample, state_ret, cache_k, cache_v, state_ssd, c, c_ctx, ada_w, ada_b, norm_mix, norm_ffn, ev_w_in, ev_w_out, ret_decay_logit, ret_norm, att_q_norm, att_k_norm, att_sink, ffn_w_gate, ffn_w_up, ffn_w_down, ssd_w_in, ssd_conv_w, ssd_conv_b, ssd_a_log, ssd_dt_bias, ssd_d, ssd_norm, ssd_w_out, moe_router, moe_w_gate, moe_w_up, moe_w_down):
    d = D_MODEL
    x = (x_prompt.reshape(T_PROMPT, d), x_sample.reshape(T_SAMPLE, d))

    cvecs = jnp.concatenate([c_ctx[None, :], c, jnp.zeros((MOD_ROWS - 1 - DEC_BATCH, d), F32)], axis=0)
    mods = _modulation(cvecs, ada_w, ada_b).reshape(2, MOD_ROWS, 6, d)
    tiles_per_seq = DEC_SEQ // ROW_TILE
    tile_row = jnp.concatenate([jnp.zeros((T_PROMPT // ROW_TILE,), jnp.int32),
                                1 + jnp.arange(T_SAMPLE // ROW_TILE, dtype=jnp.int32) // tiles_per_seq])
    modt = jnp.pad(mods[:, tile_row], ((0, 0), (0, 0), (0, 2), (0, 0)))

    ev_w_in_b, ev_w_out_b = ev_w_in[0].astype(BF16), ev_w_out[0].astype(BF16)
    ffn_wd_b = ffn_w_down[0].astype(BF16)
    ssd_w_in_b, ssd_w_out_b = ssd_w_in[0].astype(BF16), ssd_w_out[0].astype(BF16)

    proj, = _adaln_matmul(x, norm_mix[0], modt[0], 0, 1, ev_w_in_b, tn=512, out_dtype=BF16, name="even_in_proj")
    lg = jax.nn.log_sigmoid(ret_decay_logit[0].astype(F32))
    cos_r, sin_r = _rope_tables(DEC_SEQ, RET_DK)
    cos_a, sin_a = _rope_tables(DEC_SEQ, ATT_HD)
    cos_a2 = jnp.concatenate([cos_a, cos_a], axis=-1)
    sin_a2 = jnp.concatenate([-sin_a, sin_a], axis=-1)
    ret_p, new_state_ret = _retention(proj, lg, ret_norm[0], SEQ, BATCH, 0, emit_state=True)
    ret_s, = _retention(proj, lg, ret_norm[0], DEC_SEQ, DEC_BATCH, T_PROMPT // DEC_SEQ,
                        ropes=(cos_r, sin_r), s0=state_ret)
    att_p, new_k, new_v = _attention_prompt(proj, att_sink[0], att_q_norm[0], att_k_norm[0])
    att_s = _attention_latent(proj, cache_k[:, 0], cache_v[:, 0], att_sink[0], cos_a2, sin_a2,
                              att_q_norm[0], att_k_norm[0])
    x = _proj_residual([(ret_p, ret_s), (att_p, att_s)], ev_w_out_b, x, modt[0], 2, tn=1024, name="even_out_proj")
    h = _ffn_gateup(x, norm_ffn[0], modt[0], ffn_w_gate[0], ffn_w_up[0])
    x = _proj_residual([h], ffn_wd_b, x, modt[0], 5, tn=512, name="ffn_down")

    zx, dt_raw = _adaln_matmul(x, norm_mix[1], modt[1], 0, 1, ssd_w_in_b, tn=1024, n_out=SSD_ZX,
                               tail=2 * SSD_HEADS, out_dtype=BF16, name="ssd_in_proj")
    prep = _ssd_prep(dt_raw, ssd_dt_bias[0], ssd_a_log[0])
    d_exp = jnp.repeat(ssd_d[0], SSD_P)[None, :]
    conv_b = ssd_conv_b[0][None, :]
    yg_p, yss_p, new_state_ssd = _ssd_scan(zx, prep, ssd_conv_w[0], conv_b, d_exp, ssd_norm[0], SEQ, BATCH, 0,
                                           emit_state=True)
    yg_s, yss_s = _ssd_scan(zx, prep, ssd_conv_w[0], conv_b, d_exp, ssd_norm[0], DEC_SEQ, DEC_BATCH,
                            T_PROMPT // DEC_SEQ, s0=state_ssd)
    x = _proj_residual([(yg_p, yg_s)], ssd_w_out_b, x, modt[1], 2, tn=512, row_ss=(yss_p, yss_s),
                       name="ssd_out_proj")
    y_p, y_s = _moe(x, norm_ffn[1], modt[1], moe_router[0], moe_w_gate[0], moe_w_up[0], moe_w_down[0])

    y_prompt = y_p.reshape(BATCH, SEQ, d)
    y_sample = y_s.reshape(DEC_BATCH, DEC_SEQ, d)
    new_cache_k = new_k.reshape(BATCH, 1, SEQ, ATT_KV_HEADS, ATT_HD)
    new_cache_v = new_v.reshape(BATCH, 1, SEQ, ATT_KV_HEADS, ATT_HD)
    return (y_prompt, y_sample, new_state_ret, new_cache_k, new_cache_v, new_state_ssd)
```
